```python
import jax, jax.numpy as jnp
from jax import lax
import numpy as np

D_MODEL = 1024
BATCH = 8
SEQ = 2048
DEPTH = 1

CTX_LEN = 256
GRID_W = 64

H_A = 8
DN_A = 64
DR_A = 32
DV_A = 64
W_A = H_A * DV_A
Q_LORA = 256
KV_LORA = 128
ROPE_FREQS = DR_A // 4
ROPE_BASE = 10000.0
Q_BLOCK = 128

H_B = 4
DH_B = 128
W_B = H_B * DH_B
QKV_BS = 4
N_QKV_BLK = W_B // QKV_BS
CONV_W = 3
CHUNK = 64

W_MIX = W_A + W_B
N_IN = Q_LORA + KV_LORA + DR_A + W_A + 3 * W_B
SPLITS = (Q_LORA,
          Q_LORA + KV_LORA,
          Q_LORA + KV_LORA + DR_A,
          Q_LORA + KV_LORA + DR_A + W_A,
          Q_LORA + KV_LORA + DR_A + W_A + W_B,
          Q_LORA + KV_LORA + DR_A + W_A + 2 * W_B)
ALPHA = (2.0 * DEPTH) ** 0.25
BETA = (8.0 * DEPTH) ** -0.25
LN_EPS = 1e-5
RMS_EPS = 1e-6

kernel_name = "hymba_mla_mlstm_prefix_dit_layer"


def layer_norm(x, eps=LN_EPS):
    xf = x.astype(jnp.float32)
    xc = xf - xf.mean(-1, keepdims=True)
    var = jnp.mean(xc * xc, -1, keepdims=True)
    return (xc * lax.rsqrt(var + eps)).astype(x.dtype)


def rms_norm(x, g, eps=RMS_EPS):
    xf = x.astype(jnp.float32)
    return (xf * lax.rsqrt(jnp.mean(xf * xf, -1, keepdims=True) + eps)).astype(x.dtype) * g


def axial_rope_tables(seq):
    n_rows = seq // GRID_W
    row = jnp.repeat(jnp.arange(n_rows, dtype=jnp.float32), GRID_W)
    col = jnp.tile(jnp.arange(GRID_W, dtype=jnp.float32), n_rows)
    inv = ROPE_BASE ** (-jnp.arange(ROPE_FREQS, dtype=jnp.float32) / ROPE_FREQS)
    ang = jnp.stack([row[:, None] * inv, col[:, None] * inv], axis=1)
    return jnp.cos(ang), jnp.sin(ang)


def rope2d(x, cos, sin):
    xs = x.reshape(x.shape[:-1] + (2, 2, ROPE_FREQS)).astype(jnp.float32)
    x1, x2 = xs[..., 0, :], xs[..., 1, :]
    out = jnp.stack([x1 * cos - x2 * sin, x2 * cos + x1 * sin], axis=-2)
    return out.reshape(x.shape).astype(x.dtype)


def mla_project(q_a, kv_a, k_r, g_qa, w_qb, g_kva, w_kvb):
    b, t, _ = q_a.shape
    q = (rms_norm(q_a, g_qa) @ w_qb).reshape(b, t, H_A, DN_A + DR_A)
    kv = (rms_norm(kv_a, g_kva) @ w_kvb).reshape(b, t, H_A, DN_A + DV_A)
    return q[..., :DN_A], q[..., DN_A:], kv[..., :DN_A], k_r, kv[..., DN_A:]


def block_attention(qn, qr, kn, kr, v):
    b, tq, h, _ = qn.shape
    nb = tq // Q_BLOCK
    scale = (DN_A + DR_A) ** -0.5
    qn_b = qn.reshape(b, nb, Q_BLOCK, h, DN_A).swapaxes(0, 1)
    qr_b = qr.reshape(b, nb, Q_BLOCK, h, DR_A).swapaxes(0, 1)

    def one_block(args):
        qn_i, qr_i = args
        s = jnp.einsum('bqhd,bkhd->bhqk', qn_i, kn) + jnp.einsum('bqhr,bkr->bhqk', qr_i, kr)
        p = jax.nn.softmax(s.astype(jnp.float32) * scale, axis=-1).astype(v.dtype)
        return jnp.einsum('bhqk,bkhd->bqhd', p, v)

    o = lax.map(one_block, (qn_b, qr_b))
    return o.swapaxes(0, 1).reshape(b, tq, h * DV_A)


def dwconv_centred(x, w, bias):
    t = x.shape[1]
    pad = CONV_W // 2
    xp = jnp.pad(x, ((0, 0), (pad, pad), (0, 0)))
    out = bias
    for j in range(CONV_W):
        out = out + xp[:, j:j + t] * w[j]
    return out


def headwise(x, w):
    b, t, _ = x.shape
    nb, bs, _ = w.shape
    return jnp.einsum('btgi,gio->btgo', x.reshape(b, t, nb, bs), w).reshape(b, t, nb * bs)


def to_heads(x):
    b, t, _ = x.shape
    return x.reshape(b, t, H_B, DH_B).transpose(0, 2, 1, 3)


def mlstm_inputs(x_m, conv_w, conv_b, w_mq, w_mk, w_mv, w_gate, b_gate):
    xc = jax.nn.silu(dwconv_centred(x_m, conv_w, conv_b))
    q = headwise(xc, w_mq)
    k = headwise(xc, w_mk)
    v = headwise(x_m, w_mv)
    g = (jnp.concatenate([q, k, v], -1) @ w_gate + b_gate).astype(jnp.float32)
    g = g.reshape(g.shape[0], g.shape[1], 4, H_B).transpose(2, 0, 3, 1)
    gates_f = (g[0], jax.nn.log_sigmoid(g[1]))
    gates_b = (g[2], jax.nn.log_sigmoid(g[3]))
    return xc, to_heads(q), to_heads(k * DH_B ** -0.5), to_heads(v), gates_f, gates_b


def zero_state(b):
    return (jnp.zeros((b, H_B, DH_B, DH_B), jnp.float32),
            jnp.zeros((b, H_B, DH_B), jnp.float32),
            jnp.zeros((b, H_B), jnp.float32))


def mlstm_final_state(k, v, li, lf, state):
    c0, n0, m0 = state
    bcum = jnp.cumsum(lf, axis=-1)
    b_end = bcum[..., -1]
    w = b_end[..., None] - bcum + li
    m = jnp.maximum(b_end + m0, w.max(-1))
    a0 = jnp.exp(b_end + m0 - m)
    ws = jnp.exp(w - m[..., None])
    c_new = a0[..., None, None] * c0 + jnp.einsum('bhs,bhsv,bhsd->bhvd', ws, v, k)
    n_new = a0[..., None] * n0 + jnp.einsum('bhs,bhsd->bhd', ws, k)
    return (c_new, n_new, m)


def mlstm_chunkwise(q, k, v, li, lf, state):
    b, h, t, _ = q.shape
    nc = t // CHUNK

    def split(a):
        return jnp.moveaxis(a.reshape(a.shape[:2] + (nc, CHUNK) + a.shape[3:]), 2, 0)

    mask = jnp.tril(jnp.ones((CHUNK, CHUNK), dtype=bool))

    def step(st, inp):
        c0, n0, m0 = st
        qc, kc, vc, lic, lfc = inp
        bcum = jnp.cumsum(lfc, axis=-1)
        dmat = jnp.where(mask, bcum[..., :, None] - bcum[..., None, :] + lic[..., None, :], -jnp.inf)
        inter = bcum + m0[..., None]
        m = jnp.maximum(inter, dmat.max(-1))
        a_inter = jnp.exp(inter - m)
        s = jnp.einsum('bhtd,bhsd->bhts', qc, kc) * jnp.exp(dmat - m[..., None])
        num = a_inter[..., None] * jnp.einsum('bhvd,bhtd->bhtv', c0, qc) + jnp.einsum('bhts,bhsv->bhtv', s, vc)
        den = a_inter * jnp.einsum('bhd,bhtd->bht', n0, qc) + s.sum(-1)
        h_chunk = num / jnp.maximum(jnp.abs(den), jnp.exp(-m))[..., None]
        return mlstm_final_state(kc, vc, lic, lfc, st), h_chunk

    st, hs = lax.scan(step, state, tuple(split(a) for a in (q, k, v, li, lf)))
    return jnp.moveaxis(hs, 0, 2).reshape(b, h, t, v.shape[-1]), st


def flip_t(a):
    return jnp.flip(a, axis=2)


def stream_tensors(u, w_in, g_qa, w_qb, g_kva, w_kvb, conv_w, conv_b, w_mq, w_mk, w_mv, w_gate, b_gate):
    p = u @ w_in
    q_a, kv_a, k_r, z_a, x_m, o_m, z_m = jnp.split(p, SPLITS, axis=-1)
    mla = mla_project(q_a, kv_a, k_r, g_qa, w_qb, g_kva, w_kvb)
    mls = mlstm_inputs(x_m, conv_w, conv_b, w_mq, w_mk, w_mv, w_gate, b_gate)
    return mla, mls, (z_a, o_m, z_m)


def mixer_output(att, z_a, h_cell, o_m, xc, z_m, mh_g, skip, w_out):
    b, _, t, _ = h_cell.shape
    hb = jax.nn.sigmoid(o_m) * h_cell.transpose(0, 2, 1, 3).reshape(b, t, W_B).astype(o_m.dtype)
    hb = layer_norm(hb.reshape(b, t, H_B, DH_B)).reshape(b, t, W_B) * mh_g + skip * xc
    y_a = att * jax.nn.silu(z_a)
    y_b = hb * jax.nn.silu(z_m)
    return jnp.concatenate([y_a, y_b], axis=-1) @ w_out


def hybrid_layer(h, hc, c, c_ctx, cos, sin, w_ada, b_ada, w_in, g_qa, w_qb, g_kva, w_kvb,
                 conv_w, conv_b, w_mq, w_mk, w_mv, w_gate, b_gate, mh_g, skip, w_out,
                 ln_g, ln_b, update_ctx):
    shift, scale, gate = jnp.split((jax.nn.silu(c) @ w_ada + b_ada)[:, None, :], 3, axis=-1)
    shift_c, scale_c, gate_c = jnp.split(jax.nn.silu(c_ctx) @ w_ada + b_ada, 3, axis=-1)
    weights = (w_in, g_qa, w_qb, g_kva, w_kvb, conv_w, conv_b, w_mq, w_mk, w_mv, w_gate, b_gate)

    (qn, qr, kn, kr, v), (xc, q, k, vm, gf, gb), (z_a, o_m, z_m) = stream_tensors(
        h * (1 + scale) + shift, *weights)
    (qn_c, qr_c, kn_c, kr_c, v_c), (xc_c, q_c, k_c, vm_c, gf_c, gb_c), (z_a_c, o_m_c, z_m_c) = stream_tensors(
        hc * (1 + scale_c) + shift_c, *weights)

    qr = rope2d(qr, cos[:, None], sin[:, None])
    kr = rope2d(kr, cos, sin)
    att = block_attention(qn, qr,
                          jnp.concatenate([kn_c, kn], axis=1),
                          jnp.concatenate([kr_c, kr], axis=1),
                          jnp.concatenate([v_c, v], axis=1))

    st0 = zero_state(h.shape[0])
    if update_ctx:
        h_cf, st_f = mlstm_chunkwise(q_c, k_c, vm_c, *gf_c, st0)
        h_cb, st_b = mlstm_chunkwise(*map(flip_t, (q_c, k_c, vm_c, *gb_c)), st0)
    else:
        st_f = mlstm_final_state(k_c, vm_c, *gf_c, st0)
        st_b = mlstm_final_state(*map(flip_t, (k_c, vm_c, *gb_c)), st0)
    h_f, _ = mlstm_chunkwise(q, k, vm, *gf, st_f)
    h_b, _ = mlstm_chunkwise(*map(flip_t, (q, k, vm, *gb)), st_b)

    y = mixer_output(att, z_a, h_f + flip_t(h_b), o_m, xc, z_m, mh_g, skip, w_out)
    h_new = layer_norm(ALPHA * h + gate * y) * ln_g + ln_b

    if update_ctx:
        att_c = block_attention(qn_c, qr_c, kn_c, kr_c, v_c)
        y_c = mixer_output(att_c, z_a_c, h_cf + flip_t(h_cb), o_m_c, xc_c, z_m_c, mh_g, skip, w_out)
        hc = layer_norm(ALPHA * hc + gate_c * y_c) * ln_g + ln_b
    return h_new, hc


def setup_inputs(seed: int = 0) -> dict:
    key = jax.random.key(seed)
    ks = jax.random.split(key, 32)
    L = DEPTH
    f32 = jnp.float32

    def nrm(k, shape, s):
        return jax.random.normal(k, shape, f32) * s

    f_bias = jnp.linspace(3.0, 6.0, H_B, dtype=f32)
    b_gate = jnp.concatenate([nrm(ks[20], (L, H_B), 0.1),
                              f_bias + nrm(ks[21], (L, H_B), 0.1),
                              nrm(ks[22], (L, H_B), 0.1),
                              f_bias + nrm(ks[23], (L, H_B), 0.1)], axis=-1)
    return {
        "x": nrm(ks[0], (BATCH, SEQ, D_MODEL), 1.0),
        "c": nrm(ks[1], (BATCH, D_MODEL), 1.0),
        "ctx": nrm(ks[2], (BATCH, CTX_LEN, D_MODEL), 1.0),
        "c_ctx": nrm(ks[3], (D_MODEL,), 1.0),
        "ln_in_g": 1.0 + nrm(ks[4], (D_MODEL,), 0.02),
        "ln_in_b": nrm(ks[5], (D_MODEL,), 0.02),
        "w_ada": nrm(ks[6], (L, D_MODEL, 3 * D_MODEL), 0.5 * D_MODEL ** -0.5),
        "b_ada": nrm(ks[7], (L, 3 * D_MODEL), 0.02),
        "w_in": nrm(ks[8], (L, D_MODEL, N_IN), D_MODEL ** -0.5),
        "g_qa": 1.0 + nrm(ks[9], (L, Q_LORA), 0.02),
        "w_qb": nrm(ks[10], (L, Q_LORA, H_A * (DN_A + DR_A)), Q_LORA ** -0.5),
        "g_kva": 1.0 + nrm(ks[11], (L, KV_LORA), 0.02),
        "w_kvb": nrm(ks[12], (L, KV_LORA, H_A * (DN_A + DV_A)), KV_LORA ** -0.5),
        "conv_w": nrm(ks[13], (L, CONV_W, W_B), CONV_W ** -0.5),
        "conv_b": nrm(ks[14], (L, W_B), 0.02),
        "w_mq": nrm(ks[15], (L, N_QKV_BLK, QKV_BS, QKV_BS), QKV_BS ** -0.5),
        "w_mk": nrm(ks[16], (L, N_QKV_BLK, QKV_BS, QKV_BS), QKV_BS ** -0.5),
        "w_mv": nrm(ks[17], (L, N_QKV_BLK, QKV_BS, QKV_BS), QKV_BS ** -0.5),
        "w_gate": nrm(ks[18], (L, 3 * W_B, 4 * H_B), (3 * W_B) ** -0.5),
        "b_gate": b_gate,
        "mh_g": 1.0 + nrm(ks[24], (L, W_B), 0.02),
        "skip": 1.0 + nrm(ks[25], (L, W_B), 0.02),
        "w_out": nrm(ks[26], (L, W_MIX, D_MODEL), BETA * W_MIX ** -0.5),
        "ln_g": 1.0 + nrm(ks[27], (L, D_MODEL), 0.02),
        "ln_b": nrm(ks[28], (L, D_MODEL), 0.02),
    }


def reference(x, c, ctx, c_ctx, ln_in_g, ln_in_b, w_ada, b_ada, w_in, g_qa, w_qb, g_kva, w_kvb,
              conv_w, conv_b, w_mq, w_mk, w_mv, w_gate, b_gate, mh_g, skip, w_out, ln_g, ln_b):
    cos, sin = axial_rope_tables(x.shape[1])
    h = layer_norm(x) * ln_in_g + ln_in_b
    hc = layer_norm(ctx) * ln_in_g + ln_in_b
    for l in range(DEPTH):
        h, hc = hybrid_layer(h, hc, c, c_ctx, cos, sin, w_ada[l], b_ada[l], w_in[l], g_qa[l], w_qb[l],
                             g_kva[l], w_kvb[l], conv_w[l], conv_b[l], w_mq[l], w_mk[l], w_mv[l],
                             w_gate[l], b_gate[l], mh_g[l], skip[l], w_out[l], ln_g[l], ln_b[l],
                             update_ctx=(l < DEPTH - 1))
    return h
```

```python
import functools

import jax
import jax.numpy as jnp
from jax import lax
from jax.experimental import pallas as pl
from jax.experimental.pallas import tpu as pltpu

F32 = jnp.float32
BF16 = jnp.bfloat16

D_MODEL = 1024
CTX_LEN = 256
GRID_W = 64
H_A, DN_A, DR_A, DV_A = 8, 64, 32, 64
W_A = H_A * DV_A
Q_LORA, KV_LORA = 256, 128
ROPE_FREQS = DR_A // 4
ROPE_BASE = 10000.0
H_B, DH_B = 4, 128
W_B = H_B * DH_B
QKV_BS = 4
DEPTH = 1
ALPHA = (2.0 * DEPTH) ** 0.25
LN_EPS = 1e-5
RMS_EPS = 1e-6

LANES = 128
SLAB = LANES
N_IN_PAD = 2560
CHUNK = 256
VMEM_LIMIT = 56 * 1024 * 1024

_NT = (((1,), (1,)), ((), ()))


def _dot(a, b):
    return jnp.dot(a, b, preferred_element_type=F32)


def _dot_nt(a, b):
    return lax.dot_general(a, b, _NT, preferred_element_type=F32)


def _layer_norm(x):
    mu = jnp.mean(x, axis=-1, keepdims=True)
    xc = x - mu
    var = jnp.mean(xc * xc, axis=-1, keepdims=True)
    return xc * lax.rsqrt(var + LN_EPS)


def _rms_norm(x, g):
    return (x * lax.rsqrt(jnp.mean(x * x, axis=-1, keepdims=True) + RMS_EPS)) * g


def _silu(x):
    return x * jax.nn.sigmoid(x)


def _log_sigmoid(x):
    return jnp.minimum(x, 0.0) - jnp.log1p(jnp.exp(-jnp.abs(x)))


def _ada_kernel(cc_ref, w_ref, b_ref, o_ref):
    a = _silu(cc_ref[...]).astype(BF16)
    o_ref[...] = _dot(a, w_ref[...].astype(BF16)) + b_ref[...]


def _ada_call(cc, w_ada, b_ada):
    n = w_ada.shape[1]
    tn = 1024
    return pl.pallas_call(
        _ada_kernel,
        grid=(n // tn,),
        in_specs=[pl.BlockSpec((16, D_MODEL), lambda j: (0, 0)),
                  pl.BlockSpec((D_MODEL, tn), lambda j: (0, j)),
                  pl.BlockSpec((1, tn), lambda j: (0, j))],
        out_specs=pl.BlockSpec((16, tn), lambda j: (0, j)),
        out_shape=jax.ShapeDtypeStruct((16, n), F32),
        compiler_params=pltpu.CompilerParams(dimension_semantics=("arbitrary",),
                                             vmem_limit_bytes=VMEM_LIMIT),
        name="ada_mod",
    )(cc, w_ada, b_ada)


def _rope_slab(s, t1, t2):
    return s * t1 + pltpu.roll(s, 32, 1) * t2


def _proj_kernel(x_ref, mod_ref, lng_ref, lnb_ref, win_ref, gqa_ref, wq_ref, gkva_ref, wkn_ref,
                 wvt_ref, t1q_ref, t2q_ref, t1k_ref, t2k_ref, *out_refs, latent):
    if latent:
        q_ref, k_ref, vt_ref, xm_ref, sza_ref, som_ref, szm_ref = out_refs
    else:
        k_ref, vt_ref, xm_ref = out_refs
    mod = mod_ref[0]
    shift, scale = mod[:, 0:D_MODEL], mod[:, D_MODEL:2 * D_MODEL]
    h = _layer_norm(x_ref[0]) * lng_ref[...] + lnb_ref[...]
    u = (h * (1.0 + scale) + shift).astype(BF16)

    p0 = _dot(u, win_ref[:, 0:512])
    kvn = _rms_norm(p0[:, 256:384], gkva_ref[...]).astype(BF16)
    kn = _dot(kvn, wkn_ref[...])
    krr = _rope_slab(p0[:, 384:512], t1k_ref[...], t2k_ref[...])
    for hh in range(H_A):
        sl = slice(hh * SLAB, (hh + 1) * SLAB)
        k_ref[0, :, sl] = (kn[:, sl] + krr).astype(BF16)
    vt_ref[0] = _dot_nt(wvt_ref[...], kvn).astype(BF16)
    xm_ref[0] = _dot(u, win_ref[:, 1024:1536]).astype(BF16)
    if latent:
        qn = _rms_norm(p0[:, 0:256], gqa_ref[...]).astype(BF16)
        qs = _dot(qn, wq_ref[...])
        t1q, t2q = t1q_ref[...], t2q_ref[...]
        for hh in range(H_A):
            sl = slice(hh * SLAB, (hh + 1) * SLAB)
            q_ref[0, :, sl] = _rope_slab(qs[:, sl], t1q, t2q).astype(BF16)
        sza_ref[0] = _silu(_dot(u, win_ref[:, 512:1024])).astype(BF16)
        som_ref[0] = jax.nn.sigmoid(_dot(u, win_ref[:, 1536:2048])).astype(BF16)
        szm_ref[0] = _silu(_dot(u, win_ref[:, 2048:2560])).astype(BF16)


def _proj_call(x, mod3, mod_row, lng, lnb, win, gqa, wq, gkva, wkn, wvt, t1q, t2q, t1k, t2k, *, tm,
               latent):
    b, t, _ = x.shape
    const = lambda shape: pl.BlockSpec(shape, lambda i, j: (0,) * len(shape))
    tab = pl.BlockSpec((tm, SLAB), lambda i, j: (j, 0))
    row = lambda w: pl.BlockSpec((1, tm, w), lambda i, j: (i, j, 0))
    if mod_row is None:
        mod_spec = pl.BlockSpec((1, 1, 3 * D_MODEL), lambda i, j: (i, 0, 0))
    else:
        mod_spec = pl.BlockSpec((1, 1, 3 * D_MODEL), lambda i, j: (mod_row, 0, 0))
    in_specs = [row(D_MODEL), mod_spec, const((1, D_MODEL)), const((1, D_MODEL)),
                const((D_MODEL, N_IN_PAD)), const((1, Q_LORA)), const((Q_LORA, H_A * SLAB)),
                const((1, KV_LORA)), const((KV_LORA, H_A * SLAB)), const((W_A, KV_LORA)),
                tab, tab, tab, tab]
    vt_spec = pl.BlockSpec((1, W_A, tm), lambda i, j: (i, 0, j))
    k_shape = jax.ShapeDtypeStruct((b, t, H_A * SLAB), BF16)
    vt_shape = jax.ShapeDtypeStruct((b, W_A, t), BF16)
    half = jax.ShapeDtypeStruct((b, t, W_B), BF16)
    if latent:
        out_specs = [row(H_A * SLAB), row(H_A * SLAB), vt_spec, row(W_B), row(W_A), row(W_B), row(W_B)]
        out_shape = [k_shape, k_shape, vt_shape, half, half, half, half]
    else:
        out_specs = [row(H_A * SLAB), vt_spec, row(W_B)]
        out_shape = [k_shape, vt_shape, half]
    return pl.pallas_call(
        functools.partial(_proj_kernel, latent=latent),
        grid=(b, t // tm),
        in_specs=in_specs, out_specs=out_specs, out_shape=out_shape,
        compiler_params=pltpu.CompilerParams(dimension_semantics=("parallel", "parallel"),
                                             vmem_limit_bytes=VMEM_LIMIT),
        name="in_proj_latent" if latent else "in_proj_ctx",
    )(x, mod3, lng, lnb, win, gqa, wq, gkva, wkn, wvt, t1q, t2q, t1k, t2k)


def _attn_kernel(q_ref, kc_ref, kl_ref, vtc_ref, vtl_ref, sza_ref, o_ref):
    outs = []
    for hh in range(H_A):
        sl = slice(hh * SLAB, (hh + 1) * SLAB)
        vs = slice(hh * DV_A, (hh + 1) * DV_A)
        qh = q_ref[0, :, sl]
        sc = _dot_nt(kc_ref[0, :, sl], qh)
        sk = _dot_nt(kl_ref[0, :, sl], qh)
        m = jnp.maximum(jnp.max(sc, axis=0, keepdims=True), jnp.max(sk, axis=0, keepdims=True))
        pc = jnp.exp(sc - m)
        pk = jnp.exp(sk - m)
        den = jnp.sum(pc, axis=0, keepdims=True) + jnp.sum(pk, axis=0, keepdims=True)
        o = _dot(vtc_ref[0, vs, :], pc.astype(BF16)) + _dot(vtl_ref[0, vs, :], pk.astype(BF16))
        outs.append(o / den)
    att = jnp.concatenate(outs, axis=0).T
    o_ref[0] = (att * sza_ref[0].astype(F32)).astype(BF16)


def _attn_call(q, kc, kl, vtc, vtl, sza, *, tq):
    b, t, _ = q.shape
    return pl.pallas_call(
        _attn_kernel,
        grid=(b, t // tq),
        in_specs=[pl.BlockSpec((1, tq, H_A * SLAB), lambda i, j: (i, j, 0)),
                  pl.BlockSpec((1, CTX_LEN, H_A * SLAB), lambda i, j: (i, 0, 0)),
                  pl.BlockSpec((1, t, H_A * SLAB), lambda i, j: (i, 0, 0)),
                  pl.BlockSpec((1, W_A, CTX_LEN), lambda i, j: (i, 0, 0)),
                  pl.BlockSpec((1, W_A, t), lambda i, j: (i, 0, 0)),
                  pl.BlockSpec((1, tq, W_A), lambda i, j: (i, j, 0))],
        out_specs=pl.BlockSpec((1, tq, W_A), lambda i, j: (i, j, 0)),
        out_shape=jax.ShapeDtypeStruct((b, t, W_A), BF16),
        compiler_params=pltpu.CompilerParams(dimension_semantics=("parallel", "parallel"),
                                             vmem_limit_bytes=VMEM_LIMIT),
        name="mla_attention",
    )(q, kc, kl, vtc, vtl, sza)


_PAD = 8


def _mlstm_prep(n, xm_ref, cw_ref, cb_ref, wq_ref, wk_ref, wkt_ref, wv_ref, wg_ref, bg_ref,
                pad_s, xc_s, q_s, kt_s, v_s, gcol_s, grow_s):
    L = CHUNK
    zrow = jnp.zeros((_PAD, W_B), F32)
    pad_s[0:_PAD, :] = zrow
    pad_s[_PAD:_PAD + n, :] = xm_ref[...].astype(F32)
    pad_s[_PAD + n:2 * _PAD + n, :] = zrow
    lane = lax.broadcasted_iota(jnp.int32, (16, LANES), 1)
    grp = lax.broadcasted_iota(jnp.int32, (16, LANES), 0) // H_B
    for c in range(n // L):
        r0 = c * L
        xprev = pad_s[r0 + _PAD - 1:r0 + _PAD - 1 + L, :]
        xcur = pad_s[r0 + _PAD:r0 + _PAD + L, :]
        xnext = pad_s[r0 + _PAD + 1:r0 + _PAD + 1 + L, :]
        pre = cb_ref[...] + xprev * cw_ref[0:1, :] + xcur * cw_ref[1:2, :] + xnext * cw_ref[2:3, :]
        xc = _silu(pre)
        if xc_s is not None:
            xc_s[r0:r0 + L, :] = xc
        xcb = xc.astype(BF16)
        xmb = xcur.astype(BF16)
        qv = _dot(xcb, wq_ref[...]).astype(BF16)
        kv = _dot(xcb, wk_ref[...]).astype(BF16)
        vv = _dot(xmb, wv_ref[...]).astype(BF16)
        if q_s is not None:
            q_s[r0:r0 + L, :] = qv
        v_s[r0:r0 + L, :] = vv
        kt_s[c] = (_dot_nt(wkt_ref[...], xcb) * (DH_B ** -0.5)).astype(BF16)
        g = (_dot(qv, wg_ref[0:W_B, :]) + _dot(kv, wg_ref[W_B:2 * W_B, :])
             + _dot(vv, wg_ref[2 * W_B:3 * W_B, :]) + bg_ref[...])
        gt = g.T[0:16, :]
        raw =[gt[:, i * LANES:(i + 1) * LANES] for i in range(L // LANES)]
        lfs = [jnp.where((grp == 1) | (grp == 3), _log_sigmoid(r), 0.0) for r in raw]
        pre_p, suf_p = [], []
        for lf in lfs:
            pf, sf = lf, lf
            sh = 1
            while sh < LANES:
                pf = pf + jnp.where(lane >= sh, pltpu.roll(pf, sh, 1), 0.0)
                sf = sf + jnp.where(lane < LANES - sh, pltpu.roll(sf, LANES - sh, 1), 0.0)
                sh *= 2
            pre_p.append(pf)
            suf_p.append(sf)
        tot = [p[:, LANES - 1:LANES] for p in pre_p]
        rows = []
        for i in range(L // LANES):
            before = sum(tot[:i]) if i > 0 else 0.0
            after = sum(tot[i + 1:]) if i + 1 < L // LANES else 0.0
            cum = jnp.where(grp == 1, pre_p[i] + before, jnp.where(grp == 3, suf_p[i] + after, 0.0))
            rows.append(jnp.where((grp == 0) | (grp == 2), raw[i], cum))
        r16 =jnp.concatenate(rows, axis=1)
        li_f, cum_f, li_b, cum_b = r16[0:4], r16[4:8], r16[8:12], r16[12:16]
        grow_s[c] = jnp.concatenate([li_f - cum_f, li_b - cum_b], axis=0)
        colsrc = jnp.concatenate([cum_f, cum_b, li_f, li_b, jnp.zeros((LANES - 16, L), F32)], axis=0)
        gcol_s[r0:r0 + L, :] = colsrc.T


def _mlstm_chunk(qc, kt_c, vc, gcol, grow, st_ref, hd, m0, forward, want_h, consts):
    L = CHUNK
    tri_f, tri_b, ones_blk = consts
    d, hh = hd // H_B, hd % H_B
    bcol = gcol[:, hd:hd + 1]
    licol = gcol[:, 8 + hd:9 + hd]
    b_end = bcol[L - 1:L, :] if forward else bcol[0:1, :]
    st = st_ref[hd]
    v_ext = jnp.concatenate([vc, ones_blk], axis=1)
    h = None
    if want_h:
        inter = bcol + m0
        rrow = grow[hd:hd + 1, :]
        dmat = jnp.where(tri_f if forward else tri_b, bcol + rrow, -jnp.inf)
        m = jnp.maximum(inter, jnp.max(dmat, axis=1, keepdims=True))
        s = (_dot(qc, kt_c) * jnp.exp(dmat - m)).astype(BF16)
        tot = jnp.exp(inter - m) * _dot(qc, st.astype(BF16)) + _dot(s, v_ext)
        num, den = tot[:, 0:DH_B], tot[:, DH_B:DH_B + 1]
        h = num / jnp.maximum(jnp.abs(den), jnp.exp(-m))
    wcol = (b_end - bcol) + licol
    m_new = jnp.maximum(b_end + m0, jnp.max(wcol, axis=0, keepdims=True))
    vw = (v_ext.astype(F32) * jnp.exp(wcol - m_new)).astype(BF16)
    st_ref[hd] = jnp.exp(b_end + m0 - m_new) * st + _dot(kt_c, vw)
    return h, m_new


def _mlstm_kernel(xm_ref, xmc_ref, som_ref, szm_ref, cw_ref, cb_ref, wq_ref, wk_ref, wkt_ref, wv_ref,
                  wg_ref, bg_ref, mhg_ref, skip_ref, o_ref,
                  pad_s, xc_s, q_s, kt_s, v_s, gcol_s, grow_s, ktc_s, vc_s, gcolc_s, growc_s,
                  st_s, hf_s, hb_s):
    L = CHUNK
    t = xm_ref.shape[1]
    nc = t // L
    row_i = lax.broadcasted_iota(jnp.int32, (L, L), 0)
    col_i = lax.broadcasted_iota(jnp.int32, (L, L), 1)
    ones_blk = (lax.broadcasted_iota(jnp.int32, (L, DH_B), 1) == 0).astype(BF16)
    consts = (row_i >= col_i, row_i <= col_i, ones_blk)
    weights = (cw_ref, cb_ref, wq_ref, wk_ref, wkt_ref, wv_ref, wg_ref, bg_ref)

    _mlstm_prep(CTX_LEN, xmc_ref.at[0], *weights, pad_s, None, None, ktc_s, vc_s, gcolc_s, growc_s)
    st_s[...] = jnp.zeros(st_s.shape, F32)
    zero = jnp.zeros((1, 1), F32)
    ms = []
    gcolc, growc = gcolc_s[...], growc_s[0]
    for hd in range(2 * H_B):
        hh = hd % H_B
        hs = slice(hh * DH_B, (hh + 1) * DH_B)
        _, m_new = _mlstm_chunk(None, ktc_s[0, hs, :], vc_s[:, hs], gcolc, growc, st_s, hd, zero,
                                hd < H_B, False, consts)
        ms.append(m_new)

    _mlstm_prep(t, xm_ref.at[0], *weights, pad_s, xc_s, q_s, kt_s, v_s, gcol_s, grow_s)

    def body(i, ms):
        new_ms = []
        for forward in (True, False):
            ci = i if forward else nc - 1 - i
            rows = pl.ds(pl.multiple_of(ci * L, L), L)
            gcol, grow = gcol_s[rows, :], grow_s[ci]
            for hh in range(H_B):
                hd = hh if forward else H_B + hh
                hs = slice(hh * DH_B, (hh + 1) * DH_B)
                h, m_new = _mlstm_chunk(q_s[rows, hs], kt_s[ci, hs, :], v_s[rows, hs], gcol, grow,
                                        st_s, hd, ms[hd], forward, True, consts)
                (hf_s if forward else hb_s)[rows, hs] = h
                new_ms.append(m_new)
        return tuple(new_ms)

    lax.fori_loop(0, nc, body, tuple(ms))

    def epilogue(c, carry):
        r0 = pl.multiple_of(c * L, L)
        rows = pl.ds(r0, L)
        hsum = (hf_s[rows, :] + hb_s[rows, :]) * som_ref[0, rows, :].astype(F32)
        parts = [_layer_norm(hsum[:, hh * DH_B:(hh + 1) * DH_B]) for hh in range(H_B)]
        hb = jnp.concatenate(parts, axis=1) * mhg_ref[...] + skip_ref[...] * xc_s[rows, :]
        o_ref[0, rows, :] = (hb * szm_ref[0, rows, :].astype(F32)).astype(BF16)
        return carry

    lax.fori_loop(0, nc, epilogue, 0)


def _mlstm_call(xm, xmc, som, szm, conv_w, conv_b, wq, wk, wkt, wv, wg, bg, mhg, skip):
    b, t, _ = xm.shape
    nc = t // CHUNK
    seq = lambda n: pl.BlockSpec((1, n, W_B), lambda i: (i, 0, 0))
    const = lambda shape: pl.BlockSpec(shape, lambda i: (0,) * len(shape))
    scratch = [
        pltpu.VMEM((t + 2 * _PAD, W_B), F32),
        pltpu.VMEM((t, W_B), F32),
        pltpu.VMEM((t, W_B), BF16),
        pltpu.VMEM((nc, W_B, CHUNK), BF16),
        pltpu.VMEM((t, W_B), BF16),
        pltpu.VMEM((t, LANES), F32),
        pltpu.VMEM((nc, 8, CHUNK), F32),
        pltpu.VMEM((1, W_B, CHUNK), BF16),
        pltpu.VMEM((CTX_LEN, W_B), BF16),
        pltpu.VMEM((CTX_LEN, LANES), F32),
        pltpu.VMEM((1, 8, CHUNK), F32),
        pltpu.VMEM((2 * H_B, DH_B, 2 * DH_B), F32),
        pltpu.VMEM((t, W_B), F32),
        pltpu.VMEM((t, W_B), F32),
    ]
    return pl.pallas_call(
        _mlstm_kernel,
        grid=(b,),
        in_specs=[seq(t), seq(CTX_LEN), seq(t), seq(t), const((3, W_B)), const((1, W_B)),
                  const((W_B, W_B)), const((W_B, W_B)), const((W_B, W_B)), const((W_B, W_B)),
                  const((3 * W_B, LANES)), const((1, LANES)), const((1, W_B)), const((1, W_B))],
        out_specs=seq(t),
        out_shape=jax.ShapeDtypeStruct((b, t, W_B), BF16),
        scratch_shapes=scratch,
        compiler_params=pltpu.CompilerParams(dimension_semantics=("arbitrary",),
                                             vmem_limit_bytes=VMEM_LIMIT),
        name="mlstm_bidir",
    )(xm, xmc, som, szm, conv_w, conv_b, wq, wk, wkt, wv, wg, bg, mhg, skip)


def _out_kernel(x_ref, mod_ref, lng_ref, lnb_ref, ya_ref, yb_ref, wo_ref, g_ref, b_ref, o_ref):
    h = _layer_norm(x_ref[0]) * lng_ref[...] + lnb_ref[...]
    gate = mod_ref[0][:, 2 * D_MODEL:3 * D_MODEL]
    y = _dot(ya_ref[0], wo_ref[0:W_A, :]) + _dot(yb_ref[0], wo_ref[W_A:W_A + W_B, :])
    o_ref[0] = _layer_norm(ALPHA * h + gate * y) * g_ref[...] + b_ref[...]


def _out_call(x, mod3, lng, lnb, ya, yb, wo, g, bb, *, tm):
    b, t, _ = x.shape
    const = lambda shape: pl.BlockSpec(shape, lambda i, j: (0,) * len(shape))
    row = lambda w: pl.BlockSpec((1, tm, w), lambda i, j: (i, j, 0))
    return pl.pallas_call(
        _out_kernel,
        grid=(b, t // tm),
        in_specs=[row(D_MODEL), pl.BlockSpec((1, 1, 3 * D_MODEL), lambda i, j: (i, 0, 0)),
                  const((1, D_MODEL)), const((1, D_MODEL)), row(W_A), row(W_B),
                  const((W_A + W_B, D_MODEL)), const((1, D_MODEL)), const((1, D_MODEL))],
        out_specs=row(D_MODEL),
        out_shape=jax.ShapeDtypeStruct((b, t, D_MODEL), F32),
        compiler_params=pltpu.CompilerParams(dimension_semantics=("parallel", "parallel"),
                                             vmem_limit_bytes=VMEM_LIMIT),
        name="out_proj_ln",
    )(x, mod3, lng, lnb, ya, yb, wo, g, bb)


def _rot_partner(w):
    w4 = w.reshape(w.shape[:-1] + (2, 2, ROPE_FREQS))
    return jnp.stack([-w4[..., 1, :], w4[..., 0, :]], axis=-2).reshape(w.shape)


def _rope_tables(seq, scale_keep, scale_rope):
    n_rows = seq // GRID_W
    rowp = jnp.repeat(jnp.arange(n_rows, dtype=F32), GRID_W)
    colp = jnp.tile(jnp.arange(GRID_W, dtype=F32), n_rows)
    inv = ROPE_BASE ** (-jnp.arange(ROPE_FREQS, dtype=F32) / ROPE_FREQS)
    ang = jnp.stack([rowp[:, None] * inv, colp[:, None] * inv], axis=1)
    cos = jnp.broadcast_to(jnp.cos(ang)[:, :, None, :], (seq, 2, 2, ROPE_FREQS)).reshape(seq, DR_A)
    sin = jnp.broadcast_to(jnp.sin(ang)[:, :, None, :], (seq, 2, 2, ROPE_FREQS)).reshape(seq, DR_A)
    z32 = jnp.zeros((seq, DR_A), F32)
    t1 = jnp.concatenate([jnp.full((seq, DN_A), scale_keep, F32), z32, cos * scale_rope], axis=1)
    t2 = jnp.concatenate([jnp.zeros((seq, DN_A), F32), z32, sin * scale_rope], axis=1)
    return t1, t2


def _block_diag(w):
    nb, bs, _ = w.shape
    eye = jnp.eye(nb, dtype=w.dtype)
    return jnp.einsum('gio,gh->giho', w, eye).reshape(nb * bs, nb * bs)


def kernel(x, c, ctx, c_ctx, ln_in_g, ln_in_b, w_ada, b_ada, w_in, g_qa, w_qb, g_kva, w_kvb, conv_w, conv_b, w_mq, w_mk, w_mv, w_gate, b_gate, mh_g, skip, w_out, ln_g, ln_b):
    b, t, _ = x.shape
    l = 0
    r2 = lambda v: v.reshape(1, -1)

    wi = w_in[l]
    s_qa, s_kva, s_kr, s_za, s_xm, s_om = 256, 384, 416, 928, 1440, 1952
    w_kr = wi[:, s_kva:s_kr]
    krblk = jnp.concatenate([jnp.zeros((D_MODEL, DN_A), F32), _rot_partner(w_kr), w_kr], axis=1)
    win = jnp.concatenate([wi[:, 0:s_qa], wi[:, s_qa:s_kva], krblk, wi[:, s_kr:s_za], wi[:, s_za:s_xm],
                           wi[:, s_xm:s_om], wi[:, s_om:]], axis=1).astype(BF16)
    wq3 = w_qb[l].reshape(Q_LORA, H_A, DN_A + DR_A)
    wq_r = wq3[..., DN_A:]
    wq = jnp.concatenate([wq3[..., :DN_A], _rot_partner(wq_r), wq_r], axis=-1)
    wq = wq.reshape(Q_LORA, H_A * SLAB).astype(BF16)
    wkv3 = w_kvb[l].reshape(KV_LORA, H_A, DN_A + DV_A)
    wkn = jnp.concatenate([wkv3[..., :DN_A], jnp.zeros((KV_LORA, H_A, SLAB - DN_A), F32)], axis=-1)
    wkn = wkn.reshape(KV_LORA, H_A * SLAB).astype(BF16)
    wvt = wkv3[..., DN_A:].reshape(KV_LORA, W_A).T.astype(BF16)
    wmq = _block_diag(w_mq[l]).astype(BF16)
    wmk = _block_diag(w_mk[l])
    wmkt = wmk.T.astype(BF16)
    wmk = wmk.astype(BF16)
    wmv = _block_diag(w_mv[l]).astype(BF16)
    wg = jnp.pad(w_gate[l], ((0, 0), (0, LANES - 4 * H_B))).astype(BF16)
    bg = jnp.pad(b_gate[l], (0, LANES - 4 * H_B)).reshape(1, LANES)
    wo = w_out[l].astype(BF16)

    sm_scale = (DN_A + DR_A) ** -0.5
    t1q, t2q = _rope_tables(t, sm_scale, sm_scale)
    t1k, t2k = _rope_tables(t, 0.0, 1.0)
    n_ctx = ctx.shape[1]
    t1kc = jnp.concatenate([jnp.zeros((n_ctx, DN_A + DR_A), F32), jnp.ones((n_ctx, DR_A), F32)], axis=1)
    t2kc = jnp.zeros((n_ctx, SLAB), F32)

    cc = jnp.concatenate([c, c_ctx[None, :], jnp.zeros((16 - b - 1, D_MODEL), F32)], axis=0)
    mod = _ada_call(cc, w_ada[l], r2(b_ada[l]))
    mod3 = mod.reshape(16, 1, 3 * D_MODEL)

    lng, lnb = r2(ln_in_g), r2(ln_in_b)
    shared = (win, r2(g_qa[l]), wq, r2(g_kva[l]), wkn, wvt)
    q, kl, vtl, xm, sza, som, szm = _proj_call(x, mod3, None, lng, lnb, *shared, t1q, t2q, t1k, t2k,
                                               tm=512, latent=True)
    kc, vtc, xmc = _proj_call(ctx, mod3, b, lng, lnb, *shared, t1kc, t2kc, t1kc, t2kc,
                              tm=n_ctx, latent=False)

    ya = _attn_call(q, kc, kl, vtc, vtl, sza, tq=512)
    yb = _mlstm_call(xm, xmc, som, szm, conv_w[l], r2(conv_b[l]), wmq, wmk, wmkt, wmv, wg, bg,
                     r2(mh_g[l]), r2(skip[l]))
    return _out_call(x, mod3, lng, lnb, ya, yb, wo, r2(ln_g[l]), r2(ln_b[l]), tm=512)
```

```python
import functools

import jax
import jax.numpy as jnp
from jax import lax
from jax.experimental import pallas as pl
from jax.experimental.pallas import tpu as pltpu

F32 = jnp.float32
BF16 = jnp.bfloat16

D_MODEL = 1024
CTX_LEN = 256
GRID_W = 64
H_A, DN_A, DR_A, DV_A = 8, 64, 32, 64
W_A = H_A * DV_A
Q_LORA, KV_LORA = 256, 128
ROPE_FREQS = DR_A // 4
ROPE_BASE = 10000.0
H_B, DH_B = 4, 128
W_B = H_B * DH_B
QKV_BS = 4
DEPTH = 1
ALPHA = (2.0 * DEPTH) ** 0.25
LN_EPS = 1e-5
RMS_EPS = 1e-6
LOG2_E = 1.4426950408889634

LANES = 128
MXU_TILE = 256
SLAB = LANES
DV_EXT = DV_A + 16
N_IN_PAD = 2560
CHUNK = 256
N_HD = 2 * H_B
ST_ROWS = DH_B + 16
VMEM_LIMIT = 56 * 1024 * 1024

_NT = (((1,), (1,)), ((), ()))


def _dot(a, b):
    return jnp.dot(a, b, preferred_element_type=F32)


def _dot_nt(a, b):
    return lax.dot_general(a, b, _NT, preferred_element_type=F32)


def _layer_norm(x):
    mu = jnp.mean(x, axis=-1, keepdims=True)
    xc = x - mu
    var = jnp.mean(xc * xc, axis=-1, keepdims=True)
    return xc * lax.rsqrt(var + LN_EPS)


def _rms_norm(x, g):
    return (x * lax.rsqrt(jnp.mean(x * x, axis=-1, keepdims=True) + RMS_EPS)) * g


def _silu(x):
    return x * jax.nn.sigmoid(x)


def _log_sigmoid(x):
    return jnp.minimum(x, 0.0) - jnp.log1p(jnp.exp(-jnp.abs(x)))


def _ada_kernel(cc_ref, w_ref, b_ref, o_ref):
    a = _silu(cc_ref[...]).astype(BF16)
    o_ref[...] = _dot(a, w_ref[...].astype(BF16)) + b_ref[...]


def _ada_call(cc, w_ada, b_ada):
    n = w_ada.shape[1]
    tn = 1024
    return pl.pallas_call(
        _ada_kernel,
        grid=(n // tn,),
        in_specs=[pl.BlockSpec((16, D_MODEL), lambda j: (0, 0)),
                  pl.BlockSpec((D_MODEL, tn), lambda j: (0, j)),
                  pl.BlockSpec((1, tn), lambda j: (0, j))],
        out_specs=pl.BlockSpec((16, tn), lambda j: (0, j)),
        out_shape=jax.ShapeDtypeStruct((16, n), F32),
        compiler_params=pltpu.CompilerParams(dimension_semantics=("arbitrary",),
                                             vmem_limit_bytes=VMEM_LIMIT),
        name="ada_mod",
    )(cc, w_ada, b_ada)


def _rope_slab(s, t1, t2):
    return s * t1 + pltpu.roll(s, 32, 1) * t2


def _proj_kernel(x_ref, mod_ref, lng_ref, lnb_ref, win_ref, gqa_ref, wq_ref, gkva_ref, wkn_ref,
                 wvt_ref, t1q_ref, t2q_ref, t1k_ref, t2k_ref, *out_refs, latent):
    if latent:
        q_ref, k_ref, vt_ref, xm_ref, sza_ref, som_ref, szm_ref = out_refs
    else:
        k_ref, vt_ref, xm_ref = out_refs
    mod = mod_ref[0]
    shift, scale = mod[:, 0:D_MODEL], mod[:, D_MODEL:2 * D_MODEL]
    h = _layer_norm(x_ref[0]) * lng_ref[...] + lnb_ref[...]
    u = (h * (1.0 + scale) + shift).astype(BF16)

    p0 = _dot(u, win_ref[:, 0:512])
    kvn = _rms_norm(p0[:, 256:384], gkva_ref[...]).astype(BF16)
    kn = _dot(kvn, wkn_ref[...])
    krr = _rope_slab(p0[:, 384:512], t1k_ref[...], t2k_ref[...])
    for hh in range(H_A):
        sl = slice(hh * SLAB, (hh + 1) * SLAB)
        k_ref[0, :, sl] = (kn[:, sl] + krr).astype(BF16)
    vt = _dot_nt(wvt_ref[...], kvn).astype(BF16)
    ones_rows = (lax.broadcasted_iota(jnp.int32, (DV_EXT - DV_A, vt.shape[1]), 0) == 0).astype(BF16)
    for hh in range(H_A):
        vt_ref[0, hh * DV_EXT:hh * DV_EXT + DV_A, :] = vt[hh * DV_A:(hh + 1) * DV_A, :]
        vt_ref[0, hh * DV_EXT + DV_A:(hh + 1) * DV_EXT, :] = ones_rows
    xm_ref[0] = _dot(u, win_ref[:, 1024:1536]).astype(BF16)
    if latent:
        qn = _rms_norm(p0[:, 0:256], gqa_ref[...]).astype(BF16)
        qs = _dot(qn, wq_ref[...])
        t1q, t2q = t1q_ref[...], t2q_ref[...]
        for hh in range(H_A):
            sl = slice(hh * SLAB, (hh + 1) * SLAB)
            q_ref[0, :, sl] = _rope_slab(qs[:, sl], t1q, t2q).astype(BF16)
        sza_ref[0] = _silu(_dot(u, win_ref[:, 512:1024])).astype(BF16)
        som_ref[0] = jax.nn.sigmoid(_dot(u, win_ref[:, 1536:2048])).astype(BF16)
        szm_ref[0] = _silu(_dot(u, win_ref[:, 2048:2560])).astype(BF16)


def _proj_call(x, mod3, mod_row, lng, lnb, win, gqa, wq, gkva, wkn, wvt, t1q, t2q, t1k, t2k, *, tm,
               latent):
    b, t, _ = x.shape
    const = lambda shape: pl.BlockSpec(shape, lambda i, j: (0,) * len(shape))
    tab = pl.BlockSpec((tm, SLAB), lambda i, j: (j, 0))
    row = lambda w: pl.BlockSpec((1, tm, w), lambda i, j: (i, j, 0))
    if mod_row is None:
        mod_spec = pl.BlockSpec((1, 1, 3 * D_MODEL), lambda i, j: (i, 0, 0))
    else:
        mod_spec = pl.BlockSpec((1, 1, 3 * D_MODEL), lambda i, j: (mod_row, 0, 0))
    in_specs = [row(D_MODEL), mod_spec, const((1, D_MODEL)), const((1, D_MODEL)),
                const((D_MODEL, N_IN_PAD)), const((1, Q_LORA)), const((Q_LORA, H_A * SLAB)),
                const((1, KV_LORA)), const((KV_LORA, H_A * SLAB)), const((W_A, KV_LORA)),
                tab, tab, tab, tab]
    vt_spec = pl.BlockSpec((1, H_A * DV_EXT, tm), lambda i, j: (i, 0, j))
    k_shape = jax.ShapeDtypeStruct((b, t, H_A * SLAB), BF16)
    vt_shape = jax.ShapeDtypeStruct((b, H_A * DV_EXT, t), BF16)
    half = jax.ShapeDtypeStruct((b, t, W_B), BF16)
    if latent:
        out_specs = [row(H_A * SLAB), row(H_A * SLAB), vt_spec, row(W_B), row(W_A), row(W_B), row(W_B)]
        out_shape = [k_shape, k_shape, vt_shape, half, half, half, half]
    else:
        out_specs = [row(H_A * SLAB), vt_spec, row(W_B)]
        out_shape = [k_shape, vt_shape, half]
    return pl.pallas_call(
        functools.partial(_proj_kernel, latent=latent),
        grid=(b, t // tm),
        in_specs=in_specs, out_specs=out_specs, out_shape=out_shape,
        compiler_params=pltpu.CompilerParams(dimension_semantics=("parallel", "parallel"),
                                             vmem_limit_bytes=VMEM_LIMIT),
        name="in_proj_latent" if latent else "in_proj_ctx",
    )(x, mod3, lng, lnb, win, gqa, wq, gkva, wkn, wvt, t1q, t2q, t1k, t2k)


def _attn_kernel(q_ref, kc_ref, kl_ref, vtc_ref, vtl_ref, sza_ref, o_ref):
    outs = []
    for hh in range(H_A):
        sl = slice(hh * SLAB, (hh + 1) * SLAB)
        vs = slice(hh * DV_EXT, (hh + 1) * DV_EXT)
        qh = q_ref[0, :, sl]
        sc = _dot_nt(kc_ref[0, :, sl], qh)
        sk = _dot_nt(kl_ref[0, :, sl], qh)
        m = jnp.maximum(jnp.max(sc, axis=0, keepdims=True), jnp.max(sk, axis=0, keepdims=True))
        pc = jnp.exp2(sc - m).astype(BF16)
        pk = jnp.exp2(sk - m).astype(BF16)
        o = _dot(vtc_ref[0, vs, :], pc) + _dot(vtl_ref[0, vs, :], pk)
        outs.append(o[0:DV_A, :] / o[DV_A:DV_A + 1, :])
    att = jnp.concatenate(outs, axis=0).T
    o_ref[0] = (att * sza_ref[0].astype(F32)).astype(BF16)


def _attn_call(q, kc, kl, vtc, vtl, sza, *, tq):
    b, t, _ = q.shape
    return pl.pallas_call(
        _attn_kernel,
        grid=(b, t // tq),
        in_specs=[pl.BlockSpec((1, tq, H_A * SLAB), lambda i, j: (i, j, 0)),
                  pl.BlockSpec((1, CTX_LEN, H_A * SLAB), lambda i, j: (i, 0, 0)),
                  pl.BlockSpec((1, t, H_A * SLAB), lambda i, j: (i, 0, 0)),
                  pl.BlockSpec((1, H_A * DV_EXT, CTX_LEN), lambda i, j: (i, 0, 0)),
                  pl.BlockSpec((1, H_A * DV_EXT, t), lambda i, j: (i, 0, 0)),
                  pl.BlockSpec((1, tq, W_A), lambda i, j: (i, j, 0))],
        out_specs=pl.BlockSpec((1, tq, W_A), lambda i, j: (i, j, 0)),
        out_shape=jax.ShapeDtypeStruct((b, t, W_A), BF16),
        compiler_params=pltpu.CompilerParams(dimension_semantics=("parallel", "parallel"),
                                             vmem_limit_bytes=VMEM_LIMIT),
        name="mla_attention",
    )(q, kc, kl, vtc, vtl, sza)


def _gate_fold_kernel(wq_ref, wk_ref, wv_ref, wg_ref, wgc_ref, wgm_ref):
    wgc_ref[...] = (_dot(wq_ref[...], wg_ref[0:W_B, :])
                    + _dot(wk_ref[...], wg_ref[W_B:2 * W_B, :])).astype(BF16)
    wgm_ref[...] = _dot(wv_ref[...], wg_ref[2 * W_B:3 * W_B, :]).astype(BF16)


def _gate_fold_call(wq, wk, wv, wg):
    shp = jax.ShapeDtypeStruct((W_B, LANES), BF16)
    return pl.pallas_call(_gate_fold_kernel, out_shape=[shp, shp],
                          compiler_params=pltpu.CompilerParams(vmem_limit_bytes=VMEM_LIMIT),
                          name="mlstm_gate_fold")(wq, wk, wv, wg)


_PAD = 8


def _lane_scans(pieces, combine, fill):
    lane = lax.broadcasted_iota(jnp.int32, pieces[0].shape, 1)
    pre, suf = [], []
    for x in pieces:
        p, s, sh = x, x, 1
        while sh < LANES:
            p = combine(p, jnp.where(lane >= sh, pltpu.roll(p, sh, 1), fill))
            s = combine(s, jnp.where(lane < LANES - sh, pltpu.roll(s, LANES - sh, 1), fill))
            sh *= 2
        pre.append(p)
        suf.append(s)
    tot = [p[:, LANES - 1:LANES] for p in pre]
    n = len(pieces)
    run = None
    for i in range(n):
        if run is not None:
            pre[i] = combine(pre[i], run)
        run = tot[i] if run is None else combine(run, tot[i])
    run = None
    for i in reversed(range(n)):
        if run is not None:
            suf[i] = combine(suf[i], run)
        run = tot[i] if run is None else combine(run, tot[i])
    return jnp.concatenate(pre, axis=1), jnp.concatenate(suf, axis=1)


def _mlstm_prep(n, xm_ref, cw_ref, cb_ref, wq_ref, wk_ref, wv_ref, wgc_ref, wgm_ref, bg_ref,
                pad_s, xc_s, xcb_s, q_s, k_s, vt_s, gcol_s, grow_s):
    L = CHUNK
    zrow = jnp.zeros((_PAD, W_B), F32)
    pad_s[0:_PAD, :] = zrow
    pad_s[_PAD:_PAD + n, :] = xm_ref[...].astype(F32)
    pad_s[_PAD + n:2 * _PAD + n, :] = zrow
    fwd_rows = lax.broadcasted_iota(jnp.int32, (N_HD, L), 0) < H_B
    for c in range(n // L):
        rows = slice(c * L, (c + 1) * L)
        r0 = c * L + _PAD
        pre = (cb_ref[...] + pad_s[r0 - 1:r0 - 1 + L, :] * cw_ref[0:1, :]
               + pad_s[r0:r0 + L, :] * cw_ref[1:2, :] + pad_s[r0 + 1:r0 + 1 + L, :] * cw_ref[2:3, :])
        xc = _silu(pre)
        if xc_s is not None:
            xc_s[rows, :] = xc
        xcb = xc.astype(BF16)
        xcb_s[rows, :] = xcb
        xmb = xm_ref[rows, :]
        g = _dot(xcb, wgc_ref[...]) + _dot(xmb, wgm_ref[...]) + bg_ref[...]
        gt = g.T[0:2 * N_HD, :]
        li = gt[0:N_HD, :]
        lf = _log_sigmoid(gt[N_HD:2 * N_HD, :])
        pieces = [lf[:, i * LANES:(i + 1) * LANES] for i in range(L // LANES)]
        psum, ssum = _lane_scans(pieces, jnp.add, 0.0)
        cum = jnp.where(fwd_rows, psum, ssum)
        r = li - cum
        pieces = [r[:, i * LANES:(i + 1) * LANES] for i in range(L // LANES)]
        pmax, smax = _lane_scans(pieces, jnp.maximum, -jnp.inf)
        grow_s[c] = jnp.concatenate([cum, r, jnp.where(fwd_rows, pmax, smax)], axis=0)
        gcol_s[rows, :] = jnp.concatenate([r, jnp.zeros((LANES - N_HD, L), F32)], axis=0).T
        for hf in range(W_B // MXU_TILE):
            cs = slice(hf * MXU_TILE, (hf + 1) * MXU_TILE)
            if q_s is not None:
                q_s[rows, cs] = _dot(xcb[:, cs], wq_ref[cs, cs]).astype(BF16)
            k_s[rows, cs] = _dot(xcb[:, cs], wk_ref[cs, cs]).astype(BF16)
            vt_s[c, cs, :] = _dot(xmb[:, cs], wv_ref[cs, cs]).T.astype(BF16)


def _chain_scalars(g_f, g_b, m0):
    L = CHUNK
    fwd_rows = lax.broadcasted_iota(jnp.int32, (N_HD, L), 0) < H_B
    pick = lambda a: jnp.where(fwd_rows, g_f[a * N_HD:(a + 1) * N_HD, :], g_b[a * N_HD:(a + 1) * N_HD, :])
    cum, r, pm = pick(0), pick(1), pick(2)
    b_end = jnp.where(fwd_rows[:, 0:1], cum[:, L - 1:L], cum[:, 0:1])
    mu = jnp.maximum(m0, pm)
    a_int = jnp.exp(m0 - mu)
    em = jnp.exp(-(cum + mu))
    w = r + b_end
    m_new = jnp.maximum(b_end + m0, jnp.max(w, axis=1, keepdims=True))
    ws = jnp.exp(w - m_new)
    decay = jnp.exp(b_end + m0 - m_new)
    return mu, a_int, em, ws, decay, m_new


def _state_update(st_ref, hd, vt_ext, k_c, ws_row, decay_hd):
    vw = (vt_ext.astype(F32) * ws_row).astype(BF16)
    st_ref[hd] = decay_hd * st_ref[hd] + _dot(vw, k_c)


def _mlstm_kernel(xm_ref, xmc_ref, som_ref, szm_ref, cw_ref, cb_ref, wq_ref, wk_ref, wv_ref,
                  wgc_ref, wgm_ref, bg_ref, mhg_ref, skip_ref, o_ref,
                  pad_s, xc_s, xcb_s, q_s, k_s, vt_s, gcol_s, grow_s, kc_s, vtc_s, gcolc_s, growc_s,
                  st_s, hf_s, hb_s):
    L = CHUNK
    t = xm_ref.shape[1]
    nc = t // L
    row_i = lax.broadcasted_iota(jnp.int32, (L, L), 0)
    col_i = lax.broadcasted_iota(jnp.int32, (L, L), 1)
    tri = (row_i <= col_i, row_i >= col_i)
    ones_rows = (lax.broadcasted_iota(jnp.int32, (ST_ROWS - DH_B, L), 0) == 0).astype(BF16)
    weights = (cw_ref, cb_ref, wq_ref, wk_ref, wv_ref, wgc_ref, wgm_ref, bg_ref)
    hsl = lambda hh: slice(hh * DH_B, (hh + 1) * DH_B)

    _mlstm_prep(CTX_LEN, xmc_ref.at[0], *weights, pad_s, None, xcb_s, None, kc_s, vtc_s,
                gcolc_s, growc_s)
    st_s[...] = jnp.zeros(st_s.shape, F32)
    gc = growc_s[0]
    _, _, _, ws, decay, m_ctx = _chain_scalars(gc, gc, jnp.zeros((N_HD, 1), F32))
    for hd in range(N_HD):
        hh = hd % H_B
        vt_ext = jnp.concatenate([vtc_s[0, hsl(hh), :], ones_rows], axis=0)
        _state_update(st_s, hd, vt_ext, kc_s[:, hsl(hh)], ws[hd:hd + 1, :], decay[hd:hd + 1, :])

    _mlstm_prep(t, xm_ref.at[0], *weights, pad_s, xc_s, xcb_s, q_s, k_s, vt_s, gcol_s, grow_s)

    def body(i, m0):
        ci = (i, nc - 1 - i)
        rows = tuple(pl.ds(pl.multiple_of(c * L, L), L) for c in ci)
        mu, a_int, em, ws, decay, m_new = _chain_scalars(grow_s[ci[0]], grow_s[ci[1]], m0)
        gcols = tuple(gcol_s[r, :] for r in rows)
        for hd in range(N_HD):
            d, hh = hd // H_B, hd % H_B
            q_c = q_s[rows[d], hsl(hh)]
            k_c = k_s[rows[d], hsl(hh)]
            vt_ext = jnp.concatenate([vt_s[ci[d], hsl(hh), :], ones_rows], axis=0)
            rcol = gcols[d][:, hd:hd + 1]
            e = jnp.exp(jnp.where(tri[d], rcol - mu[hd:hd + 1, :], -jnp.inf))
            lhs = jnp.concatenate([k_c, st_s[hd].astype(BF16)], axis=0)
            res = _dot_nt(lhs, q_c)
            p = (res[0:L, :] * e).astype(BF16)
            tot = a_int[hd:hd + 1, :] * res[L:L + ST_ROWS, :] + _dot(vt_ext, p)
            den = tot[DH_B:DH_B + 1, :]
            h_t = tot[0:DH_B, :] / jnp.maximum(jnp.abs(den), em[hd:hd + 1, :])
            (hf_s if d == 0 else hb_s)[ci[d], hsl(hh), :] = h_t
            _state_update(st_s, hd, vt_ext, k_c, ws[hd:hd + 1, :], decay[hd:hd + 1, :])
        return m_new

    lax.fori_loop(0, nc, body, m_ctx)

    def epilogue(c, carry):
        rows = pl.ds(pl.multiple_of(c * L, L), L)
        hsum = (hf_s[c] + hb_s[c]).T * som_ref[0, rows, :].astype(F32)
        parts = [_layer_norm(hsum[:, hsl(hh)]) for hh in range(H_B)]
        hb = jnp.concatenate(parts, axis=1) * mhg_ref[...] + skip_ref[...] * xc_s[rows, :]
        o_ref[0, rows, :] = (hb * szm_ref[0, rows, :].astype(F32)).astype(BF16)
        return carry

    lax.fori_loop(0, nc, epilogue, 0)


def _mlstm_call(xm, xmc, som, szm, conv_w, conv_b, wq, wk, wv, wgc, wgm, bg, mhg, skip):
    b, t, _ = xm.shape
    nc = t // CHUNK
    seq = lambda n: pl.BlockSpec((1, n, W_B), lambda i: (i, 0, 0))
    const = lambda shape: pl.BlockSpec(shape, lambda i: (0,) * len(shape))
    scratch = [
        pltpu.VMEM((t + 2 * _PAD, W_B), F32),
        pltpu.VMEM((t, W_B), F32),
        pltpu.VMEM((t, W_B), BF16),
        pltpu.VMEM((t, W_B), BF16),
        pltpu.VMEM((t, W_B), BF16),
        pltpu.VMEM((nc, W_B, CHUNK), BF16),
        pltpu.VMEM((t, LANES), F32),
        pltpu.VMEM((nc, 3 * N_HD, CHUNK), F32),
        pltpu.VMEM((CTX_LEN, W_B), BF16),
        pltpu.VMEM((1, W_B, CHUNK), BF16),
        pltpu.VMEM((CTX_LEN, LANES), F32),
        pltpu.VMEM((1, 3 * N_HD, CHUNK), F32),
        pltpu.VMEM((N_HD, ST_ROWS, DH_B), F32),
        pltpu.VMEM((nc, W_B, CHUNK), F32),
        pltpu.VMEM((nc, W_B, CHUNK), F32),
    ]
    return pl.pallas_call(
        _mlstm_kernel,
        grid=(b,),
        in_specs=[seq(t), seq(CTX_LEN), seq(t), seq(t), const((3, W_B)), const((1, W_B)),
                  const((W_B, W_B)), const((W_B, W_B)), const((W_B, W_B)),
                  const((W_B, LANES)), const((W_B, LANES)), const((1, LANES)),
                  const((1, W_B)), const((1, W_B))],
        out_specs=seq(t),
        out_shape=jax.ShapeDtypeStruct((b, t, W_B), BF16),
        scratch_shapes=scratch,
        compiler_params=pltpu.CompilerParams(dimension_semantics=("arbitrary",),
                                             vmem_limit_bytes=VMEM_LIMIT),
        name="mlstm_bidir",
    )(xm, xmc, som, szm, conv_w, conv_b, wq, wk, wv, wgc, wgm, bg, mhg, skip)


def _out_kernel(x_ref, mod_ref, lng_ref, lnb_ref, ya_ref, yb_ref, wo_ref, g_ref, b_ref, o_ref):
    h = _layer_norm(x_ref[0]) * lng_ref[...] + lnb_ref[...]
    gate = mod_ref[0][:, 2 * D_MODEL:3 * D_MODEL]
    y = _dot(ya_ref[0], wo_ref[0:W_A, :]) + _dot(yb_ref[0], wo_ref[W_A:W_A + W_B, :])
    o_ref[0] = _layer_norm(ALPHA * h + gate * y) * g_ref[...] + b_ref[...]


def _out_call(x, mod3, lng, lnb, ya, yb, wo, g, bb, *, tm):
    b, t, _ = x.shape
    const = lambda shape: pl.BlockSpec(shape, lambda i, j: (0,) * len(shape))
    row = lambda w: pl.BlockSpec((1, tm, w), lambda i, j: (i, j, 0))
    return pl.pallas_call(
        _out_kernel,
        grid=(b, t // tm),
        in_specs=[row(D_MODEL), pl.BlockSpec((1, 1, 3 * D_MODEL), lambda i, j: (i, 0, 0)),
                  const((1, D_MODEL)), const((1, D_MODEL)), row(W_A), row(W_B),
                  const((W_A + W_B, D_MODEL)), const((1, D_MODEL)), const((1, D_MODEL))],
        out_specs=row(D_MODEL),
        out_shape=jax.ShapeDtypeStruct((b, t, D_MODEL), F32),
        compiler_params=pltpu.CompilerParams(dimension_semantics=("parallel", "parallel"),
                                             vmem_limit_bytes=VMEM_LIMIT),
        name="out_proj_ln",
    )(x, mod3, lng, lnb, ya, yb, wo, g, bb)


def _rot_partner(w):
    w4 = w.reshape(w.shape[:-1] + (2, 2, ROPE_FREQS))
    return jnp.stack([-w4[..., 1, :], w4[..., 0, :]], axis=-2).reshape(w.shape)


def _rope_tables(seq, scale_keep, scale_rope):
    n_rows = seq // GRID_W
    rowp = jnp.repeat(jnp.arange(n_rows, dtype=F32), GRID_W)
    colp = jnp.tile(jnp.arange(GRID_W, dtype=F32), n_rows)
    inv = ROPE_BASE ** (-jnp.arange(ROPE_FREQS, dtype=F32) / ROPE_FREQS)
    ang = jnp.stack([rowp[:, None] * inv, colp[:, None] * inv], axis=1)
    cos = jnp.broadcast_to(jnp.cos(ang)[:, :, None, :], (seq, 2, 2, ROPE_FREQS)).reshape(seq, DR_A)
    sin = jnp.broadcast_to(jnp.sin(ang)[:, :, None, :], (seq, 2, 2, ROPE_FREQS)).reshape(seq, DR_A)
    z32 = jnp.zeros((seq, DR_A), F32)
    t1 = jnp.concatenate([jnp.full((seq, DN_A), scale_keep, F32), z32, cos * scale_rope], axis=1)
    t2 = jnp.concatenate([jnp.zeros((seq, DN_A), F32), z32, sin * scale_rope], axis=1)
    return t1, t2


def _block_diag(w):
    nb, bs, _ = w.shape
    n = nb * bs
    rows = w.reshape(n, bs)
    ri = lax.broadcasted_iota(jnp.int32, (n, n), 0)
    ci = lax.broadcasted_iota(jnp.int32, (n, n), 1)
    out = jnp.zeros((n, n), w.dtype)
    for o in range(bs):
        out = out + jnp.where((ri // bs == ci // bs) & (ci % bs == o), rows[:, o:o + 1], 0.0)
    return out


def kernel(x, c, ctx, c_ctx, ln_in_g, ln_in_b, w_ada, b_ada, w_in, g_qa, w_qb, g_kva, w_kvb, conv_w, conv_b, w_mq, w_mk, w_mv, w_gate, b_gate, mh_g, skip, w_out, ln_g, ln_b):
    b, t, _ = x.shape
    l = 0
    r2 = lambda v: v.reshape(1, -1)

    wi = w_in[l]
    s_qa, s_kva, s_kr, s_za, s_xm, s_om = 256, 384, 416, 928, 1440, 1952
    w_kr = wi[:, s_kva:s_kr]
    krblk = jnp.concatenate([jnp.zeros((D_MODEL, DN_A), F32), _rot_partner(w_kr), w_kr], axis=1)
    win = jnp.concatenate([wi[:, 0:s_qa], wi[:, s_qa:s_kva], krblk, wi[:, s_kr:s_za], wi[:, s_za:s_xm],
                           wi[:, s_xm:s_om], wi[:, s_om:]], axis=1).astype(BF16)
    wq3 = w_qb[l].reshape(Q_LORA, H_A, DN_A + DR_A)
    wq_r = wq3[..., DN_A:]
    wq = jnp.concatenate([wq3[..., :DN_A], _rot_partner(wq_r), wq_r], axis=-1)
    wq = wq.reshape(Q_LORA, H_A * SLAB).astype(BF16)
    wkv3 = w_kvb[l].reshape(KV_LORA, H_A, DN_A + DV_A)
    wkn = jnp.concatenate([wkv3[..., :DN_A], jnp.zeros((KV_LORA, H_A, SLAB - DN_A), F32)], axis=-1)
    wkn = wkn.reshape(KV_LORA, H_A * SLAB).astype(BF16)
    wvt = wkv3[..., DN_A:].reshape(KV_LORA, W_A).T.astype(BF16)
    wmq = _block_diag(w_mq[l])
    wmk = _block_diag(w_mk[l])
    wmv = _block_diag(w_mv[l])
    perm = jnp.array([0, 1, 2, 3, 8, 9, 10, 11, 4, 5, 6, 7, 12, 13, 14, 15])
    wg = jnp.pad(w_gate[l][:, perm], ((0, 0), (0, LANES - 2 * N_HD))).astype(BF16)
    bg = jnp.pad(b_gate[l][perm], (0, LANES - 2 * N_HD)).reshape(1, LANES)
    wgc, wgm = _gate_fold_call(wmq.astype(BF16), wmk.astype(BF16), wmv.astype(BF16), wg)
    wmk_s = (wmk * (DH_B ** -0.5)).astype(BF16)
    wo = w_out[l].astype(BF16)

    sm_scale = (DN_A + DR_A) ** -0.5 * LOG2_E
    t1q, t2q = _rope_tables(t, sm_scale, sm_scale)
    t1k, t2k = _rope_tables(t, 0.0, 1.0)
    n_ctx = ctx.shape[1]
    t1kc = jnp.concatenate([jnp.zeros((n_ctx, DN_A + DR_A), F32), jnp.ones((n_ctx, DR_A), F32)], axis=1)
    t2kc = jnp.zeros((n_ctx, SLAB), F32)

    cc = jnp.concatenate([c, c_ctx[None, :], jnp.zeros((16 - b - 1, D_MODEL), F32)], axis=0)
    mod = _ada_call(cc, w_ada[l], r2(b_ada[l]))
    mod3 = mod.reshape(16, 1, 3 * D_MODEL)

    lng, lnb = r2(ln_in_g), r2(ln_in_b)
    shared = (win, r2(g_qa[l]), wq, r2(g_kva[l]), wkn, wvt)
    q, kl, vtl, xm, sza, som, szm = _proj_call(x, mod3, None, lng, lnb, *shared, t1q, t2q, t1k, t2k,
                                               tm=512, latent=True)
    kc, vtc, xmc = _proj_call(ctx, mod3, b, lng, lnb, *shared, t1kc, t2kc, t1kc, t2kc,
                              tm=n_ctx, latent=False)

    ya = _attn_call(q, kc, kl, vtc, vtl, sza, tq=512)
    yb = _mlstm_call(xm, xmc, som, szm, conv_w[l], r2(conv_b[l]), wmq.astype(BF16), wmk_s,
                     wmv.astype(BF16), wgc, wgm, bg, r2(mh_g[l]), r2(skip[l]))
    return _out_call(x, mod3, lng, lnb, ya, yb, wo, r2(ln_g[l]), r2(ln_b[l]), tm=512)
```

```python
import functools

import jax
import jax.numpy as jnp
from jax import lax
from jax.experimental import pallas as pl
from jax.experimental.pallas import tpu as pltpu

F32 = jnp.float32
BF16 = jnp.bfloat16

D_MODEL = 1024
CTX_LEN = 256
GRID_W = 64
H_A, DN_A, DR_A, DV_A = 8, 64, 32, 64
W_A = H_A * DV_A
Q_LORA, KV_LORA = 256, 128
ROPE_FREQS = DR_A // 4
ROPE_BASE = 10000.0
H_B, DH_B = 4, 128
W_B = H_B * DH_B
QKV_BS = 4
DEPTH = 1
ALPHA = (2.0 * DEPTH) ** 0.25
LN_EPS = 1e-5
RMS_EPS = 1e-6
LOG2_E = 1.4426950408889634

LANES = 128
MXU_TILE = 256
SLAB = LANES
DV_EXT = DV_A + 16
N_IN_PAD = 2560
ATTN_KEY_CHUNK = MXU_TILE
CHUNK = 256
N_HD = 2 * H_B
ST_ROWS = DH_B + 16
VMEM_LIMIT = 56 * 1024 * 1024

_NT = (((1,), (1,)), ((), ()))


def _dot(a, b):
    return jnp.dot(a, b, preferred_element_type=F32)


def _dot_nt(a, b):
    return lax.dot_general(a, b, _NT, preferred_element_type=F32)


def _layer_norm(x):
    mu = jnp.mean(x, axis=-1, keepdims=True)
    xc = x - mu
    var = jnp.mean(xc * xc, axis=-1, keepdims=True)
    return xc * lax.rsqrt(var + LN_EPS)


def _rms_norm(x, g):
    return (x * lax.rsqrt(jnp.mean(x * x, axis=-1, keepdims=True) + RMS_EPS)) * g


def _silu(x):
    return x * jax.nn.sigmoid(x)


def _log_sigmoid(x):
    return jnp.minimum(x, 0.0) - jnp.log1p(jnp.exp(-jnp.abs(x)))


def _ada_kernel(cc_ref, w_ref, b_ref, o_ref):
    a = _silu(cc_ref[...]).astype(BF16)
    o_ref[...] = _dot(a, w_ref[...].astype(BF16)) + b_ref[...]


def _ada_call(cc, w_ada, b_ada):
    n = w_ada.shape[1]
    tn = 1024
    return pl.pallas_call(
        _ada_kernel,
        grid=(n // tn,),
        in_specs=[pl.BlockSpec((16, D_MODEL), lambda j: (0, 0)),
                  pl.BlockSpec((D_MODEL, tn), lambda j: (0, j)),
                  pl.BlockSpec((1, tn), lambda j: (0, j))],
        out_specs=pl.BlockSpec((16, tn), lambda j: (0, j)),
        out_shape=jax.ShapeDtypeStruct((16, n), F32),
        compiler_params=pltpu.CompilerParams(dimension_semantics=("arbitrary",),
                                             vmem_limit_bytes=VMEM_LIMIT),
        name="ada_mod",
    )(cc, w_ada, b_ada)


def _rope_slab(s, t1, t2):
    return s * t1 + pltpu.roll(s, 32, 1) * t2


def _proj_kernel(x_ref, mod_ref, lng_ref, lnb_ref, win_ref, gqa_ref, wq_ref, gkva_ref, wkn_ref,
                 wvt_ref, t1q_ref, t2q_ref, t1k_ref, t2k_ref, *out_refs, latent):
    if latent:
        q_ref, k_ref, vt_ref, xm_ref, sza_ref, som_ref, szm_ref = out_refs
    else:
        k_ref, vt_ref, xm_ref = out_refs
    mod = mod_ref[0]
    shift, scale = mod[:, 0:D_MODEL], mod[:, D_MODEL:2 * D_MODEL]
    h = _layer_norm(x_ref[0]) * lng_ref[...] + lnb_ref[...]
    u = (h * (1.0 + scale) + shift).astype(BF16)

    p0 = _dot(u, win_ref[:, 0:512])
    kvn = _rms_norm(p0[:, 256:384], gkva_ref[...]).astype(BF16)
    kn = _dot(kvn, wkn_ref[...])
    krr = _rope_slab(p0[:, 384:512], t1k_ref[...], t2k_ref[...])
    for hh in range(H_A):
        sl = slice(hh * SLAB, (hh + 1) * SLAB)
        k_ref[0, hh] = (kn[:, sl] + krr).astype(BF16)
    vt = _dot_nt(wvt_ref[...], kvn).astype(BF16)
    ones_rows = (lax.broadcasted_iota(jnp.int32, (DV_EXT - DV_A, vt.shape[1]), 0) == 0).astype(BF16)
    for hh in range(H_A):
        vt_ref[0, hh, 0:DV_A, :] = vt[hh * DV_A:(hh + 1) * DV_A, :]
        vt_ref[0, hh, DV_A:DV_EXT, :] = ones_rows
    xm_ref[0] = _dot(u, win_ref[:, 1024:1536]).astype(BF16)
    if latent:
        qn = _rms_norm(p0[:, 0:256], gqa_ref[...]).astype(BF16)
        qs = _dot(qn, wq_ref[...])
        t1q, t2q = t1q_ref[...], t2q_ref[...]
        for hh in range(H_A):
            sl = slice(hh * SLAB, (hh + 1) * SLAB)
            q_ref[0, hh] = _rope_slab(qs[:, sl], t1q, t2q).astype(BF16)
        sza_ref[0] = _silu(_dot(u, win_ref[:, 512:1024])).astype(BF16)
        som_ref[0] = jax.nn.sigmoid(_dot(u, win_ref[:, 1536:2048])).astype(BF16)
        szm_ref[0] = _silu(_dot(u, win_ref[:, 2048:2560])).astype(BF16)


def _proj_call(x, mod3, mod_row, lng, lnb, win, gqa, wq, gkva, wkn, wvt, t1q, t2q, t1k, t2k, *, tm,
               latent):
    b, t, _ = x.shape
    const = lambda shape: pl.BlockSpec(shape, lambda i, j: (0,) * len(shape))
    tab = pl.BlockSpec((tm, SLAB), lambda i, j: (j, 0))
    row = lambda w: pl.BlockSpec((1, tm, w), lambda i, j: (i, j, 0))
    if mod_row is None:
        mod_spec = pl.BlockSpec((1, 1, 3 * D_MODEL), lambda i, j: (i, 0, 0))
    else:
        mod_spec = pl.BlockSpec((1, 1, 3 * D_MODEL), lambda i, j: (mod_row, 0, 0))
    in_specs = [row(D_MODEL), mod_spec, const((1, D_MODEL)), const((1, D_MODEL)),
                const((D_MODEL, N_IN_PAD)), const((1, Q_LORA)), const((Q_LORA, H_A * SLAB)),
                const((1, KV_LORA)), const((KV_LORA, H_A * SLAB)), const((W_A, KV_LORA)),
                tab, tab, tab, tab]
    k_spec = pl.BlockSpec((1, H_A, tm, SLAB), lambda i, j: (i, 0, j, 0))
    vt_spec = pl.BlockSpec((1, H_A, DV_EXT, tm), lambda i, j: (i, 0, 0, j))
    k_shape = jax.ShapeDtypeStruct((b, H_A, t, SLAB), BF16)
    vt_shape = jax.ShapeDtypeStruct((b, H_A, DV_EXT, t), BF16)
    half = jax.ShapeDtypeStruct((b, t, W_B), BF16)
    if latent:
        out_specs = [k_spec, k_spec, vt_spec, row(W_B), row(W_A), row(W_B), row(W_B)]
        out_shape = [k_shape, k_shape, vt_shape, half, half, half, half]
    else:
        out_specs = [k_spec, vt_spec, row(W_B)]
        out_shape = [k_shape, vt_shape, half]
    return pl.pallas_call(
        functools.partial(_proj_kernel, latent=latent),
        grid=(b, t // tm),
        in_specs=in_specs, out_specs=out_specs, out_shape=out_shape,
        compiler_params=pltpu.CompilerParams(dimension_semantics=("parallel", "parallel"),
                                             vmem_limit_bytes=VMEM_LIMIT),
        name="in_proj_latent" if latent else "in_proj_ctx",
    )(x, mod3, lng, lnb, win, gqa, wq, gkva, wkn, wvt, t1q, t2q, t1k, t2k)


def _attn_kernel(q_ref, kc_ref, kl_ref, vtc_ref, vtl_ref, sza_ref, o_ref, s_buf, ot_s):
    n_ctx, t = kc_ref.shape[2], kl_ref.shape[2]
    kc = ATTN_KEY_CHUNK

    def scores(h, slot):
        qh = q_ref[0, h]
        sc = _dot_nt(kc_ref[0, h], qh)
        sk = _dot_nt(kl_ref[0, h], qh)
        s_buf[slot, 0:n_ctx, :] = sc
        s_buf[slot, n_ctx:n_ctx + t, :] = sk
        return jnp.maximum(jnp.max(sc, axis=0, keepdims=True), jnp.max(sk, axis=0, keepdims=True))

    def values(h, slot, m):
        chunks = [(vtc_ref, c * kc, c * kc) for c in range(n_ctx // kc)]
        chunks += [(vtl_ref, c * kc, n_ctx + c * kc) for c in range(t // kc)]
        acc = None
        for vref, v0, s0 in chunks:
            p = jnp.exp2(s_buf[slot, s0:s0 + kc, :] - m).astype(BF16)
            d = _dot(vref[0, h, :, v0:v0 + kc], p)
            acc = d if acc is None else acc + d
        rows = pl.ds(pl.multiple_of(h * DV_A, DV_A), DV_A)
        ot_s[rows, :] = acc[0:DV_A, :] / acc[DV_A:DV_A + 1, :]

    def body(i, m_even):
        h = 2 * i
        m_odd = scores(h + 1, 1)
        values(h, 0, m_even)
        m_even = scores(h + 2, 0)
        values(h + 1, 1, m_odd)
        return m_even

    m_even = lax.fori_loop(0, H_A // 2 - 1, body, scores(0, 0))
    m_odd = scores(H_A - 1, 1)
    values(H_A - 2, 0, m_even)
    values(H_A - 1, 1, m_odd)
    o_ref[0] = (ot_s[...].T * sza_ref[0].astype(F32)).astype(BF16)


def _attn_call(q, kc, kl, vtc, vtl, sza, *, tq):
    b, _, t, _ = q.shape
    n_ctx = kc.shape[2]
    head_blk = lambda n, w: pl.BlockSpec((1, H_A, n, w), lambda i, j: (i, 0, 0, 0))
    return pl.pallas_call(
        _attn_kernel,
        grid=(b, t // tq),
        in_specs=[pl.BlockSpec((1, H_A, tq, SLAB), lambda i, j: (i, 0, j, 0)),
                  head_blk(n_ctx, SLAB), head_blk(t, SLAB),
                  head_blk(DV_EXT, n_ctx), head_blk(DV_EXT, t),
                  pl.BlockSpec((1, tq, W_A), lambda i, j: (i, j, 0))],
        out_specs=pl.BlockSpec((1, tq, W_A), lambda i, j: (i, j, 0)),
        out_shape=jax.ShapeDtypeStruct((b, t, W_A), BF16),
        scratch_shapes=[pltpu.VMEM((2, n_ctx + t, tq), F32),
                        pltpu.VMEM((W_A, tq), F32)],
        compiler_params=pltpu.CompilerParams(dimension_semantics=("parallel", "parallel"),
                                             vmem_limit_bytes=VMEM_LIMIT),
        name="mla_attention",
    )(q, kc, kl, vtc, vtl, sza)


def _gate_fold_kernel(wq_ref, wk_ref, wv_ref, wg_ref, wgc_ref, wgm_ref):
    wgc_ref[...] = (_dot(wq_ref[...], wg_ref[0:W_B, :])
                    + _dot(wk_ref[...], wg_ref[W_B:2 * W_B, :])).astype(BF16)
    wgm_ref[...] = _dot(wv_ref[...], wg_ref[2 * W_B:3 * W_B, :]).astype(BF16)


def _gate_fold_call(wq, wk, wv, wg):
    shp = jax.ShapeDtypeStruct((W_B, LANES), BF16)
    return pl.pallas_call(_gate_fold_kernel, out_shape=[shp, shp],
                          compiler_params=pltpu.CompilerParams(vmem_limit_bytes=VMEM_LIMIT),
                          name="mlstm_gate_fold")(wq, wk, wv, wg)


_PAD = 8


def _lane_scans(pieces, combine, fill):
    lane = lax.broadcasted_iota(jnp.int32, pieces[0].shape, 1)
    pre, suf = [], []
    for x in pieces:
        p, s, sh = x, x, 1
        while sh < LANES:
            p = combine(p, jnp.where(lane >= sh, pltpu.roll(p, sh, 1), fill))
            s = combine(s, jnp.where(lane < LANES - sh, pltpu.roll(s, LANES - sh, 1), fill))
            sh *= 2
        pre.append(p)
        suf.append(s)
    tot = [p[:, LANES - 1:LANES] for p in pre]
    n = len(pieces)
    run = None
    for i in range(n):
        if run is not None:
            pre[i] = combine(pre[i], run)
        run = tot[i] if run is None else combine(run, tot[i])
    run = None
    for i in reversed(range(n)):
        if run is not None:
            suf[i] = combine(suf[i], run)
        run = tot[i] if run is None else combine(run, tot[i])
    return jnp.concatenate(pre, axis=1), jnp.concatenate(suf, axis=1)


def _mlstm_prep(n, xm_ref, cw_ref, cb_ref, wq_ref, wk_ref, wv_ref, wgc_ref, wgm_ref, bg_ref,
                pad_s, xc_s, xcb_s, q_s, k_s, vt_s, gcol_s, grow_s):
    L = CHUNK
    zrow = jnp.zeros((_PAD, W_B), F32)
    pad_s[0:_PAD, :] = zrow
    pad_s[_PAD:_PAD + n, :] = xm_ref[...].astype(F32)
    pad_s[_PAD + n:2 * _PAD + n, :] = zrow
    fwd_rows = lax.broadcasted_iota(jnp.int32, (N_HD, L), 0) < H_B
    for c in range(n // L):
        rows = slice(c * L, (c + 1) * L)
        r0 = c * L + _PAD
        pre = (cb_ref[...] + pad_s[r0 - 1:r0 - 1 + L, :] * cw_ref[0:1, :]
               + pad_s[r0:r0 + L, :] * cw_ref[1:2, :] + pad_s[r0 + 1:r0 + 1 + L, :] * cw_ref[2:3, :])
        xc = _silu(pre)
        if xc_s is not None:
            xc_s[rows, :] = xc
        xcb = xc.astype(BF16)
        xcb_s[rows, :] = xcb
        xmb = xm_ref[rows, :]
        g = _dot(xcb, wgc_ref[...]) + _dot(xmb, wgm_ref[...]) + bg_ref[...]
        gt = g.T[0:2 * N_HD, :]
        li = gt[0:N_HD, :]
        lf = _log_sigmoid(gt[N_HD:2 * N_HD, :])
        pieces = [lf[:, i * LANES:(i + 1) * LANES] for i in range(L // LANES)]
        psum, ssum = _lane_scans(pieces, jnp.add, 0.0)
        cum = jnp.where(fwd_rows, psum, ssum)
        r = li - cum
        pieces = [r[:, i * LANES:(i + 1) * LANES] for i in range(L // LANES)]
        pmax, smax = _lane_scans(pieces, jnp.maximum, -jnp.inf)
        grow_s[c] = jnp.concatenate([cum, r, jnp.where(fwd_rows, pmax, smax)], axis=0)
        gcol_s[rows, :] = jnp.concatenate([r, jnp.zeros((LANES - N_HD, L), F32)], axis=0).T
        for hf in range(W_B // MXU_TILE):
            cs = slice(hf * MXU_TILE, (hf + 1) * MXU_TILE)
            if q_s is not None:
                q_s[rows, cs] = _dot(xcb[:, cs], wq_ref[cs, cs]).astype(BF16)
            k_s[rows, cs] = _dot(xcb[:, cs], wk_ref[cs, cs]).astype(BF16)
            vt_s[c, cs, :] = _dot(xmb[:, cs], wv_ref[cs, cs]).T.astype(BF16)


def _chain_scalars(g_f, g_b, m0):
    L = CHUNK
    fwd_rows = lax.broadcasted_iota(jnp.int32, (N_HD, L), 0) < H_B
    pick = lambda a: jnp.where(fwd_rows, g_f[a * N_HD:(a + 1) * N_HD, :], g_b[a * N_HD:(a + 1) * N_HD, :])
    cum, r, pm = pick(0), pick(1), pick(2)
    b_end = jnp.where(fwd_rows[:, 0:1], cum[:, L - 1:L], cum[:, 0:1])
    mu = jnp.maximum(m0, pm)
    a_int = jnp.exp(m0 - mu)
    em = jnp.exp(-(cum + mu))
    w = r + b_end
    m_new = jnp.maximum(b_end + m0, jnp.max(w, axis=1, keepdims=True))
    ws = jnp.exp(w - m_new)
    decay = jnp.exp(b_end + m0 - m_new)
    return mu, a_int, em, ws, decay, m_new


def _state_update(st_ref, hd, vt_ext, k_c, ws_row, decay_hd):
    vw = (vt_ext.astype(F32) * ws_row).astype(BF16)
    st_ref[hd] = decay_hd * st_ref[hd] + _dot(vw, k_c)


def _mlstm_kernel(xm_ref, xmc_ref, som_ref, szm_ref, cw_ref, cb_ref, wq_ref, wk_ref, wv_ref,
                  wgc_ref, wgm_ref, bg_ref, mhg_ref, skip_ref, o_ref,
                  pad_s, xc_s, xcb_s, q_s, k_s, vt_s, gcol_s, grow_s, kc_s, vtc_s, gcolc_s, growc_s,
                  st_s, hf_s, hb_s):
    L = CHUNK
    t = xm_ref.shape[1]
    nc = t // L
    row_i = lax.broadcasted_iota(jnp.int32, (L, L), 0)
    col_i = lax.broadcasted_iota(jnp.int32, (L, L), 1)
    tri = (row_i <= col_i, row_i >= col_i)
    ones_rows = (lax.broadcasted_iota(jnp.int32, (ST_ROWS - DH_B, L), 0) == 0).astype(BF16)
    weights = (cw_ref, cb_ref, wq_ref, wk_ref, wv_ref, wgc_ref, wgm_ref, bg_ref)
    hsl = lambda hh: slice(hh * DH_B, (hh + 1) * DH_B)

    _mlstm_prep(CTX_LEN, xmc_ref.at[0], *weights, pad_s, None, xcb_s, None, kc_s, vtc_s,
                gcolc_s, growc_s)
    st_s[...] = jnp.zeros(st_s.shape, F32)
    gc = growc_s[0]
    _, _, _, ws, decay, m_ctx = _chain_scalars(gc, gc, jnp.zeros((N_HD, 1), F32))
    for hd in range(N_HD):
        hh = hd % H_B
        vt_ext = jnp.concatenate([vtc_s[0, hsl(hh), :], ones_rows], axis=0)
        _state_update(st_s, hd, vt_ext, kc_s[:, hsl(hh)], ws[hd:hd + 1, :], decay[hd:hd + 1, :])

    _mlstm_prep(t, xm_ref.at[0], *weights, pad_s, xc_s, xcb_s, q_s, k_s, vt_s, gcol_s, grow_s)

    def body(i, m0):
        ci = (i, nc - 1 - i)
        rows = tuple(pl.ds(pl.multiple_of(c * L, L), L) for c in ci)
        mu, a_int, em, ws, decay, m_new = _chain_scalars(grow_s[ci[0]], grow_s[ci[1]], m0)
        gcols = tuple(gcol_s[r, :] for r in rows)
        for hd in range(N_HD):
            d, hh = hd // H_B, hd % H_B
            q_c = q_s[rows[d], hsl(hh)]
            k_c = k_s[rows[d], hsl(hh)]
            vt_ext = jnp.concatenate([vt_s[ci[d], hsl(hh), :], ones_rows], axis=0)
            rcol = gcols[d][:, hd:hd + 1]
            e = jnp.exp(jnp.where(tri[d], rcol - mu[hd:hd + 1, :], -jnp.inf))
            lhs = jnp.concatenate([k_c, st_s[hd].astype(BF16)], axis=0)
            res = _dot_nt(lhs, q_c)
            p = (res[0:L, :] * e).astype(BF16)
            tot = a_int[hd:hd + 1, :] * res[L:L + ST_ROWS, :] + _dot(vt_ext, p)
            den = tot[DH_B:DH_B + 1, :]
            h_t = tot[0:DH_B, :] / jnp.maximum(jnp.abs(den), em[hd:hd + 1, :])
            (hf_s if d == 0 else hb_s)[ci[d], hsl(hh), :] = h_t
            _state_update(st_s, hd, vt_ext, k_c, ws[hd:hd + 1, :], decay[hd:hd + 1, :])
        return m_new

    lax.fori_loop(0, nc, body, m_ctx)

    def epilogue(c, carry):
        rows = pl.ds(pl.multiple_of(c * L, L), L)
        hsum = (hf_s[c] + hb_s[c]).T * som_ref[0, rows, :].astype(F32)
        parts = [_layer_norm(hsum[:, hsl(hh)]) for hh in range(H_B)]
        hb = jnp.concatenate(parts, axis=1) * mhg_ref[...] + skip_ref[...] * xc_s[rows, :]
        o_ref[0, rows, :] = (hb * szm_ref[0, rows, :].astype(F32)).astype(BF16)
        return carry

    lax.fori_loop(0, nc, epilogue, 0)


def _mlstm_call(xm, xmc, som, szm, conv_w, conv_b, wq, wk, wv, wgc, wgm, bg, mhg, skip):
    b, t, _ = xm.shape
    nc = t // CHUNK
    seq = lambda n: pl.BlockSpec((1, n, W_B), lambda i: (i, 0, 0))
    const = lambda shape: pl.BlockSpec(shape, lambda i: (0,) * len(shape))
    scratch = [
        pltpu.VMEM((t + 2 * _PAD, W_B), F32),
        pltpu.VMEM((t, W_B), F32),
        pltpu.VMEM((t, W_B), BF16),
        pltpu.VMEM((t, W_B), BF16),
        pltpu.VMEM((t, W_B), BF16),
        pltpu.VMEM((nc, W_B, CHUNK), BF16),
        pltpu.VMEM((t, LANES), F32),
        pltpu.VMEM((nc, 3 * N_HD, CHUNK), F32),
        pltpu.VMEM((CTX_LEN, W_B), BF16),
        pltpu.VMEM((1, W_B, CHUNK), BF16),
        pltpu.VMEM((CTX_LEN, LANES), F32),
        pltpu.VMEM((1, 3 * N_HD, CHUNK), F32),
        pltpu.VMEM((N_HD, ST_ROWS, DH_B), F32),
        pltpu.VMEM((nc, W_B, CHUNK), F32),
        pltpu.VMEM((nc, W_B, CHUNK), F32),
    ]
    return pl.pallas_call(
        _mlstm_kernel,
        grid=(b,),
        in_specs=[seq(t), seq(CTX_LEN), seq(t), seq(t), const((3, W_B)), const((1, W_B)),
                  const((W_B, W_B)), const((W_B, W_B)), const((W_B, W_B)),
                  const((W_B, LANES)), const((W_B, LANES)), const((1, LANES)),
                  const((1, W_B)), const((1, W_B))],
        out_specs=seq(t),
        out_shape=jax.ShapeDtypeStruct((b, t, W_B), BF16),
        scratch_shapes=scratch,
        compiler_params=pltpu.CompilerParams(dimension_semantics=("arbitrary",),
                                             vmem_limit_bytes=VMEM_LIMIT),
        name="mlstm_bidir",
    )(xm, xmc, som, szm, conv_w, conv_b, wq, wk, wv, wgc, wgm, bg, mhg, skip)


def _out_kernel(x_ref, mod_ref, lng_ref, lnb_ref, ya_ref, yb_ref, wo_ref, g_ref, b_ref, o_ref):
    h = _layer_norm(x_ref[0]) * lng_ref[...] + lnb_ref[...]
    gate = mod_ref[0][:, 2 * D_MODEL:3 * D_MODEL]
    y = _dot(ya_ref[0], wo_ref[0:W_A, :]) + _dot(yb_ref[0], wo_ref[W_A:W_A + W_B, :])
    o_ref[0] = _layer_norm(ALPHA * h + gate * y) * g_ref[...] + b_ref[...]


def _out_call(x, mod3, lng, lnb, ya, yb, wo, g, bb, *, tm):
    b, t, _ = x.shape
    const = lambda shape: pl.BlockSpec(shape, lambda i, j: (0,) * len(shape))
    row = lambda w: pl.BlockSpec((1, tm, w), lambda i, j: (i, j, 0))
    return pl.pallas_call(
        _out_kernel,
        grid=(b, t // tm),
        in_specs=[row(D_MODEL), pl.BlockSpec((1, 1, 3 * D_MODEL), lambda i, j: (i, 0, 0)),
                  const((1, D_MODEL)), const((1, D_MODEL)), row(W_A), row(W_B),
                  const((W_A + W_B, D_MODEL)), const((1, D_MODEL)), const((1, D_MODEL))],
        out_specs=row(D_MODEL),
        out_shape=jax.ShapeDtypeStruct((b, t, D_MODEL), F32),
        compiler_params=pltpu.CompilerParams(dimension_semantics=("parallel", "parallel"),
                                             vmem_limit_bytes=VMEM_LIMIT),
        name="out_proj_ln",
    )(x, mod3, lng, lnb, ya, yb, wo, g, bb)


def _rot_partner(w):
    w4 = w.reshape(w.shape[:-1] + (2, 2, ROPE_FREQS))
    return jnp.stack([-w4[..., 1, :], w4[..., 0, :]], axis=-2).reshape(w.shape)


def _rope_tables(seq, scale_keep, scale_rope):
    n_rows = seq // GRID_W
    rowp = jnp.repeat(jnp.arange(n_rows, dtype=F32), GRID_W)
    colp = jnp.tile(jnp.arange(GRID_W, dtype=F32), n_rows)
    inv = ROPE_BASE ** (-jnp.arange(ROPE_FREQS, dtype=F32) / ROPE_FREQS)
    ang = jnp.stack([rowp[:, None] * inv, colp[:, None] * inv], axis=1)
    cos = jnp.broadcast_to(jnp.cos(ang)[:, :, None, :], (seq, 2, 2, ROPE_FREQS)).reshape(seq, DR_A)
    sin = jnp.broadcast_to(jnp.sin(ang)[:, :, None, :], (seq, 2, 2, ROPE_FREQS)).reshape(seq, DR_A)
    z32 = jnp.zeros((seq, DR_A), F32)
    t1 = jnp.concatenate([jnp.full((seq, DN_A), scale_keep, F32), z32, cos * scale_rope], axis=1)
    t2 = jnp.concatenate([jnp.zeros((seq, DN_A), F32), z32, sin * scale_rope], axis=1)
    return t1, t2


def _block_diag(w):
    nb, bs, _ = w.shape
    n = nb * bs
    rows = w.reshape(n, bs)
    ri = lax.broadcasted_iota(jnp.int32, (n, n), 0)
    ci = lax.broadcasted_iota(jnp.int32, (n, n), 1)
    out = jnp.zeros((n, n), w.dtype)
    for o in range(bs):
        out = out + jnp.where((ri // bs == ci // bs) & (ci % bs == o), rows[:, o:o + 1], 0.0)
    return out


def kernel(x, c, ctx, c_ctx, ln_in_g, ln_in_b, w_ada, b_ada, w_in, g_qa, w_qb, g_kva, w_kvb, conv_w, conv_b, w_mq, w_mk, w_mv, w_gate, b_gate, mh_g, skip, w_out, ln_g, ln_b):
    b, t, _ = x.shape
    l = 0
    r2 = lambda v: v.reshape(1, -1)

    wi = w_in[l]
    s_qa, s_kva, s_kr, s_za, s_xm, s_om = 256, 384, 416, 928, 1440, 1952
    w_kr = wi[:, s_kva:s_kr]
    krblk = jnp.concatenate([jnp.zeros((D_MODEL, DN_A), F32), _rot_partner(w_kr), w_kr], axis=1)
    win = jnp.concatenate([wi[:, 0:s_qa], wi[:, s_qa:s_kva], krblk, wi[:, s_kr:s_za], wi[:, s_za:s_xm],
                           wi[:, s_xm:s_om], wi[:, s_om:]], axis=1).astype(BF16)
    wq3 = w_qb[l].reshape(Q_LORA, H_A, DN_A + DR_A)
    wq_r = wq3[..., DN_A:]
    wq = jnp.concatenate([wq3[..., :DN_A], _rot_partner(wq_r), wq_r], axis=-1)
    wq = wq.reshape(Q_LORA, H_A * SLAB).astype(BF16)
    wkv3 = w_kvb[l].reshape(KV_LORA, H_A, DN_A + DV_A)
    wkn = jnp.concatenate([wkv3[..., :DN_A], jnp.zeros((KV_LORA, H_A, SLAB - DN_A), F32)], axis=-1)
    wkn = wkn.reshape(KV_LORA, H_A * SLAB).astype(BF16)
    wvt = wkv3[..., DN_A:].reshape(KV_LORA, W_A).T.astype(BF16)
    wmq = _block_diag(w_mq[l])
    wmk = _block_diag(w_mk[l])
    wmv = _block_diag(w_mv[l])
    perm = jnp.array([0, 1, 2, 3, 8, 9, 10, 11, 4, 5, 6, 7, 12, 13, 14, 15])
    wg = jnp.pad(w_gate[l][:, perm], ((0, 0), (0, LANES - 2 * N_HD))).astype(BF16)
    bg = jnp.pad(b_gate[l][perm], (0, LANES - 2 * N_HD)).reshape(1, LANES)
    wgc, wgm = _gate_fold_call(wmq.astype(BF16), wmk.astype(BF16), wmv.astype(BF16), wg)
    wmk_s = (wmk * (DH_B ** -0.5)).astype(BF16)
    wo = w_out[l].astype(BF16)

    sm_scale = (DN_A + DR_A) ** -0.5 * LOG2_E
    t1q, t2q = _rope_tables(t, sm_scale, sm_scale)
    t1k, t2k = _rope_tables(t, 0.0, 1.0)
    n_ctx = ctx.shape[1]
    t1kc = jnp.concatenate([jnp.zeros((n_ctx, DN_A + DR_A), F32), jnp.ones((n_ctx, DR_A), F32)], axis=1)
    t2kc = jnp.zeros((n_ctx, SLAB), F32)

    cc = jnp.concatenate([c, c_ctx[None, :], jnp.zeros((16 - b - 1, D_MODEL), F32)], axis=0)
    mod = _ada_call(cc, w_ada[l], r2(b_ada[l]))
    mod3 = mod.reshape(16, 1, 3 * D_MODEL)

    lng, lnb = r2(ln_in_g), r2(ln_in_b)
    shared = (win, r2(g_qa[l]), wq, r2(g_kva[l]), wkn, wvt)
    q, kl, vtl, xm, sza, som, szm = _proj_call(x, mod3, None, lng, lnb, *shared, t1q, t2q, t1k, t2k,
                                               tm=512, latent=True)
    kc, vtc, xmc = _proj_call(ctx, mod3, b, lng, lnb, *shared, t1kc, t2kc, t1kc, t2kc,
                              tm=n_ctx, latent=False)

    ya = _attn_call(q, kc, kl, vtc, vtl, sza, tq=512)
    yb = _mlstm_call(xm, xmc, som, szm, conv_w[l], r2(conv_b[l]), wmq.astype(BF16), wmk_s,
                     wmv.astype(BF16), wgc, wgm, bg, r2(mh_g[l]), r2(skip[l]))
    return _out_call(x, mod3, lng, lnb, ya, yb, wo, r2(ln_g[l]), r2(ln_b[l]), tm=512)
```

```python
import functools

import jax
import jax.numpy as jnp
from jax import lax
from jax.experimental import pallas as pl
from jax.experimental.pallas import tpu as pltpu

F32 = jnp.float32
BF16 = jnp.bfloat16

D_MODEL = 1024
CTX_LEN = 256
GRID_W = 64
H_A, DN_A, DR_A, DV_A = 8, 64, 32, 64
W_A = H_A * DV_A
Q_LORA, KV_LORA = 256, 128
ROPE_FREQS = DR_A // 4
ROPE_BASE = 10000.0
H_B, DH_B = 4, 128
W_B = H_B * DH_B
QKV_BS = 4
DEPTH = 1
ALPHA = (2.0 * DEPTH) ** 0.25
LN_EPS = 1e-5
RMS_EPS = 1e-6
LOG2_E = 1.4426950408889634

LANES = 128
MXU_TILE = 256
SLAB = LANES
DV_EXT = DV_A + 16
N_IN_PAD = 2560
ATTN_KEY_CHUNK = MXU_TILE
CHUNK = 256
N_HD = 2 * H_B
ST_ROWS = DH_B + 16
VMEM_LIMIT = 56 * 1024 * 1024

_NT = (((1,), (1,)), ((), ()))


def _dot(a, b):
    return jnp.dot(a, b, preferred_element_type=F32)


def _dot_nt(a, b):
    return lax.dot_general(a, b, _NT, preferred_element_type=F32)


def _layer_norm(x):
    mu = jnp.mean(x, axis=-1, keepdims=True)
    xc = x - mu
    var = jnp.mean(xc * xc, axis=-1, keepdims=True)
    return xc * lax.rsqrt(var + LN_EPS)


def _rms_norm(x, g):
    return (x * lax.rsqrt(jnp.mean(x * x, axis=-1, keepdims=True) + RMS_EPS)) * g


def _silu(x):
    return x * jax.nn.sigmoid(x)


def _log_sigmoid(x):
    return jnp.minimum(x, 0.0) - jnp.log1p(jnp.exp(-jnp.abs(x)))


def _ada_kernel(cc_ref, w_ref, b_ref, o_ref):
    a = _silu(cc_ref[...]).astype(BF16)
    o_ref[...] = _dot(a, w_ref[...].astype(BF16)) + b_ref[...]


def _ada_call(cc, w_ada, b_ada):
    n = w_ada.shape[1]
    tn = 1024
    return pl.pallas_call(
        _ada_kernel,
        grid=(n // tn,),
        in_specs=[pl.BlockSpec((16, D_MODEL), lambda j: (0, 0)),
                  pl.BlockSpec((D_MODEL, tn), lambda j: (0, j)),
                  pl.BlockSpec((1, tn), lambda j: (0, j))],
        out_specs=pl.BlockSpec((16, tn), lambda j: (0, j)),
        out_shape=jax.ShapeDtypeStruct((16, n), F32),
        compiler_params=pltpu.CompilerParams(dimension_semantics=("arbitrary",),
                                             vmem_limit_bytes=VMEM_LIMIT),
        name="ada_mod",
    )(cc, w_ada, b_ada)


def _rope_slab(s, t1, t2):
    return s * t1 + pltpu.roll(s, 32, 1) * t2


def _proj_kernel(x_ref, mod_ref, lng_ref, lnb_ref, win_ref, gqa_ref, wq_ref, gkva_ref, wkn_ref,
                 wvt_ref, t1q_ref, t2q_ref, t1k_ref, t2k_ref, *out_refs, latent):
    if latent:
        q_ref, k_ref, vt_ref, xm_ref, sza_ref, som_ref, szm_ref = out_refs
    else:
        k_ref, vt_ref, xm_ref = out_refs
    mod = mod_ref[0]
    shift, scale = mod[:, 0:D_MODEL], mod[:, D_MODEL:2 * D_MODEL]
    h = _layer_norm(x_ref[0]) * lng_ref[...] + lnb_ref[...]
    u = (h * (1.0 + scale) + shift).astype(BF16)

    p0 = _dot(u, win_ref[:, 0:512])
    kvn = _rms_norm(p0[:, 256:384], gkva_ref[...]).astype(BF16)
    kn = _dot(kvn, wkn_ref[...])
    krr = _rope_slab(p0[:, 384:512], t1k_ref[...], t2k_ref[...])
    for hh in range(H_A):
        sl = slice(hh * SLAB, (hh + 1) * SLAB)
        k_ref[0, hh] = (kn[:, sl] + krr).astype(BF16)
    vt = _dot_nt(wvt_ref[...], kvn).astype(BF16)
    ones_rows = (lax.broadcasted_iota(jnp.int32, (DV_EXT - DV_A, vt.shape[1]), 0) == 0).astype(BF16)
    for hh in range(H_A):
        vt_ref[0, hh, 0:DV_A, :] = vt[hh * DV_A:(hh + 1) * DV_A, :]
        vt_ref[0, hh, DV_A:DV_EXT, :] = ones_rows
    xm_ref[0] = _dot(u, win_ref[:, 1024:1536]).astype(BF16)
    if latent:
        qn = _rms_norm(p0[:, 0:256], gqa_ref[...]).astype(BF16)
        qs = _dot(qn, wq_ref[...])
        t1q, t2q = t1q_ref[...], t2q_ref[...]
        for hh in range(H_A):
            sl = slice(hh * SLAB, (hh + 1) * SLAB)
            q_ref[0, hh] = _rope_slab(qs[:, sl], t1q, t2q).astype(BF16)
        sza_ref[0] = _silu(_dot(u, win_ref[:, 512:1024])).astype(BF16)
        som_ref[0] = jax.nn.sigmoid(_dot(u, win_ref[:, 1536:2048])).astype(BF16)
        szm_ref[0] = _silu(_dot(u, win_ref[:, 2048:2560])).astype(BF16)


def _proj_call(x, mod3, mod_row, lng, lnb, win, gqa, wq, gkva, wkn, wvt, t1q, t2q, t1k, t2k, *, tm,
               latent):
    b, t, _ = x.shape
    const = lambda shape: pl.BlockSpec(shape, lambda i, j: (0,) * len(shape))
    tab = pl.BlockSpec((tm, SLAB), lambda i, j: (j, 0))
    row = lambda w: pl.BlockSpec((1, tm, w), lambda i, j: (i, j, 0))
    if mod_row is None:
        mod_spec = pl.BlockSpec((1, 1, 3 * D_MODEL), lambda i, j: (i, 0, 0))
    else:
        mod_spec = pl.BlockSpec((1, 1, 3 * D_MODEL), lambda i, j: (mod_row, 0, 0))
    in_specs = [row(D_MODEL), mod_spec, const((1, D_MODEL)), const((1, D_MODEL)),
                const((D_MODEL, N_IN_PAD)), const((1, Q_LORA)), const((Q_LORA, H_A * SLAB)),
                const((1, KV_LORA)), const((KV_LORA, H_A * SLAB)), const((W_A, KV_LORA)),
                tab, tab, tab, tab]
    k_spec = pl.BlockSpec((1, H_A, tm, SLAB), lambda i, j: (i, 0, j, 0))
    vt_spec = pl.BlockSpec((1, H_A, DV_EXT, tm), lambda i, j: (i, 0, 0, j))
    k_shape = jax.ShapeDtypeStruct((b, H_A, t, SLAB), BF16)
    vt_shape = jax.ShapeDtypeStruct((b, H_A, DV_EXT, t), BF16)
    half = jax.ShapeDtypeStruct((b, t, W_B), BF16)
    if latent:
        out_specs = [k_spec, k_spec, vt_spec, row(W_B), row(W_A), row(W_B), row(W_B)]
        out_shape = [k_shape, k_shape, vt_shape, half, half, half, half]
    else:
        out_specs = [k_spec, vt_spec, row(W_B)]
        out_shape = [k_shape, vt_shape, half]
    return pl.pallas_call(
        functools.partial(_proj_kernel, latent=latent),
        grid=(b, t // tm),
        in_specs=in_specs, out_specs=out_specs, out_shape=out_shape,
        compiler_params=pltpu.CompilerParams(dimension_semantics=("parallel", "parallel"),
                                             vmem_limit_bytes=VMEM_LIMIT),
        name="in_proj_latent" if latent else "in_proj_ctx",
    )(x, mod3, lng, lnb, win, gqa, wq, gkva, wkn, wvt, t1q, t2q, t1k, t2k)


def _attn_kernel(q_ref, kc_ref, kl_ref, vtc_ref, vtl_ref, sza_ref, o_ref, s_buf, ot_s):
    n_ctx, t = kc_ref.shape[2], kl_ref.shape[2]
    kc = ATTN_KEY_CHUNK

    def scores(h, slot):
        qh = q_ref[0, h]
        sc = _dot_nt(kc_ref[0, h], qh)
        sk = _dot_nt(kl_ref[0, h], qh)
        s_buf[slot, 0:n_ctx, :] = sc
        s_buf[slot, n_ctx:n_ctx + t, :] = sk
        return jnp.maximum(jnp.max(sc, axis=0, keepdims=True), jnp.max(sk, axis=0, keepdims=True))

    def values(h, slot, m):
        chunks = [(vtc_ref, c * kc, c * kc) for c in range(n_ctx // kc)]
        chunks += [(vtl_ref, c * kc, n_ctx + c * kc) for c in range(t // kc)]
        acc = None
        for vref, v0, s0 in chunks:
            p = jnp.exp2(s_buf[slot, s0:s0 + kc, :] - m).astype(BF16)
            d = _dot(vref[0, h, :, v0:v0 + kc], p)
            acc = d if acc is None else acc + d
        rows = pl.ds(pl.multiple_of(h * DV_A, DV_A), DV_A)
        ot_s[rows, :] = acc[0:DV_A, :] / acc[DV_A:DV_A + 1, :]

    def body(i, m_even):
        h = 2 * i
        m_odd = scores(h + 1, 1)
        values(h, 0, m_even)
        m_even = scores(h + 2, 0)
        values(h + 1, 1, m_odd)
        return m_even

    m_even = lax.fori_loop(0, H_A // 2 - 1, body, scores(0, 0))
    m_odd = scores(H_A - 1, 1)
    values(H_A - 2, 0, m_even)
    values(H_A - 1, 1, m_odd)
    o_ref[0] = (ot_s[...].T * sza_ref[0].astype(F32)).astype(BF16)


def _attn_call(q, kc, kl, vtc, vtl, sza, *, tq):
    b, _, t, _ = q.shape
    n_ctx = kc.shape[2]
    head_blk = lambda n, w: pl.BlockSpec((1, H_A, n, w), lambda i, j: (i, 0, 0, 0))
    return pl.pallas_call(
        _attn_kernel,
        grid=(b, t // tq),
        in_specs=[pl.BlockSpec((1, H_A, tq, SLAB), lambda i, j: (i, 0, j, 0)),
                  head_blk(n_ctx, SLAB), head_blk(t, SLAB),
                  head_blk(DV_EXT, n_ctx), head_blk(DV_EXT, t),
                  pl.BlockSpec((1, tq, W_A), lambda i, j: (i, j, 0))],
        out_specs=pl.BlockSpec((1, tq, W_A), lambda i, j: (i, j, 0)),
        out_shape=jax.ShapeDtypeStruct((b, t, W_A), BF16),
        scratch_shapes=[pltpu.VMEM((2, n_ctx + t, tq), F32),
                        pltpu.VMEM((W_A, tq), F32)],
        compiler_params=pltpu.CompilerParams(dimension_semantics=("parallel", "parallel"),
                                             vmem_limit_bytes=VMEM_LIMIT),
        name="mla_attention",
    )(q, kc, kl, vtc, vtl, sza)


def _gate_fold_kernel(wq_ref, wk_ref, wv_ref, wg_ref, wgc_ref, wgm_ref):
    wgc_ref[...] = (_dot(wq_ref[...], wg_ref[0:W_B, :])
                    + _dot(wk_ref[...], wg_ref[W_B:2 * W_B, :])).astype(BF16)
    wgm_ref[...] = _dot(wv_ref[...], wg_ref[2 * W_B:3 * W_B, :]).astype(BF16)


def _gate_fold_call(wq, wk, wv, wg):
    shp = jax.ShapeDtypeStruct((W_B, LANES), BF16)
    return pl.pallas_call(_gate_fold_kernel, out_shape=[shp, shp],
                          compiler_params=pltpu.CompilerParams(vmem_limit_bytes=VMEM_LIMIT),
                          name="mlstm_gate_fold")(wq, wk, wv, wg)


_PAD = 8


def _lane_scans(rows_per_chunk, combine, fill, out):
    npc = CHUNK // LANES
    flat = [x[:, i * LANES:(i + 1) * LANES] for x in rows_per_chunk for i in range(npc)]
    lane = lax.broadcasted_iota(jnp.int32, flat[0].shape, 1)
    pre, suf, sh = list(flat), list(flat), 1
    while sh < LANES:
        pre = [combine(p, jnp.where(lane >= sh, pltpu.roll(p, sh, 1), fill)) for p in pre]
        suf = [combine(s, jnp.where(lane < LANES - sh, pltpu.roll(s, LANES - sh, 1), fill)) for s in suf]
        sh *= 2
        yield
    for c in range(len(rows_per_chunk)):
        p, s = pre[c * npc:(c + 1) * npc], suf[c * npc:(c + 1) * npc]
        tot = [x[:, LANES - 1:LANES] for x in p]
        run = None
        for i in range(npc):
            if run is not None:
                p[i] = combine(p[i], run)
            run = tot[i] if run is None else combine(run, tot[i])
        run = None
        for i in reversed(range(npc)):
            if run is not None:
                s[i] = combine(s[i], run)
            run = tot[i] if run is None else combine(run, tot[i])
        out.append((jnp.concatenate(p, axis=1), jnp.concatenate(s, axis=1)))


def _gate_tables(gts, dests):
    L = CHUNK
    fwd_rows = lax.broadcasted_iota(jnp.int32, (N_HD, L), 0) < H_B
    lis = [gt[0:N_HD, :] for gt in gts]
    sums = []
    yield from _lane_scans([_log_sigmoid(gt[N_HD:2 * N_HD, :]) for gt in gts], jnp.add, 0.0, sums)
    cums = [jnp.where(fwd_rows, ps, ss) for ps, ss in sums]
    rs = [li - cum for li, cum in zip(lis, cums)]
    maxs = []
    yield from _lane_scans(rs, jnp.maximum, -jnp.inf, maxs)
    for (grow_ref, c, gcol_ref, r0), cum, r, (pm, sm) in zip(dests, cums, rs, maxs):
        grow_ref[c] = jnp.concatenate([cum, r, jnp.where(fwd_rows, pm, sm)], axis=0)
        gcol_ref[r0:r0 + L, :] = jnp.concatenate([r, jnp.zeros((LANES - N_HD, L), F32)], axis=0).T


def _fill_pad(pad_s, xm_ref, n):
    zrow = jnp.zeros((_PAD, W_B), F32)
    pad_s[0:_PAD, :] = zrow
    pad_s[_PAD:_PAD + n, :] = xm_ref[...].astype(F32)
    pad_s[_PAD + n:2 * _PAD + n, :] = zrow


def _conv_gate_stage(c, pad_s, xm_ref, cw_ref, cb_ref, wgc_ref, wgm_ref, bg_ref, xc_s, xcb_s):
    L = CHUNK
    rows = slice(c * L, (c + 1) * L)
    r0 = c * L + _PAD
    row_id = lax.broadcasted_iota(jnp.int32, (L, W_B), 0)
    xcur = pad_s[r0:r0 + L, :]
    xprev = jnp.where(row_id == 0, pad_s[r0 - 1:r0, :], pltpu.roll(xcur, 1, 0))
    xnext = jnp.where(row_id == L - 1, pad_s[r0 + L:r0 + L + 1, :], pltpu.roll(xcur, L - 1, 0))
    pre = cb_ref[...] + xprev * cw_ref[0:1, :] + xcur * cw_ref[1:2, :] + xnext * cw_ref[2:3, :]
    xc = _silu(pre)
    if xc_s is not None:
        xc_s[rows, :] = xc
    xcb = xc.astype(BF16)
    xcb_s[rows, :] = xcb
    g = _dot(xcb, wgc_ref[...]) + _dot(xm_ref[rows, :], wgm_ref[...]) + bg_ref[...]
    return g.T[0:2 * N_HD, :]


def _headwise_stage(c, xm_ref, xcb_s, wq_ref, wk_ref, wv_ref, q_s, k_s, vt_s):
    L = CHUNK
    rows = slice(c * L, (c + 1) * L)
    for hf in range(W_B // MXU_TILE):
        cs = slice(hf * MXU_TILE, (hf + 1) * MXU_TILE)
        xcb = xcb_s[rows, cs]
        if q_s is not None:
            q_s[rows, cs] = _dot(xcb, wq_ref[cs, cs]).astype(BF16)
        k_s[rows, cs] = _dot(xcb, wk_ref[cs, cs]).astype(BF16)
        vt_s[c, cs, :] = _dot(xm_ref[rows, cs], wv_ref[cs, cs]).T.astype(BF16)


def _chain_scalars(g_f, g_b, m0):
    L = CHUNK
    fwd_rows = lax.broadcasted_iota(jnp.int32, (N_HD, L), 0) < H_B
    pick = lambda a: jnp.where(fwd_rows, g_f[a * N_HD:(a + 1) * N_HD, :], g_b[a * N_HD:(a + 1) * N_HD, :])
    cum, r, pm = pick(0), pick(1), pick(2)
    b_end = jnp.where(fwd_rows[:, 0:1], cum[:, L - 1:L], cum[:, 0:1])
    mu = jnp.maximum(m0, pm)
    a_int = jnp.exp(m0 - mu)
    em = jnp.exp(-(cum + mu))
    w = r + b_end
    m_new = jnp.maximum(b_end + m0, jnp.max(w, axis=1, keepdims=True))
    ws = jnp.exp(w - m_new)
    decay = jnp.exp(b_end + m0 - m_new)
    return mu, a_int, em, ws, decay, m_new


def _state_update(st_ref, hd, vt_ext, k_c, ws_row, decay_hd):
    vw = (vt_ext.astype(F32) * ws_row).astype(BF16)
    st_ref[hd] = decay_hd * st_ref[hd] + _dot(vw, k_c)


def _mlstm_kernel(xm_ref, xmc_ref, som_ref, szm_ref, cw_ref, cb_ref, wq_ref, wk_ref, wv_ref,
                  wgc_ref, wgm_ref, bg_ref, mhg_ref, skip_ref, o_ref,
                  pad_s, xc_s, xcb_s, q_s, k_s, vt_s, gcol_s, grow_s,
                  padc_s, xcbc_s, kc_s, vtc_s, gcolc_s, growc_s, st_s, hf_s, hb_s):
    L = CHUNK
    t = xm_ref.shape[1]
    nc = t // L
    row_i = lax.broadcasted_iota(jnp.int32, (L, L), 0)
    col_i = lax.broadcasted_iota(jnp.int32, (L, L), 1)
    tri = (row_i <= col_i, row_i >= col_i)
    ones_rows = (lax.broadcasted_iota(jnp.int32, (ST_ROWS - DH_B, L), 0) == 0).astype(BF16)
    hsl = lambda hh: slice(hh * DH_B, (hh + 1) * DH_B)
    xm, xmc = xm_ref.at[0], xmc_ref.at[0]
    gate_w = (cw_ref, cb_ref, wgc_ref, wgm_ref, bg_ref)

    _fill_pad(padc_s, xmc, CTX_LEN)
    _fill_pad(pad_s, xm, t)
    gts = [_conv_gate_stage(0, padc_s, xmc, *gate_w, None, xcbc_s)]
    gts += [_conv_gate_stage(c, pad_s, xm, *gate_w, xc_s, xcb_s) for c in range(nc)]
    dests = [(growc_s, 0, gcolc_s, 0)] + [(grow_s, c, gcol_s, c * L) for c in range(nc)]
    tables = _gate_tables(gts, dests)
    _headwise_stage(0, xmc, xcbc_s, wq_ref, wk_ref, wv_ref, None, kc_s, vtc_s)
    for c in range(nc):
        next(tables, None)
        next(tables, None)
        _headwise_stage(c, xm, xcb_s, wq_ref, wk_ref, wv_ref, q_s, k_s, vt_s)
    for _ in tables:
        pass

    st_s[...] = jnp.zeros(st_s.shape, F32)
    gc = growc_s[0]
    _, _, _, ws, decay, m_ctx = _chain_scalars(gc, gc, jnp.zeros((N_HD, 1), F32))
    for hd in range(N_HD):
        hh = hd % H_B
        vt_ext = jnp.concatenate([vtc_s[0, hsl(hh), :], ones_rows], axis=0)
        _state_update(st_s, hd, vt_ext, kc_s[:, hsl(hh)], ws[hd:hd + 1, :], decay[hd:hd + 1, :])

    def body(i, m0):
        ci = (i, nc - 1 - i)
        rows = tuple(pl.ds(pl.multiple_of(c * L, L), L) for c in ci)
        mu, a_int, em, ws, decay, m_new = _chain_scalars(grow_s[ci[0]], grow_s[ci[1]], m0)
        gcols = tuple(gcol_s[r, :] for r in rows)
        live = {}

        def stage_scores(hd):
            d, hh = hd // H_B, hd % H_B
            k_c = k_s[rows[d], hsl(hh)]
            vt_ext = jnp.concatenate([vt_s[ci[d], hsl(hh), :], ones_rows], axis=0)
            lhs = jnp.concatenate([k_c, st_s[hd].astype(BF16)], axis=0)
            live[hd] = (k_c, vt_ext, _dot_nt(lhs, q_s[rows[d], hsl(hh)]))

        def stage_gate(hd):
            d = hd // H_B
            k_c, vt_ext, res = live[hd]
            rcol = gcols[d][:, hd:hd + 1]
            e = jnp.exp(jnp.where(tri[d], rcol - mu[hd:hd + 1, :], -jnp.inf))
            p = (res[0:L, :] * e).astype(BF16)
            _state_update(st_s, hd, vt_ext, k_c, ws[hd:hd + 1, :], decay[hd:hd + 1, :])
            live[hd] = (vt_ext, res[L:L + ST_ROWS, :], p)

        def stage_out(hd):
            d, hh = hd // H_B, hd % H_B
            vt_ext, inter, p = live.pop(hd)
            tot = a_int[hd:hd + 1, :] * inter + _dot(vt_ext, p)
            den = tot[DH_B:DH_B + 1, :]
            h_t = tot[0:DH_B, :] / jnp.maximum(jnp.abs(den), em[hd:hd + 1, :])
            (hf_s if d == 0 else hb_s)[ci[d], hsl(hh), :] = h_t

        stages = (stage_scores, stage_gate, stage_out)
        for step in range(N_HD + len(stages) - 1):
            for si in reversed(range(len(stages))):
                if 0 <= step - si < N_HD:
                    stages[si](step - si)
        return m_new

    lax.fori_loop(0, nc, body, m_ctx)

    def epilogue(c, carry):
        rows = pl.ds(pl.multiple_of(c * L, L), L)
        hsum = (hf_s[c] + hb_s[c]).T * som_ref[0, rows, :].astype(F32)
        parts = [_layer_norm(hsum[:, hsl(hh)]) for hh in range(H_B)]
        hb = jnp.concatenate(parts, axis=1) * mhg_ref[...] + skip_ref[...] * xc_s[rows, :]
        o_ref[0, rows, :] = (hb * szm_ref[0, rows, :].astype(F32)).astype(BF16)
        return carry

    lax.fori_loop(0, nc, epilogue, 0)


def _mlstm_call(xm, xmc, som, szm, conv_w, conv_b, wq, wk, wv, wgc, wgm, bg, mhg, skip):
    b, t, _ = xm.shape
    nc = t // CHUNK
    seq = lambda n: pl.BlockSpec((1, n, W_B), lambda i: (i, 0, 0))
    const = lambda shape: pl.BlockSpec(shape, lambda i: (0,) * len(shape))
    scratch = [
        pltpu.VMEM((t + 2 * _PAD, W_B), F32),
        pltpu.VMEM((t, W_B), F32),
        pltpu.VMEM((t, W_B), BF16),
        pltpu.VMEM((t, W_B), BF16),
        pltpu.VMEM((t, W_B), BF16),
        pltpu.VMEM((nc, W_B, CHUNK), BF16),
        pltpu.VMEM((t, LANES), F32),
        pltpu.VMEM((nc, 3 * N_HD, CHUNK), F32),
        pltpu.VMEM((CTX_LEN + 2 * _PAD, W_B), F32),
        pltpu.VMEM((CTX_LEN, W_B), BF16),
        pltpu.VMEM((CTX_LEN, W_B), BF16),
        pltpu.VMEM((1, W_B, CHUNK), BF16),
        pltpu.VMEM((CTX_LEN, LANES), F32),
        pltpu.VMEM((1, 3 * N_HD, CHUNK), F32),
        pltpu.VMEM((N_HD, ST_ROWS, DH_B), F32),
        pltpu.VMEM((nc, W_B, CHUNK), F32),
        pltpu.VMEM((nc, W_B, CHUNK), F32),
    ]
    return pl.pallas_call(
        _mlstm_kernel,
        grid=(b,),
        in_specs=[seq(t), seq(CTX_LEN), seq(t), seq(t), const((3, W_B)), const((1, W_B)),
                  const((W_B, W_B)), const((W_B, W_B)), const((W_B, W_B)),
                  const((W_B, LANES)), const((W_B, LANES)), const((1, LANES)),
                  const((1, W_B)), const((1, W_B))],
        out_specs=seq(t),
        out_shape=jax.ShapeDtypeStruct((b, t, W_B), BF16),
        scratch_shapes=scratch,
        compiler_params=pltpu.CompilerParams(dimension_semantics=("arbitrary",),
                                             vmem_limit_bytes=VMEM_LIMIT),
        name="mlstm_bidir",
    )(xm, xmc, som, szm, conv_w, conv_b, wq, wk, wv, wgc, wgm, bg, mhg, skip)


def _out_kernel(x_ref, mod_ref, lng_ref, lnb_ref, ya_ref, yb_ref, wo_ref, g_ref, b_ref, o_ref):
    h = _layer_norm(x_ref[0]) * lng_ref[...] + lnb_ref[...]
    gate = mod_ref[0][:, 2 * D_MODEL:3 * D_MODEL]
    y = _dot(ya_ref[0], wo_ref[0:W_A, :]) + _dot(yb_ref[0], wo_ref[W_A:W_A + W_B, :])
    o_ref[0] = _layer_norm(ALPHA * h + gate * y) * g_ref[...] + b_ref[...]


def _out_call(x, mod3, lng, lnb, ya, yb, wo, g, bb, *, tm):
    b, t, _ = x.shape
    const = lambda shape: pl.BlockSpec(shape, lambda i, j: (0,) * len(shape))
    row = lambda w: pl.BlockSpec((1, tm, w), lambda i, j: (i, j, 0))
    return pl.pallas_call(
        _out_kernel,
        grid=(b, t // tm),
        in_specs=[row(D_MODEL), pl.BlockSpec((1, 1, 3 * D_MODEL), lambda i, j: (i, 0, 0)),
                  const((1, D_MODEL)), const((1, D_MODEL)), row(W_A), row(W_B),
                  const((W_A + W_B, D_MODEL)), const((1, D_MODEL)), const((1, D_MODEL))],
        out_specs=row(D_MODEL),
        out_shape=jax.ShapeDtypeStruct((b, t, D_MODEL), F32),
        compiler_params=pltpu.CompilerParams(dimension_semantics=("parallel", "parallel"),
                                             vmem_limit_bytes=VMEM_LIMIT),
        name="out_proj_ln",
    )(x, mod3, lng, lnb, ya, yb, wo, g, bb)


def _rot_partner(w):
    w4 = w.reshape(w.shape[:-1] + (2, 2, ROPE_FREQS))
    return jnp.stack([-w4[..., 1, :], w4[..., 0, :]], axis=-2).reshape(w.shape)


def _rope_tables(seq, scale_keep, scale_rope):
    n_rows = seq // GRID_W
    rowp = jnp.repeat(jnp.arange(n_rows, dtype=F32), GRID_W)
    colp = jnp.tile(jnp.arange(GRID_W, dtype=F32), n_rows)
    inv = ROPE_BASE ** (-jnp.arange(ROPE_FREQS, dtype=F32) / ROPE_FREQS)
    ang = jnp.stack([rowp[:, None] * inv, colp[:, None] * inv], axis=1)
    cos = jnp.broadcast_to(jnp.cos(ang)[:, :, None, :], (seq, 2, 2, ROPE_FREQS)).reshape(seq, DR_A)
    sin = jnp.broadcast_to(jnp.sin(ang)[:, :, None, :], (seq, 2, 2, ROPE_FREQS)).reshape(seq, DR_A)
    z32 = jnp.zeros((seq, DR_A), F32)
    t1 = jnp.concatenate([jnp.full((seq, DN_A), scale_keep, F32), z32, cos * scale_rope], axis=1)
    t2 = jnp.concatenate([jnp.zeros((seq, DN_A), F32), z32, sin * scale_rope], axis=1)
    return t1, t2


def _block_diag(w):
    nb, bs, _ = w.shape
    n = nb * bs
    rows = w.reshape(n, bs)
    ri = lax.broadcasted_iota(jnp.int32, (n, n), 0)
    ci = lax.broadcasted_iota(jnp.int32, (n, n), 1)
    out = jnp.zeros((n, n), w.dtype)
    for o in range(bs):
        out = out + jnp.where((ri // bs == ci // bs) & (ci % bs == o), rows[:, o:o + 1], 0.0)
    return out


def kernel(x, c, ctx, c_ctx, ln_in_g, ln_in_b, w_ada, b_ada, w_in, g_qa, w_qb, g_kva, w_kvb, conv_w, conv_b, w_mq, w_mk, w_mv, w_gate, b_gate, mh_g, skip, w_out, ln_g, ln_b):
    b, t, _ = x.shape
    l = 0
    r2 = lambda v: v.reshape(1, -1)

    wi = w_in[l]
    s_qa, s_kva, s_kr, s_za, s_xm, s_om = 256, 384, 416, 928, 1440, 1952
    w_kr = wi[:, s_kva:s_kr]
    krblk = jnp.concatenate([jnp.zeros((D_MODEL, DN_A), F32), _rot_partner(w_kr), w_kr], axis=1)
    win = jnp.concatenate([wi[:, 0:s_qa], wi[:, s_qa:s_kva], krblk, wi[:, s_kr:s_za], wi[:, s_za:s_xm],
                           wi[:, s_xm:s_om], wi[:, s_om:]], axis=1).astype(BF16)
    wq3 = w_qb[l].reshape(Q_LORA, H_A, DN_A + DR_A)
    wq_r = wq3[..., DN_A:]
    wq = jnp.concatenate([wq3[..., :DN_A], _rot_partner(wq_r), wq_r], axis=-1)
    wq = wq.reshape(Q_LORA, H_A * SLAB).astype(BF16)
    wkv3 = w_kvb[l].reshape(KV_LORA, H_A, DN_A + DV_A)
    wkn = jnp.concatenate([wkv3[..., :DN_A], jnp.zeros((KV_LORA, H_A, SLAB - DN_A), F32)], axis=-1)
    wkn = wkn.reshape(KV_LORA, H_A * SLAB).astype(BF16)
    wvt = wkv3[..., DN_A:].reshape(KV_LORA, W_A).T.astype(BF16)
    wmq = _block_diag(w_mq[l])
    wmk = _block_diag(w_mk[l])
    wmv = _block_diag(w_mv[l])
    perm = jnp.array([0, 1, 2, 3, 8, 9, 10, 11, 4, 5, 6, 7, 12, 13, 14, 15])
    wg = jnp.pad(w_gate[l][:, perm], ((0, 0), (0, LANES - 2 * N_HD))).astype(BF16)
    bg = jnp.pad(b_gate[l][perm], (0, LANES - 2 * N_HD)).reshape(1, LANES)
    wgc, wgm = _gate_fold_call(wmq.astype(BF16), wmk.astype(BF16), wmv.astype(BF16), wg)
    wmk_s = (wmk * (DH_B ** -0.5)).astype(BF16)
    wo = w_out[l].astype(BF16)

    sm_scale = (DN_A + DR_A) ** -0.5 * LOG2_E
    t1q, t2q = _rope_tables(t, sm_scale, sm_scale)
    t1k, t2k = _rope_tables(t, 0.0, 1.0)
    n_ctx = ctx.shape[1]
    t1kc = jnp.concatenate([jnp.zeros((n_ctx, DN_A + DR_A), F32), jnp.ones((n_ctx, DR_A), F32)], axis=1)
    t2kc = jnp.zeros((n_ctx, SLAB), F32)

    cc = jnp.concatenate([c, c_ctx[None, :], jnp.zeros((16 - b - 1, D_MODEL), F32)], axis=0)
    mod = _ada_call(cc, w_ada[l], r2(b_ada[l]))
    mod3 = mod.reshape(16, 1, 3 * D_MODEL)

    lng, lnb = r2(ln_in_g), r2(ln_in_b)
    shared = (win, r2(g_qa[l]), wq, r2(g_kva[l]), wkn, wvt)
    q, kl, vtl, xm, sza, som, szm = _proj_call(x, mod3, None, lng, lnb, *shared, t1q, t2q, t1k, t2k,
                                               tm=512, latent=True)
    kc, vtc, xmc = _proj_call(ctx, mod3, b, lng, lnb, *shared, t1kc, t2kc, t1kc, t2kc,
                              tm=n_ctx, latent=False)

    ya = _attn_call(q, kc, kl, vtc, vtl, sza, tq=512)
    yb = _mlstm_call(xm, xmc, som, szm, conv_w[l], r2(conv_b[l]), wmq.astype(BF16), wmk_s,
                     wmv.astype(BF16), wgc, wgm, bg, r2(mh_g[l]), r2(skip[l]))
    return _out_call(x, mod3, lng, lnb, ya, yb, wo, r2(ln_g[l]), r2(ln_b[l]), tm=512)
```

```python
import functools

import numpy as np
import jax
import jax.numpy as jnp
from jax import lax
from jax.experimental import pallas as pl
from jax.experimental.pallas import tpu as pltpu

F32 = jnp.float32
BF16 = jnp.bfloat16

D_MODEL = 1024
CTX_LEN = 256
GRID_W = 64
H_A, DN_A, DR_A, DV_A = 8, 64, 32, 64
W_A = H_A * DV_A
Q_LORA, KV_LORA = 256, 128
ROPE_FREQS = DR_A // 4
ROPE_BASE = 10000.0
H_B, DH_B = 4, 128
W_B = H_B * DH_B
QKV_BS = 4
DEPTH = 1
ALPHA = (2.0 * DEPTH) ** 0.25
LN_EPS = 1e-5
RMS_EPS = 1e-6
LOG2_E = 1.4426950408889634

LANES = 128
MXU_TILE = 256
SLAB = LANES
DV_EXT = DV_A + 16
ATTN_KEY_CHUNK = MXU_TILE
CHUNK = 256
N_HD = 2 * H_B
ST_ROWS = DH_B + 16
VMEM_LIMIT = 56 * 1024 * 1024

_NT = (((1,), (1,)), ((), ()))


def _dot(a, b):
    return jnp.dot(a, b, preferred_element_type=F32)


def _dot_nt(a, b):
    return lax.dot_general(a, b, _NT, preferred_element_type=F32)


def _layer_norm(x):
    mu = jnp.mean(x, axis=-1, keepdims=True)
    xc = x - mu
    var = jnp.mean(xc * xc, axis=-1, keepdims=True)
    return xc * lax.rsqrt(var + LN_EPS)


def _rms_norm(x, g):
    return (x * lax.rsqrt(jnp.mean(x * x, axis=-1, keepdims=True) + RMS_EPS)) * g


def _silu(x):
    return x * jax.nn.sigmoid(x)


def _log_sigmoid(x):
    return jnp.minimum(x, 0.0) - jnp.log1p(jnp.exp(-jnp.abs(x)))


def _ada_kernel(cc_ref, w_ref, b_ref, o_ref):
    a = _silu(cc_ref[...]).astype(BF16)
    o_ref[...] = _dot(a, w_ref[...].astype(BF16)) + b_ref[...]


def _ada_call(cc, w_ada, b_ada):
    n = w_ada.shape[1]
    tn = 1024
    return pl.pallas_call(
        _ada_kernel,
        grid=(n // tn,),
        in_specs=[pl.BlockSpec((16, D_MODEL), lambda j: (0, 0)),
                  pl.BlockSpec((D_MODEL, tn), lambda j: (0, j)),
                  pl.BlockSpec((1, tn), lambda j: (0, j))],
        out_specs=pl.BlockSpec((16, tn), lambda j: (0, j)),
        out_shape=jax.ShapeDtypeStruct((16, n), F32),
        compiler_params=pltpu.CompilerParams(dimension_semantics=("arbitrary",),
                                             vmem_limit_bytes=VMEM_LIMIT),
        name="ada_mod",
    )(cc, w_ada, b_ada)


def _rope_slab(s, t1, t2):
    return s * t1 + pltpu.roll(s, 32, 1) * t2


def _proj_kernel(x_ref, mod_ref, lng_ref, lnb_ref, wina_ref, winr_ref, gqa_ref, wq_ref, gkva_ref,
                 wkn_ref, wvt_ref, t1q_ref, t2q_ref, t1k_ref, t2k_ref, *out_refs, latent, n_sub):
    if latent:
        q_ref, k_ref, vt_ref, xm_ref, sza_ref, som_ref, szm_ref = out_refs
    else:
        k_ref, vt_ref, xm_ref = out_refs
    sub = x_ref.shape[1] // n_sub
    mod = mod_ref[0]
    shift, scale1 = mod[:, 0:D_MODEL], 1.0 + mod[:, D_MODEL:2 * D_MODEL]
    ones_rows = (lax.broadcasted_iota(jnp.int32, (DV_EXT - DV_A, sub), 0) == 0).astype(BF16)

    def norm_stage(s):
        h = _layer_norm(x_ref[0, s * sub:(s + 1) * sub, :]) * lng_ref[...] + lnb_ref[...]
        return (h * scale1 + shift).astype(BF16)

    def proj_stage(s, u):
        rows = slice(s * sub, (s + 1) * sub)
        p0 = _dot(u, wina_ref[...])
        kvn = _rms_norm(p0[:, 256:384], gkva_ref[...]).astype(BF16)
        kn = _dot(kvn, wkn_ref[...])
        krr = _rope_slab(p0[:, 384:512], t1k_ref[rows, :], t2k_ref[rows, :])
        for hh in range(H_A):
            k_ref[0, hh, rows, :] = (kn[:, hh * SLAB:(hh + 1) * SLAB] + krr).astype(BF16)
        vt = _dot_nt(wvt_ref[...], kvn).astype(BF16)
        for hh in range(H_A):
            vt_ref[0, hh, 0:DV_A, rows] = vt[hh * DV_A:(hh + 1) * DV_A, :]
            vt_ref[0, hh, DV_A:DV_EXT, rows] = ones_rows
        xm_ref[0, rows, :] = _dot(u, winr_ref[:, 512:1024]).astype(BF16)
        if latent:
            qn = _rms_norm(p0[:, 0:256], gqa_ref[...]).astype(BF16)
            qs = _dot(qn, wq_ref[...])
            t1q, t2q = t1q_ref[rows, :], t2q_ref[rows, :]
            for hh in range(H_A):
                q_ref[0, hh, rows, :] = _rope_slab(qs[:, hh * SLAB:(hh + 1) * SLAB], t1q, t2q).astype(BF16)
            sza_ref[0, rows, :] = _silu(_dot(u, winr_ref[:, 0:512])).astype(BF16)
            som_ref[0, rows, :] = jax.nn.sigmoid(_dot(u, winr_ref[:, 1024:1536])).astype(BF16)
            szm_ref[0, rows, :] = _silu(_dot(u, winr_ref[:, 1536:2048])).astype(BF16)

    u = norm_stage(0)
    for s in range(n_sub):
        u_next = norm_stage(s + 1) if s + 1 < n_sub else None
        proj_stage(s, u)
        u = u_next


def _proj_call(x, mod3, mod_row, lng, lnb, wina, winr, gqa, wq, gkva, wkn, wvt, t1q, t2q, t1k, t2k, *,
               tm, n_sub, latent):
    b, t, _ = x.shape
    const = lambda shape: pl.BlockSpec(shape, lambda i, j: (0,) * len(shape))
    tab = pl.BlockSpec((tm, SLAB), lambda i, j: (j, 0))
    row = lambda w: pl.BlockSpec((1, tm, w), lambda i, j: (i, j, 0))
    if mod_row is None:
        mod_spec = pl.BlockSpec((1, 1, 3 * D_MODEL), lambda i, j: (i, 0, 0))
    else:
        mod_spec = pl.BlockSpec((1, 1, 3 * D_MODEL), lambda i, j: (mod_row, 0, 0))
    in_specs = [row(D_MODEL), mod_spec, const((1, D_MODEL)), const((1, D_MODEL)),
                const(wina.shape), const(winr.shape), const((1, Q_LORA)), const((Q_LORA, H_A * SLAB)),
                const((1, KV_LORA)), const((KV_LORA, H_A * SLAB)), const((W_A, KV_LORA)),
                tab, tab, tab, tab]
    k_spec = pl.BlockSpec((1, H_A, tm, SLAB), lambda i, j: (i, 0, j, 0))
    vt_spec = pl.BlockSpec((1, H_A, DV_EXT, tm), lambda i, j: (i, 0, 0, j))
    k_shape = jax.ShapeDtypeStruct((b, H_A, t, SLAB), BF16)
    vt_shape = jax.ShapeDtypeStruct((b, H_A, DV_EXT, t), BF16)
    half = jax.ShapeDtypeStruct((b, t, W_B), BF16)
    if latent:
        out_specs = [k_spec, k_spec, vt_spec, row(W_B), row(W_A), row(W_B), row(W_B)]
        out_shape = [k_shape, k_shape, vt_shape, half, half, half, half]
    else:
        out_specs = [k_spec, vt_spec, row(W_B)]
        out_shape = [k_shape, vt_shape, half]
    return pl.pallas_call(
        functools.partial(_proj_kernel, latent=latent, n_sub=n_sub),
        grid=(b, t // tm),
        in_specs=in_specs, out_specs=out_specs, out_shape=out_shape,
        compiler_params=pltpu.CompilerParams(dimension_semantics=("parallel", "parallel"),
                                             vmem_limit_bytes=VMEM_LIMIT),
        name="in_proj_latent" if latent else "in_proj_ctx",
    )(x, mod3, lng, lnb, wina, winr, gqa, wq, gkva, wkn, wvt, t1q, t2q, t1k, t2k)


def _attn_kernel(q_ref, kc_ref, kl_ref, vtc_ref, vtl_ref, sza_ref, o_ref, s_buf, ot_s):
    n_ctx, t = kc_ref.shape[2], kl_ref.shape[2]
    kc = ATTN_KEY_CHUNK

    def scores(h, slot):
        qh = q_ref[0, h]
        sc = _dot_nt(kc_ref[0, h], qh)
        sk = _dot_nt(kl_ref[0, h], qh)
        s_buf[slot, 0:n_ctx, :] = sc
        s_buf[slot, n_ctx:n_ctx + t, :] = sk
        return jnp.maximum(jnp.max(sc, axis=0, keepdims=True), jnp.max(sk, axis=0, keepdims=True))

    def values(h, slot, m):
        chunks = [(vtc_ref, c * kc, c * kc) for c in range(n_ctx // kc)]
        chunks += [(vtl_ref, c * kc, n_ctx + c * kc) for c in range(t // kc)]
        acc = None
        for vref, v0, s0 in chunks:
            p = jnp.exp2(s_buf[slot, s0:s0 + kc, :] - m).astype(BF16)
            d = _dot(vref[0, h, :, v0:v0 + kc], p)
            acc = d if acc is None else acc + d
        rows = pl.ds(pl.multiple_of(h * DV_A, DV_A), DV_A)
        ot_s[rows, :] = acc[0:DV_A, :] / acc[DV_A:DV_A + 1, :]

    def body(i, m_even):
        h = 2 * i
        m_odd = scores(h + 1, 1)
        values(h, 0, m_even)
        m_even = scores(h + 2, 0)
        values(h + 1, 1, m_odd)
        return m_even

    m_even = lax.fori_loop(0, H_A // 2 - 1, body, scores(0, 0))
    m_odd = scores(H_A - 1, 1)
    values(H_A - 2, 0, m_even)
    values(H_A - 1, 1, m_odd)
    o_ref[0] = (ot_s[...].T * sza_ref[0].astype(F32)).astype(BF16)


def _attn_call(q, kc, kl, vtc, vtl, sza, *, tq):
    b, _, t, _ = q.shape
    n_ctx = kc.shape[2]
    head_blk = lambda n, w: pl.BlockSpec((1, H_A, n, w), lambda i, j: (i, 0, 0, 0))
    return pl.pallas_call(
        _attn_kernel,
        grid=(b, t // tq),
        in_specs=[pl.BlockSpec((1, H_A, tq, SLAB), lambda i, j: (i, 0, j, 0)),
                  head_blk(n_ctx, SLAB), head_blk(t, SLAB),
                  head_blk(DV_EXT, n_ctx), head_blk(DV_EXT, t),
                  pl.BlockSpec((1, tq, W_A), lambda i, j: (i, j, 0))],
        out_specs=pl.BlockSpec((1, tq, W_A), lambda i, j: (i, j, 0)),
        out_shape=jax.ShapeDtypeStruct((b, t, W_A), BF16),
        scratch_shapes=[pltpu.VMEM((2, n_ctx + t, tq), F32),
                        pltpu.VMEM((W_A, tq), F32)],
        compiler_params=pltpu.CompilerParams(dimension_semantics=("parallel", "parallel"),
                                             vmem_limit_bytes=VMEM_LIMIT),
        name="mla_attention",
    )(q, kc, kl, vtc, vtl, sza)


def _gate_fold_kernel(wq_ref, wk_ref, wv_ref, wg_ref, wgc_ref, wgm_ref):
    wgc_ref[...] = (_dot(wq_ref[...], wg_ref[0:W_B, :])
                    + _dot(wk_ref[...], wg_ref[W_B:2 * W_B, :])).astype(BF16)
    wgm_ref[...] = _dot(wv_ref[...], wg_ref[2 * W_B:3 * W_B, :]).astype(BF16)


def _gate_fold_call(wq, wk, wv, wg):
    shp = jax.ShapeDtypeStruct((W_B, LANES), BF16)
    return pl.pallas_call(_gate_fold_kernel, out_shape=[shp, shp],
                          compiler_params=pltpu.CompilerParams(vmem_limit_bytes=VMEM_LIMIT),
                          name="mlstm_gate_fold")(wq, wk, wv, wg)


_PAD = 8


def _lane_scans(rows_per_chunk, combine, fill, out):
    npc = CHUNK // LANES
    flat = [x[:, i * LANES:(i + 1) * LANES] for x in rows_per_chunk for i in range(npc)]
    lane = lax.broadcasted_iota(jnp.int32, flat[0].shape, 1)
    pre, suf, sh = list(flat), list(flat), 1
    while sh < LANES:
        pre = [combine(p, jnp.where(lane >= sh, pltpu.roll(p, sh, 1), fill)) for p in pre]
        suf = [combine(s, jnp.where(lane < LANES - sh, pltpu.roll(s, LANES - sh, 1), fill)) for s in suf]
        sh *= 2
        yield
    for c in range(len(rows_per_chunk)):
        p, s = pre[c * npc:(c + 1) * npc], suf[c * npc:(c + 1) * npc]
        tot = [x[:, LANES - 1:LANES] for x in p]
        run = None
        for i in range(npc):
            if run is not None:
                p[i] = combine(p[i], run)
            run = tot[i] if run is None else combine(run, tot[i])
        run = None
        for i in reversed(range(npc)):
            if run is not None:
                s[i] = combine(s[i], run)
            run = tot[i] if run is None else combine(run, tot[i])
        out.append((jnp.concatenate(p, axis=1), jnp.concatenate(s, axis=1)))


def _gate_tables(gts, dests):
    L = CHUNK
    fwd_rows = lax.broadcasted_iota(jnp.int32, (N_HD, L), 0) < H_B
    lis = [gt[0:N_HD, :] for gt in gts]
    sums = []
    yield from _lane_scans([_log_sigmoid(gt[N_HD:2 * N_HD, :]) for gt in gts], jnp.add, 0.0, sums)
    cums = [jnp.where(fwd_rows, ps, ss) for ps, ss in sums]
    rs = [li - cum for li, cum in zip(lis, cums)]
    maxs = []
    yield from _lane_scans(rs, jnp.maximum, -jnp.inf, maxs)
    for (grow_ref, c, gcol_ref, r0), cum, r, (pm, sm) in zip(dests, cums, rs, maxs):
        grow_ref[c] = jnp.concatenate([cum, r, jnp.where(fwd_rows, pm, sm)], axis=0)
        gcol_ref[r0:r0 + L, :] = jnp.concatenate([r, jnp.zeros((LANES - N_HD, L), F32)], axis=0).T


def _fill_pad(pad_s, xm_ref, n):
    zrow = jnp.zeros((_PAD, W_B), F32)
    pad_s[0:_PAD, :] = zrow
    pad_s[_PAD:_PAD + n, :] = xm_ref[...].astype(F32)
    pad_s[_PAD + n:2 * _PAD + n, :] = zrow


def _conv_gate_stage(c, pad_s, xm_ref, cw_ref, cb_ref, wgc_ref, wgm_ref, bg_ref, xc_s, xcb_s):
    L = CHUNK
    rows = slice(c * L, (c + 1) * L)
    r0 = c * L + _PAD
    row_id = lax.broadcasted_iota(jnp.int32, (L, W_B), 0)
    xcur = pad_s[r0:r0 + L, :]
    xprev = jnp.where(row_id == 0, pad_s[r0 - 1:r0, :], pltpu.roll(xcur, 1, 0))
    xnext = jnp.where(row_id == L - 1, pad_s[r0 + L:r0 + L + 1, :], pltpu.roll(xcur, L - 1, 0))
    pre = cb_ref[...] + xprev * cw_ref[0:1, :] + xcur * cw_ref[1:2, :] + xnext * cw_ref[2:3, :]
    xc = _silu(pre)
    if xc_s is not None:
        xc_s[rows, :] = xc
    xcb = xc.astype(BF16)
    xcb_s[rows, :] = xcb
    g = _dot(xcb, wgc_ref[...]) + _dot(xm_ref[rows, :], wgm_ref[...]) + bg_ref[...]
    return g.T[0:2 * N_HD, :]


def _headwise_stage(c, xm_ref, xcb_s, wq_ref, wk_ref, wv_ref, q_s, k_s, vt_s):
    L = CHUNK
    rows = slice(c * L, (c + 1) * L)
    for hf in range(W_B // MXU_TILE):
        cs = slice(hf * MXU_TILE, (hf + 1) * MXU_TILE)
        xcb = xcb_s[rows, cs]
        if q_s is not None:
            q_s[rows, cs] = _dot(xcb, wq_ref[cs, cs]).astype(BF16)
        k_s[rows, cs] = _dot(xcb, wk_ref[cs, cs]).astype(BF16)
        vt_s[c, cs, :] = _dot(xm_ref[rows, cs], wv_ref[cs, cs]).T.astype(BF16)


def _chain_scalars(g_f, g_b, m0):
    L = CHUNK
    fwd_rows = lax.broadcasted_iota(jnp.int32, (N_HD, L), 0) < H_B
    pick = lambda a: jnp.where(fwd_rows, g_f[a * N_HD:(a + 1) * N_HD, :], g_b[a * N_HD:(a + 1) * N_HD, :])
    cum, r, pm = pick(0), pick(1), pick(2)
    b_end = jnp.where(fwd_rows[:, 0:1], cum[:, L - 1:L], cum[:, 0:1])
    mu = jnp.maximum(m0, pm)
    a_int = jnp.exp(m0 - mu)
    em = jnp.exp(-(cum + mu))
    w = r + b_end
    m_new = jnp.maximum(b_end + m0, jnp.max(w, axis=1, keepdims=True))
    ws = jnp.exp(w - m_new)
    decay = jnp.exp(b_end + m0 - m_new)
    return mu, a_int, em, ws, decay, m_new


def _state_update(st_ref, hd, vt_ext, k_c, ws_row, decay_hd):
    vw = (vt_ext.astype(F32) * ws_row).astype(BF16)
    st_ref[hd] = decay_hd * st_ref[hd] + _dot(vw, k_c)


def _mlstm_kernel(xm_ref, xmc_ref, som_ref, szm_ref, cw_ref, cb_ref, wq_ref, wk_ref, wv_ref,
                  wgc_ref, wgm_ref, bg_ref, mhg_ref, skip_ref, o_ref,
                  pad_s, xc_s, xcb_s, q_s, k_s, vt_s, gcol_s, grow_s,
                  padc_s, xcbc_s, kc_s, vtc_s, gcolc_s, growc_s, st_s, hf_s, hb_s):
    L = CHUNK
    t = xm_ref.shape[1]
    nc = t // L
    row_i = lax.broadcasted_iota(jnp.int32, (L, L), 0)
    col_i = lax.broadcasted_iota(jnp.int32, (L, L), 1)
    tri = (row_i <= col_i, row_i >= col_i)
    ones_rows = (lax.broadcasted_iota(jnp.int32, (ST_ROWS - DH_B, L), 0) == 0).astype(BF16)
    hsl = lambda hh: slice(hh * DH_B, (hh + 1) * DH_B)
    xm, xmc = xm_ref.at[0], xmc_ref.at[0]
    gate_w = (cw_ref, cb_ref, wgc_ref, wgm_ref, bg_ref)

    _fill_pad(padc_s, xmc, CTX_LEN)
    _fill_pad(pad_s, xm, t)
    gts = [_conv_gate_stage(0, padc_s, xmc, *gate_w, None, xcbc_s)]
    gts += [_conv_gate_stage(c, pad_s, xm, *gate_w, xc_s, xcb_s) for c in range(nc)]
    dests = [(growc_s, 0, gcolc_s, 0)] + [(grow_s, c, gcol_s, c * L) for c in range(nc)]
    tables = _gate_tables(gts, dests)
    _headwise_stage(0, xmc, xcbc_s, wq_ref, wk_ref, wv_ref, None, kc_s, vtc_s)
    for c in range(nc):
        next(tables, None)
        next(tables, None)
        _headwise_stage(c, xm, xcb_s, wq_ref, wk_ref, wv_ref, q_s, k_s, vt_s)
    for _ in tables:
        pass

    st_s[...] = jnp.zeros(st_s.shape, F32)
    gc = growc_s[0]
    _, _, _, ws, decay, m_ctx = _chain_scalars(gc, gc, jnp.zeros((N_HD, 1), F32))
    for hd in range(N_HD):
        hh = hd % H_B
        vt_ext = jnp.concatenate([vtc_s[0, hsl(hh), :], ones_rows], axis=0)
        _state_update(st_s, hd, vt_ext, kc_s[:, hsl(hh)], ws[hd:hd + 1, :], decay[hd:hd + 1, :])

    def body(i, m0):
        ci = (i, nc - 1 - i)
        rows = tuple(pl.ds(pl.multiple_of(c * L, L), L) for c in ci)
        mu, a_int, em, ws, decay, m_new = _chain_scalars(grow_s[ci[0]], grow_s[ci[1]], m0)
        gcols = tuple(gcol_s[r, :] for r in rows)
        live = {}

        def stage_scores(hd):
            d, hh = hd // H_B, hd % H_B
            k_c = k_s[rows[d], hsl(hh)]
            vt_ext = jnp.concatenate([vt_s[ci[d], hsl(hh), :], ones_rows], axis=0)
            lhs = jnp.concatenate([k_c, st_s[hd].astype(BF16)], axis=0)
            live[hd] = (k_c, vt_ext, _dot_nt(lhs, q_s[rows[d], hsl(hh)]))

        def stage_gate(hd):
            d = hd // H_B
            k_c, vt_ext, res = live[hd]
            rcol = gcols[d][:, hd:hd + 1]
            e = jnp.exp(jnp.where(tri[d], rcol - mu[hd:hd + 1, :], -jnp.inf))
            p = (res[0:L, :] * e).astype(BF16)
            _state_update(st_s, hd, vt_ext, k_c, ws[hd:hd + 1, :], decay[hd:hd + 1, :])
            live[hd] = (vt_ext, res[L:L + ST_ROWS, :], p)

        def stage_out(hd):
            d, hh = hd // H_B, hd % H_B
            vt_ext, inter, p = live.pop(hd)
            tot = a_int[hd:hd + 1, :] * inter + _dot(vt_ext, p)
            den = tot[DH_B:DH_B + 1, :]
            h_t = tot[0:DH_B, :] / jnp.maximum(jnp.abs(den), em[hd:hd + 1, :])
            (hf_s if d == 0 else hb_s)[ci[d], hsl(hh), :] = h_t

        stages = (stage_scores, stage_gate, stage_out)
        for step in range(N_HD + len(stages) - 1):
            for si in reversed(range(len(stages))):
                if 0 <= step - si < N_HD:
                    stages[si](step - si)
        return m_new

    lax.fori_loop(0, nc, body, m_ctx)

    def epilogue(c, carry):
        rows = pl.ds(pl.multiple_of(c * L, L), L)
        hsum = (hf_s[c] + hb_s[c]).T * som_ref[0, rows, :].astype(F32)
        parts = [_layer_norm(hsum[:, hsl(hh)]) for hh in range(H_B)]
        hb = jnp.concatenate(parts, axis=1) * mhg_ref[...] + skip_ref[...] * xc_s[rows, :]
        o_ref[0, rows, :] = (hb * szm_ref[0, rows, :].astype(F32)).astype(BF16)
        return carry

    lax.fori_loop(0, nc, epilogue, 0)


def _mlstm_call(xm, xmc, som, szm, conv_w, conv_b, wq, wk, wv, wgc, wgm, bg, mhg, skip):
    b, t, _ = xm.shape
    nc = t // CHUNK
    seq = lambda n: pl.BlockSpec((1, n, W_B), lambda i: (i, 0, 0))
    const = lambda shape: pl.BlockSpec(shape, lambda i: (0,) * len(shape))
    scratch = [
        pltpu.VMEM((t + 2 * _PAD, W_B), F32),
        pltpu.VMEM((t, W_B), F32),
        pltpu.VMEM((t, W_B), BF16),
        pltpu.VMEM((t, W_B), BF16),
        pltpu.VMEM((t, W_B), BF16),
        pltpu.VMEM((nc, W_B, CHUNK), BF16),
        pltpu.VMEM((t, LANES), F32),
        pltpu.VMEM((nc, 3 * N_HD, CHUNK), F32),
        pltpu.VMEM((CTX_LEN + 2 * _PAD, W_B), F32),
        pltpu.VMEM((CTX_LEN, W_B), BF16),
        pltpu.VMEM((CTX_LEN, W_B), BF16),
        pltpu.VMEM((1, W_B, CHUNK), BF16),
        pltpu.VMEM((CTX_LEN, LANES), F32),
        pltpu.VMEM((1, 3 * N_HD, CHUNK), F32),
        pltpu.VMEM((N_HD, ST_ROWS, DH_B), F32),
        pltpu.VMEM((nc, W_B, CHUNK), F32),
        pltpu.VMEM((nc, W_B, CHUNK), F32),
    ]
    return pl.pallas_call(
        _mlstm_kernel,
        grid=(b,),
        in_specs=[seq(t), seq(CTX_LEN), seq(t), seq(t), const((3, W_B)), const((1, W_B)),
                  const((W_B, W_B)), const((W_B, W_B)), const((W_B, W_B)),
                  const((W_B, LANES)), const((W_B, LANES)), const((1, LANES)),
                  const((1, W_B)), const((1, W_B))],
        out_specs=seq(t),
        out_shape=jax.ShapeDtypeStruct((b, t, W_B), BF16),
        scratch_shapes=scratch,
        compiler_params=pltpu.CompilerParams(dimension_semantics=("arbitrary",),
                                             vmem_limit_bytes=VMEM_LIMIT),
        name="mlstm_bidir",
    )(xm, xmc, som, szm, conv_w, conv_b, wq, wk, wv, wgc, wgm, bg, mhg, skip)


def _out_kernel(x_ref, mod_ref, lng_ref, lnb_ref, ya_ref, yb_ref, wo_ref, g_ref, b_ref, o_ref, *, n_sub):
    sub = x_ref.shape[1] // n_sub
    gate = mod_ref[0][:, 2 * D_MODEL:3 * D_MODEL]
    lng_a, lnb_a = ALPHA * lng_ref[...], ALPHA * lnb_ref[...]

    def mix_stage(s):
        rows = slice(s * sub, (s + 1) * sub)
        return (_dot(ya_ref[0, rows, :], wo_ref[0:W_A, :])
                + _dot(yb_ref[0, rows, :], wo_ref[W_A:W_A + W_B, :]))

    def norm_stage(s, y):
        rows = slice(s * sub, (s + 1) * sub)
        h_a = _layer_norm(x_ref[0, rows, :]) * lng_a + lnb_a
        o_ref[0, rows, :] = _layer_norm(h_a + gate * y) * g_ref[...] + b_ref[...]

    y = mix_stage(0)
    for s in range(n_sub):
        y_next = mix_stage(s + 1) if s + 1 < n_sub else None
        norm_stage(s, y)
        y = y_next


def _out_call(x, mod3, lng, lnb, ya, yb, wo, g, bb, *, tm, n_sub):
    b, t, _ = x.shape
    const = lambda shape: pl.BlockSpec(shape, lambda i, j: (0,) * len(shape))
    row = lambda w: pl.BlockSpec((1, tm, w), lambda i, j: (i, j, 0))
    return pl.pallas_call(
        functools.partial(_out_kernel, n_sub=n_sub),
        grid=(b, t // tm),
        in_specs=[row(D_MODEL), pl.BlockSpec((1, 1, 3 * D_MODEL), lambda i, j: (i, 0, 0)),
                  const((1, D_MODEL)), const((1, D_MODEL)), row(W_A), row(W_B),
                  const((W_A + W_B, D_MODEL)), const((1, D_MODEL)), const((1, D_MODEL))],
        out_specs=row(D_MODEL),
        out_shape=jax.ShapeDtypeStruct((b, t, D_MODEL), F32),
        compiler_params=pltpu.CompilerParams(dimension_semantics=("parallel", "parallel"),
                                             vmem_limit_bytes=VMEM_LIMIT),
        name="out_proj_ln",
    )(x, mod3, lng, lnb, ya, yb, wo, g, bb)


def _rot_partner(w):
    w4 = w.reshape(w.shape[:-1] + (2, 2, ROPE_FREQS))
    return jnp.stack([-w4[..., 1, :], w4[..., 0, :]], axis=-2).reshape(w.shape)


def _rope_tables(seq, scale_keep, scale_rope):
    n_rows = seq // GRID_W
    rowp = np.repeat(np.arange(n_rows, dtype=np.float32), GRID_W)
    colp = np.tile(np.arange(GRID_W, dtype=np.float32), n_rows)
    inv = (np.float32(ROPE_BASE) ** (-np.arange(ROPE_FREQS, dtype=np.float32) / np.float32(ROPE_FREQS)))
    ang = np.stack([rowp[:, None] * inv, colp[:, None] * inv], axis=1).astype(np.float32)
    cos = np.broadcast_to(np.cos(ang)[:, :, None, :], (seq, 2, 2, ROPE_FREQS)).reshape(seq, DR_A)
    sin = np.broadcast_to(np.sin(ang)[:, :, None, :], (seq, 2, 2, ROPE_FREQS)).reshape(seq, DR_A)
    z32 = np.zeros((seq, DR_A), np.float32)
    t1 = np.concatenate([np.full((seq, DN_A), scale_keep, np.float32), z32, cos * scale_rope], axis=1)
    t2 = np.concatenate([np.zeros((seq, DN_A), np.float32), z32, sin * scale_rope], axis=1)
    return jnp.asarray(t1, F32), jnp.asarray(t2, F32)


def _block_diag(w):
    nb, bs, _ = w.shape
    n = nb * bs
    rows = w.reshape(n, bs)
    ri = lax.broadcasted_iota(jnp.int32, (n, n), 0)
    ci = lax.broadcasted_iota(jnp.int32, (n, n), 1)
    out = jnp.zeros((n, n), w.dtype)
    for o in range(bs):
        out = out + jnp.where((ri // bs == ci // bs) & (ci % bs == o), rows[:, o:o + 1], 0.0)
    return out


def kernel(x, c, ctx, c_ctx, ln_in_g, ln_in_b, w_ada, b_ada, w_in, g_qa, w_qb, g_kva, w_kvb, conv_w, conv_b, w_mq, w_mk, w_mv, w_gate, b_gate, mh_g, skip, w_out, ln_g, ln_b):
    b, t, _ = x.shape
    l = 0
    r2 = lambda v: v.reshape(1, -1)

    wi = w_in[l]
    s_qa, s_kva, s_kr, s_za, s_xm, s_om = 256, 384, 416, 928, 1440, 1952
    w_kr = wi[:, s_kva:s_kr]
    krblk = jnp.concatenate([jnp.zeros((D_MODEL, DN_A), F32), _rot_partner(w_kr), w_kr], axis=1)
    wina = jnp.concatenate([wi[:, 0:s_kva], krblk], axis=1).astype(BF16)
    winr = wi[:, s_kr:].astype(BF16)
    wq3 = w_qb[l].reshape(Q_LORA, H_A, DN_A + DR_A)
    wq_r = wq3[..., DN_A:]
    wq = jnp.concatenate([wq3[..., :DN_A], _rot_partner(wq_r), wq_r], axis=-1)
    wq = wq.reshape(Q_LORA, H_A * SLAB).astype(BF16)
    wkv3 = w_kvb[l].reshape(KV_LORA, H_A, DN_A + DV_A)
    wkn = jnp.concatenate([wkv3[..., :DN_A], jnp.zeros((KV_LORA, H_A, SLAB - DN_A), F32)], axis=-1)
    wkn = wkn.reshape(KV_LORA, H_A * SLAB).astype(BF16)
    wvt = wkv3[..., DN_A:].reshape(KV_LORA, W_A).T.astype(BF16)
    wmq = _block_diag(w_mq[l])
    wmk = _block_diag(w_mk[l])
    wmv = _block_diag(w_mv[l])
    perm = jnp.array([0, 1, 2, 3, 8, 9, 10, 11, 4, 5, 6, 7, 12, 13, 14, 15])
    wg = jnp.pad(w_gate[l][:, perm], ((0, 0), (0, LANES - 2 * N_HD))).astype(BF16)
    bg = jnp.pad(b_gate[l][perm], (0, LANES - 2 * N_HD)).reshape(1, LANES)
    wgc, wgm = _gate_fold_call(wmq.astype(BF16), wmk.astype(BF16), wmv.astype(BF16), wg)
    wmk_s = (wmk * (DH_B ** -0.5)).astype(BF16)
    wo = w_out[l].astype(BF16)

    sm_scale = (DN_A + DR_A) ** -0.5 * LOG2_E
    t1q, t2q = _rope_tables(t, sm_scale, sm_scale)
    t1k, t2k = _rope_tables(t, 0.0, 1.0)
    n_ctx = ctx.shape[1]
    t1kc = jnp.asarray(np.concatenate([np.zeros((n_ctx, DN_A + DR_A), np.float32),
                                       np.ones((n_ctx, DR_A), np.float32)], axis=1))
    t2kc = jnp.asarray(np.zeros((n_ctx, SLAB), np.float32))

    cc = jnp.concatenate([c, c_ctx[None, :], jnp.zeros((16 - b - 1, D_MODEL), F32)], axis=0)
    mod = _ada_call(cc, w_ada[l], r2(b_ada[l]))
    mod3 = mod.reshape(16, 1, 3 * D_MODEL)

    lng, lnb = r2(ln_in_g), r2(ln_in_b)
    shared = (wina, winr, r2(g_qa[l]), wq, r2(g_kva[l]), wkn, wvt)
    q, kl, vtl, xm, sza, som, szm = _proj_call(x, mod3, None, lng, lnb, *shared, t1q, t2q, t1k, t2k,
                                               tm=1024, n_sub=2, latent=True)
    kc, vtc, xmc = _proj_call(ctx, mod3, b, lng, lnb, *shared, t1kc, t2kc, t1kc, t2kc,
                              tm=n_ctx, n_sub=1, latent=False)

    ya = _attn_call(q, kc, kl, vtc, vtl, sza, tq=512)
    yb = _mlstm_call(xm, xmc, som, szm, conv_w[l], r2(conv_b[l]), wmq.astype(BF16), wmk_s,
                     wmv.astype(BF16), wgc, wgm, bg, r2(mh_g[l]), r2(skip[l]))
    return _out_call(x, mod3, lng, lnb, ya, yb, wo, r2(ln_g[l]), r2(ln_b[l]), tm=1024, n_sub=4)
```

```python
import functools

import numpy as np
import jax
import jax.numpy as jnp
from jax import lax
from jax.experimental import pallas as pl
from jax.experimental.pallas import tpu as pltpu

F32 = jnp.float32
BF16 = jnp.bfloat16

D_MODEL = 1024
CTX_LEN = 256
GRID_W = 64
H_A, DN_A, DR_A, DV_A = 8, 64, 32, 64
W_A = H_A * DV_A
Q_LORA, KV_LORA = 256, 128
ROPE_FREQS = DR_A // 4
ROPE_BASE = 10000.0
H_B, DH_B = 4, 128
W_B = H_B * DH_B
QKV_BS = 4
DEPTH = 1
ALPHA = (2.0 * DEPTH) ** 0.25
LN_EPS = 1e-5
RMS_EPS = 1e-6
LOG2_E = 1.4426950408889634

LANES = 128
MXU_TILE = 256
SLAB = LANES
DV_EXT = DV_A + 16
ATTN_KEY_CHUNK = MXU_TILE
CHUNK = 256
N_HD = 2 * H_B
ST_ROWS = DH_B + 16
VMEM_LIMIT = 56 * 1024 * 1024

_NT = (((1,), (1,)), ((), ()))


def _dot(a, b):
    return jnp.dot(a, b, preferred_element_type=F32)


def _dot_nt(a, b):
    return lax.dot_general(a, b, _NT, preferred_element_type=F32)


def _layer_norm(x):
    mu = jnp.mean(x, axis=-1, keepdims=True)
    xc = x - mu
    var = jnp.mean(xc * xc, axis=-1, keepdims=True)
    return xc * lax.rsqrt(var + LN_EPS)


def _rms_norm(x, g):
    return (x * lax.rsqrt(jnp.mean(x * x, axis=-1, keepdims=True) + RMS_EPS)) * g


def _silu(x):
    return x * jax.nn.sigmoid(x)


def _log_sigmoid(x):
    return jnp.minimum(x, 0.0) - jnp.log1p(jnp.exp(-jnp.abs(x)))


def _ada_kernel(cc_ref, w_ref, b_ref, o_ref):
    a = _silu(cc_ref[...]).astype(BF16)
    o_ref[...] = _dot(a, w_ref[...].astype(BF16)) + b_ref[...]


def _ada_call(cc, w_ada, b_ada):
    n = w_ada.shape[1]
    tn = 1024
    return pl.pallas_call(
        _ada_kernel,
        grid=(n // tn,),
        in_specs=[pl.BlockSpec((16, D_MODEL), lambda j: (0, 0)),
                  pl.BlockSpec((D_MODEL, tn), lambda j: (0, j)),
                  pl.BlockSpec((1, tn), lambda j: (0, j))],
        out_specs=pl.BlockSpec((16, tn), lambda j: (0, j)),
        out_shape=jax.ShapeDtypeStruct((16, n), F32),
        compiler_params=pltpu.CompilerParams(dimension_semantics=("arbitrary",),
                                             vmem_limit_bytes=VMEM_LIMIT),
        name="ada_mod",
    )(cc, w_ada, b_ada)


def _rope_slab(s, t1, t2):
    return s * t1 + pltpu.roll(s, 32, 1) * t2


def _proj_kernel(x_ref, mod_ref, lng_ref, lnb_ref, wina_ref, winr_ref, gqa_ref, wq_ref, gkva_ref,
                 wkn_ref, wvt_ref, t1q_ref, t2q_ref, t1k_ref, t2k_ref, *out_refs, latent, n_sub):
    if latent:
        q_ref, k_ref, vt_ref, xm_ref, sza_ref, som_ref, szm_ref, h_ref = out_refs
    else:
        k_ref, vt_ref, xm_ref = out_refs
    sub = x_ref.shape[1] // n_sub
    mod = mod_ref[0]
    shift, scale1 = mod[:, 0:D_MODEL], 1.0 + mod[:, D_MODEL:2 * D_MODEL]
    ones_rows = (lax.broadcasted_iota(jnp.int32, (DV_EXT - DV_A, sub), 0) == 0).astype(BF16)

    def norm_stage(s):
        rows = slice(s * sub, (s + 1) * sub)
        h = _layer_norm(x_ref[0, rows, :]) * lng_ref[...] + lnb_ref[...]
        if latent:
            h_ref[0, rows, :] = h
        return (h * scale1 + shift).astype(BF16)

    def proj_stage(s, u):
        rows = slice(s * sub, (s + 1) * sub)
        p0 = _dot(u, wina_ref[...])
        kvn = _rms_norm(p0[:, 256:384], gkva_ref[...]).astype(BF16)
        kn = _dot(kvn, wkn_ref[...])
        krr = _rope_slab(p0[:, 384:512], t1k_ref[rows, :], t2k_ref[rows, :])
        for hh in range(H_A):
            k_ref[0, hh, rows, :] = (kn[:, hh * SLAB:(hh + 1) * SLAB] + krr).astype(BF16)
        vt = _dot_nt(wvt_ref[...], kvn).astype(BF16)
        for hh in range(H_A):
            vt_ref[0, hh, 0:DV_A, rows] = vt[hh * DV_A:(hh + 1) * DV_A, :]
            vt_ref[0, hh, DV_A:DV_EXT, rows] = ones_rows
        xm_ref[0, rows, :] = _dot(u, winr_ref[:, 512:1024]).astype(BF16)
        if latent:
            qn = _rms_norm(p0[:, 0:256], gqa_ref[...]).astype(BF16)
            qs = _dot(qn, wq_ref[...])
            t1q, t2q = t1q_ref[rows, :], t2q_ref[rows, :]
            for hh in range(H_A):
                q_ref[0, hh, rows, :] = _rope_slab(qs[:, hh * SLAB:(hh + 1) * SLAB], t1q, t2q).astype(BF16)
            sza_ref[0, rows, :] = _silu(_dot(u, winr_ref[:, 0:512])).astype(BF16)
            som_ref[0, rows, :] = jax.nn.sigmoid(_dot(u, winr_ref[:, 1024:1536])).astype(BF16)
            szm_ref[0, rows, :] = _silu(_dot(u, winr_ref[:, 1536:2048])).astype(BF16)

    u = norm_stage(0)
    for s in range(n_sub):
        u_next = norm_stage(s + 1) if s + 1 < n_sub else None
        proj_stage(s, u)
        u = u_next


def _proj_call(x, mod3, mod_row, lng, lnb, wina, winr, gqa, wq, gkva, wkn, wvt, t1q, t2q, t1k, t2k, *,
               tm, n_sub, latent):
    b, t, _ = x.shape
    const = lambda shape: pl.BlockSpec(shape, lambda i, j: (0,) * len(shape))
    tab = pl.BlockSpec((tm, SLAB), lambda i, j: (j, 0))
    row = lambda w: pl.BlockSpec((1, tm, w), lambda i, j: (i, j, 0))
    if mod_row is None:
        mod_spec = pl.BlockSpec((1, 1, 3 * D_MODEL), lambda i, j: (i, 0, 0))
    else:
        mod_spec = pl.BlockSpec((1, 1, 3 * D_MODEL), lambda i, j: (mod_row, 0, 0))
    in_specs = [row(D_MODEL), mod_spec, const((1, D_MODEL)), const((1, D_MODEL)),
                const(wina.shape), const(winr.shape), const((1, Q_LORA)), const((Q_LORA, H_A * SLAB)),
                const((1, KV_LORA)), const((KV_LORA, H_A * SLAB)), const((W_A, KV_LORA)),
                tab, tab, tab, tab]
    k_spec = pl.BlockSpec((1, H_A, tm, SLAB), lambda i, j: (i, 0, j, 0))
    vt_spec = pl.BlockSpec((1, H_A, DV_EXT, tm), lambda i, j: (i, 0, 0, j))
    k_shape = jax.ShapeDtypeStruct((b, H_A, t, SLAB), BF16)
    vt_shape = jax.ShapeDtypeStruct((b, H_A, DV_EXT, t), BF16)
    half = jax.ShapeDtypeStruct((b, t, W_B), BF16)
    if latent:
        out_specs = [k_spec, k_spec, vt_spec, row(W_B), row(W_A), row(W_B), row(W_B), row(D_MODEL)]
        out_shape = [k_shape, k_shape, vt_shape, half, half, half, half,
                     jax.ShapeDtypeStruct((b, t, D_MODEL), F32)]
    else:
        out_specs = [k_spec, vt_spec, row(W_B)]
        out_shape = [k_shape, vt_shape, half]
    return pl.pallas_call(
        functools.partial(_proj_kernel, latent=latent, n_sub=n_sub),
        grid=(b, t // tm),
        in_specs=in_specs, out_specs=out_specs, out_shape=out_shape,
        compiler_params=pltpu.CompilerParams(dimension_semantics=("parallel", "parallel"),
                                             vmem_limit_bytes=VMEM_LIMIT),
        name="in_proj_latent" if latent else "in_proj_ctx",
    )(x, mod3, lng, lnb, wina, winr, gqa, wq, gkva, wkn, wvt, t1q, t2q, t1k, t2k)


def _attn_kernel(q_ref, kc_ref, kl_ref, vtc_ref, vtl_ref, sza_ref, o_ref, s_buf, ot_s):
    n_ctx, t = kc_ref.shape[2], kl_ref.shape[2]
    kc = ATTN_KEY_CHUNK

    def scores(h, slot):
        qh = q_ref[0, h]
        sc = _dot_nt(kc_ref[0, h], qh)
        sk = _dot_nt(kl_ref[0, h], qh)
        s_buf[slot, 0:n_ctx, :] = sc
        s_buf[slot, n_ctx:n_ctx + t, :] = sk
        return jnp.maximum(jnp.max(sc, axis=0, keepdims=True), jnp.max(sk, axis=0, keepdims=True))

    def values(h, slot, m):
        chunks = [(vtc_ref, c * kc, c * kc) for c in range(n_ctx // kc)]
        chunks += [(vtl_ref, c * kc, n_ctx + c * kc) for c in range(t // kc)]
        acc = None
        for vref, v0, s0 in chunks:
            p = jnp.exp2(s_buf[slot, s0:s0 + kc, :] - m).astype(BF16)
            d = _dot(vref[0, h, :, v0:v0 + kc], p)
            acc = d if acc is None else acc + d
        rows = pl.ds(pl.multiple_of(h * DV_A, DV_A), DV_A)
        ot_s[rows, :] = acc[0:DV_A, :] / acc[DV_A:DV_A + 1, :]

    def body(i, m_even):
        h = 2 * i
        m_odd = scores(h + 1, 1)
        values(h, 0, m_even)
        m_even = scores(h + 2, 0)
        values(h + 1, 1, m_odd)
        return m_even

    m_even = lax.fori_loop(0, H_A // 2 - 1, body, scores(0, 0))
    m_odd = scores(H_A - 1, 1)
    values(H_A - 2, 0, m_even)
    values(H_A - 1, 1, m_odd)
    o_ref[0] = (ot_s[...].T * sza_ref[0].astype(F32)).astype(BF16)


def _attn_call(q, kc, kl, vtc, vtl, sza, *, tq):
    b, _, t, _ = q.shape
    n_ctx = kc.shape[2]
    head_blk = lambda n, w: pl.BlockSpec((1, H_A, n, w), lambda i, j: (i, 0, 0, 0))
    return pl.pallas_call(
        _attn_kernel,
        grid=(b, t // tq),
        in_specs=[pl.BlockSpec((1, H_A, tq, SLAB), lambda i, j: (i, 0, j, 0)),
                  head_blk(n_ctx, SLAB), head_blk(t, SLAB),
                  head_blk(DV_EXT, n_ctx), head_blk(DV_EXT, t),
                  pl.BlockSpec((1, tq, W_A), lambda i, j: (i, j, 0))],
        out_specs=pl.BlockSpec((1, tq, W_A), lambda i, j: (i, j, 0)),
        out_shape=jax.ShapeDtypeStruct((b, t, W_A), BF16),
        scratch_shapes=[pltpu.VMEM((2, n_ctx + t, tq), F32),
                        pltpu.VMEM((W_A, tq), F32)],
        compiler_params=pltpu.CompilerParams(dimension_semantics=("parallel", "parallel"),
                                             vmem_limit_bytes=VMEM_LIMIT),
        name="mla_attention",
    )(q, kc, kl, vtc, vtl, sza)


def _gate_fold_kernel(wq_ref, wk_ref, wv_ref, wg_ref, wgc_ref, wgm_ref):
    wgc_ref[...] = (_dot(wq_ref[...], wg_ref[0:W_B, :])
                    + _dot(wk_ref[...], wg_ref[W_B:2 * W_B, :])).astype(BF16)
    wgm_ref[...] = _dot(wv_ref[...], wg_ref[2 * W_B:3 * W_B, :]).astype(BF16)


def _gate_fold_call(wq, wk, wv, wg):
    shp = jax.ShapeDtypeStruct((W_B, LANES), BF16)
    return pl.pallas_call(_gate_fold_kernel, out_shape=[shp, shp],
                          compiler_params=pltpu.CompilerParams(vmem_limit_bytes=VMEM_LIMIT),
                          name="mlstm_gate_fold")(wq, wk, wv, wg)


_PAD = 8


def _lane_scans(rows_per_chunk, combine, fill, out):
    npc = CHUNK // LANES
    flat = [x[:, i * LANES:(i + 1) * LANES] for x in rows_per_chunk for i in range(npc)]
    lane = lax.broadcasted_iota(jnp.int32, flat[0].shape, 1)
    pre, suf, sh = list(flat), list(flat), 1
    while sh < LANES:
        pre = [combine(p, jnp.where(lane >= sh, pltpu.roll(p, sh, 1), fill)) for p in pre]
        suf = [combine(s, jnp.where(lane < LANES - sh, pltpu.roll(s, LANES - sh, 1), fill)) for s in suf]
        sh *= 2
        yield
    for c in range(len(rows_per_chunk)):
        p, s = pre[c * npc:(c + 1) * npc], suf[c * npc:(c + 1) * npc]
        tot = [x[:, LANES - 1:LANES] for x in p]
        run = None
        for i in range(npc):
            if run is not None:
                p[i] = combine(p[i], run)
            run = tot[i] if run is None else combine(run, tot[i])
        run = None
        for i in reversed(range(npc)):
            if run is not None:
                s[i] = combine(s[i], run)
            run = tot[i] if run is None else combine(run, tot[i])
        out.append((jnp.concatenate(p, axis=1), jnp.concatenate(s, axis=1)))


def _gate_tables(gts, dests):
    L = CHUNK
    fwd_rows = lax.broadcasted_iota(jnp.int32, (N_HD, L), 0) < H_B
    lis = [gt[0:N_HD, :] for gt in gts]
    sums = []
    yield from _lane_scans([_log_sigmoid(gt[N_HD:2 * N_HD, :]) for gt in gts], jnp.add, 0.0, sums)
    cums = [jnp.where(fwd_rows, ps, ss) for ps, ss in sums]
    rs = [li - cum for li, cum in zip(lis, cums)]
    maxs = []
    yield from _lane_scans(rs, jnp.maximum, -jnp.inf, maxs)
    for (grow_ref, c, gcol_ref, r0), cum, r, (pm, sm) in zip(dests, cums, rs, maxs):
        grow_ref[c] = jnp.concatenate([cum, r, jnp.where(fwd_rows, pm, sm)], axis=0)
        gcol_ref[r0:r0 + L, :] = jnp.concatenate([r, jnp.zeros((LANES - N_HD, L), F32)], axis=0).T


def _fill_pad(pad_s, xm_ref, n):
    zrow = jnp.zeros((_PAD, W_B), F32)
    pad_s[0:_PAD, :] = zrow
    pad_s[_PAD:_PAD + n, :] = xm_ref[...].astype(F32)
    pad_s[_PAD + n:2 * _PAD + n, :] = zrow


def _conv_gate_stage(c, pad_s, xm_ref, cw_ref, cb_ref, wgc_ref, wgm_ref, bg_ref, xc_s, xcb_s):
    L = CHUNK
    rows = slice(c * L, (c + 1) * L)
    r0 = c * L + _PAD
    row_id = lax.broadcasted_iota(jnp.int32, (L, W_B), 0)
    xcur = pad_s[r0:r0 + L, :]
    xprev = jnp.where(row_id == 0, pad_s[r0 - 1:r0, :], pltpu.roll(xcur, 1, 0))
    xnext = jnp.where(row_id == L - 1, pad_s[r0 + L:r0 + L + 1, :], pltpu.roll(xcur, L - 1, 0))
    pre = cb_ref[...] + xprev * cw_ref[0:1, :] + xcur * cw_ref[1:2, :] + xnext * cw_ref[2:3, :]
    xc = _silu(pre)
    if xc_s is not None:
        xc_s[rows, :] = xc
    xcb = xc.astype(BF16)
    xcb_s[rows, :] = xcb
    g = _dot(xcb, wgc_ref[...]) + _dot(xm_ref[rows, :], wgm_ref[...]) + bg_ref[...]
    return g.T[0:2 * N_HD, :]


def _headwise_stage(c, xm_ref, xcb_s, wq_ref, wk_ref, wv_ref, q_s, k_s, vt_s):
    L = CHUNK
    rows = slice(c * L, (c + 1) * L)
    for hf in range(W_B // MXU_TILE):
        cs = slice(hf * MXU_TILE, (hf + 1) * MXU_TILE)
        xcb = xcb_s[rows, cs]
        if q_s is not None:
            q_s[rows, cs] = _dot(xcb, wq_ref[cs, cs]).astype(BF16)
        k_s[rows, cs] = _dot(xcb, wk_ref[cs, cs]).astype(BF16)
        vt_s[c, cs, :] = _dot(xm_ref[rows, cs], wv_ref[cs, cs]).T.astype(BF16)


def _chain_scalars(g_f, g_b, m0):
    L = CHUNK
    fwd_rows = lax.broadcasted_iota(jnp.int32, (N_HD, L), 0) < H_B
    pick = lambda a: jnp.where(fwd_rows, g_f[a * N_HD:(a + 1) * N_HD, :], g_b[a * N_HD:(a + 1) * N_HD, :])
    cum, r, pm = pick(0), pick(1), pick(2)
    b_end = jnp.where(fwd_rows[:, 0:1], cum[:, L - 1:L], cum[:, 0:1])
    mu = jnp.maximum(m0, pm)
    a_int = jnp.exp(m0 - mu)
    em = jnp.exp(-(cum + mu))
    w = r + b_end
    m_new = jnp.maximum(b_end + m0, jnp.max(w, axis=1, keepdims=True))
    ws = jnp.exp(w - m_new)
    decay = jnp.exp(b_end + m0 - m_new)
    return mu, a_int, em, ws, decay, m_new


def _state_update(st_ref, hd, vt_ext, k_c, ws_row, decay_hd):
    vw = (vt_ext.astype(F32) * ws_row).astype(BF16)
    st_ref[hd] = decay_hd * st_ref[hd] + _dot(vw, k_c)


def _mlstm_kernel(xm_ref, xmc_ref, som_ref, szm_ref, cw_ref, cb_ref, wq_ref, wk_ref, wv_ref,
                  wgc_ref, wgm_ref, bg_ref, mhg_ref, skip_ref, o_ref,
                  pad_s, xc_s, xcb_s, q_s, k_s, vt_s, gcol_s, grow_s,
                  padc_s, xcbc_s, kc_s, vtc_s, gcolc_s, growc_s, st_s, hf_s, hb_s):
    L = CHUNK
    t = xm_ref.shape[1]
    nc = t // L
    row_i = lax.broadcasted_iota(jnp.int32, (L, L), 0)
    col_i = lax.broadcasted_iota(jnp.int32, (L, L), 1)
    tri = (row_i <= col_i, row_i >= col_i)
    ones_rows = (lax.broadcasted_iota(jnp.int32, (ST_ROWS - DH_B, L), 0) == 0).astype(BF16)
    hsl = lambda hh: slice(hh * DH_B, (hh + 1) * DH_B)
    xm, xmc = xm_ref.at[0], xmc_ref.at[0]
    gate_w = (cw_ref, cb_ref, wgc_ref, wgm_ref, bg_ref)

    _fill_pad(padc_s, xmc, CTX_LEN)
    _fill_pad(pad_s, xm, t)
    gts = [_conv_gate_stage(0, padc_s, xmc, *gate_w, None, xcbc_s)]
    gts += [_conv_gate_stage(c, pad_s, xm, *gate_w, xc_s, xcb_s) for c in range(nc)]
    dests = [(growc_s, 0, gcolc_s, 0)] + [(grow_s, c, gcol_s, c * L) for c in range(nc)]
    tables = _gate_tables(gts, dests)
    _headwise_stage(0, xmc, xcbc_s, wq_ref, wk_ref, wv_ref, None, kc_s, vtc_s)
    for c in range(nc):
        next(tables, None)
        next(tables, None)
        _headwise_stage(c, xm, xcb_s, wq_ref, wk_ref, wv_ref, q_s, k_s, vt_s)
    for _ in tables:
        pass

    st_s[...] = jnp.zeros(st_s.shape, F32)
    gc = growc_s[0]
    _, _, _, ws, decay, m_ctx = _chain_scalars(gc, gc, jnp.zeros((N_HD, 1), F32))
    for hd in range(N_HD):
        hh = hd % H_B
        vt_ext = jnp.concatenate([vtc_s[0, hsl(hh), :], ones_rows], axis=0)
        _state_update(st_s, hd, vt_ext, kc_s[:, hsl(hh)], ws[hd:hd + 1, :], decay[hd:hd + 1, :])

    def body(i, m0):
        ci = (i, nc - 1 - i)
        rows = tuple(pl.ds(pl.multiple_of(c * L, L), L) for c in ci)
        mu, a_int, em, ws, decay, m_new = _chain_scalars(grow_s[ci[0]], grow_s[ci[1]], m0)
        gcols = tuple(gcol_s[r, :] for r in rows)
        live = {}

        def stage_scores(hd):
            d, hh = hd // H_B, hd % H_B
            k_c = k_s[rows[d], hsl(hh)]
            vt_ext = jnp.concatenate([vt_s[ci[d], hsl(hh), :], ones_rows], axis=0)
            lhs = jnp.concatenate([k_c, st_s[hd].astype(BF16)], axis=0)
            live[hd] = (k_c, vt_ext, _dot_nt(lhs, q_s[rows[d], hsl(hh)]))

        def stage_gate(hd):
            d = hd // H_B
            k_c, vt_ext, res = live[hd]
            rcol = gcols[d][:, hd:hd + 1]
            e = jnp.exp(jnp.where(tri[d], rcol - mu[hd:hd + 1, :], -jnp.inf))
            p = (res[0:L, :] * e).astype(BF16)
            _state_update(st_s, hd, vt_ext, k_c, ws[hd:hd + 1, :], decay[hd:hd + 1, :])
            live[hd] = (vt_ext, res[L:L + ST_ROWS, :], p)

        def stage_out(hd):
            d, hh = hd // H_B, hd % H_B
            vt_ext, inter, p = live.pop(hd)
            tot = a_int[hd:hd + 1, :] * inter + _dot(vt_ext, p)
            den = tot[DH_B:DH_B + 1, :]
            h_t = tot[0:DH_B, :] / jnp.maximum(jnp.abs(den), em[hd:hd + 1, :])
            (hf_s if d == 0 else hb_s)[ci[d], hsl(hh), :] = h_t

        stages = (stage_scores, stage_gate, stage_out)
        for step in range(N_HD + len(stages) - 1):
            for si in reversed(range(len(stages))):
                if 0 <= step - si < N_HD:
                    stages[si](step - si)
        return m_new

    lax.fori_loop(0, nc, body, m_ctx)

    ar = lax.broadcasted_iota(jnp.int32, (MXU_TILE, MXU_TILE), 0) // DH_B
    ac = lax.broadcasted_iota(jnp.int32, (MXU_TILE, MXU_TILE), 1) // DH_B
    avg = jnp.where(ar == ac, 1.0 / DH_B, 0.0).astype(BF16)

    def head_mean(v):
        vb = v.astype(BF16)
        return jnp.concatenate([_dot(vb[:, i * MXU_TILE:(i + 1) * MXU_TILE], avg)
                                for i in range(W_B // MXU_TILE)], axis=1)

    def epilogue(c, carry):
        rows = pl.ds(pl.multiple_of(c * L, L), L)
        hsum = (hf_s[c] + hb_s[c]).T * som_ref[0, rows, :].astype(F32)
        cen = hsum - head_mean(hsum)
        hn = cen * lax.rsqrt(head_mean(cen * cen) + LN_EPS)
        hb = hn * mhg_ref[...] + skip_ref[...] * xc_s[rows, :]
        o_ref[0, rows, :] = (hb * szm_ref[0, rows, :].astype(F32)).astype(BF16)
        return carry

    lax.fori_loop(0, nc, epilogue, 0)


def _mlstm_call(xm, xmc, som, szm, conv_w, conv_b, wq, wk, wv, wgc, wgm, bg, mhg, skip):
    b, t, _ = xm.shape
    nc = t // CHUNK
    seq = lambda n: pl.BlockSpec((1, n, W_B), lambda i: (i, 0, 0))
    const = lambda shape: pl.BlockSpec(shape, lambda i: (0,) * len(shape))
    scratch = [
        pltpu.VMEM((t + 2 * _PAD, W_B), F32),
        pltpu.VMEM((t, W_B), F32),
        pltpu.VMEM((t, W_B), BF16),
        pltpu.VMEM((t, W_B), BF16),
        pltpu.VMEM((t, W_B), BF16),
        pltpu.VMEM((nc, W_B, CHUNK), BF16),
        pltpu.VMEM((t, LANES), F32),
        pltpu.VMEM((nc, 3 * N_HD, CHUNK), F32),
        pltpu.VMEM((CTX_LEN + 2 * _PAD, W_B), F32),
        pltpu.VMEM((CTX_LEN, W_B), BF16),
        pltpu.VMEM((CTX_LEN, W_B), BF16),
        pltpu.VMEM((1, W_B, CHUNK), BF16),
        pltpu.VMEM((CTX_LEN, LANES), F32),
        pltpu.VMEM((1, 3 * N_HD, CHUNK), F32),
        pltpu.VMEM((N_HD, ST_ROWS, DH_B), F32),
        pltpu.VMEM((nc, W_B, CHUNK), F32),
        pltpu.VMEM((nc, W_B, CHUNK), F32),
    ]
    return pl.pallas_call(
        _mlstm_kernel,
        grid=(b,),
        in_specs=[seq(t), seq(CTX_LEN), seq(t), seq(t), const((3, W_B)), const((1, W_B)),
                  const((W_B, W_B)), const((W_B, W_B)), const((W_B, W_B)),
                  const((W_B, LANES)), const((W_B, LANES)), const((1, LANES)),
                  const((1, W_B)), const((1, W_B))],
        out_specs=seq(t),
        out_shape=jax.ShapeDtypeStruct((b, t, W_B), BF16),
        scratch_shapes=scratch,
        compiler_params=pltpu.CompilerParams(dimension_semantics=("arbitrary",),
                                             vmem_limit_bytes=VMEM_LIMIT),
        name="mlstm_bidir",
    )(xm, xmc, som, szm, conv_w, conv_b, wq, wk, wv, wgc, wgm, bg, mhg, skip)


def _out_kernel(h_ref, mod_ref, ya_ref, yb_ref, wo_ref, g_ref, b_ref, o_ref, *, n_sub):
    sub = h_ref.shape[1] // n_sub
    gate = mod_ref[0][:, 2 * D_MODEL:3 * D_MODEL]

    def mix_stage(s):
        rows = slice(s * sub, (s + 1) * sub)
        return (_dot(ya_ref[0, rows, :], wo_ref[0:W_A, :])
                + _dot(yb_ref[0, rows, :], wo_ref[W_A:W_A + W_B, :]))

    def norm_stage(s, y):
        rows = slice(s * sub, (s + 1) * sub)
        o_ref[0, rows, :] = _layer_norm(ALPHA * h_ref[0, rows, :] + gate * y) * g_ref[...] + b_ref[...]

    y = mix_stage(0)
    for s in range(n_sub):
        y_next = mix_stage(s + 1) if s + 1 < n_sub else None
        norm_stage(s, y)
        y = y_next


def _out_call(h, mod3, ya, yb, wo, g, bb, *, tm, n_sub):
    b, t, _ = h.shape
    const = lambda shape: pl.BlockSpec(shape, lambda i, j: (0,) * len(shape))
    row = lambda w: pl.BlockSpec((1, tm, w), lambda i, j: (i, j, 0))
    return pl.pallas_call(
        functools.partial(_out_kernel, n_sub=n_sub),
        grid=(b, t // tm),
        in_specs=[row(D_MODEL), pl.BlockSpec((1, 1, 3 * D_MODEL), lambda i, j: (i, 0, 0)),
                  row(W_A), row(W_B),
                  const((W_A + W_B, D_MODEL)), const((1, D_MODEL)), const((1, D_MODEL))],
        out_specs=row(D_MODEL),
        out_shape=jax.ShapeDtypeStruct((b, t, D_MODEL), F32),
        compiler_params=pltpu.CompilerParams(dimension_semantics=("parallel", "parallel"),
                                             vmem_limit_bytes=VMEM_LIMIT),
        name="out_proj_ln",
    )(h, mod3, ya, yb, wo, g, bb)


def _rot_partner(w):
    w4 = w.reshape(w.shape[:-1] + (2, 2, ROPE_FREQS))
    return jnp.stack([-w4[..., 1, :], w4[..., 0, :]], axis=-2).reshape(w.shape)


def _rope_tables(seq, scale_keep, scale_rope):
    n_rows = seq // GRID_W
    rowp = np.repeat(np.arange(n_rows, dtype=np.float32), GRID_W)
    colp = np.tile(np.arange(GRID_W, dtype=np.float32), n_rows)
    inv = (np.float32(ROPE_BASE) ** (-np.arange(ROPE_FREQS, dtype=np.float32) / np.float32(ROPE_FREQS)))
    ang = np.stack([rowp[:, None] * inv, colp[:, None] * inv], axis=1).astype(np.float32)
    cos = np.broadcast_to(np.cos(ang)[:, :, None, :], (seq, 2, 2, ROPE_FREQS)).reshape(seq, DR_A)
    sin = np.broadcast_to(np.sin(ang)[:, :, None, :], (seq, 2, 2, ROPE_FREQS)).reshape(seq, DR_A)
    z32 = np.zeros((seq, DR_A), np.float32)
    t1 = np.concatenate([np.full((seq, DN_A), scale_keep, np.float32), z32, cos * scale_rope], axis=1)
    t2 = np.concatenate([np.zeros((seq, DN_A), np.float32), z32, sin * scale_rope], axis=1)
    return jnp.asarray(t1, F32), jnp.asarray(t2, F32)


def _block_diag(w):
    nb, bs, _ = w.shape
    n = nb * bs
    rows = w.reshape(n, bs)
    ri = lax.broadcasted_iota(jnp.int32, (n, n), 0)
    ci = lax.broadcasted_iota(jnp.int32, (n, n), 1)
    out = jnp.zeros((n, n), w.dtype)
    for o in range(bs):
        out = out + jnp.where((ri // bs == ci // bs) & (ci % bs == o), rows[:, o:o + 1], 0.0)
    return out


def kernel(x, c, ctx, c_ctx, ln_in_g, ln_in_b, w_ada, b_ada, w_in, g_qa, w_qb, g_kva, w_kvb, conv_w, conv_b, w_mq, w_mk, w_mv, w_gate, b_gate, mh_g, skip, w_out, ln_g, ln_b):
    b, t, _ = x.shape
    l = 0
    r2 = lambda v: v.reshape(1, -1)

    wi = w_in[l]
    s_qa, s_kva, s_kr, s_za, s_xm, s_om = 256, 384, 416, 928, 1440, 1952
    w_kr = wi[:, s_kva:s_kr]
    krblk = jnp.concatenate([jnp.zeros((D_MODEL, DN_A), F32), _rot_partner(w_kr), w_kr], axis=1)
    wina = jnp.concatenate([wi[:, 0:s_kva], krblk], axis=1).astype(BF16)
    winr = wi[:, s_kr:].astype(BF16)
    wq3 = w_qb[l].reshape(Q_LORA, H_A, DN_A + DR_A)
    wq_r = wq3[..., DN_A:]
    wq = jnp.concatenate([wq3[..., :DN_A], _rot_partner(wq_r), wq_r], axis=-1)
    wq = wq.reshape(Q_LORA, H_A * SLAB).astype(BF16)
    wkv3 = w_kvb[l].reshape(KV_LORA, H_A, DN_A + DV_A)
    wkn = jnp.concatenate([wkv3[..., :DN_A], jnp.zeros((KV_LORA, H_A, SLAB - DN_A), F32)], axis=-1)
    wkn = wkn.reshape(KV_LORA, H_A * SLAB).astype(BF16)
    wvt = wkv3[..., DN_A:].reshape(KV_LORA, W_A).T.astype(BF16)
    wmq = _block_diag(w_mq[l])
    wmk = _block_diag(w_mk[l])
    wmv = _block_diag(w_mv[l])
    perm = jnp.array([0, 1, 2, 3, 8, 9, 10, 11, 4, 5, 6, 7, 12, 13, 14, 15])
    wg = jnp.pad(w_gate[l][:, perm], ((0, 0), (0, LANES - 2 * N_HD))).astype(BF16)
    bg = jnp.pad(b_gate[l][perm], (0, LANES - 2 * N_HD)).reshape(1, LANES)
    wgc, wgm = _gate_fold_call(wmq.astype(BF16), wmk.astype(BF16), wmv.astype(BF16), wg)
    wmk_s = (wmk * (DH_B ** -0.5)).astype(BF16)
    wo = w_out[l].astype(BF16)

    sm_scale = (DN_A + DR_A) ** -0.5 * LOG2_E
    t1q, t2q = _rope_tables(t, sm_scale, sm_scale)
    t1k, t2k = _rope_tables(t, 0.0, 1.0)
    n_ctx = ctx.shape[1]
    t1kc = jnp.asarray(np.concatenate([np.zeros((n_ctx, DN_A + DR_A), np.float32),
                                       np.ones((n_ctx, DR_A), np.float32)], axis=1))
    t2kc = jnp.asarray(np.zeros((n_ctx, SLAB), np.float32))

    cc = jnp.concatenate([c, c_ctx[None, :], jnp.zeros((16 - b - 1, D_MODEL), F32)], axis=0)
    mod = _ada_call(cc, w_ada[l], r2(b_ada[l]))
    mod3 = mod.reshape(16, 1, 3 * D_MODEL)

    lng, lnb = r2(ln_in_g), r2(ln_in_b)
    shared = (wina, winr, r2(g_qa[l]), wq, r2(g_kva[l]), wkn, wvt)
    q, kl, vtl, xm, sza, som, szm, h = _proj_call(x, mod3, None, lng, lnb, *shared, t1q, t2q, t1k, t2k,
                                                  tm=1024, n_sub=2, latent=True)
    kc, vtc, xmc = _proj_call(ctx, mod3, b, lng, lnb, *shared, t1kc, t2kc, t1kc, t2kc,
                              tm=n_ctx, n_sub=1, latent=False)

    ya = _attn_call(q, kc, kl, vtc, vtl, sza, tq=512)
    yb = _mlstm_call(xm, xmc, som, szm, conv_w[l], r2(conv_b[l]), wmq.astype(BF16), wmk_s,
                     wmv.astype(BF16), wgc, wgm, bg, r2(mh_g[l]), r2(skip[l]))
    return _out_call(h, mod3, ya, yb, wo, r2(ln_g[l]), r2(ln_b[l]), tm=1024, n_sub=4)
```

```python
import functools

import numpy as np
import jax
import jax.numpy as jnp
from jax import lax
from jax.experimental import pallas as pl
from jax.experimental.pallas import tpu as pltpu

F32 = jnp.float32
BF16 = jnp.bfloat16

D_MODEL = 1024
CTX_LEN = 256
GRID_W = 64
H_A, DN_A, DR_A, DV_A = 8, 64, 32, 64
W_A = H_A * DV_A
Q_LORA, KV_LORA = 256, 128
ROPE_FREQS = DR_A // 4
ROPE_BASE = 10000.0
H_B, DH_B = 4, 128
W_B = H_B * DH_B
QKV_BS = 4
DEPTH = 1
ALPHA = (2.0 * DEPTH) ** 0.25
LN_EPS = 1e-5
RMS_EPS = 1e-6
LOG2_E = 1.4426950408889634

LANES = 128
MXU_TILE = 256
SLAB = LANES
DV_EXT = DV_A + 16
ATTN_KEY_CHUNK = MXU_TILE
CHUNK = 256
N_HD = 2 * H_B
ST_ROWS = DH_B + 16
VMEM_LIMIT = 56 * 1024 * 1024

_NT = (((1,), (1,)), ((), ()))


def _dot(a, b):
    return jnp.dot(a, b, preferred_element_type=F32)


def _dot_nt(a, b):
    return lax.dot_general(a, b, _NT, preferred_element_type=F32)


def _layer_norm(x):
    mu = jnp.mean(x, axis=-1, keepdims=True)
    xc = x - mu
    var = jnp.mean(xc * xc, axis=-1, keepdims=True)
    return xc * lax.rsqrt(var + LN_EPS)


def _rms_norm(x, g):
    return (x * lax.rsqrt(jnp.mean(x * x, axis=-1, keepdims=True) + RMS_EPS)) * g


def _silu(x):
    return x * jax.nn.sigmoid(x)


def _log_sigmoid(x):
    return jnp.minimum(x, 0.0) - jnp.log1p(jnp.exp(-jnp.abs(x)))


def _ada_kernel(cc_ref, w_ref, b_ref, o_ref):
    a = _silu(cc_ref[...]).astype(BF16)
    o_ref[...] = _dot(a, w_ref[...].astype(BF16)) + b_ref[...]


def _ada_call(cc, w_ada, b_ada):
    n = w_ada.shape[1]
    tn = 1024
    return pl.pallas_call(
        _ada_kernel,
        grid=(n // tn,),
        in_specs=[pl.BlockSpec((16, D_MODEL), lambda j: (0, 0)),
                  pl.BlockSpec((D_MODEL, tn), lambda j: (0, j)),
                  pl.BlockSpec((1, tn), lambda j: (0, j))],
        out_specs=pl.BlockSpec((16, tn), lambda j: (0, j)),
        out_shape=jax.ShapeDtypeStruct((16, n), F32),
        compiler_params=pltpu.CompilerParams(dimension_semantics=("arbitrary",),
                                             vmem_limit_bytes=VMEM_LIMIT),
        name="ada_mod",
    )(cc, w_ada, b_ada)


def _rope_slab(s, t1, t2):
    return s * t1 + pltpu.roll(s, 32, 1) * t2


def _proj_kernel(x_ref, mod_ref, lng_ref, lnb_ref, wina_ref, winr_ref, gqa_ref, wq_ref, gkva_ref,
                 wkn_ref, wvt_ref, t1q_ref, t2q_ref, t1k_ref, t2k_ref, *out_refs, latent, n_sub):
    if latent:
        q_ref, k_ref, vt_ref, xm_ref, sza_ref, som_ref, szm_ref, h_ref = out_refs
    else:
        k_ref, vt_ref, xm_ref = out_refs
    sub = x_ref.shape[1] // n_sub
    mod = mod_ref[0]
    shift, scale1 = mod[:, 0:D_MODEL], 1.0 + mod[:, D_MODEL:2 * D_MODEL]
    ones_rows = (lax.broadcasted_iota(jnp.int32, (DV_EXT - DV_A, sub), 0) == 0).astype(BF16)

    def norm_stage(s):
        rows = slice(s * sub, (s + 1) * sub)
        h = _layer_norm(x_ref[0, rows, :]) * lng_ref[...] + lnb_ref[...]
        if latent:
            h_ref[0, rows, :] = h
        return (h * scale1 + shift).astype(BF16)

    def proj_stage(s, u):
        rows = slice(s * sub, (s + 1) * sub)
        p0 = _dot(u, wina_ref[...])
        kvn = _rms_norm(p0[:, 256:384], gkva_ref[...]).astype(BF16)
        kn = _dot(kvn, wkn_ref[...])
        krr = _rope_slab(p0[:, 384:512], t1k_ref[rows, :], t2k_ref[rows, :])
        for hh in range(H_A):
            k_ref[0, hh, rows, :] = (kn[:, hh * SLAB:(hh + 1) * SLAB] + krr).astype(BF16)
        vt = _dot_nt(wvt_ref[...], kvn).astype(BF16)
        for hh in range(H_A):
            vt_ref[0, hh, 0:DV_A, rows] = vt[hh * DV_A:(hh + 1) * DV_A, :]
            vt_ref[0, hh, DV_A:DV_EXT, rows] = ones_rows
        xm_ref[0, rows, :] = _dot(u, winr_ref[:, 512:1024]).astype(BF16)
        if latent:
            qn = _rms_norm(p0[:, 0:256], gqa_ref[...]).astype(BF16)
            qs = _dot(qn, wq_ref[...])
            t1q, t2q = t1q_ref[rows, :], t2q_ref[rows, :]
            for hh in range(H_A):
                q_ref[0, hh, rows, :] = _rope_slab(qs[:, hh * SLAB:(hh + 1) * SLAB], t1q, t2q).astype(BF16)
            sza_ref[0, rows, :] = _silu(_dot(u, winr_ref[:, 0:512])).astype(BF16)
            som_ref[0, rows, :] = jax.nn.sigmoid(_dot(u, winr_ref[:, 1024:1536])).astype(BF16)
            szm_ref[0, rows, :] = _silu(_dot(u, winr_ref[:, 1536:2048])).astype(BF16)

    u = norm_stage(0)
    for s in range(n_sub):
        u_next = norm_stage(s + 1) if s + 1 < n_sub else None
        proj_stage(s, u)
        u = u_next


def _proj_call(x, mod3, mod_row, lng, lnb, wina, winr, gqa, wq, gkva, wkn, wvt, t1q, t2q, t1k, t2k, *,
               tm, n_sub, latent):
    b, t, _ = x.shape
    const = lambda shape: pl.BlockSpec(shape, lambda i, j: (0,) * len(shape))
    tab = pl.BlockSpec((tm, SLAB), lambda i, j: (j, 0))
    row = lambda w: pl.BlockSpec((1, tm, w), lambda i, j: (i, j, 0))
    if mod_row is None:
        mod_spec = pl.BlockSpec((1, 1, 3 * D_MODEL), lambda i, j: (i, 0, 0))
    else:
        mod_spec = pl.BlockSpec((1, 1, 3 * D_MODEL), lambda i, j: (mod_row, 0, 0))
    in_specs = [row(D_MODEL), mod_spec, const((1, D_MODEL)), const((1, D_MODEL)),
                const(wina.shape), const(winr.shape), const((1, Q_LORA)), const((Q_LORA, H_A * SLAB)),
                const((1, KV_LORA)), const((KV_LORA, H_A * SLAB)), const((W_A, KV_LORA)),
                tab, tab, tab, tab]
    k_spec = pl.BlockSpec((1, H_A, tm, SLAB), lambda i, j: (i, 0, j, 0))
    vt_spec = pl.BlockSpec((1, H_A, DV_EXT, tm), lambda i, j: (i, 0, 0, j))
    k_shape = jax.ShapeDtypeStruct((b, H_A, t, SLAB), BF16)
    vt_shape = jax.ShapeDtypeStruct((b, H_A, DV_EXT, t), BF16)
    half = jax.ShapeDtypeStruct((b, t, W_B), BF16)
    if latent:
        out_specs = [k_spec, k_spec, vt_spec, row(W_B), row(W_A), row(W_B), row(W_B), row(D_MODEL)]
        out_shape = [k_shape, k_shape, vt_shape, half, half, half, half,
                     jax.ShapeDtypeStruct((b, t, D_MODEL), F32)]
    else:
        out_specs = [k_spec, vt_spec, row(W_B)]
        out_shape = [k_shape, vt_shape, half]
    return pl.pallas_call(
        functools.partial(_proj_kernel, latent=latent, n_sub=n_sub),
        grid=(b, t // tm),
        in_specs=in_specs, out_specs=out_specs, out_shape=out_shape,
        compiler_params=pltpu.CompilerParams(dimension_semantics=("parallel", "parallel"),
                                             vmem_limit_bytes=VMEM_LIMIT),
        name="in_proj_latent" if latent else "in_proj_ctx",
    )(x, mod3, lng, lnb, wina, winr, gqa, wq, gkva, wkn, wvt, t1q, t2q, t1k, t2k)


def _attn_kernel(q_ref, kc_ref, kl_ref, vtc_ref, vtl_ref, sza_ref, o_ref, s_buf, ot_s):
    n_ctx, t = kc_ref.shape[2], kl_ref.shape[2]
    kc = ATTN_KEY_CHUNK

    chunks = [(kc_ref, vtc_ref, c * kc, c * kc) for c in range(n_ctx // kc)]
    chunks += [(kl_ref, vtl_ref, c * kc, n_ctx + c * kc) for c in range(t // kc)]

    def score_chunk(h, slot, qh, chunk, m):
        kref, _, k0, s0 = chunk
        sc = _dot_nt(kref[0, h, k0:k0 + kc, :], qh)
        s_buf[slot, s0:s0 + kc, :] = sc
        mc = jnp.max(sc, axis=0, keepdims=True)
        return mc if m is None else jnp.maximum(m, mc)

    def value_chunk(h, slot, chunk, m, acc):
        _, vref, v0, s0 = chunk
        p = jnp.exp2(s_buf[slot, s0:s0 + kc, :] - m).astype(BF16)
        d = _dot(vref[0, h, :, v0:v0 + kc], p)
        return d if acc is None else acc + d

    def finish(h, acc):
        rows = pl.ds(pl.multiple_of(h * DV_A, DV_A), DV_A)
        ot_s[rows, :] = acc[0:DV_A, :] / acc[DV_A:DV_A + 1, :]

    def step(h_next, slot_next, h_cur, slot_cur, m_cur):
        qh = None if h_next is None else q_ref[0, h_next]
        m_next, acc = None, None
        for chunk in chunks:
            if h_next is not None:
                m_next = score_chunk(h_next, slot_next, qh, chunk, m_next)
            if h_cur is not None:
                acc = value_chunk(h_cur, slot_cur, chunk, m_cur, acc)
        if h_cur is not None:
            finish(h_cur, acc)
        return m_next

    def body(i, m_even):
        h = 2 * i
        m_odd = step(h + 1, 1, h, 0, m_even)
        return step(h + 2, 0, h + 1, 1, m_odd)

    m_even = lax.fori_loop(0, H_A // 2 - 1, body, step(0, 0, None, None, None))
    m_odd = step(H_A - 1, 1, H_A - 2, 0, m_even)
    step(None, None, H_A - 1, 1, m_odd)
    o_ref[0] = (ot_s[...].T * sza_ref[0].astype(F32)).astype(BF16)


def _attn_call(q, kc, kl, vtc, vtl, sza, *, tq):
    b, _, t, _ = q.shape
    n_ctx = kc.shape[2]
    head_blk = lambda n, w: pl.BlockSpec((1, H_A, n, w), lambda i, j: (i, 0, 0, 0))
    return pl.pallas_call(
        _attn_kernel,
        grid=(b, t // tq),
        in_specs=[pl.BlockSpec((1, H_A, tq, SLAB), lambda i, j: (i, 0, j, 0)),
                  head_blk(n_ctx, SLAB), head_blk(t, SLAB),
                  head_blk(DV_EXT, n_ctx), head_blk(DV_EXT, t),
                  pl.BlockSpec((1, tq, W_A), lambda i, j: (i, j, 0))],
        out_specs=pl.BlockSpec((1, tq, W_A), lambda i, j: (i, j, 0)),
        out_shape=jax.ShapeDtypeStruct((b, t, W_A), BF16),
        scratch_shapes=[pltpu.VMEM((2, n_ctx + t, tq), F32),
                        pltpu.VMEM((W_A, tq), F32)],
        compiler_params=pltpu.CompilerParams(dimension_semantics=("parallel", "parallel"),
                                             vmem_limit_bytes=VMEM_LIMIT),
        name="mla_attention",
    )(q, kc, kl, vtc, vtl, sza)


def _gate_fold_kernel(wq_ref, wk_ref, wv_ref, wg_ref, wgc_ref, wgm_ref):
    wgc_ref[...] = (_dot(wq_ref[...], wg_ref[0:W_B, :])
                    + _dot(wk_ref[...], wg_ref[W_B:2 * W_B, :])).astype(BF16)
    wgm_ref[...] = _dot(wv_ref[...], wg_ref[2 * W_B:3 * W_B, :]).astype(BF16)


def _gate_fold_call(wq, wk, wv, wg):
    shp = jax.ShapeDtypeStruct((W_B, LANES), BF16)
    return pl.pallas_call(_gate_fold_kernel, out_shape=[shp, shp],
                          compiler_params=pltpu.CompilerParams(vmem_limit_bytes=VMEM_LIMIT),
                          name="mlstm_gate_fold")(wq, wk, wv, wg)


_PAD = 8


def _lane_scans(rows_per_chunk, combine, fill, out):
    npc = CHUNK // LANES
    flat = [x[:, i * LANES:(i + 1) * LANES] for x in rows_per_chunk for i in range(npc)]
    lane = lax.broadcasted_iota(jnp.int32, flat[0].shape, 1)
    pre, suf, sh = list(flat), list(flat), 1
    while sh < LANES:
        pre = [combine(p, jnp.where(lane >= sh, pltpu.roll(p, sh, 1), fill)) for p in pre]
        suf = [combine(s, jnp.where(lane < LANES - sh, pltpu.roll(s, LANES - sh, 1), fill)) for s in suf]
        sh *= 2
        yield
    for c in range(len(rows_per_chunk)):
        p, s = pre[c * npc:(c + 1) * npc], suf[c * npc:(c + 1) * npc]
        tot = [x[:, LANES - 1:LANES] for x in p]
        run = None
        for i in range(npc):
            if run is not None:
                p[i] = combine(p[i], run)
            run = tot[i] if run is None else combine(run, tot[i])
        run = None
        for i in reversed(range(npc)):
            if run is not None:
                s[i] = combine(s[i], run)
            run = tot[i] if run is None else combine(run, tot[i])
        out.append((jnp.concatenate(p, axis=1), jnp.concatenate(s, axis=1)))


def _gate_tables(gts, dests):
    L = CHUNK
    fwd_rows = lax.broadcasted_iota(jnp.int32, (N_HD, L), 0) < H_B
    lis = [gt[0:N_HD, :] for gt in gts]
    sums = []
    yield from _lane_scans([_log_sigmoid(gt[N_HD:2 * N_HD, :]) for gt in gts], jnp.add, 0.0, sums)
    cums = [jnp.where(fwd_rows, ps, ss) for ps, ss in sums]
    rs = [li - cum for li, cum in zip(lis, cums)]
    maxs = []
    yield from _lane_scans(rs, jnp.maximum, -jnp.inf, maxs)
    for (grow_ref, c, gcol_ref, r0), cum, r, (pm, sm) in zip(dests, cums, rs, maxs):
        grow_ref[c] = jnp.concatenate([cum, r, jnp.where(fwd_rows, pm, sm)], axis=0)
        gcol_ref[r0:r0 + L, :] = jnp.concatenate([r, jnp.zeros((LANES - N_HD, L), F32)], axis=0).T


def _fill_pad(pad_s, xm_ref, n):
    zrow = jnp.zeros((_PAD, W_B), F32)
    pad_s[0:_PAD, :] = zrow
    pad_s[_PAD:_PAD + n, :] = xm_ref[...].astype(F32)
    pad_s[_PAD + n:2 * _PAD + n, :] = zrow


def _conv_gate_stage(c, pad_s, xm_ref, cw_ref, cb_ref, wgc_ref, wgm_ref, bg_ref, xc_s, xcb_s):
    L = CHUNK
    rows = slice(c * L, (c + 1) * L)
    r0 = c * L + _PAD
    row_id = lax.broadcasted_iota(jnp.int32, (L, W_B), 0)
    xcur = pad_s[r0:r0 + L, :]
    xprev = jnp.where(row_id == 0, pad_s[r0 - 1:r0, :], pltpu.roll(xcur, 1, 0))
    xnext = jnp.where(row_id == L - 1, pad_s[r0 + L:r0 + L + 1, :], pltpu.roll(xcur, L - 1, 0))
    pre = cb_ref[...] + xprev * cw_ref[0:1, :] + xcur * cw_ref[1:2, :] + xnext * cw_ref[2:3, :]
    xc = _silu(pre)
    if xc_s is not None:
        xc_s[rows, :] = xc
    xcb = xc.astype(BF16)
    xcb_s[rows, :] = xcb
    g = _dot(xcb, wgc_ref[...]) + _dot(xm_ref[rows, :], wgm_ref[...]) + bg_ref[...]
    return g.T[0:2 * N_HD, :]


def _headwise_stage(c, xm_ref, xcb_s, wq_ref, wk_ref, wv_ref, q_s, k_s, vt_s):
    L = CHUNK
    rows = slice(c * L, (c + 1) * L)
    for hf in range(W_B // MXU_TILE):
        cs = slice(hf * MXU_TILE, (hf + 1) * MXU_TILE)
        xcb = xcb_s[rows, cs]
        if q_s is not None:
            q_s[rows, cs] = _dot(xcb, wq_ref[cs, cs]).astype(BF16)
        k_s[rows, cs] = _dot(xcb, wk_ref[cs, cs]).astype(BF16)
        vt_s[c, cs, :] = _dot(xm_ref[rows, cs], wv_ref[cs, cs]).T.astype(BF16)


_TAB_ROWS = 5 * N_HD


def _chain_tables(growc_s, grow_s, tab_s, nc):
    L = CHUNK
    fwd_rows = lax.broadcasted_iota(jnp.int32, (N_HD, L), 0) < H_B
    gc = growc_s[0]
    steps = [(gc, gc)] + [(grow_s[i], grow_s[nc - 1 - i]) for i in range(nc)]
    parts = []
    for g_f, g_b in steps:
        pick = lambda a: jnp.where(fwd_rows, g_f[a * N_HD:(a + 1) * N_HD, :], g_b[a * N_HD:(a + 1) * N_HD, :])
        cum, r, pm = pick(0), pick(1), pick(2)
        b_end = jnp.where(fwd_rows[:, 0:1], cum[:, L - 1:L], cum[:, 0:1])
        w = r + b_end
        parts.append((cum, pm, b_end, w, jnp.max(w, axis=1, keepdims=True)))
    m0 = jnp.zeros((N_HD, 1), F32)
    for idx, (cum, pm, b_end, w, wmax) in enumerate(parts):
        m_new = jnp.maximum(b_end + m0, wmax)
        mu = jnp.maximum(m0, pm)
        decay = jnp.broadcast_to(jnp.exp(b_end + m0 - m_new), (N_HD, L))
        tab_s[idx] = jnp.concatenate([mu, jnp.exp(m0 - mu), jnp.exp(-(cum + mu)),
                                      jnp.exp(w - m_new), decay], axis=0)
        m0 = m_new


def _tab(tab, a, hd):
    return tab[a * N_HD + hd:a * N_HD + hd + 1, :]


def _state_update(st_ref, hd, vt_ext, k_c, ws_row, decay_hd):
    vw = (vt_ext.astype(F32) * ws_row).astype(BF16)
    st_ref[hd] = decay_hd * st_ref[hd] + _dot(vw, k_c)


def _mlstm_kernel(xm_ref, xmc_ref, som_ref, szm_ref, cw_ref, cb_ref, wq_ref, wk_ref, wv_ref,
                  wgc_ref, wgm_ref, bg_ref, mhg_ref, skip_ref, o_ref,
                  pad_s, xc_s, xcb_s, q_s, k_s, vt_s, gcol_s, grow_s,
                  padc_s, xcbc_s, kc_s, vtc_s, gcolc_s, growc_s, tab_s, st_s, hf_s, hb_s):
    L = CHUNK
    t = xm_ref.shape[1]
    nc = t // L
    row_i = lax.broadcasted_iota(jnp.int32, (L, L), 0)
    col_i = lax.broadcasted_iota(jnp.int32, (L, L), 1)
    tri = (row_i <= col_i, row_i >= col_i)
    ones_rows = (lax.broadcasted_iota(jnp.int32, (ST_ROWS - DH_B, L), 0) == 0).astype(BF16)
    hsl = lambda hh: slice(hh * DH_B, (hh + 1) * DH_B)
    xm, xmc = xm_ref.at[0], xmc_ref.at[0]
    gate_w = (cw_ref, cb_ref, wgc_ref, wgm_ref, bg_ref)

    _fill_pad(padc_s, xmc, CTX_LEN)
    _fill_pad(pad_s, xm, t)
    gts = [_conv_gate_stage(0, padc_s, xmc, *gate_w, None, xcbc_s)]
    gts += [_conv_gate_stage(c, pad_s, xm, *gate_w, xc_s, xcb_s) for c in range(nc)]
    dests = [(growc_s, 0, gcolc_s, 0)] + [(grow_s, c, gcol_s, c * L) for c in range(nc)]
    tables = _gate_tables(gts, dests)
    _headwise_stage(0, xmc, xcbc_s, wq_ref, wk_ref, wv_ref, None, kc_s, vtc_s)
    for c in range(nc):
        next(tables, None)
        next(tables, None)
        _headwise_stage(c, xm, xcb_s, wq_ref, wk_ref, wv_ref, q_s, k_s, vt_s)
    for _ in tables:
        pass

    _chain_tables(growc_s, grow_s, tab_s, nc)
    st_s[...] = jnp.zeros(st_s.shape, F32)
    tab = tab_s[0]
    for hd in range(N_HD):
        hh = hd % H_B
        vt_ext = jnp.concatenate([vtc_s[0, hsl(hh), :], ones_rows], axis=0)
        _state_update(st_s, hd, vt_ext, kc_s[:, hsl(hh)], _tab(tab, 3, hd), _tab(tab, 4, hd)[:, 0:DH_B])

    def body(i, carry):
        ci = (i, nc - 1 - i)
        rows = tuple(pl.ds(pl.multiple_of(c * L, L), L) for c in ci)
        tab = tab_s[i + 1]
        gcols = tuple(gcol_s[r, :] for r in rows)
        live = {}

        def stage_scores(hd):
            d, hh = hd // H_B, hd % H_B
            k_c = k_s[rows[d], hsl(hh)]
            vt_ext = jnp.concatenate([vt_s[ci[d], hsl(hh), :], ones_rows], axis=0)
            lhs = jnp.concatenate([k_c, st_s[hd].astype(BF16)], axis=0)
            live[hd] = (k_c, vt_ext, _dot_nt(lhs, q_s[rows[d], hsl(hh)]))

        def stage_gate(hd):
            d = hd // H_B
            k_c, vt_ext, res = live[hd]
            rcol = gcols[d][:, hd:hd + 1]
            e = jnp.exp(jnp.where(tri[d], rcol - _tab(tab, 0, hd), -jnp.inf))
            p = (res[0:L, :] * e).astype(BF16)
            _state_update(st_s, hd, vt_ext, k_c, _tab(tab, 3, hd), _tab(tab, 4, hd)[:, 0:DH_B])
            live[hd] = (vt_ext, res[L:L + ST_ROWS, :], p)

        def stage_out(hd):
            d, hh = hd // H_B, hd % H_B
            vt_ext, inter, p = live.pop(hd)
            tot = _tab(tab, 1, hd) * inter + _dot(vt_ext, p)
            den = tot[DH_B:DH_B + 1, :]
            h_t = tot[0:DH_B, :] / jnp.maximum(jnp.abs(den), _tab(tab, 2, hd))
            (hf_s if d == 0 else hb_s)[ci[d], hsl(hh), :] = h_t

        stages = (stage_scores, stage_gate, stage_out)
        for step in range(N_HD + len(stages) - 1):
            for si in reversed(range(len(stages))):
                if 0 <= step - si < N_HD:
                    stages[si](step - si)
        return carry

    lax.fori_loop(0, nc, body, 0)

    ar = lax.broadcasted_iota(jnp.int32, (MXU_TILE, MXU_TILE), 0) // DH_B
    ac = lax.broadcasted_iota(jnp.int32, (MXU_TILE, MXU_TILE), 1) // DH_B
    avg = jnp.where(ar == ac, 1.0 / DH_B, 0.0).astype(BF16)

    def head_mean(v):
        vb = v.astype(BF16)
        return jnp.concatenate([_dot(vb[:, i * MXU_TILE:(i + 1) * MXU_TILE], avg)
                                for i in range(W_B // MXU_TILE)], axis=1)

    def epilogue(c, carry):
        rows = pl.ds(pl.multiple_of(c * L, L), L)
        hsum = (hf_s[c] + hb_s[c]).T * som_ref[0, rows, :].astype(F32)
        cen = hsum - head_mean(hsum)
        hn = cen * lax.rsqrt(head_mean(cen * cen) + LN_EPS)
        hb = hn * mhg_ref[...] + skip_ref[...] * xc_s[rows, :]
        o_ref[0, rows, :] = (hb * szm_ref[0, rows, :].astype(F32)).astype(BF16)
        return carry

    lax.fori_loop(0, nc, epilogue, 0)


def _mlstm_call(xm, xmc, som, szm, conv_w, conv_b, wq, wk, wv, wgc, wgm, bg, mhg, skip):
    b, t, _ = xm.shape
    nc = t // CHUNK
    seq = lambda n: pl.BlockSpec((1, n, W_B), lambda i: (i, 0, 0))
    const = lambda shape: pl.BlockSpec(shape, lambda i: (0,) * len(shape))
    scratch = [
        pltpu.VMEM((t + 2 * _PAD, W_B), F32),
        pltpu.VMEM((t, W_B), F32),
        pltpu.VMEM((t, W_B), BF16),
        pltpu.VMEM((t, W_B), BF16),
        pltpu.VMEM((t, W_B), BF16),
        pltpu.VMEM((nc, W_B, CHUNK), BF16),
        pltpu.VMEM((t, LANES), F32),
        pltpu.VMEM((nc, 3 * N_HD, CHUNK), F32),
        pltpu.VMEM((CTX_LEN + 2 * _PAD, W_B), F32),
        pltpu.VMEM((CTX_LEN, W_B), BF16),
        pltpu.VMEM((CTX_LEN, W_B), BF16),
        pltpu.VMEM((1, W_B, CHUNK), BF16),
        pltpu.VMEM((CTX_LEN, LANES), F32),
        pltpu.VMEM((1, 3 * N_HD, CHUNK), F32),
        pltpu.VMEM((nc + 1, _TAB_ROWS, CHUNK), F32),
        pltpu.VMEM((N_HD, ST_ROWS, DH_B), F32),
        pltpu.VMEM((nc, W_B, CHUNK), F32),
        pltpu.VMEM((nc, W_B, CHUNK), F32),
    ]
    return pl.pallas_call(
        _mlstm_kernel,
        grid=(b,),
        in_specs=[seq(t), seq(CTX_LEN), seq(t), seq(t), const((3, W_B)), const((1, W_B)),
                  const((W_B, W_B)), const((W_B, W_B)), const((W_B, W_B)),
                  const((W_B, LANES)), const((W_B, LANES)), const((1, LANES)),
                  const((1, W_B)), const((1, W_B))],
        out_specs=seq(t),
        out_shape=jax.ShapeDtypeStruct((b, t, W_B), BF16),
        scratch_shapes=scratch,
        compiler_params=pltpu.CompilerParams(dimension_semantics=("arbitrary",),
                                             vmem_limit_bytes=VMEM_LIMIT),
        name="mlstm_bidir",
    )(xm, xmc, som, szm, conv_w, conv_b, wq, wk, wv, wgc, wgm, bg, mhg, skip)


def _out_kernel(h_ref, mod_ref, ya_ref, yb_ref, wo_ref, g_ref, b_ref, o_ref, *, n_sub):
    sub = h_ref.shape[1] // n_sub
    gate = mod_ref[0][:, 2 * D_MODEL:3 * D_MODEL]

    def mix_stage(s):
        rows = slice(s * sub, (s + 1) * sub)
        return (_dot(ya_ref[0, rows, :], wo_ref[0:W_A, :])
                + _dot(yb_ref[0, rows, :], wo_ref[W_A:W_A + W_B, :]))

    def norm_stage(s, y):
        rows = slice(s * sub, (s + 1) * sub)
        o_ref[0, rows, :] = _layer_norm(ALPHA * h_ref[0, rows, :] + gate * y) * g_ref[...] + b_ref[...]

    y = mix_stage(0)
    for s in range(n_sub):
        y_next = mix_stage(s + 1) if s + 1 < n_sub else None
        norm_stage(s, y)
        y = y_next


def _out_call(h, mod3, ya, yb, wo, g, bb, *, tm, n_sub):
    b, t, _ = h.shape
    const = lambda shape: pl.BlockSpec(shape, lambda i, j: (0,) * len(shape))
    row = lambda w: pl.BlockSpec((1, tm, w), lambda i, j: (i, j, 0))
    return pl.pallas_call(
        functools.partial(_out_kernel, n_sub=n_sub),
        grid=(b, t // tm),
        in_specs=[row(D_MODEL), pl.BlockSpec((1, 1, 3 * D_MODEL), lambda i, j: (i, 0, 0)),
                  row(W_A), row(W_B),
                  const((W_A + W_B, D_MODEL)), const((1, D_MODEL)), const((1, D_MODEL))],
        out_specs=row(D_MODEL),
        out_shape=jax.ShapeDtypeStruct((b, t, D_MODEL), F32),
        compiler_params=pltpu.CompilerParams(dimension_semantics=("parallel", "parallel"),
                                             vmem_limit_bytes=VMEM_LIMIT),
        name="out_proj_ln",
    )(h, mod3, ya, yb, wo, g, bb)


def _rot_partner(w):
    w4 = w.reshape(w.shape[:-1] + (2, 2, ROPE_FREQS))
    return jnp.stack([-w4[..., 1, :], w4[..., 0, :]], axis=-2).reshape(w.shape)


def _rope_tables(seq, scale_keep, scale_rope):
    n_rows = seq // GRID_W
    rowp = np.repeat(np.arange(n_rows, dtype=np.float32), GRID_W)
    colp = np.tile(np.arange(GRID_W, dtype=np.float32), n_rows)
    inv = (np.float32(ROPE_BASE) ** (-np.arange(ROPE_FREQS, dtype=np.float32) / np.float32(ROPE_FREQS)))
    ang = np.stack([rowp[:, None] * inv, colp[:, None] * inv], axis=1).astype(np.float32)
    cos = np.broadcast_to(np.cos(ang)[:, :, None, :], (seq, 2, 2, ROPE_FREQS)).reshape(seq, DR_A)
    sin = np.broadcast_to(np.sin(ang)[:, :, None, :], (seq, 2, 2, ROPE_FREQS)).reshape(seq, DR_A)
    z32 = np.zeros((seq, DR_A), np.float32)
    t1 = np.concatenate([np.full((seq, DN_A), scale_keep, np.float32), z32, cos * scale_rope], axis=1)
    t2 = np.concatenate([np.zeros((seq, DN_A), np.float32), z32, sin * scale_rope], axis=1)
    return jnp.asarray(t1, F32), jnp.asarray(t2, F32)


def _block_diag(w):
    nb, bs, _ = w.shape
    n = nb * bs
    rows = w.reshape(n, bs)
    ri = lax.broadcasted_iota(jnp.int32, (n, n), 0)
    ci = lax.broadcasted_iota(jnp.int32, (n, n), 1)
    out = jnp.zeros((n, n), w.dtype)
    for o in range(bs):
        out = out + jnp.where((ri // bs == ci // bs) & (ci % bs == o), rows[:, o:o + 1], 0.0)
    return out


def kernel(x, c, ctx, c_ctx, ln_in_g, ln_in_b, w_ada, b_ada, w_in, g_qa, w_qb, g_kva, w_kvb, conv_w, conv_b, w_mq, w_mk, w_mv, w_gate, b_gate, mh_g, skip, w_out, ln_g, ln_b):
    b, t, _ = x.shape
    l = 0
    r2 = lambda v: v.reshape(1, -1)

    wi = w_in[l]
    s_qa, s_kva, s_kr, s_za, s_xm, s_om = 256, 384, 416, 928, 1440, 1952
    w_kr = wi[:, s_kva:s_kr]
    krblk = jnp.concatenate([jnp.zeros((D_MODEL, DN_A), F32), _rot_partner(w_kr), w_kr], axis=1)
    wina = jnp.concatenate([wi[:, 0:s_kva], krblk], axis=1).astype(BF16)
    winr = wi[:, s_kr:].astype(BF16)
    wq3 = w_qb[l].reshape(Q_LORA, H_A, DN_A + DR_A)
    wq_r = wq3[..., DN_A:]
    wq = jnp.concatenate([wq3[..., :DN_A], _rot_partner(wq_r), wq_r], axis=-1)
    wq = wq.reshape(Q_LORA, H_A * SLAB).astype(BF16)
    wkv3 = w_kvb[l].reshape(KV_LORA, H_A, DN_A + DV_A)
    wkn = jnp.concatenate([wkv3[..., :DN_A], jnp.zeros((KV_LORA, H_A, SLAB - DN_A), F32)], axis=-1)
    wkn = wkn.reshape(KV_LORA, H_A * SLAB).astype(BF16)
    wvt = wkv3[..., DN_A:].reshape(KV_LORA, W_A).T.astype(BF16)
    wmq = _block_diag(w_mq[l])
    wmk = _block_diag(w_mk[l])
    wmv = _block_diag(w_mv[l])
    perm = jnp.array([0, 1, 2, 3, 8, 9, 10, 11, 4, 5, 6, 7, 12, 13, 14, 15])
    wg = jnp.pad(w_gate[l][:, perm], ((0, 0), (0, LANES - 2 * N_HD))).astype(BF16)
    bg = jnp.pad(b_gate[l][perm], (0, LANES - 2 * N_HD)).reshape(1, LANES)
    wgc, wgm = _gate_fold_call(wmq.astype(BF16), wmk.astype(BF16), wmv.astype(BF16), wg)
    wmk_s = (wmk * (DH_B ** -0.5)).astype(BF16)
    wo = w_out[l].astype(BF16)

    sm_scale = (DN_A + DR_A) ** -0.5 * LOG2_E
    t1q, t2q = _rope_tables(t, sm_scale, sm_scale)
    t1k, t2k = _rope_tables(t, 0.0, 1.0)
    n_ctx = ctx.shape[1]
    t1kc = jnp.asarray(np.concatenate([np.zeros((n_ctx, DN_A + DR_A), np.float32),
                                       np.ones((n_ctx, DR_A), np.float32)], axis=1))
    t2kc = jnp.asarray(np.zeros((n_ctx, SLAB), np.float32))

    cc = jnp.concatenate([c, c_ctx[None, :], jnp.zeros((16 - b - 1, D_MODEL), F32)], axis=0)
    mod = _ada_call(cc, w_ada[l], r2(b_ada[l]))
    mod3 = mod.reshape(16, 1, 3 * D_MODEL)

    lng, lnb = r2(ln_in_g), r2(ln_in_b)
    shared = (wina, winr, r2(g_qa[l]), wq, r2(g_kva[l]), wkn, wvt)
    q, kl, vtl, xm, sza, som, szm, h = _proj_call(x, mod3, None, lng, lnb, *shared, t1q, t2q, t1k, t2k,
                                                  tm=1024, n_sub=2, latent=True)
    kc, vtc, xmc = _proj_call(ctx, mod3, b, lng, lnb, *shared, t1kc, t2kc, t1kc, t2kc,
                              tm=n_ctx, n_sub=1, latent=False)

    ya = _attn_call(q, kc, kl, vtc, vtl, sza, tq=512)
    yb = _mlstm_call(xm, xmc, som, szm, conv_w[l], r2(conv_b[l]), wmq.astype(BF16), wmk_s,
                     wmv.astype(BF16), wgc, wgm, bg, r2(mh_g[l]), r2(skip[l]))
    return _out_call(h, mod3, ya, yb, wo, r2(ln_g[l]), r2(ln_b[l]), tm=1024, n_sub=4)
```

```python
import functools

import numpy as np
import jax
import jax.numpy as jnp
from jax import lax
from jax.experimental import pallas as pl
from jax.experimental.pallas import tpu as pltpu

F32 = jnp.float32
BF16 = jnp.bfloat16

D_MODEL = 1024
CTX_LEN = 256
GRID_W = 64
H_A, DN_A, DR_A, DV_A = 8, 64, 32, 64
W_A = H_A * DV_A
Q_LORA, KV_LORA = 256, 128
ROPE_FREQS = DR_A // 4
ROPE_BASE = 10000.0
H_B, DH_B = 4, 128
W_B = H_B * DH_B
QKV_BS = 4
DEPTH = 1
ALPHA = (2.0 * DEPTH) ** 0.25
LN_EPS = 1e-5
RMS_EPS = 1e-6
LOG2_E = 1.4426950408889634

LANES = 128
MXU_TILE = 256
SLAB = LANES
DV_EXT = DV_A + 16
ATTN_KEY_CHUNK = MXU_TILE
CHUNK = 256
N_HD = 2 * H_B
ST_ROWS = DH_B + 16
VMEM_LIMIT = 56 * 1024 * 1024

_NT = (((1,), (1,)), ((), ()))


def _dot(a, b):
    return jnp.dot(a, b, preferred_element_type=F32)


def _dot_nt(a, b):
    return lax.dot_general(a, b, _NT, preferred_element_type=F32)


def _layer_norm(x):
    mu = jnp.mean(x, axis=-1, keepdims=True)
    xc = x - mu
    var = jnp.mean(xc * xc, axis=-1, keepdims=True)
    return xc * lax.rsqrt(var + LN_EPS)


def _rms_norm(x, g):
    return (x * lax.rsqrt(jnp.mean(x * x, axis=-1, keepdims=True) + RMS_EPS)) * g


def _silu(x):
    return x * jax.nn.sigmoid(x)


def _log_sigmoid(x):
    return jnp.minimum(x, 0.0) - jnp.log1p(jnp.exp(-jnp.abs(x)))


def _ada_kernel(cc_ref, w_ref, b_ref, o_ref):
    a = _silu(cc_ref[...]).astype(BF16)
    o_ref[...] = _dot(a, w_ref[...].astype(BF16)) + b_ref[...]


def _ada_call(cc, w_ada, b_ada):
    n = w_ada.shape[1]
    tn = 1024
    return pl.pallas_call(
        _ada_kernel,
        grid=(n // tn,),
        in_specs=[pl.BlockSpec((16, D_MODEL), lambda j: (0, 0)),
                  pl.BlockSpec((D_MODEL, tn), lambda j: (0, j)),
                  pl.BlockSpec((1, tn), lambda j: (0, j))],
        out_specs=pl.BlockSpec((16, tn), lambda j: (0, j)),
        out_shape=jax.ShapeDtypeStruct((16, n), F32),
        compiler_params=pltpu.CompilerParams(dimension_semantics=("arbitrary",),
                                             vmem_limit_bytes=VMEM_LIMIT),
        name="ada_mod",
    )(cc, w_ada, b_ada)


def _rope_slab(s, t1, t2):
    return s * t1 + pltpu.roll(s, 32, 1) * t2


def _proj_kernel(x_ref, mod_ref, lng_ref, lnb_ref, wina_ref, winr_ref, gqa_ref, wq_ref, gkva_ref,
                 wkn_ref, wvt_ref, t1q_ref, t2q_ref, t1k_ref, t2k_ref, *out_refs, latent, n_sub):
    if latent:
        q_ref, k_ref, vt_ref, xm_ref, sza_ref, som_ref, szm_ref, h_ref = out_refs
    else:
        k_ref, vt_ref, xm_ref = out_refs
    sub = x_ref.shape[1] // n_sub
    mod = mod_ref[0]
    shift, scale1 = mod[:, 0:D_MODEL], 1.0 + mod[:, D_MODEL:2 * D_MODEL]
    ones_rows = (lax.broadcasted_iota(jnp.int32, (DV_EXT - DV_A, sub), 0) == 0).astype(BF16)

    def norm_stage(s):
        rows = slice(s * sub, (s + 1) * sub)
        h = _layer_norm(x_ref[0, rows, :]) * lng_ref[...] + lnb_ref[...]
        if latent:
            h_ref[0, rows, :] = h
        return (h * scale1 + shift).astype(BF16)

    def proj_stage(s, u):
        rows = slice(s * sub, (s + 1) * sub)
        p0 = _dot(u, wina_ref[...])
        kvn = _rms_norm(p0[:, 256:384], gkva_ref[...]).astype(BF16)
        kn = _dot(kvn, wkn_ref[...])
        krr = _rope_slab(p0[:, 384:512], t1k_ref[rows, :], t2k_ref[rows, :])
        for hh in range(H_A):
            k_ref[0, hh, rows, :] = (kn[:, hh * SLAB:(hh + 1) * SLAB] + krr).astype(BF16)
        vt = _dot_nt(wvt_ref[...], kvn).astype(BF16)
        for hh in range(H_A):
            vt_ref[0, hh, 0:DV_A, rows] = vt[hh * DV_A:(hh + 1) * DV_A, :]
            vt_ref[0, hh, DV_A:DV_EXT, rows] = ones_rows
        xm_ref[0, rows, :] = _dot(u, winr_ref[:, 512:1024]).astype(BF16)
        if latent:
            qn = _rms_norm(p0[:, 0:256], gqa_ref[...]).astype(BF16)
            qs = _dot(qn, wq_ref[...])
            t1q, t2q = t1q_ref[rows, :], t2q_ref[rows, :]
            for hh in range(H_A):
                q_ref[0, hh, rows, :] = _rope_slab(qs[:, hh * SLAB:(hh + 1) * SLAB], t1q, t2q).astype(BF16)
            sza_ref[0, rows, :] = _silu(_dot(u, winr_ref[:, 0:512])).astype(BF16)
            som_ref[0, rows, :] = jax.nn.sigmoid(_dot(u, winr_ref[:, 1024:1536])).astype(BF16)
            szm_ref[0, rows, :] = _silu(_dot(u, winr_ref[:, 1536:2048])).astype(BF16)

    u = norm_stage(0)
    for s in range(n_sub):
        u_next = norm_stage(s + 1) if s + 1 < n_sub else None
        proj_stage(s, u)
        u = u_next


def _proj_call(x, mod3, mod_row, lng, lnb, wina, winr, gqa, wq, gkva, wkn, wvt, t1q, t2q, t1k, t2k, *,
               tm, n_sub, latent):
    b, t, _ = x.shape
    const = lambda shape: pl.BlockSpec(shape, lambda i, j: (0,) * len(shape))
    tab = pl.BlockSpec((tm, SLAB), lambda i, j: (j, 0))
    row = lambda w: pl.BlockSpec((1, tm, w), lambda i, j: (i, j, 0))
    if mod_row is None:
        mod_spec = pl.BlockSpec((1, 1, 3 * D_MODEL), lambda i, j: (i, 0, 0))
    else:
        mod_spec = pl.BlockSpec((1, 1, 3 * D_MODEL), lambda i, j: (mod_row, 0, 0))
    in_specs = [row(D_MODEL), mod_spec, const((1, D_MODEL)), const((1, D_MODEL)),
                const(wina.shape), const(winr.shape), const((1, Q_LORA)), const((Q_LORA, H_A * SLAB)),
                const((1, KV_LORA)), const((KV_LORA, H_A * SLAB)), const((W_A, KV_LORA)),
                tab, tab, tab, tab]
    k_spec = pl.BlockSpec((1, H_A, tm, SLAB), lambda i, j: (i, 0, j, 0))
    vt_spec = pl.BlockSpec((1, H_A, DV_EXT, tm), lambda i, j: (i, 0, 0, j))
    k_shape = jax.ShapeDtypeStruct((b, H_A, t, SLAB), BF16)
    vt_shape = jax.ShapeDtypeStruct((b, H_A, DV_EXT, t), BF16)
    half = jax.ShapeDtypeStruct((b, t, W_B), BF16)
    if latent:
        out_specs = [k_spec, k_spec, vt_spec, row(W_B), row(W_A), row(W_B), row(W_B), row(D_MODEL)]
        out_shape = [k_shape, k_shape, vt_shape, half, half, half, half,
                     jax.ShapeDtypeStruct((b, t, D_MODEL), F32)]
    else:
        out_specs = [k_spec, vt_spec, row(W_B)]
        out_shape = [k_shape, vt_shape, half]
    return pl.pallas_call(
        functools.partial(_proj_kernel, latent=latent, n_sub=n_sub),
        grid=(b, t // tm),
        in_specs=in_specs, out_specs=out_specs, out_shape=out_shape,
        compiler_params=pltpu.CompilerParams(dimension_semantics=("parallel", "parallel"),
                                             vmem_limit_bytes=VMEM_LIMIT),
        name="in_proj_latent" if latent else "in_proj_ctx",
    )(x, mod3, lng, lnb, wina, winr, gqa, wq, gkva, wkn, wvt, t1q, t2q, t1k, t2k)


def _attn_kernel(q_ref, kc_ref, kl_ref, vtc_ref, vtl_ref, sza_ref, o_ref, s_buf, ot_s):
    n_ctx, t = kc_ref.shape[2], kl_ref.shape[2]
    kc = ATTN_KEY_CHUNK

    def scores(h, slot):
        qh = q_ref[0, h]
        sc = _dot_nt(kc_ref[0, h], qh)
        sk = _dot_nt(kl_ref[0, h], qh)
        s_buf[slot, 0:n_ctx, :] = sc
        s_buf[slot, n_ctx:n_ctx + t, :] = sk
        return jnp.maximum(jnp.max(sc, axis=0, keepdims=True), jnp.max(sk, axis=0, keepdims=True))

    def values(h, slot, m):
        chunks = [(vtc_ref, c * kc, c * kc) for c in range(n_ctx // kc)]
        chunks += [(vtl_ref, c * kc, n_ctx + c * kc) for c in range(t // kc)]
        acc = None
        for vref, v0, s0 in chunks:
            p = jnp.exp2(s_buf[slot, s0:s0 + kc, :] - m).astype(BF16)
            d = _dot(vref[0, h, :, v0:v0 + kc], p)
            acc = d if acc is None else acc + d
        rows = pl.ds(pl.multiple_of(h * DV_A, DV_A), DV_A)
        ot_s[rows, :] = acc[0:DV_A, :] / acc[DV_A:DV_A + 1, :]

    def body(i, m_even):
        h = 2 * i
        m_odd = scores(h + 1, 1)
        values(h, 0, m_even)
        m_even = scores(h + 2, 0)
        values(h + 1, 1, m_odd)
        return m_even

    m_even = lax.fori_loop(0, H_A // 2 - 1, body, scores(0, 0))
    m_odd = scores(H_A - 1, 1)
    values(H_A - 2, 0, m_even)
    values(H_A - 1, 1, m_odd)
    o_ref[0] = (ot_s[...].T * sza_ref[0].astype(F32)).astype(BF16)


def _attn_call(q, kc, kl, vtc, vtl, sza, *, tq):
    b, _, t, _ = q.shape
    n_ctx = kc.shape[2]
    head_blk = lambda n, w: pl.BlockSpec((1, H_A, n, w), lambda i, j: (i, 0, 0, 0))
    return pl.pallas_call(
        _attn_kernel,
        grid=(b, t // tq),
        in_specs=[pl.BlockSpec((1, H_A, tq, SLAB), lambda i, j: (i, 0, j, 0)),
                  head_blk(n_ctx, SLAB), head_blk(t, SLAB),
                  head_blk(DV_EXT, n_ctx), head_blk(DV_EXT, t),
                  pl.BlockSpec((1, tq, W_A), lambda i, j: (i, j, 0))],
        out_specs=pl.BlockSpec((1, tq, W_A), lambda i, j: (i, j, 0)),
        out_shape=jax.ShapeDtypeStruct((b, t, W_A), BF16),
        scratch_shapes=[pltpu.VMEM((2, n_ctx + t, tq), F32),
                        pltpu.VMEM((W_A, tq), F32)],
        compiler_params=pltpu.CompilerParams(dimension_semantics=("parallel", "parallel"),
                                             vmem_limit_bytes=VMEM_LIMIT),
        name="mla_attention",
    )(q, kc, kl, vtc, vtl, sza)


def _gate_fold_kernel(wq_ref, wk_ref, wv_ref, wg_ref, wgc_ref, wgm_ref):
    wgc_ref[...] = (_dot(wq_ref[...], wg_ref[0:W_B, :])
                    + _dot(wk_ref[...], wg_ref[W_B:2 * W_B, :])).astype(BF16)
    wgm_ref[...] = _dot(wv_ref[...], wg_ref[2 * W_B:3 * W_B, :]).astype(BF16)


def _gate_fold_call(wq, wk, wv, wg):
    shp = jax.ShapeDtypeStruct((W_B, LANES), BF16)
    return pl.pallas_call(_gate_fold_kernel, out_shape=[shp, shp],
                          compiler_params=pltpu.CompilerParams(vmem_limit_bytes=VMEM_LIMIT),
                          name="mlstm_gate_fold")(wq, wk, wv, wg)


_PAD = 8


def _lane_scans(rows_per_chunk, combine, fill, out):
    npc = CHUNK // LANES
    flat = [x[:, i * LANES:(i + 1) * LANES] for x in rows_per_chunk for i in range(npc)]
    lane = lax.broadcasted_iota(jnp.int32, flat[0].shape, 1)
    pre, suf, sh = list(flat), list(flat), 1
    while sh < LANES:
        pre = [combine(p, jnp.where(lane >= sh, pltpu.roll(p, sh, 1), fill)) for p in pre]
        suf = [combine(s, jnp.where(lane < LANES - sh, pltpu.roll(s, LANES - sh, 1), fill)) for s in suf]
        sh *= 2
        yield
    for c in range(len(rows_per_chunk)):
        p, s = pre[c * npc:(c + 1) * npc], suf[c * npc:(c + 1) * npc]
        tot = [x[:, LANES - 1:LANES] for x in p]
        run = None
        for i in range(npc):
            if run is not None:
                p[i] = combine(p[i], run)
            run = tot[i] if run is None else combine(run, tot[i])
        run = None
        for i in reversed(range(npc)):
            if run is not None:
                s[i] = combine(s[i], run)
            run = tot[i] if run is None else combine(run, tot[i])
        out.append((jnp.concatenate(p, axis=1), jnp.concatenate(s, axis=1)))


def _gate_tables(gts, dests):
    L = CHUNK
    fwd_rows = lax.broadcasted_iota(jnp.int32, (N_HD, L), 0) < H_B
    lis = [gt[0:N_HD, :] for gt in gts]
    sums = []
    yield from _lane_scans([_log_sigmoid(gt[N_HD:2 * N_HD, :]) for gt in gts], jnp.add, 0.0, sums)
    cums = [jnp.where(fwd_rows, ps, ss) for ps, ss in sums]
    rs = [li - cum for li, cum in zip(lis, cums)]
    maxs = []
    yield from _lane_scans(rs, jnp.maximum, -jnp.inf, maxs)
    for (grow_ref, c, gcol_ref, r0), cum, r, (pm, sm) in zip(dests, cums, rs, maxs):
        grow_ref[c] = jnp.concatenate([cum, r, jnp.where(fwd_rows, pm, sm)], axis=0)
        gcol_ref[r0:r0 + L, :] = jnp.concatenate([r, jnp.zeros((LANES - N_HD, L), F32)], axis=0).T


def _fill_pad(pad_s, xm_ref, n):
    zrow = jnp.zeros((_PAD, W_B), F32)
    pad_s[0:_PAD, :] = zrow
    pad_s[_PAD:_PAD + n, :] = xm_ref[...].astype(F32)
    pad_s[_PAD + n:2 * _PAD + n, :] = zrow


def _conv_gate_stage(c, pad_s, xm_ref, cw_ref, cb_ref, wgc_ref, wgm_ref, bg_ref, xc_s, xcb_s):
    L = CHUNK
    rows = slice(c * L, (c + 1) * L)
    r0 = c * L + _PAD
    row_id = lax.broadcasted_iota(jnp.int32, (L, W_B), 0)
    xcur = pad_s[r0:r0 + L, :]
    xprev = jnp.where(row_id == 0, pad_s[r0 - 1:r0, :], pltpu.roll(xcur, 1, 0))
    xnext = jnp.where(row_id == L - 1, pad_s[r0 + L:r0 + L + 1, :], pltpu.roll(xcur, L - 1, 0))
    pre = cb_ref[...] + xprev * cw_ref[0:1, :] + xcur * cw_ref[1:2, :] + xnext * cw_ref[2:3, :]
    xc = _silu(pre)
    if xc_s is not None:
        xc_s[rows, :] = xc
    xcb = xc.astype(BF16)
    xcb_s[rows, :] = xcb
    g = _dot(xcb, wgc_ref[...]) + _dot(xm_ref[rows, :], wgm_ref[...]) + bg_ref[...]
    return g.T[0:2 * N_HD, :]


def _headwise_stage(c, xm_ref, xcb_s, wq_ref, wk_ref, wv_ref, q_s, k_s, vt_s):
    L = CHUNK
    rows = slice(c * L, (c + 1) * L)
    for hf in range(W_B // MXU_TILE):
        cs = slice(hf * MXU_TILE, (hf + 1) * MXU_TILE)
        xcb = xcb_s[rows, cs]
        if q_s is not None:
            q_s[rows, cs] = _dot(xcb, wq_ref[cs, cs]).astype(BF16)
        k_s[rows, cs] = _dot(xcb, wk_ref[cs, cs]).astype(BF16)
        vt_s[c, cs, :] = _dot(xm_ref[rows, cs], wv_ref[cs, cs]).T.astype(BF16)


_TAB_ROWS = 5 * N_HD


def _chain_tables(growc_s, grow_s, tab_s, nc):
    L = CHUNK
    fwd_rows = lax.broadcasted_iota(jnp.int32, (N_HD, L), 0) < H_B
    gc = growc_s[0]
    steps = [(gc, gc)] + [(grow_s[i], grow_s[nc - 1 - i]) for i in range(nc)]
    parts = []
    for g_f, g_b in steps:
        pick = lambda a: jnp.where(fwd_rows, g_f[a * N_HD:(a + 1) * N_HD, :], g_b[a * N_HD:(a + 1) * N_HD, :])
        cum, r, pm = pick(0), pick(1), pick(2)
        b_end = jnp.where(fwd_rows[:, 0:1], cum[:, L - 1:L], cum[:, 0:1])
        w = r + b_end
        parts.append((cum, pm, b_end, w, jnp.max(w, axis=1, keepdims=True)))
    m0 = jnp.zeros((N_HD, 1), F32)
    for idx, (cum, pm, b_end, w, wmax) in enumerate(parts):
        m_new = jnp.maximum(b_end + m0, wmax)
        mu = jnp.maximum(m0, pm)
        decay = jnp.broadcast_to(jnp.exp(b_end + m0 - m_new), (N_HD, L))
        tab_s[idx] = jnp.concatenate([mu, jnp.exp(m0 - mu), jnp.exp(-(cum + mu)),
                                      jnp.exp(w - m_new), decay], axis=0)
        m0 = m_new


def _tab(tab, a, hd):
    return tab[a * N_HD + hd:a * N_HD + hd + 1, :]


def _state_update(st_ref, hd, vt_ext, k_c, ws_row, decay_hd):
    vw = (vt_ext.astype(F32) * ws_row).astype(BF16)
    st_ref[hd] = decay_hd * st_ref[hd] + _dot(vw, k_c)


def _mlstm_kernel(xm_ref, xmc_ref, som_ref, szm_ref, cw_ref, cb_ref, wq_ref, wk_ref, wv_ref,
                  wgc_ref, wgm_ref, bg_ref, mhg_ref, skip_ref, o_ref,
                  pad_s, xc_s, xcb_s, q_s, k_s, vt_s, gcol_s, grow_s,
                  padc_s, xcbc_s, kc_s, vtc_s, gcolc_s, growc_s, tab_s, st_s, hf_s, hb_s):
    L = CHUNK
    t = xm_ref.shape[1]
    nc = t // L
    row_i = lax.broadcasted_iota(jnp.int32, (L, L), 0)
    col_i = lax.broadcasted_iota(jnp.int32, (L, L), 1)
    tri = (row_i <= col_i, row_i >= col_i)
    ones_rows = (lax.broadcasted_iota(jnp.int32, (ST_ROWS - DH_B, L), 0) == 0).astype(BF16)
    hsl = lambda hh: slice(hh * DH_B, (hh + 1) * DH_B)
    xm, xmc = xm_ref.at[0], xmc_ref.at[0]
    gate_w = (cw_ref, cb_ref, wgc_ref, wgm_ref, bg_ref)

    _fill_pad(padc_s, xmc, CTX_LEN)
    _fill_pad(pad_s, xm, t)
    gts = [_conv_gate_stage(0, padc_s, xmc, *gate_w, None, xcbc_s)]
    gts += [_conv_gate_stage(c, pad_s, xm, *gate_w, xc_s, xcb_s) for c in range(nc)]
    dests = [(growc_s, 0, gcolc_s, 0)] + [(grow_s, c, gcol_s, c * L) for c in range(nc)]
    tables = _gate_tables(gts, dests)
    _headwise_stage(0, xmc, xcbc_s, wq_ref, wk_ref, wv_ref, None, kc_s, vtc_s)
    for c in range(nc):
        next(tables, None)
        next(tables, None)
        _headwise_stage(c, xm, xcb_s, wq_ref, wk_ref, wv_ref, q_s, k_s, vt_s)
    for _ in tables:
        pass

    _chain_tables(growc_s, grow_s, tab_s, nc)
    st_s[...] = jnp.zeros(st_s.shape, F32)
    tab = tab_s[0]
    for hd in range(N_HD):
        hh = hd % H_B
        vt_ext = jnp.concatenate([vtc_s[0, hsl(hh), :], ones_rows], axis=0)
        _state_update(st_s, hd, vt_ext, kc_s[:, hsl(hh)], _tab(tab, 3, hd), _tab(tab, 4, hd)[:, 0:DH_B])

    def body(i, carry):
        ci = (i, nc - 1 - i)
        rows = tuple(pl.ds(pl.multiple_of(c * L, L), L) for c in ci)
        tab = tab_s[i + 1]
        gcols = tuple(gcol_s[r, :] for r in rows)
        live = {}

        def stage_scores(hd):
            d, hh = hd // H_B, hd % H_B
            k_c = k_s[rows[d], hsl(hh)]
            vt_ext = jnp.concatenate([vt_s[ci[d], hsl(hh), :], ones_rows], axis=0)
            lhs = jnp.concatenate([k_c, st_s[hd].astype(BF16)], axis=0)
            live[hd] = (k_c, vt_ext, _dot_nt(lhs, q_s[rows[d], hsl(hh)]))

        def stage_gate(hd):
            d = hd // H_B
            k_c, vt_ext, res = live[hd]
            rcol = gcols[d][:, hd:hd + 1]
            e = jnp.exp(jnp.where(tri[d], rcol - _tab(tab, 0, hd), -jnp.inf))
            p = (res[0:L, :] * e).astype(BF16)
            _state_update(st_s, hd, vt_ext, k_c, _tab(tab, 3, hd), _tab(tab, 4, hd)[:, 0:DH_B])
            live[hd] = (vt_ext, res[L:L + ST_ROWS, :], p)

        def stage_out(hd):
            d, hh = hd // H_B, hd % H_B
            vt_ext, inter, p = live.pop(hd)
            tot = _tab(tab, 1, hd) * inter + _dot(vt_ext, p)
            den = tot[DH_B:DH_B + 1, :]
            h_t = tot[0:DH_B, :] / jnp.maximum(jnp.abs(den), _tab(tab, 2, hd))
            (hf_s if d == 0 else hb_s)[ci[d], hsl(hh), :] = h_t

        stages = (stage_scores, stage_gate, stage_out)
        for step in range(N_HD + len(stages) - 1):
            for si in reversed(range(len(stages))):
                if 0 <= step - si < N_HD:
                    stages[si](step - si)
        return carry

    lax.fori_loop(0, nc, body, 0)

    ar = lax.broadcasted_iota(jnp.int32, (MXU_TILE, MXU_TILE), 0) // DH_B
    ac = lax.broadcasted_iota(jnp.int32, (MXU_TILE, MXU_TILE), 1) // DH_B
    avg = jnp.where(ar == ac, 1.0 / DH_B, 0.0).astype(BF16)

    def head_mean(v):
        vb = v.astype(BF16)
        return jnp.concatenate([_dot(vb[:, i * MXU_TILE:(i + 1) * MXU_TILE], avg)
                                for i in range(W_B // MXU_TILE)], axis=1)

    def epilogue(c, carry):
        rows = pl.ds(pl.multiple_of(c * L, L), L)
        hsum = (hf_s[c] + hb_s[c]).T * som_ref[0, rows, :].astype(F32)
        cen = hsum - head_mean(hsum)
        hn = cen * lax.rsqrt(head_mean(cen * cen) + LN_EPS)
        hb = hn * mhg_ref[...] + skip_ref[...] * xc_s[rows, :]
        o_ref[0, rows, :] = (hb * szm_ref[0, rows, :].astype(F32)).astype(BF16)
        return carry

    lax.fori_loop(0, nc, epilogue, 0)


def _mlstm_call(xm, xmc, som, szm, conv_w, conv_b, wq, wk, wv, wgc, wgm, bg, mhg, skip):
    b, t, _ = xm.shape
    nc = t // CHUNK
    seq = lambda n: pl.BlockSpec((1, n, W_B), lambda i: (i, 0, 0))
    const = lambda shape: pl.BlockSpec(shape, lambda i: (0,) * len(shape))
    scratch = [
        pltpu.VMEM((t + 2 * _PAD, W_B), F32),
        pltpu.VMEM((t, W_B), F32),
        pltpu.VMEM((t, W_B), BF16),
        pltpu.VMEM((t, W_B), BF16),
        pltpu.VMEM((t, W_B), BF16),
        pltpu.VMEM((nc, W_B, CHUNK), BF16),
        pltpu.VMEM((t, LANES), F32),
        pltpu.VMEM((nc, 3 * N_HD, CHUNK), F32),
        pltpu.VMEM((CTX_LEN + 2 * _PAD, W_B), F32),
        pltpu.VMEM((CTX_LEN, W_B), BF16),
        pltpu.VMEM((CTX_LEN, W_B), BF16),
        pltpu.VMEM((1, W_B, CHUNK), BF16),
        pltpu.VMEM((CTX_LEN, LANES), F32),
        pltpu.VMEM((1, 3 * N_HD, CHUNK), F32),
        pltpu.VMEM((nc + 1, _TAB_ROWS, CHUNK), F32),
        pltpu.VMEM((N_HD, ST_ROWS, DH_B), F32),
        pltpu.VMEM((nc, W_B, CHUNK), F32),
        pltpu.VMEM((nc, W_B, CHUNK), F32),
    ]
    return pl.pallas_call(
        _mlstm_kernel,
        grid=(b,),
        in_specs=[seq(t), seq(CTX_LEN), seq(t), seq(t), const((3, W_B)), const((1, W_B)),
                  const((W_B, W_B)), const((W_B, W_B)), const((W_B, W_B)),
                  const((W_B, LANES)), const((W_B, LANES)), const((1, LANES)),
                  const((1, W_B)), const((1, W_B))],
        out_specs=seq(t),
        out_shape=jax.ShapeDtypeStruct((b, t, W_B), BF16),
        scratch_shapes=scratch,
        compiler_params=pltpu.CompilerParams(dimension_semantics=("arbitrary",),
                                             vmem_limit_bytes=VMEM_LIMIT),
        name="mlstm_bidir",
    )(xm, xmc, som, szm, conv_w, conv_b, wq, wk, wv, wgc, wgm, bg, mhg, skip)


def _out_kernel(h_ref, mod_ref, ya_ref, yb_ref, wo_ref, g_ref, b_ref, o_ref, *, n_sub):
    sub = h_ref.shape[1] // n_sub
    gate = mod_ref[0][:, 2 * D_MODEL:3 * D_MODEL]

    def mix_stage(s):
        rows = slice(s * sub, (s + 1) * sub)
        return (_dot(ya_ref[0, rows, :], wo_ref[0:W_A, :])
                + _dot(yb_ref[0, rows, :], wo_ref[W_A:W_A + W_B, :]))

    def norm_stage(s, y):
        rows = slice(s * sub, (s + 1) * sub)
        o_ref[0, rows, :] = _layer_norm(ALPHA * h_ref[0, rows, :] + gate * y) * g_ref[...] + b_ref[...]

    y = mix_stage(0)
    for s in range(n_sub):
        y_next = mix_stage(s + 1) if s + 1 < n_sub else None
        norm_stage(s, y)
        y = y_next


def _out_call(h, mod3, ya, yb, wo, g, bb, *, tm, n_sub):
    b, t, _ = h.shape
    const = lambda shape: pl.BlockSpec(shape, lambda i, j: (0,) * len(shape))
    row = lambda w: pl.BlockSpec((1, tm, w), lambda i, j: (i, j, 0))
    return pl.pallas_call(
        functools.partial(_out_kernel, n_sub=n_sub),
        grid=(b, t // tm),
        in_specs=[row(D_MODEL), pl.BlockSpec((1, 1, 3 * D_MODEL), lambda i, j: (i, 0, 0)),
                  row(W_A), row(W_B),
                  const((W_A + W_B, D_MODEL)), const((1, D_MODEL)), const((1, D_MODEL))],
        out_specs=row(D_MODEL),
        out_shape=jax.ShapeDtypeStruct((b, t, D_MODEL), F32),
        compiler_params=pltpu.CompilerParams(dimension_semantics=("parallel", "parallel"),
                                             vmem_limit_bytes=VMEM_LIMIT),
        name="out_proj_ln",
    )(h, mod3, ya, yb, wo, g, bb)


def _rot_partner(w):
    w4 = w.reshape(w.shape[:-1] + (2, 2, ROPE_FREQS))
    return jnp.stack([-w4[..., 1, :], w4[..., 0, :]], axis=-2).reshape(w.shape)


def _rope_tables(seq, scale_keep, scale_rope):
    n_rows = seq // GRID_W
    rowp = np.repeat(np.arange(n_rows, dtype=np.float32), GRID_W)
    colp = np.tile(np.arange(GRID_W, dtype=np.float32), n_rows)
    inv = (np.float32(ROPE_BASE) ** (-np.arange(ROPE_FREQS, dtype=np.float32) / np.float32(ROPE_FREQS)))
    ang = np.stack([rowp[:, None] * inv, colp[:, None] * inv], axis=1).astype(np.float32)
    cos = np.broadcast_to(np.cos(ang)[:, :, None, :], (seq, 2, 2, ROPE_FREQS)).reshape(seq, DR_A)
    sin = np.broadcast_to(np.sin(ang)[:, :, None, :], (seq, 2, 2, ROPE_FREQS)).reshape(seq, DR_A)
    z32 = np.zeros((seq, DR_A), np.float32)
    t1 = np.concatenate([np.full((seq, DN_A), scale_keep, np.float32), z32, cos * scale_rope], axis=1)
    t2 = np.concatenate([np.zeros((seq, DN_A), np.float32), z32, sin * scale_rope], axis=1)
    return jnp.asarray(t1, F32), jnp.asarray(t2, F32)


def _block_diag(w):
    nb, bs, _ = w.shape
    n = nb * bs
    rows = w.reshape(n, bs)
    ri = lax.broadcasted_iota(jnp.int32, (n, n), 0)
    ci = lax.broadcasted_iota(jnp.int32, (n, n), 1)
    out = jnp.zeros((n, n), w.dtype)
    for o in range(bs):
        out = out + jnp.where((ri // bs == ci // bs) & (ci % bs == o), rows[:, o:o + 1], 0.0)
    return out


def kernel(x, c, ctx, c_ctx, ln_in_g, ln_in_b, w_ada, b_ada, w_in, g_qa, w_qb, g_kva, w_kvb, conv_w, conv_b, w_mq, w_mk, w_mv, w_gate, b_gate, mh_g, skip, w_out, ln_g, ln_b):
    b, t, _ = x.shape
    l = 0
    r2 = lambda v: v.reshape(1, -1)

    wi = w_in[l]
    s_qa, s_kva, s_kr, s_za, s_xm, s_om = 256, 384, 416, 928, 1440, 1952
    w_kr = wi[:, s_kva:s_kr]
    krblk = jnp.concatenate([jnp.zeros((D_MODEL, DN_A), F32), _rot_partner(w_kr), w_kr], axis=1)
    wina = jnp.concatenate([wi[:, 0:s_kva], krblk], axis=1).astype(BF16)
    winr = wi[:, s_kr:].astype(BF16)
    wq3 = w_qb[l].reshape(Q_LORA, H_A, DN_A + DR_A)
    wq_r = wq3[..., DN_A:]
    wq = jnp.concatenate([wq3[..., :DN_A], _rot_partner(wq_r), wq_r], axis=-1)
    wq = wq.reshape(Q_LORA, H_A * SLAB).astype(BF16)
    wkv3 = w_kvb[l].reshape(KV_LORA, H_A, DN_A + DV_A)
    wkn = jnp.concatenate([wkv3[..., :DN_A], jnp.zeros((KV_LORA, H_A, SLAB - DN_A), F32)], axis=-1)
    wkn = wkn.reshape(KV_LORA, H_A * SLAB).astype(BF16)
    wvt = wkv3[..., DN_A:].reshape(KV_LORA, W_A).T.astype(BF16)
    wmq = _block_diag(w_mq[l])
    wmk = _block_diag(w_mk[l])
    wmv = _block_diag(w_mv[l])
    perm = jnp.array([0, 1, 2, 3, 8, 9, 10, 11, 4, 5, 6, 7, 12, 13, 14, 15])
    wg = jnp.pad(w_gate[l][:, perm], ((0, 0), (0, LANES - 2 * N_HD))).astype(BF16)
    bg = jnp.pad(b_gate[l][perm], (0, LANES - 2 * N_HD)).reshape(1, LANES)
    wgc, wgm = _gate_fold_call(wmq.astype(BF16), wmk.astype(BF16), wmv.astype(BF16), wg)
    wmk_s = (wmk * (DH_B ** -0.5)).astype(BF16)
    wo = w_out[l].astype(BF16)

    sm_scale = (DN_A + DR_A) ** -0.5 * LOG2_E
    t1q, t2q = _rope_tables(t, sm_scale, sm_scale)
    t1k, t2k = _rope_tables(t, 0.0, 1.0)
    n_ctx = ctx.shape[1]
    t1kc = jnp.asarray(np.concatenate([np.zeros((n_ctx, DN_A + DR_A), np.float32),
                                       np.ones((n_ctx, DR_A), np.float32)], axis=1))
    t2kc = jnp.asarray(np.zeros((n_ctx, SLAB), np.float32))

    cc = jnp.concatenate([c, c_ctx[None, :], jnp.zeros((16 - b - 1, D_MODEL), F32)], axis=0)
    mod = _ada_call(cc, w_ada[l], r2(b_ada[l]))
    mod3 = mod.reshape(16, 1, 3 * D_MODEL)

    lng, lnb = r2(ln_in_g), r2(ln_in_b)
    shared = (wina, winr, r2(g_qa[l]), wq, r2(g_kva[l]), wkn, wvt)
    q, kl, vtl, xm, sza, som, szm, h = _proj_call(x, mod3, None, lng, lnb, *shared, t1q, t2q, t1k, t2k,
                                                  tm=1024, n_sub=2, latent=True)
    kc, vtc, xmc = _proj_call(ctx, mod3, b, lng, lnb, *shared, t1kc, t2kc, t1kc, t2kc,
                              tm=n_ctx, n_sub=1, latent=False)

    ya = _attn_call(q, kc, kl, vtc, vtl, sza, tq=512)
    yb = _mlstm_call(xm, xmc, som, szm, conv_w[l], r2(conv_b[l]), wmq.astype(BF16), wmk_s,
                     wmv.astype(BF16), wgc, wgm, bg, r2(mh_g[l]), r2(skip[l]))
    return _out_call(h, mod3, ya, yb, wo, r2(ln_g[l]), r2(ln_b[l]), tm=1024, n_sub=4)
```

```python
import functools

import numpy as np
import jax
import jax.numpy as jnp
from jax import lax
from jax.experimental import pallas as pl
from jax.experimental.pallas import tpu as pltpu

F32 = jnp.float32
BF16 = jnp.bfloat16

D_MODEL = 1024
CTX_LEN = 256
GRID_W = 64
H_A, DN_A, DR_A, DV_A = 8, 64, 32, 64
W_A = H_A * DV_A
Q_LORA, KV_LORA = 256, 128
ROPE_FREQS = DR_A // 4
ROPE_BASE = 10000.0
H_B, DH_B = 4, 128
W_B = H_B * DH_B
QKV_BS = 4
DEPTH = 1
ALPHA = (2.0 * DEPTH) ** 0.25
LN_EPS = 1e-5
RMS_EPS = 1e-6
LOG2_E = 1.4426950408889634

LANES = 128
MXU_TILE = 256
SLAB = LANES
DV_EXT = DV_A + 16
ATTN_KEY_CHUNK = MXU_TILE
CHUNK = 256
N_HD = 2 * H_B
ST_ROWS = DH_B + 16
VMEM_LIMIT = 56 * 1024 * 1024

_NT = (((1,), (1,)), ((), ()))


def _dot(a, b):
    return jnp.dot(a, b, preferred_element_type=F32)


def _dot_nt(a, b):
    return lax.dot_general(a, b, _NT, preferred_element_type=F32)


def _layer_norm(x):
    mu = jnp.mean(x, axis=-1, keepdims=True)
    xc = x - mu
    var = jnp.mean(xc * xc, axis=-1, keepdims=True)
    return xc * lax.rsqrt(var + LN_EPS)


def _rms_norm(x, g):
    return (x * lax.rsqrt(jnp.mean(x * x, axis=-1, keepdims=True) + RMS_EPS)) * g


def _silu(x):
    return x * jax.nn.sigmoid(x)


def _log_sigmoid(x):
    return jnp.minimum(x, 0.0) - jnp.log1p(jnp.exp(-jnp.abs(x)))


def _ada_kernel(cc_ref, w_ref, b_ref, o_ref):
    a = _silu(cc_ref[...]).astype(BF16)
    o_ref[...] = _dot(a, w_ref[...].astype(BF16)) + b_ref[...]


def _ada_call(cc, w_ada, b_ada):
    n = w_ada.shape[1]
    tn = 1024
    return pl.pallas_call(
        _ada_kernel,
        grid=(n // tn,),
        in_specs=[pl.BlockSpec((16, D_MODEL), lambda j: (0, 0)),
                  pl.BlockSpec((D_MODEL, tn), lambda j: (0, j)),
                  pl.BlockSpec((1, tn), lambda j: (0, j))],
        out_specs=pl.BlockSpec((16, tn), lambda j: (0, j)),
        out_shape=jax.ShapeDtypeStruct((16, n), F32),
        compiler_params=pltpu.CompilerParams(dimension_semantics=("arbitrary",),
                                             vmem_limit_bytes=VMEM_LIMIT),
        name="ada_mod",
    )(cc, w_ada, b_ada)


_S_KVA, _S_KR = Q_LORA + KV_LORA, Q_LORA + KV_LORA + DR_A


def _win_kernel(wt_ref, wina_ref, winr_ref):
    n_r = winr_ref.shape[1]
    for j in range(_S_KVA // LANES):
        wina_ref[:, j * LANES:(j + 1) * LANES] = wt_ref[j * LANES:(j + 1) * LANES, :].T.astype(BF16)
    kr = wt_ref[_S_KVA:_S_KR, :]
    a0, b0, a1, b1 = (kr[i * ROPE_FREQS:(i + 1) * ROPE_FREQS, :] for i in range(4))
    blk = jnp.concatenate([jnp.zeros((DN_A, D_MODEL), F32), -b0, a0, -b1, a1, kr], axis=0)
    wina_ref[:, _S_KVA:_S_KVA + LANES] = blk.T.astype(BF16)
    for j in range(n_r // MXU_TILE):
        rows = slice(_S_KR + j * MXU_TILE, _S_KR + (j + 1) * MXU_TILE)
        winr_ref[:, j * MXU_TILE:(j + 1) * MXU_TILE] = wt_ref[rows, :].T.astype(BF16)


def _win_call(wt):
    n_in, d = wt.shape
    return pl.pallas_call(
        _win_kernel,
        out_shape=[jax.ShapeDtypeStruct((d, _S_KVA + LANES), BF16),
                   jax.ShapeDtypeStruct((d, n_in - _S_KR), BF16)],
        compiler_params=pltpu.CompilerParams(vmem_limit_bytes=VMEM_LIMIT),
        name="win_relayout",
    )(wt)


def _rope_slab(s, t1, t2):
    return s * t1 + pltpu.roll(s, 32, 1) * t2


def _proj_kernel(x_ref, mod_ref, lng_ref, lnb_ref, wina_ref, winr_ref, gqa_ref, wq_ref, gkva_ref,
                 wkn_ref, wvt_ref, t1q_ref, t2q_ref, t1k_ref, t2k_ref, *out_refs, latent, n_sub):
    if latent:
        q_ref, k_ref, vt_ref, xm_ref, sza_ref, som_ref, szm_ref, h_ref = out_refs
    else:
        k_ref, vt_ref, xm_ref = out_refs
    sub = x_ref.shape[1] // n_sub
    mod = mod_ref[0]
    shift, scale1 = mod[:, 0:D_MODEL], 1.0 + mod[:, D_MODEL:2 * D_MODEL]
    ones_rows = (lax.broadcasted_iota(jnp.int32, (DV_EXT - DV_A, sub), 0) == 0).astype(BF16)

    def norm_stage(s):
        rows = slice(s * sub, (s + 1) * sub)
        h = _layer_norm(x_ref[0, rows, :]) * lng_ref[...] + lnb_ref[...]
        if latent:
            h_ref[0, rows, :] = h
        return (h * scale1 + shift).astype(BF16)

    def proj_stage(s, u):
        rows = slice(s * sub, (s + 1) * sub)
        p0 = _dot(u, wina_ref[...])
        kvn = _rms_norm(p0[:, 256:384], gkva_ref[...]).astype(BF16)
        kn = _dot(kvn, wkn_ref[...])
        krr = _rope_slab(p0[:, 384:512], t1k_ref[rows, :], t2k_ref[rows, :])
        for hh in range(H_A):
            k_ref[0, hh, rows, :] = (kn[:, hh * SLAB:(hh + 1) * SLAB] + krr).astype(BF16)
        vt = _dot_nt(wvt_ref[...], kvn).astype(BF16)
        for hh in range(H_A):
            vt_ref[0, hh, 0:DV_A, rows] = vt[hh * DV_A:(hh + 1) * DV_A, :]
            vt_ref[0, hh, DV_A:DV_EXT, rows] = ones_rows
        xm_ref[0, rows, :] = _dot(u, winr_ref[:, 512:1024]).astype(BF16)
        if latent:
            qn = _rms_norm(p0[:, 0:256], gqa_ref[...]).astype(BF16)
            qs = _dot(qn, wq_ref[...])
            t1q, t2q = t1q_ref[rows, :], t2q_ref[rows, :]
            for hh in range(H_A):
                q_ref[0, hh, rows, :] = _rope_slab(qs[:, hh * SLAB:(hh + 1) * SLAB], t1q, t2q).astype(BF16)
            sza_ref[0, rows, :] = _silu(_dot(u, winr_ref[:, 0:512])).astype(BF16)
            som_ref[0, rows, :] = jax.nn.sigmoid(_dot(u, winr_ref[:, 1024:1536])).astype(BF16)
            szm_ref[0, rows, :] = _silu(_dot(u, winr_ref[:, 1536:2048])).astype(BF16)

    u = norm_stage(0)
    for s in range(n_sub):
        u_next = norm_stage(s + 1) if s + 1 < n_sub else None
        proj_stage(s, u)
        u = u_next


def _proj_call(x, mod3, mod_row, lng, lnb, wina, winr, gqa, wq, gkva, wkn, wvt, t1q, t2q, t1k, t2k, *,
               tm, n_sub, latent):
    b, t, _ = x.shape
    const = lambda shape: pl.BlockSpec(shape, lambda i, j: (0,) * len(shape))
    tab = pl.BlockSpec((tm, SLAB), lambda i, j: (j, 0))
    row = lambda w: pl.BlockSpec((1, tm, w), lambda i, j: (i, j, 0))
    if mod_row is None:
        mod_spec = pl.BlockSpec((1, 1, 3 * D_MODEL), lambda i, j: (i, 0, 0))
    else:
        mod_spec = pl.BlockSpec((1, 1, 3 * D_MODEL), lambda i, j: (mod_row, 0, 0))
    in_specs = [row(D_MODEL), mod_spec, const((1, D_MODEL)), const((1, D_MODEL)),
                const(wina.shape), const(winr.shape), const((1, Q_LORA)), const((Q_LORA, H_A * SLAB)),
                const((1, KV_LORA)), const((KV_LORA, H_A * SLAB)), const((W_A, KV_LORA)),
                tab, tab, tab, tab]
    k_spec = pl.BlockSpec((1, H_A, tm, SLAB), lambda i, j: (i, 0, j, 0))
    vt_spec = pl.BlockSpec((1, H_A, DV_EXT, tm), lambda i, j: (i, 0, 0, j))
    k_shape = jax.ShapeDtypeStruct((b, H_A, t, SLAB), BF16)
    vt_shape = jax.ShapeDtypeStruct((b, H_A, DV_EXT, t), BF16)
    half = jax.ShapeDtypeStruct((b, t, W_B), BF16)
    if latent:
        out_specs = [k_spec, k_spec, vt_spec, row(W_B), row(W_A), row(W_B), row(W_B), row(D_MODEL)]
        out_shape = [k_shape, k_shape, vt_shape, half, half, half, half,
                     jax.ShapeDtypeStruct((b, t, D_MODEL), F32)]
    else:
        out_specs = [k_spec, vt_spec, row(W_B)]
        out_shape = [k_shape, vt_shape, half]
    return pl.pallas_call(
        functools.partial(_proj_kernel, latent=latent, n_sub=n_sub),
        grid=(b, t // tm),
        in_specs=in_specs, out_specs=out_specs, out_shape=out_shape,
        compiler_params=pltpu.CompilerParams(dimension_semantics=("parallel", "parallel"),
                                             vmem_limit_bytes=VMEM_LIMIT),
        name="in_proj_latent" if latent else "in_proj_ctx",
    )(x, mod3, lng, lnb, wina, winr, gqa, wq, gkva, wkn, wvt, t1q, t2q, t1k, t2k)


def _attn_kernel(q_ref, kc_ref, kl_ref, vtc_ref, vtl_ref, sza_ref, o_ref, s_buf, ot_s):
    n_ctx, t = kc_ref.shape[2], kl_ref.shape[2]
    kc = ATTN_KEY_CHUNK

    def scores(h, slot):
        qh = q_ref[0, h]
        sc = _dot_nt(kc_ref[0, h], qh)
        sk = _dot_nt(kl_ref[0, h], qh)
        s_buf[slot, 0:n_ctx, :] = sc
        s_buf[slot, n_ctx:n_ctx + t, :] = sk
        return jnp.maximum(jnp.max(sc, axis=0, keepdims=True), jnp.max(sk, axis=0, keepdims=True))

    def values(h, slot, m):
        chunks = [(vtc_ref, c * kc, c * kc) for c in range(n_ctx // kc)]
        chunks += [(vtl_ref, c * kc, n_ctx + c * kc) for c in range(t // kc)]
        acc = None
        for vref, v0, s0 in chunks:
            p = jnp.exp2(s_buf[slot, s0:s0 + kc, :] - m).astype(BF16)
            d = _dot(vref[0, h, :, v0:v0 + kc], p)
            acc = d if acc is None else acc + d
        rows = pl.ds(pl.multiple_of(h * DV_A, DV_A), DV_A)
        ot_s[rows, :] = acc[0:DV_A, :] / acc[DV_A:DV_A + 1, :]

    def body(i, m_even):
        h = 2 * i
        m_odd = scores(h + 1, 1)
        values(h, 0, m_even)
        m_even = scores(h + 2, 0)
        values(h + 1, 1, m_odd)
        return m_even

    m_even = lax.fori_loop(0, H_A // 2 - 1, body, scores(0, 0))
    m_odd = scores(H_A - 1, 1)
    values(H_A - 2, 0, m_even)
    values(H_A - 1, 1, m_odd)
    o_ref[0] = (ot_s[...].T * sza_ref[0].astype(F32)).astype(BF16)


def _attn_call(q, kc, kl, vtc, vtl, sza, *, tq):
    b, _, t, _ = q.shape
    n_ctx = kc.shape[2]
    head_blk = lambda n, w: pl.BlockSpec((1, H_A, n, w), lambda i, j: (i, 0, 0, 0))
    return pl.pallas_call(
        _attn_kernel,
        grid=(b, t // tq),
        in_specs=[pl.BlockSpec((1, H_A, tq, SLAB), lambda i, j: (i, 0, j, 0)),
                  head_blk(n_ctx, SLAB), head_blk(t, SLAB),
                  head_blk(DV_EXT, n_ctx), head_blk(DV_EXT, t),
                  pl.BlockSpec((1, tq, W_A), lambda i, j: (i, j, 0))],
        out_specs=pl.BlockSpec((1, tq, W_A), lambda i, j: (i, j, 0)),
        out_shape=jax.ShapeDtypeStruct((b, t, W_A), BF16),
        scratch_shapes=[pltpu.VMEM((2, n_ctx + t, tq), F32),
                        pltpu.VMEM((W_A, tq), F32)],
        compiler_params=pltpu.CompilerParams(dimension_semantics=("parallel", "parallel"),
                                             vmem_limit_bytes=VMEM_LIMIT),
        name="mla_attention",
    )(q, kc, kl, vtc, vtl, sza)


def _gate_fold_kernel(wq_ref, wk_ref, wv_ref, wg_ref, wgc_ref, wgm_ref):
    wgc_ref[...] = (_dot(wq_ref[...], wg_ref[0:W_B, :])
                    + _dot(wk_ref[...], wg_ref[W_B:2 * W_B, :])).astype(BF16)
    wgm_ref[...] = _dot(wv_ref[...], wg_ref[2 * W_B:3 * W_B, :]).astype(BF16)


def _gate_fold_call(wq, wk, wv, wg):
    shp = jax.ShapeDtypeStruct((W_B, LANES), BF16)
    return pl.pallas_call(_gate_fold_kernel, out_shape=[shp, shp],
                          compiler_params=pltpu.CompilerParams(vmem_limit_bytes=VMEM_LIMIT),
                          name="mlstm_gate_fold")(wq, wk, wv, wg)


_PAD = 8


def _lane_scans(rows_per_chunk, combine, fill, out):
    npc = CHUNK // LANES
    flat = [x[:, i * LANES:(i + 1) * LANES] for x in rows_per_chunk for i in range(npc)]
    lane = lax.broadcasted_iota(jnp.int32, flat[0].shape, 1)
    pre, suf, sh = list(flat), list(flat), 1
    while sh < LANES:
        pre = [combine(p, jnp.where(lane >= sh, pltpu.roll(p, sh, 1), fill)) for p in pre]
        suf = [combine(s, jnp.where(lane < LANES - sh, pltpu.roll(s, LANES - sh, 1), fill)) for s in suf]
        sh *= 2
        yield
    for c in range(len(rows_per_chunk)):
        p, s = pre[c * npc:(c + 1) * npc], suf[c * npc:(c + 1) * npc]
        tot = [x[:, LANES - 1:LANES] for x in p]
        run = None
        for i in range(npc):
            if run is not None:
                p[i] = combine(p[i], run)
            run = tot[i] if run is None else combine(run, tot[i])
        run = None
        for i in reversed(range(npc)):
            if run is not None:
                s[i] = combine(s[i], run)
            run = tot[i] if run is None else combine(run, tot[i])
        out.append((jnp.concatenate(p, axis=1), jnp.concatenate(s, axis=1)))


def _gate_tables(gts, dests):
    L = CHUNK
    fwd_rows = lax.broadcasted_iota(jnp.int32, (N_HD, L), 0) < H_B
    lis = [gt[0:N_HD, :] for gt in gts]
    sums = []
    yield from _lane_scans([_log_sigmoid(gt[N_HD:2 * N_HD, :]) for gt in gts], jnp.add, 0.0, sums)
    cums = [jnp.where(fwd_rows, ps, ss) for ps, ss in sums]
    rs = [li - cum for li, cum in zip(lis, cums)]
    maxs = []
    yield from _lane_scans(rs, jnp.maximum, -jnp.inf, maxs)
    for (grow_ref, c, gcol_ref, r0), cum, r, (pm, sm) in zip(dests, cums, rs, maxs):
        grow_ref[c] = jnp.concatenate([cum, r, jnp.where(fwd_rows, pm, sm)], axis=0)
        gcol_ref[r0:r0 + L, :] = jnp.concatenate([r, jnp.zeros((LANES - N_HD, L), F32)], axis=0).T


def _fill_pad(pad_s, xm_ref, n):
    zrow = jnp.zeros((_PAD, W_B), F32)
    pad_s[0:_PAD, :] = zrow
    pad_s[_PAD:_PAD + n, :] = xm_ref[...].astype(F32)
    pad_s[_PAD + n:2 * _PAD + n, :] = zrow


def _conv_gate_stage(c, pad_s, xm_ref, cw_ref, cb_ref, wgc_ref, wgm_ref, bg_ref, xc_s, xcb_s):
    L = CHUNK
    rows = slice(c * L, (c + 1) * L)
    r0 = c * L + _PAD
    row_id = lax.broadcasted_iota(jnp.int32, (L, W_B), 0)
    xcur = pad_s[r0:r0 + L, :]
    xprev = jnp.where(row_id == 0, pad_s[r0 - 1:r0, :], pltpu.roll(xcur, 1, 0))
    xnext = jnp.where(row_id == L - 1, pad_s[r0 + L:r0 + L + 1, :], pltpu.roll(xcur, L - 1, 0))
    pre = cb_ref[...] + xprev * cw_ref[0:1, :] + xcur * cw_ref[1:2, :] + xnext * cw_ref[2:3, :]
    xc = _silu(pre)
    if xc_s is not None:
        xc_s[rows, :] = xc
    xcb = xc.astype(BF16)
    xcb_s[rows, :] = xcb
    g = _dot(xcb, wgc_ref[...]) + _dot(xm_ref[rows, :], wgm_ref[...]) + bg_ref[...]
    return g.T[0:2 * N_HD, :]


def _headwise_stage(c, xm_ref, xcb_s, wq_ref, wk_ref, wv_ref, q_s, k_s, vt_s):
    L = CHUNK
    rows = slice(c * L, (c + 1) * L)
    for hf in range(W_B // MXU_TILE):
        cs = slice(hf * MXU_TILE, (hf + 1) * MXU_TILE)
        xcb = xcb_s[rows, cs]
        if q_s is not None:
            q_s[rows, cs] = _dot(xcb, wq_ref[cs, cs]).astype(BF16)
        k_s[rows, cs] = _dot(xcb, wk_ref[cs, cs]).astype(BF16)
        vt_s[c, cs, :] = _dot(xm_ref[rows, cs], wv_ref[cs, cs]).T.astype(BF16)


_TAB_ROWS = 5 * N_HD


def _chain_tables(growc_s, grow_s, tab_s, nc):
    L = CHUNK
    fwd_rows = lax.broadcasted_iota(jnp.int32, (N_HD, L), 0) < H_B
    gc = growc_s[0]
    steps = [(gc, gc)] + [(grow_s[i], grow_s[nc - 1 - i]) for i in range(nc)]
    parts = []
    for g_f, g_b in steps:
        pick = lambda a: jnp.where(fwd_rows, g_f[a * N_HD:(a + 1) * N_HD, :], g_b[a * N_HD:(a + 1) * N_HD, :])
        cum, r, pm = pick(0), pick(1), pick(2)
        b_end = jnp.where(fwd_rows[:, 0:1], cum[:, L - 1:L], cum[:, 0:1])
        w = r + b_end
        parts.append((cum, pm, b_end, w, jnp.max(w, axis=1, keepdims=True)))
    m0 = jnp.zeros((N_HD, 1), F32)
    for idx, (cum, pm, b_end, w, wmax) in enumerate(parts):
        m_new = jnp.maximum(b_end + m0, wmax)
        mu = jnp.maximum(m0, pm)
        decay = jnp.broadcast_to(jnp.exp(b_end + m0 - m_new), (N_HD, L))
        tab_s[idx] = jnp.concatenate([mu, jnp.exp(m0 - mu), jnp.exp(-(cum + mu)),
                                      jnp.exp(w - m_new), decay], axis=0)
        m0 = m_new


def _tab(tab, a, hd):
    return tab[a * N_HD + hd:a * N_HD + hd + 1, :]


def _state_update(st_ref, hd, vt_ext, k_c, ws_row, decay_hd):
    vw = (vt_ext.astype(F32) * ws_row).astype(BF16)
    st_ref[hd] = decay_hd * st_ref[hd] + _dot(vw, k_c)


def _mlstm_kernel(xm_ref, xmc_ref, som_ref, szm_ref, cw_ref, cb_ref, wq_ref, wk_ref, wv_ref,
                  wgc_ref, wgm_ref, bg_ref, mhg_ref, skip_ref, o_ref,
                  pad_s, xc_s, xcb_s, q_s, k_s, vt_s, gcol_s, grow_s,
                  padc_s, xcbc_s, kc_s, vtc_s, gcolc_s, growc_s, tab_s, st_s, hf_s, hb_s):
    L = CHUNK
    t = xm_ref.shape[1]
    nc = t // L
    row_i = lax.broadcasted_iota(jnp.int32, (L, L), 0)
    col_i = lax.broadcasted_iota(jnp.int32, (L, L), 1)
    tri = (row_i <= col_i, row_i >= col_i)
    ones_rows = (lax.broadcasted_iota(jnp.int32, (ST_ROWS - DH_B, L), 0) == 0).astype(BF16)
    hsl = lambda hh: slice(hh * DH_B, (hh + 1) * DH_B)
    xm, xmc = xm_ref.at[0], xmc_ref.at[0]
    gate_w = (cw_ref, cb_ref, wgc_ref, wgm_ref, bg_ref)

    _fill_pad(padc_s, xmc, CTX_LEN)
    _fill_pad(pad_s, xm, t)
    gts = [_conv_gate_stage(0, padc_s, xmc, *gate_w, None, xcbc_s)]
    gts += [_conv_gate_stage(c, pad_s, xm, *gate_w, xc_s, xcb_s) for c in range(nc)]
    dests = [(growc_s, 0, gcolc_s, 0)] + [(grow_s, c, gcol_s, c * L) for c in range(nc)]
    tables = _gate_tables(gts, dests)
    _headwise_stage(0, xmc, xcbc_s, wq_ref, wk_ref, wv_ref, None, kc_s, vtc_s)
    for c in range(nc):
        next(tables, None)
        next(tables, None)
        _headwise_stage(c, xm, xcb_s, wq_ref, wk_ref, wv_ref, q_s, k_s, vt_s)
    for _ in tables:
        pass

    _chain_tables(growc_s, grow_s, tab_s, nc)
    st_s[...] = jnp.zeros(st_s.shape, F32)
    tab = tab_s[0]
    for hd in range(N_HD):
        hh = hd % H_B
        vt_ext = jnp.concatenate([vtc_s[0, hsl(hh), :], ones_rows], axis=0)
        _state_update(st_s, hd, vt_ext, kc_s[:, hsl(hh)], _tab(tab, 3, hd), _tab(tab, 4, hd)[:, 0:DH_B])

    def body(i, carry):
        ci = (i, nc - 1 - i)
        rows = tuple(pl.ds(pl.multiple_of(c * L, L), L) for c in ci)
        tab = tab_s[i + 1]
        gcols = tuple(gcol_s[r, :] for r in rows)
        live = {}

        def stage_scores(hd):
            d, hh = hd // H_B, hd % H_B
            k_c = k_s[rows[d], hsl(hh)]
            vt_ext = jnp.concatenate([vt_s[ci[d], hsl(hh), :], ones_rows], axis=0)
            lhs = jnp.concatenate([k_c, st_s[hd].astype(BF16)], axis=0)
            live[hd] = (k_c, vt_ext, _dot_nt(lhs, q_s[rows[d], hsl(hh)]))

        def stage_gate(hd):
            d = hd // H_B
            k_c, vt_ext, res = live[hd]
            rcol = gcols[d][:, hd:hd + 1]
            e = jnp.exp(jnp.where(tri[d], rcol - _tab(tab, 0, hd), -jnp.inf))
            p = (res[0:L, :] * e).astype(BF16)
            _state_update(st_s, hd, vt_ext, k_c, _tab(tab, 3, hd), _tab(tab, 4, hd)[:, 0:DH_B])
            live[hd] = (vt_ext, res[L:L + ST_ROWS, :], p)

        def stage_out(hd):
            d, hh = hd // H_B, hd % H_B
            vt_ext, inter, p = live.pop(hd)
            tot = _tab(tab, 1, hd) * inter + _dot(vt_ext, p)
            den = tot[DH_B:DH_B + 1, :]
            h_t = tot[0:DH_B, :] / jnp.maximum(jnp.abs(den), _tab(tab, 2, hd))
            (hf_s if d == 0 else hb_s)[ci[d], hsl(hh), :] = h_t

        stages = (stage_scores, stage_gate, stage_out)
        for step in range(N_HD + len(stages) - 1):
            for si in reversed(range(len(stages))):
                if 0 <= step - si < N_HD:
                    stages[si](step - si)
        return carry

    lax.fori_loop(0, nc, body, 0)

    ar = lax.broadcasted_iota(jnp.int32, (MXU_TILE, MXU_TILE), 0) // DH_B
    ac = lax.broadcasted_iota(jnp.int32, (MXU_TILE, MXU_TILE), 1) // DH_B
    avg = jnp.where(ar == ac, 1.0 / DH_B, 0.0).astype(BF16)

    def head_mean(v):
        vb = v.astype(BF16)
        return jnp.concatenate([_dot(vb[:, i * MXU_TILE:(i + 1) * MXU_TILE], avg)
                                for i in range(W_B // MXU_TILE)], axis=1)

    def epilogue(c, carry):
        rows = pl.ds(pl.multiple_of(c * L, L), L)
        hsum = (hf_s[c] + hb_s[c]).T * som_ref[0, rows, :].astype(F32)
        cen = hsum - head_mean(hsum)
        hn = cen * lax.rsqrt(head_mean(cen * cen) + LN_EPS)
        hb = hn * mhg_ref[...] + skip_ref[...] * xc_s[rows, :]
        o_ref[0, rows, :] = (hb * szm_ref[0, rows, :].astype(F32)).astype(BF16)
        return carry

    lax.fori_loop(0, nc, epilogue, 0)


def _mlstm_call(xm, xmc, som, szm, conv_w, conv_b, wq, wk, wv, wgc, wgm, bg, mhg, skip):
    b, t, _ = xm.shape
    nc = t // CHUNK
    seq = lambda n: pl.BlockSpec((1, n, W_B), lambda i: (i, 0, 0))
    const = lambda shape: pl.BlockSpec(shape, lambda i: (0,) * len(shape))
    scratch = [
        pltpu.VMEM((t + 2 * _PAD, W_B), F32),
        pltpu.VMEM((t, W_B), F32),
        pltpu.VMEM((t, W_B), BF16),
        pltpu.VMEM((t, W_B), BF16),
        pltpu.VMEM((t, W_B), BF16),
        pltpu.VMEM((nc, W_B, CHUNK), BF16),
        pltpu.VMEM((t, LANES), F32),
        pltpu.VMEM((nc, 3 * N_HD, CHUNK), F32),
        pltpu.VMEM((CTX_LEN + 2 * _PAD, W_B), F32),
        pltpu.VMEM((CTX_LEN, W_B), BF16),
        pltpu.VMEM((CTX_LEN, W_B), BF16),
        pltpu.VMEM((1, W_B, CHUNK), BF16),
        pltpu.VMEM((CTX_LEN, LANES), F32),
        pltpu.VMEM((1, 3 * N_HD, CHUNK), F32),
        pltpu.VMEM((nc + 1, _TAB_ROWS, CHUNK), F32),
        pltpu.VMEM((N_HD, ST_ROWS, DH_B), F32),
        pltpu.VMEM((nc, W_B, CHUNK), F32),
        pltpu.VMEM((nc, W_B, CHUNK), F32),
    ]
    return pl.pallas_call(
        _mlstm_kernel,
        grid=(b,),
        in_specs=[seq(t), seq(CTX_LEN), seq(t), seq(t), const((3, W_B)), const((1, W_B)),
                  const((W_B, W_B)), const((W_B, W_B)), const((W_B, W_B)),
                  const((W_B, LANES)), const((W_B, LANES)), const((1, LANES)),
                  const((1, W_B)), const((1, W_B))],
        out_specs=seq(t),
        out_shape=jax.ShapeDtypeStruct((b, t, W_B), BF16),
        scratch_shapes=scratch,
        compiler_params=pltpu.CompilerParams(dimension_semantics=("arbitrary",),
                                             vmem_limit_bytes=VMEM_LIMIT),
        name="mlstm_bidir",
    )(xm, xmc, som, szm, conv_w, conv_b, wq, wk, wv, wgc, wgm, bg, mhg, skip)


def _out_kernel(h_ref, mod_ref, ya_ref, yb_ref, wo_ref, g_ref, b_ref, o_ref, *, n_sub):
    sub = h_ref.shape[1] // n_sub
    gate = mod_ref[0][:, 2 * D_MODEL:3 * D_MODEL]

    def mix_stage(s):
        rows = slice(s * sub, (s + 1) * sub)
        return (_dot(ya_ref[0, rows, :], wo_ref[0:W_A, :])
                + _dot(yb_ref[0, rows, :], wo_ref[W_A:W_A + W_B, :]))

    def norm_stage(s, y):
        rows = slice(s * sub, (s + 1) * sub)
        o_ref[0, rows, :] = _layer_norm(ALPHA * h_ref[0, rows, :] + gate * y) * g_ref[...] + b_ref[...]

    y = mix_stage(0)
    for s in range(n_sub):
        y_next = mix_stage(s + 1) if s + 1 < n_sub else None
        norm_stage(s, y)
        y = y_next


def _out_call(h, mod3, ya, yb, wo, g, bb, *, tm, n_sub):
    b, t, _ = h.shape
    const = lambda shape: pl.BlockSpec(shape, lambda i, j: (0,) * len(shape))
    row = lambda w: pl.BlockSpec((1, tm, w), lambda i, j: (i, j, 0))
    return pl.pallas_call(
        functools.partial(_out_kernel, n_sub=n_sub),
        grid=(b, t // tm),
        in_specs=[row(D_MODEL), pl.BlockSpec((1, 1, 3 * D_MODEL), lambda i, j: (i, 0, 0)),
                  row(W_A), row(W_B),
                  const((W_A + W_B, D_MODEL)), const((1, D_MODEL)), const((1, D_MODEL))],
        out_specs=row(D_MODEL),
        out_shape=jax.ShapeDtypeStruct((b, t, D_MODEL), F32),
        compiler_params=pltpu.CompilerParams(dimension_semantics=("parallel", "parallel"),
                                             vmem_limit_bytes=VMEM_LIMIT),
        name="out_proj_ln",
    )(h, mod3, ya, yb, wo, g, bb)


def _rot_partner(w):
    w4 = w.reshape(w.shape[:-1] + (2, 2, ROPE_FREQS))
    return jnp.stack([-w4[..., 1, :], w4[..., 0, :]], axis=-2).reshape(w.shape)


def _rope_tables(seq, scale_keep, scale_rope):
    n_rows = seq // GRID_W
    rowp = np.repeat(np.arange(n_rows, dtype=np.float32), GRID_W)
    colp = np.tile(np.arange(GRID_W, dtype=np.float32), n_rows)
    inv = (np.float32(ROPE_BASE) ** (-np.arange(ROPE_FREQS, dtype=np.float32) / np.float32(ROPE_FREQS)))
    ang = np.stack([rowp[:, None] * inv, colp[:, None] * inv], axis=1).astype(np.float32)
    cos = np.broadcast_to(np.cos(ang)[:, :, None, :], (seq, 2, 2, ROPE_FREQS)).reshape(seq, DR_A)
    sin = np.broadcast_to(np.sin(ang)[:, :, None, :], (seq, 2, 2, ROPE_FREQS)).reshape(seq, DR_A)
    z32 = np.zeros((seq, DR_A), np.float32)
    t1 = np.concatenate([np.full((seq, DN_A), scale_keep, np.float32), z32, cos * scale_rope], axis=1)
    t2 = np.concatenate([np.zeros((seq, DN_A), np.float32), z32, sin * scale_rope], axis=1)
    return jnp.asarray(t1, F32), jnp.asarray(t2, F32)


def _block_diag(w):
    nb, bs, _ = w.shape
    n = nb * bs
    rows = w.reshape(n, bs)
    ri = lax.broadcasted_iota(jnp.int32, (n, n), 0)
    ci = lax.broadcasted_iota(jnp.int32, (n, n), 1)
    out = jnp.zeros((n, n), w.dtype)
    for o in range(bs):
        out = out + jnp.where((ri // bs == ci // bs) & (ci % bs == o), rows[:, o:o + 1], 0.0)
    return out


def kernel(x, c, ctx, c_ctx, ln_in_g, ln_in_b, w_ada, b_ada, w_in, g_qa, w_qb, g_kva, w_kvb, conv_w, conv_b, w_mq, w_mk, w_mv, w_gate, b_gate, mh_g, skip, w_out, ln_g, ln_b):
    b, t, _ = x.shape
    l = 0
    r2 = lambda v: v.reshape(1, -1)

    wina, winr = _win_call(jnp.swapaxes(w_in[l], 0, 1))
    wq3 = w_qb[l].reshape(Q_LORA, H_A, DN_A + DR_A)
    wq_r = wq3[..., DN_A:]
    wq = jnp.concatenate([wq3[..., :DN_A], _rot_partner(wq_r), wq_r], axis=-1)
    wq = wq.reshape(Q_LORA, H_A * SLAB).astype(BF16)
    wkv3 = w_kvb[l].reshape(KV_LORA, H_A, DN_A + DV_A)
    wkn = jnp.concatenate([wkv3[..., :DN_A], jnp.zeros((KV_LORA, H_A, SLAB - DN_A), F32)], axis=-1)
    wkn = wkn.reshape(KV_LORA, H_A * SLAB).astype(BF16)
    wvt = wkv3[..., DN_A:].reshape(KV_LORA, W_A).T.astype(BF16)
    wmq = _block_diag(w_mq[l])
    wmk = _block_diag(w_mk[l])
    wmv = _block_diag(w_mv[l])
    perm = jnp.array([0, 1, 2, 3, 8, 9, 10, 11, 4, 5, 6, 7, 12, 13, 14, 15])
    wg = jnp.pad(w_gate[l][:, perm], ((0, 0), (0, LANES - 2 * N_HD))).astype(BF16)
    bg = jnp.pad(b_gate[l][perm], (0, LANES - 2 * N_HD)).reshape(1, LANES)
    wgc, wgm = _gate_fold_call(wmq.astype(BF16), wmk.astype(BF16), wmv.astype(BF16), wg)
    wmk_s = (wmk * (DH_B ** -0.5)).astype(BF16)
    wo = w_out[l].astype(BF16)

    sm_scale = (DN_A + DR_A) ** -0.5 * LOG2_E
    t1q, t2q = _rope_tables(t, sm_scale, sm_scale)
    t1k, t2k = _rope_tables(t, 0.0, 1.0)
    n_ctx = ctx.shape[1]
    t1kc = jnp.asarray(np.concatenate([np.zeros((n_ctx, DN_A + DR_A), np.float32),
                                       np.ones((n_ctx, DR_A), np.float32)], axis=1))
    t2kc = jnp.asarray(np.zeros((n_ctx, SLAB), np.float32))

    cc = jnp.concatenate([c, c_ctx[None, :], jnp.zeros((16 - b - 1, D_MODEL), F32)], axis=0)
    mod = _ada_call(cc, w_ada[l], r2(b_ada[l]))
    mod3 = mod.reshape(16, 1, 3 * D_MODEL)

    lng, lnb = r2(ln_in_g), r2(ln_in_b)
    shared = (wina, winr, r2(g_qa[l]), wq, r2(g_kva[l]), wkn, wvt)
    q, kl, vtl, xm, sza, som, szm, h = _proj_call(x, mod3, None, lng, lnb, *shared, t1q, t2q, t1k, t2k,
                                                  tm=1024, n_sub=2, latent=True)
    kc, vtc, xmc = _proj_call(ctx, mod3, b, lng, lnb, *shared, t1kc, t2kc, t1kc, t2kc,
                              tm=n_ctx, n_sub=1, latent=False)

    ya = _attn_call(q, kc, kl, vtc, vtl, sza, tq=512)
    yb = _mlstm_call(xm, xmc, som, szm, conv_w[l], r2(conv_b[l]), wmq.astype(BF16), wmk_s,
                     wmv.astype(BF16), wgc, wgm, bg, r2(mh_g[l]), r2(skip[l]))
    return _out_call(h, mod3, ya, yb, wo, r2(ln_g[l]), r2(ln_b[l]), tm=1024, n_sub=4)
```

```python
import functools

import numpy as np
import jax
import jax.numpy as jnp
from jax import lax
from jax.experimental import pallas as pl
from jax.experimental.pallas import tpu as pltpu

F32 = jnp.float32
BF16 = jnp.bfloat16

D_MODEL = 1024
CTX_LEN = 256
GRID_W = 64
H_A, DN_A, DR_A, DV_A = 8, 64, 32, 64
W_A = H_A * DV_A
Q_LORA, KV_LORA = 256, 128
ROPE_FREQS = DR_A // 4
ROPE_BASE = 10000.0
H_B, DH_B = 4, 128
W_B = H_B * DH_B
QKV_BS = 4
DEPTH = 1
ALPHA = (2.0 * DEPTH) ** 0.25
LN_EPS = 1e-5
RMS_EPS = 1e-6
LOG2_E = 1.4426950408889634

LANES = 128
MXU_TILE = 256
SLAB = LANES
DV_EXT = DV_A + 16
ATTN_KEY_CHUNK = MXU_TILE
CHUNK = 256
N_HD = 2 * H_B
ST_ROWS = DH_B + 16
VMEM_LIMIT = 56 * 1024 * 1024

_NT = (((1,), (1,)), ((), ()))


def _dot(a, b):
    return jnp.dot(a, b, preferred_element_type=F32)


def _dot_nt(a, b):
    return lax.dot_general(a, b, _NT, preferred_element_type=F32)


def _layer_norm(x):
    mu = jnp.mean(x, axis=-1, keepdims=True)
    xc = x - mu
    var = jnp.mean(xc * xc, axis=-1, keepdims=True)
    return xc * lax.rsqrt(var + LN_EPS)


def _rms_norm(x, g):
    return (x * lax.rsqrt(jnp.mean(x * x, axis=-1, keepdims=True) + RMS_EPS)) * g


def _silu(x):
    return x * jax.nn.sigmoid(x)


def _log_sigmoid(x):
    return jnp.minimum(x, 0.0) - jnp.log1p(jnp.exp(-jnp.abs(x)))


def _ada_kernel(cc_ref, w_ref, b_ref, o_ref):
    a = _silu(cc_ref[...]).astype(BF16)
    o_ref[...] = _dot(a, w_ref[...].astype(BF16)) + b_ref[...]


def _ada_call(cc, w_ada, b_ada):
    n = w_ada.shape[1]
    tn = 1024
    return pl.pallas_call(
        _ada_kernel,
        grid=(n // tn,),
        in_specs=[pl.BlockSpec((16, D_MODEL), lambda j: (0, 0)),
                  pl.BlockSpec((D_MODEL, tn), lambda j: (0, j)),
                  pl.BlockSpec((1, tn), lambda j: (0, j))],
        out_specs=pl.BlockSpec((16, tn), lambda j: (0, j)),
        out_shape=jax.ShapeDtypeStruct((16, n), F32),
        compiler_params=pltpu.CompilerParams(dimension_semantics=("arbitrary",),
                                             vmem_limit_bytes=VMEM_LIMIT),
        name="ada_mod",
    )(cc, w_ada, b_ada)


_S_KVA, _S_KR = Q_LORA + KV_LORA, Q_LORA + KV_LORA + DR_A


def _win_kernel(wt_ref, wina_ref, winr_ref):
    n_r = winr_ref.shape[1]
    for j in range(_S_KVA // LANES):
        wina_ref[:, j * LANES:(j + 1) * LANES] = wt_ref[j * LANES:(j + 1) * LANES, :].T.astype(BF16)
    kr = wt_ref[_S_KVA:_S_KR, :]
    a0, b0, a1, b1 = (kr[i * ROPE_FREQS:(i + 1) * ROPE_FREQS, :] for i in range(4))
    blk = jnp.concatenate([jnp.zeros((DN_A, D_MODEL), F32), -b0, a0, -b1, a1, kr], axis=0)
    wina_ref[:, _S_KVA:_S_KVA + LANES] = blk.T.astype(BF16)
    for j in range(n_r // MXU_TILE):
        rows = slice(_S_KR + j * MXU_TILE, _S_KR + (j + 1) * MXU_TILE)
        winr_ref[:, j * MXU_TILE:(j + 1) * MXU_TILE] = wt_ref[rows, :].T.astype(BF16)


def _win_call(wt):
    n_in, d = wt.shape
    return pl.pallas_call(
        _win_kernel,
        out_shape=[jax.ShapeDtypeStruct((d, _S_KVA + LANES), BF16),
                   jax.ShapeDtypeStruct((d, n_in - _S_KR), BF16)],
        compiler_params=pltpu.CompilerParams(vmem_limit_bytes=VMEM_LIMIT),
        name="win_relayout",
    )(wt)


def _rope_slab(s, t1, t2):
    return s * t1 + pltpu.roll(s, 32, 1) * t2


def _proj_kernel(x_ref, mod_ref, lng_ref, lnb_ref, wina_ref, winr_ref, gqa_ref, wq_ref, gkva_ref,
                 wkn_ref, wvt_ref, t1q_ref, t2q_ref, t1k_ref, t2k_ref, *out_refs, latent, n_sub):
    if latent:
        q_ref, k_ref, vt_ref, xm_ref, sza_ref, som_ref, szm_ref, h_ref = out_refs
    else:
        k_ref, vt_ref, xm_ref = out_refs
    sub = x_ref.shape[1] // n_sub
    mod = mod_ref[0]
    shift, scale1 = mod[:, 0:D_MODEL], 1.0 + mod[:, D_MODEL:2 * D_MODEL]
    ones_rows = (lax.broadcasted_iota(jnp.int32, (DV_EXT - DV_A, sub), 0) == 0).astype(BF16)

    def norm_stage(s):
        rows = slice(s * sub, (s + 1) * sub)
        h = _layer_norm(x_ref[0, rows, :]) * lng_ref[...] + lnb_ref[...]
        if latent:
            h_ref[0, rows, :] = h
        return (h * scale1 + shift).astype(BF16)

    def proj_stage(s, u):
        rows = slice(s * sub, (s + 1) * sub)
        p0 = _dot(u, wina_ref[...])
        kvn = _rms_norm(p0[:, 256:384], gkva_ref[...]).astype(BF16)
        kn = _dot(kvn, wkn_ref[...])
        krr = _rope_slab(p0[:, 384:512], t1k_ref[rows, :], t2k_ref[rows, :])
        for hh in range(H_A):
            k_ref[0, hh, rows, :] = (kn[:, hh * SLAB:(hh + 1) * SLAB] + krr).astype(BF16)
        vt = _dot_nt(wvt_ref[...], kvn).astype(BF16)
        for hh in range(H_A):
            vt_ref[0, hh, 0:DV_A, rows] = vt[hh * DV_A:(hh + 1) * DV_A, :]
            vt_ref[0, hh, DV_A:DV_EXT, rows] = ones_rows
        xm_ref[0, rows, :] = _dot(u, winr_ref[:, 512:1024]).astype(BF16)
        if latent:
            qn = _rms_norm(p0[:, 0:256], gqa_ref[...]).astype(BF16)
            qs = _dot(qn, wq_ref[...])
            t1q, t2q = t1q_ref[rows, :], t2q_ref[rows, :]
            for hh in range(H_A):
                q_ref[0, hh, rows, :] = _rope_slab(qs[:, hh * SLAB:(hh + 1) * SLAB], t1q, t2q).astype(BF16)
            sza_ref[0, rows, :] = _silu(_dot(u, winr_ref[:, 0:512])).astype(BF16)
            som_ref[0, rows, :] = jax.nn.sigmoid(_dot(u, winr_ref[:, 1024:1536])).astype(BF16)
            szm_ref[0, rows, :] = _silu(_dot(u, winr_ref[:, 1536:2048])).astype(BF16)

    u = norm_stage(0)
    for s in range(n_sub):
        u_next = norm_stage(s + 1) if s + 1 < n_sub else None
        proj_stage(s, u)
        u = u_next


def _proj_call(x, mod3, mod_row, lng, lnb, wina, winr, gqa, wq, gkva, wkn, wvt, t1q, t2q, t1k, t2k, *,
               tm, n_sub, latent):
    b, t, _ = x.shape
    const = lambda shape: pl.BlockSpec(shape, lambda i, j: (0,) * len(shape))
    tab = pl.BlockSpec((tm, SLAB), lambda i, j: (j, 0))
    row = lambda w: pl.BlockSpec((1, tm, w), lambda i, j: (i, j, 0))
    if mod_row is None:
        mod_spec = pl.BlockSpec((1, 1, 3 * D_MODEL), lambda i, j: (i, 0, 0))
    else:
        mod_spec = pl.BlockSpec((1, 1, 3 * D_MODEL), lambda i, j: (mod_row, 0, 0))
    in_specs = [row(D_MODEL), mod_spec, const((1, D_MODEL)), const((1, D_MODEL)),
                const(wina.shape), const(winr.shape), const((1, Q_LORA)), const((Q_LORA, H_A * SLAB)),
                const((1, KV_LORA)), const((KV_LORA, H_A * SLAB)), const((W_A, KV_LORA)),
                tab, tab, tab, tab]
    k_spec = pl.BlockSpec((1, H_A, tm, SLAB), lambda i, j: (i, 0, j, 0))
    vt_spec = pl.BlockSpec((1, H_A, DV_EXT, tm), lambda i, j: (i, 0, 0, j))
    k_shape = jax.ShapeDtypeStruct((b, H_A, t, SLAB), BF16)
    vt_shape = jax.ShapeDtypeStruct((b, H_A, DV_EXT, t), BF16)
    half = jax.ShapeDtypeStruct((b, t, W_B), BF16)
    if latent:
        out_specs = [k_spec, k_spec, vt_spec, row(W_B), row(W_A), row(W_B), row(W_B), row(D_MODEL)]
        out_shape = [k_shape, k_shape, vt_shape, half, half, half, half,
                     jax.ShapeDtypeStruct((b, t, D_MODEL), F32)]
    else:
        out_specs = [k_spec, vt_spec, row(W_B)]
        out_shape = [k_shape, vt_shape, half]
    return pl.pallas_call(
        functools.partial(_proj_kernel, latent=latent, n_sub=n_sub),
        grid=(b, t // tm),
        in_specs=in_specs, out_specs=out_specs, out_shape=out_shape,
        compiler_params=pltpu.CompilerParams(dimension_semantics=("parallel", "parallel"),
                                             vmem_limit_bytes=VMEM_LIMIT),
        name="in_proj_latent" if latent else "in_proj_ctx",
    )(x, mod3, lng, lnb, wina, winr, gqa, wq, gkva, wkn, wvt, t1q, t2q, t1k, t2k)


def _attn_kernel(q_ref, kc_ref, kl_ref, vtc_ref, vtl_ref, sza_ref, hs_ref, som_ref, szm_ref, xc_ref,
                 mhg_ref, skip_ref, o_ref, yb_ref, s_buf, ot_s):
    n_ctx, t = kc_ref.shape[2], kl_ref.shape[2]
    kc = ATTN_KEY_CHUNK

    def scores(h, slot):
        qh = q_ref[0, h]
        sc = _dot_nt(kc_ref[0, h], qh)
        sk = _dot_nt(kl_ref[0, h], qh)
        s_buf[slot, 0:n_ctx, :] = sc
        s_buf[slot, n_ctx:n_ctx + t, :] = sk
        return jnp.maximum(jnp.max(sc, axis=0, keepdims=True), jnp.max(sk, axis=0, keepdims=True))

    def values(h, slot, m):
        chunks = [(vtc_ref, c * kc, c * kc) for c in range(n_ctx // kc)]
        chunks += [(vtl_ref, c * kc, n_ctx + c * kc) for c in range(t // kc)]
        acc = None
        for vref, v0, s0 in chunks:
            p = jnp.exp2(s_buf[slot, s0:s0 + kc, :] - m).astype(BF16)
            d = _dot(vref[0, h, :, v0:v0 + kc], p)
            acc = d if acc is None else acc + d
        rows = pl.ds(pl.multiple_of(h * DV_A, DV_A), DV_A)
        ot_s[rows, :] = acc[0:DV_A, :] / acc[DV_A:DV_A + 1, :]

    def body(i, m_even):
        h = 2 * i
        m_odd = scores(h + 1, 1)
        values(h, 0, m_even)
        m_even = scores(h + 2, 0)
        values(h + 1, 1, m_odd)
        return m_even

    m_first = scores(0, 0)
    for c in range(hs_ref.shape[1]):
        rows = slice(c * CHUNK, (c + 1) * CHUNK)
        yb_ref[0, rows, :] = _mlstm_out(hs_ref[0, c], som_ref[0, rows, :], szm_ref[0, rows, :],
                                        xc_ref[0, rows, :], mhg_ref[...], skip_ref[...])
    m_even = lax.fori_loop(0, H_A // 2 - 1, body, m_first)
    m_odd = scores(H_A - 1, 1)
    values(H_A - 2, 0, m_even)
    values(H_A - 1, 1, m_odd)
    o_ref[0] = (ot_s[...].T * sza_ref[0].astype(F32)).astype(BF16)


def _attn_call(q, kc, kl, vtc, vtl, sza, hs, som, szm, xc, mhg, skip, *, tq):
    b, _, t, _ = q.shape
    n_ctx = kc.shape[2]
    head_blk = lambda n, w: pl.BlockSpec((1, H_A, n, w), lambda i, j: (i, 0, 0, 0))
    const = lambda shape: pl.BlockSpec(shape, lambda i, j: (0,) * len(shape))
    row = lambda w: pl.BlockSpec((1, tq, w), lambda i, j: (i, j, 0))
    half = jax.ShapeDtypeStruct((b, t, W_A), BF16)
    return pl.pallas_call(
        _attn_kernel,
        grid=(b, t // tq),
        in_specs=[pl.BlockSpec((1, H_A, tq, SLAB), lambda i, j: (i, 0, j, 0)),
                  head_blk(n_ctx, SLAB), head_blk(t, SLAB),
                  head_blk(DV_EXT, n_ctx), head_blk(DV_EXT, t), row(W_A),
                  pl.BlockSpec((1, tq // CHUNK, W_B, CHUNK), lambda i, j: (i, j, 0, 0)),
                  row(W_B), row(W_B), row(W_B), const((1, W_B)), const((1, W_B))],
        out_specs=[row(W_A), row(W_B)],
        out_shape=[half, jax.ShapeDtypeStruct((b, t, W_B), BF16)],
        scratch_shapes=[pltpu.VMEM((2, n_ctx + t, tq), F32),
                        pltpu.VMEM((W_A, tq), F32)],
        compiler_params=pltpu.CompilerParams(dimension_semantics=("parallel", "parallel"),
                                             vmem_limit_bytes=VMEM_LIMIT),
        name="mla_attention",
    )(q, kc, kl, vtc, vtl, sza, hs, som, szm, xc, mhg, skip)


def _gate_fold_kernel(wq_ref, wk_ref, wv_ref, wg_ref, wgc_ref, wgm_ref):
    wgc_ref[...] = (_dot(wq_ref[...], wg_ref[0:W_B, :])
                    + _dot(wk_ref[...], wg_ref[W_B:2 * W_B, :])).astype(BF16)
    wgm_ref[...] = _dot(wv_ref[...], wg_ref[2 * W_B:3 * W_B, :]).astype(BF16)


def _gate_fold_call(wq, wk, wv, wg):
    shp = jax.ShapeDtypeStruct((W_B, LANES), BF16)
    return pl.pallas_call(_gate_fold_kernel, out_shape=[shp, shp],
                          compiler_params=pltpu.CompilerParams(vmem_limit_bytes=VMEM_LIMIT),
                          name="mlstm_gate_fold")(wq, wk, wv, wg)


_PAD = 8


def _lane_scans(rows_per_chunk, combine, fill, out):
    npc = CHUNK // LANES
    flat = [x[:, i * LANES:(i + 1) * LANES] for x in rows_per_chunk for i in range(npc)]
    lane = lax.broadcasted_iota(jnp.int32, flat[0].shape, 1)
    pre, suf, sh = list(flat), list(flat), 1
    while sh < LANES:
        pre = [combine(p, jnp.where(lane >= sh, pltpu.roll(p, sh, 1), fill)) for p in pre]
        suf = [combine(s, jnp.where(lane < LANES - sh, pltpu.roll(s, LANES - sh, 1), fill)) for s in suf]
        sh *= 2
        yield
    for c in range(len(rows_per_chunk)):
        p, s = pre[c * npc:(c + 1) * npc], suf[c * npc:(c + 1) * npc]
        tot = [x[:, LANES - 1:LANES] for x in p]
        run = None
        for i in range(npc):
            if run is not None:
                p[i] = combine(p[i], run)
            run = tot[i] if run is None else combine(run, tot[i])
        run = None
        for i in reversed(range(npc)):
            if run is not None:
                s[i] = combine(s[i], run)
            run = tot[i] if run is None else combine(run, tot[i])
        out.append((jnp.concatenate(p, axis=1), jnp.concatenate(s, axis=1)))


def _gate_tables(gts, dests):
    L = CHUNK
    fwd_rows = lax.broadcasted_iota(jnp.int32, (N_HD, L), 0) < H_B
    lis = [gt[0:N_HD, :] for gt in gts]
    sums = []
    yield from _lane_scans([_log_sigmoid(gt[N_HD:2 * N_HD, :]) for gt in gts], jnp.add, 0.0, sums)
    cums = [jnp.where(fwd_rows, ps, ss) for ps, ss in sums]
    rs = [li - cum for li, cum in zip(lis, cums)]
    maxs = []
    yield from _lane_scans(rs, jnp.maximum, -jnp.inf, maxs)
    for (grow_ref, c, gcol_ref, r0), cum, r, (pm, sm) in zip(dests, cums, rs, maxs):
        grow_ref[c] = jnp.concatenate([cum, r, jnp.where(fwd_rows, pm, sm)], axis=0)
        gcol_ref[r0:r0 + L, :] = jnp.concatenate([r, jnp.zeros((LANES - N_HD, L), F32)], axis=0).T


def _fill_pad(pad_s, xm_ref, n):
    zrow = jnp.zeros((_PAD, W_B), F32)
    pad_s[0:_PAD, :] = zrow
    pad_s[_PAD:_PAD + n, :] = xm_ref[...].astype(F32)
    pad_s[_PAD + n:2 * _PAD + n, :] = zrow


def _conv_gate_stage(c, pad_s, xm_ref, cw_ref, cb_ref, wgc_ref, wgm_ref, bg_ref, xc_out, xcb_s):
    L = CHUNK
    rows = slice(c * L, (c + 1) * L)
    r0 = c * L + _PAD
    row_id = lax.broadcasted_iota(jnp.int32, (L, W_B), 0)
    xcur = pad_s[r0:r0 + L, :]
    xprev = jnp.where(row_id == 0, pad_s[r0 - 1:r0, :], pltpu.roll(xcur, 1, 0))
    xnext = jnp.where(row_id == L - 1, pad_s[r0 + L:r0 + L + 1, :], pltpu.roll(xcur, L - 1, 0))
    pre = cb_ref[...] + xprev * cw_ref[0:1, :] + xcur * cw_ref[1:2, :] + xnext * cw_ref[2:3, :]
    xc = _silu(pre)
    xcb = xc.astype(BF16)
    if xc_out is not None:
        xc_out[rows, :] = xcb
    xcb_s[rows, :] = xcb
    g = _dot(xcb, wgc_ref[...]) + _dot(xm_ref[rows, :], wgm_ref[...]) + bg_ref[...]
    return g.T[0:2 * N_HD, :]


def _headwise_stage(c, xm_ref, xcb_s, wq_ref, wk_ref, wv_ref, q_s, k_s, vt_s):
    L = CHUNK
    rows = slice(c * L, (c + 1) * L)
    for hf in range(W_B // MXU_TILE):
        cs = slice(hf * MXU_TILE, (hf + 1) * MXU_TILE)
        xcb = xcb_s[rows, cs]
        if q_s is not None:
            q_s[rows, cs] = _dot(xcb, wq_ref[cs, cs]).astype(BF16)
        k_s[rows, cs] = _dot(xcb, wk_ref[cs, cs]).astype(BF16)
        vt_s[c, cs, :] = _dot(xm_ref[rows, cs], wv_ref[cs, cs]).T.astype(BF16)


_TAB_ROWS = 5 * N_HD


def _chain_tables(growc_s, grow_s, tab_s, nc):
    L = CHUNK
    fwd_rows = lax.broadcasted_iota(jnp.int32, (N_HD, L), 0) < H_B
    gc = growc_s[0]
    steps = [(gc, gc)] + [(grow_s[i], grow_s[nc - 1 - i]) for i in range(nc)]
    parts = []
    for g_f, g_b in steps:
        pick = lambda a: jnp.where(fwd_rows, g_f[a * N_HD:(a + 1) * N_HD, :], g_b[a * N_HD:(a + 1) * N_HD, :])
        cum, r, pm = pick(0), pick(1), pick(2)
        b_end = jnp.where(fwd_rows[:, 0:1], cum[:, L - 1:L], cum[:, 0:1])
        w = r + b_end
        parts.append((cum, pm, b_end, w, jnp.max(w, axis=1, keepdims=True)))
    m0 = jnp.zeros((N_HD, 1), F32)
    for idx, (cum, pm, b_end, w, wmax) in enumerate(parts):
        m_new = jnp.maximum(b_end + m0, wmax)
        mu = jnp.maximum(m0, pm)
        decay = jnp.broadcast_to(jnp.exp(b_end + m0 - m_new), (N_HD, L))
        tab_s[idx] = jnp.concatenate([mu, jnp.exp(m0 - mu), jnp.exp(-(cum + mu)),
                                      jnp.exp(w - m_new), decay], axis=0)
        m0 = m_new


def _tab(tab, a, hd):
    return tab[a * N_HD + hd:a * N_HD + hd + 1, :]


def _state_update(st_ref, hd, vt_ext, k_c, ws_row, decay_hd):
    vw = (vt_ext.astype(F32) * ws_row).astype(BF16)
    st_ref[hd] = decay_hd * st_ref[hd] + _dot(vw, k_c)


def _mlstm_kernel(xm_ref, xmc_ref, cw_ref, cb_ref, wq_ref, wk_ref, wv_ref, wgc_ref, wgm_ref, bg_ref,
                  hs_ref, xc_ref,
                  pad_s, xcb_s, q_s, k_s, vt_s, gcol_s, grow_s,
                  padc_s, xcbc_s, kc_s, vtc_s, gcolc_s, growc_s, tab_s, st_s):
    L = CHUNK
    t = xm_ref.shape[1]
    nc = t // L
    row_i = lax.broadcasted_iota(jnp.int32, (L, L), 0)
    col_i = lax.broadcasted_iota(jnp.int32, (L, L), 1)
    tri = (row_i <= col_i, row_i >= col_i)
    ones_rows = (lax.broadcasted_iota(jnp.int32, (ST_ROWS - DH_B, L), 0) == 0).astype(BF16)
    hsl = lambda hh: slice(hh * DH_B, (hh + 1) * DH_B)
    xm, xmc = xm_ref.at[0], xmc_ref.at[0]
    gate_w = (cw_ref, cb_ref, wgc_ref, wgm_ref, bg_ref)

    _fill_pad(padc_s, xmc, CTX_LEN)
    _fill_pad(pad_s, xm, t)
    gts = [_conv_gate_stage(0, padc_s, xmc, *gate_w, None, xcbc_s)]
    gts += [_conv_gate_stage(c, pad_s, xm, *gate_w, xc_ref.at[0], xcb_s) for c in range(nc)]
    dests = [(growc_s, 0, gcolc_s, 0)] + [(grow_s, c, gcol_s, c * L) for c in range(nc)]
    tables = _gate_tables(gts, dests)
    _headwise_stage(0, xmc, xcbc_s, wq_ref, wk_ref, wv_ref, None, kc_s, vtc_s)
    for c in range(nc):
        next(tables, None)
        next(tables, None)
        _headwise_stage(c, xm, xcb_s, wq_ref, wk_ref, wv_ref, q_s, k_s, vt_s)
    for _ in tables:
        pass

    _chain_tables(growc_s, grow_s, tab_s, nc)
    st_s[...] = jnp.zeros(st_s.shape, F32)
    tab = tab_s[0]
    for hd in range(N_HD):
        hh = hd % H_B
        vt_ext = jnp.concatenate([vtc_s[0, hsl(hh), :], ones_rows], axis=0)
        _state_update(st_s, hd, vt_ext, kc_s[:, hsl(hh)], _tab(tab, 3, hd), _tab(tab, 4, hd)[:, 0:DH_B])

    def body(i, carry, *, accumulate):
        ci = (i, nc - 1 - i)
        rows = tuple(pl.ds(pl.multiple_of(c * L, L), L) for c in ci)
        tab = tab_s[i + 1]
        gcols = tuple(gcol_s[r, :] for r in rows)
        live = {}

        def stage_scores(hd):
            d, hh = hd // H_B, hd % H_B
            k_c = k_s[rows[d], hsl(hh)]
            vt_ext = jnp.concatenate([vt_s[ci[d], hsl(hh), :], ones_rows], axis=0)
            lhs = jnp.concatenate([k_c, st_s[hd].astype(BF16)], axis=0)
            live[hd] = (k_c, vt_ext, _dot_nt(lhs, q_s[rows[d], hsl(hh)]))

        def stage_gate(hd):
            d = hd // H_B
            k_c, vt_ext, res = live[hd]
            rcol = gcols[d][:, hd:hd + 1]
            e = jnp.exp(jnp.where(tri[d], rcol - _tab(tab, 0, hd), -jnp.inf))
            p = (res[0:L, :] * e).astype(BF16)
            _state_update(st_s, hd, vt_ext, k_c, _tab(tab, 3, hd), _tab(tab, 4, hd)[:, 0:DH_B])
            live[hd] = (vt_ext, res[L:L + ST_ROWS, :], p)

        def stage_out(hd):
            d, hh = hd // H_B, hd % H_B
            vt_ext, inter, p = live.pop(hd)
            tot = _tab(tab, 1, hd) * inter + _dot(vt_ext, p)
            den = tot[DH_B:DH_B + 1, :]
            h_t = tot[0:DH_B, :] / jnp.maximum(jnp.abs(den), _tab(tab, 2, hd))
            if accumulate:
                hs_ref[0, ci[d], hsl(hh), :] += h_t
            else:
                hs_ref[0, ci[d], hsl(hh), :] = h_t

        stages = (stage_scores, stage_gate, stage_out)
        for step in range(N_HD + len(stages) - 1):
            for si in reversed(range(len(stages))):
                if 0 <= step - si < N_HD:
                    stages[si](step - si)
        return carry

    lax.fori_loop(0, nc // 2, functools.partial(body, accumulate=False), 0)
    lax.fori_loop(nc // 2, nc, functools.partial(body, accumulate=True), 0)


def _mlstm_out(hs_t, som, szm, xc, mhg, skip):
    hsum = hs_t.T * som.astype(F32)
    parts = [_layer_norm(hsum[:, hh * DH_B:(hh + 1) * DH_B]) for hh in range(H_B)]
    hb = jnp.concatenate(parts, axis=1) * mhg + skip * xc.astype(F32)
    return (hb * szm.astype(F32)).astype(BF16)


def _mlstm_call(xm, xmc, conv_w, conv_b, wq, wk, wv, wgc, wgm, bg):
    b, t, _ = xm.shape
    nc = t // CHUNK
    assert nc % 2 == 0
    seq = lambda n: pl.BlockSpec((1, n, W_B), lambda i: (i, 0, 0))
    const = lambda shape: pl.BlockSpec(shape, lambda i: (0,) * len(shape))
    scratch = [
        pltpu.VMEM((t + 2 * _PAD, W_B), F32),
        pltpu.VMEM((t, W_B), BF16),
        pltpu.VMEM((t, W_B), BF16),
        pltpu.VMEM((t, W_B), BF16),
        pltpu.VMEM((nc, W_B, CHUNK), BF16),
        pltpu.VMEM((t, LANES), F32),
        pltpu.VMEM((nc, 3 * N_HD, CHUNK), F32),
        pltpu.VMEM((CTX_LEN + 2 * _PAD, W_B), F32),
        pltpu.VMEM((CTX_LEN, W_B), BF16),
        pltpu.VMEM((CTX_LEN, W_B), BF16),
        pltpu.VMEM((1, W_B, CHUNK), BF16),
        pltpu.VMEM((CTX_LEN, LANES), F32),
        pltpu.VMEM((1, 3 * N_HD, CHUNK), F32),
        pltpu.VMEM((nc + 1, _TAB_ROWS, CHUNK), F32),
        pltpu.VMEM((N_HD, ST_ROWS, DH_B), F32),
    ]
    return pl.pallas_call(
        _mlstm_kernel,
        grid=(b,),
        in_specs=[seq(t), seq(CTX_LEN), const((3, W_B)), const((1, W_B)),
                  const((W_B, W_B)), const((W_B, W_B)), const((W_B, W_B)),
                  const((W_B, LANES)), const((W_B, LANES)), const((1, LANES))],
        out_specs=[pl.BlockSpec((1, nc, W_B, CHUNK), lambda i: (i, 0, 0, 0)), seq(t)],
        out_shape=[jax.ShapeDtypeStruct((b, nc, W_B, CHUNK), F32),
                   jax.ShapeDtypeStruct((b, t, W_B), BF16)],
        scratch_shapes=scratch,
        compiler_params=pltpu.CompilerParams(dimension_semantics=("arbitrary",),
                                             vmem_limit_bytes=VMEM_LIMIT),
        name="mlstm_bidir",
    )(xm, xmc, conv_w, conv_b, wq, wk, wv, wgc, wgm, bg)


def _out_kernel(h_ref, mod_ref, ya_ref, yb_ref, wo_ref, g_ref, b_ref, o_ref, *, n_sub):
    sub = h_ref.shape[1] // n_sub
    gate = mod_ref[0][:, 2 * D_MODEL:3 * D_MODEL]

    def mix_stage(s):
        rows = slice(s * sub, (s + 1) * sub)
        return (_dot(ya_ref[0, rows, :], wo_ref[0:W_A, :])
                + _dot(yb_ref[0, rows, :], wo_ref[W_A:W_A + W_B, :]))

    def norm_stage(s, y):
        rows = slice(s * sub, (s + 1) * sub)
        o_ref[0, rows, :] = _layer_norm(ALPHA * h_ref[0, rows, :] + gate * y) * g_ref[...] + b_ref[...]

    y = mix_stage(0)
    for s in range(n_sub):
        y_next = mix_stage(s + 1) if s + 1 < n_sub else None
        norm_stage(s, y)
        y = y_next


def _out_call(h, mod3, ya, yb, wo, g, bb, *, tm, n_sub):
    b, t, _ = h.shape
    const = lambda shape: pl.BlockSpec(shape, lambda i, j: (0,) * len(shape))
    row = lambda w: pl.BlockSpec((1, tm, w), lambda i, j: (i, j, 0))
    return pl.pallas_call(
        functools.partial(_out_kernel, n_sub=n_sub),
        grid=(b, t // tm),
        in_specs=[row(D_MODEL), pl.BlockSpec((1, 1, 3 * D_MODEL), lambda i, j: (i, 0, 0)),
                  row(W_A), row(W_B),
                  const((W_A + W_B, D_MODEL)), const((1, D_MODEL)), const((1, D_MODEL))],
        out_specs=row(D_MODEL),
        out_shape=jax.ShapeDtypeStruct((b, t, D_MODEL), F32),
        compiler_params=pltpu.CompilerParams(dimension_semantics=("parallel", "parallel"),
                                             vmem_limit_bytes=VMEM_LIMIT),
        name="out_proj_ln",
    )(h, mod3, ya, yb, wo, g, bb)


def _rot_partner(w):
    w4 = w.reshape(w.shape[:-1] + (2, 2, ROPE_FREQS))
    return jnp.stack([-w4[..., 1, :], w4[..., 0, :]], axis=-2).reshape(w.shape)


def _rope_tables(seq, scale_keep, scale_rope):
    n_rows = seq // GRID_W
    rowp = np.repeat(np.arange(n_rows, dtype=np.float32), GRID_W)
    colp = np.tile(np.arange(GRID_W, dtype=np.float32), n_rows)
    inv = (np.float32(ROPE_BASE) ** (-np.arange(ROPE_FREQS, dtype=np.float32) / np.float32(ROPE_FREQS)))
    ang = np.stack([rowp[:, None] * inv, colp[:, None] * inv], axis=1).astype(np.float32)
    cos = np.broadcast_to(np.cos(ang)[:, :, None, :], (seq, 2, 2, ROPE_FREQS)).reshape(seq, DR_A)
    sin = np.broadcast_to(np.sin(ang)[:, :, None, :], (seq, 2, 2, ROPE_FREQS)).reshape(seq, DR_A)
    z32 = np.zeros((seq, DR_A), np.float32)
    t1 = np.concatenate([np.full((seq, DN_A), scale_keep, np.float32), z32, cos * scale_rope], axis=1)
    t2 = np.concatenate([np.zeros((seq, DN_A), np.float32), z32, sin * scale_rope], axis=1)
    return jnp.asarray(t1, F32), jnp.asarray(t2, F32)


def _block_diag(w):
    nb, bs, _ = w.shape
    n = nb * bs
    rows = w.reshape(n, bs)
    ri = lax.broadcasted_iota(jnp.int32, (n, n), 0)
    ci = lax.broadcasted_iota(jnp.int32, (n, n), 1)
    out = jnp.zeros((n, n), w.dtype)
    for o in range(bs):
        out = out + jnp.where((ri // bs == ci // bs) & (ci % bs == o), rows[:, o:o + 1], 0.0)
    return out


def kernel(x, c, ctx, c_ctx, ln_in_g, ln_in_b, w_ada, b_ada, w_in, g_qa, w_qb, g_kva, w_kvb, conv_w, conv_b, w_mq, w_mk, w_mv, w_gate, b_gate, mh_g, skip, w_out, ln_g, ln_b):
    b, t, _ = x.shape
    l = 0
    r2 = lambda v: v.reshape(1, -1)

    wina, winr = _win_call(jnp.swapaxes(w_in[l], 0, 1))
    wq3 = w_qb[l].reshape(Q_LORA, H_A, DN_A + DR_A)
    wq_r = wq3[..., DN_A:]
    wq = jnp.concatenate([wq3[..., :DN_A], _rot_partner(wq_r), wq_r], axis=-1)
    wq = wq.reshape(Q_LORA, H_A * SLAB).astype(BF16)
    wkv3 = w_kvb[l].reshape(KV_LORA, H_A, DN_A + DV_A)
    wkn = jnp.concatenate([wkv3[..., :DN_A], jnp.zeros((KV_LORA, H_A, SLAB - DN_A), F32)], axis=-1)
    wkn = wkn.reshape(KV_LORA, H_A * SLAB).astype(BF16)
    wvt = wkv3[..., DN_A:].reshape(KV_LORA, W_A).T.astype(BF16)
    wmq = _block_diag(w_mq[l])
    wmk = _block_diag(w_mk[l])
    wmv = _block_diag(w_mv[l])
    perm = jnp.array([0, 1, 2, 3, 8, 9, 10, 11, 4, 5, 6, 7, 12, 13, 14, 15])
    wg = jnp.pad(w_gate[l][:, perm], ((0, 0), (0, LANES - 2 * N_HD))).astype(BF16)
    bg = jnp.pad(b_gate[l][perm], (0, LANES - 2 * N_HD)).reshape(1, LANES)
    wgc, wgm = _gate_fold_call(wmq.astype(BF16), wmk.astype(BF16), wmv.astype(BF16), wg)
    wmk_s = (wmk * (DH_B ** -0.5)).astype(BF16)
    wo = w_out[l].astype(BF16)

    sm_scale = (DN_A + DR_A) ** -0.5 * LOG2_E
    t1q, t2q = _rope_tables(t, sm_scale, sm_scale)
    t1k, t2k = _rope_tables(t, 0.0, 1.0)
    n_ctx = ctx.shape[1]
    t1kc = jnp.asarray(np.concatenate([np.zeros((n_ctx, DN_A + DR_A), np.float32),
                                       np.ones((n_ctx, DR_A), np.float32)], axis=1))
    t2kc = jnp.asarray(np.zeros((n_ctx, SLAB), np.float32))

    cc = jnp.concatenate([c, c_ctx[None, :], jnp.zeros((16 - b - 1, D_MODEL), F32)], axis=0)
    mod = _ada_call(cc, w_ada[l], r2(b_ada[l]))
    mod3 = mod.reshape(16, 1, 3 * D_MODEL)

    lng, lnb = r2(ln_in_g), r2(ln_in_b)
    shared = (wina, winr, r2(g_qa[l]), wq, r2(g_kva[l]), wkn, wvt)
    q, kl, vtl, xm, sza, som, szm, h = _proj_call(x, mod3, None, lng, lnb, *shared, t1q, t2q, t1k, t2k,
                                                  tm=1024, n_sub=2, latent=True)
    kc, vtc, xmc = _proj_call(ctx, mod3, b, lng, lnb, *shared, t1kc, t2kc, t1kc, t2kc,
                              tm=n_ctx, n_sub=1, latent=False)

    hs, xc = _mlstm_call(xm, xmc, conv_w[l], r2(conv_b[l]), wmq.astype(BF16), wmk_s, wmv.astype(BF16),
                         wgc, wgm, bg)
    ya, yb = _attn_call(q, kc, kl, vtc, vtl, sza, hs, som, szm, xc, r2(mh_g[l]), r2(skip[l]), tq=512)
    return _out_call(h, mod3, ya, yb, wo, r2(ln_g[l]), r2(ln_b[l]), tm=1024, n_sub=4)
```

```python
import functools

import numpy as np
import jax
import jax.numpy as jnp
from jax import lax
from jax.experimental import pallas as pl
from jax.experimental.pallas import tpu as pltpu

F32 = jnp.float32
BF16 = jnp.bfloat16

D_MODEL = 1024
CTX_LEN = 256
GRID_W = 64
H_A, DN_A, DR_A, DV_A = 8, 64, 32, 64
W_A = H_A * DV_A
Q_LORA, KV_LORA = 256, 128
ROPE_FREQS = DR_A // 4
ROPE_BASE = 10000.0
H_B, DH_B = 4, 128
W_B = H_B * DH_B
QKV_BS = 4
DEPTH = 1
ALPHA = (2.0 * DEPTH) ** 0.25
LN_EPS = 1e-5
RMS_EPS = 1e-6
LOG2_E = 1.4426950408889634

LANES = 128
MXU_TILE = 256
SLAB = LANES
DV_EXT = DV_A + 16
ATTN_KEY_CHUNK = MXU_TILE
CHUNK = 256
N_HD = 2 * H_B
ST_ROWS = DH_B + 16
VMEM_LIMIT = 56 * 1024 * 1024

_NT = (((1,), (1,)), ((), ()))


def _dot(a, b):
    return jnp.dot(a, b, preferred_element_type=F32)


def _dot_nt(a, b):
    return lax.dot_general(a, b, _NT, preferred_element_type=F32)


def _layer_norm(x):
    mu = jnp.mean(x, axis=-1, keepdims=True)
    xc = x - mu
    var = jnp.mean(xc * xc, axis=-1, keepdims=True)
    return xc * lax.rsqrt(var + LN_EPS)


def _rms_norm(x, g):
    return (x * lax.rsqrt(jnp.mean(x * x, axis=-1, keepdims=True) + RMS_EPS)) * g


def _silu(x):
    return x * jax.nn.sigmoid(x)


def _log_sigmoid(x):
    return jnp.minimum(x, 0.0) - jnp.log1p(jnp.exp(-jnp.abs(x)))


def _ada_kernel(cc_ref, w_ref, b_ref, o_ref):
    a = _silu(cc_ref[...]).astype(BF16)
    o_ref[...] = _dot(a, w_ref[...].astype(BF16)) + b_ref[...]


def _ada_call(cc, w_ada, b_ada):
    n = w_ada.shape[1]
    tn = 1024
    return pl.pallas_call(
        _ada_kernel,
        grid=(n // tn,),
        in_specs=[pl.BlockSpec((16, D_MODEL), lambda j: (0, 0)),
                  pl.BlockSpec((D_MODEL, tn), lambda j: (0, j)),
                  pl.BlockSpec((1, tn), lambda j: (0, j))],
        out_specs=pl.BlockSpec((16, tn), lambda j: (0, j)),
        out_shape=jax.ShapeDtypeStruct((16, n), F32),
        compiler_params=pltpu.CompilerParams(dimension_semantics=("arbitrary",),
                                             vmem_limit_bytes=VMEM_LIMIT),
        name="ada_mod",
    )(cc, w_ada, b_ada)


_S_KVA, _S_KR = Q_LORA + KV_LORA, Q_LORA + KV_LORA + DR_A


def _win_kernel(wt_ref, wina_ref, winr_ref):
    n_r = winr_ref.shape[1]
    for j in range(_S_KVA // LANES):
        wina_ref[:, j * LANES:(j + 1) * LANES] = wt_ref[j * LANES:(j + 1) * LANES, :].T.astype(BF16)
    kr = wt_ref[_S_KVA:_S_KR, :]
    a0, b0, a1, b1 = (kr[i * ROPE_FREQS:(i + 1) * ROPE_FREQS, :] for i in range(4))
    blk = jnp.concatenate([jnp.zeros((DN_A, D_MODEL), F32), -b0, a0, -b1, a1, kr], axis=0)
    wina_ref[:, _S_KVA:_S_KVA + LANES] = blk.T.astype(BF16)
    for j in range(n_r // MXU_TILE):
        rows = slice(_S_KR + j * MXU_TILE, _S_KR + (j + 1) * MXU_TILE)
        winr_ref[:, j * MXU_TILE:(j + 1) * MXU_TILE] = wt_ref[rows, :].T.astype(BF16)


def _win_call(wt):
    n_in, d = wt.shape
    return pl.pallas_call(
        _win_kernel,
        out_shape=[jax.ShapeDtypeStruct((d, _S_KVA + LANES), BF16),
                   jax.ShapeDtypeStruct((d, n_in - _S_KR), BF16)],
        compiler_params=pltpu.CompilerParams(vmem_limit_bytes=VMEM_LIMIT),
        name="win_relayout",
    )(wt)


def _rope_slab(s, t1, t2):
    return s * t1 + pltpu.roll(s, 32, 1) * t2


def _proj_kernel(x_ref, mod_ref, lng_ref, lnb_ref, wina_ref, winr_ref, gqa_ref, wq_ref, gkva_ref,
                 wkn_ref, wvt_ref, t1q_ref, t2q_ref, t1k_ref, t2k_ref, *out_refs, latent, n_sub):
    if latent:
        q_ref, k_ref, vt_ref, xm_ref, sza_ref, som_ref, szm_ref, h_ref = out_refs
    else:
        k_ref, vt_ref, xm_ref = out_refs
    sub = x_ref.shape[1] // n_sub
    mod = mod_ref[0]
    shift, scale1 = mod[:, 0:D_MODEL], 1.0 + mod[:, D_MODEL:2 * D_MODEL]
    ones_rows = (lax.broadcasted_iota(jnp.int32, (DV_EXT - DV_A, sub), 0) == 0).astype(BF16)

    def norm_stage(s):
        rows = slice(s * sub, (s + 1) * sub)
        h = _layer_norm(x_ref[0, rows, :]) * lng_ref[...] + lnb_ref[...]
        if latent:
            h_ref[0, rows, :] = h
        return (h * scale1 + shift).astype(BF16)

    def proj_stage(s, u):
        rows = slice(s * sub, (s + 1) * sub)
        p0 = _dot(u, wina_ref[...])
        kvn = _rms_norm(p0[:, 256:384], gkva_ref[...]).astype(BF16)
        kn = _dot(kvn, wkn_ref[...])
        krr = _rope_slab(p0[:, 384:512], t1k_ref[rows, :], t2k_ref[rows, :])
        for hh in range(H_A):
            k_ref[0, hh, rows, :] = (kn[:, hh * SLAB:(hh + 1) * SLAB] + krr).astype(BF16)
        vt = _dot_nt(wvt_ref[...], kvn).astype(BF16)
        for hh in range(H_A):
            vt_ref[0, hh, 0:DV_A, rows] = vt[hh * DV_A:(hh + 1) * DV_A, :]
            vt_ref[0, hh, DV_A:DV_EXT, rows] = ones_rows
        xm_ref[0, rows, :] = _dot(u, winr_ref[:, 512:1024]).astype(BF16)
        if latent:
            qn = _rms_norm(p0[:, 0:256], gqa_ref[...]).astype(BF16)
            qs = _dot(qn, wq_ref[...])
            t1q, t2q = t1q_ref[rows, :], t2q_ref[rows, :]
            for hh in range(H_A):
                q_ref[0, hh, rows, :] = _rope_slab(qs[:, hh * SLAB:(hh + 1) * SLAB], t1q, t2q).astype(BF16)
            sza_ref[0, rows, :] = _silu(_dot(u, winr_ref[:, 0:512])).astype(BF16)
            som_ref[0, rows, :] = jax.nn.sigmoid(_dot(u, winr_ref[:, 1024:1536])).astype(BF16)
            szm_ref[0, rows, :] = _silu(_dot(u, winr_ref[:, 1536:2048])).astype(BF16)

    u = norm_stage(0)
    for s in range(n_sub):
        u_next = norm_stage(s + 1) if s + 1 < n_sub else None
        proj_stage(s, u)
        u = u_next


def _proj_call(x, mod3, mod_row, lng, lnb, wina, winr, gqa, wq, gkva, wkn, wvt, t1q, t2q, t1k, t2k, *,
               tm, n_sub, latent):
    b, t, _ = x.shape
    const = lambda shape: pl.BlockSpec(shape, lambda i, j: (0,) * len(shape))
    tab = pl.BlockSpec((tm, SLAB), lambda i, j: (j, 0))
    row = lambda w: pl.BlockSpec((1, tm, w), lambda i, j: (i, j, 0))
    if mod_row is None:
        mod_spec = pl.BlockSpec((1, 1, 3 * D_MODEL), lambda i, j: (i, 0, 0))
    else:
        mod_spec = pl.BlockSpec((1, 1, 3 * D_MODEL), lambda i, j: (mod_row, 0, 0))
    in_specs = [row(D_MODEL), mod_spec, const((1, D_MODEL)), const((1, D_MODEL)),
                const(wina.shape), const(winr.shape), const((1, Q_LORA)), const((Q_LORA, H_A * SLAB)),
                const((1, KV_LORA)), const((KV_LORA, H_A * SLAB)), const((W_A, KV_LORA)),
                tab, tab, tab, tab]
    k_spec = pl.BlockSpec((1, H_A, tm, SLAB), lambda i, j: (i, 0, j, 0))
    vt_spec = pl.BlockSpec((1, H_A, DV_EXT, tm), lambda i, j: (i, 0, 0, j))
    k_shape = jax.ShapeDtypeStruct((b, H_A, t, SLAB), BF16)
    vt_shape = jax.ShapeDtypeStruct((b, H_A, DV_EXT, t), BF16)
    half = jax.ShapeDtypeStruct((b, t, W_B), BF16)
    if latent:
        out_specs = [k_spec, k_spec, vt_spec, row(W_B), row(W_A), row(W_B), row(W_B), row(D_MODEL)]
        out_shape = [k_shape, k_shape, vt_shape, half, half, half, half,
                     jax.ShapeDtypeStruct((b, t, D_MODEL), F32)]
    else:
        out_specs = [k_spec, vt_spec, row(W_B)]
        out_shape = [k_shape, vt_shape, half]
    return pl.pallas_call(
        functools.partial(_proj_kernel, latent=latent, n_sub=n_sub),
        grid=(b, t // tm),
        in_specs=in_specs, out_specs=out_specs, out_shape=out_shape,
        compiler_params=pltpu.CompilerParams(dimension_semantics=("parallel", "parallel"),
                                             vmem_limit_bytes=VMEM_LIMIT),
        name="in_proj_latent" if latent else "in_proj_ctx",
    )(x, mod3, lng, lnb, wina, winr, gqa, wq, gkva, wkn, wvt, t1q, t2q, t1k, t2k)


def _attn_kernel(q_ref, kc_ref, kl_ref, vtc_ref, vtl_ref, sza_ref, hs_ref, som_ref, szm_ref, xc_ref,
                 mhg_ref, skip_ref, o_ref, yb_ref, s_buf, ot_s):
    n_ctx, t = kc_ref.shape[2], kl_ref.shape[2]
    kc = ATTN_KEY_CHUNK

    def scores(h, slot):
        qh = q_ref[0, h]
        sc = _dot_nt(kc_ref[0, h], qh)
        sk = _dot_nt(kl_ref[0, h], qh)
        s_buf[slot, 0:n_ctx, :] = sc
        s_buf[slot, n_ctx:n_ctx + t, :] = sk
        return jnp.maximum(jnp.max(sc, axis=0, keepdims=True), jnp.max(sk, axis=0, keepdims=True))

    def values(h, slot, m):
        chunks = [(vtc_ref, c * kc, c * kc) for c in range(n_ctx // kc)]
        chunks += [(vtl_ref, c * kc, n_ctx + c * kc) for c in range(t // kc)]
        acc = None
        for vref, v0, s0 in chunks:
            p = jnp.exp2(s_buf[slot, s0:s0 + kc, :] - m).astype(BF16)
            d = _dot(vref[0, h, :, v0:v0 + kc], p)
            acc = d if acc is None else acc + d
        rows = pl.ds(pl.multiple_of(h * DV_A, DV_A), DV_A)
        ot_s[rows, :] = acc[0:DV_A, :] / acc[DV_A:DV_A + 1, :]

    def cell_out(pc):
        rows = pl.ds(pl.multiple_of(pc * LANES, LANES), LANES)
        yb_ref[0, rows, :] = _mlstm_out(hs_ref[0, pc], som_ref[0, rows, :], szm_ref[0, rows, :],
                                        xc_ref[0, rows, :], mhg_ref[...], skip_ref[...])

    def body(i, m_even):
        h = 2 * i
        m_odd = scores(h + 1, 1)
        cell_out(i + 1)
        values(h, 0, m_even)
        m_even = scores(h + 2, 0)
        values(h + 1, 1, m_odd)
        return m_even

    assert hs_ref.shape[1] == H_A // 2
    m_first = scores(0, 0)
    cell_out(0)
    m_even = lax.fori_loop(0, H_A // 2 - 1, body, m_first)
    m_odd = scores(H_A - 1, 1)
    values(H_A - 2, 0, m_even)
    values(H_A - 1, 1, m_odd)
    o_ref[0] = (ot_s[...].T * sza_ref[0].astype(F32)).astype(BF16)


def _attn_call(q, kc, kl, vtc, vtl, sza, hs, som, szm, xc, mhg, skip, *, tq):
    b, _, t, _ = q.shape
    n_ctx = kc.shape[2]
    head_blk = lambda n, w: pl.BlockSpec((1, H_A, n, w), lambda i, j: (i, 0, 0, 0))
    const = lambda shape: pl.BlockSpec(shape, lambda i, j: (0,) * len(shape))
    row = lambda w: pl.BlockSpec((1, tq, w), lambda i, j: (i, j, 0))
    half = jax.ShapeDtypeStruct((b, t, W_A), BF16)
    return pl.pallas_call(
        _attn_kernel,
        grid=(b, t // tq),
        in_specs=[pl.BlockSpec((1, H_A, tq, SLAB), lambda i, j: (i, 0, j, 0)),
                  head_blk(n_ctx, SLAB), head_blk(t, SLAB),
                  head_blk(DV_EXT, n_ctx), head_blk(DV_EXT, t), row(W_A),
                  pl.BlockSpec((1, tq // LANES, W_B, LANES), lambda i, j: (i, j, 0, 0)),
                  row(W_B), row(W_B), row(W_B), const((1, W_B)), const((1, W_B))],
        out_specs=[row(W_A), row(W_B)],
        out_shape=[half, jax.ShapeDtypeStruct((b, t, W_B), BF16)],
        scratch_shapes=[pltpu.VMEM((2, n_ctx + t, tq), F32),
                        pltpu.VMEM((W_A, tq), F32)],
        compiler_params=pltpu.CompilerParams(dimension_semantics=("parallel", "parallel"),
                                             vmem_limit_bytes=VMEM_LIMIT),
        name="mla_attention",
    )(q, kc, kl, vtc, vtl, sza, hs, som, szm, xc, mhg, skip)


def _gate_fold_kernel(wq_ref, wk_ref, wv_ref, wg_ref, wgc_ref, wgm_ref):
    wgc_ref[...] = (_dot(wq_ref[...], wg_ref[0:W_B, :])
                    + _dot(wk_ref[...], wg_ref[W_B:2 * W_B, :])).astype(BF16)
    wgm_ref[...] = _dot(wv_ref[...], wg_ref[2 * W_B:3 * W_B, :]).astype(BF16)


def _gate_fold_call(wq, wk, wv, wg):
    shp = jax.ShapeDtypeStruct((W_B, LANES), BF16)
    return pl.pallas_call(_gate_fold_kernel, out_shape=[shp, shp],
                          compiler_params=pltpu.CompilerParams(vmem_limit_bytes=VMEM_LIMIT),
                          name="mlstm_gate_fold")(wq, wk, wv, wg)


_PAD = 8


def _lane_scans(rows_per_chunk, combine, fill, out):
    npc = CHUNK // LANES
    flat = [x[:, i * LANES:(i + 1) * LANES] for x in rows_per_chunk for i in range(npc)]
    lane = lax.broadcasted_iota(jnp.int32, flat[0].shape, 1)
    pre, suf, sh = list(flat), list(flat), 1
    while sh < LANES:
        pre = [combine(p, jnp.where(lane >= sh, pltpu.roll(p, sh, 1), fill)) for p in pre]
        suf = [combine(s, jnp.where(lane < LANES - sh, pltpu.roll(s, LANES - sh, 1), fill)) for s in suf]
        sh *= 2
        yield
    for c in range(len(rows_per_chunk)):
        p, s = pre[c * npc:(c + 1) * npc], suf[c * npc:(c + 1) * npc]
        tot = [x[:, LANES - 1:LANES] for x in p]
        run = None
        for i in range(npc):
            if run is not None:
                p[i] = combine(p[i], run)
            run = tot[i] if run is None else combine(run, tot[i])
        run = None
        for i in reversed(range(npc)):
            if run is not None:
                s[i] = combine(s[i], run)
            run = tot[i] if run is None else combine(run, tot[i])
        out.append((jnp.concatenate(p, axis=1), jnp.concatenate(s, axis=1)))


def _gate_tables(gts, dests):
    L = CHUNK
    fwd_rows = lax.broadcasted_iota(jnp.int32, (N_HD, L), 0) < H_B
    lis = [gt[0:N_HD, :] for gt in gts]
    sums = []
    yield from _lane_scans([_log_sigmoid(gt[N_HD:2 * N_HD, :]) for gt in gts], jnp.add, 0.0, sums)
    cums = [jnp.where(fwd_rows, ps, ss) for ps, ss in sums]
    rs = [li - cum for li, cum in zip(lis, cums)]
    maxs = []
    yield from _lane_scans(rs, jnp.maximum, -jnp.inf, maxs)
    for (grow_ref, c, gcol_ref, r0), cum, r, (pm, sm) in zip(dests, cums, rs, maxs):
        grow_ref[c] = jnp.concatenate([cum, r, jnp.where(fwd_rows, pm, sm)], axis=0)
        gcol_ref[r0:r0 + L, :] = jnp.concatenate([r, jnp.zeros((LANES - N_HD, L), F32)], axis=0).T


def _fill_pad(pad_s, xm_ref, n):
    zrow = jnp.zeros((_PAD, W_B), F32)
    pad_s[0:_PAD, :] = zrow
    pad_s[_PAD:_PAD + n, :] = xm_ref[...].astype(F32)
    pad_s[_PAD + n:2 * _PAD + n, :] = zrow


def _conv_gate_stage(c, pad_s, xm_ref, cw_ref, cb_ref, wgc_ref, wgm_ref, bg_ref, xc_out, xcb_s):
    L = CHUNK
    rows = slice(c * L, (c + 1) * L)
    r0 = c * L + _PAD
    row_id = lax.broadcasted_iota(jnp.int32, (L, W_B), 0)
    xcur = pad_s[r0:r0 + L, :]
    xprev = jnp.where(row_id == 0, pad_s[r0 - 1:r0, :], pltpu.roll(xcur, 1, 0))
    xnext = jnp.where(row_id == L - 1, pad_s[r0 + L:r0 + L + 1, :], pltpu.roll(xcur, L - 1, 0))
    pre = cb_ref[...] + xprev * cw_ref[0:1, :] + xcur * cw_ref[1:2, :] + xnext * cw_ref[2:3, :]
    xc = _silu(pre)
    xcb = xc.astype(BF16)
    if xc_out is not None:
        xc_out[rows, :] = xcb
    xcb_s[rows, :] = xcb
    g = _dot(xcb, wgc_ref[...]) + _dot(xm_ref[rows, :], wgm_ref[...]) + bg_ref[...]
    return g.T[0:2 * N_HD, :]


def _headwise_stage(c, xm_ref, xcb_s, wq_ref, wk_ref, wv_ref, q_s, k_s, vt_s):
    L = CHUNK
    rows = slice(c * L, (c + 1) * L)
    for hf in range(W_B // MXU_TILE):
        cs = slice(hf * MXU_TILE, (hf + 1) * MXU_TILE)
        xcb = xcb_s[rows, cs]
        if q_s is not None:
            q_s[rows, cs] = _dot(xcb, wq_ref[cs, cs]).astype(BF16)
        k_s[rows, cs] = _dot(xcb, wk_ref[cs, cs]).astype(BF16)
        vt_s[c, cs, :] = _dot(xm_ref[rows, cs], wv_ref[cs, cs]).T.astype(BF16)


_TAB_ROWS = 5 * N_HD


def _chain_tables(growc_s, grow_s, tab_s, nc):
    L = CHUNK
    fwd_rows = lax.broadcasted_iota(jnp.int32, (N_HD, L), 0) < H_B
    gc = growc_s[0]
    steps = [(gc, gc)] + [(grow_s[i], grow_s[nc - 1 - i]) for i in range(nc)]
    parts = []
    for g_f, g_b in steps:
        pick = lambda a: jnp.where(fwd_rows, g_f[a * N_HD:(a + 1) * N_HD, :], g_b[a * N_HD:(a + 1) * N_HD, :])
        cum, r, pm = pick(0), pick(1), pick(2)
        b_end = jnp.where(fwd_rows[:, 0:1], cum[:, L - 1:L], cum[:, 0:1])
        w = r + b_end
        parts.append((cum, pm, b_end, w, jnp.max(w, axis=1, keepdims=True)))
    m0 = jnp.zeros((N_HD, 1), F32)
    for idx, (cum, pm, b_end, w, wmax) in enumerate(parts):
        m_new = jnp.maximum(b_end + m0, wmax)
        mu = jnp.maximum(m0, pm)
        decay = jnp.broadcast_to(jnp.exp(b_end + m0 - m_new), (N_HD, L))
        tab_s[idx] = jnp.concatenate([mu, jnp.exp(m0 - mu), jnp.exp(-(cum + mu)),
                                      jnp.exp(w - m_new), decay], axis=0)
        m0 = m_new


def _tab(tab, a, hd):
    return tab[a * N_HD + hd:a * N_HD + hd + 1, :]


def _state_update(st_ref, hd, vt_ext, k_c, ws_row, decay_hd):
    vw = (vt_ext.astype(F32) * ws_row).astype(BF16)
    st_ref[hd] = decay_hd * st_ref[hd] + _dot(vw, k_c)


def _mlstm_kernel(xm_ref, xmc_ref, cw_ref, cb_ref, wq_ref, wk_ref, wv_ref, wgc_ref, wgm_ref, bg_ref,
                  hs_ref, xc_ref,
                  pad_s, xcb_s, q_s, k_s, vt_s, gcol_s, grow_s,
                  padc_s, xcbc_s, kc_s, vtc_s, gcolc_s, growc_s, tab_s, st_s):
    L = CHUNK
    t = xm_ref.shape[1]
    nc = t // L
    row_i = lax.broadcasted_iota(jnp.int32, (L, L), 0)
    col_i = lax.broadcasted_iota(jnp.int32, (L, L), 1)
    tri = (row_i <= col_i, row_i >= col_i)
    ones_rows = (lax.broadcasted_iota(jnp.int32, (ST_ROWS - DH_B, L), 0) == 0).astype(BF16)
    hsl = lambda hh: slice(hh * DH_B, (hh + 1) * DH_B)
    xm, xmc = xm_ref.at[0], xmc_ref.at[0]
    gate_w = (cw_ref, cb_ref, wgc_ref, wgm_ref, bg_ref)

    _fill_pad(padc_s, xmc, CTX_LEN)
    _fill_pad(pad_s, xm, t)
    gts = [_conv_gate_stage(0, padc_s, xmc, *gate_w, None, xcbc_s)]
    gts += [_conv_gate_stage(c, pad_s, xm, *gate_w, xc_ref.at[0], xcb_s) for c in range(nc)]
    dests = [(growc_s, 0, gcolc_s, 0)] + [(grow_s, c, gcol_s, c * L) for c in range(nc)]
    tables = _gate_tables(gts, dests)
    _headwise_stage(0, xmc, xcbc_s, wq_ref, wk_ref, wv_ref, None, kc_s, vtc_s)
    for c in range(nc):
        next(tables, None)
        next(tables, None)
        _headwise_stage(c, xm, xcb_s, wq_ref, wk_ref, wv_ref, q_s, k_s, vt_s)
    for _ in tables:
        pass

    _chain_tables(growc_s, grow_s, tab_s, nc)
    st_s[...] = jnp.zeros(st_s.shape, F32)
    tab = tab_s[0]
    for hd in range(N_HD):
        hh = hd % H_B
        vt_ext = jnp.concatenate([vtc_s[0, hsl(hh), :], ones_rows], axis=0)
        _state_update(st_s, hd, vt_ext, kc_s[:, hsl(hh)], _tab(tab, 3, hd), _tab(tab, 4, hd)[:, 0:DH_B])

    def body(i, carry, *, accumulate):
        ci = (i, nc - 1 - i)
        rows = tuple(pl.ds(pl.multiple_of(c * L, L), L) for c in ci)
        tab = tab_s[i + 1]
        gcols = tuple(gcol_s[r, :] for r in rows)
        live = {}

        def stage_scores(hd):
            d, hh = hd // H_B, hd % H_B
            k_c = k_s[rows[d], hsl(hh)]
            vt_ext = jnp.concatenate([vt_s[ci[d], hsl(hh), :], ones_rows], axis=0)
            lhs = jnp.concatenate([k_c, st_s[hd].astype(BF16)], axis=0)
            live[hd] = (k_c, vt_ext, _dot_nt(lhs, q_s[rows[d], hsl(hh)]))

        def stage_gate(hd):
            d = hd // H_B
            k_c, vt_ext, res = live[hd]
            rcol = gcols[d][:, hd:hd + 1]
            e = jnp.exp(jnp.where(tri[d], rcol - _tab(tab, 0, hd), -jnp.inf))
            p = (res[0:L, :] * e).astype(BF16)
            _state_update(st_s, hd, vt_ext, k_c, _tab(tab, 3, hd), _tab(tab, 4, hd)[:, 0:DH_B])
            live[hd] = (vt_ext, res[L:L + ST_ROWS, :], p)

        def stage_out(hd):
            d, hh = hd // H_B, hd % H_B
            vt_ext, inter, p = live.pop(hd)
            tot = _tab(tab, 1, hd) * inter + _dot(vt_ext, p)
            den = tot[DH_B:DH_B + 1, :]
            h_t = tot[0:DH_B, :] / jnp.maximum(jnp.abs(den), _tab(tab, 2, hd))
            for pc in range(L // LANES):
                piece = h_t[:, pc * LANES:(pc + 1) * LANES]
                if accumulate:
                    hs_ref[0, ci[d] * (L // LANES) + pc, hsl(hh), :] += piece
                else:
                    hs_ref[0, ci[d] * (L // LANES) + pc, hsl(hh), :] = piece

        stages = (stage_scores, stage_gate, stage_out)
        for step in range(N_HD + len(stages) - 1):
            for si in reversed(range(len(stages))):
                if 0 <= step - si < N_HD:
                    stages[si](step - si)
        return carry

    lax.fori_loop(0, nc // 2, functools.partial(body, accumulate=False), 0)
    lax.fori_loop(nc // 2, nc, functools.partial(body, accumulate=True), 0)


def _mlstm_out(hs_t, som, szm, xc, mhg, skip):
    hsum = hs_t.T * som.astype(F32)
    parts = [_layer_norm(hsum[:, hh * DH_B:(hh + 1) * DH_B]) for hh in range(H_B)]
    hb = jnp.concatenate(parts, axis=1) * mhg + skip * xc.astype(F32)
    return (hb * szm.astype(F32)).astype(BF16)


def _mlstm_call(xm, xmc, conv_w, conv_b, wq, wk, wv, wgc, wgm, bg):
    b, t, _ = xm.shape
    nc = t // CHUNK
    assert nc % 2 == 0
    seq = lambda n: pl.BlockSpec((1, n, W_B), lambda i: (i, 0, 0))
    const = lambda shape: pl.BlockSpec(shape, lambda i: (0,) * len(shape))
    scratch = [
        pltpu.VMEM((t + 2 * _PAD, W_B), F32),
        pltpu.VMEM((t, W_B), BF16),
        pltpu.VMEM((t, W_B), BF16),
        pltpu.VMEM((t, W_B), BF16),
        pltpu.VMEM((nc, W_B, CHUNK), BF16),
        pltpu.VMEM((t, LANES), F32),
        pltpu.VMEM((nc, 3 * N_HD, CHUNK), F32),
        pltpu.VMEM((CTX_LEN + 2 * _PAD, W_B), F32),
        pltpu.VMEM((CTX_LEN, W_B), BF16),
        pltpu.VMEM((CTX_LEN, W_B), BF16),
        pltpu.VMEM((1, W_B, CHUNK), BF16),
        pltpu.VMEM((CTX_LEN, LANES), F32),
        pltpu.VMEM((1, 3 * N_HD, CHUNK), F32),
        pltpu.VMEM((nc + 1, _TAB_ROWS, CHUNK), F32),
        pltpu.VMEM((N_HD, ST_ROWS, DH_B), F32),
    ]
    return pl.pallas_call(
        _mlstm_kernel,
        grid=(b,),
        in_specs=[seq(t), seq(CTX_LEN), const((3, W_B)), const((1, W_B)),
                  const((W_B, W_B)), const((W_B, W_B)), const((W_B, W_B)),
                  const((W_B, LANES)), const((W_B, LANES)), const((1, LANES))],
        out_specs=[pl.BlockSpec((1, t // LANES, W_B, LANES), lambda i: (i, 0, 0, 0)), seq(t)],
        out_shape=[jax.ShapeDtypeStruct((b, t // LANES, W_B, LANES), F32),
                   jax.ShapeDtypeStruct((b, t, W_B), BF16)],
        scratch_shapes=scratch,
        compiler_params=pltpu.CompilerParams(dimension_semantics=("arbitrary",),
                                             vmem_limit_bytes=VMEM_LIMIT),
        name="mlstm_bidir",
    )(xm, xmc, conv_w, conv_b, wq, wk, wv, wgc, wgm, bg)


def _out_kernel(h_ref, mod_ref, ya_ref, yb_ref, wo_ref, g_ref, b_ref, o_ref, *, n_sub):
    sub = h_ref.shape[1] // n_sub
    gate = mod_ref[0][:, 2 * D_MODEL:3 * D_MODEL]

    def mix_stage(s):
        rows = slice(s * sub, (s + 1) * sub)
        return (_dot(ya_ref[0, rows, :], wo_ref[0:W_A, :])
                + _dot(yb_ref[0, rows, :], wo_ref[W_A:W_A + W_B, :]))

    def norm_stage(s, y):
        rows = slice(s * sub, (s + 1) * sub)
        o_ref[0, rows, :] = _layer_norm(ALPHA * h_ref[0, rows, :] + gate * y) * g_ref[...] + b_ref[...]

    y = mix_stage(0)
    for s in range(n_sub):
        y_next = mix_stage(s + 1) if s + 1 < n_sub else None
        norm_stage(s, y)
        y = y_next


def _out_call(h, mod3, ya, yb, wo, g, bb, *, tm, n_sub):
    b, t, _ = h.shape
    const = lambda shape: pl.BlockSpec(shape, lambda i, j: (0,) * len(shape))
    row = lambda w: pl.BlockSpec((1, tm, w), lambda i, j: (i, j, 0))
    return pl.pallas_call(
        functools.partial(_out_kernel, n_sub=n_sub),
        grid=(b, t // tm),
        in_specs=[row(D_MODEL), pl.BlockSpec((1, 1, 3 * D_MODEL), lambda i, j: (i, 0, 0)),
                  row(W_A), row(W_B),
                  const((W_A + W_B, D_MODEL)), const((1, D_MODEL)), const((1, D_MODEL))],
        out_specs=row(D_MODEL),
        out_shape=jax.ShapeDtypeStruct((b, t, D_MODEL), F32),
        compiler_params=pltpu.CompilerParams(dimension_semantics=("parallel", "parallel"),
                                             vmem_limit_bytes=VMEM_LIMIT),
        name="out_proj_ln",
    )(h, mod3, ya, yb, wo, g, bb)


def _rot_partner(w):
    w4 = w.reshape(w.shape[:-1] + (2, 2, ROPE_FREQS))
    return jnp.stack([-w4[..., 1, :], w4[..., 0, :]], axis=-2).reshape(w.shape)


def _rope_tables(seq, scale_keep, scale_rope):
    n_rows = seq // GRID_W
    rowp = np.repeat(np.arange(n_rows, dtype=np.float32), GRID_W)
    colp = np.tile(np.arange(GRID_W, dtype=np.float32), n_rows)
    inv = (np.float32(ROPE_BASE) ** (-np.arange(ROPE_FREQS, dtype=np.float32) / np.float32(ROPE_FREQS)))
    ang = np.stack([rowp[:, None] * inv, colp[:, None] * inv], axis=1).astype(np.float32)
    cos = np.broadcast_to(np.cos(ang)[:, :, None, :], (seq, 2, 2, ROPE_FREQS)).reshape(seq, DR_A)
    sin = np.broadcast_to(np.sin(ang)[:, :, None, :], (seq, 2, 2, ROPE_FREQS)).reshape(seq, DR_A)
    z32 = np.zeros((seq, DR_A), np.float32)
    t1 = np.concatenate([np.full((seq, DN_A), scale_keep, np.float32), z32, cos * scale_rope], axis=1)
    t2 = np.concatenate([np.zeros((seq, DN_A), np.float32), z32, sin * scale_rope], axis=1)
    return jnp.asarray(t1, F32), jnp.asarray(t2, F32)


def _block_diag(w):
    nb, bs, _ = w.shape
    n = nb * bs
    rows = w.reshape(n, bs)
    ri = lax.broadcasted_iota(jnp.int32, (n, n), 0)
    ci = lax.broadcasted_iota(jnp.int32, (n, n), 1)
    out = jnp.zeros((n, n), w.dtype)
    for o in range(bs):
        out = out + jnp.where((ri // bs == ci // bs) & (ci % bs == o), rows[:, o:o + 1], 0.0)
    return out


def kernel(x, c, ctx, c_ctx, ln_in_g, ln_in_b, w_ada, b_ada, w_in, g_qa, w_qb, g_kva, w_kvb, conv_w, conv_b, w_mq, w_mk, w_mv, w_gate, b_gate, mh_g, skip, w_out, ln_g, ln_b):
    b, t, _ = x.shape
    l = 0
    r2 = lambda v: v.reshape(1, -1)

    wina, winr = _win_call(jnp.swapaxes(w_in[l], 0, 1))
    wq3 = w_qb[l].reshape(Q_LORA, H_A, DN_A + DR_A)
    wq_r = wq3[..., DN_A:]
    wq = jnp.concatenate([wq3[..., :DN_A], _rot_partner(wq_r), wq_r], axis=-1)
    wq = wq.reshape(Q_LORA, H_A * SLAB).astype(BF16)
    wkv3 = w_kvb[l].reshape(KV_LORA, H_A, DN_A + DV_A)
    wkn = jnp.concatenate([wkv3[..., :DN_A], jnp.zeros((KV_LORA, H_A, SLAB - DN_A), F32)], axis=-1)
    wkn = wkn.reshape(KV_LORA, H_A * SLAB).astype(BF16)
    wvt = wkv3[..., DN_A:].reshape(KV_LORA, W_A).T.astype(BF16)
    wmq = _block_diag(w_mq[l])
    wmk = _block_diag(w_mk[l])
    wmv = _block_diag(w_mv[l])
    perm = jnp.array([0, 1, 2, 3, 8, 9, 10, 11, 4, 5, 6, 7, 12, 13, 14, 15])
    wg = jnp.pad(w_gate[l][:, perm], ((0, 0), (0, LANES - 2 * N_HD))).astype(BF16)
    bg = jnp.pad(b_gate[l][perm], (0, LANES - 2 * N_HD)).reshape(1, LANES)
    wgc, wgm = _gate_fold_call(wmq.astype(BF16), wmk.astype(BF16), wmv.astype(BF16), wg)
    wmk_s = (wmk * (DH_B ** -0.5)).astype(BF16)
    wo = w_out[l].astype(BF16)

    sm_scale = (DN_A + DR_A) ** -0.5 * LOG2_E
    t1q, t2q = _rope_tables(t, sm_scale, sm_scale)
    t1k, t2k = _rope_tables(t, 0.0, 1.0)
    n_ctx = ctx.shape[1]
    t1kc = jnp.asarray(np.concatenate([np.zeros((n_ctx, DN_A + DR_A), np.float32),
                                       np.ones((n_ctx, DR_A), np.float32)], axis=1))
    t2kc = jnp.asarray(np.zeros((n_ctx, SLAB), np.float32))

    cc = jnp.concatenate([c, c_ctx[None, :], jnp.zeros((16 - b - 1, D_MODEL), F32)], axis=0)
    mod = _ada_call(cc, w_ada[l], r2(b_ada[l]))
    mod3 = mod.reshape(16, 1, 3 * D_MODEL)

    lng, lnb = r2(ln_in_g), r2(ln_in_b)
    shared = (wina, winr, r2(g_qa[l]), wq, r2(g_kva[l]), wkn, wvt)
    q, kl, vtl, xm, sza, som, szm, h = _proj_call(x, mod3, None, lng, lnb, *shared, t1q, t2q, t1k, t2k,
                                                  tm=1024, n_sub=2, latent=True)
    kc, vtc, xmc = _proj_call(ctx, mod3, b, lng, lnb, *shared, t1kc, t2kc, t1kc, t2kc,
                              tm=n_ctx, n_sub=1, latent=False)

    hs, xc = _mlstm_call(xm, xmc, conv_w[l], r2(conv_b[l]), wmq.astype(BF16), wmk_s, wmv.astype(BF16),
                         wgc, wgm, bg)
    ya, yb = _attn_call(q, kc, kl, vtc, vtl, sza, hs, som, szm, xc, r2(mh_g[l]), r2(skip[l]), tq=512)
    return _out_call(h, mod3, ya, yb, wo, r2(ln_g[l]), r2(ln_b[l]), tm=1024, n_sub=4)
```

```python
import functools

import numpy as np
import jax
import jax.numpy as jnp
from jax import lax
from jax.experimental import pallas as pl
from jax.experimental.pallas import tpu as pltpu

F32 = jnp.float32
BF16 = jnp.bfloat16

D_MODEL = 1024
CTX_LEN = 256
GRID_W = 64
H_A, DN_A, DR_A, DV_A = 8, 64, 32, 64
W_A = H_A * DV_A
Q_LORA, KV_LORA = 256, 128
ROPE_FREQS = DR_A // 4
ROPE_BASE = 10000.0
H_B, DH_B = 4, 128
W_B = H_B * DH_B
QKV_BS = 4
DEPTH = 1
ALPHA = (2.0 * DEPTH) ** 0.25
LN_EPS = 1e-5
RMS_EPS = 1e-6
LOG2_E = 1.4426950408889634

LANES = 128
MXU_TILE = 256
SLAB = LANES
DV_EXT = DV_A + 16
ATTN_KEY_CHUNK = MXU_TILE
CHUNK = 256
N_HD = 2 * H_B
ST_ROWS = DH_B + 16
VMEM_LIMIT = 56 * 1024 * 1024

_NT = (((1,), (1,)), ((), ()))


def _dot(a, b):
    return jnp.dot(a, b, preferred_element_type=F32)


def _dot_nt(a, b):
    return lax.dot_general(a, b, _NT, preferred_element_type=F32)


def _layer_norm(x):
    mu = jnp.mean(x, axis=-1, keepdims=True)
    xc = x - mu
    var = jnp.mean(xc * xc, axis=-1, keepdims=True)
    return xc * lax.rsqrt(var + LN_EPS)


def _rms_norm(x, g):
    return (x * lax.rsqrt(jnp.mean(x * x, axis=-1, keepdims=True) + RMS_EPS)) * g


def _silu(x):
    return x * jax.nn.sigmoid(x)


def _log_sigmoid(x):
    return jnp.minimum(x, 0.0) - jnp.log1p(jnp.exp(-jnp.abs(x)))


def _ada_kernel(cc_ref, w_ref, b_ref, o_ref):
    a = _silu(cc_ref[...]).astype(BF16)
    o_ref[...] = _dot(a, w_ref[...].astype(BF16)) + b_ref[...]


def _ada_call(cc, w_ada, b_ada):
    n = w_ada.shape[1]
    tn = 1024
    return pl.pallas_call(
        _ada_kernel,
        grid=(n // tn,),
        in_specs=[pl.BlockSpec((16, D_MODEL), lambda j: (0, 0)),
                  pl.BlockSpec((D_MODEL, tn), lambda j: (0, j)),
                  pl.BlockSpec((1, tn), lambda j: (0, j))],
        out_specs=pl.BlockSpec((16, tn), lambda j: (0, j)),
        out_shape=jax.ShapeDtypeStruct((16, n), F32),
        compiler_params=pltpu.CompilerParams(dimension_semantics=("arbitrary",),
                                             vmem_limit_bytes=VMEM_LIMIT),
        name="ada_mod",
    )(cc, w_ada, b_ada)


_S_KVA, _S_KR = Q_LORA + KV_LORA, Q_LORA + KV_LORA + DR_A


def _win_kernel(wt_ref, wina_ref, winr_ref):
    n_r = winr_ref.shape[1]
    for j in range(_S_KVA // LANES):
        wina_ref[:, j * LANES:(j + 1) * LANES] = wt_ref[j * LANES:(j + 1) * LANES, :].T.astype(BF16)
    kr = wt_ref[_S_KVA:_S_KR, :]
    a0, b0, a1, b1 = (kr[i * ROPE_FREQS:(i + 1) * ROPE_FREQS, :] for i in range(4))
    blk = jnp.concatenate([jnp.zeros((DN_A, D_MODEL), F32), -b0, a0, -b1, a1, kr], axis=0)
    wina_ref[:, _S_KVA:_S_KVA + LANES] = blk.T.astype(BF16)
    for j in range(n_r // MXU_TILE):
        rows = slice(_S_KR + j * MXU_TILE, _S_KR + (j + 1) * MXU_TILE)
        winr_ref[:, j * MXU_TILE:(j + 1) * MXU_TILE] = wt_ref[rows, :].T.astype(BF16)


def _win_call(wt):
    n_in, d = wt.shape
    return pl.pallas_call(
        _win_kernel,
        out_shape=[jax.ShapeDtypeStruct((d, _S_KVA + LANES), BF16),
                   jax.ShapeDtypeStruct((d, n_in - _S_KR), BF16)],
        compiler_params=pltpu.CompilerParams(vmem_limit_bytes=VMEM_LIMIT),
        name="win_relayout",
    )(wt)


def _rope_slab(s, t1, t2):
    return s * t1 + pltpu.roll(s, 32, 1) * t2


def _proj_kernel(x_ref, mod_ref, lng_ref, lnb_ref, wina_ref, winr_ref, gqa_ref, wq_ref, gkva_ref,
                 wkn_ref, wvt_ref, t1q_ref, t2q_ref, t1k_ref, t2k_ref, *out_refs, latent, n_sub):
    if latent:
        q_ref, k_ref, vt_ref, xm_ref, sza_ref, som_ref, szm_ref, h_ref = out_refs
    else:
        k_ref, vt_ref, xm_ref = out_refs
    sub = x_ref.shape[1] // n_sub
    mod = mod_ref[0]
    shift, scale1 = mod[:, 0:D_MODEL], 1.0 + mod[:, D_MODEL:2 * D_MODEL]
    ones_rows = (lax.broadcasted_iota(jnp.int32, (DV_EXT - DV_A, sub), 0) == 0).astype(BF16)

    def norm_stage(s):
        rows = slice(s * sub, (s + 1) * sub)
        h = _layer_norm(x_ref[0, rows, :]) * lng_ref[...] + lnb_ref[...]
        if latent:
            h_ref[0, rows, :] = h
        return (h * scale1 + shift).astype(BF16)

    def proj_stage(s, u):
        rows = slice(s * sub, (s + 1) * sub)
        p0 = _dot(u, wina_ref[...])
        kvn = _rms_norm(p0[:, 256:384], gkva_ref[...]).astype(BF16)
        kn = _dot(kvn, wkn_ref[...])
        krr = _rope_slab(p0[:, 384:512], t1k_ref[rows, :], t2k_ref[rows, :])
        for hh in range(H_A):
            k_ref[0, hh, rows, :] = (kn[:, hh * SLAB:(hh + 1) * SLAB] + krr).astype(BF16)
        vt = _dot_nt(wvt_ref[...], kvn).astype(BF16)
        for hh in range(H_A):
            vt_ref[0, hh, 0:DV_A, rows] = vt[hh * DV_A:(hh + 1) * DV_A, :]
            vt_ref[0, hh, DV_A:DV_EXT, rows] = ones_rows
        xm_ref[0, rows, :] = _dot(u, winr_ref[:, 512:1024]).astype(BF16)
        if latent:
            qn = _rms_norm(p0[:, 0:256], gqa_ref[...]).astype(BF16)
            qs = _dot(qn, wq_ref[...])
            t1q, t2q = t1q_ref[rows, :], t2q_ref[rows, :]
            for hh in range(H_A):
                q_ref[0, hh, rows, :] = _rope_slab(qs[:, hh * SLAB:(hh + 1) * SLAB], t1q, t2q).astype(BF16)
            sza_ref[0, rows, :] = _silu(_dot(u, winr_ref[:, 0:512])).astype(BF16)
            som_ref[0, rows, :] = jax.nn.sigmoid(_dot(u, winr_ref[:, 1024:1536])).astype(BF16)
            szm_ref[0, rows, :] = _silu(_dot(u, winr_ref[:, 1536:2048])).astype(BF16)

    u = norm_stage(0)
    for s in range(n_sub):
        u_next = norm_stage(s + 1) if s + 1 < n_sub else None
        proj_stage(s, u)
        u = u_next


def _proj_call(x, mod3, mod_row, lng, lnb, wina, winr, gqa, wq, gkva, wkn, wvt, t1q, t2q, t1k, t2k, *,
               tm, n_sub, latent):
    b, t, _ = x.shape
    const = lambda shape: pl.BlockSpec(shape, lambda i, j: (0,) * len(shape))
    tab = pl.BlockSpec((tm, SLAB), lambda i, j: (j, 0))
    row = lambda w: pl.BlockSpec((1, tm, w), lambda i, j: (i, j, 0))
    if mod_row is None:
        mod_spec = pl.BlockSpec((1, 1, 3 * D_MODEL), lambda i, j: (i, 0, 0))
    else:
        mod_spec = pl.BlockSpec((1, 1, 3 * D_MODEL), lambda i, j: (mod_row, 0, 0))
    in_specs = [row(D_MODEL), mod_spec, const((1, D_MODEL)), const((1, D_MODEL)),
                const(wina.shape), const(winr.shape), const((1, Q_LORA)), const((Q_LORA, H_A * SLAB)),
                const((1, KV_LORA)), const((KV_LORA, H_A * SLAB)), const((W_A, KV_LORA)),
                tab, tab, tab, tab]
    k_spec = pl.BlockSpec((1, H_A, tm, SLAB), lambda i, j: (i, 0, j, 0))
    vt_spec = pl.BlockSpec((1, H_A, DV_EXT, tm), lambda i, j: (i, 0, 0, j))
    k_shape = jax.ShapeDtypeStruct((b, H_A, t, SLAB), BF16)
    vt_shape = jax.ShapeDtypeStruct((b, H_A, DV_EXT, t), BF16)
    half = jax.ShapeDtypeStruct((b, t, W_B), BF16)
    if latent:
        out_specs = [k_spec, k_spec, vt_spec, row(W_B), row(W_A), row(W_B), row(W_B), row(D_MODEL)]
        out_shape = [k_shape, k_shape, vt_shape, half, half, half, half,
                     jax.ShapeDtypeStruct((b, t, D_MODEL), F32)]
    else:
        out_specs = [k_spec, vt_spec, row(W_B)]
        out_shape = [k_shape, vt_shape, half]
    return pl.pallas_call(
        functools.partial(_proj_kernel, latent=latent, n_sub=n_sub),
        grid=(b, t // tm),
        in_specs=in_specs, out_specs=out_specs, out_shape=out_shape,
        compiler_params=pltpu.CompilerParams(dimension_semantics=("parallel", "parallel"),
                                             vmem_limit_bytes=VMEM_LIMIT),
        name="in_proj_latent" if latent else "in_proj_ctx",
    )(x, mod3, lng, lnb, wina, winr, gqa, wq, gkva, wkn, wvt, t1q, t2q, t1k, t2k)


def _attn_kernel(q_ref, kc_ref, kl_ref, vtc_ref, vtl_ref, sza_ref, hs_ref, som_ref, szm_ref, xc_ref,
                 mhg_ref, skip_ref, o_ref, yb_ref, s_buf, ot_s):
    n_ctx, t = kc_ref.shape[2], kl_ref.shape[2]
    kc = ATTN_KEY_CHUNK

    def scores(h, slot):
        qh = q_ref[0, h]
        sc = _dot_nt(kc_ref[0, h], qh)
        sk = _dot_nt(kl_ref[0, h], qh)
        s_buf[slot, 0:n_ctx, :] = sc
        s_buf[slot, n_ctx:n_ctx + t, :] = sk
        return jnp.maximum(jnp.max(sc, axis=0, keepdims=True), jnp.max(sk, axis=0, keepdims=True))

    def values(h, slot, m):
        chunks = [(vtc_ref, c * kc, c * kc) for c in range(n_ctx // kc)]
        chunks += [(vtl_ref, c * kc, n_ctx + c * kc) for c in range(t // kc)]
        acc = None
        for vref, v0, s0 in chunks:
            p = jnp.exp2(s_buf[slot, s0:s0 + kc, :] - m).astype(BF16)
            d = _dot(vref[0, h, :, v0:v0 + kc], p)
            acc = d if acc is None else acc + d
        rows = pl.ds(pl.multiple_of(h * DV_A, DV_A), DV_A)
        ot_s[rows, :] = acc[0:DV_A, :] / acc[DV_A:DV_A + 1, :]

    def cell_out(pc):
        rows = pl.ds(pl.multiple_of(pc * LANES, LANES), LANES)
        yb_ref[0, rows, :] = _mlstm_out(hs_ref[0, pc], som_ref[0, rows, :], szm_ref[0, rows, :],
                                        xc_ref[0, rows, :], mhg_ref[...], skip_ref[...])

    def body(i, m_even):
        h = 2 * i
        m_odd = scores(h + 1, 1)
        cell_out(i + 1)
        values(h, 0, m_even)
        m_even = scores(h + 2, 0)
        values(h + 1, 1, m_odd)
        return m_even

    assert hs_ref.shape[1] == H_A // 2
    m_first = scores(0, 0)
    cell_out(0)
    m_even = lax.fori_loop(0, H_A // 2 - 1, body, m_first)
    m_odd = scores(H_A - 1, 1)
    values(H_A - 2, 0, m_even)
    values(H_A - 1, 1, m_odd)
    o_ref[0] = (ot_s[...].T * sza_ref[0].astype(F32)).astype(BF16)


def _attn_call(q, kc, kl, vtc, vtl, sza, hs, som, szm, xc, mhg, skip, *, tq):
    b, _, t, _ = q.shape
    n_ctx = kc.shape[2]
    head_blk = lambda n, w: pl.BlockSpec((1, H_A, n, w), lambda i, j: (i, 0, 0, 0))
    const = lambda shape: pl.BlockSpec(shape, lambda i, j: (0,) * len(shape))
    row = lambda w: pl.BlockSpec((1, tq, w), lambda i, j: (i, j, 0))
    half = jax.ShapeDtypeStruct((b, t, W_A), BF16)
    return pl.pallas_call(
        _attn_kernel,
        grid=(b, t // tq),
        in_specs=[pl.BlockSpec((1, H_A, tq, SLAB), lambda i, j: (i, 0, j, 0)),
                  head_blk(n_ctx, SLAB), head_blk(t, SLAB),
                  head_blk(DV_EXT, n_ctx), head_blk(DV_EXT, t), row(W_A),
                  pl.BlockSpec((1, tq // LANES, W_B, LANES), lambda i, j: (i, j, 0, 0)),
                  row(W_B), row(W_B), row(W_B), const((1, W_B)), const((1, W_B))],
        out_specs=[row(W_A), row(W_B)],
        out_shape=[half, jax.ShapeDtypeStruct((b, t, W_B), BF16)],
        scratch_shapes=[pltpu.VMEM((2, n_ctx + t, tq), F32),
                        pltpu.VMEM((W_A, tq), F32)],
        compiler_params=pltpu.CompilerParams(dimension_semantics=("parallel", "parallel"),
                                             vmem_limit_bytes=VMEM_LIMIT),
        name="mla_attention",
    )(q, kc, kl, vtc, vtl, sza, hs, som, szm, xc, mhg, skip)


def _gate_fold_kernel(wq_ref, wk_ref, wv_ref, wg_ref, wgc_ref, wgm_ref):
    wgc_ref[...] = (_dot(wq_ref[...], wg_ref[0:W_B, :])
                    + _dot(wk_ref[...], wg_ref[W_B:2 * W_B, :])).astype(BF16)
    wgm_ref[...] = _dot(wv_ref[...], wg_ref[2 * W_B:3 * W_B, :]).astype(BF16)


def _gate_fold_call(wq, wk, wv, wg):
    shp = jax.ShapeDtypeStruct((W_B, LANES), BF16)
    return pl.pallas_call(_gate_fold_kernel, out_shape=[shp, shp],
                          compiler_params=pltpu.CompilerParams(vmem_limit_bytes=VMEM_LIMIT),
                          name="mlstm_gate_fold")(wq, wk, wv, wg)


_PAD = 8


def _lane_scans(rows_per_chunk, combine, fill, out):
    npc = CHUNK // LANES
    flat = [x[:, i * LANES:(i + 1) * LANES] for x in rows_per_chunk for i in range(npc)]
    lane = lax.broadcasted_iota(jnp.int32, flat[0].shape, 1)
    pre, suf, sh = list(flat), list(flat), 1
    while sh < LANES:
        pre = [combine(p, jnp.where(lane >= sh, pltpu.roll(p, sh, 1), fill)) for p in pre]
        suf = [combine(s, jnp.where(lane < LANES - sh, pltpu.roll(s, LANES - sh, 1), fill)) for s in suf]
        sh *= 2
        yield
    for c in range(len(rows_per_chunk)):
        p, s = pre[c * npc:(c + 1) * npc], suf[c * npc:(c + 1) * npc]
        tot = [x[:, LANES - 1:LANES] for x in p]
        run = None
        for i in range(npc):
            if run is not None:
                p[i] = combine(p[i], run)
            run = tot[i] if run is None else combine(run, tot[i])
        run = None
        for i in reversed(range(npc)):
            if run is not None:
                s[i] = combine(s[i], run)
            run = tot[i] if run is None else combine(run, tot[i])
        out.append((jnp.concatenate(p, axis=1), jnp.concatenate(s, axis=1)))


def _gate_tables(gts, dests):
    L = CHUNK
    fwd_rows = lax.broadcasted_iota(jnp.int32, (N_HD, L), 0) < H_B
    lis = [gt[0:N_HD, :] for gt in gts]
    sums = []
    yield from _lane_scans([_log_sigmoid(gt[N_HD:2 * N_HD, :]) for gt in gts], jnp.add, 0.0, sums)
    cums = [jnp.where(fwd_rows, ps, ss) for ps, ss in sums]
    rs = [li - cum for li, cum in zip(lis, cums)]
    maxs = []
    yield from _lane_scans(rs, jnp.maximum, -jnp.inf, maxs)
    for (grow_ref, c, gcol_ref, r0), cum, r, (pm, sm) in zip(dests, cums, rs, maxs):
        grow_ref[c] = jnp.concatenate([cum, r, jnp.where(fwd_rows, pm, sm)], axis=0)
        gcol_ref[r0:r0 + L, :] = jnp.concatenate([r, jnp.zeros((LANES - N_HD, L), F32)], axis=0).T


def _fill_pad(pad_s, xm_ref, n):
    zrow = jnp.zeros((_PAD, W_B), F32)
    pad_s[0:_PAD, :] = zrow
    pad_s[_PAD:_PAD + n, :] = xm_ref[...].astype(F32)
    pad_s[_PAD + n:2 * _PAD + n, :] = zrow


def _conv_gate_stage(c, pad_s, xm_ref, cw_ref, cb_ref, wgc_ref, wgm_ref, bg_ref, xc_out, xcb_s):
    L = CHUNK
    rows = slice(c * L, (c + 1) * L)
    r0 = c * L + _PAD
    row_id = lax.broadcasted_iota(jnp.int32, (L, W_B), 0)
    xcur = pad_s[r0:r0 + L, :]
    xprev = jnp.where(row_id == 0, pad_s[r0 - 1:r0, :], pltpu.roll(xcur, 1, 0))
    xnext = jnp.where(row_id == L - 1, pad_s[r0 + L:r0 + L + 1, :], pltpu.roll(xcur, L - 1, 0))
    pre = cb_ref[...] + xprev * cw_ref[0:1, :] + xcur * cw_ref[1:2, :] + xnext * cw_ref[2:3, :]
    xc = _silu(pre)
    xcb = xc.astype(BF16)
    if xc_out is not None:
        xc_out[rows, :] = xcb
    xcb_s[rows, :] = xcb
    g = _dot(xcb, wgc_ref[...]) + _dot(xm_ref[rows, :], wgm_ref[...]) + bg_ref[...]
    return g.T[0:2 * N_HD, :]


def _headwise_stage(c, xm_ref, xcb_s, wq_ref, wk_ref, wv_ref, q_s, k_s, vt_s):
    L = CHUNK
    rows = slice(c * L, (c + 1) * L)
    for hf in range(W_B // MXU_TILE):
        cs = slice(hf * MXU_TILE, (hf + 1) * MXU_TILE)
        xcb = xcb_s[rows, cs]
        if q_s is not None:
            q_s[rows, cs] = _dot(xcb, wq_ref[cs, cs]).astype(BF16)
        k_s[rows, cs] = _dot(xcb, wk_ref[cs, cs]).astype(BF16)
        vt_s[c, cs, :] = _dot(xm_ref[rows, cs], wv_ref[cs, cs]).T.astype(BF16)


_TAB_ROWS = 5 * N_HD


def _chain_tables(growc_s, grow_s, tab_s, nc):
    L = CHUNK
    fwd_rows = lax.broadcasted_iota(jnp.int32, (N_HD, L), 0) < H_B
    gc = growc_s[0]
    steps = [(gc, gc)] + [(grow_s[i], grow_s[nc - 1 - i]) for i in range(nc)]
    parts = []
    for g_f, g_b in steps:
        pick = lambda a: jnp.where(fwd_rows, g_f[a * N_HD:(a + 1) * N_HD, :], g_b[a * N_HD:(a + 1) * N_HD, :])
        cum, r, pm = pick(0), pick(1), pick(2)
        b_end = jnp.where(fwd_rows[:, 0:1], cum[:, L - 1:L], cum[:, 0:1])
        w = r + b_end
        parts.append((cum, pm, b_end, w, jnp.max(w, axis=1, keepdims=True)))
    m0 = jnp.zeros((N_HD, 1), F32)
    for idx, (cum, pm, b_end, w, wmax) in enumerate(parts):
        m_new = jnp.maximum(b_end + m0, wmax)
        mu = jnp.maximum(m0, pm)
        decay = jnp.broadcast_to(jnp.exp(b_end + m0 - m_new), (N_HD, L))
        tab_s[idx] = jnp.concatenate([mu, jnp.exp(m0 - mu), jnp.exp(-(cum + mu)),
                                      jnp.exp(w - m_new), decay], axis=0)
        m0 = m_new


def _tab(tab, a, hd):
    return tab[a * N_HD + hd:a * N_HD + hd + 1, :]


def _state_update(st_ref, hd, vt_ext, k_c, ws_row, decay_hd):
    vw = (vt_ext.astype(F32) * ws_row).astype(BF16)
    st_ref[hd] = decay_hd * st_ref[hd] + _dot(vw, k_c)


def _mlstm_kernel(xm_ref, xmc_ref, cw_ref, cb_ref, wq_ref, wk_ref, wv_ref, wgc_ref, wgm_ref, bg_ref,
                  hs_ref, xc_ref,
                  pad_s, xcb_s, q_s, k_s, vt_s, gcol_s, grow_s,
                  padc_s, xcbc_s, kc_s, vtc_s, gcolc_s, growc_s, tab_s, st_s):
    L = CHUNK
    t = xm_ref.shape[1]
    nc = t // L
    row_i = lax.broadcasted_iota(jnp.int32, (L, L), 0)
    col_i = lax.broadcasted_iota(jnp.int32, (L, L), 1)
    tri = (row_i <= col_i, row_i >= col_i)
    ones_rows = (lax.broadcasted_iota(jnp.int32, (ST_ROWS - DH_B, L), 0) == 0).astype(BF16)
    hsl = lambda hh: slice(hh * DH_B, (hh + 1) * DH_B)
    xm, xmc = xm_ref.at[0], xmc_ref.at[0]
    gate_w = (cw_ref, cb_ref, wgc_ref, wgm_ref, bg_ref)

    _fill_pad(padc_s, xmc, CTX_LEN)
    _fill_pad(pad_s, xm, t)
    gts = [_conv_gate_stage(0, padc_s, xmc, *gate_w, None, xcbc_s)]
    gts += [_conv_gate_stage(c, pad_s, xm, *gate_w, xc_ref.at[0], xcb_s) for c in range(nc)]
    dests = [(growc_s, 0, gcolc_s, 0)] + [(grow_s, c, gcol_s, c * L) for c in range(nc)]
    tables = _gate_tables(gts, dests)
    _headwise_stage(0, xmc, xcbc_s, wq_ref, wk_ref, wv_ref, None, kc_s, vtc_s)
    for c in range(nc):
        next(tables, None)
        next(tables, None)
        _headwise_stage(c, xm, xcb_s, wq_ref, wk_ref, wv_ref, q_s, k_s, vt_s)
    for _ in tables:
        pass

    _chain_tables(growc_s, grow_s, tab_s, nc)
    st_s[...] = jnp.zeros(st_s.shape, F32)
    tab = tab_s[0]
    for hd in range(N_HD):
        hh = hd % H_B
        vt_ext = jnp.concatenate([vtc_s[0, hsl(hh), :], ones_rows], axis=0)
        _state_update(st_s, hd, vt_ext, kc_s[:, hsl(hh)], _tab(tab, 3, hd), _tab(tab, 4, hd)[:, 0:DH_B])

    def body(i, carry, *, accumulate):
        ci = (i, nc - 1 - i)
        rows = tuple(pl.ds(pl.multiple_of(c * L, L), L) for c in ci)
        tab = tab_s[i + 1]
        gcols = tuple(gcol_s[r, :] for r in rows)
        live = {}

        def stage_scores(hd):
            d, hh = hd // H_B, hd % H_B
            k_c = k_s[rows[d], hsl(hh)]
            vt_ext = jnp.concatenate([vt_s[ci[d], hsl(hh), :], ones_rows], axis=0)
            lhs = jnp.concatenate([k_c, st_s[hd].astype(BF16)], axis=0)
            live[hd] = (k_c, vt_ext, _dot_nt(lhs, q_s[rows[d], hsl(hh)]))

        def stage_gate(hd):
            d = hd // H_B
            k_c, vt_ext, res = live[hd]
            rcol = gcols[d][:, hd:hd + 1]
            e = jnp.exp(jnp.where(tri[d], rcol - _tab(tab, 0, hd), -jnp.inf))
            p = (res[0:L, :] * e).astype(BF16)
            _state_update(st_s, hd, vt_ext, k_c, _tab(tab, 3, hd), _tab(tab, 4, hd)[:, 0:DH_B])
            live[hd] = (vt_ext, res[L:L + ST_ROWS, :], p)

        def stage_out(hd):
            d, hh = hd // H_B, hd % H_B
            vt_ext, inter, p = live.pop(hd)
            tot = _tab(tab, 1, hd) * inter + _dot(vt_ext, p)
            den = tot[DH_B:DH_B + 1, :]
            h_t = tot[0:DH_B, :] / jnp.maximum(jnp.abs(den), _tab(tab, 2, hd))
            for pc in range(L // LANES):
                piece = h_t[:, pc * LANES:(pc + 1) * LANES]
                if accumulate:
                    hs_ref[0, ci[d] * (L // LANES) + pc, hsl(hh), :] += piece
                else:
                    hs_ref[0, ci[d] * (L // LANES) + pc, hsl(hh), :] = piece

        stages = (stage_scores, stage_gate, stage_out)
        for step in range(N_HD + len(stages) - 1):
            for si in range(len(stages)):
                if 0 <= step - si < N_HD:
                    stages[si](step - si)
        return carry

    lax.fori_loop(0, nc // 2, functools.partial(body, accumulate=False), 0)
    lax.fori_loop(nc // 2, nc, functools.partial(body, accumulate=True), 0)


def _mlstm_out(hs_t, som, szm, xc, mhg, skip):
    hsum = hs_t.T * som.astype(F32)
    parts = [_layer_norm(hsum[:, hh * DH_B:(hh + 1) * DH_B]) for hh in range(H_B)]
    hb = jnp.concatenate(parts, axis=1) * mhg + skip * xc.astype(F32)
    return (hb * szm.astype(F32)).astype(BF16)


def _mlstm_call(xm, xmc, conv_w, conv_b, wq, wk, wv, wgc, wgm, bg):
    b, t, _ = xm.shape
    nc = t // CHUNK
    assert nc % 2 == 0
    seq = lambda n: pl.BlockSpec((1, n, W_B), lambda i: (i, 0, 0))
    const = lambda shape: pl.BlockSpec(shape, lambda i: (0,) * len(shape))
    scratch = [
        pltpu.VMEM((t + 2 * _PAD, W_B), F32),
        pltpu.VMEM((t, W_B), BF16),
        pltpu.VMEM((t, W_B), BF16),
        pltpu.VMEM((t, W_B), BF16),
        pltpu.VMEM((nc, W_B, CHUNK), BF16),
        pltpu.VMEM((t, LANES), F32),
        pltpu.VMEM((nc, 3 * N_HD, CHUNK), F32),
        pltpu.VMEM((CTX_LEN + 2 * _PAD, W_B), F32),
        pltpu.VMEM((CTX_LEN, W_B), BF16),
        pltpu.VMEM((CTX_LEN, W_B), BF16),
        pltpu.VMEM((1, W_B, CHUNK), BF16),
        pltpu.VMEM((CTX_LEN, LANES), F32),
        pltpu.VMEM((1, 3 * N_HD, CHUNK), F32),
        pltpu.VMEM((nc + 1, _TAB_ROWS, CHUNK), F32),
        pltpu.VMEM((N_HD, ST_ROWS, DH_B), F32),
    ]
    return pl.pallas_call(
        _mlstm_kernel,
        grid=(b,),
        in_specs=[seq(t), seq(CTX_LEN), const((3, W_B)), const((1, W_B)),
                  const((W_B, W_B)), const((W_B, W_B)), const((W_B, W_B)),
                  const((W_B, LANES)), const((W_B, LANES)), const((1, LANES))],
        out_specs=[pl.BlockSpec((1, t // LANES, W_B, LANES), lambda i: (i, 0, 0, 0)), seq(t)],
        out_shape=[jax.ShapeDtypeStruct((b, t // LANES, W_B, LANES), F32),
                   jax.ShapeDtypeStruct((b, t, W_B), BF16)],
        scratch_shapes=scratch,
        compiler_params=pltpu.CompilerParams(dimension_semantics=("arbitrary",),
                                             vmem_limit_bytes=VMEM_LIMIT),
        name="mlstm_bidir",
    )(xm, xmc, conv_w, conv_b, wq, wk, wv, wgc, wgm, bg)


def _out_kernel(h_ref, mod_ref, ya_ref, yb_ref, wo_ref, g_ref, b_ref, o_ref, *, n_sub):
    sub = h_ref.shape[1] // n_sub
    gate = mod_ref[0][:, 2 * D_MODEL:3 * D_MODEL]

    def mix_stage(s):
        rows = slice(s * sub, (s + 1) * sub)
        return (_dot(ya_ref[0, rows, :], wo_ref[0:W_A, :])
                + _dot(yb_ref[0, rows, :], wo_ref[W_A:W_A + W_B, :]))

    def norm_stage(s, y):
        rows = slice(s * sub, (s + 1) * sub)
        o_ref[0, rows, :] = _layer_norm(ALPHA * h_ref[0, rows, :] + gate * y) * g_ref[...] + b_ref[...]

    y = mix_stage(0)
    for s in range(n_sub):
        y_next = mix_stage(s + 1) if s + 1 < n_sub else None
        norm_stage(s, y)
        y = y_next


def _out_call(h, mod3, ya, yb, wo, g, bb, *, tm, n_sub):
    b, t, _ = h.shape
    const = lambda shape: pl.BlockSpec(shape, lambda i, j: (0,) * len(shape))
    row = lambda w: pl.BlockSpec((1, tm, w), lambda i, j: (i, j, 0))
    return pl.pallas_call(
        functools.partial(_out_kernel, n_sub=n_sub),
        grid=(b, t // tm),
        in_specs=[row(D_MODEL), pl.BlockSpec((1, 1, 3 * D_MODEL), lambda i, j: (i, 0, 0)),
                  row(W_A), row(W_B),
                  const((W_A + W_B, D_MODEL)), const((1, D_MODEL)), const((1, D_MODEL))],
        out_specs=row(D_MODEL),
        out_shape=jax.ShapeDtypeStruct((b, t, D_MODEL), F32),
        compiler_params=pltpu.CompilerParams(dimension_semantics=("parallel", "parallel"),
                                             vmem_limit_bytes=VMEM_LIMIT),
        name="out_proj_ln",
    )(h, mod3, ya, yb, wo, g, bb)


def _rot_partner(w):
    w4 = w.reshape(w.shape[:-1] + (2, 2, ROPE_FREQS))
    return jnp.stack([-w4[..., 1, :], w4[..., 0, :]], axis=-2).reshape(w.shape)


def _rope_tables(seq, scale_keep, scale_rope):
    n_rows = seq // GRID_W
    rowp = np.repeat(np.arange(n_rows, dtype=np.float32), GRID_W)
    colp = np.tile(np.arange(GRID_W, dtype=np.float32), n_rows)
    inv = (np.float32(ROPE_BASE) ** (-np.arange(ROPE_FREQS, dtype=np.float32) / np.float32(ROPE_FREQS)))
    ang = np.stack([rowp[:, None] * inv, colp[:, None] * inv], axis=1).astype(np.float32)
    cos = np.broadcast_to(np.cos(ang)[:, :, None, :], (seq, 2, 2, ROPE_FREQS)).reshape(seq, DR_A)
    sin = np.broadcast_to(np.sin(ang)[:, :, None, :], (seq, 2, 2, ROPE_FREQS)).reshape(seq, DR_A)
    z32 = np.zeros((seq, DR_A), np.float32)
    t1 = np.concatenate([np.full((seq, DN_A), scale_keep, np.float32), z32, cos * scale_rope], axis=1)
    t2 = np.concatenate([np.zeros((seq, DN_A), np.float32), z32, sin * scale_rope], axis=1)
    return jnp.asarray(t1, F32), jnp.asarray(t2, F32)


def _block_diag(w):
    nb, bs, _ = w.shape
    n = nb * bs
    rows = w.reshape(n, bs)
    ri = lax.broadcasted_iota(jnp.int32, (n, n), 0)
    ci = lax.broadcasted_iota(jnp.int32, (n, n), 1)
    out = jnp.zeros((n, n), w.dtype)
    for o in range(bs):
        out = out + jnp.where((ri // bs == ci // bs) & (ci % bs == o), rows[:, o:o + 1], 0.0)
    return out


def kernel(x, c, ctx, c_ctx, ln_in_g, ln_in_b, w_ada, b_ada, w_in, g_qa, w_qb, g_kva, w_kvb, conv_w, conv_b, w_mq, w_mk, w_mv, w_gate, b_gate, mh_g, skip, w_out, ln_g, ln_b):
    b, t, _ = x.shape
    l = 0
    r2 = lambda v: v.reshape(1, -1)

    wina, winr = _win_call(jnp.swapaxes(w_in[l], 0, 1))
    wq3 = w_qb[l].reshape(Q_LORA, H_A, DN_A + DR_A)
    wq_r = wq3[..., DN_A:]
    wq = jnp.concatenate([wq3[..., :DN_A], _rot_partner(wq_r), wq_r], axis=-1)
    wq = wq.reshape(Q_LORA, H_A * SLAB).astype(BF16)
    wkv3 = w_kvb[l].reshape(KV_LORA, H_A, DN_A + DV_A)
    wkn = jnp.concatenate([wkv3[..., :DN_A], jnp.zeros((KV_LORA, H_A, SLAB - DN_A), F32)], axis=-1)
    wkn = wkn.reshape(KV_LORA, H_A * SLAB).astype(BF16)
    wvt = wkv3[..., DN_A:].reshape(KV_LORA, W_A).T.astype(BF16)
    wmq = _block_diag(w_mq[l])
    wmk = _block_diag(w_mk[l])
    wmv = _block_diag(w_mv[l])
    perm = jnp.array([0, 1, 2, 3, 8, 9, 10, 11, 4, 5, 6, 7, 12, 13, 14, 15])
    wg = jnp.pad(w_gate[l][:, perm], ((0, 0), (0, LANES - 2 * N_HD))).astype(BF16)
    bg = jnp.pad(b_gate[l][perm], (0, LANES - 2 * N_HD)).reshape(1, LANES)
    wgc, wgm = _gate_fold_call(wmq.astype(BF16), wmk.astype(BF16), wmv.astype(BF16), wg)
    wmk_s = (wmk * (DH_B ** -0.5)).astype(BF16)
    wo = w_out[l].astype(BF16)

    sm_scale = (DN_A + DR_A) ** -0.5 * LOG2_E
    t1q, t2q = _rope_tables(t, sm_scale, sm_scale)
    t1k, t2k = _rope_tables(t, 0.0, 1.0)
    n_ctx = ctx.shape[1]
    t1kc = jnp.asarray(np.concatenate([np.zeros((n_ctx, DN_A + DR_A), np.float32),
                                       np.ones((n_ctx, DR_A), np.float32)], axis=1))
    t2kc = jnp.asarray(np.zeros((n_ctx, SLAB), np.float32))

    cc = jnp.concatenate([c, c_ctx[None, :], jnp.zeros((16 - b - 1, D_MODEL), F32)], axis=0)
    mod = _ada_call(cc, w_ada[l], r2(b_ada[l]))
    mod3 = mod.reshape(16, 1, 3 * D_MODEL)

    lng, lnb = r2(ln_in_g), r2(ln_in_b)
    shared = (wina, winr, r2(g_qa[l]), wq, r2(g_kva[l]), wkn, wvt)
    q, kl, vtl, xm, sza, som, szm, h = _proj_call(x, mod3, None, lng, lnb, *shared, t1q, t2q, t1k, t2k,
                                                  tm=1024, n_sub=2, latent=True)
    kc, vtc, xmc = _proj_call(ctx, mod3, b, lng, lnb, *shared, t1kc, t2kc, t1kc, t2kc,
                              tm=n_ctx, n_sub=1, latent=False)

    hs, xc = _mlstm_call(xm, xmc, conv_w[l], r2(conv_b[l]), wmq.astype(BF16), wmk_s, wmv.astype(BF16),
                         wgc, wgm, bg)
    ya, yb = _attn_call(q, kc, kl, vtc, vtl, sza, hs, som, szm, xc, r2(mh_g[l]), r2(skip[l]), tq=512)
    return _out_call(h, mod3, ya, yb, wo, r2(ln_g[l]), r2(ln_b[l]), tm=1024, n_sub=4)
```

```python
import functools

import numpy as np
import jax
import jax.numpy as jnp
from jax import lax
from jax.experimental import pallas as pl
from jax.experimental.pallas import tpu as pltpu

F32 = jnp.float32
BF16 = jnp.bfloat16

D_MODEL = 1024
CTX_LEN = 256
GRID_W = 64
H_A, DN_A, DR_A, DV_A = 8, 64, 32, 64
W_A = H_A * DV_A
Q_LORA, KV_LORA = 256, 128
ROPE_FREQS = DR_A // 4
ROPE_BASE = 10000.0
H_B, DH_B = 4, 128
W_B = H_B * DH_B
QKV_BS = 4
DEPTH = 1
ALPHA = (2.0 * DEPTH) ** 0.25
LN_EPS = 1e-5
RMS_EPS = 1e-6
LOG2_E = 1.4426950408889634

LANES = 128
MXU_TILE = 256
SLAB = LANES
DV_EXT = DV_A + 16
ATTN_KEY_CHUNK = MXU_TILE
CHUNK = 256
N_HD = 2 * H_B
ST_ROWS = DH_B + 16
VMEM_LIMIT = 56 * 1024 * 1024

_NT = (((1,), (1,)), ((), ()))


def _dot(a, b):
    return jnp.dot(a, b, preferred_element_type=F32)


def _dot_nt(a, b):
    return lax.dot_general(a, b, _NT, preferred_element_type=F32)


def _layer_norm(x):
    mu = jnp.mean(x, axis=-1, keepdims=True)
    xc = x - mu
    var = jnp.mean(xc * xc, axis=-1, keepdims=True)
    return xc * lax.rsqrt(var + LN_EPS)


def _rms_norm(x, g):
    return (x * lax.rsqrt(jnp.mean(x * x, axis=-1, keepdims=True) + RMS_EPS)) * g


def _silu(x):
    return x * jax.nn.sigmoid(x)


def _log_sigmoid(x):
    return jnp.minimum(x, 0.0) - jnp.log1p(jnp.exp(-jnp.abs(x)))


def _ada_kernel(cc_ref, w_ref, b_ref, o_ref):
    a = _silu(cc_ref[...]).astype(BF16)
    o_ref[...] = _dot(a, w_ref[...].astype(BF16)) + b_ref[...]


def _ada_call(cc, w_ada, b_ada):
    n = w_ada.shape[1]
    tn = 1024
    return pl.pallas_call(
        _ada_kernel,
        grid=(n // tn,),
        in_specs=[pl.BlockSpec((16, D_MODEL), lambda j: (0, 0)),
                  pl.BlockSpec((D_MODEL, tn), lambda j: (0, j)),
                  pl.BlockSpec((1, tn), lambda j: (0, j))],
        out_specs=pl.BlockSpec((16, tn), lambda j: (0, j)),
        out_shape=jax.ShapeDtypeStruct((16, n), F32),
        compiler_params=pltpu.CompilerParams(dimension_semantics=("arbitrary",),
                                             vmem_limit_bytes=VMEM_LIMIT),
        name="ada_mod",
    )(cc, w_ada, b_ada)


_S_KVA, _S_KR = Q_LORA + KV_LORA, Q_LORA + KV_LORA + DR_A


def _win_kernel(wt_ref, wina_ref, winr_ref):
    n_r = winr_ref.shape[1]
    for j in range(_S_KVA // LANES):
        wina_ref[:, j * LANES:(j + 1) * LANES] = wt_ref[j * LANES:(j + 1) * LANES, :].T.astype(BF16)
    kr = wt_ref[_S_KVA:_S_KR, :]
    a0, b0, a1, b1 = (kr[i * ROPE_FREQS:(i + 1) * ROPE_FREQS, :] for i in range(4))
    blk = jnp.concatenate([jnp.zeros((DN_A, D_MODEL), F32), -b0, a0, -b1, a1, kr], axis=0)
    wina_ref[:, _S_KVA:_S_KVA + LANES] = blk.T.astype(BF16)
    for j in range(n_r // MXU_TILE):
        rows = slice(_S_KR + j * MXU_TILE, _S_KR + (j + 1) * MXU_TILE)
        winr_ref[:, j * MXU_TILE:(j + 1) * MXU_TILE] = wt_ref[rows, :].T.astype(BF16)


def _win_call(wt):
    n_in, d = wt.shape
    return pl.pallas_call(
        _win_kernel,
        out_shape=[jax.ShapeDtypeStruct((d, _S_KVA + LANES), BF16),
                   jax.ShapeDtypeStruct((d, n_in - _S_KR), BF16)],
        compiler_params=pltpu.CompilerParams(vmem_limit_bytes=VMEM_LIMIT),
        name="win_relayout",
    )(wt)


def _rope_slab(s, t1, t2):
    return s * t1 + pltpu.roll(s, 32, 1) * t2


def _proj_kernel(x_ref, mod_ref, lng_ref, lnb_ref, wina_ref, winr_ref, gqa_ref, wq_ref, gkva_ref,
                 wkn_ref, wvt_ref, t1q_ref, t2q_ref, t1k_ref, t2k_ref, *out_refs, latent, n_sub):
    if latent:
        q_ref, k_ref, vt_ref, xm_ref, sza_ref, som_ref, szm_ref, h_ref = out_refs
    else:
        k_ref, vt_ref, xm_ref = out_refs
    sub = x_ref.shape[1] // n_sub
    mod = mod_ref[0]
    shift, scale1 = mod[:, 0:D_MODEL], 1.0 + mod[:, D_MODEL:2 * D_MODEL]
    ones_rows = (lax.broadcasted_iota(jnp.int32, (DV_EXT - DV_A, sub), 0) == 0).astype(BF16)

    def norm_stage(s):
        rows = slice(s * sub, (s + 1) * sub)
        h = _layer_norm(x_ref[0, rows, :]) * lng_ref[...] + lnb_ref[...]
        if latent:
            h_ref[0, rows, :] = h
        return (h * scale1 + shift).astype(BF16)

    def proj_stage(s, u):
        rows = slice(s * sub, (s + 1) * sub)
        p0 = _dot(u, wina_ref[...])
        d_xm = _dot(u, winr_ref[:, 512:1024])
        kvn = _rms_norm(p0[:, 256:384], gkva_ref[...]).astype(BF16)
        kn = _dot(kvn, wkn_ref[...])
        d_za = _dot(u, winr_ref[:, 0:512]) if latent else None
        krr = _rope_slab(p0[:, 384:512], t1k_ref[rows, :], t2k_ref[rows, :])
        for hh in range(H_A):
            k_ref[0, hh, rows, :] = (kn[:, hh * SLAB:(hh + 1) * SLAB] + krr).astype(BF16)
        vt = _dot_nt(wvt_ref[...], kvn).astype(BF16)
        for hh in range(H_A):
            vt_ref[0, hh, 0:DV_A, rows] = vt[hh * DV_A:(hh + 1) * DV_A, :]
            vt_ref[0, hh, DV_A:DV_EXT, rows] = ones_rows
        xm_ref[0, rows, :] = d_xm.astype(BF16)
        if latent:
            d_om = _dot(u, winr_ref[:, 1024:1536])
            qn = _rms_norm(p0[:, 0:256], gqa_ref[...]).astype(BF16)
            qs = _dot(qn, wq_ref[...])
            sza_ref[0, rows, :] = _silu(d_za).astype(BF16)
            d_zm = _dot(u, winr_ref[:, 1536:2048])
            t1q, t2q = t1q_ref[rows, :], t2q_ref[rows, :]
            for hh in range(H_A):
                q_ref[0, hh, rows, :] = _rope_slab(qs[:, hh * SLAB:(hh + 1) * SLAB], t1q, t2q).astype(BF16)
            som_ref[0, rows, :] = jax.nn.sigmoid(d_om).astype(BF16)
            szm_ref[0, rows, :] = _silu(d_zm).astype(BF16)

    u = norm_stage(0)
    for s in range(n_sub):
        u_next = norm_stage(s + 1) if s + 1 < n_sub else None
        proj_stage(s, u)
        u = u_next


def _proj_call(x, mod3, mod_row, lng, lnb, wina, winr, gqa, wq, gkva, wkn, wvt, t1q, t2q, t1k, t2k, *,
               tm, n_sub, latent):
    b, t, _ = x.shape
    const = lambda shape: pl.BlockSpec(shape, lambda i, j: (0,) * len(shape))
    tab = pl.BlockSpec((tm, SLAB), lambda i, j: (j, 0))
    row = lambda w: pl.BlockSpec((1, tm, w), lambda i, j: (i, j, 0))
    if mod_row is None:
        mod_spec = pl.BlockSpec((1, 1, 3 * D_MODEL), lambda i, j: (i, 0, 0))
    else:
        mod_spec = pl.BlockSpec((1, 1, 3 * D_MODEL), lambda i, j: (mod_row, 0, 0))
    in_specs = [row(D_MODEL), mod_spec, const((1, D_MODEL)), const((1, D_MODEL)),
                const(wina.shape), const(winr.shape), const((1, Q_LORA)), const((Q_LORA, H_A * SLAB)),
                const((1, KV_LORA)), const((KV_LORA, H_A * SLAB)), const((W_A, KV_LORA)),
                tab, tab, tab, tab]
    k_spec = pl.BlockSpec((1, H_A, tm, SLAB), lambda i, j: (i, 0, j, 0))
    vt_spec = pl.BlockSpec((1, H_A, DV_EXT, tm), lambda i, j: (i, 0, 0, j))
    k_shape = jax.ShapeDtypeStruct((b, H_A, t, SLAB), BF16)
    vt_shape = jax.ShapeDtypeStruct((b, H_A, DV_EXT, t), BF16)
    half = jax.ShapeDtypeStruct((b, t, W_B), BF16)
    if latent:
        out_specs = [k_spec, k_spec, vt_spec, row(W_B), row(W_A), row(W_B), row(W_B), row(D_MODEL)]
        out_shape = [k_shape, k_shape, vt_shape, half, half, half, half,
                     jax.ShapeDtypeStruct((b, t, D_MODEL), F32)]
    else:
        out_specs = [k_spec, vt_spec, row(W_B)]
        out_shape = [k_shape, vt_shape, half]
    return pl.pallas_call(
        functools.partial(_proj_kernel, latent=latent, n_sub=n_sub),
        grid=(b, t // tm),
        in_specs=in_specs, out_specs=out_specs, out_shape=out_shape,
        compiler_params=pltpu.CompilerParams(dimension_semantics=("parallel", "parallel"),
                                             vmem_limit_bytes=VMEM_LIMIT),
        name="in_proj_latent" if latent else "in_proj_ctx",
    )(x, mod3, lng, lnb, wina, winr, gqa, wq, gkva, wkn, wvt, t1q, t2q, t1k, t2k)


def _attn_kernel(q_ref, kc_ref, kl_ref, vtc_ref, vtl_ref, sza_ref, hs_ref, som_ref, szm_ref, xc_ref,
                 mhg_ref, skip_ref, o_ref, yb_ref, s_buf, ot_s):
    n_ctx, t = kc_ref.shape[2], kl_ref.shape[2]
    kc = ATTN_KEY_CHUNK

    def scores(h, slot):
        qh = q_ref[0, h]
        sc = _dot_nt(kc_ref[0, h], qh)
        sk = _dot_nt(kl_ref[0, h], qh)
        s_buf[slot, 0:n_ctx, :] = sc
        s_buf[slot, n_ctx:n_ctx + t, :] = sk
        return jnp.maximum(jnp.max(sc, axis=0, keepdims=True), jnp.max(sk, axis=0, keepdims=True))

    def values(h, slot, m):
        chunks = [(vtc_ref, c * kc, c * kc) for c in range(n_ctx // kc)]
        chunks += [(vtl_ref, c * kc, n_ctx + c * kc) for c in range(t // kc)]
        acc = None
        for vref, v0, s0 in chunks:
            p = jnp.exp2(s_buf[slot, s0:s0 + kc, :] - m).astype(BF16)
            d = _dot(vref[0, h, :, v0:v0 + kc], p)
            acc = d if acc is None else acc + d
        rows = pl.ds(pl.multiple_of(h * DV_A, DV_A), DV_A)
        ot_s[rows, :] = acc[0:DV_A, :] / acc[DV_A:DV_A + 1, :]

    def cell_out(pc):
        rows = pl.ds(pl.multiple_of(pc * LANES, LANES), LANES)
        yb_ref[0, rows, :] = _mlstm_out(hs_ref[0, pc], som_ref[0, rows, :], szm_ref[0, rows, :],
                                        xc_ref[0, rows, :], mhg_ref[...], skip_ref[...])

    def body(i, m_even):
        h = 2 * i
        m_odd = scores(h + 1, 1)
        cell_out(i + 1)
        values(h, 0, m_even)
        m_even = scores(h + 2, 0)
        values(h + 1, 1, m_odd)
        return m_even

    assert hs_ref.shape[1] == H_A // 2
    m_first = scores(0, 0)
    cell_out(0)
    m_even = lax.fori_loop(0, H_A // 2 - 1, body, m_first)
    m_odd = scores(H_A - 1, 1)
    values(H_A - 2, 0, m_even)
    values(H_A - 1, 1, m_odd)
    o_ref[0] = (ot_s[...].T * sza_ref[0].astype(F32)).astype(BF16)


def _attn_call(q, kc, kl, vtc, vtl, sza, hs, som, szm, xc, mhg, skip, *, tq):
    b, _, t, _ = q.shape
    n_ctx = kc.shape[2]
    head_blk = lambda n, w: pl.BlockSpec((1, H_A, n, w), lambda i, j: (i, 0, 0, 0))
    const = lambda shape: pl.BlockSpec(shape, lambda i, j: (0,) * len(shape))
    row = lambda w: pl.BlockSpec((1, tq, w), lambda i, j: (i, j, 0))
    half = jax.ShapeDtypeStruct((b, t, W_A), BF16)
    return pl.pallas_call(
        _attn_kernel,
        grid=(b, t // tq),
        in_specs=[pl.BlockSpec((1, H_A, tq, SLAB), lambda i, j: (i, 0, j, 0)),
                  head_blk(n_ctx, SLAB), head_blk(t, SLAB),
                  head_blk(DV_EXT, n_ctx), head_blk(DV_EXT, t), row(W_A),
                  pl.BlockSpec((1, tq // LANES, W_B, LANES), lambda i, j: (i, j, 0, 0)),
                  row(W_B), row(W_B), row(W_B), const((1, W_B)), const((1, W_B))],
        out_specs=[row(W_A), row(W_B)],
        out_shape=[half, jax.ShapeDtypeStruct((b, t, W_B), BF16)],
        scratch_shapes=[pltpu.VMEM((2, n_ctx + t, tq), F32),
                        pltpu.VMEM((W_A, tq), F32)],
        compiler_params=pltpu.CompilerParams(dimension_semantics=("parallel", "parallel"),
                                             vmem_limit_bytes=VMEM_LIMIT),
        name="mla_attention",
    )(q, kc, kl, vtc, vtl, sza, hs, som, szm, xc, mhg, skip)


def _gate_fold_kernel(wq_ref, wk_ref, wv_ref, wg_ref, wgc_ref, wgm_ref):
    wgc_ref[...] = (_dot(wq_ref[...], wg_ref[0:W_B, :])
                    + _dot(wk_ref[...], wg_ref[W_B:2 * W_B, :])).astype(BF16)
    wgm_ref[...] = _dot(wv_ref[...], wg_ref[2 * W_B:3 * W_B, :]).astype(BF16)


def _gate_fold_call(wq, wk, wv, wg):
    shp = jax.ShapeDtypeStruct((W_B, LANES), BF16)
    return pl.pallas_call(_gate_fold_kernel, out_shape=[shp, shp],
                          compiler_params=pltpu.CompilerParams(vmem_limit_bytes=VMEM_LIMIT),
                          name="mlstm_gate_fold")(wq, wk, wv, wg)


_PAD = 8


def _lane_scans(rows_per_chunk, combine, fill, out):
    npc = CHUNK // LANES
    flat = [x[:, i * LANES:(i + 1) * LANES] for x in rows_per_chunk for i in range(npc)]
    lane = lax.broadcasted_iota(jnp.int32, flat[0].shape, 1)
    pre, suf, sh = list(flat), list(flat), 1
    while sh < LANES:
        pre = [combine(p, jnp.where(lane >= sh, pltpu.roll(p, sh, 1), fill)) for p in pre]
        suf = [combine(s, jnp.where(lane < LANES - sh, pltpu.roll(s, LANES - sh, 1), fill)) for s in suf]
        sh *= 2
        yield
    for c in range(len(rows_per_chunk)):
        p, s = pre[c * npc:(c + 1) * npc], suf[c * npc:(c + 1) * npc]
        tot = [x[:, LANES - 1:LANES] for x in p]
        run = None
        for i in range(npc):
            if run is not None:
                p[i] = combine(p[i], run)
            run = tot[i] if run is None else combine(run, tot[i])
        run = None
        for i in reversed(range(npc)):
            if run is not None:
                s[i] = combine(s[i], run)
            run = tot[i] if run is None else combine(run, tot[i])
        out.append((jnp.concatenate(p, axis=1), jnp.concatenate(s, axis=1)))


def _gate_tables(gts, dests):
    L = CHUNK
    fwd_rows = lax.broadcasted_iota(jnp.int32, (N_HD, L), 0) < H_B
    lis = [gt[0:N_HD, :] for gt in gts]
    sums = []
    yield from _lane_scans([_log_sigmoid(gt[N_HD:2 * N_HD, :]) for gt in gts], jnp.add, 0.0, sums)
    cums = [jnp.where(fwd_rows, ps, ss) for ps, ss in sums]
    rs = [li - cum for li, cum in zip(lis, cums)]
    maxs = []
    yield from _lane_scans(rs, jnp.maximum, -jnp.inf, maxs)
    for (grow_ref, c, gcol_ref, r0), cum, r, (pm, sm) in zip(dests, cums, rs, maxs):
        grow_ref[c] = jnp.concatenate([cum, r, jnp.where(fwd_rows, pm, sm)], axis=0)
        gcol_ref[r0:r0 + L, :] = jnp.concatenate([r, jnp.zeros((LANES - N_HD, L), F32)], axis=0).T


def _fill_pad(pad_s, xm_ref, n):
    zrow = jnp.zeros((_PAD, W_B), F32)
    pad_s[0:_PAD, :] = zrow
    pad_s[_PAD:_PAD + n, :] = xm_ref[...].astype(F32)
    pad_s[_PAD + n:2 * _PAD + n, :] = zrow


def _conv_gate_stage(c, pad_s, xm_ref, cw_ref, cb_ref, wgc_ref, wgm_ref, bg_ref, xc_out, xcb_s):
    L = CHUNK
    rows = slice(c * L, (c + 1) * L)
    r0 = c * L + _PAD
    row_id = lax.broadcasted_iota(jnp.int32, (L, W_B), 0)
    xcur = pad_s[r0:r0 + L, :]
    xprev = jnp.where(row_id == 0, pad_s[r0 - 1:r0, :], pltpu.roll(xcur, 1, 0))
    xnext = jnp.where(row_id == L - 1, pad_s[r0 + L:r0 + L + 1, :], pltpu.roll(xcur, L - 1, 0))
    pre = cb_ref[...] + xprev * cw_ref[0:1, :] + xcur * cw_ref[1:2, :] + xnext * cw_ref[2:3, :]
    xc = _silu(pre)
    xcb = xc.astype(BF16)
    if xc_out is not None:
        xc_out[rows, :] = xcb
    xcb_s[rows, :] = xcb
    g = _dot(xcb, wgc_ref[...]) + _dot(xm_ref[rows, :], wgm_ref[...]) + bg_ref[...]
    return g.T[0:2 * N_HD, :]


def _headwise_stage(c, xm_ref, xcb_s, wq_ref, wk_ref, wv_ref, q_s, k_s, vt_s):
    L = CHUNK
    rows = slice(c * L, (c + 1) * L)
    for hf in range(W_B // MXU_TILE):
        cs = slice(hf * MXU_TILE, (hf + 1) * MXU_TILE)
        xcb = xcb_s[rows, cs]
        if q_s is not None:
            q_s[rows, cs] = _dot(xcb, wq_ref[cs, cs]).astype(BF16)
        k_s[rows, cs] = _dot(xcb, wk_ref[cs, cs]).astype(BF16)
        vt_s[c, cs, :] = _dot(xm_ref[rows, cs], wv_ref[cs, cs]).T.astype(BF16)


_TAB_ROWS = 5 * N_HD


def _chain_tables(growc_s, grow_s, tab_s, nc):
    L = CHUNK
    fwd_rows = lax.broadcasted_iota(jnp.int32, (N_HD, L), 0) < H_B
    gc = growc_s[0]
    steps = [(gc, gc)] + [(grow_s[i], grow_s[nc - 1 - i]) for i in range(nc)]
    parts = []
    for g_f, g_b in steps:
        pick = lambda a: jnp.where(fwd_rows, g_f[a * N_HD:(a + 1) * N_HD, :], g_b[a * N_HD:(a + 1) * N_HD, :])
        cum, r, pm = pick(0), pick(1), pick(2)
        b_end = jnp.where(fwd_rows[:, 0:1], cum[:, L - 1:L], cum[:, 0:1])
        w = r + b_end
        parts.append((cum, pm, b_end, w, jnp.max(w, axis=1, keepdims=True)))
    m0 = jnp.zeros((N_HD, 1), F32)
    for idx, (cum, pm, b_end, w, wmax) in enumerate(parts):
        m_new = jnp.maximum(b_end + m0, wmax)
        mu = jnp.maximum(m0, pm)
        decay = jnp.broadcast_to(jnp.exp(b_end + m0 - m_new), (N_HD, L))
        tab_s[idx] = jnp.concatenate([mu, jnp.exp(m0 - mu), jnp.exp(-(cum + mu)),
                                      jnp.exp(w - m_new), decay], axis=0)
        m0 = m_new


def _tab(tab, a, hd):
    return tab[a * N_HD + hd:a * N_HD + hd + 1, :]


def _state_update(st_ref, hd, vt_ext, k_c, ws_row, decay_hd):
    vw = (vt_ext.astype(F32) * ws_row).astype(BF16)
    st_ref[hd] = decay_hd * st_ref[hd] + _dot(vw, k_c)


def _mlstm_kernel(xm_ref, xmc_ref, cw_ref, cb_ref, wq_ref, wk_ref, wv_ref, wgc_ref, wgm_ref, bg_ref,
                  hs_ref, xc_ref,
                  pad_s, xcb_s, q_s, k_s, vt_s, gcol_s, grow_s,
                  padc_s, xcbc_s, kc_s, vtc_s, gcolc_s, growc_s, tab_s, st_s):
    L = CHUNK
    t = xm_ref.shape[1]
    nc = t // L
    row_i = lax.broadcasted_iota(jnp.int32, (L, L), 0)
    col_i = lax.broadcasted_iota(jnp.int32, (L, L), 1)
    tri = (row_i <= col_i, row_i >= col_i)
    ones_rows = (lax.broadcasted_iota(jnp.int32, (ST_ROWS - DH_B, L), 0) == 0).astype(BF16)
    hsl = lambda hh: slice(hh * DH_B, (hh + 1) * DH_B)
    xm, xmc = xm_ref.at[0], xmc_ref.at[0]
    gate_w = (cw_ref, cb_ref, wgc_ref, wgm_ref, bg_ref)

    _fill_pad(padc_s, xmc, CTX_LEN)
    _fill_pad(pad_s, xm, t)
    gts = [_conv_gate_stage(0, padc_s, xmc, *gate_w, None, xcbc_s)]
    gts += [_conv_gate_stage(c, pad_s, xm, *gate_w, xc_ref.at[0], xcb_s) for c in range(nc)]
    dests = [(growc_s, 0, gcolc_s, 0)] + [(grow_s, c, gcol_s, c * L) for c in range(nc)]
    tables = _gate_tables(gts, dests)
    _headwise_stage(0, xmc, xcbc_s, wq_ref, wk_ref, wv_ref, None, kc_s, vtc_s)
    for c in range(nc):
        next(tables, None)
        next(tables, None)
        _headwise_stage(c, xm, xcb_s, wq_ref, wk_ref, wv_ref, q_s, k_s, vt_s)
    for _ in tables:
        pass

    _chain_tables(growc_s, grow_s, tab_s, nc)
    st_s[...] = jnp.zeros(st_s.shape, F32)
    tab = tab_s[0]
    for hd in range(N_HD):
        hh = hd % H_B
        vt_ext = jnp.concatenate([vtc_s[0, hsl(hh), :], ones_rows], axis=0)
        _state_update(st_s, hd, vt_ext, kc_s[:, hsl(hh)], _tab(tab, 3, hd), _tab(tab, 4, hd)[:, 0:DH_B])

    def body(i, carry, *, accumulate):
        ci = (i, nc - 1 - i)
        rows = tuple(pl.ds(pl.multiple_of(c * L, L), L) for c in ci)
        tab = tab_s[i + 1]
        gcols = tuple(gcol_s[r, :] for r in rows)
        live = {}

        def stage_scores(hd):
            d, hh = hd // H_B, hd % H_B
            k_c = k_s[rows[d], hsl(hh)]
            vt_ext = jnp.concatenate([vt_s[ci[d], hsl(hh), :], ones_rows], axis=0)
            lhs = jnp.concatenate([k_c, st_s[hd].astype(BF16)], axis=0)
            live[hd] = (k_c, vt_ext, _dot_nt(lhs, q_s[rows[d], hsl(hh)]))

        def stage_gate(hd):
            d = hd // H_B
            k_c, vt_ext, res = live[hd]
            rcol = gcols[d][:, hd:hd + 1]
            e = jnp.exp(jnp.where(tri[d], rcol - _tab(tab, 0, hd), -jnp.inf))
            p = (res[0:L, :] * e).astype(BF16)
            _state_update(st_s, hd, vt_ext, k_c, _tab(tab, 3, hd), _tab(tab, 4, hd)[:, 0:DH_B])
            live[hd] = (vt_ext, res[L:L + ST_ROWS, :], p)

        def stage_out(hd):
            d, hh = hd // H_B, hd % H_B
            vt_ext, inter, p = live.pop(hd)
            tot = _tab(tab, 1, hd) * inter + _dot(vt_ext, p)
            den = tot[DH_B:DH_B + 1, :]
            h_t = tot[0:DH_B, :] / jnp.maximum(jnp.abs(den), _tab(tab, 2, hd))
            for pc in range(L // LANES):
                piece = h_t[:, pc * LANES:(pc + 1) * LANES]
                if accumulate:
                    hs_ref[0, ci[d] * (L // LANES) + pc, hsl(hh), :] += piece
                else:
                    hs_ref[0, ci[d] * (L // LANES) + pc, hsl(hh), :] = piece

        stages = (stage_scores, stage_gate, stage_out)
        for step in range(N_HD + len(stages) - 1):
            for si in range(len(stages)):
                if 0 <= step - si < N_HD:
                    stages[si](step - si)
        return carry

    lax.fori_loop(0, nc // 2, functools.partial(body, accumulate=False), 0)
    lax.fori_loop(nc // 2, nc, functools.partial(body, accumulate=True), 0)


def _mlstm_out(hs_t, som, szm, xc, mhg, skip):
    hsum = hs_t.T * som.astype(F32)
    parts = [_layer_norm(hsum[:, hh * DH_B:(hh + 1) * DH_B]) for hh in range(H_B)]
    hb = jnp.concatenate(parts, axis=1) * mhg + skip * xc.astype(F32)
    return (hb * szm.astype(F32)).astype(BF16)


def _mlstm_call(xm, xmc, conv_w, conv_b, wq, wk, wv, wgc, wgm, bg):
    b, t, _ = xm.shape
    nc = t // CHUNK
    assert nc % 2 == 0
    seq = lambda n: pl.BlockSpec((1, n, W_B), lambda i: (i, 0, 0))
    const = lambda shape: pl.BlockSpec(shape, lambda i: (0,) * len(shape))
    scratch = [
        pltpu.VMEM((t + 2 * _PAD, W_B), F32),
        pltpu.VMEM((t, W_B), BF16),
        pltpu.VMEM((t, W_B), BF16),
        pltpu.VMEM((t, W_B), BF16),
        pltpu.VMEM((nc, W_B, CHUNK), BF16),
        pltpu.VMEM((t, LANES), F32),
        pltpu.VMEM((nc, 3 * N_HD, CHUNK), F32),
        pltpu.VMEM((CTX_LEN + 2 * _PAD, W_B), F32),
        pltpu.VMEM((CTX_LEN, W_B), BF16),
        pltpu.VMEM((CTX_LEN, W_B), BF16),
        pltpu.VMEM((1, W_B, CHUNK), BF16),
        pltpu.VMEM((CTX_LEN, LANES), F32),
        pltpu.VMEM((1, 3 * N_HD, CHUNK), F32),
        pltpu.VMEM((nc + 1, _TAB_ROWS, CHUNK), F32),
        pltpu.VMEM((N_HD, ST_ROWS, DH_B), F32),
    ]
    return pl.pallas_call(
        _mlstm_kernel,
        grid=(b,),
        in_specs=[seq(t), seq(CTX_LEN), const((3, W_B)), const((1, W_B)),
                  const((W_B, W_B)), const((W_B, W_B)), const((W_B, W_B)),
                  const((W_B, LANES)), const((W_B, LANES)), const((1, LANES))],
        out_specs=[pl.BlockSpec((1, t // LANES, W_B, LANES), lambda i: (i, 0, 0, 0)), seq(t)],
        out_shape=[jax.ShapeDtypeStruct((b, t // LANES, W_B, LANES), F32),
                   jax.ShapeDtypeStruct((b, t, W_B), BF16)],
        scratch_shapes=scratch,
        compiler_params=pltpu.CompilerParams(dimension_semantics=("arbitrary",),
                                             vmem_limit_bytes=VMEM_LIMIT),
        name="mlstm_bidir",
    )(xm, xmc, conv_w, conv_b, wq, wk, wv, wgc, wgm, bg)


def _out_kernel(h_ref, mod_ref, ya_ref, yb_ref, wo_ref, g_ref, b_ref, o_ref, *, n_sub):
    sub = h_ref.shape[1] // n_sub
    gate = mod_ref[0][:, 2 * D_MODEL:3 * D_MODEL]

    def mix_stage(s):
        rows = slice(s * sub, (s + 1) * sub)
        return (_dot(ya_ref[0, rows, :], wo_ref[0:W_A, :])
                + _dot(yb_ref[0, rows, :], wo_ref[W_A:W_A + W_B, :]))

    def norm_stage(s, y):
        rows = slice(s * sub, (s + 1) * sub)
        o_ref[0, rows, :] = _layer_norm(ALPHA * h_ref[0, rows, :] + gate * y) * g_ref[...] + b_ref[...]

    y = mix_stage(0)
    for s in range(n_sub):
        y_next = mix_stage(s + 1) if s + 1 < n_sub else None
        norm_stage(s, y)
        y = y_next


def _out_call(h, mod3, ya, yb, wo, g, bb, *, tm, n_sub):
    b, t, _ = h.shape
    const = lambda shape: pl.BlockSpec(shape, lambda i, j: (0,) * len(shape))
    row = lambda w: pl.BlockSpec((1, tm, w), lambda i, j: (i, j, 0))
    return pl.pallas_call(
        functools.partial(_out_kernel, n_sub=n_sub),
        grid=(b, t // tm),
        in_specs=[row(D_MODEL), pl.BlockSpec((1, 1, 3 * D_MODEL), lambda i, j: (i, 0, 0)),
                  row(W_A), row(W_B),
                  const((W_A + W_B, D_MODEL)), const((1, D_MODEL)), const((1, D_MODEL))],
        out_specs=row(D_MODEL),
        out_shape=jax.ShapeDtypeStruct((b, t, D_MODEL), F32),
        compiler_params=pltpu.CompilerParams(dimension_semantics=("parallel", "parallel"),
                                             vmem_limit_bytes=VMEM_LIMIT),
        name="out_proj_ln",
    )(h, mod3, ya, yb, wo, g, bb)


def _rot_partner(w):
    w4 = w.reshape(w.shape[:-1] + (2, 2, ROPE_FREQS))
    return jnp.stack([-w4[..., 1, :], w4[..., 0, :]], axis=-2).reshape(w.shape)


def _rope_tables(seq, scale_keep, scale_rope):
    n_rows = seq // GRID_W
    rowp = np.repeat(np.arange(n_rows, dtype=np.float32), GRID_W)
    colp = np.tile(np.arange(GRID_W, dtype=np.float32), n_rows)
    inv = (np.float32(ROPE_BASE) ** (-np.arange(ROPE_FREQS, dtype=np.float32) / np.float32(ROPE_FREQS)))
    ang = np.stack([rowp[:, None] * inv, colp[:, None] * inv], axis=1).astype(np.float32)
    cos = np.broadcast_to(np.cos(ang)[:, :, None, :], (seq, 2, 2, ROPE_FREQS)).reshape(seq, DR_A)
    sin = np.broadcast_to(np.sin(ang)[:, :, None, :], (seq, 2, 2, ROPE_FREQS)).reshape(seq, DR_A)
    z32 = np.zeros((seq, DR_A), np.float32)
    t1 = np.concatenate([np.full((seq, DN_A), scale_keep, np.float32), z32, cos * scale_rope], axis=1)
    t2 = np.concatenate([np.zeros((seq, DN_A), np.float32), z32, sin * scale_rope], axis=1)
    return jnp.asarray(t1, F32), jnp.asarray(t2, F32)


def _block_diag(w):
    nb, bs, _ = w.shape
    n = nb * bs
    rows = w.reshape(n, bs)
    ri = lax.broadcasted_iota(jnp.int32, (n, n), 0)
    ci = lax.broadcasted_iota(jnp.int32, (n, n), 1)
    out = jnp.zeros((n, n), w.dtype)
    for o in range(bs):
        out = out + jnp.where((ri // bs == ci // bs) & (ci % bs == o), rows[:, o:o + 1], 0.0)
    return out


def kernel(x, c, ctx, c_ctx, ln_in_g, ln_in_b, w_ada, b_ada, w_in, g_qa, w_qb, g_kva, w_kvb, conv_w, conv_b, w_mq, w_mk, w_mv, w_gate, b_gate, mh_g, skip, w_out, ln_g, ln_b):
    b, t, _ = x.shape
    l = 0
    r2 = lambda v: v.reshape(1, -1)

    wina, winr = _win_call(jnp.swapaxes(w_in[l], 0, 1))
    wq3 = w_qb[l].reshape(Q_LORA, H_A, DN_A + DR_A)
    wq_r = wq3[..., DN_A:]
    wq = jnp.concatenate([wq3[..., :DN_A], _rot_partner(wq_r), wq_r], axis=-1)
    wq = wq.reshape(Q_LORA, H_A * SLAB).astype(BF16)
    wkv3 = w_kvb[l].reshape(KV_LORA, H_A, DN_A + DV_A)
    wkn = jnp.concatenate([wkv3[..., :DN_A], jnp.zeros((KV_LORA, H_A, SLAB - DN_A), F32)], axis=-1)
    wkn = wkn.reshape(KV_LORA, H_A * SLAB).astype(BF16)
    wvt = wkv3[..., DN_A:].reshape(KV_LORA, W_A).T.astype(BF16)
    wmq = _block_diag(w_mq[l])
    wmk = _block_diag(w_mk[l])
    wmv = _block_diag(w_mv[l])
    perm = jnp.array([0, 1, 2, 3, 8, 9, 10, 11, 4, 5, 6, 7, 12, 13, 14, 15])
    wg = jnp.pad(w_gate[l][:, perm], ((0, 0), (0, LANES - 2 * N_HD))).astype(BF16)
    bg = jnp.pad(b_gate[l][perm], (0, LANES - 2 * N_HD)).reshape(1, LANES)
    wgc, wgm = _gate_fold_call(wmq.astype(BF16), wmk.astype(BF16), wmv.astype(BF16), wg)
    wmk_s = (wmk * (DH_B ** -0.5)).astype(BF16)
    wo = w_out[l].astype(BF16)

    sm_scale = (DN_A + DR_A) ** -0.5 * LOG2_E
    t1q, t2q = _rope_tables(t, sm_scale, sm_scale)
    t1k, t2k = _rope_tables(t, 0.0, 1.0)
    n_ctx = ctx.shape[1]
    t1kc = jnp.asarray(np.concatenate([np.zeros((n_ctx, DN_A + DR_A), np.float32),
                                       np.ones((n_ctx, DR_A), np.float32)], axis=1))
    t2kc = jnp.asarray(np.zeros((n_ctx, SLAB), np.float32))

    cc = jnp.concatenate([c, c_ctx[None, :], jnp.zeros((16 - b - 1, D_MODEL), F32)], axis=0)
    mod = _ada_call(cc, w_ada[l], r2(b_ada[l]))
    mod3 = mod.reshape(16, 1, 3 * D_MODEL)

    lng, lnb = r2(ln_in_g), r2(ln_in_b)
    shared = (wina, winr, r2(g_qa[l]), wq, r2(g_kva[l]), wkn, wvt)
    q, kl, vtl, xm, sza, som, szm, h = _proj_call(x, mod3, None, lng, lnb, *shared, t1q, t2q, t1k, t2k,
                                                  tm=1024, n_sub=2, latent=True)
    kc, vtc, xmc = _proj_call(ctx, mod3, b, lng, lnb, *shared, t1kc, t2kc, t1kc, t2kc,
                              tm=n_ctx, n_sub=1, latent=False)

    hs, xc = _mlstm_call(xm, xmc, conv_w[l], r2(conv_b[l]), wmq.astype(BF16), wmk_s, wmv.astype(BF16),
                         wgc, wgm, bg)
    ya, yb = _attn_call(q, kc, kl, vtc, vtl, sza, hs, som, szm, xc, r2(mh_g[l]), r2(skip[l]), tq=512)
    return _out_call(h, mod3, ya, yb, wo, r2(ln_g[l]), r2(ln_b[l]), tm=1024, n_sub=4)
```

```python
import functools

import numpy as np
import jax
import jax.numpy as jnp
from jax import lax
from jax.experimental import pallas as pl
from jax.experimental.pallas import tpu as pltpu

F32 = jnp.float32
BF16 = jnp.bfloat16

D_MODEL = 1024
CTX_LEN = 256
GRID_W = 64
H_A, DN_A, DR_A, DV_A = 8, 64, 32, 64
W_A = H_A * DV_A
Q_LORA, KV_LORA = 256, 128
ROPE_FREQS = DR_A // 4
ROPE_BASE = 10000.0
H_B, DH_B = 4, 128
W_B = H_B * DH_B
QKV_BS = 4
DEPTH = 1
ALPHA = (2.0 * DEPTH) ** 0.25
LN_EPS = 1e-5
RMS_EPS = 1e-6
LOG2_E = 1.4426950408889634

LANES = 128
MXU_TILE = 256
SLAB = LANES
DV_EXT = DV_A + 16
ATTN_KEY_CHUNK = MXU_TILE
CHUNK = 256
N_HD = 2 * H_B
ST_ROWS = DH_B + 16
VMEM_LIMIT = 56 * 1024 * 1024

_NT = (((1,), (1,)), ((), ()))


def _dot(a, b):
    return jnp.dot(a, b, preferred_element_type=F32)


def _dot_nt(a, b):
    return lax.dot_general(a, b, _NT, preferred_element_type=F32)


def _layer_norm(x):
    mu = jnp.mean(x, axis=-1, keepdims=True)
    xc = x - mu
    var = jnp.mean(xc * xc, axis=-1, keepdims=True)
    return xc * lax.rsqrt(var + LN_EPS)


def _rms_norm(x, g):
    return (x * lax.rsqrt(jnp.mean(x * x, axis=-1, keepdims=True) + RMS_EPS)) * g


def _silu(x):
    return x * jax.nn.sigmoid(x)


def _log_sigmoid(x):
    return jnp.minimum(x, 0.0) - jnp.log1p(jnp.exp(-jnp.abs(x)))


def _ada_kernel(cc_ref, w_ref, b_ref, o_ref):
    a = _silu(cc_ref[...]).astype(BF16)
    o_ref[...] = _dot(a, w_ref[...].astype(BF16)) + b_ref[...]


_S_KVA, _S_KR = Q_LORA + KV_LORA, Q_LORA + KV_LORA + DR_A


def _win_kernel(wt_ref, wina_ref, winr_ref):
    n_r = winr_ref.shape[1]
    for j in range(_S_KVA // LANES):
        wina_ref[:, j * LANES:(j + 1) * LANES] = wt_ref[j * LANES:(j + 1) * LANES, :].T.astype(BF16)
    kr = wt_ref[_S_KVA:_S_KR, :]
    a0, b0, a1, b1 = (kr[i * ROPE_FREQS:(i + 1) * ROPE_FREQS, :] for i in range(4))
    blk = jnp.concatenate([jnp.zeros((DN_A, D_MODEL), F32), -b0, a0, -b1, a1, kr], axis=0)
    wina_ref[:, _S_KVA:_S_KVA + LANES] = blk.T.astype(BF16)
    for j in range(n_r // MXU_TILE):
        rows = slice(_S_KR + j * MXU_TILE, _S_KR + (j + 1) * MXU_TILE)
        winr_ref[:, j * MXU_TILE:(j + 1) * MXU_TILE] = wt_ref[rows, :].T.astype(BF16)


def _rope_slab(s, t1, t2):
    return s * t1 + pltpu.roll(s, 32, 1) * t2


def _proj_kernel(x_ref, mod_ref, lng_ref, lnb_ref, wina_ref, winr_ref, gqa_ref, wq_ref, gkva_ref,
                 wkn_ref, wvt_ref, t1q_ref, t2q_ref, t1k_ref, t2k_ref, *out_refs, latent, n_sub):
    if latent:
        q_ref, k_ref, vt_ref, xm_ref, sza_ref, som_ref, szm_ref, h_ref = out_refs
    else:
        k_ref, vt_ref, xm_ref = out_refs
    sub = x_ref.shape[1] // n_sub
    mod = mod_ref[0]
    shift, scale1 = mod[:, 0:D_MODEL], 1.0 + mod[:, D_MODEL:2 * D_MODEL]
    ones_rows = (lax.broadcasted_iota(jnp.int32, (DV_EXT - DV_A, sub), 0) == 0).astype(BF16)

    def norm_stage(s):
        rows = slice(s * sub, (s + 1) * sub)
        h = _layer_norm(x_ref[0, rows, :]) * lng_ref[...] + lnb_ref[...]
        if latent:
            h_ref[0, rows, :] = h
        return (h * scale1 + shift).astype(BF16)

    def proj_stage(s, u):
        rows = slice(s * sub, (s + 1) * sub)
        p0 = _dot(u, wina_ref[...])
        d_xm = _dot(u, winr_ref[:, 512:1024])
        kvn = _rms_norm(p0[:, 256:384], gkva_ref[...]).astype(BF16)
        kn = _dot(kvn, wkn_ref[...])
        d_za = _dot(u, winr_ref[:, 0:512]) if latent else None
        krr = _rope_slab(p0[:, 384:512], t1k_ref[rows, :], t2k_ref[rows, :])
        for hh in range(H_A):
            k_ref[0, hh, rows, :] = (kn[:, hh * SLAB:(hh + 1) * SLAB] + krr).astype(BF16)
        vt = _dot_nt(wvt_ref[...], kvn).astype(BF16)
        for hh in range(H_A):
            vt_ref[0, hh, 0:DV_A, rows] = vt[hh * DV_A:(hh + 1) * DV_A, :]
            vt_ref[0, hh, DV_A:DV_EXT, rows] = ones_rows
        xm_ref[0, rows, :] = d_xm.astype(BF16)
        if latent:
            d_om = _dot(u, winr_ref[:, 1024:1536])
            qn = _rms_norm(p0[:, 0:256], gqa_ref[...]).astype(BF16)
            qs = _dot(qn, wq_ref[...])
            sza_ref[0, rows, :] = _silu(d_za).astype(BF16)
            d_zm = _dot(u, winr_ref[:, 1536:2048])
            t1q, t2q = t1q_ref[rows, :], t2q_ref[rows, :]
            for hh in range(H_A):
                q_ref[0, hh, rows, :] = _rope_slab(qs[:, hh * SLAB:(hh + 1) * SLAB], t1q, t2q).astype(BF16)
            som_ref[0, rows, :] = jax.nn.sigmoid(d_om).astype(BF16)
            szm_ref[0, rows, :] = _silu(d_zm).astype(BF16)

    u = norm_stage(0)
    for s in range(n_sub):
        u_next = norm_stage(s + 1) if s + 1 < n_sub else None
        proj_stage(s, u)
        u = u_next


def _proj_call(x, mod3, mod_row, lng, lnb, wina, winr, gqa, wq, gkva, wkn, wvt, t1q, t2q, t1k, t2k, *,
               tm, n_sub, latent):
    b, t, _ = x.shape
    const = lambda shape: pl.BlockSpec(shape, lambda i, j: (0,) * len(shape))
    tab = pl.BlockSpec((tm, SLAB), lambda i, j: (j, 0))
    row = lambda w: pl.BlockSpec((1, tm, w), lambda i, j: (i, j, 0))
    if mod_row is None:
        mod_spec = pl.BlockSpec((1, 1, 3 * D_MODEL), lambda i, j: (i, 0, 0))
    else:
        mod_spec = pl.BlockSpec((1, 1, 3 * D_MODEL), lambda i, j: (mod_row, 0, 0))
    in_specs = [row(D_MODEL), mod_spec, const((1, D_MODEL)), const((1, D_MODEL)),
                const(wina.shape), const(winr.shape), const((1, Q_LORA)), const((Q_LORA, H_A * SLAB)),
                const((1, KV_LORA)), const((KV_LORA, H_A * SLAB)), const((W_A, KV_LORA)),
                tab, tab, tab, tab]
    k_spec = pl.BlockSpec((1, H_A, tm, SLAB), lambda i, j: (i, 0, j, 0))
    vt_spec = pl.BlockSpec((1, H_A, DV_EXT, tm), lambda i, j: (i, 0, 0, j))
    k_shape = jax.ShapeDtypeStruct((b, H_A, t, SLAB), BF16)
    vt_shape = jax.ShapeDtypeStruct((b, H_A, DV_EXT, t), BF16)
    half = jax.ShapeDtypeStruct((b, t, W_B), BF16)
    if latent:
        out_specs = [k_spec, k_spec, vt_spec, row(W_B), row(W_A), row(W_B), row(W_B), row(D_MODEL)]
        out_shape = [k_shape, k_shape, vt_shape, half, half, half, half,
                     jax.ShapeDtypeStruct((b, t, D_MODEL), F32)]
    else:
        out_specs = [k_spec, vt_spec, row(W_B)]
        out_shape = [k_shape, vt_shape, half]
    return pl.pallas_call(
        functools.partial(_proj_kernel, latent=latent, n_sub=n_sub),
        grid=(b, t // tm),
        in_specs=in_specs, out_specs=out_specs, out_shape=out_shape,
        compiler_params=pltpu.CompilerParams(dimension_semantics=("parallel", "parallel"),
                                             vmem_limit_bytes=VMEM_LIMIT),
        name="in_proj_latent" if latent else "in_proj_ctx",
    )(x, mod3, lng, lnb, wina, winr, gqa, wq, gkva, wkn, wvt, t1q, t2q, t1k, t2k)


def _attn_kernel(q_ref, kc_ref, kl_ref, vtc_ref, vtl_ref, sza_ref, hs_ref, som_ref, szm_ref, xc_ref,
                 mhg_ref, skip_ref, o_ref, yb_ref, s_buf, ot_s):
    n_ctx, t = kc_ref.shape[2], kl_ref.shape[2]
    kc = ATTN_KEY_CHUNK

    def scores(h, slot):
        qh = q_ref[0, h]
        sc = _dot_nt(kc_ref[0, h], qh)
        sk = _dot_nt(kl_ref[0, h], qh)
        s_buf[slot, 0:n_ctx, :] = sc
        s_buf[slot, n_ctx:n_ctx + t, :] = sk
        return jnp.maximum(jnp.max(sc, axis=0, keepdims=True), jnp.max(sk, axis=0, keepdims=True))

    def values(h, slot, m):
        chunks = [(vtc_ref, c * kc, c * kc) for c in range(n_ctx // kc)]
        chunks += [(vtl_ref, c * kc, n_ctx + c * kc) for c in range(t // kc)]
        acc = None
        for vref, v0, s0 in chunks:
            p = jnp.exp2(s_buf[slot, s0:s0 + kc, :] - m).astype(BF16)
            d = _dot(vref[0, h, :, v0:v0 + kc], p)
            acc = d if acc is None else acc + d
        rows = pl.ds(pl.multiple_of(h * DV_A, DV_A), DV_A)
        ot_s[rows, :] = acc[0:DV_A, :] / acc[DV_A:DV_A + 1, :]

    def cell_out(pc):
        rows = pl.ds(pl.multiple_of(pc * LANES, LANES), LANES)
        yb_ref[0, rows, :] = _mlstm_out(hs_ref[0, pc], som_ref[0, rows, :], szm_ref[0, rows, :],
                                        xc_ref[0, rows, :], mhg_ref[...], skip_ref[...])

    def body(i, m_even):
        h = 2 * i
        m_odd = scores(h + 1, 1)
        cell_out(i + 1)
        values(h, 0, m_even)
        m_even = scores(h + 2, 0)
        values(h + 1, 1, m_odd)
        return m_even

    assert hs_ref.shape[1] == H_A // 2
    m_first = scores(0, 0)
    cell_out(0)
    m_even = lax.fori_loop(0, H_A // 2 - 1, body, m_first)
    m_odd = scores(H_A - 1, 1)
    values(H_A - 2, 0, m_even)
    values(H_A - 1, 1, m_odd)
    o_ref[0] = (ot_s[...].T * sza_ref[0].astype(F32)).astype(BF16)


def _attn_call(q, kc, kl, vtc, vtl, sza, hs, som, szm, xc, mhg, skip, *, tq):
    b, _, t, _ = q.shape
    n_ctx = kc.shape[2]
    head_blk = lambda n, w: pl.BlockSpec((1, H_A, n, w), lambda i, j: (i, 0, 0, 0))
    const = lambda shape: pl.BlockSpec(shape, lambda i, j: (0,) * len(shape))
    row = lambda w: pl.BlockSpec((1, tq, w), lambda i, j: (i, j, 0))
    half = jax.ShapeDtypeStruct((b, t, W_A), BF16)
    return pl.pallas_call(
        _attn_kernel,
        grid=(b, t // tq),
        in_specs=[pl.BlockSpec((1, H_A, tq, SLAB), lambda i, j: (i, 0, j, 0)),
                  head_blk(n_ctx, SLAB), head_blk(t, SLAB),
                  head_blk(DV_EXT, n_ctx), head_blk(DV_EXT, t), row(W_A),
                  pl.BlockSpec((1, tq // LANES, W_B, LANES), lambda i, j: (i, j, 0, 0)),
                  row(W_B), row(W_B), row(W_B), const((1, W_B)), const((1, W_B))],
        out_specs=[row(W_A), row(W_B)],
        out_shape=[half, jax.ShapeDtypeStruct((b, t, W_B), BF16)],
        scratch_shapes=[pltpu.VMEM((2, n_ctx + t, tq), F32),
                        pltpu.VMEM((W_A, tq), F32)],
        compiler_params=pltpu.CompilerParams(dimension_semantics=("parallel", "parallel"),
                                             vmem_limit_bytes=VMEM_LIMIT),
        name="mla_attention",
    )(q, kc, kl, vtc, vtl, sza, hs, som, szm, xc, mhg, skip)


def _gate_fold_kernel(wq_ref, wk_ref, wv_ref, wg_ref, wgc_ref, wgm_ref):
    wgc_ref[...] = (_dot(wq_ref[...], wg_ref[0:W_B, :])
                    + _dot(wk_ref[...], wg_ref[W_B:2 * W_B, :])).astype(BF16)
    wgm_ref[...] = _dot(wv_ref[...], wg_ref[2 * W_B:3 * W_B, :]).astype(BF16)


def _prep_kernel(cc_ref, wada_ref, bada_ref, wt_ref, wq_ref, wk_ref, wv_ref, wg_ref,
                 mod_ref, wina_ref, winr_ref, wgc_ref, wgm_ref):
    _ada_kernel(cc_ref, wada_ref, bada_ref, mod_ref)
    _win_kernel(wt_ref, wina_ref, winr_ref)
    _gate_fold_kernel(wq_ref, wk_ref, wv_ref, wg_ref, wgc_ref, wgm_ref)


def _prep_call(cc, w_ada, b_ada, wt, wq, wk, wv, wg):
    n_in, d = wt.shape
    gate = jax.ShapeDtypeStruct((W_B, LANES), BF16)
    return pl.pallas_call(
        _prep_kernel,
        out_shape=[jax.ShapeDtypeStruct((cc.shape[0], w_ada.shape[1]), F32),
                   jax.ShapeDtypeStruct((d, _S_KVA + LANES), BF16),
                   jax.ShapeDtypeStruct((d, n_in - _S_KR), BF16), gate, gate],
        compiler_params=pltpu.CompilerParams(vmem_limit_bytes=VMEM_LIMIT),
        name="param_prep",
    )(cc, w_ada, b_ada, wt, wq, wk, wv, wg)


_PAD = 8


def _lane_scans(rows_per_chunk, combine, fill, out):
    npc = CHUNK // LANES
    flat = [x[:, i * LANES:(i + 1) * LANES] for x in rows_per_chunk for i in range(npc)]
    lane = lax.broadcasted_iota(jnp.int32, flat[0].shape, 1)
    pre, suf, sh = list(flat), list(flat), 1
    while sh < LANES:
        pre = [combine(p, jnp.where(lane >= sh, pltpu.roll(p, sh, 1), fill)) for p in pre]
        suf = [combine(s, jnp.where(lane < LANES - sh, pltpu.roll(s, LANES - sh, 1), fill)) for s in suf]
        sh *= 2
        yield
    for c in range(len(rows_per_chunk)):
        p, s = pre[c * npc:(c + 1) * npc], suf[c * npc:(c + 1) * npc]
        tot = [x[:, LANES - 1:LANES] for x in p]
        run = None
        for i in range(npc):
            if run is not None:
                p[i] = combine(p[i], run)
            run = tot[i] if run is None else combine(run, tot[i])
        run = None
        for i in reversed(range(npc)):
            if run is not None:
                s[i] = combine(s[i], run)
            run = tot[i] if run is None else combine(run, tot[i])
        out.append((jnp.concatenate(p, axis=1), jnp.concatenate(s, axis=1)))


def _gate_tables(gts, dests):
    L = CHUNK
    fwd_rows = lax.broadcasted_iota(jnp.int32, (N_HD, L), 0) < H_B
    lis = [gt[0:N_HD, :] for gt in gts]
    sums = []
    yield from _lane_scans([_log_sigmoid(gt[N_HD:2 * N_HD, :]) for gt in gts], jnp.add, 0.0, sums)
    cums = [jnp.where(fwd_rows, ps, ss) for ps, ss in sums]
    rs = [li - cum for li, cum in zip(lis, cums)]
    maxs = []
    yield from _lane_scans(rs, jnp.maximum, -jnp.inf, maxs)
    for (grow_ref, c, gcol_ref, r0), cum, r, (pm, sm) in zip(dests, cums, rs, maxs):
        grow_ref[c] = jnp.concatenate([cum, r, jnp.where(fwd_rows, pm, sm)], axis=0)
        gcol_ref[r0:r0 + L, :] = jnp.concatenate([r, jnp.zeros((LANES - N_HD, L), F32)], axis=0).T


def _fill_pad(pad_s, xm_ref, n):
    zrow = jnp.zeros((_PAD, W_B), F32)
    pad_s[0:_PAD, :] = zrow
    pad_s[_PAD:_PAD + n, :] = xm_ref[...].astype(F32)
    pad_s[_PAD + n:2 * _PAD + n, :] = zrow


def _conv_gate_stage(c, pad_s, xm_ref, cw_ref, cb_ref, wgc_ref, wgm_ref, bg_ref, xc_out, xcb_s):
    L = CHUNK
    rows = slice(c * L, (c + 1) * L)
    r0 = c * L + _PAD
    row_id = lax.broadcasted_iota(jnp.int32, (L, W_B), 0)
    xcur = pad_s[r0:r0 + L, :]
    xprev = jnp.where(row_id == 0, pad_s[r0 - 1:r0, :], pltpu.roll(xcur, 1, 0))
    xnext = jnp.where(row_id == L - 1, pad_s[r0 + L:r0 + L + 1, :], pltpu.roll(xcur, L - 1, 0))
    pre = cb_ref[...] + xprev * cw_ref[0:1, :] + xcur * cw_ref[1:2, :] + xnext * cw_ref[2:3, :]
    xc = _silu(pre)
    xcb = xc.astype(BF16)
    if xc_out is not None:
        xc_out[rows, :] = xcb
    xcb_s[rows, :] = xcb
    g = _dot(xcb, wgc_ref[...]) + _dot(xm_ref[rows, :], wgm_ref[...]) + bg_ref[...]
    return g.T[0:2 * N_HD, :]


def _headwise_stage(c, xm_ref, xcb_s, wq_ref, wk_ref, wv_ref, q_s, k_s, vt_s):
    L = CHUNK
    rows = slice(c * L, (c + 1) * L)
    for hf in range(W_B // MXU_TILE):
        cs = slice(hf * MXU_TILE, (hf + 1) * MXU_TILE)
        xcb = xcb_s[rows, cs]
        if q_s is not None:
            q_s[rows, cs] = _dot(xcb, wq_ref[cs, cs]).astype(BF16)
        k_s[rows, cs] = _dot(xcb, wk_ref[cs, cs]).astype(BF16)
        vt_s[c, cs, :] = _dot(xm_ref[rows, cs], wv_ref[cs, cs]).T.astype(BF16)


_TAB_ROWS = 5 * N_HD


def _chain_tables(growc_s, grow_s, tab_s, nc):
    L = CHUNK
    fwd_rows = lax.broadcasted_iota(jnp.int32, (N_HD, L), 0) < H_B
    gc = growc_s[0]
    steps = [(gc, gc)] + [(grow_s[i], grow_s[nc - 1 - i]) for i in range(nc)]
    parts = []
    for g_f, g_b in steps:
        pick = lambda a: jnp.where(fwd_rows, g_f[a * N_HD:(a + 1) * N_HD, :], g_b[a * N_HD:(a + 1) * N_HD, :])
        cum, r, pm = pick(0), pick(1), pick(2)
        b_end = jnp.where(fwd_rows[:, 0:1], cum[:, L - 1:L], cum[:, 0:1])
        w = r + b_end
        parts.append((cum, pm, b_end, w, jnp.max(w, axis=1, keepdims=True)))
    m0 = jnp.zeros((N_HD, 1), F32)
    for idx, (cum, pm, b_end, w, wmax) in enumerate(parts):
        m_new = jnp.maximum(b_end + m0, wmax)
        mu = jnp.maximum(m0, pm)
        decay = jnp.broadcast_to(jnp.exp(b_end + m0 - m_new), (N_HD, L))
        tab_s[idx] = jnp.concatenate([mu, jnp.exp(m0 - mu), jnp.exp(-(cum + mu)),
                                      jnp.exp(w - m_new), decay], axis=0)
        m0 = m_new


def _tab(tab, a, hd):
    return tab[a * N_HD + hd:a * N_HD + hd + 1, :]


def _state_update(st_ref, hd, vt_ext, k_c, ws_row, decay_hd):
    vw = (vt_ext.astype(F32) * ws_row).astype(BF16)
    st_ref[hd] = decay_hd * st_ref[hd] + _dot(vw, k_c)


def _mlstm_kernel(xm_ref, xmc_ref, cw_ref, cb_ref, wq_ref, wk_ref, wv_ref, wgc_ref, wgm_ref, bg_ref,
                  hs_ref, xc_ref,
                  pad_s, xcb_s, q_s, k_s, vt_s, gcol_s, grow_s,
                  padc_s, xcbc_s, kc_s, vtc_s, gcolc_s, growc_s, tab_s, st_s):
    L = CHUNK
    t = xm_ref.shape[1]
    nc = t // L
    row_i = lax.broadcasted_iota(jnp.int32, (L, L), 0)
    col_i = lax.broadcasted_iota(jnp.int32, (L, L), 1)
    tri = (row_i <= col_i, row_i >= col_i)
    ones_rows = (lax.broadcasted_iota(jnp.int32, (ST_ROWS - DH_B, L), 0) == 0).astype(BF16)
    hsl = lambda hh: slice(hh * DH_B, (hh + 1) * DH_B)
    xm, xmc = xm_ref.at[0], xmc_ref.at[0]
    gate_w = (cw_ref, cb_ref, wgc_ref, wgm_ref, bg_ref)

    _fill_pad(padc_s, xmc, CTX_LEN)
    _fill_pad(pad_s, xm, t)
    gts = [_conv_gate_stage(0, padc_s, xmc, *gate_w, None, xcbc_s)]
    gts += [_conv_gate_stage(c, pad_s, xm, *gate_w, xc_ref.at[0], xcb_s) for c in range(nc)]
    dests = [(growc_s, 0, gcolc_s, 0)] + [(grow_s, c, gcol_s, c * L) for c in range(nc)]
    tables = _gate_tables(gts, dests)
    _headwise_stage(0, xmc, xcbc_s, wq_ref, wk_ref, wv_ref, None, kc_s, vtc_s)
    for c in range(nc):
        next(tables, None)
        next(tables, None)
        _headwise_stage(c, xm, xcb_s, wq_ref, wk_ref, wv_ref, q_s, k_s, vt_s)
    for _ in tables:
        pass

    _chain_tables(growc_s, grow_s, tab_s, nc)
    st_s[...] = jnp.zeros(st_s.shape, F32)
    tab = tab_s[0]
    for hd in range(N_HD):
        hh = hd % H_B
        vt_ext = jnp.concatenate([vtc_s[0, hsl(hh), :], ones_rows], axis=0)
        _state_update(st_s, hd, vt_ext, kc_s[:, hsl(hh)], _tab(tab, 3, hd), _tab(tab, 4, hd)[:, 0:DH_B])

    def body(i, carry, *, accumulate):
        ci = (i, nc - 1 - i)
        rows = tuple(pl.ds(pl.multiple_of(c * L, L), L) for c in ci)
        tab = tab_s[i + 1]
        gcols = tuple(gcol_s[r, :] for r in rows)
        live = {}

        def stage_scores(hd):
            d, hh = hd // H_B, hd % H_B
            k_c = k_s[rows[d], hsl(hh)]
            vt_ext = jnp.concatenate([vt_s[ci[d], hsl(hh), :], ones_rows], axis=0)
            lhs = jnp.concatenate([k_c, st_s[hd].astype(BF16)], axis=0)
            live[hd] = (k_c, vt_ext, _dot_nt(lhs, q_s[rows[d], hsl(hh)]))

        def stage_gate(hd):
            d = hd // H_B
            k_c, vt_ext, res = live[hd]
            rcol = gcols[d][:, hd:hd + 1]
            e = jnp.exp(jnp.where(tri[d], rcol - _tab(tab, 0, hd), -jnp.inf))
            p = (res[0:L, :] * e).astype(BF16)
            _state_update(st_s, hd, vt_ext, k_c, _tab(tab, 3, hd), _tab(tab, 4, hd)[:, 0:DH_B])
            live[hd] = (vt_ext, res[L:L + ST_ROWS, :], p)

        def stage_out(hd):
            d, hh = hd // H_B, hd % H_B
            vt_ext, inter, p = live.pop(hd)
            tot = _tab(tab, 1, hd) * inter + _dot(vt_ext, p)
            den = tot[DH_B:DH_B + 1, :]
            h_t = tot[0:DH_B, :] / jnp.maximum(jnp.abs(den), _tab(tab, 2, hd))
            for pc in range(L // LANES):
                piece = h_t[:, pc * LANES:(pc + 1) * LANES]
                if accumulate:
                    hs_ref[0, ci[d] * (L // LANES) + pc, hsl(hh), :] += piece
                else:
                    hs_ref[0, ci[d] * (L // LANES) + pc, hsl(hh), :] = piece

        stages = (stage_scores, stage_gate, stage_out)
        for step in range(N_HD + len(stages) - 1):
            for si in range(len(stages)):
                if 0 <= step - si < N_HD:
                    stages[si](step - si)
        return carry

    lax.fori_loop(0, nc // 2, functools.partial(body, accumulate=False), 0)
    lax.fori_loop(nc // 2, nc, functools.partial(body, accumulate=True), 0)


def _mlstm_out(hs_t, som, szm, xc, mhg, skip):
    hsum = hs_t.T * som.astype(F32)
    parts = [_layer_norm(hsum[:, hh * DH_B:(hh + 1) * DH_B]) for hh in range(H_B)]
    hb = jnp.concatenate(parts, axis=1) * mhg + skip * xc.astype(F32)
    return (hb * szm.astype(F32)).astype(BF16)


def _mlstm_call(xm, xmc, conv_w, conv_b, wq, wk, wv, wgc, wgm, bg):
    b, t, _ = xm.shape
    nc = t // CHUNK
    assert nc % 2 == 0
    seq = lambda n: pl.BlockSpec((1, n, W_B), lambda i: (i, 0, 0))
    const = lambda shape: pl.BlockSpec(shape, lambda i: (0,) * len(shape))
    scratch = [
        pltpu.VMEM((t + 2 * _PAD, W_B), F32),
        pltpu.VMEM((t, W_B), BF16),
        pltpu.VMEM((t, W_B), BF16),
        pltpu.VMEM((t, W_B), BF16),
        pltpu.VMEM((nc, W_B, CHUNK), BF16),
        pltpu.VMEM((t, LANES), F32),
        pltpu.VMEM((nc, 3 * N_HD, CHUNK), F32),
        pltpu.VMEM((CTX_LEN + 2 * _PAD, W_B), F32),
        pltpu.VMEM((CTX_LEN, W_B), BF16),
        pltpu.VMEM((CTX_LEN, W_B), BF16),
        pltpu.VMEM((1, W_B, CHUNK), BF16),
        pltpu.VMEM((CTX_LEN, LANES), F32),
        pltpu.VMEM((1, 3 * N_HD, CHUNK), F32),
        pltpu.VMEM((nc + 1, _TAB_ROWS, CHUNK), F32),
        pltpu.VMEM((N_HD, ST_ROWS, DH_B), F32),
    ]
    return pl.pallas_call(
        _mlstm_kernel,
        grid=(b,),
        in_specs=[seq(t), seq(CTX_LEN), const((3, W_B)), const((1, W_B)),
                  const((W_B, W_B)), const((W_B, W_B)), const((W_B, W_B)),
                  const((W_B, LANES)), const((W_B, LANES)), const((1, LANES))],
        out_specs=[pl.BlockSpec((1, t // LANES, W_B, LANES), lambda i: (i, 0, 0, 0)), seq(t)],
        out_shape=[jax.ShapeDtypeStruct((b, t // LANES, W_B, LANES), F32),
                   jax.ShapeDtypeStruct((b, t, W_B), BF16)],
        scratch_shapes=scratch,
        compiler_params=pltpu.CompilerParams(dimension_semantics=("arbitrary",),
                                             vmem_limit_bytes=VMEM_LIMIT),
        name="mlstm_bidir",
    )(xm, xmc, conv_w, conv_b, wq, wk, wv, wgc, wgm, bg)


def _out_kernel(h_ref, mod_ref, ya_ref, yb_ref, wo_ref, g_ref, b_ref, o_ref, *, n_sub):
    sub = h_ref.shape[1] // n_sub
    gate = mod_ref[0][:, 2 * D_MODEL:3 * D_MODEL]

    def mix_stage(s):
        rows = slice(s * sub, (s + 1) * sub)
        return (_dot(ya_ref[0, rows, :], wo_ref[0:W_A, :])
                + _dot(yb_ref[0, rows, :], wo_ref[W_A:W_A + W_B, :]))

    def norm_stage(s, y):
        rows = slice(s * sub, (s + 1) * sub)
        o_ref[0, rows, :] = _layer_norm(ALPHA * h_ref[0, rows, :] + gate * y) * g_ref[...] + b_ref[...]

    y = mix_stage(0)
    for s in range(n_sub):
        y_next = mix_stage(s + 1) if s + 1 < n_sub else None
        norm_stage(s, y)
        y = y_next


def _out_call(h, mod3, ya, yb, wo, g, bb, *, tm, n_sub):
    b, t, _ = h.shape
    const = lambda shape: pl.BlockSpec(shape, lambda i, j: (0,) * len(shape))
    row = lambda w: pl.BlockSpec((1, tm, w), lambda i, j: (i, j, 0))
    return pl.pallas_call(
        functools.partial(_out_kernel, n_sub=n_sub),
        grid=(b, t // tm),
        in_specs=[row(D_MODEL), pl.BlockSpec((1, 1, 3 * D_MODEL), lambda i, j: (i, 0, 0)),
                  row(W_A), row(W_B),
                  const((W_A + W_B, D_MODEL)), const((1, D_MODEL)), const((1, D_MODEL))],
        out_specs=row(D_MODEL),
        out_shape=jax.ShapeDtypeStruct((b, t, D_MODEL), F32),
        compiler_params=pltpu.CompilerParams(dimension_semantics=("parallel", "parallel"),
                                             vmem_limit_bytes=VMEM_LIMIT),
        name="out_proj_ln",
    )(h, mod3, ya, yb, wo, g, bb)


def _rot_partner(w):
    w4 = w.reshape(w.shape[:-1] + (2, 2, ROPE_FREQS))
    return jnp.stack([-w4[..., 1, :], w4[..., 0, :]], axis=-2).reshape(w.shape)


def _rope_tables(seq, scale_keep, scale_rope):
    n_rows = seq // GRID_W
    rowp = np.repeat(np.arange(n_rows, dtype=np.float32), GRID_W)
    colp = np.tile(np.arange(GRID_W, dtype=np.float32), n_rows)
    inv = (np.float32(ROPE_BASE) ** (-np.arange(ROPE_FREQS, dtype=np.float32) / np.float32(ROPE_FREQS)))
    ang = np.stack([rowp[:, None] * inv, colp[:, None] * inv], axis=1).astype(np.float32)
    cos = np.broadcast_to(np.cos(ang)[:, :, None, :], (seq, 2, 2, ROPE_FREQS)).reshape(seq, DR_A)
    sin = np.broadcast_to(np.sin(ang)[:, :, None, :], (seq, 2, 2, ROPE_FREQS)).reshape(seq, DR_A)
    z32 = np.zeros((seq, DR_A), np.float32)
    t1 = np.concatenate([np.full((seq, DN_A), scale_keep, np.float32), z32, cos * scale_rope], axis=1)
    t2 = np.concatenate([np.zeros((seq, DN_A), np.float32), z32, sin * scale_rope], axis=1)
    return jnp.asarray(t1, F32), jnp.asarray(t2, F32)


def _block_diag(w):
    nb, bs, _ = w.shape
    n = nb * bs
    rows = w.reshape(n, bs)
    ri = lax.broadcasted_iota(jnp.int32, (n, n), 0)
    ci = lax.broadcasted_iota(jnp.int32, (n, n), 1)
    out = jnp.zeros((n, n), w.dtype)
    for o in range(bs):
        out = out + jnp.where((ri // bs == ci // bs) & (ci % bs == o), rows[:, o:o + 1], 0.0)
    return out


def kernel(x, c, ctx, c_ctx, ln_in_g, ln_in_b, w_ada, b_ada, w_in, g_qa, w_qb, g_kva, w_kvb, conv_w, conv_b, w_mq, w_mk, w_mv, w_gate, b_gate, mh_g, skip, w_out, ln_g, ln_b):
    b, t, _ = x.shape
    l = 0
    r2 = lambda v: v.reshape(1, -1)

    wq3 =w_qb[l].reshape(Q_LORA, H_A, DN_A + DR_A)
    wq_r = wq3[..., DN_A:]
    wq = jnp.concatenate([wq3[..., :DN_A], _rot_partner(wq_r), wq_r], axis=-1)
    wq = wq.reshape(Q_LORA, H_A * SLAB).astype(BF16)
    wkv3 = w_kvb[l].reshape(KV_LORA, H_A, DN_A + DV_A)
    wkn = jnp.concatenate([wkv3[..., :DN_A], jnp.zeros((KV_LORA, H_A, SLAB - DN_A), F32)], axis=-1)
    wkn = wkn.reshape(KV_LORA, H_A * SLAB).astype(BF16)
    wvt = wkv3[..., DN_A:].reshape(KV_LORA, W_A).T.astype(BF16)
    wmq = _block_diag(w_mq[l])
    wmk = _block_diag(w_mk[l])
    wmv = _block_diag(w_mv[l])
    perm = jnp.array([0, 1, 2, 3, 8, 9, 10, 11, 4, 5, 6, 7, 12, 13, 14, 15])
    wg = jnp.pad(w_gate[l][:, perm], ((0, 0), (0, LANES - 2 * N_HD))).astype(BF16)
    bg = jnp.pad(b_gate[l][perm], (0, LANES - 2 * N_HD)).reshape(1, LANES)
    wmk_s = (wmk * (DH_B ** -0.5)).astype(BF16)
    wo = w_out[l].astype(BF16)

    sm_scale = (DN_A + DR_A) ** -0.5 * LOG2_E
    t1q, t2q = _rope_tables(t, sm_scale, sm_scale)
    t1k, t2k = _rope_tables(t, 0.0, 1.0)
    n_ctx = ctx.shape[1]
    t1kc = jnp.asarray(np.concatenate([np.zeros((n_ctx, DN_A + DR_A), np.float32),
                                       np.ones((n_ctx, DR_A), np.float32)], axis=1))
    t2kc = jnp.asarray(np.zeros((n_ctx, SLAB), np.float32))

    cc = jnp.concatenate([c, c_ctx[None, :], jnp.zeros((16 - b - 1, D_MODEL), F32)], axis=0)
    mod, wina, winr, wgc, wgm = _prep_call(cc, w_ada[l], r2(b_ada[l]), jnp.swapaxes(w_in[l], 0, 1),
                                           wmq.astype(BF16), wmk.astype(BF16), wmv.astype(BF16), wg)
    mod3 = mod.reshape(16, 1, 3 * D_MODEL)

    lng, lnb = r2(ln_in_g), r2(ln_in_b)
    shared = (wina, winr, r2(g_qa[l]), wq, r2(g_kva[l]), wkn, wvt)
    q, kl, vtl, xm, sza, som, szm, h = _proj_call(x, mod3, None, lng, lnb, *shared, t1q, t2q, t1k, t2k,
                                                  tm=1024, n_sub=2, latent=True)
    kc, vtc, xmc = _proj_call(ctx, mod3, b, lng, lnb, *shared, t1kc, t2kc, t1kc, t2kc,
                              tm=n_ctx, n_sub=1, latent=False)

    hs, xc = _mlstm_call(xm, xmc, conv_w[l], r2(conv_b[l]), wmq.astype(BF16), wmk_s, wmv.astype(BF16),
                         wgc, wgm, bg)
    ya, yb = _attn_call(q, kc, kl, vtc, vtl, sza, hs, som, szm, xc, r2(mh_g[l]), r2(skip[l]), tq=512)
    return _out_call(h, mod3, ya, yb, wo, r2(ln_g[l]), r2(ln_b[l]), tm=1024, n_sub=4)
```

```python
import functools

import numpy as np
import jax
import jax.numpy as jnp
from jax import lax
from jax.experimental import pallas as pl
from jax.experimental.pallas import tpu as pltpu

F32 = jnp.float32
BF16 = jnp.bfloat16

D_MODEL = 1024
CTX_LEN = 256
GRID_W = 64
H_A, DN_A, DR_A, DV_A = 8, 64, 32, 64
W_A = H_A * DV_A
Q_LORA, KV_LORA = 256, 128
ROPE_FREQS = DR_A // 4
ROPE_BASE = 10000.0
H_B, DH_B = 4, 128
W_B = H_B * DH_B
QKV_BS = 4
DEPTH = 1
ALPHA = (2.0 * DEPTH) ** 0.25
LN_EPS = 1e-5
RMS_EPS = 1e-6
LOG2_E = 1.4426950408889634

LANES = 128
MXU_TILE = 256
SLAB = LANES
DV_EXT = DV_A + 16
ATTN_KEY_CHUNK = MXU_TILE
CHUNK = 256
N_HD = 2 * H_B
ST_ROWS = DH_B + 16
VMEM_LIMIT = 56 * 1024 * 1024

_NT = (((1,), (1,)), ((), ()))


def _dot(a, b):
    return jnp.dot(a, b, preferred_element_type=F32)


def _dot_nt(a, b):
    return lax.dot_general(a, b, _NT, preferred_element_type=F32)


def _layer_norm(x):
    mu = jnp.mean(x, axis=-1, keepdims=True)
    xc = x - mu
    var = jnp.mean(xc * xc, axis=-1, keepdims=True)
    return xc * lax.rsqrt(var + LN_EPS)


def _rms_norm(x, g):
    return (x * lax.rsqrt(jnp.mean(x * x, axis=-1, keepdims=True) + RMS_EPS)) * g


def _silu(x):
    return x * jax.nn.sigmoid(x)


def _log_sigmoid(x):
    return jnp.minimum(x, 0.0) - jnp.log1p(jnp.exp(-jnp.abs(x)))


def _ada_kernel(cc_ref, w_ref, b_ref, o_ref):
    a = _silu(cc_ref[...]).astype(BF16)
    mod = _dot(a, w_ref[...].astype(BF16)) + b_ref[...]
    for r in range(o_ref.shape[0]):
        o_ref[r] = mod[r:r + 1, :]


_S_KVA, _S_KR = Q_LORA + KV_LORA, Q_LORA + KV_LORA + DR_A


def _win_kernel(wt_ref, wina_ref, winr_ref):
    n_r = winr_ref.shape[1]
    for j in range(_S_KVA // LANES):
        wina_ref[:, j * LANES:(j + 1) * LANES] = wt_ref[j * LANES:(j + 1) * LANES, :].T.astype(BF16)
    kr = wt_ref[_S_KVA:_S_KR, :]
    a0, b0, a1, b1 = (kr[i * ROPE_FREQS:(i + 1) * ROPE_FREQS, :] for i in range(4))
    blk = jnp.concatenate([jnp.zeros((DN_A, D_MODEL), F32), -b0, a0, -b1, a1, kr], axis=0)
    wina_ref[:, _S_KVA:_S_KVA + LANES] = blk.T.astype(BF16)
    for j in range(n_r // MXU_TILE):
        rows = slice(_S_KR + j * MXU_TILE, _S_KR + (j + 1) * MXU_TILE)
        winr_ref[:, j * MXU_TILE:(j + 1) * MXU_TILE] = wt_ref[rows, :].T.astype(BF16)


def _rope_slab(s, t1, t2):
    return s * t1 + pltpu.roll(s, 32, 1) * t2


def _proj_kernel(x_ref, mod_ref, lng_ref, lnb_ref, wina_ref, winr_ref, gqa_ref, wq_ref, gkva_ref,
                 wkn_ref, wvt_ref, t1q_ref, t2q_ref, t1k_ref, t2k_ref, *out_refs, latent, n_sub):
    if latent:
        q_ref, k_ref, vt_ref, xm_ref, sza_ref, som_ref, szm_ref, h_ref = out_refs
    else:
        k_ref, vt_ref, xm_ref = out_refs
    sub = x_ref.shape[1] // n_sub
    mod = mod_ref[0]
    shift, scale1 = mod[:, 0:D_MODEL], 1.0 + mod[:, D_MODEL:2 * D_MODEL]
    ones_rows = (lax.broadcasted_iota(jnp.int32, (DV_EXT - DV_A, sub), 0) == 0).astype(BF16)

    def norm_stage(s):
        rows = slice(s * sub, (s + 1) * sub)
        h = _layer_norm(x_ref[0, rows, :]) * lng_ref[...] + lnb_ref[...]
        if latent:
            h_ref[0, rows, :] = h
        return (h * scale1 + shift).astype(BF16)

    def proj_stage(s, u):
        rows = slice(s * sub, (s + 1) * sub)
        p0 = _dot(u, wina_ref[...])
        d_xm = _dot(u, winr_ref[:, 512:1024])
        kvn = _rms_norm(p0[:, 256:384], gkva_ref[...]).astype(BF16)
        kn = _dot(kvn, wkn_ref[...])
        d_za = _dot(u, winr_ref[:, 0:512]) if latent else None
        krr = _rope_slab(p0[:, 384:512], t1k_ref[rows, :], t2k_ref[rows, :])
        for hh in range(H_A):
            k_ref[0, hh, rows, :] = (kn[:, hh * SLAB:(hh + 1) * SLAB] + krr).astype(BF16)
        vt = _dot_nt(wvt_ref[...], kvn).astype(BF16)
        for hh in range(H_A):
            vt_ref[0, hh, 0:DV_A, rows] = vt[hh * DV_A:(hh + 1) * DV_A, :]
            vt_ref[0, hh, DV_A:DV_EXT, rows] = ones_rows
        xm_ref[0, rows, :] = d_xm.astype(BF16)
        if latent:
            d_om = _dot(u, winr_ref[:, 1024:1536])
            qn = _rms_norm(p0[:, 0:256], gqa_ref[...]).astype(BF16)
            qs = _dot(qn, wq_ref[...])
            sza_ref[0, rows, :] = _silu(d_za).astype(BF16)
            d_zm = _dot(u, winr_ref[:, 1536:2048])
            t1q, t2q = t1q_ref[rows, :], t2q_ref[rows, :]
            for hh in range(H_A):
                q_ref[0, hh, rows, :] = _rope_slab(qs[:, hh * SLAB:(hh + 1) * SLAB], t1q, t2q).astype(BF16)
            som_ref[0, rows, :] = jax.nn.sigmoid(d_om).astype(BF16)
            szm_ref[0, rows, :] = _silu(d_zm).astype(BF16)

    u = norm_stage(0)
    for s in range(n_sub):
        u_next = norm_stage(s + 1) if s + 1 < n_sub else None
        proj_stage(s, u)
        u = u_next


def _proj_call(x, mod3, mod_row, lng, lnb, wina, winr, gqa, wq, gkva, wkn, wvt, t1q, t2q, t1k, t2k, *,
               tm, n_sub, latent):
    b, t, _ = x.shape
    const = lambda shape: pl.BlockSpec(shape, lambda i, j: (0,) * len(shape))
    tab = pl.BlockSpec((tm, SLAB), (lambda i, j: (j, 0)) if latent else (lambda i, j: (0, 0)))
    row = lambda w: pl.BlockSpec((1, tm, w), lambda i, j: (i, j, 0))
    if mod_row is None:
        mod_spec = pl.BlockSpec((1, 1, 3 * D_MODEL), lambda i, j: (i, 0, 0))
    else:
        mod_spec = pl.BlockSpec((1, 1, 3 * D_MODEL), lambda i, j: (mod_row, 0, 0))
    in_specs = [row(D_MODEL), mod_spec, const((1, D_MODEL)), const((1, D_MODEL)),
                const(wina.shape), const(winr.shape), const((1, Q_LORA)), const((Q_LORA, H_A * SLAB)),
                const((1, KV_LORA)), const((KV_LORA, H_A * SLAB)), const((W_A, KV_LORA)),
                tab, tab, tab, tab]
    k_spec = pl.BlockSpec((1, H_A, tm, SLAB), lambda i, j: (i, 0, j, 0))
    vt_spec = pl.BlockSpec((1, H_A, DV_EXT, tm), lambda i, j: (i, 0, 0, j))
    k_shape = jax.ShapeDtypeStruct((b, H_A, t, SLAB), BF16)
    vt_shape = jax.ShapeDtypeStruct((b, H_A, DV_EXT, t), BF16)
    half = jax.ShapeDtypeStruct((b, t, W_B), BF16)
    if latent:
        out_specs = [k_spec, k_spec, vt_spec, row(W_B), row(W_A), row(W_B), row(W_B), row(D_MODEL)]
        out_shape = [k_shape, k_shape, vt_shape, half, half, half, half,
                     jax.ShapeDtypeStruct((b, t, D_MODEL), F32)]
    else:
        out_specs = [k_spec, vt_spec, row(W_B)]
        out_shape = [k_shape, vt_shape, half]
    return pl.pallas_call(
        functools.partial(_proj_kernel, latent=latent, n_sub=n_sub),
        grid=(b, t // tm),
        in_specs=in_specs, out_specs=out_specs, out_shape=out_shape,
        compiler_params=pltpu.CompilerParams(dimension_semantics=("parallel", "parallel"),
                                             vmem_limit_bytes=VMEM_LIMIT),
        name="in_proj_latent" if latent else "in_proj_ctx",
    )(x, mod3, lng, lnb, wina, winr, gqa, wq, gkva, wkn, wvt, t1q, t2q, t1k, t2k)


def _attn_kernel(q_ref, kc_ref, kl_ref, vtc_ref, vtl_ref, sza_ref, hs_ref, som_ref, szm_ref, xc_ref,
                 mhg_ref, skip_ref, o_ref, yb_ref, s_buf, ot_s):
    n_ctx, t = kc_ref.shape[2], kl_ref.shape[2]
    kc = ATTN_KEY_CHUNK

    def scores(h, slot):
        qh = q_ref[0, h]
        sc = _dot_nt(kc_ref[0, h], qh)
        sk = _dot_nt(kl_ref[0, h], qh)
        s_buf[slot, 0:n_ctx, :] = sc
        s_buf[slot, n_ctx:n_ctx + t, :] = sk
        return jnp.maximum(jnp.max(sc, axis=0, keepdims=True), jnp.max(sk, axis=0, keepdims=True))

    def values(h, slot, m):
        chunks = [(vtc_ref, c * kc, c * kc) for c in range(n_ctx // kc)]
        chunks += [(vtl_ref, c * kc, n_ctx + c * kc) for c in range(t // kc)]
        acc = None
        for vref, v0, s0 in chunks:
            p = jnp.exp2(s_buf[slot, s0:s0 + kc, :] - m).astype(BF16)
            d = _dot(vref[0, h, :, v0:v0 + kc], p)
            acc = d if acc is None else acc + d
        rows = pl.ds(pl.multiple_of(h * DV_A, DV_A), DV_A)
        ot_s[rows, :] = acc[0:DV_A, :] / acc[DV_A:DV_A + 1, :]

    def cell_out(pc):
        rows = pl.ds(pl.multiple_of(pc * LANES, LANES), LANES)
        yb_ref[0, rows, :] = _mlstm_out(hs_ref[0, pc], som_ref[0, rows, :], szm_ref[0, rows, :],
                                        xc_ref[0, rows, :], mhg_ref[...], skip_ref[...])

    def body(i, m_even):
        h = 2 * i
        m_odd = scores(h + 1, 1)
        cell_out(i + 1)
        values(h, 0, m_even)
        m_even = scores(h + 2, 0)
        values(h + 1, 1, m_odd)
        return m_even

    assert hs_ref.shape[1] == H_A // 2
    m_first = scores(0, 0)
    cell_out(0)
    m_even = lax.fori_loop(0, H_A // 2 - 1, body, m_first)
    m_odd = scores(H_A - 1, 1)
    values(H_A - 2, 0, m_even)
    values(H_A - 1, 1, m_odd)
    o_ref[0] = (ot_s[...].T * sza_ref[0].astype(F32)).astype(BF16)


def _attn_call(q, kc, kl, vtc, vtl, sza, hs, som, szm, xc, mhg, skip, *, tq):
    b, _, t, _ = q.shape
    n_ctx = kc.shape[2] // b
    head_blk = lambda n, w: pl.BlockSpec((1, H_A, n, w), lambda i, j: (i, 0, 0, 0))
    const = lambda shape: pl.BlockSpec(shape, lambda i, j: (0,) * len(shape))
    row = lambda w: pl.BlockSpec((1, tq, w), lambda i, j: (i, j, 0))
    half = jax.ShapeDtypeStruct((b, t, W_A), BF16)
    return pl.pallas_call(
        _attn_kernel,
        grid=(b, t // tq),
        in_specs=[pl.BlockSpec((1, H_A, tq, SLAB), lambda i, j: (i, 0, j, 0)),
                  pl.BlockSpec((1, H_A, n_ctx, SLAB), lambda i, j: (0, 0, i, 0)), head_blk(t, SLAB),
                  pl.BlockSpec((1, H_A, DV_EXT, n_ctx), lambda i, j: (0, 0, 0, i)), head_blk(DV_EXT, t),
                  row(W_A),
                  pl.BlockSpec((1, tq // LANES, W_B, LANES), lambda i, j: (i, j, 0, 0)),
                  row(W_B), row(W_B), row(W_B), const((1, W_B)), const((1, W_B))],
        out_specs=[row(W_A), row(W_B)],
        out_shape=[half, jax.ShapeDtypeStruct((b, t, W_B), BF16)],
        scratch_shapes=[pltpu.VMEM((2, n_ctx + t, tq), F32),
                        pltpu.VMEM((W_A, tq), F32)],
        compiler_params=pltpu.CompilerParams(dimension_semantics=("parallel", "parallel"),
                                             vmem_limit_bytes=VMEM_LIMIT),
        name="mla_attention",
    )(q, kc, kl, vtc, vtl, sza, hs, som, szm, xc, mhg, skip)


def _gate_fold_kernel(wq_ref, wk_ref, wv_ref, wg_ref, wgc_ref, wgm_ref):
    wgc_ref[...] = (_dot(wq_ref[...], wg_ref[0:W_B, :])
                    + _dot(wk_ref[...], wg_ref[W_B:2 * W_B, :])).astype(BF16)
    wgm_ref[...] = _dot(wv_ref[...], wg_ref[2 * W_B:3 * W_B, :]).astype(BF16)


def _prep_kernel(cc_ref, wada_ref, bada_ref, wt_ref, wq_ref, wk_ref, wv_ref, wg_ref,
                 mod_ref, wina_ref, winr_ref, wgc_ref, wgm_ref):
    _ada_kernel(cc_ref, wada_ref, bada_ref, mod_ref)
    _win_kernel(wt_ref, wina_ref, winr_ref)
    _gate_fold_kernel(wq_ref, wk_ref, wv_ref, wg_ref, wgc_ref, wgm_ref)


def _prep_call(cc, n_mod, w_ada, b_ada, wt, wq, wk, wv, wg):
    n_in, d = wt.shape
    gate = jax.ShapeDtypeStruct((W_B, LANES), BF16)
    return pl.pallas_call(
        _prep_kernel,
        out_shape=[jax.ShapeDtypeStruct((n_mod, 1, w_ada.shape[1]), F32),
                   jax.ShapeDtypeStruct((d, _S_KVA + LANES), BF16),
                   jax.ShapeDtypeStruct((d, n_in - _S_KR), BF16), gate, gate],
        compiler_params=pltpu.CompilerParams(vmem_limit_bytes=VMEM_LIMIT),
        name="param_prep",
    )(cc, w_ada, b_ada, wt, wq, wk, wv, wg)


_PAD = 8


def _lane_scans(rows_per_chunk, combine, fill, out):
    npc = CHUNK // LANES
    flat = [x[:, i * LANES:(i + 1) * LANES] for x in rows_per_chunk for i in range(npc)]
    lane = lax.broadcasted_iota(jnp.int32, flat[0].shape, 1)
    pre, suf, sh = list(flat), list(flat), 1
    while sh < LANES:
        pre = [combine(p, jnp.where(lane >= sh, pltpu.roll(p, sh, 1), fill)) for p in pre]
        suf = [combine(s, jnp.where(lane < LANES - sh, pltpu.roll(s, LANES - sh, 1), fill)) for s in suf]
        sh *= 2
        yield
    for c in range(len(rows_per_chunk)):
        p, s = pre[c * npc:(c + 1) * npc], suf[c * npc:(c + 1) * npc]
        tot = [x[:, LANES - 1:LANES] for x in p]
        run = None
        for i in range(npc):
            if run is not None:
                p[i] = combine(p[i], run)
            run = tot[i] if run is None else combine(run, tot[i])
        run = None
        for i in reversed(range(npc)):
            if run is not None:
                s[i] = combine(s[i], run)
            run = tot[i] if run is None else combine(run, tot[i])
        out.append((jnp.concatenate(p, axis=1), jnp.concatenate(s, axis=1)))


def _gate_tables(gts, dests):
    L = CHUNK
    fwd_rows = lax.broadcasted_iota(jnp.int32, (N_HD, L), 0) < H_B
    lis = [gt[0:N_HD, :] for gt in gts]
    sums = []
    yield from _lane_scans([_log_sigmoid(gt[N_HD:2 * N_HD, :]) for gt in gts], jnp.add, 0.0, sums)
    cums = [jnp.where(fwd_rows, ps, ss) for ps, ss in sums]
    rs = [li - cum for li, cum in zip(lis, cums)]
    maxs = []
    yield from _lane_scans(rs, jnp.maximum, -jnp.inf, maxs)
    for (grow_ref, c, gcol_ref, r0), cum, r, (pm, sm) in zip(dests, cums, rs, maxs):
        grow_ref[c] = jnp.concatenate([cum, r, jnp.where(fwd_rows, pm, sm)], axis=0)
        gcol_ref[r0:r0 + L, :] = jnp.concatenate([r, jnp.zeros((LANES - N_HD, L), F32)], axis=0).T


def _fill_pad(pad_s, xm_ref, n):
    zrow = jnp.zeros((_PAD, W_B), F32)
    pad_s[0:_PAD, :] = zrow
    pad_s[_PAD:_PAD + n, :] = xm_ref[...].astype(F32)
    pad_s[_PAD + n:2 * _PAD + n, :] = zrow


def _conv_gate_stage(c, pad_s, xm_ref, cw_ref, cb_ref, wgc_ref, wgm_ref, bg_ref, xc_out, xcb_s):
    L = CHUNK
    rows = slice(c * L, (c + 1) * L)
    r0 = c * L + _PAD
    row_id = lax.broadcasted_iota(jnp.int32, (L, W_B), 0)
    xcur = pad_s[r0:r0 + L, :]
    xprev = jnp.where(row_id == 0, pad_s[r0 - 1:r0, :], pltpu.roll(xcur, 1, 0))
    xnext = jnp.where(row_id == L - 1, pad_s[r0 + L:r0 + L + 1, :], pltpu.roll(xcur, L - 1, 0))
    pre = cb_ref[...] + xprev * cw_ref[0:1, :] + xcur * cw_ref[1:2, :] + xnext * cw_ref[2:3, :]
    xc = _silu(pre)
    xcb = xc.astype(BF16)
    if xc_out is not None:
        xc_out[rows, :] = xcb
    xcb_s[rows, :] = xcb
    g = _dot(xcb, wgc_ref[...]) + _dot(xm_ref[rows, :], wgm_ref[...]) + bg_ref[...]
    return g.T[0:2 * N_HD, :]


def _headwise_stage(c, xm_ref, xcb_s, wq_ref, wk_ref, wv_ref, q_s, k_s, vt_s):
    L = CHUNK
    rows = slice(c * L, (c + 1) * L)
    for hf in range(W_B // MXU_TILE):
        cs = slice(hf * MXU_TILE, (hf + 1) * MXU_TILE)
        xcb = xcb_s[rows, cs]
        if q_s is not None:
            q_s[rows, cs] = _dot(xcb, wq_ref[cs, cs]).astype(BF16)
        k_s[rows, cs] = _dot(xcb, wk_ref[cs, cs]).astype(BF16)
        vt_s[c, cs, :] = _dot(xm_ref[rows, cs], wv_ref[cs, cs]).T.astype(BF16)


_TAB_ROWS = 5 * N_HD


def _chain_tables(growc_s, grow_s, tab_s, nc):
    L = CHUNK
    fwd_rows = lax.broadcasted_iota(jnp.int32, (N_HD, L), 0) < H_B
    gc = growc_s[0]
    steps = [(gc, gc)] + [(grow_s[i], grow_s[nc - 1 - i]) for i in range(nc)]
    parts = []
    for g_f, g_b in steps:
        pick = lambda a: jnp.where(fwd_rows, g_f[a * N_HD:(a + 1) * N_HD, :], g_b[a * N_HD:(a + 1) * N_HD, :])
        cum, r, pm = pick(0), pick(1), pick(2)
        b_end = jnp.where(fwd_rows[:, 0:1], cum[:, L - 1:L], cum[:, 0:1])
        w = r + b_end
        parts.append((cum, pm, b_end, w, jnp.max(w, axis=1, keepdims=True)))
    m0 = jnp.zeros((N_HD, 1), F32)
    for idx, (cum, pm, b_end, w, wmax) in enumerate(parts):
        m_new = jnp.maximum(b_end + m0, wmax)
        mu = jnp.maximum(m0, pm)
        decay = jnp.broadcast_to(jnp.exp(b_end + m0 - m_new), (N_HD, L))
        tab_s[idx] = jnp.concatenate([mu, jnp.exp(m0 - mu), jnp.exp(-(cum + mu)),
                                      jnp.exp(w - m_new), decay], axis=0)
        m0 = m_new


def _tab(tab, a, hd):
    return tab[a * N_HD + hd:a * N_HD + hd + 1, :]


def _state_update(st_ref, hd, vt_ext, k_c, ws_row, decay_hd):
    vw = (vt_ext.astype(F32) * ws_row).astype(BF16)
    st_ref[hd] = decay_hd * st_ref[hd] + _dot(vw, k_c)


def _mlstm_kernel(xm_ref, xmc_ref, cw_ref, cb_ref, wq_ref, wk_ref, wv_ref, wgc_ref, wgm_ref, bg_ref,
                  hs_ref, xc_ref,
                  pad_s, xcb_s, q_s, k_s, vt_s, gcol_s, grow_s,
                  padc_s, xcbc_s, kc_s, vtc_s, gcolc_s, growc_s, tab_s, st_s):
    L = CHUNK
    t = xm_ref.shape[1]
    nc = t // L
    row_i = lax.broadcasted_iota(jnp.int32, (L, L), 0)
    col_i = lax.broadcasted_iota(jnp.int32, (L, L), 1)
    tri = (row_i <= col_i, row_i >= col_i)
    ones_rows = (lax.broadcasted_iota(jnp.int32, (ST_ROWS - DH_B, L), 0) == 0).astype(BF16)
    hsl = lambda hh: slice(hh * DH_B, (hh + 1) * DH_B)
    xm, xmc = xm_ref.at[0], xmc_ref.at[0]
    gate_w = (cw_ref, cb_ref, wgc_ref, wgm_ref, bg_ref)

    _fill_pad(padc_s, xmc, CTX_LEN)
    _fill_pad(pad_s, xm, t)
    gts = [_conv_gate_stage(0, padc_s, xmc, *gate_w, None, xcbc_s)]
    gts += [_conv_gate_stage(c, pad_s, xm, *gate_w, xc_ref.at[0], xcb_s) for c in range(nc)]
    dests = [(growc_s, 0, gcolc_s, 0)] + [(grow_s, c, gcol_s, c * L) for c in range(nc)]
    tables = _gate_tables(gts, dests)
    _headwise_stage(0, xmc, xcbc_s, wq_ref, wk_ref, wv_ref, None, kc_s, vtc_s)
    for c in range(nc):
        next(tables, None)
        next(tables, None)
        _headwise_stage(c, xm, xcb_s, wq_ref, wk_ref, wv_ref, q_s, k_s, vt_s)
    for _ in tables:
        pass

    _chain_tables(growc_s, grow_s, tab_s, nc)
    st_s[...] = jnp.zeros(st_s.shape, F32)
    tab = tab_s[0]
    for hd in range(N_HD):
        hh = hd % H_B
        vt_ext = jnp.concatenate([vtc_s[0, hsl(hh), :], ones_rows], axis=0)
        _state_update(st_s, hd, vt_ext, kc_s[:, hsl(hh)], _tab(tab, 3, hd), _tab(tab, 4, hd)[:, 0:DH_B])

    def body(i, carry, *, accumulate):
        ci = (i, nc - 1 - i)
        rows = tuple(pl.ds(pl.multiple_of(c * L, L), L) for c in ci)
        tab = tab_s[i + 1]
        gcols = tuple(gcol_s[r, :] for r in rows)
        live = {}

        def stage_scores(hd):
            d, hh = hd // H_B, hd % H_B
            k_c = k_s[rows[d], hsl(hh)]
            vt_ext = jnp.concatenate([vt_s[ci[d], hsl(hh), :], ones_rows], axis=0)
            lhs = jnp.concatenate([k_c, st_s[hd].astype(BF16)], axis=0)
            live[hd] = (k_c, vt_ext, _dot_nt(lhs, q_s[rows[d], hsl(hh)]))

        def stage_gate(hd):
            d = hd // H_B
            k_c, vt_ext, res = live[hd]
            rcol = gcols[d][:, hd:hd + 1]
            e = jnp.exp(jnp.where(tri[d], rcol - _tab(tab, 0, hd), -jnp.inf))
            p = (res[0:L, :] * e).astype(BF16)
            _state_update(st_s, hd, vt_ext, k_c, _tab(tab, 3, hd), _tab(tab, 4, hd)[:, 0:DH_B])
            live[hd] = (vt_ext, res[L:L + ST_ROWS, :], p)

        def stage_out(hd):
            d, hh = hd // H_B, hd % H_B
            vt_ext, inter, p = live.pop(hd)
            tot = _tab(tab, 1, hd) * inter + _dot(vt_ext, p)
            den = tot[DH_B:DH_B + 1, :]
            h_t = tot[0:DH_B, :] / jnp.maximum(jnp.abs(den), _tab(tab, 2, hd))
            for pc in range(L // LANES):
                piece = h_t[:, pc * LANES:(pc + 1) * LANES]
                if accumulate:
                    hs_ref[0, ci[d] * (L // LANES) + pc, hsl(hh), :] += piece
                else:
                    hs_ref[0, ci[d] * (L // LANES) + pc, hsl(hh), :] = piece

        stages = (stage_scores, stage_gate, stage_out)
        for step in range(N_HD + len(stages) - 1):
            for si in range(len(stages)):
                if 0 <= step - si < N_HD:
                    stages[si](step - si)
        return carry

    lax.fori_loop(0, nc // 2, functools.partial(body, accumulate=False), 0)
    lax.fori_loop(nc // 2, nc, functools.partial(body, accumulate=True), 0)


def _mlstm_out(hs_t, som, szm, xc, mhg, skip):
    hsum = hs_t.T * som.astype(F32)
    parts = [_layer_norm(hsum[:, hh * DH_B:(hh + 1) * DH_B]) for hh in range(H_B)]
    hb = jnp.concatenate(parts, axis=1) * mhg + skip * xc.astype(F32)
    return (hb * szm.astype(F32)).astype(BF16)


def _mlstm_call(xm, xmc, conv_w, conv_b, wq, wk, wv, wgc, wgm, bg):
    b, t, _ = xm.shape
    nc = t // CHUNK
    assert nc % 2 == 0
    seq = lambda n: pl.BlockSpec((1, n, W_B), lambda i: (i, 0, 0))
    const = lambda shape: pl.BlockSpec(shape, lambda i: (0,) * len(shape))
    scratch = [
        pltpu.VMEM((t + 2 * _PAD, W_B), F32),
        pltpu.VMEM((t, W_B), BF16),
        pltpu.VMEM((t, W_B), BF16),
        pltpu.VMEM((t, W_B), BF16),
        pltpu.VMEM((nc, W_B, CHUNK), BF16),
        pltpu.VMEM((t, LANES), F32),
        pltpu.VMEM((nc, 3 * N_HD, CHUNK), F32),
        pltpu.VMEM((CTX_LEN + 2 * _PAD, W_B), F32),
        pltpu.VMEM((CTX_LEN, W_B), BF16),
        pltpu.VMEM((CTX_LEN, W_B), BF16),
        pltpu.VMEM((1, W_B, CHUNK), BF16),
        pltpu.VMEM((CTX_LEN, LANES), F32),
        pltpu.VMEM((1, 3 * N_HD, CHUNK), F32),
        pltpu.VMEM((nc + 1, _TAB_ROWS, CHUNK), F32),
        pltpu.VMEM((N_HD, ST_ROWS, DH_B), F32),
    ]
    return pl.pallas_call(
        _mlstm_kernel,
        grid=(b,),
        in_specs=[seq(t), seq(CTX_LEN), const((3, W_B)), const((1, W_B)),
                  const((W_B, W_B)), const((W_B, W_B)), const((W_B, W_B)),
                  const((W_B, LANES)), const((W_B, LANES)), const((1, LANES))],
        out_specs=[pl.BlockSpec((1, t // LANES, W_B, LANES), lambda i: (i, 0, 0, 0)), seq(t)],
        out_shape=[jax.ShapeDtypeStruct((b, t // LANES, W_B, LANES), F32),
                   jax.ShapeDtypeStruct((b, t, W_B), BF16)],
        scratch_shapes=scratch,
        compiler_params=pltpu.CompilerParams(dimension_semantics=("arbitrary",),
                                             vmem_limit_bytes=VMEM_LIMIT),
        name="mlstm_bidir",
    )(xm, xmc, conv_w, conv_b, wq, wk, wv, wgc, wgm, bg)


def _out_kernel(h_ref, mod_ref, ya_ref, yb_ref, wo_ref, g_ref, b_ref, o_ref, *, n_sub):
    sub = h_ref.shape[1] // n_sub
    gate = mod_ref[0][:, 2 * D_MODEL:3 * D_MODEL]

    def mix_stage(s):
        rows = slice(s * sub, (s + 1) * sub)
        return (_dot(ya_ref[0, rows, :], wo_ref[0:W_A, :])
                + _dot(yb_ref[0, rows, :], wo_ref[W_A:W_A + W_B, :]))

    def norm_stage(s, y):
        rows = slice(s * sub, (s + 1) * sub)
        o_ref[0, rows, :] = _layer_norm(ALPHA * h_ref[0, rows, :] + gate * y) * g_ref[...] + b_ref[...]

    y = mix_stage(0)
    for s in range(n_sub):
        y_next = mix_stage(s + 1) if s + 1 < n_sub else None
        norm_stage(s, y)
        y = y_next


def _out_call(h, mod3, ya, yb, wo, g, bb, *, tm, n_sub):
    b, t, _ = h.shape
    const = lambda shape: pl.BlockSpec(shape, lambda i, j: (0,) * len(shape))
    row = lambda w: pl.BlockSpec((1, tm, w), lambda i, j: (i, j, 0))
    return pl.pallas_call(
        functools.partial(_out_kernel, n_sub=n_sub),
        grid=(b, t // tm),
        in_specs=[row(D_MODEL), pl.BlockSpec((1, 1, 3 * D_MODEL), lambda i, j: (i, 0, 0)),
                  row(W_A), row(W_B),
                  const((W_A + W_B, D_MODEL)), const((1, D_MODEL)), const((1, D_MODEL))],
        out_specs=row(D_MODEL),
        out_shape=jax.ShapeDtypeStruct((b, t, D_MODEL), F32),
        compiler_params=pltpu.CompilerParams(dimension_semantics=("parallel", "parallel"),
                                             vmem_limit_bytes=VMEM_LIMIT),
        name="out_proj_ln",
    )(h, mod3, ya, yb, wo, g, bb)


def _rot_partner(w):
    w4 = w.reshape(w.shape[:-1] + (2, 2, ROPE_FREQS))
    return jnp.stack([-w4[..., 1, :], w4[..., 0, :]], axis=-2).reshape(w.shape)


def _rope_tables(seq, scale_keep, scale_rope):
    n_rows = seq // GRID_W
    rowp = np.repeat(np.arange(n_rows, dtype=np.float32), GRID_W)
    colp = np.tile(np.arange(GRID_W, dtype=np.float32), n_rows)
    inv = (np.float32(ROPE_BASE) ** (-np.arange(ROPE_FREQS, dtype=np.float32) / np.float32(ROPE_FREQS)))
    ang = np.stack([rowp[:, None] * inv, colp[:, None] * inv], axis=1).astype(np.float32)
    cos = np.broadcast_to(np.cos(ang)[:, :, None, :], (seq, 2, 2, ROPE_FREQS)).reshape(seq, DR_A)
    sin = np.broadcast_to(np.sin(ang)[:, :, None, :], (seq, 2, 2, ROPE_FREQS)).reshape(seq, DR_A)
    z32 = np.zeros((seq, DR_A), np.float32)
    t1 = np.concatenate([np.full((seq, DN_A), scale_keep, np.float32), z32, cos * scale_rope], axis=1)
    t2 = np.concatenate([np.zeros((seq, DN_A), np.float32), z32, sin * scale_rope], axis=1)
    return jnp.asarray(t1, F32), jnp.asarray(t2, F32)


def _block_diag(w):
    nb, bs, _ = w.shape
    n = nb * bs
    rows = w.reshape(n, bs)
    ri = lax.broadcasted_iota(jnp.int32, (n, n), 0)
    ci = lax.broadcasted_iota(jnp.int32, (n, n), 1)
    out = jnp.zeros((n, n), w.dtype)
    for o in range(bs):
        out = out + jnp.where((ri // bs == ci // bs) & (ci % bs == o), rows[:, o:o + 1], 0.0)
    return out


def kernel(x, c, ctx, c_ctx, ln_in_g, ln_in_b, w_ada, b_ada, w_in, g_qa, w_qb, g_kva, w_kvb, conv_w, conv_b, w_mq, w_mk, w_mv, w_gate, b_gate, mh_g, skip, w_out, ln_g, ln_b):
    b, t, _ = x.shape
    l = 0
    r2 = lambda v: v.reshape(1, -1)

    wq3 =w_qb[l].reshape(Q_LORA, H_A, DN_A + DR_A)
    wq_r = wq3[..., DN_A:]
    wq = jnp.concatenate([wq3[..., :DN_A], _rot_partner(wq_r), wq_r], axis=-1)
    wq = wq.reshape(Q_LORA, H_A * SLAB).astype(BF16)
    wkv3 = w_kvb[l].reshape(KV_LORA, H_A, DN_A + DV_A)
    wkn = jnp.concatenate([wkv3[..., :DN_A], jnp.zeros((KV_LORA, H_A, SLAB - DN_A), F32)], axis=-1)
    wkn = wkn.reshape(KV_LORA, H_A * SLAB).astype(BF16)
    wvt = wkv3[..., DN_A:].reshape(KV_LORA, W_A).T.astype(BF16)
    wmq = _block_diag(w_mq[l])
    wmk = _block_diag(w_mk[l])
    wmv = _block_diag(w_mv[l])
    regroup = lambda g: jnp.concatenate(
        [g[..., 0:H_B], g[..., 2 * H_B:3 * H_B], g[..., H_B:2 * H_B], g[..., 3 * H_B:4 * H_B],
         jnp.zeros(g.shape[:-1] + (LANES - 2 * N_HD,), g.dtype)], axis=-1)
    wg = regroup(w_gate[l]).astype(BF16)
    bg = regroup(b_gate[l]).reshape(1, LANES)
    wmk_s = (wmk * (DH_B ** -0.5)).astype(BF16)
    wo = w_out[l].astype(BF16)

    sm_scale = (DN_A + DR_A) ** -0.5 * LOG2_E
    t1q, t2q = _rope_tables(t, sm_scale, sm_scale)
    t1k, t2k = _rope_tables(t, 0.0, 1.0)
    n_ctx = ctx.shape[1]
    ctx_tm = min(1024, b * n_ctx)
    t1kc = jnp.asarray(np.concatenate([np.zeros((ctx_tm, DN_A + DR_A), np.float32),
                                       np.ones((ctx_tm, DR_A), np.float32)], axis=1))
    t2kc = jnp.asarray(np.zeros((ctx_tm, SLAB), np.float32))

    cc = jnp.concatenate([c, c_ctx[None, :], jnp.zeros((16 - b - 1, D_MODEL), F32)], axis=0)
    mod3, wina, winr, wgc, wgm = _prep_call(cc, b + 1, w_ada[l], r2(b_ada[l]), jnp.swapaxes(w_in[l], 0, 1),
                                            wmq.astype(BF16), wmk.astype(BF16), wmv.astype(BF16), wg)

    lng, lnb = r2(ln_in_g), r2(ln_in_b)
    shared = (wina, winr, r2(g_qa[l]), wq, r2(g_kva[l]), wkn, wvt)
    q, kl, vtl, xm, sza, som, szm, h = _proj_call(x, mod3, None, lng, lnb, *shared, t1q, t2q, t1k, t2k,
                                                  tm=1024, n_sub=2, latent=True)
    kc, vtc, xmc = _proj_call(ctx.reshape(1, b * n_ctx, D_MODEL), mod3, b, lng, lnb, *shared,
                              t1kc, t2kc, t1kc, t2kc, tm=ctx_tm, n_sub=2, latent=False)
    xmc = xmc.reshape(b, n_ctx, W_B)

    hs, xc = _mlstm_call(xm, xmc, conv_w[l], r2(conv_b[l]), wmq.astype(BF16), wmk_s, wmv.astype(BF16),
                         wgc, wgm, bg)
    ya, yb = _attn_call(q, kc, kl, vtc, vtl, sza, hs, som, szm, xc, r2(mh_g[l]), r2(skip[l]), tq=512)
    return _out_call(h, mod3, ya, yb, wo, r2(ln_g[l]), r2(ln_b[l]), tm=1024, n_sub=4)
```

```python
import functools

import numpy as np
import jax
import jax.numpy as jnp
from jax import lax
from jax.experimental import pallas as pl
from jax.experimental.pallas import tpu as pltpu

F32 = jnp.float32
BF16 = jnp.bfloat16

D_MODEL = 1024
CTX_LEN = 256
GRID_W = 64
H_A, DN_A, DR_A, DV_A = 8, 64, 32, 64
W_A = H_A * DV_A
Q_LORA, KV_LORA = 256, 128
ROPE_FREQS = DR_A // 4
ROPE_BASE = 10000.0
H_B, DH_B = 4, 128
W_B = H_B * DH_B
QKV_BS = 4
DEPTH = 1
ALPHA = (2.0 * DEPTH) ** 0.25
LN_EPS = 1e-5
RMS_EPS = 1e-6
LOG2_E = 1.4426950408889634

LANES = 128
MXU_TILE = 256
SLAB = LANES
DV_EXT = DV_A + 16
ATTN_KEY_CHUNK = MXU_TILE
CHUNK = 256
N_HD = 2 * H_B
ST_ROWS = DH_B + 16
VMEM_LIMIT = 56 * 1024 * 1024

_NT = (((1,), (1,)), ((), ()))


def _dot(a, b):
    return jnp.dot(a, b, preferred_element_type=F32)


def _dot_nt(a, b):
    return lax.dot_general(a, b, _NT, preferred_element_type=F32)


def _layer_norm(x):
    mu = jnp.mean(x, axis=-1, keepdims=True)
    xc = x - mu
    var = jnp.mean(xc * xc, axis=-1, keepdims=True)
    return xc * lax.rsqrt(var + LN_EPS)


def _rms_norm(x, g):
    return (x * lax.rsqrt(jnp.mean(x * x, axis=-1, keepdims=True) + RMS_EPS)) * g


def _silu(x):
    return x * jax.nn.sigmoid(x)


def _log_sigmoid(x):
    return jnp.minimum(x, 0.0) - jnp.log1p(jnp.exp(-jnp.abs(x)))


def _ada_kernel(c_ref, cctx_ref, w_ref, b_ref, o_ref):
    first = lax.broadcasted_iota(jnp.int32, (8, D_MODEL), 0) == 0
    cc = jnp.concatenate([c_ref[...], jnp.where(first, cctx_ref[...], 0.0)], axis=0)
    a = _silu(cc).astype(BF16)
    mod = _dot(a, w_ref[...].astype(BF16)) + b_ref[...]
    for r in range(o_ref.shape[0]):
        o_ref[r] = mod[r:r + 1, :]


_S_KVA, _S_KR = Q_LORA + KV_LORA, Q_LORA + KV_LORA + DR_A


def _win_kernel(wt_ref, wina_ref, winr_ref):
    n_r = winr_ref.shape[1]
    for j in range(_S_KVA // LANES):
        wina_ref[:, j * LANES:(j + 1) * LANES] = wt_ref[j * LANES:(j + 1) * LANES, :].T.astype(BF16)
    kr = wt_ref[_S_KVA:_S_KR, :]
    a0, b0, a1, b1 = (kr[i * ROPE_FREQS:(i + 1) * ROPE_FREQS, :] for i in range(4))
    blk = jnp.concatenate([jnp.zeros((DN_A, D_MODEL), F32), -b0, a0, -b1, a1, kr], axis=0)
    wina_ref[:, _S_KVA:_S_KVA + LANES] = blk.T.astype(BF16)
    for j in range(n_r // MXU_TILE):
        rows = slice(_S_KR + j * MXU_TILE, _S_KR + (j + 1) * MXU_TILE)
        winr_ref[:, j * MXU_TILE:(j + 1) * MXU_TILE] = wt_ref[rows, :].T.astype(BF16)


def _rope_slab(s, t1, t2):
    return s * t1 + pltpu.roll(s, 32, 1) * t2


def _proj_kernel(x_ref, mod_ref, lng_ref, lnb_ref, wina_ref, winr_ref, gqa_ref, wq_ref, gkva_ref,
                 wkn_ref, wvt_ref, *refs, latent, n_sub):
    if latent:
        t1q_ref, t2q_ref, t1k_ref, t2k_ref, q_ref, k_ref, vt_ref, xm_ref, sza_ref, som_ref, szm_ref, h_ref = refs
    else:
        k_ref, vt_ref, xm_ref = refs
    sub = x_ref.shape[1] // n_sub
    mod = mod_ref[0]
    shift, scale1 = mod[:, 0:D_MODEL], 1.0 + mod[:, D_MODEL:2 * D_MODEL]
    ones_rows = (lax.broadcasted_iota(jnp.int32, (DV_EXT - DV_A, sub), 0) == 0).astype(BF16)

    def norm_stage(s):
        rows = slice(s * sub, (s + 1) * sub)
        h = _layer_norm(x_ref[0, rows, :]) * lng_ref[...] + lnb_ref[...]
        if latent:
            h_ref[0, rows, :] = h
        return (h * scale1 + shift).astype(BF16)

    def proj_stage(s, u):
        rows = slice(s * sub, (s + 1) * sub)
        p0 = _dot(u, wina_ref[...])
        d_xm = _dot(u, winr_ref[:, 512:1024])
        kvn = _rms_norm(p0[:, 256:384], gkva_ref[...]).astype(BF16)
        kn = _dot(kvn, wkn_ref[...])
        d_za = _dot(u, winr_ref[:, 0:512]) if latent else None
        if latent:
            krr = _rope_slab(p0[:, 384:512], t1k_ref[rows, :], t2k_ref[rows, :])
        else:
            rope_lanes = lax.broadcasted_iota(jnp.int32, (sub, SLAB), 1) >= SLAB - DR_A
            krr = jnp.where(rope_lanes, p0[:, 384:512], 0.0)
        for hh in range(H_A):
            k_ref[0, hh, rows, :] = (kn[:, hh * SLAB:(hh + 1) * SLAB] + krr).astype(BF16)
        vt = _dot_nt(wvt_ref[...], kvn).astype(BF16)
        for hh in range(H_A):
            vt_ref[0, hh, 0:DV_A, rows] = vt[hh * DV_A:(hh + 1) * DV_A, :]
            vt_ref[0, hh, DV_A:DV_EXT, rows] = ones_rows
        xm_ref[0, rows, :] = d_xm.astype(BF16)
        if latent:
            d_om = _dot(u, winr_ref[:, 1024:1536])
            qn = _rms_norm(p0[:, 0:256], gqa_ref[...]).astype(BF16)
            qs = _dot(qn, wq_ref[...])
            sza_ref[0, rows, :] = _silu(d_za).astype(BF16)
            d_zm = _dot(u, winr_ref[:, 1536:2048])
            t1q, t2q = t1q_ref[rows, :], t2q_ref[rows, :]
            for hh in range(H_A):
                q_ref[0, hh, rows, :] = _rope_slab(qs[:, hh * SLAB:(hh + 1) * SLAB], t1q, t2q).astype(BF16)
            som_ref[0, rows, :] = jax.nn.sigmoid(d_om).astype(BF16)
            szm_ref[0, rows, :] = _silu(d_zm).astype(BF16)

    u = norm_stage(0)
    for s in range(n_sub):
        u_next = norm_stage(s + 1) if s + 1 < n_sub else None
        proj_stage(s, u)
        u = u_next


def _proj_call(x, mod3, mod_row, lng, lnb, wina, winr, gqa, wq, gkva, wkn, wvt, *tables, tm, n_sub, latent):
    b, t, _ = x.shape
    const = lambda shape: pl.BlockSpec(shape, lambda i, j: (0,) * len(shape))
    tab = pl.BlockSpec((tm, SLAB), lambda i, j: (j, 0))
    assert len(tables) == (4 if latent else 0)
    row = lambda w: pl.BlockSpec((1, tm, w), lambda i, j: (i, j, 0))
    if mod_row is None:
        mod_spec = pl.BlockSpec((1, 1, 3 * D_MODEL), lambda i, j: (i, 0, 0))
    else:
        mod_spec = pl.BlockSpec((1, 1, 3 * D_MODEL), lambda i, j: (mod_row, 0, 0))
    in_specs = [row(D_MODEL), mod_spec, const((1, D_MODEL)), const((1, D_MODEL)),
                const(wina.shape), const(winr.shape), const((1, Q_LORA)), const((Q_LORA, H_A * SLAB)),
                const((1, KV_LORA)), const((KV_LORA, H_A * SLAB)), const((W_A, KV_LORA))]
    in_specs += [tab] * len(tables)
    k_spec = pl.BlockSpec((1, H_A, tm, SLAB), lambda i, j: (i, 0, j, 0))
    vt_spec = pl.BlockSpec((1, H_A, DV_EXT, tm), lambda i, j: (i, 0, 0, j))
    k_shape = jax.ShapeDtypeStruct((b, H_A, t, SLAB), BF16)
    vt_shape = jax.ShapeDtypeStruct((b, H_A, DV_EXT, t), BF16)
    half = jax.ShapeDtypeStruct((b, t, W_B), BF16)
    if latent:
        out_specs = [k_spec, k_spec, vt_spec, row(W_B), row(W_A), row(W_B), row(W_B), row(D_MODEL)]
        out_shape = [k_shape, k_shape, vt_shape, half, half, half, half,
                     jax.ShapeDtypeStruct((b, t, D_MODEL), F32)]
    else:
        out_specs = [k_spec, vt_spec, row(W_B)]
        out_shape = [k_shape, vt_shape, half]
    return pl.pallas_call(
        functools.partial(_proj_kernel, latent=latent, n_sub=n_sub),
        grid=(b, t // tm),
        in_specs=in_specs, out_specs=out_specs, out_shape=out_shape,
        compiler_params=pltpu.CompilerParams(dimension_semantics=("parallel", "parallel"),
                                             vmem_limit_bytes=VMEM_LIMIT),
        name="in_proj_latent" if latent else "in_proj_ctx",
    )(x, mod3, lng, lnb, wina, winr, gqa, wq, gkva, wkn, wvt, *tables)


def _attn_kernel(q_ref, kc_ref, kl_ref, vtc_ref, vtl_ref, sza_ref, hs_ref, som_ref, szm_ref, xc_ref,
                 mhg_ref, skip_ref, o_ref, yb_ref, s_buf, ot_s):
    n_ctx, t = kc_ref.shape[2], kl_ref.shape[2]
    kc = ATTN_KEY_CHUNK

    def scores(h, slot):
        qh = q_ref[0, h]
        sc = _dot_nt(kc_ref[0, h], qh)
        sk = _dot_nt(kl_ref[0, h], qh)
        s_buf[slot, 0:n_ctx, :] = sc
        s_buf[slot, n_ctx:n_ctx + t, :] = sk
        return jnp.maximum(jnp.max(sc, axis=0, keepdims=True), jnp.max(sk, axis=0, keepdims=True))

    def values(h, slot, m):
        chunks = [(vtc_ref, c * kc, c * kc) for c in range(n_ctx // kc)]
        chunks += [(vtl_ref, c * kc, n_ctx + c * kc) for c in range(t // kc)]
        acc = None
        for vref, v0, s0 in chunks:
            p = jnp.exp2(s_buf[slot, s0:s0 + kc, :] - m).astype(BF16)
            d = _dot(vref[0, h, :, v0:v0 + kc], p)
            acc = d if acc is None else acc + d
        rows = pl.ds(pl.multiple_of(h * DV_A, DV_A), DV_A)
        ot_s[rows, :] = acc[0:DV_A, :] / acc[DV_A:DV_A + 1, :]

    def cell_out(pc):
        rows = pl.ds(pl.multiple_of(pc * LANES, LANES), LANES)
        yb_ref[0, rows, :] = _mlstm_out(hs_ref[0, pc], som_ref[0, rows, :], szm_ref[0, rows, :],
                                        xc_ref[0, rows, :], mhg_ref[...], skip_ref[...])

    def body(i, m_even):
        h = 2 * i
        m_odd = scores(h + 1, 1)
        cell_out(i + 1)
        values(h, 0, m_even)
        m_even = scores(h + 2, 0)
        values(h + 1, 1, m_odd)
        return m_even

    assert hs_ref.shape[1] == H_A // 2
    m_first = scores(0, 0)
    cell_out(0)
    m_even = lax.fori_loop(0, H_A // 2 - 1, body, m_first)
    m_odd = scores(H_A - 1, 1)
    values(H_A - 2, 0, m_even)
    values(H_A - 1, 1, m_odd)
    o_ref[0] = (ot_s[...].T * sza_ref[0].astype(F32)).astype(BF16)


def _attn_call(q, kc, kl, vtc, vtl, sza, hs, som, szm, xc, mhg, skip, *, tq):
    b, _, t, _ = q.shape
    n_ctx = kc.shape[2] // b
    head_blk = lambda n, w: pl.BlockSpec((1, H_A, n, w), lambda i, j: (i, 0, 0, 0))
    const = lambda shape: pl.BlockSpec(shape, lambda i, j: (0,) * len(shape))
    row = lambda w: pl.BlockSpec((1, tq, w), lambda i, j: (i, j, 0))
    half = jax.ShapeDtypeStruct((b, t, W_A), BF16)
    return pl.pallas_call(
        _attn_kernel,
        grid=(b, t // tq),
        in_specs=[pl.BlockSpec((1, H_A, tq, SLAB), lambda i, j: (i, 0, j, 0)),
                  pl.BlockSpec((1, H_A, n_ctx, SLAB), lambda i, j: (0, 0, i, 0)), head_blk(t, SLAB),
                  pl.BlockSpec((1, H_A, DV_EXT, n_ctx), lambda i, j: (0, 0, 0, i)), head_blk(DV_EXT, t),
                  row(W_A),
                  pl.BlockSpec((1, tq // LANES, W_B, LANES), lambda i, j: (i, j, 0, 0)),
                  row(W_B), row(W_B), row(W_B), const((1, W_B)), const((1, W_B))],
        out_specs=[row(W_A), row(W_B)],
        out_shape=[half, jax.ShapeDtypeStruct((b, t, W_B), BF16)],
        scratch_shapes=[pltpu.VMEM((2, n_ctx + t, tq), F32),
                        pltpu.VMEM((W_A, tq), F32)],
        compiler_params=pltpu.CompilerParams(dimension_semantics=("parallel", "parallel"),
                                             vmem_limit_bytes=VMEM_LIMIT),
        name="mla_attention",
    )(q, kc, kl, vtc, vtl, sza, hs, som, szm, xc, mhg, skip)


def _gate_fold_kernel(wq_ref, wk_ref, wv_ref, wgate_ref, bgate_ref, wgc_ref, wgm_ref, bg_ref, wg_s):
    wg_s[...] = jnp.zeros(wg_s.shape, F32)
    bg_ref[...] = jnp.zeros(bg_ref.shape, F32)
    for dst, src in enumerate((0, 2, 1, 3)):
        wg_s[:, dst * H_B:(dst + 1) * H_B] = wgate_ref[:, src * H_B:(src + 1) * H_B]
        bg_ref[:, dst * H_B:(dst + 1) * H_B] = bgate_ref[:, src * H_B:(src + 1) * H_B]
    wg = wg_s[...].astype(BF16)
    wgc_ref[...] = (_dot(wq_ref[...], wg[0:W_B, :]) + _dot(wk_ref[...], wg[W_B:2 * W_B, :])).astype(BF16)
    wgm_ref[...] = _dot(wv_ref[...], wg[2 * W_B:3 * W_B, :]).astype(BF16)


def _prep_kernel(c_ref, cctx_ref, wada_ref, bada_ref, wt_ref, wq_ref, wk_ref, wv_ref, wgate_ref, bgate_ref,
                 mod_ref, wina_ref, winr_ref, wgc_ref, wgm_ref, bg_ref, wg_s):
    _ada_kernel(c_ref, cctx_ref, wada_ref, bada_ref, mod_ref)
    _win_kernel(wt_ref, wina_ref, winr_ref)
    _gate_fold_kernel(wq_ref, wk_ref, wv_ref, wgate_ref, bgate_ref, wgc_ref, wgm_ref, bg_ref, wg_s)


def _prep_call(c, c_ctx, w_ada, b_ada, wt, wq, wk, wv, w_gate, b_gate):
    n_in, d = wt.shape
    assert c.shape[0] % 8 == 0 and w_gate.shape == (3 * W_B, 2 * N_HD)
    gate = jax.ShapeDtypeStruct((W_B, LANES), BF16)
    return pl.pallas_call(
        _prep_kernel,
        out_shape=[jax.ShapeDtypeStruct((c.shape[0] + 1, 1, w_ada.shape[1]), F32),
                   jax.ShapeDtypeStruct((d, _S_KVA + LANES), BF16),
                   jax.ShapeDtypeStruct((d, n_in - _S_KR), BF16), gate, gate,
                   jax.ShapeDtypeStruct((1, LANES), F32)],
        scratch_shapes=[pltpu.VMEM((3 * W_B, LANES), F32)],
        compiler_params=pltpu.CompilerParams(vmem_limit_bytes=VMEM_LIMIT),
        name="param_prep",
    )(c, c_ctx, w_ada, b_ada, wt, wq, wk, wv, w_gate, b_gate)


_PAD = 8


def _lane_scans(rows_per_chunk, combine, fill, out):
    npc = CHUNK // LANES
    flat = [x[:, i * LANES:(i + 1) * LANES] for x in rows_per_chunk for i in range(npc)]
    lane = lax.broadcasted_iota(jnp.int32, flat[0].shape, 1)
    pre, suf, sh = list(flat), list(flat), 1
    while sh < LANES:
        pre = [combine(p, jnp.where(lane >= sh, pltpu.roll(p, sh, 1), fill)) for p in pre]
        suf = [combine(s, jnp.where(lane < LANES - sh, pltpu.roll(s, LANES - sh, 1), fill)) for s in suf]
        sh *= 2
        yield
    for c in range(len(rows_per_chunk)):
        p, s = pre[c * npc:(c + 1) * npc], suf[c * npc:(c + 1) * npc]
        tot = [x[:, LANES - 1:LANES] for x in p]
        run = None
        for i in range(npc):
            if run is not None:
                p[i] = combine(p[i], run)
            run = tot[i] if run is None else combine(run, tot[i])
        run = None
        for i in reversed(range(npc)):
            if run is not None:
                s[i] = combine(s[i], run)
            run = tot[i] if run is None else combine(run, tot[i])
        out.append((jnp.concatenate(p, axis=1), jnp.concatenate(s, axis=1)))


def _gate_tables(gts, dests):
    L = CHUNK
    fwd_rows = lax.broadcasted_iota(jnp.int32, (N_HD, L), 0) < H_B
    lis = [gt[0:N_HD, :] for gt in gts]
    sums = []
    yield from _lane_scans([_log_sigmoid(gt[N_HD:2 * N_HD, :]) for gt in gts], jnp.add, 0.0, sums)
    cums = [jnp.where(fwd_rows, ps, ss) for ps, ss in sums]
    rs = [li - cum for li, cum in zip(lis, cums)]
    maxs = []
    yield from _lane_scans(rs, jnp.maximum, -jnp.inf, maxs)
    for (grow_ref, c, gcol_ref, r0), cum, r, (pm, sm) in zip(dests, cums, rs, maxs):
        grow_ref[c] = jnp.concatenate([cum, r, jnp.where(fwd_rows, pm, sm)], axis=0)
        gcol_ref[r0:r0 + L, :] = jnp.concatenate([r, jnp.zeros((LANES - N_HD, L), F32)], axis=0).T


def _fill_pad(pad_s, xm_ref, n):
    zrow = jnp.zeros((_PAD, W_B), F32)
    pad_s[0:_PAD, :] = zrow
    pad_s[_PAD:_PAD + n, :] = xm_ref[...].astype(F32)
    pad_s[_PAD + n:2 * _PAD + n, :] = zrow


def _conv_gate_stage(c, pad_s, xm_ref, cw_ref, cb_ref, wgc_ref, wgm_ref, bg_ref, xc_out, xcb_s):
    L = CHUNK
    rows = slice(c * L, (c + 1) * L)
    r0 = c * L + _PAD
    row_id = lax.broadcasted_iota(jnp.int32, (L, W_B), 0)
    xcur = pad_s[r0:r0 + L, :]
    xprev = jnp.where(row_id == 0, pad_s[r0 - 1:r0, :], pltpu.roll(xcur, 1, 0))
    xnext = jnp.where(row_id == L - 1, pad_s[r0 + L:r0 + L + 1, :], pltpu.roll(xcur, L - 1, 0))
    pre = cb_ref[...] + xprev * cw_ref[0:1, :] + xcur * cw_ref[1:2, :] + xnext * cw_ref[2:3, :]
    xc = _silu(pre)
    xcb = xc.astype(BF16)
    if xc_out is not None:
        xc_out[rows, :] = xcb
    xcb_s[rows, :] = xcb
    g = _dot(xcb, wgc_ref[...]) + _dot(xm_ref[rows, :], wgm_ref[...]) + bg_ref[...]
    return g.T[0:2 * N_HD, :]


def _headwise_stage(c, xm_ref, xcb_s, wq_ref, wk_ref, wv_ref, q_s, k_s, vt_s):
    L = CHUNK
    rows = slice(c * L, (c + 1) * L)
    for hf in range(W_B // MXU_TILE):
        cs = slice(hf * MXU_TILE, (hf + 1) * MXU_TILE)
        xcb = xcb_s[rows, cs]
        if q_s is not None:
            q_s[rows, cs] = _dot(xcb, wq_ref[cs, cs]).astype(BF16)
        k_s[rows, cs] = _dot(xcb, wk_ref[cs, cs]).astype(BF16)
        vt_s[c, cs, :] = _dot(xm_ref[rows, cs], wv_ref[cs, cs]).T.astype(BF16)


_TAB_ROWS = 5 * N_HD


def _chain_tables(growc_s, grow_s, tab_s, nc):
    L = CHUNK
    fwd_rows = lax.broadcasted_iota(jnp.int32, (N_HD, L), 0) < H_B
    gc = growc_s[0]
    steps = [(gc, gc)] + [(grow_s[i], grow_s[nc - 1 - i]) for i in range(nc)]
    parts = []
    for g_f, g_b in steps:
        pick = lambda a: jnp.where(fwd_rows, g_f[a * N_HD:(a + 1) * N_HD, :], g_b[a * N_HD:(a + 1) * N_HD, :])
        cum, r, pm = pick(0), pick(1), pick(2)
        b_end = jnp.where(fwd_rows[:, 0:1], cum[:, L - 1:L], cum[:, 0:1])
        w = r + b_end
        parts.append((cum, pm, b_end, w, jnp.max(w, axis=1, keepdims=True)))
    m0 = jnp.zeros((N_HD, 1), F32)
    for idx, (cum, pm, b_end, w, wmax) in enumerate(parts):
        m_new = jnp.maximum(b_end + m0, wmax)
        mu = jnp.maximum(m0, pm)
        decay = jnp.broadcast_to(jnp.exp(b_end + m0 - m_new), (N_HD, L))
        tab_s[idx] = jnp.concatenate([mu, jnp.exp(m0 - mu), jnp.exp(-(cum + mu)),
                                      jnp.exp(w - m_new), decay], axis=0)
        m0 = m_new


def _tab(tab, a, hd):
    return tab[a * N_HD + hd:a * N_HD + hd + 1, :]


def _state_update(st_ref, hd, vt_ext, k_c, ws_row, decay_hd):
    vw = (vt_ext.astype(F32) * ws_row).astype(BF16)
    st_ref[hd] = decay_hd * st_ref[hd] + _dot(vw, k_c)


def _mlstm_kernel(xm_ref, xmc_ref, cw_ref, cb_ref, wq_ref, wk_ref, wv_ref, wgc_ref, wgm_ref, bg_ref,
                  hs_ref, xc_ref,
                  pad_s, xcb_s, q_s, k_s, vt_s, gcol_s, grow_s,
                  padc_s, xcbc_s, kc_s, vtc_s, gcolc_s, growc_s, tab_s, st_s):
    L = CHUNK
    t = xm_ref.shape[1]
    nc = t // L
    row_i = lax.broadcasted_iota(jnp.int32, (L, L), 0)
    col_i = lax.broadcasted_iota(jnp.int32, (L, L), 1)
    tri = (row_i <= col_i, row_i >= col_i)
    ones_rows = (lax.broadcasted_iota(jnp.int32, (ST_ROWS - DH_B, L), 0) == 0).astype(BF16)
    hsl = lambda hh: slice(hh * DH_B, (hh + 1) * DH_B)
    xm, xmc = xm_ref.at[0], xmc_ref.at[0]
    gate_w = (cw_ref, cb_ref, wgc_ref, wgm_ref, bg_ref)

    _fill_pad(padc_s, xmc, CTX_LEN)
    _fill_pad(pad_s, xm, t)
    gts = [_conv_gate_stage(0, padc_s, xmc, *gate_w, None, xcbc_s)]
    gts += [_conv_gate_stage(c, pad_s, xm, *gate_w, xc_ref.at[0], xcb_s) for c in range(nc)]
    dests = [(growc_s, 0, gcolc_s, 0)] + [(grow_s, c, gcol_s, c * L) for c in range(nc)]
    tables = _gate_tables(gts, dests)
    _headwise_stage(0, xmc, xcbc_s, wq_ref, wk_ref, wv_ref, None, kc_s, vtc_s)
    for c in range(nc):
        next(tables, None)
        next(tables, None)
        _headwise_stage(c, xm, xcb_s, wq_ref, wk_ref, wv_ref, q_s, k_s, vt_s)
    for _ in tables:
        pass

    _chain_tables(growc_s, grow_s, tab_s, nc)
    st_s[...] = jnp.zeros(st_s.shape, F32)
    tab = tab_s[0]
    for hd in range(N_HD):
        hh = hd % H_B
        vt_ext = jnp.concatenate([vtc_s[0, hsl(hh), :], ones_rows], axis=0)
        _state_update(st_s, hd, vt_ext, kc_s[:, hsl(hh)], _tab(tab, 3, hd), _tab(tab, 4, hd)[:, 0:DH_B])

    def body(i, carry, *, accumulate):
        ci = (i, nc - 1 - i)
        rows = tuple(pl.ds(pl.multiple_of(c * L, L), L) for c in ci)
        tab = tab_s[i + 1]
        gcols = tuple(gcol_s[r, :] for r in rows)
        live = {}

        def stage_scores(hd):
            d, hh = hd // H_B, hd % H_B
            k_c = k_s[rows[d], hsl(hh)]
            vt_ext = jnp.concatenate([vt_s[ci[d], hsl(hh), :], ones_rows], axis=0)
            lhs = jnp.concatenate([k_c, st_s[hd].astype(BF16)], axis=0)
            live[hd] = (k_c, vt_ext, _dot_nt(lhs, q_s[rows[d], hsl(hh)]))

        def stage_gate(hd):
            d = hd // H_B
            k_c, vt_ext, res = live[hd]
            rcol = gcols[d][:, hd:hd + 1]
            e = jnp.exp(jnp.where(tri[d], rcol - _tab(tab, 0, hd), -jnp.inf))
            p = (res[0:L, :] * e).astype(BF16)
            _state_update(st_s, hd, vt_ext, k_c, _tab(tab, 3, hd), _tab(tab, 4, hd)[:, 0:DH_B])
            live[hd] = (vt_ext, res[L:L + ST_ROWS, :], p)

        def stage_out(hd):
            d, hh = hd // H_B, hd % H_B
            vt_ext, inter, p = live.pop(hd)
            tot = _tab(tab, 1, hd) * inter + _dot(vt_ext, p)
            den = tot[DH_B:DH_B + 1, :]
            h_t = tot[0:DH_B, :] / jnp.maximum(jnp.abs(den), _tab(tab, 2, hd))
            for pc in range(L // LANES):
                piece = h_t[:, pc * LANES:(pc + 1) * LANES]
                if accumulate:
                    hs_ref[0, ci[d] * (L // LANES) + pc, hsl(hh), :] += piece
                else:
                    hs_ref[0, ci[d] * (L // LANES) + pc, hsl(hh), :] = piece

        stages = (stage_scores, stage_gate, stage_out)
        for step in range(N_HD + len(stages) - 1):
            for si in range(len(stages)):
                if 0 <= step - si < N_HD:
                    stages[si](step - si)
        return carry

    lax.fori_loop(0, nc // 2, functools.partial(body, accumulate=False), 0)
    lax.fori_loop(nc // 2, nc, functools.partial(body, accumulate=True), 0)


def _mlstm_out(hs_t, som, szm, xc, mhg, skip):
    hsum = hs_t.T * som.astype(F32)
    parts = [_layer_norm(hsum[:, hh * DH_B:(hh + 1) * DH_B]) for hh in range(H_B)]
    hb = jnp.concatenate(parts, axis=1) * mhg + skip * xc.astype(F32)
    return (hb * szm.astype(F32)).astype(BF16)


def _mlstm_call(xm, xmc, conv_w, conv_b, wq, wk, wv, wgc, wgm, bg):
    b, t, _ = xm.shape
    nc = t // CHUNK
    assert nc % 2 == 0
    seq = lambda n: pl.BlockSpec((1, n, W_B), lambda i: (i, 0, 0))
    const = lambda shape: pl.BlockSpec(shape, lambda i: (0,) * len(shape))
    scratch = [
        pltpu.VMEM((t + 2 * _PAD, W_B), F32),
        pltpu.VMEM((t, W_B), BF16),
        pltpu.VMEM((t, W_B), BF16),
        pltpu.VMEM((t, W_B), BF16),
        pltpu.VMEM((nc, W_B, CHUNK), BF16),
        pltpu.VMEM((t, LANES), F32),
        pltpu.VMEM((nc, 3 * N_HD, CHUNK), F32),
        pltpu.VMEM((CTX_LEN + 2 * _PAD, W_B), F32),
        pltpu.VMEM((CTX_LEN, W_B), BF16),
        pltpu.VMEM((CTX_LEN, W_B), BF16),
        pltpu.VMEM((1, W_B, CHUNK), BF16),
        pltpu.VMEM((CTX_LEN, LANES), F32),
        pltpu.VMEM((1, 3 * N_HD, CHUNK), F32),
        pltpu.VMEM((nc + 1, _TAB_ROWS, CHUNK), F32),
        pltpu.VMEM((N_HD, ST_ROWS, DH_B), F32),
    ]
    return pl.pallas_call(
        _mlstm_kernel,
        grid=(b,),
        in_specs=[seq(t), seq(CTX_LEN), const((3, W_B)), const((1, W_B)),
                  const((W_B, W_B)), const((W_B, W_B)), const((W_B, W_B)),
                  const((W_B, LANES)), const((W_B, LANES)), const((1, LANES))],
        out_specs=[pl.BlockSpec((1, t // LANES, W_B, LANES), lambda i: (i, 0, 0, 0)), seq(t)],
        out_shape=[jax.ShapeDtypeStruct((b, t // LANES, W_B, LANES), F32),
                   jax.ShapeDtypeStruct((b, t, W_B), BF16)],
        scratch_shapes=scratch,
        compiler_params=pltpu.CompilerParams(dimension_semantics=("arbitrary",),
                                             vmem_limit_bytes=VMEM_LIMIT),
        name="mlstm_bidir",
    )(xm, xmc, conv_w, conv_b, wq, wk, wv, wgc, wgm, bg)


def _out_kernel(h_ref, mod_ref, ya_ref, yb_ref, wo_ref, g_ref, b_ref, o_ref, *, n_sub):
    sub = h_ref.shape[1] // n_sub
    gate = mod_ref[0][:, 2 * D_MODEL:3 * D_MODEL]

    def mix_stage(s):
        rows = slice(s * sub, (s + 1) * sub)
        return (_dot(ya_ref[0, rows, :], wo_ref[0:W_A, :])
                + _dot(yb_ref[0, rows, :], wo_ref[W_A:W_A + W_B, :]))

    def norm_stage(s, y):
        rows = slice(s * sub, (s + 1) * sub)
        o_ref[0, rows, :] = _layer_norm(ALPHA * h_ref[0, rows, :] + gate * y) * g_ref[...] + b_ref[...]

    y = mix_stage(0)
    for s in range(n_sub):
        y_next = mix_stage(s + 1) if s + 1 < n_sub else None
        norm_stage(s, y)
        y = y_next


def _out_call(h, mod3, ya, yb, wo, g, bb, *, tm, n_sub):
    b, t, _ = h.shape
    const = lambda shape: pl.BlockSpec(shape, lambda i, j: (0,) * len(shape))
    row = lambda w: pl.BlockSpec((1, tm, w), lambda i, j: (i, j, 0))
    return pl.pallas_call(
        functools.partial(_out_kernel, n_sub=n_sub),
        grid=(b, t // tm),
        in_specs=[row(D_MODEL), pl.BlockSpec((1, 1, 3 * D_MODEL), lambda i, j: (i, 0, 0)),
                  row(W_A), row(W_B),
                  const((W_A + W_B, D_MODEL)), const((1, D_MODEL)), const((1, D_MODEL))],
        out_specs=row(D_MODEL),
        out_shape=jax.ShapeDtypeStruct((b, t, D_MODEL), F32),
        compiler_params=pltpu.CompilerParams(dimension_semantics=("parallel", "parallel"),
                                             vmem_limit_bytes=VMEM_LIMIT),
        name="out_proj_ln",
    )(h, mod3, ya, yb, wo, g, bb)


def _rot_partner(w):
    w4 = w.reshape(w.shape[:-1] + (2, 2, ROPE_FREQS))
    return jnp.stack([-w4[..., 1, :], w4[..., 0, :]], axis=-2).reshape(w.shape)


def _rope_tables(seq, scale_keep, scale_rope):
    n_rows = seq // GRID_W
    rowp = np.repeat(np.arange(n_rows, dtype=np.float32), GRID_W)
    colp = np.tile(np.arange(GRID_W, dtype=np.float32), n_rows)
    inv = (np.float32(ROPE_BASE) ** (-np.arange(ROPE_FREQS, dtype=np.float32) / np.float32(ROPE_FREQS)))
    ang = np.stack([rowp[:, None] * inv, colp[:, None] * inv], axis=1).astype(np.float32)
    cos = np.broadcast_to(np.cos(ang)[:, :, None, :], (seq, 2, 2, ROPE_FREQS)).reshape(seq, DR_A)
    sin = np.broadcast_to(np.sin(ang)[:, :, None, :], (seq, 2, 2, ROPE_FREQS)).reshape(seq, DR_A)
    z32 = np.zeros((seq, DR_A), np.float32)
    t1 = np.concatenate([np.full((seq, DN_A), scale_keep, np.float32), z32, cos * scale_rope], axis=1)
    t2 = np.concatenate([np.zeros((seq, DN_A), np.float32), z32, sin * scale_rope], axis=1)
    return jnp.asarray(t1, F32), jnp.asarray(t2, F32)


def _block_diag(w):
    nb, bs, _ = w.shape
    n = nb * bs
    rows = w.reshape(n, bs)
    ri = lax.broadcasted_iota(jnp.int32, (n, n), 0)
    ci = lax.broadcasted_iota(jnp.int32, (n, n), 1)
    out = jnp.zeros((n, n), w.dtype)
    for o in range(bs):
        out = out + jnp.where((ri // bs == ci // bs) & (ci % bs == o), rows[:, o:o + 1], 0.0)
    return out


def kernel(x, c, ctx, c_ctx, ln_in_g, ln_in_b, w_ada, b_ada, w_in, g_qa, w_qb, g_kva, w_kvb, conv_w, conv_b, w_mq, w_mk, w_mv, w_gate, b_gate, mh_g, skip, w_out, ln_g, ln_b):
    b, t, _ = x.shape
    l = 0
    r2 = lambda v: v.reshape(1, -1)

    wq3 =w_qb[l].reshape(Q_LORA, H_A, DN_A + DR_A)
    wq_r = wq3[..., DN_A:]
    wq = jnp.concatenate([wq3[..., :DN_A], _rot_partner(wq_r), wq_r], axis=-1)
    wq = wq.reshape(Q_LORA, H_A * SLAB).astype(BF16)
    wkv3 = w_kvb[l].reshape(KV_LORA, H_A, DN_A + DV_A)
    wkn = jnp.concatenate([wkv3[..., :DN_A], jnp.zeros((KV_LORA, H_A, SLAB - DN_A), F32)], axis=-1)
    wkn = wkn.reshape(KV_LORA, H_A * SLAB).astype(BF16)
    wvt = wkv3[..., DN_A:].reshape(KV_LORA, W_A).T.astype(BF16)
    wmq = _block_diag(w_mq[l])
    wmk = _block_diag(w_mk[l])
    wmv = _block_diag(w_mv[l])
    wmk_s = (wmk * (DH_B ** -0.5)).astype(BF16)
    wo = w_out[l].astype(BF16)

    sm_scale = (DN_A + DR_A) ** -0.5 * LOG2_E
    t1q, t2q = _rope_tables(t, sm_scale, sm_scale)
    t1k, t2k = _rope_tables(t, 0.0, 1.0)
    n_ctx = ctx.shape[1]

    mod3, wina, winr, wgc, wgm, bg = _prep_call(c, r2(c_ctx), w_ada[l], r2(b_ada[l]), jnp.swapaxes(w_in[l], 0, 1),
                                                wmq.astype(BF16), wmk.astype(BF16), wmv.astype(BF16),
                                                w_gate[l], r2(b_gate[l]))

    lng, lnb = r2(ln_in_g), r2(ln_in_b)
    shared = (wina, winr, r2(g_qa[l]), wq, r2(g_kva[l]), wkn, wvt)
    q, kl, vtl, xm, sza, som, szm, h = _proj_call(x, mod3, None, lng, lnb, *shared, t1q, t2q, t1k, t2k,
                                                  tm=1024, n_sub=2, latent=True)
    kc, vtc, xmc = _proj_call(ctx.reshape(1, b * n_ctx, D_MODEL), mod3, b, lng, lnb, *shared,
                              tm=min(1024, b * n_ctx), n_sub=2, latent=False)
    xmc = xmc.reshape(b, n_ctx, W_B)

    hs, xc = _mlstm_call(xm, xmc, conv_w[l], r2(conv_b[l]), wmq.astype(BF16), wmk_s, wmv.astype(BF16),
                         wgc, wgm, bg)
    ya, yb = _attn_call(q, kc, kl, vtc, vtl, sza, hs, som, szm, xc, r2(mh_g[l]), r2(skip[l]), tq=512)
    return _out_call(h, mod3, ya, yb, wo, r2(ln_g[l]), r2(ln_b[l]), tm=1024, n_sub=4)
```

```python
import functools

import numpy as np
import jax
import jax.numpy as jnp
from jax import lax
from jax.experimental import pallas as pl
from jax.experimental.pallas import tpu as pltpu

F32 = jnp.float32
BF16 = jnp.bfloat16

D_MODEL = 1024
CTX_LEN = 256
GRID_W = 64
H_A, DN_A, DR_A, DV_A = 8, 64, 32, 64
W_A = H_A * DV_A
Q_LORA, KV_LORA = 256, 128
ROPE_FREQS = DR_A // 4
ROPE_BASE = 10000.0
H_B, DH_B = 4, 128
W_B = H_B * DH_B
QKV_BS = 4
DEPTH = 1
ALPHA = (2.0 * DEPTH) ** 0.25
LN_EPS = 1e-5
RMS_EPS = 1e-6
LOG2_E = 1.4426950408889634

LANES = 128
MXU_TILE = 256
SLAB = LANES
DV_EXT = DV_A + 16
ATTN_KEY_CHUNK = MXU_TILE
CHUNK = 256
N_HD = 2 * H_B
ST_ROWS = DH_B + 16
VMEM_LIMIT = 56 * 1024 * 1024

_NT = (((1,), (1,)), ((), ()))


def _dot(a, b):
    return jnp.dot(a, b, preferred_element_type=F32)


def _dot_nt(a, b):
    return lax.dot_general(a, b, _NT, preferred_element_type=F32)


def _layer_norm(x):
    mu = jnp.mean(x, axis=-1, keepdims=True)
    xc = x - mu
    var = jnp.mean(xc * xc, axis=-1, keepdims=True)
    return xc * lax.rsqrt(var + LN_EPS)


def _rms_norm(x, g):
    return (x * lax.rsqrt(jnp.mean(x * x, axis=-1, keepdims=True) + RMS_EPS)) * g


def _silu(x):
    return x * jax.nn.sigmoid(x)


def _log_sigmoid(x):
    return jnp.minimum(x, 0.0) - jnp.log1p(jnp.exp(-jnp.abs(x)))


def _ada_kernel(c_ref, cctx_ref, w_ref, b_ref, o_ref):
    first = lax.broadcasted_iota(jnp.int32, (8, D_MODEL), 0) == 0
    cc = jnp.concatenate([c_ref[...], jnp.where(first, cctx_ref[...], 0.0)], axis=0)
    a = _silu(cc).astype(BF16)
    mod = _dot(a, w_ref[...].astype(BF16)) + b_ref[...]
    for r in range(o_ref.shape[0]):
        o_ref[r] = mod[r:r + 1, :]


_S_KVA, _S_KR = Q_LORA + KV_LORA, Q_LORA + KV_LORA + DR_A


def _win_kernel(wt_ref, wina_ref, winr_ref):
    n_r = winr_ref.shape[1]
    for j in range(_S_KVA // LANES):
        wina_ref[:, j * LANES:(j + 1) * LANES] = wt_ref[j * LANES:(j + 1) * LANES, :].T.astype(BF16)
    kr = wt_ref[_S_KVA:_S_KR, :]
    a0, b0, a1, b1 = (kr[i * ROPE_FREQS:(i + 1) * ROPE_FREQS, :] for i in range(4))
    blk = jnp.concatenate([jnp.zeros((DN_A, D_MODEL), F32), -b0, a0, -b1, a1, kr], axis=0)
    wina_ref[:, _S_KVA:_S_KVA + LANES] = blk.T.astype(BF16)
    for j in range(n_r // MXU_TILE):
        rows = slice(_S_KR + j * MXU_TILE, _S_KR + (j + 1) * MXU_TILE)
        winr_ref[:, j * MXU_TILE:(j + 1) * MXU_TILE] = wt_ref[rows, :].T.astype(BF16)


def _rope_slab(s, t1, t2):
    return s * t1 + pltpu.roll(s, 32, 1) * t2


def _proj_kernel(x_ref, mod_ref, lng_ref, lnb_ref, wina_ref, winr_ref, gqa_ref, wq_ref, gkva_ref,
                 wkn_ref, wvt_ref, *refs, latent, n_sub):
    if latent:
        t1q_ref, t2q_ref, t1k_ref, t2k_ref, q_ref, k_ref, vt_ref, xm_ref, sza_ref, som_ref, szm_ref, h_ref = refs
    else:
        k_ref, vt_ref, xm_ref = refs
    sub = x_ref.shape[1] // n_sub
    mod = mod_ref[0]
    shift, scale1 = mod[:, 0:D_MODEL], 1.0 + mod[:, D_MODEL:2 * D_MODEL]
    ones_rows = (lax.broadcasted_iota(jnp.int32, (DV_EXT - DV_A, sub), 0) == 0).astype(BF16)

    def norm_stage(s):
        rows = slice(s * sub, (s + 1) * sub)
        h = _layer_norm(x_ref[0, rows, :]) * lng_ref[...] + lnb_ref[...]
        if latent:
            h_ref[0, rows, :] = h
        return (h * scale1 + shift).astype(BF16)

    def proj_stage(s, u):
        rows = slice(s * sub, (s + 1) * sub)
        p0 = _dot(u, wina_ref[...])
        d_xm = _dot(u, winr_ref[:, 512:1024])
        kvn = _rms_norm(p0[:, 256:384], gkva_ref[...]).astype(BF16)
        kn = _dot(kvn, wkn_ref[...])
        d_za = _dot(u, winr_ref[:, 0:512]) if latent else None
        if latent:
            krr = _rope_slab(p0[:, 384:512], t1k_ref[rows, :], t2k_ref[rows, :])
        else:
            rope_lanes = lax.broadcasted_iota(jnp.int32, (sub, SLAB), 1) >= SLAB - DR_A
            krr = jnp.where(rope_lanes, p0[:, 384:512], 0.0)
        for hh in range(H_A):
            k_ref[0, hh, rows, :] = (kn[:, hh * SLAB:(hh + 1) * SLAB] + krr).astype(BF16)
        vt = _dot_nt(wvt_ref[...], kvn).astype(BF16)
        for hh in range(H_A):
            vt_ref[0, hh, 0:DV_A, rows] = vt[hh * DV_A:(hh + 1) * DV_A, :]
            vt_ref[0, hh, DV_A:DV_EXT, rows] = ones_rows
        xm_ref[0, rows, :] = d_xm.astype(BF16)
        if latent:
            d_om = _dot(u, winr_ref[:, 1024:1536])
            qn = _rms_norm(p0[:, 0:256], gqa_ref[...]).astype(BF16)
            qs = _dot(qn, wq_ref[...])
            sza_ref[0, rows, :] = _silu(d_za).astype(BF16)
            d_zm = _dot(u, winr_ref[:, 1536:2048])
            t1q, t2q = t1q_ref[rows, :], t2q_ref[rows, :]
            for hh in range(H_A):
                q_ref[0, hh, rows, :] = _rope_slab(qs[:, hh * SLAB:(hh + 1) * SLAB], t1q, t2q).astype(BF16)
            som_ref[0, rows, :] = jax.nn.sigmoid(d_om).astype(BF16)
            szm_ref[0, rows, :] = _silu(d_zm).astype(BF16)

    u = norm_stage(0)
    for s in range(n_sub):
        u_next = norm_stage(s + 1) if s + 1 < n_sub else None
        proj_stage(s, u)
        u = u_next


def _proj_call(x, mod3, mod_row, lng, lnb, wina, winr, gqa, wq, gkva, wkn, wvt, *tables, tm, n_sub, latent):
    b, t, _ = x.shape
    const = lambda shape: pl.BlockSpec(shape, lambda i, j: (0,) * len(shape))
    tab = pl.BlockSpec((tm, SLAB), lambda i, j: (j, 0))
    assert len(tables) == (4 if latent else 0)
    row = lambda w: pl.BlockSpec((1, tm, w), lambda i, j: (i, j, 0))
    if mod_row is None:
        mod_spec = pl.BlockSpec((1, 1, 3 * D_MODEL), lambda i, j: (i, 0, 0))
    else:
        mod_spec = pl.BlockSpec((1, 1, 3 * D_MODEL), lambda i, j: (mod_row, 0, 0))
    in_specs = [row(D_MODEL), mod_spec, const((1, D_MODEL)), const((1, D_MODEL)),
                const(wina.shape), const(winr.shape), const((1, Q_LORA)), const((Q_LORA, H_A * SLAB)),
                const((1, KV_LORA)), const((KV_LORA, H_A * SLAB)), const((W_A, KV_LORA))]
    in_specs += [tab] * len(tables)
    k_spec = pl.BlockSpec((1, H_A, tm, SLAB), lambda i, j: (i, 0, j, 0))
    vt_spec = pl.BlockSpec((1, H_A, DV_EXT, tm), lambda i, j: (i, 0, 0, j))
    k_shape = jax.ShapeDtypeStruct((b, H_A, t, SLAB), BF16)
    vt_shape = jax.ShapeDtypeStruct((b, H_A, DV_EXT, t), BF16)
    half = jax.ShapeDtypeStruct((b, t, W_B), BF16)
    if latent:
        out_specs = [k_spec, k_spec, vt_spec, row(W_B), row(W_A), row(W_B), row(W_B), row(D_MODEL)]
        out_shape = [k_shape, k_shape, vt_shape, half, half, half, half,
                     jax.ShapeDtypeStruct((b, t, D_MODEL), F32)]
    else:
        out_specs = [k_spec, vt_spec, row(W_B)]
        out_shape = [k_shape, vt_shape, half]
    return pl.pallas_call(
        functools.partial(_proj_kernel, latent=latent, n_sub=n_sub),
        grid=(b, t // tm),
        in_specs=in_specs, out_specs=out_specs, out_shape=out_shape,
        compiler_params=pltpu.CompilerParams(dimension_semantics=("parallel", "parallel"),
                                             vmem_limit_bytes=VMEM_LIMIT),
        name="in_proj_latent" if latent else "in_proj_ctx",
    )(x, mod3, lng, lnb, wina, winr, gqa, wq, gkva, wkn, wvt, *tables)


def _attn_kernel(q_ref, kc_ref, kl_ref, vtc_ref, vtl_ref, sza_ref, hs_ref, som_ref, szm_ref, xc_ref,
                 mhg_ref, skip_ref, o_ref, yb_ref, s_buf, ot_s):
    n_ctx, t = kc_ref.shape[2], kl_ref.shape[2]
    kc = ATTN_KEY_CHUNK

    def scores(h, slot):
        qh = q_ref[0, h]
        sc = _dot_nt(kc_ref[0, h], qh)
        sk = _dot_nt(kl_ref[0, h], qh)
        s_buf[slot, 0:n_ctx, :] = sc
        s_buf[slot, n_ctx:n_ctx + t, :] = sk
        return jnp.maximum(jnp.max(sc, axis=0, keepdims=True), jnp.max(sk, axis=0, keepdims=True))

    def values(h, slot, m):
        chunks = [(vtc_ref, c * kc, c * kc) for c in range(n_ctx // kc)]
        chunks += [(vtl_ref, c * kc, n_ctx + c * kc) for c in range(t // kc)]
        acc = None
        for vref, v0, s0 in chunks:
            p = jnp.exp2(s_buf[slot, s0:s0 + kc, :] - m).astype(BF16)
            d = _dot(vref[0, h, :, v0:v0 + kc], p)
            acc = d if acc is None else acc + d
        rows = pl.ds(pl.multiple_of(h * DV_A, DV_A), DV_A)
        ot_s[rows, :] = acc[0:DV_A, :] / acc[DV_A:DV_A + 1, :]

    def cell_out(pc):
        rows = pl.ds(pl.multiple_of(pc * LANES, LANES), LANES)
        yb_ref[0, rows, :] = _mlstm_out(hs_ref[0, pc], som_ref[0, rows, :], szm_ref[0, rows, :],
                                        xc_ref[0, rows, :], mhg_ref[...], skip_ref[...])

    def body(i, m_even):
        h = 2 * i
        m_odd = scores(h + 1, 1)
        cell_out(i + 1)
        values(h, 0, m_even)
        m_even = scores(h + 2, 0)
        values(h + 1, 1, m_odd)
        return m_even

    assert hs_ref.shape[1] == H_A // 2
    m_first = scores(0, 0)
    cell_out(0)
    m_even = lax.fori_loop(0, H_A // 2 - 1, body, m_first)
    m_odd = scores(H_A - 1, 1)
    values(H_A - 2, 0, m_even)
    values(H_A - 1, 1, m_odd)
    o_ref[0] = (ot_s[...].T * sza_ref[0].astype(F32)).astype(BF16)


def _attn_call(q, kc, kl, vtc, vtl, sza, hs, som, szm, xc, mhg, skip, *, tq):
    b, _, t, _ = q.shape
    n_ctx = kc.shape[2] // b
    head_blk = lambda n, w: pl.BlockSpec((1, H_A, n, w), lambda i, j: (i, 0, 0, 0))
    const = lambda shape: pl.BlockSpec(shape, lambda i, j: (0,) * len(shape))
    row = lambda w: pl.BlockSpec((1, tq, w), lambda i, j: (i, j, 0))
    half = jax.ShapeDtypeStruct((b, t, W_A), BF16)
    return pl.pallas_call(
        _attn_kernel,
        grid=(b, t // tq),
        in_specs=[pl.BlockSpec((1, H_A, tq, SLAB), lambda i, j: (i, 0, j, 0)),
                  pl.BlockSpec((1, H_A, n_ctx, SLAB), lambda i, j: (0, 0, i, 0)), head_blk(t, SLAB),
                  pl.BlockSpec((1, H_A, DV_EXT, n_ctx), lambda i, j: (0, 0, 0, i)), head_blk(DV_EXT, t),
                  row(W_A),
                  pl.BlockSpec((1, tq // LANES, W_B, LANES), lambda i, j: (i, j, 0, 0)),
                  row(W_B), row(W_B), row(W_B), const((1, W_B)), const((1, W_B))],
        out_specs=[row(W_A), row(W_B)],
        out_shape=[half, jax.ShapeDtypeStruct((b, t, W_B), BF16)],
        scratch_shapes=[pltpu.VMEM((2, n_ctx + t, tq), F32),
                        pltpu.VMEM((W_A, tq), F32)],
        compiler_params=pltpu.CompilerParams(dimension_semantics=("parallel", "parallel"),
                                             vmem_limit_bytes=VMEM_LIMIT),
        name="mla_attention",
    )(q, kc, kl, vtc, vtl, sza, hs, som, szm, xc, mhg, skip)


def _onehot(cond):
    return cond.astype(F32).astype(BF16)


def _block_diag(a):
    bs, nb = QKV_BS, W_B // QKV_BS
    lb, sq = bs.bit_length() - 1, bs * bs
    assert bs == 1 << lb
    iota = lambda shape, d: lax.broadcasted_iota(jnp.int32, shape, d)
    spread = _onehot(iota((nb, W_B), 1) >> lb == iota((nb, W_B), 0))
    b = _dot(a, spread)
    col_o = iota((sq, W_B), 1) & (bs - 1)
    bm = jnp.concatenate([jnp.where(col_o == o, b, 0.0) for o in range(bs)], axis=0).astype(BF16)
    j = iota((W_B, bs * sq), 1)
    pick = _onehot((j & (sq - 1)) == ((iota((W_B, bs * sq), 0) & (bs - 1)) << lb) + (j >> (2 * lb)))
    same_block = iota((W_B, W_B), 0) >> lb == iota((W_B, W_B), 1) >> lb
    return jnp.where(same_block, _dot(pick, bm), 0.0).astype(BF16)


def _mlstm_weight_kernel(wm_ref, wgt_ref, bgate_ref, wmq_ref, wmk_ref, wmv_ref, wgc_ref, wgm_ref, bg_ref):
    sq = QKV_BS * QKV_BS
    a_q, a_k, a_v = (wm_ref[i * sq:(i + 1) * sq, :] for i in range(3))
    wq, wk, wv = (_block_diag(a.astype(BF16)) for a in (a_q, a_k, a_v))
    wmq_ref[...] = wq
    wmk_ref[...] = _block_diag((a_k * (DH_B ** -0.5)).astype(BF16))
    wmv_ref[...] = wv
    dst, src = (lax.broadcasted_iota(jnp.int32, (LANES, 2 * N_HD), d) for d in (0, 1))
    grp = dst >> (H_B.bit_length() - 1)
    src_of_dst = (dst & (H_B - 1)) + H_B * jnp.where(grp == 1, 2, jnp.where(grp == 2, 1, grp))
    perm = _onehot((dst < 2 * N_HD) & (src == src_of_dst))
    wg_t = _dot(perm, wgt_ref[...].astype(BF16)).astype(BF16)
    wgc_ref[...] = (_dot_nt(wq, wg_t[:, 0:W_B]) + _dot_nt(wk, wg_t[:, W_B:2 * W_B])).astype(BF16)
    wgm_ref[...] = _dot_nt(wv, wg_t[:, 2 * W_B:3 * W_B]).astype(BF16)
    bg_ref[...] = jnp.zeros(bg_ref.shape, F32)
    for d, s in enumerate((0, 2, 1, 3)):
        bg_ref[:, d * H_B:(d + 1) * H_B] = bgate_ref[:, s * H_B:(s + 1) * H_B]


def _kvb_kernel(wkvb_ref, wkn_ref, wvt_ref):
    w = wkvb_ref[...]
    assert DN_A + DV_A == SLAB
    lane = lax.broadcasted_iota(jnp.int32, w.shape, 1)
    wkn_ref[...] = jnp.where((lane & (SLAB - 1)) < DN_A, w, 0.0).astype(BF16)
    wt = w.T
    for hh in range(H_A):
        wvt_ref[hh * DV_A:(hh + 1) * DV_A, :] = wt[hh * SLAB + DN_A:(hh + 1) * SLAB, :].astype(BF16)


def _prep_kernel(c_ref, cctx_ref, wada_ref, bada_ref, wt_ref, wm_ref, wgt_ref, bgate_ref, wkvb_ref,
                 mod_ref, wina_ref, winr_ref, wmq_ref, wmk_ref, wmv_ref, wgc_ref, wgm_ref, bg_ref,
                 wkn_ref, wvt_ref):
    _ada_kernel(c_ref, cctx_ref, wada_ref, bada_ref, mod_ref)
    _win_kernel(wt_ref, wina_ref, winr_ref)
    _mlstm_weight_kernel(wm_ref, wgt_ref, bgate_ref, wmq_ref, wmk_ref, wmv_ref, wgc_ref, wgm_ref, bg_ref)
    _kvb_kernel(wkvb_ref, wkn_ref, wvt_ref)


def _prep_call(c, c_ctx, w_ada, b_ada, wt, wm, wgt, b_gate, w_kvb):
    n_in, d = wt.shape
    assert c.shape[0] % 8 == 0 and wgt.shape == (2 * N_HD, 3 * W_B)
    assert wm.shape == (3 * QKV_BS * QKV_BS, W_B // QKV_BS)
    bf = lambda *shape: jax.ShapeDtypeStruct(shape, BF16)
    return pl.pallas_call(
        _prep_kernel,
        out_shape=[jax.ShapeDtypeStruct((c.shape[0] + 1, 1, w_ada.shape[1]), F32),
                   bf(d, _S_KVA + LANES), bf(d, n_in - _S_KR),
                   bf(W_B, W_B), bf(W_B, W_B), bf(W_B, W_B), bf(W_B, LANES), bf(W_B, LANES),
                   jax.ShapeDtypeStruct((1, LANES), F32),
                   bf(KV_LORA, H_A * SLAB), bf(W_A, KV_LORA)],
        compiler_params=pltpu.CompilerParams(vmem_limit_bytes=VMEM_LIMIT),
        name="param_prep",
    )(c, c_ctx, w_ada, b_ada, wt, wm, wgt, b_gate, w_kvb)


_PAD = 8


def _lane_scans(rows_per_chunk, combine, fill, out):
    npc = CHUNK // LANES
    flat = [x[:, i * LANES:(i + 1) * LANES] for x in rows_per_chunk for i in range(npc)]
    lane = lax.broadcasted_iota(jnp.int32, flat[0].shape, 1)
    pre, suf, sh = list(flat), list(flat), 1
    while sh < LANES:
        pre = [combine(p, jnp.where(lane >= sh, pltpu.roll(p, sh, 1), fill)) for p in pre]
        suf = [combine(s, jnp.where(lane < LANES - sh, pltpu.roll(s, LANES - sh, 1), fill)) for s in suf]
        sh *= 2
        yield
    for c in range(len(rows_per_chunk)):
        p, s = pre[c * npc:(c + 1) * npc], suf[c * npc:(c + 1) * npc]
        tot = [x[:, LANES - 1:LANES] for x in p]
        run = None
        for i in range(npc):
            if run is not None:
                p[i] = combine(p[i], run)
            run = tot[i] if run is None else combine(run, tot[i])
        run = None
        for i in reversed(range(npc)):
            if run is not None:
                s[i] = combine(s[i], run)
            run = tot[i] if run is None else combine(run, tot[i])
        out.append((jnp.concatenate(p, axis=1), jnp.concatenate(s, axis=1)))


def _gate_tables(gts, dests):
    L = CHUNK
    fwd_rows = lax.broadcasted_iota(jnp.int32, (N_HD, L), 0) < H_B
    lis = [gt[0:N_HD, :] for gt in gts]
    sums = []
    yield from _lane_scans([_log_sigmoid(gt[N_HD:2 * N_HD, :]) for gt in gts], jnp.add, 0.0, sums)
    cums = [jnp.where(fwd_rows, ps, ss) for ps, ss in sums]
    rs = [li - cum for li, cum in zip(lis, cums)]
    maxs = []
    yield from _lane_scans(rs, jnp.maximum, -jnp.inf, maxs)
    for (grow_ref, c, gcol_ref, r0), cum, r, (pm, sm) in zip(dests, cums, rs, maxs):
        grow_ref[c] = jnp.concatenate([cum, r, jnp.where(fwd_rows, pm, sm)], axis=0)
        gcol_ref[r0:r0 + L, :] = jnp.concatenate([r, jnp.zeros((LANES - N_HD, L), F32)], axis=0).T


def _fill_pad(pad_s, xm_ref, n):
    zrow = jnp.zeros((_PAD, W_B), F32)
    pad_s[0:_PAD, :] = zrow
    pad_s[_PAD:_PAD + n, :] = xm_ref[...].astype(F32)
    pad_s[_PAD + n:2 * _PAD + n, :] = zrow


def _conv_gate_stage(c, pad_s, xm_ref, cw_ref, cb_ref, wgc_ref, wgm_ref, bg_ref, xc_out, xcb_s):
    L = CHUNK
    rows = slice(c * L, (c + 1) * L)
    r0 = c * L + _PAD
    row_id = lax.broadcasted_iota(jnp.int32, (L, W_B), 0)
    xcur = pad_s[r0:r0 + L, :]
    xprev = jnp.where(row_id == 0, pad_s[r0 - 1:r0, :], pltpu.roll(xcur, 1, 0))
    xnext = jnp.where(row_id == L - 1, pad_s[r0 + L:r0 + L + 1, :], pltpu.roll(xcur, L - 1, 0))
    pre = cb_ref[...] + xprev * cw_ref[0:1, :] + xcur * cw_ref[1:2, :] + xnext * cw_ref[2:3, :]
    xc = _silu(pre)
    xcb = xc.astype(BF16)
    if xc_out is not None:
        xc_out[rows, :] = xcb
    xcb_s[rows, :] = xcb
    g = _dot(xcb, wgc_ref[...]) + _dot(xm_ref[rows, :], wgm_ref[...]) + bg_ref[...]
    return g.T[0:2 * N_HD, :]


def _headwise_stage(c, xm_ref, xcb_s, wq_ref, wk_ref, wv_ref, q_s, k_s, vt_s):
    L = CHUNK
    rows = slice(c * L, (c + 1) * L)
    for hf in range(W_B // MXU_TILE):
        cs = slice(hf * MXU_TILE, (hf + 1) * MXU_TILE)
        xcb = xcb_s[rows, cs]
        if q_s is not None:
            q_s[rows, cs] = _dot(xcb, wq_ref[cs, cs]).astype(BF16)
        k_s[rows, cs] = _dot(xcb, wk_ref[cs, cs]).astype(BF16)
        vt_s[c, cs, :] = _dot(xm_ref[rows, cs], wv_ref[cs, cs]).T.astype(BF16)


_TAB_ROWS = 5 * N_HD


def _chain_tables(growc_s, grow_s, tab_s, nc):
    L = CHUNK
    fwd_rows = lax.broadcasted_iota(jnp.int32, (N_HD, L), 0) < H_B
    gc = growc_s[0]
    steps = [(gc, gc)] + [(grow_s[i], grow_s[nc - 1 - i]) for i in range(nc)]
    parts = []
    for g_f, g_b in steps:
        pick = lambda a: jnp.where(fwd_rows, g_f[a * N_HD:(a + 1) * N_HD, :], g_b[a * N_HD:(a + 1) * N_HD, :])
        cum, r, pm = pick(0), pick(1), pick(2)
        b_end = jnp.where(fwd_rows[:, 0:1], cum[:, L - 1:L], cum[:, 0:1])
        w = r + b_end
        parts.append((cum, pm, b_end, w, jnp.max(w, axis=1, keepdims=True)))
    m0 = jnp.zeros((N_HD, 1), F32)
    for idx, (cum, pm, b_end, w, wmax) in enumerate(parts):
        m_new = jnp.maximum(b_end + m0, wmax)
        mu = jnp.maximum(m0, pm)
        decay = jnp.broadcast_to(jnp.exp(b_end + m0 - m_new), (N_HD, L))
        tab_s[idx] = jnp.concatenate([mu, jnp.exp(m0 - mu), jnp.exp(-(cum + mu)),
                                      jnp.exp(w - m_new), decay], axis=0)
        m0 = m_new


def _tab(tab, a, hd):
    return tab[a * N_HD + hd:a * N_HD + hd + 1, :]


def _state_update(st_ref, hd, vt_ext, k_c, ws_row, decay_hd):
    vw = (vt_ext.astype(F32) * ws_row).astype(BF16)
    st_ref[hd] = decay_hd * st_ref[hd] + _dot(vw, k_c)


def _mlstm_kernel(xm_ref, xmc_ref, cw_ref, cb_ref, wq_ref, wk_ref, wv_ref, wgc_ref, wgm_ref, bg_ref,
                  hs_ref, xc_ref,
                  pad_s, xcb_s, q_s, k_s, vt_s, gcol_s, grow_s,
                  padc_s, xcbc_s, kc_s, vtc_s, gcolc_s, growc_s, tab_s, st_s):
    L = CHUNK
    t = xm_ref.shape[1]
    nc = t // L
    row_i = lax.broadcasted_iota(jnp.int32, (L, L), 0)
    col_i = lax.broadcasted_iota(jnp.int32, (L, L), 1)
    tri = (row_i <= col_i, row_i >= col_i)
    ones_rows = (lax.broadcasted_iota(jnp.int32, (ST_ROWS - DH_B, L), 0) == 0).astype(BF16)
    hsl = lambda hh: slice(hh * DH_B, (hh + 1) * DH_B)
    xm, xmc = xm_ref.at[0], xmc_ref.at[0]
    gate_w = (cw_ref, cb_ref, wgc_ref, wgm_ref, bg_ref)

    _fill_pad(padc_s, xmc, CTX_LEN)
    _fill_pad(pad_s, xm, t)
    gts = [_conv_gate_stage(0, padc_s, xmc, *gate_w, None, xcbc_s)]
    gts += [_conv_gate_stage(c, pad_s, xm, *gate_w, xc_ref.at[0], xcb_s) for c in range(nc)]
    dests = [(growc_s, 0, gcolc_s, 0)] + [(grow_s, c, gcol_s, c * L) for c in range(nc)]
    tables = _gate_tables(gts, dests)
    _headwise_stage(0, xmc, xcbc_s, wq_ref, wk_ref, wv_ref, None, kc_s, vtc_s)
    for c in range(nc):
        next(tables, None)
        next(tables, None)
        _headwise_stage(c, xm, xcb_s, wq_ref, wk_ref, wv_ref, q_s, k_s, vt_s)
    for _ in tables:
        pass

    _chain_tables(growc_s, grow_s, tab_s, nc)
    st_s[...] = jnp.zeros(st_s.shape, F32)
    tab = tab_s[0]
    for hd in range(N_HD):
        hh = hd % H_B
        vt_ext = jnp.concatenate([vtc_s[0, hsl(hh), :], ones_rows], axis=0)
        _state_update(st_s, hd, vt_ext, kc_s[:, hsl(hh)], _tab(tab, 3, hd), _tab(tab, 4, hd)[:, 0:DH_B])

    def body(i, carry, *, accumulate):
        ci = (i, nc - 1 - i)
        rows = tuple(pl.ds(pl.multiple_of(c * L, L), L) for c in ci)
        tab = tab_s[i + 1]
        gcols = tuple(gcol_s[r, :] for r in rows)
        live = {}

        def stage_scores(hd):
            d, hh = hd // H_B, hd % H_B
            k_c = k_s[rows[d], hsl(hh)]
            vt_ext = jnp.concatenate([vt_s[ci[d], hsl(hh), :], ones_rows], axis=0)
            lhs = jnp.concatenate([k_c, st_s[hd].astype(BF16)], axis=0)
            live[hd] = (k_c, vt_ext, _dot_nt(lhs, q_s[rows[d], hsl(hh)]))

        def stage_gate(hd):
            d = hd // H_B
            k_c, vt_ext, res = live[hd]
            rcol = gcols[d][:, hd:hd + 1]
            e = jnp.exp(jnp.where(tri[d], rcol - _tab(tab, 0, hd), -jnp.inf))
            p = (res[0:L, :] * e).astype(BF16)
            _state_update(st_s, hd, vt_ext, k_c, _tab(tab, 3, hd), _tab(tab, 4, hd)[:, 0:DH_B])
            live[hd] = (vt_ext, res[L:L + ST_ROWS, :], p)

        def stage_out(hd):
            d, hh = hd // H_B, hd % H_B
            vt_ext, inter, p = live.pop(hd)
            tot = _tab(tab, 1, hd) * inter + _dot(vt_ext, p)
            den = tot[DH_B:DH_B + 1, :]
            h_t = tot[0:DH_B, :] / jnp.maximum(jnp.abs(den), _tab(tab, 2, hd))
            for pc in range(L // LANES):
                piece = h_t[:, pc * LANES:(pc + 1) * LANES]
                if accumulate:
                    hs_ref[0, ci[d] * (L // LANES) + pc, hsl(hh), :] += piece
                else:
                    hs_ref[0, ci[d] * (L // LANES) + pc, hsl(hh), :] = piece

        stages = (stage_scores, stage_gate, stage_out)
        for step in range(N_HD + len(stages) - 1):
            for si in range(len(stages)):
                if 0 <= step - si < N_HD:
                    stages[si](step - si)
        return carry

    lax.fori_loop(0, nc // 2, functools.partial(body, accumulate=False), 0)
    lax.fori_loop(nc // 2, nc, functools.partial(body, accumulate=True), 0)


def _mlstm_out(hs_t, som, szm, xc, mhg, skip):
    hsum = hs_t.T * som.astype(F32)
    parts = [_layer_norm(hsum[:, hh * DH_B:(hh + 1) * DH_B]) for hh in range(H_B)]
    hb = jnp.concatenate(parts, axis=1) * mhg + skip * xc.astype(F32)
    return (hb * szm.astype(F32)).astype(BF16)


def _mlstm_call(xm, xmc, conv_w, conv_b, wq, wk, wv, wgc, wgm, bg):
    b, t, _ = xm.shape
    nc = t // CHUNK
    assert nc % 2 == 0
    seq = lambda n: pl.BlockSpec((1, n, W_B), lambda i: (i, 0, 0))
    const = lambda shape: pl.BlockSpec(shape, lambda i: (0,) * len(shape))
    scratch = [
        pltpu.VMEM((t + 2 * _PAD, W_B), F32),
        pltpu.VMEM((t, W_B), BF16),
        pltpu.VMEM((t, W_B), BF16),
        pltpu.VMEM((t, W_B), BF16),
        pltpu.VMEM((nc, W_B, CHUNK), BF16),
        pltpu.VMEM((t, LANES), F32),
        pltpu.VMEM((nc, 3 * N_HD, CHUNK), F32),
        pltpu.VMEM((CTX_LEN + 2 * _PAD, W_B), F32),
        pltpu.VMEM((CTX_LEN, W_B), BF16),
        pltpu.VMEM((CTX_LEN, W_B), BF16),
        pltpu.VMEM((1, W_B, CHUNK), BF16),
        pltpu.VMEM((CTX_LEN, LANES), F32),
        pltpu.VMEM((1, 3 * N_HD, CHUNK), F32),
        pltpu.VMEM((nc + 1, _TAB_ROWS, CHUNK), F32),
        pltpu.VMEM((N_HD, ST_ROWS, DH_B), F32),
    ]
    return pl.pallas_call(
        _mlstm_kernel,
        grid=(b,),
        in_specs=[seq(t), seq(CTX_LEN), const((3, W_B)), const((1, W_B)),
                  const((W_B, W_B)), const((W_B, W_B)), const((W_B, W_B)),
                  const((W_B, LANES)), const((W_B, LANES)), const((1, LANES))],
        out_specs=[pl.BlockSpec((1, t // LANES, W_B, LANES), lambda i: (i, 0, 0, 0)), seq(t)],
        out_shape=[jax.ShapeDtypeStruct((b, t // LANES, W_B, LANES), F32),
                   jax.ShapeDtypeStruct((b, t, W_B), BF16)],
        scratch_shapes=scratch,
        compiler_params=pltpu.CompilerParams(dimension_semantics=("arbitrary",),
                                             vmem_limit_bytes=VMEM_LIMIT),
        name="mlstm_bidir",
    )(xm, xmc, conv_w, conv_b, wq, wk, wv, wgc, wgm, bg)


def _out_kernel(h_ref, mod_ref, ya_ref, yb_ref, wo_ref, g_ref, b_ref, o_ref, *, n_sub):
    sub = h_ref.shape[1] // n_sub
    gate = mod_ref[0][:, 2 * D_MODEL:3 * D_MODEL]

    def mix_stage(s):
        rows = slice(s * sub, (s + 1) * sub)
        return (_dot(ya_ref[0, rows, :], wo_ref[0:W_A, :])
                + _dot(yb_ref[0, rows, :], wo_ref[W_A:W_A + W_B, :]))

    def norm_stage(s, y):
        rows = slice(s * sub, (s + 1) * sub)
        o_ref[0, rows, :] = _layer_norm(ALPHA * h_ref[0, rows, :] + gate * y) * g_ref[...] + b_ref[...]

    y = mix_stage(0)
    for s in range(n_sub):
        y_next = mix_stage(s + 1) if s + 1 < n_sub else None
        norm_stage(s, y)
        y = y_next


def _out_call(h, mod3, ya, yb, wo, g, bb, *, tm, n_sub):
    b, t, _ = h.shape
    const = lambda shape: pl.BlockSpec(shape, lambda i, j: (0,) * len(shape))
    row = lambda w: pl.BlockSpec((1, tm, w), lambda i, j: (i, j, 0))
    return pl.pallas_call(
        functools.partial(_out_kernel, n_sub=n_sub),
        grid=(b, t // tm),
        in_specs=[row(D_MODEL), pl.BlockSpec((1, 1, 3 * D_MODEL), lambda i, j: (i, 0, 0)),
                  row(W_A), row(W_B),
                  const((W_A + W_B, D_MODEL)), const((1, D_MODEL)), const((1, D_MODEL))],
        out_specs=row(D_MODEL),
        out_shape=jax.ShapeDtypeStruct((b, t, D_MODEL), F32),
        compiler_params=pltpu.CompilerParams(dimension_semantics=("parallel", "parallel"),
                                             vmem_limit_bytes=VMEM_LIMIT),
        name="out_proj_ln",
    )(h, mod3, ya, yb, wo, g, bb)


def _rot_partner(w):
    w4 = w.reshape(w.shape[:-1] + (2, 2, ROPE_FREQS))
    return jnp.stack([-w4[..., 1, :], w4[..., 0, :]], axis=-2).reshape(w.shape)


def _rope_tables(seq, scale_keep, scale_rope):
    n_rows = seq // GRID_W
    rowp = np.repeat(np.arange(n_rows, dtype=np.float32), GRID_W)
    colp = np.tile(np.arange(GRID_W, dtype=np.float32), n_rows)
    inv = (np.float32(ROPE_BASE) ** (-np.arange(ROPE_FREQS, dtype=np.float32) / np.float32(ROPE_FREQS)))
    ang = np.stack([rowp[:, None] * inv, colp[:, None] * inv], axis=1).astype(np.float32)
    cos = np.broadcast_to(np.cos(ang)[:, :, None, :], (seq, 2, 2, ROPE_FREQS)).reshape(seq, DR_A)
    sin = np.broadcast_to(np.sin(ang)[:, :, None, :], (seq, 2, 2, ROPE_FREQS)).reshape(seq, DR_A)
    z32 = np.zeros((seq, DR_A), np.float32)
    t1 = np.concatenate([np.full((seq, DN_A), scale_keep, np.float32), z32, cos * scale_rope], axis=1)
    t2 = np.concatenate([np.zeros((seq, DN_A), np.float32), z32, sin * scale_rope], axis=1)
    return jnp.asarray(t1, F32), jnp.asarray(t2, F32)


def kernel(x, c, ctx, c_ctx, ln_in_g, ln_in_b, w_ada, b_ada, w_in, g_qa, w_qb, g_kva, w_kvb, conv_w, conv_b, w_mq, w_mk, w_mv, w_gate, b_gate, mh_g, skip, w_out, ln_g, ln_b):
    b, t, _ = x.shape
    l = 0
    r2 = lambda v: v.reshape(1, -1)

    wq3 =w_qb[l].reshape(Q_LORA, H_A, DN_A + DR_A)
    wq_r = wq3[..., DN_A:]
    wq = jnp.concatenate([wq3[..., :DN_A], _rot_partner(wq_r), wq_r], axis=-1)
    wq = wq.reshape(Q_LORA, H_A * SLAB).astype(BF16)
    wm = jnp.concatenate([jnp.transpose(w[l], (1, 2, 0)).reshape(QKV_BS * QKV_BS, -1)
                          for w in (w_mq, w_mk, w_mv)], axis=0)
    wo = w_out[l].astype(BF16)

    sm_scale = (DN_A + DR_A) ** -0.5 * LOG2_E
    t1q, t2q = _rope_tables(t, sm_scale, sm_scale)
    t1k, t2k = _rope_tables(t, 0.0, 1.0)
    n_ctx = ctx.shape[1]

    mod3, wina, winr, wmq, wmk_s, wmv, wgc, wgm, bg, wkn, wvt = _prep_call(
        c, r2(c_ctx), w_ada[l], r2(b_ada[l]), jnp.swapaxes(w_in[l], 0, 1), wm,
        jnp.swapaxes(w_gate[l], 0, 1), r2(b_gate[l]), w_kvb[l])

    lng, lnb = r2(ln_in_g), r2(ln_in_b)
    shared = (wina, winr, r2(g_qa[l]), wq, r2(g_kva[l]), wkn, wvt)
    q, kl, vtl, xm, sza, som, szm, h = _proj_call(x, mod3, None, lng, lnb, *shared, t1q, t2q, t1k, t2k,
                                                  tm=1024, n_sub=2, latent=True)
    kc, vtc, xmc = _proj_call(ctx.reshape(1, b * n_ctx, D_MODEL), mod3, b, lng, lnb, *shared,
                              tm=min(1024, b * n_ctx), n_sub=2, latent=False)
    xmc = xmc.reshape(b, n_ctx, W_B)

    hs, xc = _mlstm_call(xm, xmc, conv_w[l], r2(conv_b[l]), wmq, wmk_s, wmv, wgc, wgm, bg)
    ya, yb = _attn_call(q, kc, kl, vtc, vtl, sza, hs, som, szm, xc, r2(mh_g[l]), r2(skip[l]), tq=512)
    return _out_call(h, mod3, ya, yb, wo, r2(ln_g[l]), r2(ln_b[l]), tm=1024, n_sub=4)
```

```python
import functools

import numpy as np
import jax
import jax.numpy as jnp
from jax import lax
from jax.experimental import pallas as pl
from jax.experimental.pallas import tpu as pltpu

F32 = jnp.float32
BF16 = jnp.bfloat16

D_MODEL = 1024
CTX_LEN = 256
GRID_W = 64
H_A, DN_A, DR_A, DV_A = 8, 64, 32, 64
W_A = H_A * DV_A
Q_LORA, KV_LORA = 256, 128
ROPE_FREQS = DR_A // 4
ROPE_BASE = 10000.0
H_B, DH_B = 4, 128
W_B = H_B * DH_B
QKV_BS = 4
DEPTH = 1
ALPHA = (2.0 * DEPTH) ** 0.25
LN_EPS = 1e-5
RMS_EPS = 1e-6
LOG2_E = 1.4426950408889634

LANES = 128
MXU_TILE = 256
SLAB = LANES
DV_EXT = DV_A + 16
ATTN_KEY_CHUNK = MXU_TILE
CHUNK = 256
N_HD = 2 * H_B
ST_ROWS = DH_B + 16
VMEM_LIMIT = 56 * 1024 * 1024

_NT = (((1,), (1,)), ((), ()))


def _dot(a, b):
    return jnp.dot(a, b, preferred_element_type=F32)


def _dot_nt(a, b):
    return lax.dot_general(a, b, _NT, preferred_element_type=F32)


def _layer_norm(x):
    mu = jnp.mean(x, axis=-1, keepdims=True)
    xc = x - mu
    var = jnp.mean(xc * xc, axis=-1, keepdims=True)
    return xc * lax.rsqrt(var + LN_EPS)


def _rms_norm(x, g):
    return (x * lax.rsqrt(jnp.mean(x * x, axis=-1, keepdims=True) + RMS_EPS)) * g


def _silu(x):
    return x * jax.nn.sigmoid(x)


def _log_sigmoid(x):
    return jnp.minimum(x, 0.0) - jnp.log1p(jnp.exp(-jnp.abs(x)))


def _ada_kernel(c_ref, cctx_ref, w_ref, b_ref, o_ref):
    first = lax.broadcasted_iota(jnp.int32, (8, D_MODEL), 0) == 0
    cc = jnp.concatenate([c_ref[...], jnp.where(first, cctx_ref[...], 0.0)], axis=0)
    a = _silu(cc).astype(BF16)
    mod = _dot(a, w_ref[...].astype(BF16)) + b_ref[...]
    for r in range(o_ref.shape[0]):
        o_ref[r] = mod[r:r + 1, :]


_S_KVA, _S_KR = Q_LORA + KV_LORA, Q_LORA + KV_LORA + DR_A


def _win_kernel(wt_ref, wina_ref, winr_ref):
    n_r = winr_ref.shape[1]
    for j in range(_S_KVA // LANES):
        wina_ref[:, j * LANES:(j + 1) * LANES] = wt_ref[j * LANES:(j + 1) * LANES, :].T.astype(BF16)
    kr = wt_ref[_S_KVA:_S_KR, :]
    a0, b0, a1, b1 = (kr[i * ROPE_FREQS:(i + 1) * ROPE_FREQS, :] for i in range(4))
    blk = jnp.concatenate([jnp.zeros((DN_A, D_MODEL), F32), -b0, a0, -b1, a1, kr], axis=0)
    wina_ref[:, _S_KVA:_S_KVA + LANES] = blk.T.astype(BF16)
    for j in range(n_r // MXU_TILE):
        rows = slice(_S_KR + j * MXU_TILE, _S_KR + (j + 1) * MXU_TILE)
        winr_ref[:, j * MXU_TILE:(j + 1) * MXU_TILE] = wt_ref[rows, :].T.astype(BF16)


def _rope_slab(s, t1, t2):
    return s * t1 + pltpu.roll(s, 32, 1) * t2


def _proj_kernel(x_ref, mod_ref, lng_ref, lnb_ref, wina_ref, winr_ref, gqa_ref, wq_ref, gkva_ref,
                 wkn_ref, wvt_ref, *refs, latent, n_sub):
    if latent:
        t1q_ref, t2q_ref, t1k_ref, t2k_ref, q_ref, k_ref, vt_ref, xm_ref, sza_ref, som_ref, szm_ref, h_ref = refs
    else:
        k_ref, vt_ref, xm_ref = refs
    sub = x_ref.shape[1] // n_sub
    mod = mod_ref[0]
    shift, scale1 = mod[:, 0:D_MODEL], 1.0 + mod[:, D_MODEL:2 * D_MODEL]
    ones_rows = (lax.broadcasted_iota(jnp.int32, (DV_EXT - DV_A, sub), 0) == 0).astype(BF16)

    def norm_stage(s):
        rows = slice(s * sub, (s + 1) * sub)
        h = _layer_norm(x_ref[0, rows, :]) * lng_ref[...] + lnb_ref[...]
        if latent:
            h_ref[0, rows, :] = h
        return (h * scale1 + shift).astype(BF16)

    def proj_stage(s, u):
        rows = slice(s * sub, (s + 1) * sub)
        p0 = _dot(u, wina_ref[...])
        d_xm = _dot(u, winr_ref[:, 512:1024])
        kvn = _rms_norm(p0[:, 256:384], gkva_ref[...]).astype(BF16)
        kn = _dot(kvn, wkn_ref[...])
        d_za = _dot(u, winr_ref[:, 0:512]) if latent else None
        if latent:
            krr = _rope_slab(p0[:, 384:512], t1k_ref[rows, :], t2k_ref[rows, :])
        else:
            rope_lanes = lax.broadcasted_iota(jnp.int32, (sub, SLAB), 1) >= SLAB - DR_A
            krr = jnp.where(rope_lanes, p0[:, 384:512], 0.0)
        for hh in range(H_A):
            k_ref[0, hh, rows, :] = (kn[:, hh * SLAB:(hh + 1) * SLAB] + krr).astype(BF16)
        vt = _dot_nt(wvt_ref[...], kvn).astype(BF16)
        for hh in range(H_A):
            vt_ref[0, hh, 0:DV_A, rows] = vt[hh * DV_A:(hh + 1) * DV_A, :]
            vt_ref[0, hh, DV_A:DV_EXT, rows] = ones_rows
        xm_ref[0, rows, :] = d_xm.astype(BF16)
        if latent:
            d_om = _dot(u, winr_ref[:, 1024:1536])
            qn = _rms_norm(p0[:, 0:256], gqa_ref[...]).astype(BF16)
            qs = _dot(qn, wq_ref[...])
            sza_ref[0, rows, :] = _silu(d_za).astype(BF16)
            d_zm = _dot(u, winr_ref[:, 1536:2048])
            t1q, t2q = t1q_ref[rows, :], t2q_ref[rows, :]
            for hh in range(H_A):
                q_ref[0, hh, rows, :] = _rope_slab(qs[:, hh * SLAB:(hh + 1) * SLAB], t1q, t2q).astype(BF16)
            som_ref[0, rows, :] = jax.nn.sigmoid(d_om).astype(BF16)
            szm_ref[0, rows, :] = _silu(d_zm).astype(BF16)

    u = norm_stage(0)
    for s in range(n_sub):
        u_next = norm_stage(s + 1) if s + 1 < n_sub else None
        proj_stage(s, u)
        u = u_next


def _proj_call(x, mod3, mod_row, lng, lnb, wina, winr, gqa, wq, gkva, wkn, wvt, *tables, tm, n_sub, latent):
    b, t, _ = x.shape
    const = lambda shape: pl.BlockSpec(shape, lambda i, j: (0,) * len(shape))
    tab = pl.BlockSpec((tm, SLAB), lambda i, j: (j, 0))
    assert len(tables) == (4 if latent else 0)
    row = lambda w: pl.BlockSpec((1, tm, w), lambda i, j: (i, j, 0))
    if mod_row is None:
        mod_spec = pl.BlockSpec((1, 1, 3 * D_MODEL), lambda i, j: (i, 0, 0))
    else:
        mod_spec = pl.BlockSpec((1, 1, 3 * D_MODEL), lambda i, j: (mod_row, 0, 0))
    in_specs = [row(D_MODEL), mod_spec, const((1, D_MODEL)), const((1, D_MODEL)),
                const(wina.shape), const(winr.shape), const((1, Q_LORA)), const((Q_LORA, H_A * SLAB)),
                const((1, KV_LORA)), const((KV_LORA, H_A * SLAB)), const((W_A, KV_LORA))]
    in_specs += [tab] * len(tables)
    k_spec = pl.BlockSpec((1, H_A, tm, SLAB), lambda i, j: (i, 0, j, 0))
    vt_spec = pl.BlockSpec((1, H_A, DV_EXT, tm), lambda i, j: (i, 0, 0, j))
    k_shape = jax.ShapeDtypeStruct((b, H_A, t, SLAB), BF16)
    vt_shape = jax.ShapeDtypeStruct((b, H_A, DV_EXT, t), BF16)
    half = jax.ShapeDtypeStruct((b, t, W_B), BF16)
    if latent:
        out_specs = [k_spec, k_spec, vt_spec, row(W_B), row(W_A), row(W_B), row(W_B), row(D_MODEL)]
        out_shape = [k_shape, k_shape, vt_shape, half, half, half, half,
                     jax.ShapeDtypeStruct((b, t, D_MODEL), F32)]
    else:
        out_specs = [k_spec, vt_spec, row(W_B)]
        out_shape = [k_shape, vt_shape, half]
    return pl.pallas_call(
        functools.partial(_proj_kernel, latent=latent, n_sub=n_sub),
        grid=(b, t // tm),
        in_specs=in_specs, out_specs=out_specs, out_shape=out_shape,
        compiler_params=pltpu.CompilerParams(dimension_semantics=("parallel", "parallel"),
                                             vmem_limit_bytes=VMEM_LIMIT),
        name="in_proj_latent" if latent else "in_proj_ctx",
    )(x, mod3, lng, lnb, wina, winr, gqa, wq, gkva, wkn, wvt, *tables)


def _attn_kernel(q_ref, kc_ref, kl_ref, vtc_ref, vtl_ref, sza_ref, hs_ref, som_ref, szm_ref, xc_ref,
                 mhg_ref, skip_ref, o_ref, yb_ref, s_buf, ot_s):
    n_ctx, t = kc_ref.shape[2], kl_ref.shape[2]
    kc = ATTN_KEY_CHUNK

    def scores(h, slot):
        qh = q_ref[0, h]
        sc = _dot_nt(kc_ref[0, h], qh)
        sk = _dot_nt(kl_ref[0, h], qh)
        s_buf[slot, 0:n_ctx, :] = sc
        s_buf[slot, n_ctx:n_ctx + t, :] = sk
        return jnp.maximum(jnp.max(sc, axis=0, keepdims=True), jnp.max(sk, axis=0, keepdims=True))

    def values(h, slot, m):
        chunks = [(vtc_ref, c * kc, c * kc) for c in range(n_ctx // kc)]
        chunks += [(vtl_ref, c * kc, n_ctx + c * kc) for c in range(t // kc)]
        acc = None
        for vref, v0, s0 in chunks:
            p = jnp.exp2(s_buf[slot, s0:s0 + kc, :] - m).astype(BF16)
            d = _dot(vref[0, h, :, v0:v0 + kc], p)
            acc = d if acc is None else acc + d
        rows = pl.ds(pl.multiple_of(h * DV_A, DV_A), DV_A)
        ot_s[rows, :] = acc[0:DV_A, :] / acc[DV_A:DV_A + 1, :]

    def cell_out(pc):
        rows = pl.ds(pl.multiple_of(pc * LANES, LANES), LANES)
        yb_ref[0, rows, :] = _mlstm_out(hs_ref[0, pc], som_ref[0, rows, :], szm_ref[0, rows, :],
                                        xc_ref[0, rows, :], mhg_ref[...], skip_ref[...])

    def body(i, m_even):
        h = 2 * i
        m_odd = scores(h + 1, 1)
        cell_out(i + 1)
        values(h, 0, m_even)
        m_even = scores(h + 2, 0)
        values(h + 1, 1, m_odd)
        return m_even

    assert hs_ref.shape[1] == H_A // 2
    m_first = scores(0, 0)
    cell_out(0)
    m_even = lax.fori_loop(0, H_A // 2 - 1, body, m_first)
    m_odd = scores(H_A - 1, 1)
    values(H_A - 2, 0, m_even)
    values(H_A - 1, 1, m_odd)
    o_ref[0] = (ot_s[...].T * sza_ref[0].astype(F32)).astype(BF16)


def _attn_call(q, kc, kl, vtc, vtl, sza, hs, som, szm, xc, mhg, skip, *, tq):
    b, _, t, _ = q.shape
    n_ctx = kc.shape[2] // b
    head_blk = lambda n, w: pl.BlockSpec((1, H_A, n, w), lambda i, j: (i, 0, 0, 0))
    const = lambda shape: pl.BlockSpec(shape, lambda i, j: (0,) * len(shape))
    row = lambda w: pl.BlockSpec((1, tq, w), lambda i, j: (i, j, 0))
    half = jax.ShapeDtypeStruct((b, t, W_A), BF16)
    return pl.pallas_call(
        _attn_kernel,
        grid=(b, t // tq),
        in_specs=[pl.BlockSpec((1, H_A, tq, SLAB), lambda i, j: (i, 0, j, 0)),
                  pl.BlockSpec((1, H_A, n_ctx, SLAB), lambda i, j: (0, 0, i, 0)), head_blk(t, SLAB),
                  pl.BlockSpec((1, H_A, DV_EXT, n_ctx), lambda i, j: (0, 0, 0, i)), head_blk(DV_EXT, t),
                  row(W_A),
                  pl.BlockSpec((1, tq // LANES, W_B, LANES), lambda i, j: (i, j, 0, 0)),
                  row(W_B), row(W_B), row(W_B), const((1, W_B)), const((1, W_B))],
        out_specs=[row(W_A), row(W_B)],
        out_shape=[half, jax.ShapeDtypeStruct((b, t, W_B), BF16)],
        scratch_shapes=[pltpu.VMEM((2, n_ctx + t, tq), F32),
                        pltpu.VMEM((W_A, tq), F32)],
        compiler_params=pltpu.CompilerParams(dimension_semantics=("parallel", "parallel"),
                                             vmem_limit_bytes=VMEM_LIMIT),
        name="mla_attention",
    )(q, kc, kl, vtc, vtl, sza, hs, som, szm, xc, mhg, skip)


def _onehot(cond):
    return cond.astype(F32).astype(BF16)


def _block_diag(a):
    bs, nb = QKV_BS, W_B // QKV_BS
    lb, sq = bs.bit_length() - 1, bs * bs
    assert bs == 1 << lb
    iota = lambda shape, d: lax.broadcasted_iota(jnp.int32, shape, d)
    spread = _onehot(iota((nb, W_B), 1) >> lb == iota((nb, W_B), 0))
    b = _dot(a, spread)
    col_o = iota((sq, W_B), 1) & (bs - 1)
    bm = jnp.concatenate([jnp.where(col_o == o, b, 0.0) for o in range(bs)], axis=0).astype(BF16)
    j = iota((W_B, bs * sq), 1)
    pick = _onehot((j & (sq - 1)) == ((iota((W_B, bs * sq), 0) & (bs - 1)) << lb) + (j >> (2 * lb)))
    same_block = iota((W_B, W_B), 0) >> lb == iota((W_B, W_B), 1) >> lb
    return jnp.where(same_block, _dot(pick, bm), 0.0).astype(BF16)


def _mlstm_weight_kernel(aq_ref, ak_ref, av_ref, wgt_ref, bgate_ref,
                         wmq_ref, wmk_ref, wmv_ref, wgc_ref, wgm_ref, bg_ref):
    a_q, a_k, a_v = aq_ref[...], ak_ref[...], av_ref[...]
    wq, wk, wv = (_block_diag(a.astype(BF16)) for a in (a_q, a_k, a_v))
    wmq_ref[...] = wq
    wmk_ref[...] = _block_diag((a_k * (DH_B ** -0.5)).astype(BF16))
    wmv_ref[...] = wv
    dst, src = (lax.broadcasted_iota(jnp.int32, (LANES, 2 * N_HD), d) for d in (0, 1))
    grp = dst >> (H_B.bit_length() - 1)
    src_of_dst = (dst & (H_B - 1)) + H_B * jnp.where(grp == 1, 2, jnp.where(grp == 2, 1, grp))
    perm = _onehot((dst < 2 * N_HD) & (src == src_of_dst))
    wg_t = _dot(perm, wgt_ref[...].astype(BF16)).astype(BF16)
    wgc_ref[...] = (_dot_nt(wq, wg_t[:, 0:W_B]) + _dot_nt(wk, wg_t[:, W_B:2 * W_B])).astype(BF16)
    wgm_ref[...] = _dot_nt(wv, wg_t[:, 2 * W_B:3 * W_B]).astype(BF16)
    bg_ref[...] = jnp.zeros(bg_ref.shape, F32)
    for d, s in enumerate((0, 2, 1, 3)):
        bg_ref[:, d * H_B:(d + 1) * H_B] = bgate_ref[:, s * H_B:(s + 1) * H_B]


def _qb_kernel(wqb_ref, wq_ref):
    wt = wqb_ref[...].T
    hw = DN_A + DR_A
    for hh in range(H_A):
        rope = wt[hh * hw + DN_A:(hh + 1) * hw, :]
        a0, b0, a1, b1 = (rope[i * ROPE_FREQS:(i + 1) * ROPE_FREQS, :] for i in range(4))
        slab = jnp.concatenate([wt[hh * hw:hh * hw + DN_A, :], -b0, a0, -b1, a1, rope], axis=0)
        wq_ref[:, hh * SLAB:(hh + 1) * SLAB] = slab.T.astype(BF16)


def _kvb_kernel(wkvb_ref, wkn_ref, wvt_ref):
    w = wkvb_ref[...]
    assert DN_A + DV_A == SLAB
    lane = lax.broadcasted_iota(jnp.int32, w.shape, 1)
    wkn_ref[...] = jnp.where((lane & (SLAB - 1)) < DN_A, w, 0.0).astype(BF16)
    wt = w.T
    for hh in range(H_A):
        wvt_ref[hh * DV_A:(hh + 1) * DV_A, :] = wt[hh * SLAB + DN_A:(hh + 1) * SLAB, :].astype(BF16)


def _prep_kernel(c_ref, cctx_ref, wada_ref, bada_ref, wt_ref, aq_ref, ak_ref, av_ref, wgt_ref, bgate_ref,
                 wqb_ref, wkvb_ref,
                 mod_ref, wina_ref, winr_ref, wmq_ref, wmk_ref, wmv_ref, wgc_ref, wgm_ref, bg_ref,
                 wq_ref, wkn_ref, wvt_ref):
    _ada_kernel(c_ref, cctx_ref, wada_ref, bada_ref, mod_ref)
    _win_kernel(wt_ref, wina_ref, winr_ref)
    _mlstm_weight_kernel(aq_ref, ak_ref, av_ref, wgt_ref, bgate_ref,
                         wmq_ref, wmk_ref, wmv_ref, wgc_ref, wgm_ref, bg_ref)
    _qb_kernel(wqb_ref, wq_ref)
    _kvb_kernel(wkvb_ref, wkn_ref, wvt_ref)


def _prep_call(c, c_ctx, w_ada, b_ada, wt, aq, ak, av, wgt, b_gate, w_qb, w_kvb):
    n_in, d = wt.shape
    assert c.shape[0] % 8 == 0 and wgt.shape == (2 * N_HD, 3 * W_B)
    assert aq.shape == ak.shape == av.shape == (QKV_BS * QKV_BS, W_B // QKV_BS)
    bf = lambda *shape: jax.ShapeDtypeStruct(shape, BF16)
    return pl.pallas_call(
        _prep_kernel,
        out_shape=[jax.ShapeDtypeStruct((c.shape[0] + 1, 1, w_ada.shape[1]), F32),
                   bf(d, _S_KVA + LANES), bf(d, n_in - _S_KR),
                   bf(W_B, W_B), bf(W_B, W_B), bf(W_B, W_B), bf(W_B, LANES), bf(W_B, LANES),
                   jax.ShapeDtypeStruct((1, LANES), F32),
                   bf(Q_LORA, H_A * SLAB), bf(KV_LORA, H_A * SLAB), bf(W_A, KV_LORA)],
        compiler_params=pltpu.CompilerParams(vmem_limit_bytes=VMEM_LIMIT),
        name="param_prep",
    )(c, c_ctx, w_ada, b_ada, wt, aq, ak, av, wgt, b_gate, w_qb, w_kvb)


_PAD = 8


def _lane_scans(rows_per_chunk, combine, fill, out):
    npc = CHUNK // LANES
    flat = [x[:, i * LANES:(i + 1) * LANES] for x in rows_per_chunk for i in range(npc)]
    lane = lax.broadcasted_iota(jnp.int32, flat[0].shape, 1)
    pre, suf, sh = list(flat), list(flat), 1
    while sh < LANES:
        pre = [combine(p, jnp.where(lane >= sh, pltpu.roll(p, sh, 1), fill)) for p in pre]
        suf = [combine(s, jnp.where(lane < LANES - sh, pltpu.roll(s, LANES - sh, 1), fill)) for s in suf]
        sh *= 2
        yield
    for c in range(len(rows_per_chunk)):
        p, s = pre[c * npc:(c + 1) * npc], suf[c * npc:(c + 1) * npc]
        tot = [x[:, LANES - 1:LANES] for x in p]
        run = None
        for i in range(npc):
            if run is not None:
                p[i] = combine(p[i], run)
            run = tot[i] if run is None else combine(run, tot[i])
        run = None
        for i in reversed(range(npc)):
            if run is not None:
                s[i] = combine(s[i], run)
            run = tot[i] if run is None else combine(run, tot[i])
        out.append((jnp.concatenate(p, axis=1), jnp.concatenate(s, axis=1)))


def _gate_tables(gts, dests):
    L = CHUNK
    fwd_rows = lax.broadcasted_iota(jnp.int32, (N_HD, L), 0) < H_B
    lis = [gt[0:N_HD, :] for gt in gts]
    sums = []
    yield from _lane_scans([_log_sigmoid(gt[N_HD:2 * N_HD, :]) for gt in gts], jnp.add, 0.0, sums)
    cums = [jnp.where(fwd_rows, ps, ss) for ps, ss in sums]
    rs = [li - cum for li, cum in zip(lis, cums)]
    maxs = []
    yield from _lane_scans(rs, jnp.maximum, -jnp.inf, maxs)
    for (grow_ref, c, gcol_ref, r0), cum, r, (pm, sm) in zip(dests, cums, rs, maxs):
        grow_ref[c] = jnp.concatenate([cum, r, jnp.where(fwd_rows, pm, sm)], axis=0)
        gcol_ref[r0:r0 + L, :] = jnp.concatenate([r, jnp.zeros((LANES - N_HD, L), F32)], axis=0).T


def _fill_pad(pad_s, xm_ref, n):
    zrow = jnp.zeros((_PAD, W_B), F32)
    pad_s[0:_PAD, :] = zrow
    pad_s[_PAD:_PAD + n, :] = xm_ref[...].astype(F32)
    pad_s[_PAD + n:2 * _PAD + n, :] = zrow


def _conv_gate_stage(c, pad_s, xm_ref, cw_ref, cb_ref, wgc_ref, wgm_ref, bg_ref, xc_out, xcb_s):
    L = CHUNK
    rows = slice(c * L, (c + 1) * L)
    r0 = c * L + _PAD
    row_id = lax.broadcasted_iota(jnp.int32, (L, W_B), 0)
    xcur = pad_s[r0:r0 + L, :]
    xprev = jnp.where(row_id == 0, pad_s[r0 - 1:r0, :], pltpu.roll(xcur, 1, 0))
    xnext = jnp.where(row_id == L - 1, pad_s[r0 + L:r0 + L + 1, :], pltpu.roll(xcur, L - 1, 0))
    pre = cb_ref[...] + xprev * cw_ref[0:1, :] + xcur * cw_ref[1:2, :] + xnext * cw_ref[2:3, :]
    xc = _silu(pre)
    xcb = xc.astype(BF16)
    if xc_out is not None:
        xc_out[rows, :] = xcb
    xcb_s[rows, :] = xcb
    g = _dot(xcb, wgc_ref[...]) + _dot(xm_ref[rows, :], wgm_ref[...]) + bg_ref[...]
    return g.T[0:2 * N_HD, :]


def _headwise_stage(c, xm_ref, xcb_s, wq_ref, wk_ref, wv_ref, q_s, k_s, vt_s):
    L = CHUNK
    rows = slice(c * L, (c + 1) * L)
    for hf in range(W_B // MXU_TILE):
        cs = slice(hf * MXU_TILE, (hf + 1) * MXU_TILE)
        xcb = xcb_s[rows, cs]
        if q_s is not None:
            q_s[rows, cs] = _dot(xcb, wq_ref[cs, cs]).astype(BF16)
        k_s[rows, cs] = _dot(xcb, wk_ref[cs, cs]).astype(BF16)
        vt_s[c, cs, :] = _dot(xm_ref[rows, cs], wv_ref[cs, cs]).T.astype(BF16)


_TAB_ROWS = 5 * N_HD


def _chain_tables(growc_s, grow_s, tab_s, nc):
    L = CHUNK
    fwd_rows = lax.broadcasted_iota(jnp.int32, (N_HD, L), 0) < H_B
    gc = growc_s[0]
    steps = [(gc, gc)] + [(grow_s[i], grow_s[nc - 1 - i]) for i in range(nc)]
    parts = []
    for g_f, g_b in steps:
        pick = lambda a: jnp.where(fwd_rows, g_f[a * N_HD:(a + 1) * N_HD, :], g_b[a * N_HD:(a + 1) * N_HD, :])
        cum, r, pm = pick(0), pick(1), pick(2)
        b_end = jnp.where(fwd_rows[:, 0:1], cum[:, L - 1:L], cum[:, 0:1])
        w = r + b_end
        parts.append((cum, pm, b_end, w, jnp.max(w, axis=1, keepdims=True)))
    m0 = jnp.zeros((N_HD, 1), F32)
    for idx, (cum, pm, b_end, w, wmax) in enumerate(parts):
        m_new = jnp.maximum(b_end + m0, wmax)
        mu = jnp.maximum(m0, pm)
        decay = jnp.broadcast_to(jnp.exp(b_end + m0 - m_new), (N_HD, L))
        tab_s[idx] = jnp.concatenate([mu, jnp.exp(m0 - mu), jnp.exp(-(cum + mu)),
                                      jnp.exp(w - m_new), decay], axis=0)
        m0 = m_new


def _tab(tab, a, hd):
    return tab[a * N_HD + hd:a * N_HD + hd + 1, :]


def _state_update(st_ref, hd, vt_ext, k_c, ws_row, decay_hd):
    vw = (vt_ext.astype(F32) * ws_row).astype(BF16)
    st_ref[hd] = decay_hd * st_ref[hd] + _dot(vw, k_c)


def _mlstm_kernel(xm_ref, xmc_ref, cw_ref, cb_ref, wq_ref, wk_ref, wv_ref, wgc_ref, wgm_ref, bg_ref,
                  hs_ref, xc_ref,
                  pad_s, xcb_s, q_s, k_s, vt_s, gcol_s, grow_s,
                  padc_s, xcbc_s, kc_s, vtc_s, gcolc_s, growc_s, tab_s, st_s):
    L = CHUNK
    t = xm_ref.shape[1]
    nc = t // L
    row_i = lax.broadcasted_iota(jnp.int32, (L, L), 0)
    col_i = lax.broadcasted_iota(jnp.int32, (L, L), 1)
    tri = (row_i <= col_i, row_i >= col_i)
    ones_rows = (lax.broadcasted_iota(jnp.int32, (ST_ROWS - DH_B, L), 0) == 0).astype(BF16)
    hsl = lambda hh: slice(hh * DH_B, (hh + 1) * DH_B)
    xm, xmc = xm_ref.at[0], xmc_ref.at[0]
    gate_w = (cw_ref, cb_ref, wgc_ref, wgm_ref, bg_ref)

    _fill_pad(padc_s, xmc, CTX_LEN)
    _fill_pad(pad_s, xm, t)
    gts = [_conv_gate_stage(0, padc_s, xmc, *gate_w, None, xcbc_s)]
    gts += [_conv_gate_stage(c, pad_s, xm, *gate_w, xc_ref.at[0], xcb_s) for c in range(nc)]
    dests = [(growc_s, 0, gcolc_s, 0)] + [(grow_s, c, gcol_s, c * L) for c in range(nc)]
    tables = _gate_tables(gts, dests)
    _headwise_stage(0, xmc, xcbc_s, wq_ref, wk_ref, wv_ref, None, kc_s, vtc_s)
    for c in range(nc):
        next(tables, None)
        next(tables, None)
        _headwise_stage(c, xm, xcb_s, wq_ref, wk_ref, wv_ref, q_s, k_s, vt_s)
    for _ in tables:
        pass

    _chain_tables(growc_s, grow_s, tab_s, nc)
    st_s[...] = jnp.zeros(st_s.shape, F32)
    tab = tab_s[0]
    for hd in range(N_HD):
        hh = hd % H_B
        vt_ext = jnp.concatenate([vtc_s[0, hsl(hh), :], ones_rows], axis=0)
        _state_update(st_s, hd, vt_ext, kc_s[:, hsl(hh)], _tab(tab, 3, hd), _tab(tab, 4, hd)[:, 0:DH_B])

    def body(i, carry, *, accumulate):
        ci = (i, nc - 1 - i)
        rows = tuple(pl.ds(pl.multiple_of(c * L, L), L) for c in ci)
        tab = tab_s[i + 1]
        gcols = tuple(gcol_s[r, :] for r in rows)
        live = {}

        def stage_scores(hd):
            d, hh = hd // H_B, hd % H_B
            k_c = k_s[rows[d], hsl(hh)]
            vt_ext = jnp.concatenate([vt_s[ci[d], hsl(hh), :], ones_rows], axis=0)
            lhs = jnp.concatenate([k_c, st_s[hd].astype(BF16)], axis=0)
            live[hd] = (k_c, vt_ext, _dot_nt(lhs, q_s[rows[d], hsl(hh)]))

        def stage_gate(hd):
            d = hd // H_B
            k_c, vt_ext, res = live[hd]
            rcol = gcols[d][:, hd:hd + 1]
            e = jnp.exp(jnp.where(tri[d], rcol - _tab(tab, 0, hd), -jnp.inf))
            p = (res[0:L, :] * e).astype(BF16)
            _state_update(st_s, hd, vt_ext, k_c, _tab(tab, 3, hd), _tab(tab, 4, hd)[:, 0:DH_B])
            live[hd] = (vt_ext, res[L:L + ST_ROWS, :], p)

        def stage_out(hd):
            d, hh = hd // H_B, hd % H_B
            vt_ext, inter, p = live.pop(hd)
            tot = _tab(tab, 1, hd) * inter + _dot(vt_ext, p)
            den = tot[DH_B:DH_B + 1, :]
            h_t = tot[0:DH_B, :] / jnp.maximum(jnp.abs(den), _tab(tab, 2, hd))
            for pc in range(L // LANES):
                piece = h_t[:, pc * LANES:(pc + 1) * LANES]
                if accumulate:
                    hs_ref[0, ci[d] * (L // LANES) + pc, hsl(hh), :] += piece
                else:
                    hs_ref[0, ci[d] * (L // LANES) + pc, hsl(hh), :] = piece

        stages = (stage_scores, stage_gate, stage_out)
        for step in range(N_HD + len(stages) - 1):
            for si in range(len(stages)):
                if 0 <= step - si < N_HD:
                    stages[si](step - si)
        return carry

    lax.fori_loop(0, nc // 2, functools.partial(body, accumulate=False), 0)
    lax.fori_loop(nc // 2, nc, functools.partial(body, accumulate=True), 0)


def _mlstm_out(hs_t, som, szm, xc, mhg, skip):
    hsum = hs_t.T * som.astype(F32)
    parts = [_layer_norm(hsum[:, hh * DH_B:(hh + 1) * DH_B]) for hh in range(H_B)]
    hb = jnp.concatenate(parts, axis=1) * mhg + skip * xc.astype(F32)
    return (hb * szm.astype(F32)).astype(BF16)


def _mlstm_call(xm, xmc, conv_w, conv_b, wq, wk, wv, wgc, wgm, bg):
    b, t, _ = xm.shape
    nc = t // CHUNK
    assert nc % 2 == 0
    seq = lambda n: pl.BlockSpec((1, n, W_B), lambda i: (i, 0, 0))
    const = lambda shape: pl.BlockSpec(shape, lambda i: (0,) * len(shape))
    scratch = [
        pltpu.VMEM((t + 2 * _PAD, W_B), F32),
        pltpu.VMEM((t, W_B), BF16),
        pltpu.VMEM((t, W_B), BF16),
        pltpu.VMEM((t, W_B), BF16),
        pltpu.VMEM((nc, W_B, CHUNK), BF16),
        pltpu.VMEM((t, LANES), F32),
        pltpu.VMEM((nc, 3 * N_HD, CHUNK), F32),
        pltpu.VMEM((CTX_LEN + 2 * _PAD, W_B), F32),
        pltpu.VMEM((CTX_LEN, W_B), BF16),
        pltpu.VMEM((CTX_LEN, W_B), BF16),
        pltpu.VMEM((1, W_B, CHUNK), BF16),
        pltpu.VMEM((CTX_LEN, LANES), F32),
        pltpu.VMEM((1, 3 * N_HD, CHUNK), F32),
        pltpu.VMEM((nc + 1, _TAB_ROWS, CHUNK), F32),
        pltpu.VMEM((N_HD, ST_ROWS, DH_B), F32),
    ]
    return pl.pallas_call(
        _mlstm_kernel,
        grid=(b,),
        in_specs=[seq(t), seq(CTX_LEN), const((3, W_B)), const((1, W_B)),
                  const((W_B, W_B)), const((W_B, W_B)), const((W_B, W_B)),
                  const((W_B, LANES)), const((W_B, LANES)), const((1, LANES))],
        out_specs=[pl.BlockSpec((1, t // LANES, W_B, LANES), lambda i: (i, 0, 0, 0)), seq(t)],
        out_shape=[jax.ShapeDtypeStruct((b, t // LANES, W_B, LANES), F32),
                   jax.ShapeDtypeStruct((b, t, W_B), BF16)],
        scratch_shapes=scratch,
        compiler_params=pltpu.CompilerParams(dimension_semantics=("arbitrary",),
                                             vmem_limit_bytes=VMEM_LIMIT),
        name="mlstm_bidir",
    )(xm, xmc, conv_w, conv_b, wq, wk, wv, wgc, wgm, bg)


def _out_kernel(h_ref, mod_ref, ya_ref, yb_ref, wo_ref, g_ref, b_ref, o_ref, *, n_sub):
    sub = h_ref.shape[1] // n_sub
    gate = mod_ref[0][:, 2 * D_MODEL:3 * D_MODEL]

    def mix_stage(s):
        rows = slice(s * sub, (s + 1) * sub)
        return (_dot(ya_ref[0, rows, :], wo_ref[0:W_A, :])
                + _dot(yb_ref[0, rows, :], wo_ref[W_A:W_A + W_B, :]))

    def norm_stage(s, y):
        rows = slice(s * sub, (s + 1) * sub)
        o_ref[0, rows, :] = _layer_norm(ALPHA * h_ref[0, rows, :] + gate * y) * g_ref[...] + b_ref[...]

    y = mix_stage(0)
    for s in range(n_sub):
        y_next = mix_stage(s + 1) if s + 1 < n_sub else None
        norm_stage(s, y)
        y = y_next


def _out_call(h, mod3, ya, yb, wo, g, bb, *, tm, n_sub):
    b, t, _ = h.shape
    const = lambda shape: pl.BlockSpec(shape, lambda i, j: (0,) * len(shape))
    row = lambda w: pl.BlockSpec((1, tm, w), lambda i, j: (i, j, 0))
    return pl.pallas_call(
        functools.partial(_out_kernel, n_sub=n_sub),
        grid=(b, t // tm),
        in_specs=[row(D_MODEL), pl.BlockSpec((1, 1, 3 * D_MODEL), lambda i, j: (i, 0, 0)),
                  row(W_A), row(W_B),
                  const((W_A + W_B, D_MODEL)), const((1, D_MODEL)), const((1, D_MODEL))],
        out_specs=row(D_MODEL),
        out_shape=jax.ShapeDtypeStruct((b, t, D_MODEL), F32),
        compiler_params=pltpu.CompilerParams(dimension_semantics=("parallel", "parallel"),
                                             vmem_limit_bytes=VMEM_LIMIT),
        name="out_proj_ln",
    )(h, mod3, ya, yb, wo, g, bb)


def _rope_tables(seq, scale_keep, scale_rope):
    n_rows = seq // GRID_W
    rowp = np.repeat(np.arange(n_rows, dtype=np.float32), GRID_W)
    colp = np.tile(np.arange(GRID_W, dtype=np.float32), n_rows)
    inv = (np.float32(ROPE_BASE) ** (-np.arange(ROPE_FREQS, dtype=np.float32) / np.float32(ROPE_FREQS)))
    ang = np.stack([rowp[:, None] * inv, colp[:, None] * inv], axis=1).astype(np.float32)
    cos = np.broadcast_to(np.cos(ang)[:, :, None, :], (seq, 2, 2, ROPE_FREQS)).reshape(seq, DR_A)
    sin = np.broadcast_to(np.sin(ang)[:, :, None, :], (seq, 2, 2, ROPE_FREQS)).reshape(seq, DR_A)
    z32 = np.zeros((seq, DR_A), np.float32)
    t1 = np.concatenate([np.full((seq, DN_A), scale_keep, np.float32), z32, cos * scale_rope], axis=1)
    t2 = np.concatenate([np.zeros((seq, DN_A), np.float32), z32, sin * scale_rope], axis=1)
    return jnp.asarray(t1, F32), jnp.asarray(t2, F32)


def kernel(x, c, ctx, c_ctx, ln_in_g, ln_in_b, w_ada, b_ada, w_in, g_qa, w_qb, g_kva, w_kvb, conv_w, conv_b, w_mq, w_mk, w_mv, w_gate, b_gate, mh_g, skip, w_out, ln_g, ln_b):
    b, t, _ = x.shape
    l = 0
    r2 = lambda v: v.reshape(1, -1)

    blocks = lambda w: jnp.transpose(w[l], (1, 2, 0)).reshape(QKV_BS * QKV_BS, -1)
    wo = w_out[l].astype(BF16)

    sm_scale = (DN_A + DR_A) ** -0.5 * LOG2_E
    t1q, t2q = _rope_tables(t, sm_scale, sm_scale)
    t1k, t2k = _rope_tables(t, 0.0, 1.0)
    n_ctx = ctx.shape[1]

    mod3, wina, winr, wmq, wmk_s, wmv, wgc, wgm, bg, wq, wkn, wvt = _prep_call(
        c, r2(c_ctx), w_ada[l], r2(b_ada[l]), jnp.swapaxes(w_in[l], 0, 1), blocks(w_mq), blocks(w_mk),
        blocks(w_mv), jnp.swapaxes(w_gate[l], 0, 1), r2(b_gate[l]), w_qb[l], w_kvb[l])

    lng, lnb = r2(ln_in_g), r2(ln_in_b)
    shared = (wina, winr, r2(g_qa[l]), wq, r2(g_kva[l]), wkn, wvt)
    q, kl, vtl, xm, sza, som, szm, h = _proj_call(x, mod3, None, lng, lnb, *shared, t1q, t2q, t1k, t2k,
                                                  tm=1024, n_sub=2, latent=True)
    kc, vtc, xmc = _proj_call(ctx.reshape(1, b * n_ctx, D_MODEL), mod3, b, lng, lnb, *shared,
                              tm=min(1024, b * n_ctx), n_sub=2, latent=False)
    xmc = xmc.reshape(b, n_ctx, W_B)

    hs, xc = _mlstm_call(xm, xmc, conv_w[l], r2(conv_b[l]), wmq, wmk_s, wmv, wgc, wgm, bg)
    ya, yb = _attn_call(q, kc, kl, vtc, vtl, sza, hs, som, szm, xc, r2(mh_g[l]), r2(skip[l]), tq=512)
    return _out_call(h, mod3, ya, yb, wo, r2(ln_g[l]), r2(ln_b[l]), tm=1024, n_sub=4)
```

```python
import functools

import numpy as np
import jax
import jax.numpy as jnp
from jax import lax
from jax.experimental import pallas as pl
from jax.experimental.pallas import tpu as pltpu

F32 = jnp.float32
BF16 = jnp.bfloat16

D_MODEL = 1024
CTX_LEN = 256
GRID_W = 64
H_A, DN_A, DR_A, DV_A = 8, 64, 32, 64
W_A = H_A * DV_A
Q_LORA, KV_LORA = 256, 128
ROPE_FREQS = DR_A // 4
ROPE_BASE = 10000.0
H_B, DH_B = 4, 128
W_B = H_B * DH_B
QKV_BS = 4
DEPTH = 1
ALPHA = (2.0 * DEPTH) ** 0.25
LN_EPS = 1e-5
RMS_EPS = 1e-6
LOG2_E = 1.4426950408889634

LANES = 128
MXU_TILE = 256
SLAB = LANES
DV_EXT = DV_A + 16
ATTN_KEY_CHUNK = MXU_TILE
CHUNK = 256
N_HD = 2 * H_B
ST_ROWS = DH_B + 16
VMEM_LIMIT = 56 * 1024 * 1024

_NT = (((1,), (1,)), ((), ()))


def _dot(a, b):
    return jnp.dot(a, b, preferred_element_type=F32)


def _dot_nt(a, b):
    return lax.dot_general(a, b, _NT, preferred_element_type=F32)


def _layer_norm(x):
    mu = jnp.mean(x, axis=-1, keepdims=True)
    xc = x - mu
    var = jnp.mean(xc * xc, axis=-1, keepdims=True)
    return xc * lax.rsqrt(var + LN_EPS)


def _rms_norm(x, g):
    return (x * lax.rsqrt(jnp.mean(x * x, axis=-1, keepdims=True) + RMS_EPS)) * g


def _silu(x):
    return x * jax.nn.sigmoid(x)


def _log_sigmoid(x):
    return jnp.minimum(x, 0.0) - jnp.log1p(jnp.exp(-jnp.abs(x)))


def _ada_kernel(c_ref, cctx_ref, w_ref, b_ref, o_ref):
    first = lax.broadcasted_iota(jnp.int32, (8, D_MODEL), 0) == 0
    cc = jnp.concatenate([c_ref[...], jnp.where(first, cctx_ref[...], 0.0)], axis=0)
    a = _silu(cc).astype(BF16)
    mod = _dot(a, w_ref[...].astype(BF16)) + b_ref[...]
    for r in range(o_ref.shape[0]):
        o_ref[r] = mod[r:r + 1, :]


_S_KVA, _S_KR = Q_LORA + KV_LORA, Q_LORA + KV_LORA + DR_A


def _win_kernel(wt_ref, wina_ref, winr_ref):
    n_r = winr_ref.shape[1]
    for j in range(_S_KVA // LANES):
        wina_ref[:, j * LANES:(j + 1) * LANES] = wt_ref[j * LANES:(j + 1) * LANES, :].T.astype(BF16)
    kr = wt_ref[_S_KVA:_S_KR, :]
    a0, b0, a1, b1 = (kr[i * ROPE_FREQS:(i + 1) * ROPE_FREQS, :] for i in range(4))
    blk = jnp.concatenate([jnp.zeros((DN_A, D_MODEL), F32), -b0, a0, -b1, a1, kr], axis=0)
    wina_ref[:, _S_KVA:_S_KVA + LANES] = blk.T.astype(BF16)
    for j in range(n_r // MXU_TILE):
        rows = slice(_S_KR + j * MXU_TILE, _S_KR + (j + 1) * MXU_TILE)
        winr_ref[:, j * MXU_TILE:(j + 1) * MXU_TILE] = wt_ref[rows, :].T.astype(BF16)


def _rope_slab(s, t1, t2):
    return s * t1 + pltpu.roll(s, 32, 1) * t2


def _proj_kernel(x_ref, mod_ref, lng_ref, lnb_ref, wina_ref, winr_ref, gqa_ref, wq_ref, gkva_ref,
                 wkn_ref, wvt_ref, *refs, latent, n_sub):
    if latent:
        t1q_ref, t2q_ref, t1k_ref, t2k_ref, q_ref, k_ref, vt_ref, xm_ref, sza_ref, som_ref, szm_ref, h_ref = refs
    else:
        k_ref, vt_ref, xm_ref = refs
    sub = x_ref.shape[1] // n_sub
    mod = mod_ref[0]
    shift, scale1 = mod[:, 0:D_MODEL], 1.0 + mod[:, D_MODEL:2 * D_MODEL]
    ones_rows = (lax.broadcasted_iota(jnp.int32, (DV_EXT - DV_A, sub), 0) == 0).astype(BF16)

    def norm_stage(s):
        rows = slice(s * sub, (s + 1) * sub)
        h = _layer_norm(x_ref[0, rows, :]) * lng_ref[...] + lnb_ref[...]
        if latent:
            h_ref[0, rows, :] = h
        return (h * scale1 + shift).astype(BF16)

    def proj_stage(s, u):
        rows = slice(s * sub, (s + 1) * sub)
        p0 = _dot(u, wina_ref[...])
        d_xm = _dot(u, winr_ref[:, 512:1024])
        kvn = _rms_norm(p0[:, 256:384], gkva_ref[...]).astype(BF16)
        kn = _dot(kvn, wkn_ref[...])
        d_za = _dot(u, winr_ref[:, 0:512]) if latent else None
        if latent:
            krr = _rope_slab(p0[:, 384:512], t1k_ref[rows, :], t2k_ref[rows, :])
        else:
            rope_lanes = lax.broadcasted_iota(jnp.int32, (sub, SLAB), 1) >= SLAB - DR_A
            krr = jnp.where(rope_lanes, p0[:, 384:512], 0.0)
        for hh in range(H_A):
            k_ref[0, hh, rows, :] = (kn[:, hh * SLAB:(hh + 1) * SLAB] + krr).astype(BF16)
        vt = _dot_nt(wvt_ref[...], kvn).astype(BF16)
        for hh in range(H_A):
            vt_ref[0, hh, 0:DV_A, rows] = vt[hh * DV_A:(hh + 1) * DV_A, :]
            vt_ref[0, hh, DV_A:DV_EXT, rows] = ones_rows
        xm_ref[0, rows, :] = d_xm.astype(BF16)
        if latent:
            d_om = _dot(u, winr_ref[:, 1024:1536])
            qn = _rms_norm(p0[:, 0:256], gqa_ref[...]).astype(BF16)
            qs = _dot(qn, wq_ref[...])
            sza_ref[0, rows, :] = _silu(d_za).astype(BF16)
            d_zm = _dot(u, winr_ref[:, 1536:2048])
            t1q, t2q = t1q_ref[rows, :], t2q_ref[rows, :]
            for hh in range(H_A):
                q_ref[0, hh, rows, :] = _rope_slab(qs[:, hh * SLAB:(hh + 1) * SLAB], t1q, t2q).astype(BF16)
            som_ref[0, rows, :] = jax.nn.sigmoid(d_om).astype(BF16)
            szm_ref[0, rows, :] = _silu(d_zm).astype(BF16)

    u = norm_stage(0)
    for s in range(n_sub):
        u_next = norm_stage(s + 1) if s + 1 < n_sub else None
        proj_stage(s, u)
        u = u_next


def _proj_call(x, mod3, mod_row, lng, lnb, wina, winr, gqa, wq, gkva, wkn, wvt, *tables, tm, n_sub, latent):
    b, t, _ = x.shape
    const = lambda shape: pl.BlockSpec(shape, lambda i, j: (0,) * len(shape))
    tab = pl.BlockSpec((tm, SLAB), lambda i, j: (j, 0))
    assert len(tables) == (4 if latent else 0)
    row = lambda w: pl.BlockSpec((1, tm, w), lambda i, j: (i, j, 0))
    if mod_row is None:
        mod_spec = pl.BlockSpec((1, 1, 3 * D_MODEL), lambda i, j: (i, 0, 0))
    else:
        mod_spec = pl.BlockSpec((1, 1, 3 * D_MODEL), lambda i, j: (mod_row, 0, 0))
    in_specs = [row(D_MODEL), mod_spec, const((1, D_MODEL)), const((1, D_MODEL)),
                const(wina.shape), const(winr.shape), const((1, Q_LORA)), const((Q_LORA, H_A * SLAB)),
                const((1, KV_LORA)), const((KV_LORA, H_A * SLAB)), const((W_A, KV_LORA))]
    in_specs += [tab] * len(tables)
    k_spec = pl.BlockSpec((1, H_A, tm, SLAB), lambda i, j: (i, 0, j, 0))
    vt_spec = pl.BlockSpec((1, H_A, DV_EXT, tm), lambda i, j: (i, 0, 0, j))
    k_shape = jax.ShapeDtypeStruct((b, H_A, t, SLAB), BF16)
    vt_shape = jax.ShapeDtypeStruct((b, H_A, DV_EXT, t), BF16)
    half = jax.ShapeDtypeStruct((b, t, W_B), BF16)
    if latent:
        out_specs = [k_spec, k_spec, vt_spec, row(W_B), row(W_A), row(W_B), row(W_B), row(D_MODEL)]
        out_shape = [k_shape, k_shape, vt_shape, half, half, half, half,
                     jax.ShapeDtypeStruct((b, t, D_MODEL), F32)]
    else:
        out_specs = [k_spec, vt_spec, row(W_B)]
        out_shape = [k_shape, vt_shape, half]
    return pl.pallas_call(
        functools.partial(_proj_kernel, latent=latent, n_sub=n_sub),
        grid=(b, t // tm),
        in_specs=in_specs, out_specs=out_specs, out_shape=out_shape,
        compiler_params=pltpu.CompilerParams(dimension_semantics=("parallel", "parallel"),
                                             vmem_limit_bytes=VMEM_LIMIT),
        name="in_proj_latent" if latent else "in_proj_ctx",
    )(x, mod3, lng, lnb, wina, winr, gqa, wq, gkva, wkn, wvt, *tables)


def _attn_kernel(q_ref, kc_ref, kl_ref, vtc_ref, vtl_ref, sza_ref, hs_ref, som_ref, szm_ref, xc_ref,
                 mhg_ref, skip_ref, o_ref, yb_ref, s_buf, ot_s):
    n_ctx, t = kc_ref.shape[2], kl_ref.shape[2]
    kc = ATTN_KEY_CHUNK

    def scores(h, slot):
        qh = q_ref[0, h]
        sc = _dot_nt(kc_ref[0, h], qh)
        sk = _dot_nt(kl_ref[0, h], qh)
        s_buf[slot, 0:n_ctx, :] = sc
        s_buf[slot, n_ctx:n_ctx + t, :] = sk
        return jnp.maximum(jnp.max(sc, axis=0, keepdims=True), jnp.max(sk, axis=0, keepdims=True))

    def values(h, slot, m):
        chunks = [(vtc_ref, c * kc, c * kc) for c in range(n_ctx // kc)]
        chunks += [(vtl_ref, c * kc, n_ctx + c * kc) for c in range(t // kc)]
        acc = None
        for vref, v0, s0 in chunks:
            p = jnp.exp2(s_buf[slot, s0:s0 + kc, :] - m).astype(BF16)
            d = _dot(vref[0, h, :, v0:v0 + kc], p)
            acc = d if acc is None else acc + d
        rows = pl.ds(pl.multiple_of(h * DV_A, DV_A), DV_A)
        ot_s[rows, :] = acc[0:DV_A, :] / acc[DV_A:DV_A + 1, :]

    def cell_out(pc):
        rows = pl.ds(pl.multiple_of(pc * LANES, LANES), LANES)
        yb_ref[0, rows, :] = _mlstm_out(hs_ref[0, pc], som_ref[0, rows, :], szm_ref[0, rows, :],
                                        xc_ref[0, rows, :], mhg_ref[...], skip_ref[...])

    def body(i, m_even):
        h = 2 * i
        m_odd = scores(h + 1, 1)
        cell_out(i + 1)
        values(h, 0, m_even)
        m_even = scores(h + 2, 0)
        values(h + 1, 1, m_odd)
        return m_even

    assert hs_ref.shape[1] == H_A // 2
    m_first = scores(0, 0)
    cell_out(0)
    m_even = lax.fori_loop(0, H_A // 2 - 1, body, m_first)
    m_odd = scores(H_A - 1, 1)
    values(H_A - 2, 0, m_even)
    values(H_A - 1, 1, m_odd)
    o_ref[0] = (ot_s[...].T * sza_ref[0].astype(F32)).astype(BF16)


def _attn_call(q, kc, kl, vtc, vtl, sza, hs, som, szm, xc, mhg, skip, *, tq):
    b, _, t, _ = q.shape
    n_ctx = kc.shape[2] // b
    head_blk = lambda n, w: pl.BlockSpec((1, H_A, n, w), lambda i, j: (i, 0, 0, 0))
    const = lambda shape: pl.BlockSpec(shape, lambda i, j: (0,) * len(shape))
    row = lambda w: pl.BlockSpec((1, tq, w), lambda i, j: (i, j, 0))
    half = jax.ShapeDtypeStruct((b, t, W_A), BF16)
    return pl.pallas_call(
        _attn_kernel,
        grid=(b, t // tq),
        in_specs=[pl.BlockSpec((1, H_A, tq, SLAB), lambda i, j: (i, 0, j, 0)),
                  pl.BlockSpec((1, H_A, n_ctx, SLAB), lambda i, j: (0, 0, i, 0)), head_blk(t, SLAB),
                  pl.BlockSpec((1, H_A, DV_EXT, n_ctx), lambda i, j: (0, 0, 0, i)), head_blk(DV_EXT, t),
                  row(W_A),
                  pl.BlockSpec((1, tq // LANES, W_B, LANES), lambda i, j: (i, j, 0, 0)),
                  row(W_B), row(W_B), row(W_B), const((1, W_B)), const((1, W_B))],
        out_specs=[row(W_A), row(W_B)],
        out_shape=[half, jax.ShapeDtypeStruct((b, t, W_B), BF16)],
        scratch_shapes=[pltpu.VMEM((2, n_ctx + t, tq), F32),
                        pltpu.VMEM((W_A, tq), F32)],
        compiler_params=pltpu.CompilerParams(dimension_semantics=("parallel", "parallel"),
                                             vmem_limit_bytes=VMEM_LIMIT),
        name="mla_attention",
    )(q, kc, kl, vtc, vtl, sza, hs, som, szm, xc, mhg, skip)


def _onehot(cond):
    return cond.astype(F32).astype(BF16)


def _block_diag(a):
    bs, nb = QKV_BS, W_B // QKV_BS
    lb, sq = bs.bit_length() - 1, bs * bs
    assert bs == 1 << lb
    iota = lambda shape, d: lax.broadcasted_iota(jnp.int32, shape, d)
    spread = _onehot(iota((nb, W_B), 1) >> lb == iota((nb, W_B), 0))
    b = _dot(a, spread)
    col_o = iota((sq, W_B), 1) & (bs - 1)
    bm = jnp.concatenate([jnp.where(col_o == o, b, 0.0) for o in range(bs)], axis=0).astype(BF16)
    j = iota((W_B, bs * sq), 1)
    pick = _onehot((j & (sq - 1)) == ((iota((W_B, bs * sq), 0) & (bs - 1)) << lb) + (j >> (2 * lb)))
    same_block = iota((W_B, W_B), 0) >> lb == iota((W_B, W_B), 1) >> lb
    return jnp.where(same_block, _dot(pick, bm), 0.0).astype(BF16)


def _mlstm_weight_kernel(aq_ref, ak_ref, av_ref, wgt_ref, bgate_ref,
                         wmq_ref, wmk_ref, wmv_ref, wgc_ref, wgm_ref, bg_ref):
    a_q, a_k, a_v = aq_ref[...], ak_ref[...], av_ref[...]
    wq, wk, wv = (_block_diag(a.astype(BF16)) for a in (a_q, a_k, a_v))
    wmq_ref[...] = wq
    wmk_ref[...] = _block_diag((a_k * (DH_B ** -0.5)).astype(BF16))
    wmv_ref[...] = wv
    dst, src = (lax.broadcasted_iota(jnp.int32, (LANES, 2 * N_HD), d) for d in (0, 1))
    grp = dst >> (H_B.bit_length() - 1)
    src_of_dst = (dst & (H_B - 1)) + H_B * jnp.where(grp == 1, 2, jnp.where(grp == 2, 1, grp))
    perm = _onehot((dst < 2 * N_HD) & (src == src_of_dst))
    wg_t = _dot(perm, wgt_ref[...].astype(BF16)).astype(BF16)
    wgc_ref[...] = (_dot_nt(wq, wg_t[:, 0:W_B]) + _dot_nt(wk, wg_t[:, W_B:2 * W_B])).astype(BF16)
    wgm_ref[...] = _dot_nt(wv, wg_t[:, 2 * W_B:3 * W_B]).astype(BF16)
    bg_ref[...] = jnp.zeros(bg_ref.shape, F32)
    for d, s in enumerate((0, 2, 1, 3)):
        bg_ref[:, d * H_B:(d + 1) * H_B] = bgate_ref[:, s * H_B:(s + 1) * H_B]


def _qb_kernel(wqb_ref, wq_ref):
    wt = wqb_ref[...].T
    hw = DN_A + DR_A
    for hh in range(H_A):
        rope = wt[hh * hw + DN_A:(hh + 1) * hw, :]
        a0, b0, a1, b1 = (rope[i * ROPE_FREQS:(i + 1) * ROPE_FREQS, :] for i in range(4))
        slab = jnp.concatenate([wt[hh * hw:hh * hw + DN_A, :], -b0, a0, -b1, a1, rope], axis=0)
        wq_ref[:, hh * SLAB:(hh + 1) * SLAB] = slab.T.astype(BF16)


def _kvb_kernel(wkvb_ref, wkn_ref, wvt_ref):
    w = wkvb_ref[...]
    assert DN_A + DV_A == SLAB
    lane = lax.broadcasted_iota(jnp.int32, w.shape, 1)
    wkn_ref[...] = jnp.where((lane & (SLAB - 1)) < DN_A, w, 0.0).astype(BF16)
    wt = w.T
    for hh in range(H_A):
        wvt_ref[hh * DV_A:(hh + 1) * DV_A, :] = wt[hh * SLAB + DN_A:(hh + 1) * SLAB, :].astype(BF16)


def _prep_kernel(c_ref, cctx_ref, wada_ref, bada_ref, wt_ref, aq_ref, ak_ref, av_ref, wgt_ref, bgate_ref,
                 wqb_ref, wkvb_ref,
                 mod_ref, wina_ref, winr_ref, wmq_ref, wmk_ref, wmv_ref, wgc_ref, wgm_ref, bg_ref,
                 wq_ref, wkn_ref, wvt_ref):
    _ada_kernel(c_ref, cctx_ref, wada_ref, bada_ref, mod_ref)
    _win_kernel(wt_ref, wina_ref, winr_ref)
    _mlstm_weight_kernel(aq_ref, ak_ref, av_ref, wgt_ref, bgate_ref,
                         wmq_ref, wmk_ref, wmv_ref, wgc_ref, wgm_ref, bg_ref)
    _qb_kernel(wqb_ref, wq_ref)
    _kvb_kernel(wkvb_ref, wkn_ref, wvt_ref)


def _prep_call(c, c_ctx, w_ada, b_ada, wt, aq, ak, av, wgt, b_gate, w_qb, w_kvb):
    n_in, d = wt.shape
    assert c.shape[0] % 8 == 0 and wgt.shape == (2 * N_HD, 3 * W_B)
    assert aq.shape == ak.shape == av.shape == (QKV_BS * QKV_BS, W_B // QKV_BS)
    bf = lambda *shape: jax.ShapeDtypeStruct(shape, BF16)
    return pl.pallas_call(
        _prep_kernel,
        out_shape=[jax.ShapeDtypeStruct((c.shape[0] + 1, 1, w_ada.shape[1]), F32),
                   bf(d, _S_KVA + LANES), bf(d, n_in - _S_KR),
                   bf(W_B, W_B), bf(W_B, W_B), bf(W_B, W_B), bf(W_B, LANES), bf(W_B, LANES),
                   jax.ShapeDtypeStruct((1, LANES), F32),
                   bf(Q_LORA, H_A * SLAB), bf(KV_LORA, H_A * SLAB), bf(W_A, KV_LORA)],
        compiler_params=pltpu.CompilerParams(vmem_limit_bytes=VMEM_LIMIT),
        name="param_prep",
    )(c, c_ctx, w_ada, b_ada, wt, aq, ak, av, wgt, b_gate, w_qb, w_kvb)


_PAD = 8


def _lane_scans(rows_per_chunk, combine, fill, out):
    npc = CHUNK // LANES
    flat = [x[:, i * LANES:(i + 1) * LANES] for x in rows_per_chunk for i in range(npc)]
    lane = lax.broadcasted_iota(jnp.int32, flat[0].shape, 1)
    pre, suf, sh = list(flat), list(flat), 1
    while sh < LANES:
        pre = [combine(p, jnp.where(lane >= sh, pltpu.roll(p, sh, 1), fill)) for p in pre]
        suf = [combine(s, jnp.where(lane < LANES - sh, pltpu.roll(s, LANES - sh, 1), fill)) for s in suf]
        sh *= 2
        yield
    for c in range(len(rows_per_chunk)):
        p, s = pre[c * npc:(c + 1) * npc], suf[c * npc:(c + 1) * npc]
        tot = [x[:, LANES - 1:LANES] for x in p]
        run = None
        for i in range(npc):
            if run is not None:
                p[i] = combine(p[i], run)
            run = tot[i] if run is None else combine(run, tot[i])
        run = None
        for i in reversed(range(npc)):
            if run is not None:
                s[i] = combine(s[i], run)
            run = tot[i] if run is None else combine(run, tot[i])
        out.append((jnp.concatenate(p, axis=1), jnp.concatenate(s, axis=1)))


def _gate_tables(gts, dests):
    L = CHUNK
    fwd_rows = lax.broadcasted_iota(jnp.int32, (N_HD, L), 0) < H_B
    lis = [gt[0:N_HD, :] for gt in gts]
    sums = []
    yield from _lane_scans([_log_sigmoid(gt[N_HD:2 * N_HD, :]) for gt in gts], jnp.add, 0.0, sums)
    cums = [jnp.where(fwd_rows, ps, ss) for ps, ss in sums]
    rs = [li - cum for li, cum in zip(lis, cums)]
    maxs = []
    yield from _lane_scans(rs, jnp.maximum, -jnp.inf, maxs)
    for (grow_ref, c, gcol_ref, r0), cum, r, (pm, sm) in zip(dests, cums, rs, maxs):
        grow_ref[c] = jnp.concatenate([cum, r, jnp.where(fwd_rows, pm, sm)], axis=0)
        gcol_ref[r0:r0 + L, :] = jnp.concatenate([r, jnp.zeros((LANES - N_HD, L), F32)], axis=0).T


def _fill_pad(pad_s, xm_ref, n):
    zrow = jnp.zeros((_PAD, W_B), F32)
    pad_s[0:_PAD, :] = zrow
    pad_s[_PAD:_PAD + n, :] = xm_ref[...].astype(F32)
    pad_s[_PAD + n:2 * _PAD + n, :] = zrow


def _conv_gate_stage(c, pad_s, xm_ref, cw_ref, cb_ref, wgc_ref, wgm_ref, bg_ref, xc_out, xcb_s):
    L = CHUNK
    rows = slice(c * L, (c + 1) * L)
    r0 = c * L + _PAD
    row_id = lax.broadcasted_iota(jnp.int32, (L, W_B), 0)
    xcur = pad_s[r0:r0 + L, :]
    xprev = jnp.where(row_id == 0, pad_s[r0 - 1:r0, :], pltpu.roll(xcur, 1, 0))
    xnext = jnp.where(row_id == L - 1, pad_s[r0 + L:r0 + L + 1, :], pltpu.roll(xcur, L - 1, 0))
    pre = cb_ref[...] + xprev * cw_ref[0:1, :] + xcur * cw_ref[1:2, :] + xnext * cw_ref[2:3, :]
    xc = _silu(pre)
    xcb = xc.astype(BF16)
    if xc_out is not None:
        xc_out[rows, :] = xcb
    xcb_s[rows, :] = xcb
    g = _dot(xcb, wgc_ref[...]) + _dot(xm_ref[rows, :], wgm_ref[...]) + bg_ref[...]
    return g.T[0:2 * N_HD, :]


def _headwise_stage(c, xm_ref, xcb_s, wq_ref, wk_ref, wv_ref, q_s, k_s, vt_s):
    L = CHUNK
    rows = slice(c * L, (c + 1) * L)
    for hf in range(W_B // MXU_TILE):
        cs = slice(hf * MXU_TILE, (hf + 1) * MXU_TILE)
        xcb = xcb_s[rows, cs]
        if q_s is not None:
            q_s[rows, cs] = _dot(xcb, wq_ref[cs, cs]).astype(BF16)
        k_s[rows, cs] = _dot(xcb, wk_ref[cs, cs]).astype(BF16)
        vt_s[c, cs, :] = _dot(xm_ref[rows, cs], wv_ref[cs, cs]).T.astype(BF16)


_TAB_ROWS = 5 * N_HD


def _chain_tables(growc_s, grow_s, tab_s, nc):
    L = CHUNK
    fwd_rows = lax.broadcasted_iota(jnp.int32, (N_HD, L), 0) < H_B
    gc = growc_s[0]
    steps = [(gc, gc)] + [(grow_s[i], grow_s[nc - 1 - i]) for i in range(nc)]
    parts = []
    for g_f, g_b in steps:
        pick = lambda a: jnp.where(fwd_rows, g_f[a * N_HD:(a + 1) * N_HD, :], g_b[a * N_HD:(a + 1) * N_HD, :])
        cum, r, pm = pick(0), pick(1), pick(2)
        b_end = jnp.where(fwd_rows[:, 0:1], cum[:, L - 1:L], cum[:, 0:1])
        w = r + b_end
        parts.append((cum, pm, b_end, w, jnp.max(w, axis=1, keepdims=True)))
    m0 = jnp.zeros((N_HD, 1), F32)
    for idx, (cum, pm, b_end, w, wmax) in enumerate(parts):
        m_new = jnp.maximum(b_end + m0, wmax)
        mu = jnp.maximum(m0, pm)
        decay = jnp.broadcast_to(jnp.exp(b_end + m0 - m_new), (N_HD, L))
        tab_s[idx] = jnp.concatenate([mu, jnp.exp(m0 - mu), jnp.exp(-(cum + mu)),
                                      jnp.exp(w - m_new), decay], axis=0)
        m0 = m_new


def _tab(tab, a, hd):
    return tab[a * N_HD + hd:a * N_HD + hd + 1, :]


def _state_update(st_ref, hd, vt_ext, k_c, ws_row, decay_hd):
    vw = (vt_ext.astype(F32) * ws_row).astype(BF16)
    st_ref[hd] = decay_hd * st_ref[hd] + _dot(vw, k_c)


def _mlstm_kernel(xm_ref, xmc_ref, cw_ref, cb_ref, wq_ref, wk_ref, wv_ref, wgc_ref, wgm_ref, bg_ref,
                  hs_ref, xc_ref,
                  pad_s, xcb_s, q_s, k_s, vt_s, gcol_s, grow_s,
                  padc_s, xcbc_s, kc_s, vtc_s, gcolc_s, growc_s, tab_s, st_s):
    L = CHUNK
    t = xm_ref.shape[1]
    nc = t // L
    row_i = lax.broadcasted_iota(jnp.int32, (L, L), 0)
    col_i = lax.broadcasted_iota(jnp.int32, (L, L), 1)
    tri = (row_i <= col_i, row_i >= col_i)
    ones_rows = (lax.broadcasted_iota(jnp.int32, (ST_ROWS - DH_B, L), 0) == 0).astype(BF16)
    hsl = lambda hh: slice(hh * DH_B, (hh + 1) * DH_B)
    xm, xmc = xm_ref.at[0], xmc_ref.at[0]
    gate_w = (cw_ref, cb_ref, wgc_ref, wgm_ref, bg_ref)

    _fill_pad(padc_s, xmc, CTX_LEN)
    _fill_pad(pad_s, xm, t)
    gts = [_conv_gate_stage(0, padc_s, xmc, *gate_w, None, xcbc_s)]
    gts += [_conv_gate_stage(c, pad_s, xm, *gate_w, xc_ref.at[0], xcb_s) for c in range(nc)]
    dests = [(growc_s, 0, gcolc_s, 0)] + [(grow_s, c, gcol_s, c * L) for c in range(nc)]
    tables = _gate_tables(gts, dests)
    _headwise_stage(0, xmc, xcbc_s, wq_ref, wk_ref, wv_ref, None, kc_s, vtc_s)
    for c in range(nc):
        next(tables, None)
        next(tables, None)
        _headwise_stage(c, xm, xcb_s, wq_ref, wk_ref, wv_ref, q_s, k_s, vt_s)
    for _ in tables:
        pass

    _chain_tables(growc_s, grow_s, tab_s, nc)
    st_s[...] = jnp.zeros(st_s.shape, F32)
    tab = tab_s[0]
    for hd in range(N_HD):
        hh = hd % H_B
        vt_ext = jnp.concatenate([vtc_s[0, hsl(hh), :], ones_rows], axis=0)
        _state_update(st_s, hd, vt_ext, kc_s[:, hsl(hh)], _tab(tab, 3, hd), _tab(tab, 4, hd)[:, 0:DH_B])

    def body(i, carry, *, accumulate):
        ci = (i, nc - 1 - i)
        rows = tuple(pl.ds(pl.multiple_of(c * L, L), L) for c in ci)
        tab = tab_s[i + 1]
        gcols = tuple(gcol_s[r, :] for r in rows)
        live = {}

        def stage_scores(hd):
            d, hh = hd // H_B, hd % H_B
            k_c = k_s[rows[d], hsl(hh)]
            vt_ext = jnp.concatenate([vt_s[ci[d], hsl(hh), :], ones_rows], axis=0)
            lhs = jnp.concatenate([k_c, st_s[hd].astype(BF16)], axis=0)
            live[hd] = (k_c, vt_ext, _dot_nt(lhs, q_s[rows[d], hsl(hh)]))

        def stage_gate(hd):
            d = hd // H_B
            k_c, vt_ext, res = live[hd]
            rcol = gcols[d][:, hd:hd + 1]
            e = jnp.exp(jnp.where(tri[d], rcol - _tab(tab, 0, hd), -jnp.inf))
            p = (res[0:L, :] * e).astype(BF16)
            _state_update(st_s, hd, vt_ext, k_c, _tab(tab, 3, hd), _tab(tab, 4, hd)[:, 0:DH_B])
            live[hd] = (vt_ext, res[L:L + ST_ROWS, :], p)

        def stage_out(hd):
            d, hh = hd // H_B, hd % H_B
            vt_ext, inter, p = live.pop(hd)
            tot = _tab(tab, 1, hd) * inter + _dot(vt_ext, p)
            den = tot[DH_B:DH_B + 1, :]
            h_t = tot[0:DH_B, :] / jnp.maximum(jnp.abs(den), _tab(tab, 2, hd))
            for pc in range(L // LANES):
                piece = h_t[:, pc * LANES:(pc + 1) * LANES]
                if accumulate:
                    hs_ref[0, ci[d] * (L // LANES) + pc, hsl(hh), :] += piece
                else:
                    hs_ref[0, ci[d] * (L // LANES) + pc, hsl(hh), :] = piece

        stages = (stage_scores, stage_gate, stage_out)
        for step in range(N_HD + len(stages) - 1):
            for si in range(len(stages)):
                if 0 <= step - si < N_HD:
                    stages[si](step - si)
        return carry

    lax.fori_loop(0, nc // 2, functools.partial(body, accumulate=False), 0)
    lax.fori_loop(nc // 2, nc, functools.partial(body, accumulate=True), 0)


def _mlstm_out(hs_t, som, szm, xc, mhg, skip):
    hsum = hs_t.T * som.astype(F32)
    parts = [_layer_norm(hsum[:, hh * DH_B:(hh + 1) * DH_B]) for hh in range(H_B)]
    hb = jnp.concatenate(parts, axis=1) * mhg + skip * xc.astype(F32)
    return (hb * szm.astype(F32)).astype(BF16)


def _mlstm_call(xm, xmc, conv_w, conv_b, wq, wk, wv, wgc, wgm, bg):
    b, t, _ = xm.shape
    nc = t // CHUNK
    assert nc % 2 == 0
    seq = lambda n: pl.BlockSpec((1, n, W_B), lambda i: (i, 0, 0))
    const = lambda shape: pl.BlockSpec(shape, lambda i: (0,) * len(shape))
    scratch = [
        pltpu.VMEM((t + 2 * _PAD, W_B), F32),
        pltpu.VMEM((t, W_B), BF16),
        pltpu.VMEM((t, W_B), BF16),
        pltpu.VMEM((t, W_B), BF16),
        pltpu.VMEM((nc, W_B, CHUNK), BF16),
        pltpu.VMEM((t, LANES), F32),
        pltpu.VMEM((nc, 3 * N_HD, CHUNK), F32),
        pltpu.VMEM((CTX_LEN + 2 * _PAD, W_B), F32),
        pltpu.VMEM((CTX_LEN, W_B), BF16),
        pltpu.VMEM((CTX_LEN, W_B), BF16),
        pltpu.VMEM((1, W_B, CHUNK), BF16),
        pltpu.VMEM((CTX_LEN, LANES), F32),
        pltpu.VMEM((1, 3 * N_HD, CHUNK), F32),
        pltpu.VMEM((nc + 1, _TAB_ROWS, CHUNK), F32),
        pltpu.VMEM((N_HD, ST_ROWS, DH_B), F32),
    ]
    return pl.pallas_call(
        _mlstm_kernel,
        grid=(b,),
        in_specs=[seq(t), seq(CTX_LEN), const((3, W_B)), const((1, W_B)),
                  const((W_B, W_B)), const((W_B, W_B)), const((W_B, W_B)),
                  const((W_B, LANES)), const((W_B, LANES)), const((1, LANES))],
        out_specs=[pl.BlockSpec((1, t // LANES, W_B, LANES), lambda i: (i, 0, 0, 0)), seq(t)],
        out_shape=[jax.ShapeDtypeStruct((b, t // LANES, W_B, LANES), F32),
                   jax.ShapeDtypeStruct((b, t, W_B), BF16)],
        scratch_shapes=scratch,
        compiler_params=pltpu.CompilerParams(dimension_semantics=("arbitrary",),
                                             vmem_limit_bytes=VMEM_LIMIT),
        name="mlstm_bidir",
    )(xm, xmc, conv_w, conv_b, wq, wk, wv, wgc, wgm, bg)


def _out_kernel(h_ref, mod_ref, ya_ref, yb_ref, wo_ref, g_ref, b_ref, o_ref, *, n_sub):
    sub = h_ref.shape[1] // n_sub
    gate = mod_ref[0][:, 2 * D_MODEL:3 * D_MODEL]

    def mix_stage(s):
        rows = slice(s * sub, (s + 1) * sub)
        return (_dot(ya_ref[0, rows, :], wo_ref[0:W_A, :])
                + _dot(yb_ref[0, rows, :], wo_ref[W_A:W_A + W_B, :]))

    def norm_stage(s, y):
        rows = slice(s * sub, (s + 1) * sub)
        o_ref[0, rows, :] = _layer_norm(ALPHA * h_ref[0, rows, :] + gate * y) * g_ref[...] + b_ref[...]

    y = mix_stage(0)
    for s in range(n_sub):
        y_next = mix_stage(s + 1) if s + 1 < n_sub else None
        norm_stage(s, y)
        y = y_next


def _out_call(h, mod3, ya, yb, wo, g, bb, *, tm, n_sub):
    b, t, _ = h.shape
    const = lambda shape: pl.BlockSpec(shape, lambda i, j: (0,) * len(shape))
    row = lambda w: pl.BlockSpec((1, tm, w), lambda i, j: (i, j, 0))
    return pl.pallas_call(
        functools.partial(_out_kernel, n_sub=n_sub),
        grid=(b, t // tm),
        in_specs=[row(D_MODEL), pl.BlockSpec((1, 1, 3 * D_MODEL), lambda i, j: (i, 0, 0)),
                  row(W_A), row(W_B),
                  const((W_A + W_B, D_MODEL)), const((1, D_MODEL)), const((1, D_MODEL))],
        out_specs=row(D_MODEL),
        out_shape=jax.ShapeDtypeStruct((b, t, D_MODEL), F32),
        compiler_params=pltpu.CompilerParams(dimension_semantics=("parallel", "parallel"),
                                             vmem_limit_bytes=VMEM_LIMIT),
        name="out_proj_ln",
    )(h, mod3, ya, yb, wo, g, bb)


def _rope_tables(seq, scale_keep, scale_rope):
    n_rows = seq // GRID_W
    rowp = np.repeat(np.arange(n_rows, dtype=np.float32), GRID_W)
    colp = np.tile(np.arange(GRID_W, dtype=np.float32), n_rows)
    inv = (np.float32(ROPE_BASE) ** (-np.arange(ROPE_FREQS, dtype=np.float32) / np.float32(ROPE_FREQS)))
    ang = np.stack([rowp[:, None] * inv, colp[:, None] * inv], axis=1).astype(np.float32)
    cos = np.broadcast_to(np.cos(ang)[:, :, None, :], (seq, 2, 2, ROPE_FREQS)).reshape(seq, DR_A)
    sin = np.broadcast_to(np.sin(ang)[:, :, None, :], (seq, 2, 2, ROPE_FREQS)).reshape(seq, DR_A)
    z32 = np.zeros((seq, DR_A), np.float32)
    t1 = np.concatenate([np.full((seq, DN_A), scale_keep, np.float32), z32, cos * scale_rope], axis=1)
    t2 = np.concatenate([np.zeros((seq, DN_A), np.float32), z32, sin * scale_rope], axis=1)
    return jnp.asarray(t1, F32), jnp.asarray(t2, F32)


def kernel(x, c, ctx, c_ctx, ln_in_g, ln_in_b, w_ada, b_ada, w_in, g_qa, w_qb, g_kva, w_kvb, conv_w, conv_b, w_mq, w_mk, w_mv, w_gate, b_gate, mh_g, skip, w_out, ln_g, ln_b):
    b, t, _ = x.shape
    l = 0
    r2 = lambda v: v.reshape(1, -1)

    blocks = lambda w: jnp.transpose(w[l], (1, 2, 0)).reshape(QKV_BS * QKV_BS, -1)
    wo = w_out[l].astype(BF16)

    sm_scale = (DN_A + DR_A) ** -0.5 * LOG2_E
    t1q, t2q = _rope_tables(t, sm_scale, sm_scale)
    t1k, t2k = _rope_tables(t, 0.0, 1.0)
    n_ctx = ctx.shape[1]

    mod3, wina, winr, wmq, wmk_s, wmv, wgc, wgm, bg, wq, wkn, wvt = _prep_call(
        c, r2(c_ctx), w_ada[l], r2(b_ada[l]), jnp.swapaxes(w_in[l], 0, 1), blocks(w_mq), blocks(w_mk),
        blocks(w_mv), jnp.swapaxes(w_gate[l], 0, 1), r2(b_gate[l]), w_qb[l], w_kvb[l])

    lng, lnb = r2(ln_in_g), r2(ln_in_b)
    shared = (wina, winr, r2(g_qa[l]), wq, r2(g_kva[l]), wkn, wvt)
    q, kl, vtl, xm, sza, som, szm, h = _proj_call(x, mod3, None, lng, lnb, *shared, t1q, t2q, t1k, t2k,
                                                  tm=1024, n_sub=4, latent=True)
    kc, vtc, xmc = _proj_call(ctx.reshape(1, b * n_ctx, D_MODEL), mod3, b, lng, lnb, *shared,
                              tm=min(1024, b * n_ctx), n_sub=2, latent=False)
    xmc = xmc.reshape(b, n_ctx, W_B)

    hs, xc = _mlstm_call(xm, xmc, conv_w[l], r2(conv_b[l]), wmq, wmk_s, wmv, wgc, wgm, bg)
    ya, yb = _attn_call(q, kc, kl, vtc, vtl, sza, hs, som, szm, xc, r2(mh_g[l]), r2(skip[l]), tq=512)
    return _out_call(h, mod3, ya, yb, wo, r2(ln_g[l]), r2(ln_b[l]), tm=1024, n_sub=4)
```

```python
import functools

import numpy as np
import jax
import jax.numpy as jnp
from jax import lax
from jax.experimental import pallas as pl
from jax.experimental.pallas import tpu as pltpu

F32 = jnp.float32
BF16 = jnp.bfloat16

D_MODEL = 1024
CTX_LEN = 256
GRID_W = 64
H_A, DN_A, DR_A, DV_A = 8, 64, 32, 64
W_A = H_A * DV_A
Q_LORA, KV_LORA = 256, 128
ROPE_FREQS = DR_A // 4
ROPE_BASE = 10000.0
H_B, DH_B = 4, 128
W_B = H_B * DH_B
QKV_BS = 4
DEPTH = 1
ALPHA = (2.0 * DEPTH) ** 0.25
LN_EPS = 1e-5
RMS_EPS = 1e-6
LOG2_E = 1.4426950408889634

LANES = 128
SUBLANES = 8
MXU_TILE = 256
SLAB = LANES
DV_EXT = DV_A + 16
ATTN_KEY_CHUNK = MXU_TILE
CHUNK = 256
N_HD = 2 * H_B
ST_ROWS = DH_B + 16
VMEM_LIMIT = 56 * 1024 * 1024

_NT = (((1,), (1,)), ((), ()))


def _dot(a, b):
    return jnp.dot(a, b, preferred_element_type=F32)


def _dot_nt(a, b):
    return lax.dot_general(a, b, _NT, preferred_element_type=F32)


def _layer_norm(x):
    mu = jnp.mean(x, axis=-1, keepdims=True)
    xc = x - mu
    var = jnp.mean(xc * xc, axis=-1, keepdims=True)
    return xc * lax.rsqrt(var + LN_EPS)


def _rms_norm(x, g):
    return (x * lax.rsqrt(jnp.mean(x * x, axis=-1, keepdims=True) + RMS_EPS)) * g


def _silu(x):
    return x * jax.nn.sigmoid(x)


def _log_sigmoid(x):
    return jnp.minimum(x, 0.0) - jnp.log1p(jnp.exp(-jnp.abs(x)))


def _ada_kernel(c_ref, cctx_ref, w_ref, b_ref, o_ref):
    first = lax.broadcasted_iota(jnp.int32, (SUBLANES, D_MODEL), 0) == 0
    cc = jnp.concatenate([c_ref[...], jnp.where(first, cctx_ref[...], 0.0)], axis=0)
    a = _silu(cc).astype(BF16)
    mod = _dot(a, w_ref[...].astype(BF16)) + b_ref[...]
    for r in range(o_ref.shape[0]):
        o_ref[r] = mod[r:r + 1, :]


_S_KVA, _S_KR = Q_LORA + KV_LORA, Q_LORA + KV_LORA + DR_A


def _win_kernel(wt_ref, wina_ref, winr_ref):
    n_r = winr_ref.shape[1]
    for j in range(_S_KVA // LANES):
        wina_ref[:, j * LANES:(j + 1) * LANES] = wt_ref[j * LANES:(j + 1) * LANES, :].T.astype(BF16)
    kr = wt_ref[_S_KVA:_S_KR, :]
    a0, b0, a1, b1 = (kr[i * ROPE_FREQS:(i + 1) * ROPE_FREQS, :] for i in range(4))
    blk = jnp.concatenate([jnp.zeros((DN_A, D_MODEL), F32), -b0, a0, -b1, a1, kr], axis=0)
    wina_ref[:, _S_KVA:_S_KVA + LANES] = blk.T.astype(BF16)
    for j in range(n_r // MXU_TILE):
        rows = slice(_S_KR + j * MXU_TILE, _S_KR + (j + 1) * MXU_TILE)
        winr_ref[:, j * MXU_TILE:(j + 1) * MXU_TILE] = wt_ref[rows, :].T.astype(BF16)


def _rope_slab(s, t1, t2):
    return s * t1 + pltpu.roll(s, DR_A, 1) * t2


def _proj_kernel(x_ref, mod_ref, lng_ref, lnb_ref, wina_ref, winr_ref, gqa_ref, wq_ref, gkva_ref,
                 wkn_ref, wvt_ref, *refs, latent, n_sub):
    if latent:
        t1q_ref, t2q_ref, t1k_ref, t2k_ref, q_ref, k_ref, vt_ref, xm_ref, sza_ref, som_ref, szm_ref, h_ref = refs
    else:
        k_ref, vt_ref, xm_ref = refs
    sub = x_ref.shape[1] // n_sub
    mod = mod_ref[0]
    shift, scale1 = mod[:, 0:D_MODEL], 1.0 + mod[:, D_MODEL:2 * D_MODEL]
    ones_rows = (lax.broadcasted_iota(jnp.int32, (DV_EXT - DV_A, sub), 0) == 0).astype(BF16)

    def norm_stage(s):
        rows = slice(s * sub, (s + 1) * sub)
        h = _layer_norm(x_ref[0, rows, :]) * lng_ref[...] + lnb_ref[...]
        if latent:
            h_ref[0, rows, :] = h
        return (h * scale1 + shift).astype(BF16)

    def proj_stage(s, u):
        rows = slice(s * sub, (s + 1) * sub)
        p0 = _dot(u, wina_ref[...])
        d_0 = _dot(u, winr_ref[:, 0:512] if latent else winr_ref[:, 512:1024])
        kvn = _rms_norm(p0[:, 256:384], gkva_ref[...]).astype(BF16)
        kn = _dot(kvn, wkn_ref[...])
        d_om = _dot(u, winr_ref[:, 1024:1536]) if latent else None
        if latent:
            krr = _rope_slab(p0[:, 384:512], t1k_ref[rows, :], t2k_ref[rows, :])
        else:
            rope_lanes = lax.broadcasted_iota(jnp.int32, (sub, SLAB), 1) >= SLAB - DR_A
            krr = jnp.where(rope_lanes, p0[:, 384:512], 0.0)
        for hh in range(H_A):
            k_ref[0, hh, rows, :] = (kn[:, hh * SLAB:(hh + 1) * SLAB] + krr).astype(BF16)
        vt = _dot_nt(wvt_ref[...], kvn).astype(BF16)
        for hh in range(H_A):
            vt_ref[0, hh, 0:DV_A, rows] = vt[hh * DV_A:(hh + 1) * DV_A, :]
            vt_ref[0, hh, DV_A:DV_EXT, rows] = ones_rows
        if not latent:
            xm_ref[0, rows, :] = d_0.astype(BF16)
        else:
            d_zm = _dot(u, winr_ref[:, 1536:2048])
            qn = _rms_norm(p0[:, 0:256], gqa_ref[...]).astype(BF16)
            qs = _dot(qn, wq_ref[...])
            sza_ref[0, rows, :] = _silu(d_0).astype(BF16)
            d_xm = _dot(u, winr_ref[:, 512:1024])
            t1q, t2q = t1q_ref[rows, :], t2q_ref[rows, :]
            for hh in range(H_A):
                q_ref[0, hh, rows, :] = _rope_slab(qs[:, hh * SLAB:(hh + 1) * SLAB], t1q, t2q).astype(BF16)
            som_ref[0, rows, :] = jax.nn.sigmoid(d_om).astype(BF16)
            szm_ref[0, rows, :] = _silu(d_zm).astype(BF16)
            xm_ref[0, rows, :] = d_xm.astype(BF16)

    u = norm_stage(0)
    for s in range(n_sub):
        u_next = norm_stage(s + 1) if s + 1 < n_sub else None
        proj_stage(s, u)
        u = u_next


def _proj_call(x, mod3, mod_row, lng, lnb, wina, winr, gqa, wq, gkva, wkn, wvt, *tables, tm, n_sub, latent):
    b, t, _ = x.shape
    const = lambda shape: pl.BlockSpec(shape, lambda i, j: (0,) * len(shape))
    tab = pl.BlockSpec((tm, SLAB), lambda i, j: (j, 0))
    assert len(tables) == (4 if latent else 0)
    row = lambda w: pl.BlockSpec((1, tm, w), lambda i, j: (i, j, 0))
    if mod_row is None:
        mod_spec = pl.BlockSpec((1, 1, 3 * D_MODEL), lambda i, j: (i, 0, 0))
    else:
        mod_spec = pl.BlockSpec((1, 1, 3 * D_MODEL), lambda i, j: (mod_row, 0, 0))
    in_specs = [row(D_MODEL), mod_spec, const((1, D_MODEL)), const((1, D_MODEL)),
                const(wina.shape), const(winr.shape), const((1, Q_LORA)), const((Q_LORA, H_A * SLAB)),
                const((1, KV_LORA)), const((KV_LORA, H_A * SLAB)), const((W_A, KV_LORA))]
    in_specs += [tab] * len(tables)
    k_spec = pl.BlockSpec((1, H_A, tm, SLAB), lambda i, j: (i, 0, j, 0))
    vt_spec = pl.BlockSpec((1, H_A, DV_EXT, tm), lambda i, j: (i, 0, 0, j))
    k_shape = jax.ShapeDtypeStruct((b, H_A, t, SLAB), BF16)
    vt_shape = jax.ShapeDtypeStruct((b, H_A, DV_EXT, t), BF16)
    half = jax.ShapeDtypeStruct((b, t, W_B), BF16)
    if latent:
        out_specs = [k_spec, k_spec, vt_spec, row(W_B), row(W_A), row(W_B), row(W_B), row(D_MODEL)]
        out_shape = [k_shape, k_shape, vt_shape, half, half, half, half,
                     jax.ShapeDtypeStruct((b, t, D_MODEL), F32)]
    else:
        out_specs = [k_spec, vt_spec, row(W_B)]
        out_shape = [k_shape, vt_shape, half]
    return pl.pallas_call(
        functools.partial(_proj_kernel, latent=latent, n_sub=n_sub),
        grid=(b, t // tm),
        in_specs=in_specs, out_specs=out_specs, out_shape=out_shape,
        compiler_params=pltpu.CompilerParams(dimension_semantics=("parallel", "parallel"),
                                             vmem_limit_bytes=VMEM_LIMIT),
        name="in_proj_latent" if latent else "in_proj_ctx",
    )(x, mod3, lng, lnb, wina, winr, gqa, wq, gkva, wkn, wvt, *tables)


def _attn_kernel(q_ref, kc_ref, kl_ref, vtc_ref, vtl_ref, sza_ref, hs_ref, som_ref, szm_ref, xc_ref,
                 mhg_ref, skip_ref, o_ref, yb_ref, s_buf, ot_s):
    n_ctx, t = kc_ref.shape[2], kl_ref.shape[2]
    kc = ATTN_KEY_CHUNK

    def scores(h, slot):
        qh = q_ref[0, h]
        sc = _dot_nt(kc_ref[0, h], qh)
        sk = _dot_nt(kl_ref[0, h], qh)
        s_buf[slot, 0:n_ctx, :] = sc
        s_buf[slot, n_ctx:n_ctx + t, :] = sk
        return jnp.maximum(jnp.max(sc, axis=0, keepdims=True), jnp.max(sk, axis=0, keepdims=True))

    def values(h, slot, m):
        chunks = [(vtc_ref, c * kc, c * kc) for c in range(n_ctx // kc)]
        chunks += [(vtl_ref, c * kc, n_ctx + c * kc) for c in range(t // kc)]
        acc = None
        for vref, v0, s0 in chunks:
            p = jnp.exp2(s_buf[slot, s0:s0 + kc, :] - m).astype(BF16)
            d = _dot(vref[0, h, :, v0:v0 + kc], p)
            acc = d if acc is None else acc + d
        rows = pl.ds(pl.multiple_of(h * DV_A, DV_A), DV_A)
        ot_s[rows, :] = acc[0:DV_A, :] / acc[DV_A:DV_A + 1, :]

    def cell_out(pc):
        rows = pl.ds(pl.multiple_of(pc * LANES, LANES), LANES)
        yb_ref[0, rows, :] = _mlstm_out(hs_ref[0, pc], som_ref[0, rows, :], szm_ref[0, rows, :],
                                        xc_ref[0, rows, :], mhg_ref[...], skip_ref[...])

    def body(i, m_even):
        h = 2 * i
        m_odd = scores(h + 1, 1)
        cell_out(i + 1)
        values(h, 0, m_even)
        m_even = scores(h + 2, 0)
        values(h + 1, 1, m_odd)
        return m_even

    assert hs_ref.shape[1] == H_A // 2
    m_first = scores(0, 0)
    cell_out(0)
    m_even = lax.fori_loop(0, H_A // 2 - 1, body, m_first)
    m_odd = scores(H_A - 1, 1)
    values(H_A - 2, 0, m_even)
    values(H_A - 1, 1, m_odd)
    o_ref[0] = (ot_s[...].T * sza_ref[0].astype(F32)).astype(BF16)


def _attn_call(q, kc, kl, vtc, vtl, sza, hs, som, szm, xc, mhg, skip, *, tq):
    b, _, t, _ = q.shape
    n_ctx = kc.shape[2] // b
    head_blk = lambda n, w: pl.BlockSpec((1, H_A, n, w), lambda i, j: (i, 0, 0, 0))
    const = lambda shape: pl.BlockSpec(shape, lambda i, j: (0,) * len(shape))
    row = lambda w: pl.BlockSpec((1, tq, w), lambda i, j: (i, j, 0))
    half = jax.ShapeDtypeStruct((b, t, W_A), BF16)
    return pl.pallas_call(
        _attn_kernel,
        grid=(b, t // tq),
        in_specs=[pl.BlockSpec((1, H_A, tq, SLAB), lambda i, j: (i, 0, j, 0)),
                  pl.BlockSpec((1, H_A, n_ctx, SLAB), lambda i, j: (0, 0, i, 0)), head_blk(t, SLAB),
                  pl.BlockSpec((1, H_A, DV_EXT, n_ctx), lambda i, j: (0, 0, 0, i)), head_blk(DV_EXT, t),
                  row(W_A),
                  pl.BlockSpec((1, tq // LANES, W_B, LANES), lambda i, j: (i, j, 0, 0)),
                  row(W_B), row(W_B), row(W_B), const((1, W_B)), const((1, W_B))],
        out_specs=[row(W_A), row(W_B)],
        out_shape=[half, jax.ShapeDtypeStruct((b, t, W_B), BF16)],
        scratch_shapes=[pltpu.VMEM((2, n_ctx + t, tq), F32),
                        pltpu.VMEM((W_A, tq), F32)],
        compiler_params=pltpu.CompilerParams(dimension_semantics=("parallel", "parallel"),
                                             vmem_limit_bytes=VMEM_LIMIT),
        name="mla_attention",
    )(q, kc, kl, vtc, vtl, sza, hs, som, szm, xc, mhg, skip)


def _onehot(cond):
    return cond.astype(F32).astype(BF16)


def _block_diag(a):
    bs, nb = QKV_BS, W_B // QKV_BS
    lb, sq = bs.bit_length() - 1, bs * bs
    assert bs == 1 << lb
    iota = lambda shape, d: lax.broadcasted_iota(jnp.int32, shape, d)
    spread = _onehot(iota((nb, W_B), 1) >> lb == iota((nb, W_B), 0))
    b = _dot(a, spread)
    col_o = iota((sq, W_B), 1) & (bs - 1)
    bm = jnp.concatenate([jnp.where(col_o == o, b, 0.0) for o in range(bs)], axis=0).astype(BF16)
    j = iota((W_B, bs * sq), 1)
    pick = _onehot((j & (sq - 1)) == ((iota((W_B, bs * sq), 0) & (bs - 1)) << lb) + (j >> (2 * lb)))
    same_block = iota((W_B, W_B), 0) >> lb == iota((W_B, W_B), 1) >> lb
    return jnp.where(same_block, _dot(pick, bm), 0.0).astype(BF16)


def _mlstm_weight_kernel(aq_ref, ak_ref, av_ref, wgt_ref, bgate_ref,
                         wmq_ref, wmk_ref, wmv_ref, wgc_ref, wgm_ref, bg_ref):
    a_q, a_k, a_v = aq_ref[...], ak_ref[...], av_ref[...]
    wq, wk, wv = (_block_diag(a.astype(BF16)) for a in (a_q, a_k, a_v))
    wmq_ref[...] = wq
    wmk_ref[...] = _block_diag((a_k * (DH_B ** -0.5)).astype(BF16))
    wmv_ref[...] = wv
    dst, src = (lax.broadcasted_iota(jnp.int32, (LANES, 2 * N_HD), d) for d in (0, 1))
    grp = dst >> (H_B.bit_length() - 1)
    src_of_dst = (dst & (H_B - 1)) + H_B * jnp.where(grp == 1, 2, jnp.where(grp == 2, 1, grp))
    perm = _onehot((dst < 2 * N_HD) & (src == src_of_dst))
    wg_t = _dot(perm, wgt_ref[...].astype(BF16)).astype(BF16)
    wgc_ref[...] = (_dot_nt(wq, wg_t[:, 0:W_B]) + _dot_nt(wk, wg_t[:, W_B:2 * W_B])).astype(BF16)
    wgm_ref[...] = _dot_nt(wv, wg_t[:, 2 * W_B:3 * W_B]).astype(BF16)
    bg_ref[...] = jnp.zeros(bg_ref.shape, F32)
    for d, s in enumerate((0, 2, 1, 3)):
        bg_ref[:, d * H_B:(d + 1) * H_B] = bgate_ref[:, s * H_B:(s + 1) * H_B]


def _qb_kernel(wqb_ref, wq_ref):
    wt = wqb_ref[...].T
    hw = DN_A + DR_A
    for hh in range(H_A):
        rope = wt[hh * hw + DN_A:(hh + 1) * hw, :]
        a0, b0, a1, b1 = (rope[i * ROPE_FREQS:(i + 1) * ROPE_FREQS, :] for i in range(4))
        slab = jnp.concatenate([wt[hh * hw:hh * hw + DN_A, :], -b0, a0, -b1, a1, rope], axis=0)
        wq_ref[:, hh * SLAB:(hh + 1) * SLAB] = slab.T.astype(BF16)


def _kvb_kernel(wkvb_ref, wkn_ref, wvt_ref):
    w = wkvb_ref[...]
    assert DN_A + DV_A == SLAB
    lane = lax.broadcasted_iota(jnp.int32, w.shape, 1)
    wkn_ref[...] = jnp.where((lane & (SLAB - 1)) < DN_A, w, 0.0).astype(BF16)
    wt = w.T
    for hh in range(H_A):
        wvt_ref[hh * DV_A:(hh + 1) * DV_A, :] = wt[hh * SLAB + DN_A:(hh + 1) * SLAB, :].astype(BF16)


def _prep_kernel(c_ref, cctx_ref, wada_ref, bada_ref, wt_ref, aq_ref, ak_ref, av_ref, wgt_ref, bgate_ref,
                 wqb_ref, wkvb_ref,
                 mod_ref, wina_ref, winr_ref, wmq_ref, wmk_ref, wmv_ref, wgc_ref, wgm_ref, bg_ref,
                 wq_ref, wkn_ref, wvt_ref):
    _ada_kernel(c_ref, cctx_ref, wada_ref, bada_ref, mod_ref)
    _win_kernel(wt_ref, wina_ref, winr_ref)
    _mlstm_weight_kernel(aq_ref, ak_ref, av_ref, wgt_ref, bgate_ref,
                         wmq_ref, wmk_ref, wmv_ref, wgc_ref, wgm_ref, bg_ref)
    _qb_kernel(wqb_ref, wq_ref)
    _kvb_kernel(wkvb_ref, wkn_ref, wvt_ref)


def _prep_call(c, c_ctx, w_ada, b_ada, wt, aq, ak, av, wgt, b_gate, w_qb, w_kvb):
    n_in, d = wt.shape
    assert c.shape[0] % SUBLANES == 0 and wgt.shape == (2 * N_HD, 3 * W_B)
    assert aq.shape == ak.shape == av.shape == (QKV_BS * QKV_BS, W_B // QKV_BS)
    bf = lambda *shape: jax.ShapeDtypeStruct(shape, BF16)
    return pl.pallas_call(
        _prep_kernel,
        out_shape=[jax.ShapeDtypeStruct((c.shape[0] + 1, 1, w_ada.shape[1]), F32),
                   bf(d, _S_KVA + LANES), bf(d, n_in - _S_KR),
                   bf(W_B, W_B), bf(W_B, W_B), bf(W_B, W_B), bf(W_B, LANES), bf(W_B, LANES),
                   jax.ShapeDtypeStruct((1, LANES), F32),
                   bf(Q_LORA, H_A * SLAB), bf(KV_LORA, H_A * SLAB), bf(W_A, KV_LORA)],
        compiler_params=pltpu.CompilerParams(vmem_limit_bytes=VMEM_LIMIT),
        name="param_prep",
    )(c, c_ctx, w_ada, b_ada, wt, aq, ak, av, wgt, b_gate, w_qb, w_kvb)


_PAD = SUBLANES


def _lane_scans(rows_per_chunk, combine, fill, out):
    npc = CHUNK // LANES
    flat = [x[:, i * LANES:(i + 1) * LANES] for x in rows_per_chunk for i in range(npc)]
    lane = lax.broadcasted_iota(jnp.int32, flat[0].shape, 1)
    pre, suf, sh = list(flat), list(flat), 1
    while sh < LANES:
        pre = [combine(p, jnp.where(lane >= sh, pltpu.roll(p, sh, 1), fill)) for p in pre]
        suf = [combine(s, jnp.where(lane < LANES - sh, pltpu.roll(s, LANES - sh, 1), fill)) for s in suf]
        sh *= 2
        yield
    for c in range(len(rows_per_chunk)):
        p, s = pre[c * npc:(c + 1) * npc], suf[c * npc:(c + 1) * npc]
        tot = [x[:, LANES - 1:LANES] for x in p]
        run = None
        for i in range(npc):
            if run is not None:
                p[i] = combine(p[i], run)
            run = tot[i] if run is None else combine(run, tot[i])
        run = None
        for i in reversed(range(npc)):
            if run is not None:
                s[i] = combine(s[i], run)
            run = tot[i] if run is None else combine(run, tot[i])
        out.append((jnp.concatenate(p, axis=1), jnp.concatenate(s, axis=1)))


def _gate_tables(gts, dests):
    L = CHUNK
    fwd_rows = lax.broadcasted_iota(jnp.int32, (N_HD, L), 0) < H_B
    lis = [gt[0:N_HD, :] for gt in gts]
    sums = []
    yield from _lane_scans([_log_sigmoid(gt[N_HD:2 * N_HD, :]) for gt in gts], jnp.add, 0.0, sums)
    cums = [jnp.where(fwd_rows, ps, ss) for ps, ss in sums]
    rs = [li - cum for li, cum in zip(lis, cums)]
    maxs = []
    yield from _lane_scans(rs, jnp.maximum, -jnp.inf, maxs)
    for (grow_ref, c, gcol_ref, r0), cum, r, (pm, sm) in zip(dests, cums, rs, maxs):
        grow_ref[c] = jnp.concatenate([cum, r, jnp.where(fwd_rows, pm, sm)], axis=0)
        gcol_ref[r0:r0 + L, :] = jnp.concatenate([r, jnp.zeros((LANES - N_HD, L), F32)], axis=0).T


def _fill_pad(pad_s, xm_ref, n):
    zrow = jnp.zeros((_PAD, W_B), F32)
    pad_s[0:_PAD, :] = zrow
    pad_s[_PAD:_PAD + n, :] = xm_ref[...].astype(F32)
    pad_s[_PAD + n:2 * _PAD + n, :] = zrow


def _conv_gate_stage(c, pad_s, xm_ref, cw_ref, cb_ref, wgc_ref, wgm_ref, bg_ref, xc_out, xcb_s):
    L = CHUNK
    rows = slice(c * L, (c + 1) * L)
    r0 = c * L + _PAD
    row_id = lax.broadcasted_iota(jnp.int32, (L, W_B), 0)
    xcur = pad_s[r0:r0 + L, :]
    xprev = jnp.where(row_id == 0, pad_s[r0 - 1:r0, :], pltpu.roll(xcur, 1, 0))
    xnext = jnp.where(row_id == L - 1, pad_s[r0 + L:r0 + L + 1, :], pltpu.roll(xcur, L - 1, 0))
    pre = cb_ref[...] + xprev * cw_ref[0:1, :] + xcur * cw_ref[1:2, :] + xnext * cw_ref[2:3, :]
    xc = _silu(pre)
    xcb = xc.astype(BF16)
    if xc_out is not None:
        xc_out[rows, :] = xcb
    xcb_s[rows, :] = xcb
    g = _dot(xcb, wgc_ref[...]) + _dot(xm_ref[rows, :], wgm_ref[...]) + bg_ref[...]
    return g.T[0:2 * N_HD, :]


def _headwise_stage(c, xm_ref, xcb_s, wq_ref, wk_ref, wv_ref, q_s, k_s, vt_s):
    L = CHUNK
    rows = slice(c * L, (c + 1) * L)
    for hf in range(W_B // MXU_TILE):
        cs = slice(hf * MXU_TILE, (hf + 1) * MXU_TILE)
        xcb = xcb_s[rows, cs]
        if q_s is not None:
            q_s[rows, cs] = _dot(xcb, wq_ref[cs, cs]).astype(BF16)
        k_s[rows, cs] = _dot(xcb, wk_ref[cs, cs]).astype(BF16)
        vt_s[c, cs, :] = _dot(xm_ref[rows, cs], wv_ref[cs, cs]).T.astype(BF16)


_TAB_ROWS = 5 * N_HD


def _chain_tables(growc_s, grow_s, tab_s, nc):
    L = CHUNK
    fwd_rows = lax.broadcasted_iota(jnp.int32, (N_HD, L), 0) < H_B
    gc = growc_s[0]
    steps = [(gc, gc)] + [(grow_s[i], grow_s[nc - 1 - i]) for i in range(nc)]
    parts = []
    for g_f, g_b in steps:
        pick = lambda a: jnp.where(fwd_rows, g_f[a * N_HD:(a + 1) * N_HD, :], g_b[a * N_HD:(a + 1) * N_HD, :])
        cum, r, pm = pick(0), pick(1), pick(2)
        b_end = jnp.where(fwd_rows[:, 0:1], cum[:, L - 1:L], cum[:, 0:1])
        w = r + b_end
        parts.append((cum, pm, b_end, w, jnp.max(w, axis=1, keepdims=True)))
    m0 = jnp.zeros((N_HD, 1), F32)
    for idx, (cum, pm, b_end, w, wmax) in enumerate(parts):
        m_new = jnp.maximum(b_end + m0, wmax)
        mu = jnp.maximum(m0, pm)
        decay = jnp.broadcast_to(jnp.exp(b_end + m0 - m_new), (N_HD, L))
        tab_s[idx] = jnp.concatenate([mu, jnp.exp(m0 - mu), jnp.exp(-(cum + mu)),
                                      jnp.exp(w - m_new), decay], axis=0)
        m0 = m_new


def _tab(tab, a, hd):
    return tab[a * N_HD + hd:a * N_HD + hd + 1, :]


def _state_update(st_ref, hd, vt_ext, k_c, ws_row, decay_hd):
    vw = (vt_ext.astype(F32) * ws_row).astype(BF16)
    st_ref[hd] = decay_hd * st_ref[hd] + _dot(vw, k_c)


def _mlstm_kernel(xm_ref, xmc_ref, cw_ref, cb_ref, wq_ref, wk_ref, wv_ref, wgc_ref, wgm_ref, bg_ref,
                  hs_ref, xc_ref,
                  pad_s, xcb_s, q_s, k_s, vt_s, gcol_s, grow_s,
                  padc_s, xcbc_s, kc_s, vtc_s, gcolc_s, growc_s, tab_s, st_s):
    L = CHUNK
    t = xm_ref.shape[1]
    nc = t // L
    row_i = lax.broadcasted_iota(jnp.int32, (L, L), 0)
    col_i = lax.broadcasted_iota(jnp.int32, (L, L), 1)
    tri = (row_i <= col_i, row_i >= col_i)
    ones_rows = (lax.broadcasted_iota(jnp.int32, (ST_ROWS - DH_B, L), 0) == 0).astype(BF16)
    hsl = lambda hh: slice(hh * DH_B, (hh + 1) * DH_B)
    xm, xmc = xm_ref.at[0], xmc_ref.at[0]
    gate_w = (cw_ref, cb_ref, wgc_ref, wgm_ref, bg_ref)

    _fill_pad(padc_s, xmc, CTX_LEN)
    _fill_pad(pad_s, xm, t)
    gts = [_conv_gate_stage(0, padc_s, xmc, *gate_w, None, xcbc_s)]
    gts += [_conv_gate_stage(c, pad_s, xm, *gate_w, xc_ref.at[0], xcb_s) for c in range(nc)]
    dests = [(growc_s, 0, gcolc_s, 0)] + [(grow_s, c, gcol_s, c * L) for c in range(nc)]
    tables = _gate_tables(gts, dests)
    _headwise_stage(0, xmc, xcbc_s, wq_ref, wk_ref, wv_ref, None, kc_s, vtc_s)
    for c in range(nc):
        next(tables, None)
        next(tables, None)
        _headwise_stage(c, xm, xcb_s, wq_ref, wk_ref, wv_ref, q_s, k_s, vt_s)
    for _ in tables:
        pass

    _chain_tables(growc_s, grow_s, tab_s, nc)
    st_s[...] = jnp.zeros(st_s.shape, F32)
    tab = tab_s[0]
    for hd in range(N_HD):
        hh = hd % H_B
        vt_ext = jnp.concatenate([vtc_s[0, hsl(hh), :], ones_rows], axis=0)
        _state_update(st_s, hd, vt_ext, kc_s[:, hsl(hh)], _tab(tab, 3, hd), _tab(tab, 4, hd)[:, 0:DH_B])

    def body(i, carry, *, accumulate):
        ci = (i, nc - 1 - i)
        rows = tuple(pl.ds(pl.multiple_of(c * L, L), L) for c in ci)
        tab = tab_s[i + 1]
        gcols = tuple(gcol_s[r, :] for r in rows)
        live = {}

        def stage_scores(hd):
            d, hh = hd // H_B, hd % H_B
            k_c = k_s[rows[d], hsl(hh)]
            vt_ext = jnp.concatenate([vt_s[ci[d], hsl(hh), :], ones_rows], axis=0)
            lhs = jnp.concatenate([k_c, st_s[hd].astype(BF16)], axis=0)
            live[hd] = (k_c, vt_ext, _dot_nt(lhs, q_s[rows[d], hsl(hh)]))

        def stage_gate(hd):
            d = hd // H_B
            k_c, vt_ext, res = live[hd]
            rcol = gcols[d][:, hd:hd + 1]
            e = jnp.exp(jnp.where(tri[d], rcol - _tab(tab, 0, hd), -jnp.inf))
            p = (res[0:L, :] * e).astype(BF16)
            _state_update(st_s, hd, vt_ext, k_c, _tab(tab, 3, hd), _tab(tab, 4, hd)[:, 0:DH_B])
            live[hd] = (vt_ext, res[L:L + ST_ROWS, :], p)

        def stage_out(hd):
            d, hh = hd // H_B, hd % H_B
            vt_ext, inter, p = live.pop(hd)
            tot = _tab(tab, 1, hd) * inter + _dot(vt_ext, p)
            den = tot[DH_B:DH_B + 1, :]
            h_t = tot[0:DH_B, :] / jnp.maximum(jnp.abs(den), _tab(tab, 2, hd))
            for pc in range(L // LANES):
                piece = h_t[:, pc * LANES:(pc + 1) * LANES]
                if accumulate:
                    hs_ref[0, ci[d] * (L // LANES) + pc, hsl(hh), :] += piece
                else:
                    hs_ref[0, ci[d] * (L // LANES) + pc, hsl(hh), :] = piece

        stages = (stage_scores, stage_gate, stage_out)
        for step in range(N_HD + len(stages) - 1):
            for si in range(len(stages)):
                if 0 <= step - si < N_HD:
                    stages[si](step - si)
        return carry

    lax.fori_loop(0, nc // 2, functools.partial(body, accumulate=False), 0)
    lax.fori_loop(nc // 2, nc, functools.partial(body, accumulate=True), 0)


def _mlstm_out(hs_t, som, szm, xc, mhg, skip):
    hsum = hs_t.T * som.astype(F32)
    parts = [_layer_norm(hsum[:, hh * DH_B:(hh + 1) * DH_B]) for hh in range(H_B)]
    hb = jnp.concatenate(parts, axis=1) * mhg + skip * xc.astype(F32)
    return (hb * szm.astype(F32)).astype(BF16)


def _mlstm_call(xm, xmc, conv_w, conv_b, wq, wk, wv, wgc, wgm, bg):
    b, t, _ = xm.shape
    nc = t // CHUNK
    assert nc % 2 == 0
    seq = lambda n: pl.BlockSpec((1, n, W_B), lambda i: (i, 0, 0))
    const = lambda shape: pl.BlockSpec(shape, lambda i: (0,) * len(shape))
    scratch = [
        pltpu.VMEM((t + 2 * _PAD, W_B), F32),
        pltpu.VMEM((t, W_B), BF16),
        pltpu.VMEM((t, W_B), BF16),
        pltpu.VMEM((t, W_B), BF16),
        pltpu.VMEM((nc, W_B, CHUNK), BF16),
        pltpu.VMEM((t, LANES), F32),
        pltpu.VMEM((nc, 3 * N_HD, CHUNK), F32),
        pltpu.VMEM((CTX_LEN + 2 * _PAD, W_B), F32),
        pltpu.VMEM((CTX_LEN, W_B), BF16),
        pltpu.VMEM((CTX_LEN, W_B), BF16),
        pltpu.VMEM((1, W_B, CHUNK), BF16),
        pltpu.VMEM((CTX_LEN, LANES), F32),
        pltpu.VMEM((1, 3 * N_HD, CHUNK), F32),
        pltpu.VMEM((nc + 1, _TAB_ROWS, CHUNK), F32),
        pltpu.VMEM((N_HD, ST_ROWS, DH_B), F32),
    ]
    return pl.pallas_call(
        _mlstm_kernel,
        grid=(b,),
        in_specs=[seq(t), seq(CTX_LEN), const((3, W_B)), const((1, W_B)),
                  const((W_B, W_B)), const((W_B, W_B)), const((W_B, W_B)),
                  const((W_B, LANES)), const((W_B, LANES)), const((1, LANES))],
        out_specs=[pl.BlockSpec((1, t // LANES, W_B, LANES), lambda i: (i, 0, 0, 0)), seq(t)],
        out_shape=[jax.ShapeDtypeStruct((b, t // LANES, W_B, LANES), F32),
                   jax.ShapeDtypeStruct((b, t, W_B), BF16)],
        scratch_shapes=scratch,
        compiler_params=pltpu.CompilerParams(dimension_semantics=("arbitrary",),
                                             vmem_limit_bytes=VMEM_LIMIT),
        name="mlstm_bidir",
    )(xm, xmc, conv_w, conv_b, wq, wk, wv, wgc, wgm, bg)


def _out_kernel(h_ref, mod_ref, ya_ref, yb_ref, wo_ref, g_ref, b_ref, o_ref, *, n_sub):
    sub = h_ref.shape[1] // n_sub
    gate = mod_ref[0][:, 2 * D_MODEL:3 * D_MODEL]

    def mix_stage(s):
        rows = slice(s * sub, (s + 1) * sub)
        return (_dot(ya_ref[0, rows, :], wo_ref[0:W_A, :])
                + _dot(yb_ref[0, rows, :], wo_ref[W_A:W_A + W_B, :]))

    def norm_stage(s, y):
        rows = slice(s * sub, (s + 1) * sub)
        o_ref[0, rows, :] = _layer_norm(ALPHA * h_ref[0, rows, :] + gate * y) * g_ref[...] + b_ref[...]

    y = mix_stage(0)
    for s in range(n_sub):
        y_next = mix_stage(s + 1) if s + 1 < n_sub else None
        norm_stage(s, y)
        y = y_next


def _out_call(h, mod3, ya, yb, wo, g, bb, *, tm, n_sub):
    b, t, _ = h.shape
    const = lambda shape: pl.BlockSpec(shape, lambda i, j: (0,) * len(shape))
    row = lambda w: pl.BlockSpec((1, tm, w), lambda i, j: (i, j, 0))
    return pl.pallas_call(
        functools.partial(_out_kernel, n_sub=n_sub),
        grid=(b, t // tm),
        in_specs=[row(D_MODEL), pl.BlockSpec((1, 1, 3 * D_MODEL), lambda i, j: (i, 0, 0)),
                  row(W_A), row(W_B),
                  const((W_A + W_B, D_MODEL)), const((1, D_MODEL)), const((1, D_MODEL))],
        out_specs=row(D_MODEL),
        out_shape=jax.ShapeDtypeStruct((b, t, D_MODEL), F32),
        compiler_params=pltpu.CompilerParams(dimension_semantics=("parallel", "parallel"),
                                             vmem_limit_bytes=VMEM_LIMIT),
        name="out_proj_ln",
    )(h, mod3, ya, yb, wo, g, bb)


def _rope_tables(seq, scale_keep, scale_rope):
    n_rows = seq // GRID_W
    rowp = np.repeat(np.arange(n_rows, dtype=np.float32), GRID_W)
    colp = np.tile(np.arange(GRID_W, dtype=np.float32), n_rows)
    inv = (np.float32(ROPE_BASE) ** (-np.arange(ROPE_FREQS, dtype=np.float32) / np.float32(ROPE_FREQS)))
    ang = np.stack([rowp[:, None] * inv, colp[:, None] * inv], axis=1).astype(np.float32)
    cos = np.broadcast_to(np.cos(ang)[:, :, None, :], (seq, 2, 2, ROPE_FREQS)).reshape(seq, DR_A)
    sin = np.broadcast_to(np.sin(ang)[:, :, None, :], (seq, 2, 2, ROPE_FREQS)).reshape(seq, DR_A)
    z32 = np.zeros((seq, DR_A), np.float32)
    t1 = np.concatenate([np.full((seq, DN_A), scale_keep, np.float32), z32, cos * scale_rope], axis=1)
    t2 = np.concatenate([np.zeros((seq, DN_A), np.float32), z32, sin * scale_rope], axis=1)
    return jnp.asarray(t1, F32), jnp.asarray(t2, F32)


def kernel(x, c, ctx, c_ctx, ln_in_g, ln_in_b, w_ada, b_ada, w_in, g_qa, w_qb, g_kva, w_kvb, conv_w, conv_b, w_mq, w_mk, w_mv, w_gate, b_gate, mh_g, skip, w_out, ln_g, ln_b):
    b, t, _ = x.shape
    l = 0
    r2 = lambda v: v.reshape(1, -1)

    blocks = lambda w: jnp.transpose(w[l], (1, 2, 0)).reshape(QKV_BS * QKV_BS, -1)
    wo = w_out[l].astype(BF16)

    sm_scale = (DN_A + DR_A) ** -0.5 * LOG2_E
    t1q, t2q = _rope_tables(t, sm_scale, sm_scale)
    t1k, t2k = _rope_tables(t, 0.0, 1.0)
    n_ctx = ctx.shape[1]

    mod3, wina, winr, wmq, wmk_s, wmv, wgc, wgm, bg, wq, wkn, wvt = _prep_call(
        c, r2(c_ctx), w_ada[l], r2(b_ada[l]), jnp.swapaxes(w_in[l], 0, 1), blocks(w_mq), blocks(w_mk),
        blocks(w_mv), jnp.swapaxes(w_gate[l], 0, 1), r2(b_gate[l]), w_qb[l], w_kvb[l])

    lng, lnb = r2(ln_in_g), r2(ln_in_b)
    shared = (wina, winr, r2(g_qa[l]), wq, r2(g_kva[l]), wkn, wvt)
    q, kl, vtl, xm, sza, som, szm, h = _proj_call(x, mod3, None, lng, lnb, *shared, t1q, t2q, t1k, t2k,
                                                  tm=1024, n_sub=4, latent=True)
    kc, vtc, xmc = _proj_call(ctx.reshape(1, b * n_ctx, D_MODEL), mod3, b, lng, lnb, *shared,
                              tm=min(1024, b * n_ctx), n_sub=2, latent=False)
    xmc = xmc.reshape(b, n_ctx, W_B)

    hs, xc = _mlstm_call(xm, xmc, conv_w[l], r2(conv_b[l]), wmq, wmk_s, wmv, wgc, wgm, bg)
    ya, yb = _attn_call(q, kc, kl, vtc, vtl, sza, hs, som, szm, xc, r2(mh_g[l]), r2(skip[l]), tq=512)
    return _out_call(h, mod3, ya, yb, wo, r2(ln_g[l]), r2(ln_b[l]), tm=1024, n_sub=4)
```

```python
import functools

import numpy as np
import jax
import jax.numpy as jnp
from jax import lax
from jax.experimental import pallas as pl
from jax.experimental.pallas import tpu as pltpu

F32 = jnp.float32
BF16 = jnp.bfloat16

D_MODEL = 1024
CTX_LEN = 256
GRID_W = 64
H_A, DN_A, DR_A, DV_A = 8, 64, 32, 64
W_A = H_A * DV_A
Q_LORA, KV_LORA = 256, 128
ROPE_FREQS = DR_A // 4
ROPE_BASE = 10000.0
H_B, DH_B = 4, 128
W_B = H_B * DH_B
QKV_BS = 4
DEPTH = 1
ALPHA = (2.0 * DEPTH) ** 0.25
LN_EPS = 1e-5
RMS_EPS = 1e-6
LOG2_E = 1.4426950408889634

LANES = 128
SUBLANES = 8
MXU_TILE = 256
SLAB = LANES
DV_EXT = DV_A + 16
ATTN_KEY_CHUNK = MXU_TILE
CHUNK = 256
N_HD = 2 * H_B
ST_ROWS = DH_B + 16
VMEM_LIMIT = 56 * 1024 * 1024

_NT = (((1,), (1,)), ((), ()))


def _dot(a, b):
    return jnp.dot(a, b, preferred_element_type=F32)


def _dot_nt(a, b):
    return lax.dot_general(a, b, _NT, preferred_element_type=F32)


def _layer_norm(x):
    mu = jnp.mean(x, axis=-1, keepdims=True)
    xc = x - mu
    var = jnp.mean(xc * xc, axis=-1, keepdims=True)
    return xc * lax.rsqrt(var + LN_EPS)


def _rms_norm(x, g):
    return (x * lax.rsqrt(jnp.mean(x * x, axis=-1, keepdims=True) + RMS_EPS)) * g


def _silu(x):
    return x * jax.nn.sigmoid(x)


def _log_sigmoid(x):
    return jnp.minimum(x, 0.0) - jnp.log1p(jnp.exp(-jnp.abs(x)))


def _ada_kernel(c_ref, cctx_ref, w_ref, b_ref, o_ref):
    first = lax.broadcasted_iota(jnp.int32, (SUBLANES, D_MODEL), 0) == 0
    cc = jnp.concatenate([c_ref[...], jnp.where(first, cctx_ref[...], 0.0)], axis=0)
    a = _silu(cc).astype(BF16)
    mod = _dot(a, w_ref[...].astype(BF16)) + b_ref[...]
    for r in range(o_ref.shape[0]):
        o_ref[r] = mod[r:r + 1, :]


_S_KVA, _S_KR = Q_LORA + KV_LORA, Q_LORA + KV_LORA + DR_A


def _win_kernel(wt_ref, wina_ref, winr_ref):
    n_r = winr_ref.shape[1]
    for j in range(_S_KVA // LANES):
        wina_ref[:, j * LANES:(j + 1) * LANES] = wt_ref[j * LANES:(j + 1) * LANES, :].T.astype(BF16)
    kr = wt_ref[_S_KVA:_S_KR, :]
    a0, b0, a1, b1 = (kr[i * ROPE_FREQS:(i + 1) * ROPE_FREQS, :] for i in range(4))
    blk = jnp.concatenate([jnp.zeros((DN_A, D_MODEL), F32), -b0, a0, -b1, a1, kr], axis=0)
    wina_ref[:, _S_KVA:_S_KVA + LANES] = blk.T.astype(BF16)
    for j in range(n_r // MXU_TILE):
        rows = slice(_S_KR + j * MXU_TILE, _S_KR + (j + 1) * MXU_TILE)
        winr_ref[:, j * MXU_TILE:(j + 1) * MXU_TILE] = wt_ref[rows, :].T.astype(BF16)


def _rope_slab(s, t1, t2):
    return s * t1 + pltpu.roll(s, DR_A, 1) * t2


def _proj_kernel(x_ref, mod_ref, lng_ref, lnb_ref, wina_ref, winr_ref, gqa_ref, wq_ref, gkva_ref,
                 wkn_ref, wvt_ref, *refs, latent, n_sub):
    if latent:
        t1q_ref, t2q_ref, t1k_ref, t2k_ref, q_ref, k_ref, vt_ref, xm_ref, sza_ref, som_ref, szm_ref, h_ref = refs
    else:
        k_ref, vt_ref, xm_ref = refs
    sub = x_ref.shape[1] // n_sub
    mod = mod_ref[0]
    shift, scale1 = mod[:, 0:D_MODEL], 1.0 + mod[:, D_MODEL:2 * D_MODEL]
    ones_rows = (lax.broadcasted_iota(jnp.int32, (DV_EXT - DV_A, sub), 0) == 0).astype(BF16)

    def norm_stage(s):
        rows = slice(s * sub, (s + 1) * sub)
        h = _layer_norm(x_ref[0, rows, :]) * lng_ref[...] + lnb_ref[...]
        if latent:
            h_ref[0, rows, :] = h
        return (h * scale1 + shift).astype(BF16)

    def proj_stage(s, u):
        rows = slice(s * sub, (s + 1) * sub)
        p0 = _dot(u, wina_ref[...])
        d_0 = _dot(u, winr_ref[:, 0:512] if latent else winr_ref[:, 512:1024])
        d_om = _dot(u, winr_ref[:, 1024:1536]) if latent else None
        d_zm = _dot(u, winr_ref[:, 1536:2048]) if latent else None
        kvn = _rms_norm(p0[:, 256:384], gkva_ref[...]).astype(BF16)
        kn = _dot(kvn, wkn_ref[...])
        if latent:
            sza_ref[0, rows, :] = _silu(d_0).astype(BF16)
            krr = _rope_slab(p0[:, 384:512], t1k_ref[rows, :], t2k_ref[rows, :])
        else:
            rope_lanes = lax.broadcasted_iota(jnp.int32, (sub, SLAB), 1) >= SLAB - DR_A
            krr = jnp.where(rope_lanes, p0[:, 384:512], 0.0)
        for hh in range(H_A):
            k_ref[0, hh, rows, :] = (kn[:, hh * SLAB:(hh + 1) * SLAB] + krr).astype(BF16)
        vt = _dot_nt(wvt_ref[...], kvn).astype(BF16)
        for hh in range(H_A):
            vt_ref[0, hh, 0:DV_A, rows] = vt[hh * DV_A:(hh + 1) * DV_A, :]
            vt_ref[0, hh, DV_A:DV_EXT, rows] = ones_rows
        if not latent:
            xm_ref[0, rows, :] = d_0.astype(BF16)
        else:
            qn = _rms_norm(p0[:, 0:256], gqa_ref[...]).astype(BF16)
            qs = _dot(qn, wq_ref[...])
            d_xm = _dot(u, winr_ref[:, 512:1024])
            t1q, t2q = t1q_ref[rows, :], t2q_ref[rows, :]
            for hh in range(H_A):
                q_ref[0, hh, rows, :] = _rope_slab(qs[:, hh * SLAB:(hh + 1) * SLAB], t1q, t2q).astype(BF16)
            som_ref[0, rows, :] = jax.nn.sigmoid(d_om).astype(BF16)
            szm_ref[0, rows, :] = _silu(d_zm).astype(BF16)
            xm_ref[0, rows, :] = d_xm.astype(BF16)

    u = norm_stage(0)
    for s in range(n_sub):
        u_next = norm_stage(s + 1) if s + 1 < n_sub else None
        proj_stage(s, u)
        u = u_next


def _proj_call(x, mod3, mod_row, lng, lnb, wina, winr, gqa, wq, gkva, wkn, wvt, *tables, tm, n_sub, latent):
    b, t, _ = x.shape
    const = lambda shape: pl.BlockSpec(shape, lambda i, j: (0,) * len(shape))
    tab = pl.BlockSpec((tm, SLAB), lambda i, j: (j, 0))
    assert len(tables) == (4 if latent else 0)
    row = lambda w: pl.BlockSpec((1, tm, w), lambda i, j: (i, j, 0))
    if mod_row is None:
        mod_spec = pl.BlockSpec((1, 1, 3 * D_MODEL), lambda i, j: (i, 0, 0))
    else:
        mod_spec = pl.BlockSpec((1, 1, 3 * D_MODEL), lambda i, j: (mod_row, 0, 0))
    in_specs = [row(D_MODEL), mod_spec, const((1, D_MODEL)), const((1, D_MODEL)),
                const(wina.shape), const(winr.shape), const((1, Q_LORA)), const((Q_LORA, H_A * SLAB)),
                const((1, KV_LORA)), const((KV_LORA, H_A * SLAB)), const((W_A, KV_LORA))]
    in_specs += [tab] * len(tables)
    k_spec = pl.BlockSpec((1, H_A, tm, SLAB), lambda i, j: (i, 0, j, 0))
    vt_spec = pl.BlockSpec((1, H_A, DV_EXT, tm), lambda i, j: (i, 0, 0, j))
    k_shape = jax.ShapeDtypeStruct((b, H_A, t, SLAB), BF16)
    vt_shape = jax.ShapeDtypeStruct((b, H_A, DV_EXT, t), BF16)
    half = jax.ShapeDtypeStruct((b, t, W_B), BF16)
    if latent:
        out_specs = [k_spec, k_spec, vt_spec, row(W_B), row(W_A), row(W_B), row(W_B), row(D_MODEL)]
        out_shape = [k_shape, k_shape, vt_shape, half, half, half, half,
                     jax.ShapeDtypeStruct((b, t, D_MODEL), F32)]
    else:
        out_specs = [k_spec, vt_spec, row(W_B)]
        out_shape = [k_shape, vt_shape, half]
    return pl.pallas_call(
        functools.partial(_proj_kernel, latent=latent, n_sub=n_sub),
        grid=(b, t // tm),
        in_specs=in_specs, out_specs=out_specs, out_shape=out_shape,
        compiler_params=pltpu.CompilerParams(dimension_semantics=("parallel", "parallel"),
                                             vmem_limit_bytes=VMEM_LIMIT),
        name="in_proj_latent" if latent else "in_proj_ctx",
    )(x, mod3, lng, lnb, wina, winr, gqa, wq, gkva, wkn, wvt, *tables)


def _attn_kernel(q_ref, kc_ref, kl_ref, vtc_ref, vtl_ref, sza_ref, hs_ref, som_ref, szm_ref, xc_ref,
                 mhg_ref, skip_ref, o_ref, yb_ref, s_buf, ot_s):
    n_ctx, t = kc_ref.shape[2], kl_ref.shape[2]
    kc = ATTN_KEY_CHUNK

    def scores(h, slot):
        qh = q_ref[0, h]
        sc = _dot_nt(kc_ref[0, h], qh)
        sk = _dot_nt(kl_ref[0, h], qh)
        s_buf[slot, 0:n_ctx, :] = sc
        s_buf[slot, n_ctx:n_ctx + t, :] = sk
        return jnp.maximum(jnp.max(sc, axis=0, keepdims=True), jnp.max(sk, axis=0, keepdims=True))

    def values(h, slot, m):
        chunks = [(vtc_ref, c * kc, c * kc) for c in range(n_ctx // kc)]
        chunks += [(vtl_ref, c * kc, n_ctx + c * kc) for c in range(t // kc)]
        acc = None
        for vref, v0, s0 in chunks:
            p = jnp.exp2(s_buf[slot, s0:s0 + kc, :] - m).astype(BF16)
            d = _dot(vref[0, h, :, v0:v0 + kc], p)
            acc = d if acc is None else acc + d
        rows = pl.ds(pl.multiple_of(h * DV_A, DV_A), DV_A)
        ot_s[rows, :] = acc[0:DV_A, :] / acc[DV_A:DV_A + 1, :]

    def cell_out(pc):
        rows = pl.ds(pl.multiple_of(pc * LANES, LANES), LANES)
        yb_ref[0, rows, :] = _mlstm_out(hs_ref[0, pc], som_ref[0, rows, :], szm_ref[0, rows, :],
                                        xc_ref[0, rows, :], mhg_ref[...], skip_ref[...])

    def body(i, m_even):
        h = 2 * i
        m_odd = scores(h + 1, 1)
        cell_out(i + 1)
        values(h, 0, m_even)
        m_even = scores(h + 2, 0)
        values(h + 1, 1, m_odd)
        return m_even

    assert hs_ref.shape[1] == H_A // 2
    m_first = scores(0, 0)
    cell_out(0)
    m_even = lax.fori_loop(0, H_A // 2 - 1, body, m_first)
    m_odd = scores(H_A - 1, 1)
    values(H_A - 2, 0, m_even)
    values(H_A - 1, 1, m_odd)
    o_ref[0] = (ot_s[...].T * sza_ref[0].astype(F32)).astype(BF16)


def _attn_call(q, kc, kl, vtc, vtl, sza, hs, som, szm, xc, mhg, skip, *, tq):
    b, _, t, _ = q.shape
    n_ctx = kc.shape[2] // b
    head_blk = lambda n, w: pl.BlockSpec((1, H_A, n, w), lambda i, j: (i, 0, 0, 0))
    const = lambda shape: pl.BlockSpec(shape, lambda i, j: (0,) * len(shape))
    row = lambda w: pl.BlockSpec((1, tq, w), lambda i, j: (i, j, 0))
    half = jax.ShapeDtypeStruct((b, t, W_A), BF16)
    return pl.pallas_call(
        _attn_kernel,
        grid=(b, t // tq),
        in_specs=[pl.BlockSpec((1, H_A, tq, SLAB), lambda i, j: (i, 0, j, 0)),
                  pl.BlockSpec((1, H_A, n_ctx, SLAB), lambda i, j: (0, 0, i, 0)), head_blk(t, SLAB),
                  pl.BlockSpec((1, H_A, DV_EXT, n_ctx), lambda i, j: (0, 0, 0, i)), head_blk(DV_EXT, t),
                  row(W_A),
                  pl.BlockSpec((1, tq // LANES, W_B, LANES), lambda i, j: (i, j, 0, 0)),
                  row(W_B), row(W_B), row(W_B), const((1, W_B)), const((1, W_B))],
        out_specs=[row(W_A), row(W_B)],
        out_shape=[half, jax.ShapeDtypeStruct((b, t, W_B), BF16)],
        scratch_shapes=[pltpu.VMEM((2, n_ctx + t, tq), F32),
                        pltpu.VMEM((W_A, tq), F32)],
        compiler_params=pltpu.CompilerParams(dimension_semantics=("parallel", "parallel"),
                                             vmem_limit_bytes=VMEM_LIMIT),
        name="mla_attention",
    )(q, kc, kl, vtc, vtl, sza, hs, som, szm, xc, mhg, skip)


def _onehot(cond):
    return cond.astype(F32).astype(BF16)


def _block_diag(a):
    bs, nb = QKV_BS, W_B // QKV_BS
    lb, sq = bs.bit_length() - 1, bs * bs
    assert bs == 1 << lb
    iota = lambda shape, d: lax.broadcasted_iota(jnp.int32, shape, d)
    spread = _onehot(iota((nb, W_B), 1) >> lb == iota((nb, W_B), 0))
    b = _dot(a, spread)
    col_o = iota((sq, W_B), 1) & (bs - 1)
    bm = jnp.concatenate([jnp.where(col_o == o, b, 0.0) for o in range(bs)], axis=0).astype(BF16)
    j = iota((W_B, bs * sq), 1)
    pick = _onehot((j & (sq - 1)) == ((iota((W_B, bs * sq), 0) & (bs - 1)) << lb) + (j >> (2 * lb)))
    same_block = iota((W_B, W_B), 0) >> lb == iota((W_B, W_B), 1) >> lb
    return jnp.where(same_block, _dot(pick, bm), 0.0).astype(BF16)


def _mlstm_weight_kernel(aq_ref, ak_ref, av_ref, wgt_ref, bgate_ref,
                         wmq_ref, wmk_ref, wmv_ref, wgc_ref, wgm_ref, bg_ref):
    a_q, a_k, a_v = aq_ref[...], ak_ref[...], av_ref[...]
    wq, wk, wv = (_block_diag(a.astype(BF16)) for a in (a_q, a_k, a_v))
    wmq_ref[...] = wq
    wmk_ref[...] = _block_diag((a_k * (DH_B ** -0.5)).astype(BF16))
    wmv_ref[...] = wv
    dst, src = (lax.broadcasted_iota(jnp.int32, (LANES, 2 * N_HD), d) for d in (0, 1))
    grp = dst >> (H_B.bit_length() - 1)
    src_of_dst = (dst & (H_B - 1)) + H_B * jnp.where(grp == 1, 2, jnp.where(grp == 2, 1, grp))
    perm = _onehot((dst < 2 * N_HD) & (src == src_of_dst))
    wg_t = _dot(perm, wgt_ref[...].astype(BF16)).astype(BF16)
    wgc_ref[...] = (_dot_nt(wq, wg_t[:, 0:W_B]) + _dot_nt(wk, wg_t[:, W_B:2 * W_B])).astype(BF16)
    wgm_ref[...] = _dot_nt(wv, wg_t[:, 2 * W_B:3 * W_B]).astype(BF16)
    bg_ref[...] = jnp.zeros(bg_ref.shape, F32)
    for d, s in enumerate((0, 2, 1, 3)):
        bg_ref[:, d * H_B:(d + 1) * H_B] = bgate_ref[:, s * H_B:(s + 1) * H_B]


def _qb_kernel(wqb_ref, wq_ref):
    wt = wqb_ref[...].T
    hw = DN_A + DR_A
    for hh in range(H_A):
        rope = wt[hh * hw + DN_A:(hh + 1) * hw, :]
        a0, b0, a1, b1 = (rope[i * ROPE_FREQS:(i + 1) * ROPE_FREQS, :] for i in range(4))
        slab = jnp.concatenate([wt[hh * hw:hh * hw + DN_A, :], -b0, a0, -b1, a1, rope], axis=0)
        wq_ref[:, hh * SLAB:(hh + 1) * SLAB] = slab.T.astype(BF16)


def _kvb_kernel(wkvb_ref, wkn_ref, wvt_ref):
    w = wkvb_ref[...]
    assert DN_A + DV_A == SLAB
    lane = lax.broadcasted_iota(jnp.int32, w.shape, 1)
    wkn_ref[...] = jnp.where((lane & (SLAB - 1)) < DN_A, w, 0.0).astype(BF16)
    wt = w.T
    for hh in range(H_A):
        wvt_ref[hh * DV_A:(hh + 1) * DV_A, :] = wt[hh * SLAB + DN_A:(hh + 1) * SLAB, :].astype(BF16)


def _prep_kernel(c_ref, cctx_ref, wada_ref, bada_ref, wt_ref, aq_ref, ak_ref, av_ref, wgt_ref, bgate_ref,
                 wqb_ref, wkvb_ref,
                 mod_ref, wina_ref, winr_ref, wmq_ref, wmk_ref, wmv_ref, wgc_ref, wgm_ref, bg_ref,
                 wq_ref, wkn_ref, wvt_ref):
    _ada_kernel(c_ref, cctx_ref, wada_ref, bada_ref, mod_ref)
    _win_kernel(wt_ref, wina_ref, winr_ref)
    _mlstm_weight_kernel(aq_ref, ak_ref, av_ref, wgt_ref, bgate_ref,
                         wmq_ref, wmk_ref, wmv_ref, wgc_ref, wgm_ref, bg_ref)
    _qb_kernel(wqb_ref, wq_ref)
    _kvb_kernel(wkvb_ref, wkn_ref, wvt_ref)


def _prep_call(c, c_ctx, w_ada, b_ada, wt, aq, ak, av, wgt, b_gate, w_qb, w_kvb):
    n_in, d = wt.shape
    assert c.shape[0] % SUBLANES == 0 and wgt.shape == (2 * N_HD, 3 * W_B)
    assert aq.shape == ak.shape == av.shape == (QKV_BS * QKV_BS, W_B // QKV_BS)
    bf = lambda *shape: jax.ShapeDtypeStruct(shape, BF16)
    return pl.pallas_call(
        _prep_kernel,
        out_shape=[jax.ShapeDtypeStruct((c.shape[0] + 1, 1, w_ada.shape[1]), F32),
                   bf(d, _S_KVA + LANES), bf(d, n_in - _S_KR),
                   bf(W_B, W_B), bf(W_B, W_B), bf(W_B, W_B), bf(W_B, LANES), bf(W_B, LANES),
                   jax.ShapeDtypeStruct((1, LANES), F32),
                   bf(Q_LORA, H_A * SLAB), bf(KV_LORA, H_A * SLAB), bf(W_A, KV_LORA)],
        compiler_params=pltpu.CompilerParams(vmem_limit_bytes=VMEM_LIMIT),
        name="param_prep",
    )(c, c_ctx, w_ada, b_ada, wt, aq, ak, av, wgt, b_gate, w_qb, w_kvb)


_PAD = SUBLANES


def _lane_scans(rows_per_chunk, combine, fill, out):
    npc = CHUNK // LANES
    flat = [x[:, i * LANES:(i + 1) * LANES] for x in rows_per_chunk for i in range(npc)]
    lane = lax.broadcasted_iota(jnp.int32, flat[0].shape, 1)
    pre, suf, sh = list(flat), list(flat), 1
    while sh < LANES:
        pre = [combine(p, jnp.where(lane >= sh, pltpu.roll(p, sh, 1), fill)) for p in pre]
        suf = [combine(s, jnp.where(lane < LANES - sh, pltpu.roll(s, LANES - sh, 1), fill)) for s in suf]
        sh *= 2
        yield
    for c in range(len(rows_per_chunk)):
        p, s = pre[c * npc:(c + 1) * npc], suf[c * npc:(c + 1) * npc]
        tot = [x[:, LANES - 1:LANES] for x in p]
        run = None
        for i in range(npc):
            if run is not None:
                p[i] = combine(p[i], run)
            run = tot[i] if run is None else combine(run, tot[i])
        run = None
        for i in reversed(range(npc)):
            if run is not None:
                s[i] = combine(s[i], run)
            run = tot[i] if run is None else combine(run, tot[i])
        out.append((jnp.concatenate(p, axis=1), jnp.concatenate(s, axis=1)))


def _gate_tables(gts, dests):
    L = CHUNK
    fwd_rows = lax.broadcasted_iota(jnp.int32, (N_HD, L), 0) < H_B
    lis = [gt[0:N_HD, :] for gt in gts]
    sums = []
    yield from _lane_scans([_log_sigmoid(gt[N_HD:2 * N_HD, :]) for gt in gts], jnp.add, 0.0, sums)
    cums = [jnp.where(fwd_rows, ps, ss) for ps, ss in sums]
    rs = [li - cum for li, cum in zip(lis, cums)]
    maxs = []
    yield from _lane_scans(rs, jnp.maximum, -jnp.inf, maxs)
    for (grow_ref, c, gcol_ref, r0), cum, r, (pm, sm) in zip(dests, cums, rs, maxs):
        grow_ref[c] = jnp.concatenate([cum, r, jnp.where(fwd_rows, pm, sm)], axis=0)
        gcol_ref[r0:r0 + L, :] = jnp.concatenate([r, jnp.zeros((LANES - N_HD, L), F32)], axis=0).T


def _fill_pad(pad_s, xm_ref, n):
    zrow = jnp.zeros((_PAD, W_B), F32)
    pad_s[0:_PAD, :] = zrow
    pad_s[_PAD:_PAD + n, :] = xm_ref[...].astype(F32)
    pad_s[_PAD + n:2 * _PAD + n, :] = zrow


def _conv_gate_stage(c, pad_s, xm_ref, cw_ref, cb_ref, wgc_ref, wgm_ref, bg_ref, xc_out, xcb_s):
    L = CHUNK
    rows = slice(c * L, (c + 1) * L)
    r0 = c * L + _PAD
    row_id = lax.broadcasted_iota(jnp.int32, (L, W_B), 0)
    xcur = pad_s[r0:r0 + L, :]
    xprev = jnp.where(row_id == 0, pad_s[r0 - 1:r0, :], pltpu.roll(xcur, 1, 0))
    xnext = jnp.where(row_id == L - 1, pad_s[r0 + L:r0 + L + 1, :], pltpu.roll(xcur, L - 1, 0))
    pre = cb_ref[...] + xprev * cw_ref[0:1, :] + xcur * cw_ref[1:2, :] + xnext * cw_ref[2:3, :]
    xc = _silu(pre)
    xcb = xc.astype(BF16)
    if xc_out is not None:
        xc_out[rows, :] = xcb
    xcb_s[rows, :] = xcb
    g = _dot(xcb, wgc_ref[...]) + _dot(xm_ref[rows, :], wgm_ref[...]) + bg_ref[...]
    return g.T[0:2 * N_HD, :]


def _headwise_stage(c, xm_ref, xcb_s, wq_ref, wk_ref, wv_ref, q_s, k_s, vt_s):
    L = CHUNK
    rows = slice(c * L, (c + 1) * L)
    for hf in range(W_B // MXU_TILE):
        cs = slice(hf * MXU_TILE, (hf + 1) * MXU_TILE)
        xcb = xcb_s[rows, cs]
        if q_s is not None:
            q_s[rows, cs] = _dot(xcb, wq_ref[cs, cs]).astype(BF16)
        k_s[rows, cs] = _dot(xcb, wk_ref[cs, cs]).astype(BF16)
        vt_s[c, cs, :] = _dot(xm_ref[rows, cs], wv_ref[cs, cs]).T.astype(BF16)


_TAB_ROWS = 5 * N_HD


def _chain_tables(growc_s, grow_s, tab_s, nc):
    L = CHUNK
    fwd_rows = lax.broadcasted_iota(jnp.int32, (N_HD, L), 0) < H_B
    gc = growc_s[0]
    steps = [(gc, gc)] + [(grow_s[i], grow_s[nc - 1 - i]) for i in range(nc)]
    parts = []
    for g_f, g_b in steps:
        pick = lambda a: jnp.where(fwd_rows, g_f[a * N_HD:(a + 1) * N_HD, :], g_b[a * N_HD:(a + 1) * N_HD, :])
        cum, r, pm = pick(0), pick(1), pick(2)
        b_end = jnp.where(fwd_rows[:, 0:1], cum[:, L - 1:L], cum[:, 0:1])
        w = r + b_end
        parts.append((cum, pm, b_end, w, jnp.max(w, axis=1, keepdims=True)))
    m0 = jnp.zeros((N_HD, 1), F32)
    for idx, (cum, pm, b_end, w, wmax) in enumerate(parts):
        m_new = jnp.maximum(b_end + m0, wmax)
        mu = jnp.maximum(m0, pm)
        decay = jnp.broadcast_to(jnp.exp(b_end + m0 - m_new), (N_HD, L))
        tab_s[idx] = jnp.concatenate([mu, jnp.exp(m0 - mu), jnp.exp(-(cum + mu)),
                                      jnp.exp(w - m_new), decay], axis=0)
        m0 = m_new


def _tab(tab, a, hd):
    return tab[a * N_HD + hd:a * N_HD + hd + 1, :]


def _state_update(st_ref, hd, vt_ext, k_c, ws_row, decay_hd):
    vw = (vt_ext.astype(F32) * ws_row).astype(BF16)
    st_ref[hd] = decay_hd * st_ref[hd] + _dot(vw, k_c)


def _mlstm_kernel(xm_ref, xmc_ref, cw_ref, cb_ref, wq_ref, wk_ref, wv_ref, wgc_ref, wgm_ref, bg_ref,
                  hs_ref, xc_ref,
                  pad_s, xcb_s, q_s, k_s, vt_s, gcol_s, grow_s,
                  padc_s, xcbc_s, kc_s, vtc_s, gcolc_s, growc_s, tab_s, st_s):
    L = CHUNK
    t = xm_ref.shape[1]
    nc = t // L
    row_i = lax.broadcasted_iota(jnp.int32, (L, L), 0)
    col_i = lax.broadcasted_iota(jnp.int32, (L, L), 1)
    tri = (row_i <= col_i, row_i >= col_i)
    ones_rows = (lax.broadcasted_iota(jnp.int32, (ST_ROWS - DH_B, L), 0) == 0).astype(BF16)
    hsl = lambda hh: slice(hh * DH_B, (hh + 1) * DH_B)
    xm, xmc = xm_ref.at[0], xmc_ref.at[0]
    gate_w = (cw_ref, cb_ref, wgc_ref, wgm_ref, bg_ref)

    _fill_pad(padc_s, xmc, CTX_LEN)
    _fill_pad(pad_s, xm, t)
    gts = [_conv_gate_stage(0, padc_s, xmc, *gate_w, None, xcbc_s)]
    gts += [_conv_gate_stage(c, pad_s, xm, *gate_w, xc_ref.at[0], xcb_s) for c in range(nc)]
    dests = [(growc_s, 0, gcolc_s, 0)] + [(grow_s, c, gcol_s, c * L) for c in range(nc)]
    tables = _gate_tables(gts, dests)
    _headwise_stage(0, xmc, xcbc_s, wq_ref, wk_ref, wv_ref, None, kc_s, vtc_s)
    for c in range(nc):
        next(tables, None)
        next(tables, None)
        _headwise_stage(c, xm, xcb_s, wq_ref, wk_ref, wv_ref, q_s, k_s, vt_s)
    for _ in tables:
        pass

    _chain_tables(growc_s, grow_s, tab_s, nc)
    st_s[...] = jnp.zeros(st_s.shape, F32)
    tab = tab_s[0]
    for hd in range(N_HD):
        hh = hd % H_B
        vt_ext = jnp.concatenate([vtc_s[0, hsl(hh), :], ones_rows], axis=0)
        _state_update(st_s, hd, vt_ext, kc_s[:, hsl(hh)], _tab(tab, 3, hd), _tab(tab, 4, hd)[:, 0:DH_B])

    def body(i, carry, *, accumulate):
        ci = (i, nc - 1 - i)
        rows = tuple(pl.ds(pl.multiple_of(c * L, L), L) for c in ci)
        tab = tab_s[i + 1]
        gcols = tuple(gcol_s[r, :] for r in rows)
        live = {}

        def stage_scores(hd):
            d, hh = hd // H_B, hd % H_B
            k_c = k_s[rows[d], hsl(hh)]
            vt_ext = jnp.concatenate([vt_s[ci[d], hsl(hh), :], ones_rows], axis=0)
            lhs = jnp.concatenate([k_c, st_s[hd].astype(BF16)], axis=0)
            live[hd] = (k_c, vt_ext, _dot_nt(lhs, q_s[rows[d], hsl(hh)]))

        def stage_gate(hd):
            d = hd // H_B
            k_c, vt_ext, res = live[hd]
            rcol = gcols[d][:, hd:hd + 1]
            e = jnp.exp(jnp.where(tri[d], rcol - _tab(tab, 0, hd), -jnp.inf))
            p = (res[0:L, :] * e).astype(BF16)
            _state_update(st_s, hd, vt_ext, k_c, _tab(tab, 3, hd), _tab(tab, 4, hd)[:, 0:DH_B])
            live[hd] = (vt_ext, res[L:L + ST_ROWS, :], p)

        def stage_out(hd):
            d, hh = hd // H_B, hd % H_B
            vt_ext, inter, p = live.pop(hd)
            tot = _tab(tab, 1, hd) * inter + _dot(vt_ext, p)
            den = tot[DH_B:DH_B + 1, :]
            h_t = tot[0:DH_B, :] / jnp.maximum(jnp.abs(den), _tab(tab, 2, hd))
            for pc in range(L // LANES):
                piece = h_t[:, pc * LANES:(pc + 1) * LANES]
                if accumulate:
                    hs_ref[0, ci[d] * (L // LANES) + pc, hsl(hh), :] += piece
                else:
                    hs_ref[0, ci[d] * (L // LANES) + pc, hsl(hh), :] = piece

        stages = (stage_scores, stage_gate, stage_out)
        for step in range(N_HD + len(stages) - 1):
            for si in range(len(stages)):
                if 0 <= step - si < N_HD:
                    stages[si](step - si)
        return carry

    lax.fori_loop(0, nc // 2, functools.partial(body, accumulate=False), 0)
    lax.fori_loop(nc // 2, nc, functools.partial(body, accumulate=True), 0)


def _mlstm_out(hs_t, som, szm, xc, mhg, skip):
    hsum = hs_t.T * som.astype(F32)
    parts = [_layer_norm(hsum[:, hh * DH_B:(hh + 1) * DH_B]) for hh in range(H_B)]
    hb = jnp.concatenate(parts, axis=1) * mhg + skip * xc.astype(F32)
    return (hb * szm.astype(F32)).astype(BF16)


def _mlstm_call(xm, xmc, conv_w, conv_b, wq, wk, wv, wgc, wgm, bg):
    b, t, _ = xm.shape
    nc = t // CHUNK
    assert nc % 2 == 0
    seq = lambda n: pl.BlockSpec((1, n, W_B), lambda i: (i, 0, 0))
    const = lambda shape: pl.BlockSpec(shape, lambda i: (0,) * len(shape))
    scratch = [
        pltpu.VMEM((t + 2 * _PAD, W_B), F32),
        pltpu.VMEM((t, W_B), BF16),
        pltpu.VMEM((t, W_B), BF16),
        pltpu.VMEM((t, W_B), BF16),
        pltpu.VMEM((nc, W_B, CHUNK), BF16),
        pltpu.VMEM((t, LANES), F32),
        pltpu.VMEM((nc, 3 * N_HD, CHUNK), F32),
        pltpu.VMEM((CTX_LEN + 2 * _PAD, W_B), F32),
        pltpu.VMEM((CTX_LEN, W_B), BF16),
        pltpu.VMEM((CTX_LEN, W_B), BF16),
        pltpu.VMEM((1, W_B, CHUNK), BF16),
        pltpu.VMEM((CTX_LEN, LANES), F32),
        pltpu.VMEM((1, 3 * N_HD, CHUNK), F32),
        pltpu.VMEM((nc + 1, _TAB_ROWS, CHUNK), F32),
        pltpu.VMEM((N_HD, ST_ROWS, DH_B), F32),
    ]
    return pl.pallas_call(
        _mlstm_kernel,
        grid=(b,),
        in_specs=[seq(t), seq(CTX_LEN), const((3, W_B)), const((1, W_B)),
                  const((W_B, W_B)), const((W_B, W_B)), const((W_B, W_B)),
                  const((W_B, LANES)), const((W_B, LANES)), const((1, LANES))],
        out_specs=[pl.BlockSpec((1, t // LANES, W_B, LANES), lambda i: (i, 0, 0, 0)), seq(t)],
        out_shape=[jax.ShapeDtypeStruct((b, t // LANES, W_B, LANES), F32),
                   jax.ShapeDtypeStruct((b, t, W_B), BF16)],
        scratch_shapes=scratch,
        compiler_params=pltpu.CompilerParams(dimension_semantics=("arbitrary",),
                                             vmem_limit_bytes=VMEM_LIMIT),
        name="mlstm_bidir",
    )(xm, xmc, conv_w, conv_b, wq, wk, wv, wgc, wgm, bg)


def _out_kernel(h_ref, mod_ref, ya_ref, yb_ref, wo_ref, g_ref, b_ref, o_ref, *, n_sub):
    sub = h_ref.shape[1] // n_sub
    gate = mod_ref[0][:, 2 * D_MODEL:3 * D_MODEL]

    def mix_stage(s):
        rows = slice(s * sub, (s + 1) * sub)
        return (_dot(ya_ref[0, rows, :], wo_ref[0:W_A, :])
                + _dot(yb_ref[0, rows, :], wo_ref[W_A:W_A + W_B, :]))

    def norm_stage(s, y):
        rows = slice(s * sub, (s + 1) * sub)
        o_ref[0, rows, :] = _layer_norm(ALPHA * h_ref[0, rows, :] + gate * y) * g_ref[...] + b_ref[...]

    y = mix_stage(0)
    for s in range(n_sub):
        y_next = mix_stage(s + 1) if s + 1 < n_sub else None
        norm_stage(s, y)
        y = y_next


def _out_call(h, mod3, ya, yb, wo, g, bb, *, tm, n_sub):
    b, t, _ = h.shape
    const = lambda shape: pl.BlockSpec(shape, lambda i, j: (0,) * len(shape))
    row = lambda w: pl.BlockSpec((1, tm, w), lambda i, j: (i, j, 0))
    return pl.pallas_call(
        functools.partial(_out_kernel, n_sub=n_sub),
        grid=(b, t // tm),
        in_specs=[row(D_MODEL), pl.BlockSpec((1, 1, 3 * D_MODEL), lambda i, j: (i, 0, 0)),
                  row(W_A), row(W_B),
                  const((W_A + W_B, D_MODEL)), const((1, D_MODEL)), const((1, D_MODEL))],
        out_specs=row(D_MODEL),
        out_shape=jax.ShapeDtypeStruct((b, t, D_MODEL), F32),
        compiler_params=pltpu.CompilerParams(dimension_semantics=("parallel", "parallel"),
                                             vmem_limit_bytes=VMEM_LIMIT),
        name="out_proj_ln",
    )(h, mod3, ya, yb, wo, g, bb)


def _rope_tables(seq, scale_keep, scale_rope):
    n_rows = seq // GRID_W
    rowp = np.repeat(np.arange(n_rows, dtype=np.float32), GRID_W)
    colp = np.tile(np.arange(GRID_W, dtype=np.float32), n_rows)
    inv = (np.float32(ROPE_BASE) ** (-np.arange(ROPE_FREQS, dtype=np.float32) / np.float32(ROPE_FREQS)))
    ang = np.stack([rowp[:, None] * inv, colp[:, None] * inv], axis=1).astype(np.float32)
    cos = np.broadcast_to(np.cos(ang)[:, :, None, :], (seq, 2, 2, ROPE_FREQS)).reshape(seq, DR_A)
    sin = np.broadcast_to(np.sin(ang)[:, :, None, :], (seq, 2, 2, ROPE_FREQS)).reshape(seq, DR_A)
    z32 = np.zeros((seq, DR_A), np.float32)
    t1 = np.concatenate([np.full((seq, DN_A), scale_keep, np.float32), z32, cos * scale_rope], axis=1)
    t2 = np.concatenate([np.zeros((seq, DN_A), np.float32), z32, sin * scale_rope], axis=1)
    return jnp.asarray(t1, F32), jnp.asarray(t2, F32)


def kernel(x, c, ctx, c_ctx, ln_in_g, ln_in_b, w_ada, b_ada, w_in, g_qa, w_qb, g_kva, w_kvb, conv_w, conv_b, w_mq, w_mk, w_mv, w_gate, b_gate, mh_g, skip, w_out, ln_g, ln_b):
    b, t, _ = x.shape
    l = 0
    r2 = lambda v: v.reshape(1, -1)

    blocks = lambda w: jnp.transpose(w[l], (1, 2, 0)).reshape(QKV_BS * QKV_BS, -1)
    wo = w_out[l].astype(BF16)

    sm_scale = (DN_A + DR_A) ** -0.5 * LOG2_E
    t1q, t2q = _rope_tables(t, sm_scale, sm_scale)
    t1k, t2k = _rope_tables(t, 0.0, 1.0)
    n_ctx = ctx.shape[1]

    mod3, wina, winr, wmq, wmk_s, wmv, wgc, wgm, bg, wq, wkn, wvt = _prep_call(
        c, r2(c_ctx), w_ada[l], r2(b_ada[l]), jnp.swapaxes(w_in[l], 0, 1), blocks(w_mq), blocks(w_mk),
        blocks(w_mv), jnp.swapaxes(w_gate[l], 0, 1), r2(b_gate[l]), w_qb[l], w_kvb[l])

    lng, lnb = r2(ln_in_g), r2(ln_in_b)
    shared = (wina, winr, r2(g_qa[l]), wq, r2(g_kva[l]), wkn, wvt)
    q, kl, vtl, xm, sza, som, szm, h = _proj_call(x, mod3, None, lng, lnb, *shared, t1q, t2q, t1k, t2k,
                                                  tm=1024, n_sub=4, latent=True)
    kc, vtc, xmc = _proj_call(ctx.reshape(1, b * n_ctx, D_MODEL), mod3, b, lng, lnb, *shared,
                              tm=min(1024, b * n_ctx), n_sub=2, latent=False)
    xmc = xmc.reshape(b, n_ctx, W_B)

    hs, xc = _mlstm_call(xm, xmc, conv_w[l], r2(conv_b[l]), wmq, wmk_s, wmv, wgc, wgm, bg)
    ya, yb = _attn_call(q, kc, kl, vtc, vtl, sza, hs, som, szm, xc, r2(mh_g[l]), r2(skip[l]), tq=512)
    return _out_call(h, mod3, ya, yb, wo, r2(ln_g[l]), r2(ln_b[l]), tm=1024, n_sub=4)
```

```python
import functools

import numpy as np
import jax
import jax.numpy as jnp
from jax import lax
from jax.experimental import pallas as pl
from jax.experimental.pallas import tpu as pltpu

F32 = jnp.float32
BF16 = jnp.bfloat16

D_MODEL = 1024
CTX_LEN = 256
GRID_W = 64
H_A, DN_A, DR_A, DV_A = 8, 64, 32, 64
W_A = H_A * DV_A
Q_LORA, KV_LORA = 256, 128
ROPE_FREQS = DR_A // 4
ROPE_BASE = 10000.0
H_B, DH_B = 4, 128
W_B = H_B * DH_B
QKV_BS = 4
DEPTH = 1
ALPHA = (2.0 * DEPTH) ** 0.25
LN_EPS = 1e-5
RMS_EPS = 1e-6
LOG2_E = 1.4426950408889634

LANES = 128
SUBLANES = 8
MXU_TILE = 256
SLAB = LANES
DV_EXT = DV_A + 16
ATTN_KEY_CHUNK = MXU_TILE
CHUNK = 256
N_HD = 2 * H_B
ST_ROWS = DH_B + 16
VMEM_LIMIT = 56 * 1024 * 1024

_NT = (((1,), (1,)), ((), ()))


def _dot(a, b):
    return jnp.dot(a, b, preferred_element_type=F32)


def _dot_nt(a, b):
    return lax.dot_general(a, b, _NT, preferred_element_type=F32)


def _layer_norm(x):
    mu = jnp.mean(x, axis=-1, keepdims=True)
    xc = x - mu
    var = jnp.mean(xc * xc, axis=-1, keepdims=True)
    return xc * lax.rsqrt(var + LN_EPS)


def _rms_norm(x, g):
    return (x * lax.rsqrt(jnp.mean(x * x, axis=-1, keepdims=True) + RMS_EPS)) * g


def _silu(x):
    return x * jax.nn.sigmoid(x)


def _log_sigmoid(x):
    return jnp.minimum(x, 0.0) - jnp.log1p(jnp.exp(-jnp.abs(x)))


def _ada_kernel(c_ref, cctx_ref, w_ref, b_ref, o_ref):
    first = lax.broadcasted_iota(jnp.int32, (SUBLANES, D_MODEL), 0) == 0
    cc = jnp.concatenate([c_ref[...], jnp.where(first, cctx_ref[...], 0.0)], axis=0)
    a = _silu(cc).astype(BF16)
    mod = _dot(a, w_ref[...].astype(BF16)) + b_ref[...]
    for r in range(o_ref.shape[0]):
        o_ref[r] = mod[r:r + 1, :]


_S_KVA, _S_KR = Q_LORA + KV_LORA, Q_LORA + KV_LORA + DR_A


def _win_kernel(wt_ref, wina_ref, winr_ref):
    n_r = winr_ref.shape[1]
    for j in range(_S_KVA // LANES):
        wina_ref[:, j * LANES:(j + 1) * LANES] = wt_ref[j * LANES:(j + 1) * LANES, :].T.astype(BF16)
    kr = wt_ref[_S_KVA:_S_KR, :]
    a0, b0, a1, b1 = (kr[i * ROPE_FREQS:(i + 1) * ROPE_FREQS, :] for i in range(4))
    blk = jnp.concatenate([jnp.zeros((DN_A, D_MODEL), F32), -b0, a0, -b1, a1, kr], axis=0)
    wina_ref[:, _S_KVA:_S_KVA + LANES] = blk.T.astype(BF16)
    for j in range(n_r // MXU_TILE):
        rows = slice(_S_KR + j * MXU_TILE, _S_KR + (j + 1) * MXU_TILE)
        winr_ref[:, j * MXU_TILE:(j + 1) * MXU_TILE] = wt_ref[rows, :].T.astype(BF16)


def _rope_slab(s, t1, t2):
    return s * t1 + pltpu.roll(s, DR_A, 1) * t2


def _proj_kernel(x_ref, mod_ref, lng_ref, lnb_ref, wina_ref, winr_ref, gqa_ref, wq_ref, gkva_ref,
                 wkn_ref, wvt_ref, *refs, latent, n_sub):
    if latent:
        t1q_ref, t2q_ref, t1k_ref, t2k_ref, q_ref, k_ref, vt_ref, xm_ref, sza_ref, som_ref, szm_ref, h_ref = refs
    else:
        k_ref, vt_ref, xm_ref = refs
    sub = x_ref.shape[1] // n_sub
    mod = mod_ref[0]
    shift, scale1 = mod[:, 0:D_MODEL], 1.0 + mod[:, D_MODEL:2 * D_MODEL]
    ones_rows = (lax.broadcasted_iota(jnp.int32, (DV_EXT - DV_A, sub), 0) == 0).astype(BF16)

    def norm_stage(s):
        rows = slice(s * sub, (s + 1) * sub)
        h = _layer_norm(x_ref[0, rows, :]) * lng_ref[...] + lnb_ref[...]
        if latent:
            h_ref[0, rows, :] = h
        return (h * scale1 + shift).astype(BF16)

    def proj_stage(s, u):
        rows = slice(s * sub, (s + 1) * sub)
        p0 = _dot(u, wina_ref[...])
        d_0 = _dot(u, winr_ref[:, 0:512] if latent else winr_ref[:, 512:1024])
        d_om = _dot(u, winr_ref[:, 1024:1536]) if latent else None
        d_zm = _dot(u, winr_ref[:, 1536:2048]) if latent else None
        kvn = _rms_norm(p0[:, 256:384], gkva_ref[...]).astype(BF16)
        kn = _dot(kvn, wkn_ref[...])
        if latent:
            sza_ref[0, rows, :] = _silu(d_0).astype(BF16)
            krr = _rope_slab(p0[:, 384:512], t1k_ref[rows, :], t2k_ref[rows, :])
        else:
            rope_lanes = lax.broadcasted_iota(jnp.int32, (sub, SLAB), 1) >= SLAB - DR_A
            krr = jnp.where(rope_lanes, p0[:, 384:512], 0.0)
        for hh in range(H_A):
            k_ref[0, hh, rows, :] = (kn[:, hh * SLAB:(hh + 1) * SLAB] + krr).astype(BF16)
        vt = _dot_nt(wvt_ref[...], kvn).astype(BF16)
        for hh in range(H_A):
            vt_ref[0, hh, 0:DV_A, rows] = vt[hh * DV_A:(hh + 1) * DV_A, :]
            vt_ref[0, hh, DV_A:DV_EXT, rows] = ones_rows
        if not latent:
            xm_ref[0, rows, :] = d_0.astype(BF16)
        else:
            qn = _rms_norm(p0[:, 0:256], gqa_ref[...]).astype(BF16)
            qs = _dot(qn, wq_ref[...])
            d_xm = _dot(u, winr_ref[:, 512:1024])
            t1q, t2q = t1q_ref[rows, :], t2q_ref[rows, :]
            for hh in range(H_A):
                q_ref[0, hh, rows, :] = _rope_slab(qs[:, hh * SLAB:(hh + 1) * SLAB], t1q, t2q).astype(BF16)
            som_ref[0, rows, :] = jax.nn.sigmoid(d_om).astype(BF16)
            szm_ref[0, rows, :] = _silu(d_zm).astype(BF16)
            xm_ref[0, rows, :] = d_xm.astype(BF16)

    u = norm_stage(0)
    for s in range(n_sub):
        u_next = norm_stage(s + 1) if s + 1 < n_sub else None
        proj_stage(s, u)
        u = u_next


def _proj_call(x, mod3, mod_row, lng, lnb, wina, winr, gqa, wq, gkva, wkn, wvt, *tables, tm, n_sub, latent):
    b, t, _ = x.shape
    const = lambda shape: pl.BlockSpec(shape, lambda i, j: (0,) * len(shape))
    tab = pl.BlockSpec((tm, SLAB), lambda i, j: (j, 0))
    assert len(tables) == (4 if latent else 0)
    row = lambda w: pl.BlockSpec((1, tm, w), lambda i, j: (i, j, 0))
    if mod_row is None:
        mod_spec = pl.BlockSpec((1, 1, 3 * D_MODEL), lambda i, j: (i, 0, 0))
    else:
        mod_spec = pl.BlockSpec((1, 1, 3 * D_MODEL), lambda i, j: (mod_row, 0, 0))
    in_specs = [row(D_MODEL), mod_spec, const((1, D_MODEL)), const((1, D_MODEL)),
                const(wina.shape), const(winr.shape), const((1, Q_LORA)), const((Q_LORA, H_A * SLAB)),
                const((1, KV_LORA)), const((KV_LORA, H_A * SLAB)), const((W_A, KV_LORA))]
    in_specs += [tab] * len(tables)
    k_spec = pl.BlockSpec((1, H_A, tm, SLAB), lambda i, j: (i, 0, j, 0))
    vt_spec = pl.BlockSpec((1, H_A, DV_EXT, tm), lambda i, j: (i, 0, 0, j))
    k_shape = jax.ShapeDtypeStruct((b, H_A, t, SLAB), BF16)
    vt_shape = jax.ShapeDtypeStruct((b, H_A, DV_EXT, t), BF16)
    half = jax.ShapeDtypeStruct((b, t, W_B), BF16)
    if latent:
        out_specs = [k_spec, k_spec, vt_spec, row(W_B), row(W_A), row(W_B), row(W_B), row(D_MODEL)]
        out_shape = [k_shape, k_shape, vt_shape, half, half, half, half,
                     jax.ShapeDtypeStruct((b, t, D_MODEL), F32)]
    else:
        out_specs = [k_spec, vt_spec, row(W_B)]
        out_shape = [k_shape, vt_shape, half]
    return pl.pallas_call(
        functools.partial(_proj_kernel, latent=latent, n_sub=n_sub),
        grid=(b, t // tm),
        in_specs=in_specs, out_specs=out_specs, out_shape=out_shape,
        compiler_params=pltpu.CompilerParams(dimension_semantics=("parallel", "parallel"),
                                             vmem_limit_bytes=VMEM_LIMIT),
        name="in_proj_latent" if latent else "in_proj_ctx",
    )(x, mod3, lng, lnb, wina, winr, gqa, wq, gkva, wkn, wvt, *tables)


def _attn_kernel(q_ref, kc_ref, kl_ref, vtc_ref, vtl_ref, sza_ref, hs_ref, som_ref, szm_ref, xc_ref,
                 mhg_ref, skip_ref, o_ref, yb_ref, s_buf, ot_s):
    n_ctx, t = kc_ref.shape[2], kl_ref.shape[2]
    kc = ATTN_KEY_CHUNK

    def scores(h, slot):
        qh = q_ref[0, h]
        sc = _dot_nt(kc_ref[0, h], qh)
        sk = _dot_nt(kl_ref[0, h], qh)
        s_buf[slot, 0:n_ctx, :] = sc
        s_buf[slot, n_ctx:n_ctx + t, :] = sk
        return jnp.maximum(jnp.max(sc, axis=0, keepdims=True), jnp.max(sk, axis=0, keepdims=True))

    def values(h, slot, m):
        chunks = [(vtc_ref, c * kc, c * kc) for c in range(n_ctx // kc)]
        chunks += [(vtl_ref, c * kc, n_ctx + c * kc) for c in range(t // kc)]
        acc = None
        for vref, v0, s0 in chunks:
            p = jnp.exp2(s_buf[slot, s0:s0 + kc, :] - m).astype(BF16)
            d = _dot(vref[0, h, :, v0:v0 + kc], p)
            acc = d if acc is None else acc + d
        rows = pl.ds(pl.multiple_of(h * DV_A, DV_A), DV_A)
        ot_s[rows, :] = acc[0:DV_A, :] / acc[DV_A:DV_A + 1, :]

    def cell_out(pc):
        rows = pl.ds(pl.multiple_of(pc * LANES, LANES), LANES)
        yb_ref[0, rows, :] = _mlstm_out(hs_ref[0, pc], som_ref[0, rows, :], szm_ref[0, rows, :],
                                        xc_ref[0, rows, :], mhg_ref[...], skip_ref[...])

    def body(i, m_even):
        h = 2 * i
        m_odd = scores(h + 1, 1)
        cell_out(i + 1)
        values(h, 0, m_even)
        m_even = scores(h + 2, 0)
        values(h + 1, 1, m_odd)
        return m_even

    assert hs_ref.shape[1] == H_A // 2
    m_first = scores(0, 0)
    cell_out(0)
    m_even = lax.fori_loop(0, H_A // 2 - 1, body, m_first)
    m_odd = scores(H_A - 1, 1)
    values(H_A - 2, 0, m_even)
    values(H_A - 1, 1, m_odd)
    o_ref[0] = (ot_s[...].T * sza_ref[0].astype(F32)).astype(BF16)


def _attn_call(q, kc, kl, vtc, vtl, sza, hs, som, szm, xc, mhg, skip, *, tq):
    b, _, t, _ = q.shape
    n_ctx = kc.shape[2] // b
    head_blk = lambda n, w: pl.BlockSpec((1, H_A, n, w), lambda i, j: (i, 0, 0, 0))
    const = lambda shape: pl.BlockSpec(shape, lambda i, j: (0,) * len(shape))
    row = lambda w: pl.BlockSpec((1, tq, w), lambda i, j: (i, j, 0))
    half = jax.ShapeDtypeStruct((b, t, W_A), BF16)
    return pl.pallas_call(
        _attn_kernel,
        grid=(b, t // tq),
        in_specs=[pl.BlockSpec((1, H_A, tq, SLAB), lambda i, j: (i, 0, j, 0)),
                  pl.BlockSpec((1, H_A, n_ctx, SLAB), lambda i, j: (0, 0, i, 0)), head_blk(t, SLAB),
                  pl.BlockSpec((1, H_A, DV_EXT, n_ctx), lambda i, j: (0, 0, 0, i)), head_blk(DV_EXT, t),
                  row(W_A),
                  pl.BlockSpec((1, tq // LANES, W_B, LANES), lambda i, j: (i, j, 0, 0)),
                  row(W_B), row(W_B), row(W_B), const((1, W_B)), const((1, W_B))],
        out_specs=[row(W_A), row(W_B)],
        out_shape=[half, jax.ShapeDtypeStruct((b, t, W_B), BF16)],
        scratch_shapes=[pltpu.VMEM((2, n_ctx + t, tq), F32),
                        pltpu.VMEM((W_A, tq), F32)],
        compiler_params=pltpu.CompilerParams(dimension_semantics=("parallel", "parallel"),
                                             vmem_limit_bytes=VMEM_LIMIT),
        name="mla_attention",
    )(q, kc, kl, vtc, vtl, sza, hs, som, szm, xc, mhg, skip)


def _onehot(cond):
    return cond.astype(F32).astype(BF16)


def _block_diag(a):
    bs, nb = QKV_BS, W_B // QKV_BS
    lb, sq = bs.bit_length() - 1, bs * bs
    assert bs == 1 << lb
    iota = lambda shape, d: lax.broadcasted_iota(jnp.int32, shape, d)
    spread = _onehot(iota((nb, W_B), 1) >> lb == iota((nb, W_B), 0))
    b = _dot(a, spread)
    col_o = iota((sq, W_B), 1) & (bs - 1)
    bm = jnp.concatenate([jnp.where(col_o == o, b, 0.0) for o in range(bs)], axis=0).astype(BF16)
    j = iota((W_B, bs * sq), 1)
    pick = _onehot((j & (sq - 1)) == ((iota((W_B, bs * sq), 0) & (bs - 1)) << lb) + (j >> (2 * lb)))
    same_block = iota((W_B, W_B), 0) >> lb == iota((W_B, W_B), 1) >> lb
    return jnp.where(same_block, _dot(pick, bm), 0.0).astype(BF16)


def _mlstm_weight_kernel(aq_ref, ak_ref, av_ref, wgt_ref, bgate_ref,
                         wmq_ref, wmk_ref, wmv_ref, wgc_ref, wgm_ref, bg_ref):
    a_q, a_k, a_v = aq_ref[...], ak_ref[...], av_ref[...]
    wq, wk, wv = (_block_diag(a.astype(BF16)) for a in (a_q, a_k, a_v))
    wmq_ref[...] = wq
    wmk_ref[...] = _block_diag((a_k * (DH_B ** -0.5)).astype(BF16))
    wmv_ref[...] = wv
    dst, src = (lax.broadcasted_iota(jnp.int32, (LANES, 2 * N_HD), d) for d in (0, 1))
    grp = dst >> (H_B.bit_length() - 1)
    src_of_dst = (dst & (H_B - 1)) + H_B * jnp.where(grp == 1, 2, jnp.where(grp == 2, 1, grp))
    perm = _onehot((dst < 2 * N_HD) & (src == src_of_dst))
    wg_t = _dot(perm, wgt_ref[...].astype(BF16)).astype(BF16)
    wgc_ref[...] = (_dot_nt(wq, wg_t[:, 0:W_B]) + _dot_nt(wk, wg_t[:, W_B:2 * W_B])).astype(BF16)
    wgm_ref[...] = _dot_nt(wv, wg_t[:, 2 * W_B:3 * W_B]).astype(BF16)
    bg_ref[...] = jnp.zeros(bg_ref.shape, F32)
    for d, s in enumerate((0, 2, 1, 3)):
        bg_ref[:, d * H_B:(d + 1) * H_B] = bgate_ref[:, s * H_B:(s + 1) * H_B]


def _qb_kernel(wqb_ref, wq_ref):
    wt = wqb_ref[...].T
    hw = DN_A + DR_A
    for hh in range(H_A):
        rope = wt[hh * hw + DN_A:(hh + 1) * hw, :]
        a0, b0, a1, b1 = (rope[i * ROPE_FREQS:(i + 1) * ROPE_FREQS, :] for i in range(4))
        slab = jnp.concatenate([wt[hh * hw:hh * hw + DN_A, :], -b0, a0, -b1, a1, rope], axis=0)
        wq_ref[:, hh * SLAB:(hh + 1) * SLAB] = slab.T.astype(BF16)


def _kvb_kernel(wkvb_ref, wkn_ref, wvt_ref):
    w = wkvb_ref[...]
    assert DN_A + DV_A == SLAB
    lane = lax.broadcasted_iota(jnp.int32, w.shape, 1)
    wkn_ref[...] = jnp.where((lane & (SLAB - 1)) < DN_A, w, 0.0).astype(BF16)
    wt = w.T
    for hh in range(H_A):
        wvt_ref[hh * DV_A:(hh + 1) * DV_A, :] = wt[hh * SLAB + DN_A:(hh + 1) * SLAB, :].astype(BF16)


def _prep_kernel(c_ref, cctx_ref, wada_ref, bada_ref, wt_ref, aq_ref, ak_ref, av_ref, wgt_ref, bgate_ref,
                 wqb_ref, wkvb_ref,
                 mod_ref, wina_ref, winr_ref, wmq_ref, wmk_ref, wmv_ref, wgc_ref, wgm_ref, bg_ref,
                 wq_ref, wkn_ref, wvt_ref):
    _ada_kernel(c_ref, cctx_ref, wada_ref, bada_ref, mod_ref)
    _win_kernel(wt_ref, wina_ref, winr_ref)
    _mlstm_weight_kernel(aq_ref, ak_ref, av_ref, wgt_ref, bgate_ref,
                         wmq_ref, wmk_ref, wmv_ref, wgc_ref, wgm_ref, bg_ref)
    _qb_kernel(wqb_ref, wq_ref)
    _kvb_kernel(wkvb_ref, wkn_ref, wvt_ref)


def _prep_call(c, c_ctx, w_ada, b_ada, wt, aq, ak, av, wgt, b_gate, w_qb, w_kvb):
    n_in, d = wt.shape
    assert c.shape[0] % SUBLANES == 0 and wgt.shape == (2 * N_HD, 3 * W_B)
    assert aq.shape == ak.shape == av.shape == (QKV_BS * QKV_BS, W_B // QKV_BS)
    bf = lambda *shape: jax.ShapeDtypeStruct(shape, BF16)
    return pl.pallas_call(
        _prep_kernel,
        out_shape=[jax.ShapeDtypeStruct((c.shape[0] + 1, 1, w_ada.shape[1]), F32),
                   bf(d, _S_KVA + LANES), bf(d, n_in - _S_KR),
                   bf(W_B, W_B), bf(W_B, W_B), bf(W_B, W_B), bf(W_B, LANES), bf(W_B, LANES),
                   jax.ShapeDtypeStruct((1, LANES), F32),
                   bf(Q_LORA, H_A * SLAB), bf(KV_LORA, H_A * SLAB), bf(W_A, KV_LORA)],
        compiler_params=pltpu.CompilerParams(vmem_limit_bytes=VMEM_LIMIT),
        name="param_prep",
    )(c, c_ctx, w_ada, b_ada, wt, aq, ak, av, wgt, b_gate, w_qb, w_kvb)


_PAD = SUBLANES


def _lane_scans(rows_per_chunk, combine, fill, out):
    npc = CHUNK // LANES
    flat = [x[:, i * LANES:(i + 1) * LANES] for x in rows_per_chunk for i in range(npc)]
    lane = lax.broadcasted_iota(jnp.int32, flat[0].shape, 1)
    pre, suf, sh = list(flat), list(flat), 1
    while sh < LANES:
        pre = [combine(p, jnp.where(lane >= sh, pltpu.roll(p, sh, 1), fill)) for p in pre]
        suf = [combine(s, jnp.where(lane < LANES - sh, pltpu.roll(s, LANES - sh, 1), fill)) for s in suf]
        sh *= 2
        yield
    for c in range(len(rows_per_chunk)):
        p, s = pre[c * npc:(c + 1) * npc], suf[c * npc:(c + 1) * npc]
        tot = [x[:, LANES - 1:LANES] for x in p]
        run = None
        for i in range(npc):
            if run is not None:
                p[i] = combine(p[i], run)
            run = tot[i] if run is None else combine(run, tot[i])
        run = None
        for i in reversed(range(npc)):
            if run is not None:
                s[i] = combine(s[i], run)
            run = tot[i] if run is None else combine(run, tot[i])
        out.append((jnp.concatenate(p, axis=1), jnp.concatenate(s, axis=1)))


def _gate_tables(gts, dests):
    L = CHUNK
    fwd_rows = lax.broadcasted_iota(jnp.int32, (N_HD, L), 0) < H_B
    lis = [gt[0:N_HD, :] for gt in gts]
    sums = []
    yield from _lane_scans([_log_sigmoid(gt[N_HD:2 * N_HD, :]) for gt in gts], jnp.add, 0.0, sums)
    cums = [jnp.where(fwd_rows, ps, ss) for ps, ss in sums]
    rs = [li - cum for li, cum in zip(lis, cums)]
    maxs = []
    yield from _lane_scans(rs, jnp.maximum, -jnp.inf, maxs)
    for (grow_ref, c, gcol_ref, r0), cum, r, (pm, sm) in zip(dests, cums, rs, maxs):
        grow_ref[c] = jnp.concatenate([cum, r, jnp.where(fwd_rows, pm, sm)], axis=0)
        gcol_ref[r0:r0 + L, :] = jnp.concatenate([r, jnp.zeros((LANES - N_HD, L), F32)], axis=0).T


def _fill_pad(pad_s, xm_ref, n):
    zrow = jnp.zeros((_PAD, W_B), F32)
    pad_s[0:_PAD, :] = zrow
    pad_s[_PAD:_PAD + n, :] = xm_ref[...].astype(F32)
    pad_s[_PAD + n:2 * _PAD + n, :] = zrow


def _conv_gate_stage(c, pad_s, xm_ref, cw_ref, cb_ref, wgc_ref, wgm_ref, bg_ref, xc_out, xcb_s):
    L = CHUNK
    rows = slice(c * L, (c + 1) * L)
    r0 = c * L + _PAD
    row_id = lax.broadcasted_iota(jnp.int32, (L, W_B), 0)
    xcur = pad_s[r0:r0 + L, :]
    xprev = jnp.where(row_id == 0, pad_s[r0 - 1:r0, :], pltpu.roll(xcur, 1, 0))
    xnext = jnp.where(row_id == L - 1, pad_s[r0 + L:r0 + L + 1, :], pltpu.roll(xcur, L - 1, 0))
    pre = cb_ref[...] + xprev * cw_ref[0:1, :] + xcur * cw_ref[1:2, :] + xnext * cw_ref[2:3, :]
    xc = _silu(pre)
    xcb = xc.astype(BF16)
    if xc_out is not None:
        xc_out[rows, :] = xcb
    xcb_s[rows, :] = xcb
    g = _dot(xcb, wgc_ref[...]) + _dot(xm_ref[rows, :], wgm_ref[...]) + bg_ref[...]
    return g.T[0:2 * N_HD, :]


def _headwise_stage(c, xm_ref, xcb_s, wq_ref, wk_ref, wv_ref, q_s, k_s, vt_s):
    L = CHUNK
    rows = slice(c * L, (c + 1) * L)
    for hf in range(W_B // MXU_TILE):
        cs = slice(hf * MXU_TILE, (hf + 1) * MXU_TILE)
        xcb = xcb_s[rows, cs]
        if q_s is not None:
            q_s[rows, cs] = _dot(xcb, wq_ref[cs, cs]).astype(BF16)
        k_s[rows, cs] = _dot(xcb, wk_ref[cs, cs]).astype(BF16)
        vt_s[c, cs, :] = _dot(xm_ref[rows, cs], wv_ref[cs, cs]).T.astype(BF16)


_TAB_ROWS = 5 * N_HD


def _chain_tables(growc_s, grow_s, tab_s, nc):
    L = CHUNK
    fwd_rows = lax.broadcasted_iota(jnp.int32, (N_HD, L), 0) < H_B
    gc = growc_s[0]
    steps = [(gc, gc)] + [(grow_s[i], grow_s[nc - 1 - i]) for i in range(nc)]
    parts = []
    for g_f, g_b in steps:
        pick = lambda a: jnp.where(fwd_rows, g_f[a * N_HD:(a + 1) * N_HD, :], g_b[a * N_HD:(a + 1) * N_HD, :])
        cum, r, pm = pick(0), pick(1), pick(2)
        b_end = jnp.where(fwd_rows[:, 0:1], cum[:, L - 1:L], cum[:, 0:1])
        w = r + b_end
        parts.append((cum, pm, b_end, w, jnp.max(w, axis=1, keepdims=True)))
    m0 = jnp.zeros((N_HD, 1), F32)
    for idx, (cum, pm, b_end, w, wmax) in enumerate(parts):
        m_new = jnp.maximum(b_end + m0, wmax)
        mu = jnp.maximum(m0, pm)
        decay = jnp.broadcast_to(jnp.exp(b_end + m0 - m_new), (N_HD, L))
        tab_s[idx] = jnp.concatenate([mu, jnp.exp(m0 - mu), jnp.exp(-(cum + mu)),
                                      jnp.exp(w - m_new), decay], axis=0)
        m0 = m_new


def _tab(tab, a, hd):
    return tab[a * N_HD + hd:a * N_HD + hd + 1, :]


def _state_update(st_ref, hd, vt_ext, k_c, ws_row, decay_hd):
    vw = (vt_ext.astype(F32) * ws_row).astype(BF16)
    st_ref[hd] = decay_hd * st_ref[hd] + _dot(vw, k_c)


def _mlstm_kernel(xm_ref, xmc_ref, cw_ref, cb_ref, wq_ref, wk_ref, wv_ref, wgc_ref, wgm_ref, bg_ref,
                  hs_ref, xc_ref,
                  pad_s, xcb_s, q_s, k_s, vt_s, gcol_s, grow_s,
                  padc_s, xcbc_s, kc_s, vtc_s, gcolc_s, growc_s, tab_s, st_s):
    L = CHUNK
    t = xm_ref.shape[1]
    nc = t // L
    row_i = lax.broadcasted_iota(jnp.int32, (L, L), 0)
    col_i = lax.broadcasted_iota(jnp.int32, (L, L), 1)
    tri = (row_i <= col_i, row_i >= col_i)
    ones_rows = (lax.broadcasted_iota(jnp.int32, (ST_ROWS - DH_B, L), 0) == 0).astype(BF16)
    hsl = lambda hh: slice(hh * DH_B, (hh + 1) * DH_B)
    xm, xmc = xm_ref.at[0], xmc_ref.at[0]
    gate_w = (cw_ref, cb_ref, wgc_ref, wgm_ref, bg_ref)

    _fill_pad(padc_s, xmc, CTX_LEN)
    _fill_pad(pad_s, xm, t)
    gts = [_conv_gate_stage(0, padc_s, xmc, *gate_w, None, xcbc_s)]
    gts += [_conv_gate_stage(c, pad_s, xm, *gate_w, xc_ref.at[0], xcb_s) for c in range(nc)]
    dests = [(growc_s, 0, gcolc_s, 0)] + [(grow_s, c, gcol_s, c * L) for c in range(nc)]
    tables = _gate_tables(gts, dests)
    _headwise_stage(0, xmc, xcbc_s, wq_ref, wk_ref, wv_ref, None, kc_s, vtc_s)
    for c in range(nc):
        next(tables, None)
        next(tables, None)
        _headwise_stage(c, xm, xcb_s, wq_ref, wk_ref, wv_ref, q_s, k_s, vt_s)
    for _ in tables:
        pass

    _chain_tables(growc_s, grow_s, tab_s, nc)
    st_s[...] = jnp.zeros(st_s.shape, F32)
    tab = tab_s[0]
    for hd in range(N_HD):
        hh = hd % H_B
        vt_ext = jnp.concatenate([vtc_s[0, hsl(hh), :], ones_rows], axis=0)
        _state_update(st_s, hd, vt_ext, kc_s[:, hsl(hh)], _tab(tab, 3, hd), _tab(tab, 4, hd)[:, 0:DH_B])

    def body(i, carry, *, accumulate):
        ci = (i, nc - 1 - i)
        rows = tuple(pl.ds(pl.multiple_of(c * L, L), L) for c in ci)
        tab = tab_s[i + 1]
        gcols = tuple(gcol_s[r, :] for r in rows)
        live = {}

        def stage_scores(hd):
            d, hh = hd // H_B, hd % H_B
            k_c = k_s[rows[d], hsl(hh)]
            vt_ext = jnp.concatenate([vt_s[ci[d], hsl(hh), :], ones_rows], axis=0)
            lhs = jnp.concatenate([k_c, st_s[hd].astype(BF16)], axis=0)
            live[hd] = (k_c, vt_ext, _dot_nt(lhs, q_s[rows[d], hsl(hh)]))

        def stage_gate(hd):
            d = hd // H_B
            k_c, vt_ext, res = live[hd]
            rcol = gcols[d][:, hd:hd + 1]
            e = jnp.exp(jnp.where(tri[d], rcol - _tab(tab, 0, hd), -jnp.inf))
            p = (res[0:L, :] * e).astype(BF16)
            _state_update(st_s, hd, vt_ext, k_c, _tab(tab, 3, hd), _tab(tab, 4, hd)[:, 0:DH_B])
            live[hd] = (vt_ext, res[L:L + ST_ROWS, :], p)

        def stage_out(hd):
            d, hh = hd // H_B, hd % H_B
            vt_ext, inter, p = live.pop(hd)
            tot = _tab(tab, 1, hd) * inter + _dot(vt_ext, p)
            den = tot[DH_B:DH_B + 1, :]
            h_t = tot[0:DH_B, :] / jnp.maximum(jnp.abs(den), _tab(tab, 2, hd))
            for pc in range(L // LANES):
                piece = h_t[:, pc * LANES:(pc + 1) * LANES]
                if accumulate:
                    hs_ref[0, ci[d] * (L // LANES) + pc, hsl(hh), :] += piece
                else:
                    hs_ref[0, ci[d] * (L // LANES) + pc, hsl(hh), :] = piece

        stages = (stage_scores, stage_gate, stage_out)
        for step in range(N_HD + len(stages) - 1):
            for si in range(len(stages)):
                if 0 <= step - si < N_HD:
                    stages[si](step - si)
        return carry

    lax.fori_loop(0, nc // 2, functools.partial(body, accumulate=False), 0)
    lax.fori_loop(nc // 2, nc, functools.partial(body, accumulate=True), 0)


def _mlstm_out(hs_t, som, szm, xc, mhg, skip):
    hsum = hs_t.T * som.astype(F32)
    parts = [_layer_norm(hsum[:, hh * DH_B:(hh + 1) * DH_B]) for hh in range(H_B)]
    hb = jnp.concatenate(parts, axis=1) * mhg + skip * xc.astype(F32)
    return (hb * szm.astype(F32)).astype(BF16)


def _mlstm_call(xm, xmc, conv_w, conv_b, wq, wk, wv, wgc, wgm, bg):
    b, t, _ = xm.shape
    nc = t // CHUNK
    assert nc % 2 == 0
    seq = lambda n: pl.BlockSpec((1, n, W_B), lambda i: (i, 0, 0))
    const = lambda shape: pl.BlockSpec(shape, lambda i: (0,) * len(shape))
    scratch = [
        pltpu.VMEM((t + 2 * _PAD, W_B), F32),
        pltpu.VMEM((t, W_B), BF16),
        pltpu.VMEM((t, W_B), BF16),
        pltpu.VMEM((t, W_B), BF16),
        pltpu.VMEM((nc, W_B, CHUNK), BF16),
        pltpu.VMEM((t, LANES), F32),
        pltpu.VMEM((nc, 3 * N_HD, CHUNK), F32),
        pltpu.VMEM((CTX_LEN + 2 * _PAD, W_B), F32),
        pltpu.VMEM((CTX_LEN, W_B), BF16),
        pltpu.VMEM((CTX_LEN, W_B), BF16),
        pltpu.VMEM((1, W_B, CHUNK), BF16),
        pltpu.VMEM((CTX_LEN, LANES), F32),
        pltpu.VMEM((1, 3 * N_HD, CHUNK), F32),
        pltpu.VMEM((nc + 1, _TAB_ROWS, CHUNK), F32),
        pltpu.VMEM((N_HD, ST_ROWS, DH_B), F32),
    ]
    return pl.pallas_call(
        _mlstm_kernel,
        grid=(b,),
        in_specs=[seq(t), seq(CTX_LEN), const((3, W_B)), const((1, W_B)),
                  const((W_B, W_B)), const((W_B, W_B)), const((W_B, W_B)),
                  const((W_B, LANES)), const((W_B, LANES)), const((1, LANES))],
        out_specs=[pl.BlockSpec((1, t // LANES, W_B, LANES), lambda i: (i, 0, 0, 0)), seq(t)],
        out_shape=[jax.ShapeDtypeStruct((b, t // LANES, W_B, LANES), F32),
                   jax.ShapeDtypeStruct((b, t, W_B), BF16)],
        scratch_shapes=scratch,
        compiler_params=pltpu.CompilerParams(dimension_semantics=("arbitrary",),
                                             vmem_limit_bytes=VMEM_LIMIT),
        name="mlstm_bidir",
    )(xm, xmc, conv_w, conv_b, wq, wk, wv, wgc, wgm, bg)


_H_SLOTS = 3


def _out_kernel(h_hbm, mod_ref, ya_ref, yb_ref, wo_ref, g_ref, b_ref, o_ref, h_buf, h_sem, *, n_sub):
    tm = o_ref.shape[1]
    sub = tm // n_sub
    gate = mod_ref[0][:, 2 * D_MODEL:3 * D_MODEL]
    nj = pl.num_programs(1)
    step, n_steps = pl.program_id(0) * nj + pl.program_id(1), pl.num_programs(0) * nj

    def h_copy(k):
        slot = k % _H_SLOTS
        src = h_hbm.at[k // nj, pl.ds(pl.multiple_of((k % nj) * tm, tm), tm), :]
        return pltpu.make_async_copy(src, h_buf.at[slot], h_sem.at[slot])

    @pl.when(step == 0)
    def _():
        h_copy(step).start()
        h_copy(step + 1).start()

    @pl.when(step + 2 < n_steps)
    def _():
        h_copy(step + 2).start()

    h_copy(step).wait()
    h_ref = h_buf.at[step % _H_SLOTS]

    def mix_stage(s):
        rows = slice(s * sub, (s + 1) * sub)
        return (_dot(ya_ref[0, rows, :], wo_ref[0:W_A, :])
                + _dot(yb_ref[0, rows, :], wo_ref[W_A:W_A + W_B, :]))

    def norm_stage(s, y):
        rows = slice(s * sub, (s + 1) * sub)
        o_ref[0, rows, :] = _layer_norm(ALPHA * h_ref[rows, :] + gate * y) * g_ref[...] + b_ref[...]

    y = mix_stage(0)
    for s in range(n_sub):
        y_next = mix_stage(s + 1) if s + 1 < n_sub else None
        norm_stage(s, y)
        y = y_next


def _out_call(h, mod3, ya, yb, wo, g, bb, *, tm, n_sub):
    b, t, _ = h.shape
    const = lambda shape: pl.BlockSpec(shape, lambda i, j: (0,) * len(shape))
    row = lambda w: pl.BlockSpec((1, tm, w), lambda i, j: (i, j, 0))
    assert b * (t // tm) >= 2
    return pl.pallas_call(
        functools.partial(_out_kernel, n_sub=n_sub),
        grid=(b, t // tm),
        in_specs=[pl.BlockSpec(memory_space=pl.ANY), pl.BlockSpec((1, 1, 3 * D_MODEL), lambda i, j: (i, 0, 0)),
                  row(W_A), row(W_B),
                  const((W_A + W_B, D_MODEL)), const((1, D_MODEL)), const((1, D_MODEL))],
        out_specs=row(D_MODEL),
        out_shape=jax.ShapeDtypeStruct((b, t, D_MODEL), F32),
        scratch_shapes=[pltpu.VMEM((_H_SLOTS, tm, D_MODEL), F32), pltpu.SemaphoreType.DMA((_H_SLOTS,))],
        compiler_params=pltpu.CompilerParams(dimension_semantics=("arbitrary", "arbitrary"),
                                             vmem_limit_bytes=VMEM_LIMIT),
        name="out_proj_ln",
    )(h, mod3, ya, yb, wo, g, bb)


def _rope_tables(seq, scale_keep, scale_rope):
    n_rows = seq // GRID_W
    rowp = np.repeat(np.arange(n_rows, dtype=np.float32), GRID_W)
    colp = np.tile(np.arange(GRID_W, dtype=np.float32), n_rows)
    inv = (np.float32(ROPE_BASE) ** (-np.arange(ROPE_FREQS, dtype=np.float32) / np.float32(ROPE_FREQS)))
    ang = np.stack([rowp[:, None] * inv, colp[:, None] * inv], axis=1).astype(np.float32)
    cos = np.broadcast_to(np.cos(ang)[:, :, None, :], (seq, 2, 2, ROPE_FREQS)).reshape(seq, DR_A)
    sin = np.broadcast_to(np.sin(ang)[:, :, None, :], (seq, 2, 2, ROPE_FREQS)).reshape(seq, DR_A)
    z32 = np.zeros((seq, DR_A), np.float32)
    t1 = np.concatenate([np.full((seq, DN_A), scale_keep, np.float32), z32, cos * scale_rope], axis=1)
    t2 = np.concatenate([np.zeros((seq, DN_A), np.float32), z32, sin * scale_rope], axis=1)
    return jnp.asarray(t1, F32), jnp.asarray(t2, F32)


def kernel(x, c, ctx, c_ctx, ln_in_g, ln_in_b, w_ada, b_ada, w_in, g_qa, w_qb, g_kva, w_kvb, conv_w, conv_b, w_mq, w_mk, w_mv, w_gate, b_gate, mh_g, skip, w_out, ln_g, ln_b):
    b, t, _ = x.shape
    l = 0
    r2 = lambda v: v.reshape(1, -1)

    blocks = lambda w: jnp.transpose(w[l], (1, 2, 0)).reshape(QKV_BS * QKV_BS, -1)
    wo = w_out[l].astype(BF16)

    sm_scale = (DN_A + DR_A) ** -0.5 * LOG2_E
    t1q, t2q = _rope_tables(t, sm_scale, sm_scale)
    t1k, t2k = _rope_tables(t, 0.0, 1.0)
    n_ctx = ctx.shape[1]

    mod3, wina, winr, wmq, wmk_s, wmv, wgc, wgm, bg, wq, wkn, wvt = _prep_call(
        c, r2(c_ctx), w_ada[l], r2(b_ada[l]), jnp.swapaxes(w_in[l], 0, 1), blocks(w_mq), blocks(w_mk),
        blocks(w_mv), jnp.swapaxes(w_gate[l], 0, 1), r2(b_gate[l]), w_qb[l], w_kvb[l])

    lng, lnb = r2(ln_in_g), r2(ln_in_b)
    shared = (wina, winr, r2(g_qa[l]), wq, r2(g_kva[l]), wkn, wvt)
    q, kl, vtl, xm, sza, som, szm, h = _proj_call(x, mod3, None, lng, lnb, *shared, t1q, t2q, t1k, t2k,
                                                  tm=1024, n_sub=4, latent=True)
    kc, vtc, xmc = _proj_call(ctx.reshape(1, b * n_ctx, D_MODEL), mod3, b, lng, lnb, *shared,
                              tm=min(1024, b * n_ctx), n_sub=2, latent=False)
    xmc = xmc.reshape(b, n_ctx, W_B)

    hs, xc = _mlstm_call(xm, xmc, conv_w[l], r2(conv_b[l]), wmq, wmk_s, wmv, wgc, wgm, bg)
    ya, yb = _attn_call(q, kc, kl, vtc, vtl, sza, hs, som, szm, xc, r2(mh_g[l]), r2(skip[l]), tq=512)
    return _out_call(h, mod3, ya, yb, wo, r2(ln_g[l]), r2(ln_b[l]), tm=1024, n_sub=4)
```

```python
import functools

import numpy as np
import jax
import jax.numpy as jnp
from jax import lax
from jax.experimental import pallas as pl
from jax.experimental.pallas import tpu as pltpu

F32 = jnp.float32
BF16 = jnp.bfloat16

D_MODEL = 1024
CTX_LEN = 256
GRID_W = 64
H_A, DN_A, DR_A, DV_A = 8, 64, 32, 64
W_A = H_A * DV_A
Q_LORA, KV_LORA = 256, 128
ROPE_FREQS = DR_A // 4
ROPE_BASE = 10000.0
H_B, DH_B = 4, 128
W_B = H_B * DH_B
QKV_BS = 4
DEPTH = 1
ALPHA = (2.0 * DEPTH) ** 0.25
LN_EPS = 1e-5
RMS_EPS = 1e-6
LOG2_E = 1.4426950408889634

LANES = 128
SUBLANES = 8
MXU_TILE = 256
SLAB = LANES
DV_EXT = DV_A + 16
ATTN_KEY_CHUNK = MXU_TILE
CHUNK = 256
N_HD = 2 * H_B
ST_ROWS = DH_B + 16
VMEM_LIMIT = 56 * 1024 * 1024

_NT = (((1,), (1,)), ((), ()))


def _dot(a, b):
    return jnp.dot(a, b, preferred_element_type=F32)


def _dot_nt(a, b):
    return lax.dot_general(a, b, _NT, preferred_element_type=F32)


def _layer_norm(x):
    mu = jnp.mean(x, axis=-1, keepdims=True)
    xc = x - mu
    var = jnp.mean(xc * xc, axis=-1, keepdims=True)
    return xc * lax.rsqrt(var + LN_EPS)


def _rms_norm(x, g):
    return (x * lax.rsqrt(jnp.mean(x * x, axis=-1, keepdims=True) + RMS_EPS)) * g


def _silu(x):
    return x * jax.nn.sigmoid(x)


def _log_sigmoid(x):
    return jnp.minimum(x, 0.0) - jnp.log1p(jnp.exp(-jnp.abs(x)))


def _ada_kernel(c_ref, cctx_ref, w_ref, b_ref, o_ref):
    first = lax.broadcasted_iota(jnp.int32, (SUBLANES, D_MODEL), 0) == 0
    cc = jnp.concatenate([c_ref[...], jnp.where(first, cctx_ref[...], 0.0)], axis=0)
    a = _silu(cc).astype(BF16)
    mod = _dot(a, w_ref[...].astype(BF16)) + b_ref[...]
    for r in range(o_ref.shape[0]):
        o_ref[r] = mod[r:r + 1, :]


_S_KVA, _S_KR = Q_LORA + KV_LORA, Q_LORA + KV_LORA + DR_A


def _win_kernel(wt_ref, wina_ref, winr_ref):
    n_r = winr_ref.shape[1]
    for j in range(_S_KVA // LANES):
        wina_ref[:, j * LANES:(j + 1) * LANES] = wt_ref[j * LANES:(j + 1) * LANES, :].T.astype(BF16)
    kr = wt_ref[_S_KVA:_S_KR, :]
    a0, b0, a1, b1 = (kr[i * ROPE_FREQS:(i + 1) * ROPE_FREQS, :] for i in range(4))
    blk = jnp.concatenate([jnp.zeros((DN_A, D_MODEL), F32), -b0, a0, -b1, a1, kr], axis=0)
    wina_ref[:, _S_KVA:_S_KVA + LANES] = blk.T.astype(BF16)
    for j in range(n_r // MXU_TILE):
        rows = slice(_S_KR + j * MXU_TILE, _S_KR + (j + 1) * MXU_TILE)
        winr_ref[:, j * MXU_TILE:(j + 1) * MXU_TILE] = wt_ref[rows, :].T.astype(BF16)


def _rope_slab(s, t1, t2):
    return s * t1 + pltpu.roll(s, DR_A, 1) * t2


def _proj_kernel(x_ref, mod_ref, lng_ref, lnb_ref, wina_ref, winr_ref, gqa_ref, wq_ref, gkva_ref,
                 wkn_ref, wvt_ref, *refs, latent, n_sub):
    if latent:
        t1q_ref, t2q_ref, t1k_ref, t2k_ref, q_ref, k_ref, vt_ref, xm_ref, sza_ref, som_ref, szm_ref, h_ref = refs
    else:
        k_ref, vt_ref, xm_ref = refs
    sub = x_ref.shape[1] // n_sub
    mod = mod_ref[0]
    shift, scale1 = mod[:, 0:D_MODEL], 1.0 + mod[:, D_MODEL:2 * D_MODEL]
    ones_rows = (lax.broadcasted_iota(jnp.int32, (DV_EXT - DV_A, sub), 0) == 0).astype(BF16)

    def norm_stage(s):
        rows = slice(s * sub, (s + 1) * sub)
        h = _layer_norm(x_ref[0, rows, :]) * lng_ref[...] + lnb_ref[...]
        if latent:
            h_ref[0, rows, :] = h
        return (h * scale1 + shift).astype(BF16)

    def proj_stage(s, u):
        rows = slice(s * sub, (s + 1) * sub)
        p0 = _dot(u, wina_ref[...])
        d_0 = _dot(u, winr_ref[:, 0:512] if latent else winr_ref[:, 512:1024])
        d_om = _dot(u, winr_ref[:, 1024:1536]) if latent else None
        d_zm = _dot(u, winr_ref[:, 1536:2048]) if latent else None
        kvn = _rms_norm(p0[:, 256:384], gkva_ref[...]).astype(BF16)
        kn = _dot(kvn, wkn_ref[...])
        if latent:
            sza_ref[0, rows, :] = _silu(d_0).astype(BF16)
            krr = _rope_slab(p0[:, 384:512], t1k_ref[rows, :], t2k_ref[rows, :])
        else:
            rope_lanes = lax.broadcasted_iota(jnp.int32, (sub, SLAB), 1) >= SLAB - DR_A
            krr = jnp.where(rope_lanes, p0[:, 384:512], 0.0)
        for hh in range(H_A):
            k_ref[0, hh, rows, :] = (kn[:, hh * SLAB:(hh + 1) * SLAB] + krr).astype(BF16)
        vt = _dot_nt(wvt_ref[...], kvn).astype(BF16)
        for hh in range(H_A):
            vt_ref[0, hh, 0:DV_A, rows] = vt[hh * DV_A:(hh + 1) * DV_A, :]
            vt_ref[0, hh, DV_A:DV_EXT, rows] = ones_rows
        if not latent:
            xm_ref[0, rows, :] = d_0.astype(BF16)
        else:
            qn = _rms_norm(p0[:, 0:256], gqa_ref[...]).astype(BF16)
            qs = _dot(qn, wq_ref[...])
            d_xm = _dot(u, winr_ref[:, 512:1024])
            t1q, t2q = t1q_ref[rows, :], t2q_ref[rows, :]
            for hh in range(H_A):
                q_ref[0, hh, rows, :] = _rope_slab(qs[:, hh * SLAB:(hh + 1) * SLAB], t1q, t2q).astype(BF16)
            som_ref[0, rows, :] = jax.nn.sigmoid(d_om).astype(BF16)
            szm_ref[0, rows, :] = _silu(d_zm).astype(BF16)
            xm_ref[0, rows, :] = d_xm.astype(BF16)

    u = norm_stage(0)
    for s in range(n_sub):
        u_next = norm_stage(s + 1) if s + 1 < n_sub else None
        proj_stage(s, u)
        u = u_next


def _proj_call(x, mod3, mod_row, lng, lnb, wina, winr, gqa, wq, gkva, wkn, wvt, *tables, tm, n_sub, latent):
    b, t, _ = x.shape
    const = lambda shape: pl.BlockSpec(shape, lambda i, j: (0,) * len(shape))
    tab = pl.BlockSpec((tm, SLAB), lambda i, j: (j, 0))
    assert len(tables) == (4 if latent else 0)
    row = lambda w: pl.BlockSpec((1, tm, w), lambda i, j: (i, j, 0))
    if mod_row is None:
        mod_spec = pl.BlockSpec((1, 1, 3 * D_MODEL), lambda i, j: (i, 0, 0))
    else:
        mod_spec = pl.BlockSpec((1, 1, 3 * D_MODEL), lambda i, j: (mod_row, 0, 0))
    in_specs = [row(D_MODEL), mod_spec, const((1, D_MODEL)), const((1, D_MODEL)),
                const(wina.shape), const(winr.shape), const((1, Q_LORA)), const((Q_LORA, H_A * SLAB)),
                const((1, KV_LORA)), const((KV_LORA, H_A * SLAB)), const((W_A, KV_LORA))]
    in_specs += [tab] * len(tables)
    k_spec = pl.BlockSpec((1, H_A, tm, SLAB), lambda i, j: (i, 0, j, 0))
    vt_spec = pl.BlockSpec((1, H_A, DV_EXT, tm), lambda i, j: (i, 0, 0, j))
    k_shape = jax.ShapeDtypeStruct((b, H_A, t, SLAB), BF16)
    vt_shape = jax.ShapeDtypeStruct((b, H_A, DV_EXT, t), BF16)
    half = jax.ShapeDtypeStruct((b, t, W_B), BF16)
    if latent:
        out_specs = [k_spec, k_spec, vt_spec, row(W_B), row(W_A), row(W_B), row(W_B), row(D_MODEL)]
        out_shape = [k_shape, k_shape, vt_shape, half, half, half, half,
                     jax.ShapeDtypeStruct((b, t, D_MODEL), F32)]
    else:
        out_specs = [k_spec, vt_spec, row(W_B)]
        out_shape = [k_shape, vt_shape, half]
    return pl.pallas_call(
        functools.partial(_proj_kernel, latent=latent, n_sub=n_sub),
        grid=(b, t // tm),
        in_specs=in_specs, out_specs=out_specs, out_shape=out_shape,
        compiler_params=pltpu.CompilerParams(dimension_semantics=("parallel", "parallel"),
                                             vmem_limit_bytes=VMEM_LIMIT),
        name="in_proj_latent" if latent else "in_proj_ctx",
    )(x, mod3, lng, lnb, wina, winr, gqa, wq, gkva, wkn, wvt, *tables)


def _attn_kernel(q_ref, kc_ref, kl_ref, vtc_ref, vtl_ref, sza_ref, hs_ref, som_ref, szm_ref, xc_ref,
                 mhg_ref, skip_ref, o_ref, yb_ref, s_buf, ot_s):
    n_ctx, t = kc_ref.shape[2], kl_ref.shape[2]
    kc = ATTN_KEY_CHUNK

    def scores(h, slot):
        qh = q_ref[0, h]
        sc = _dot_nt(kc_ref[0, h], qh)
        sk = _dot_nt(kl_ref[0, h], qh)
        s_buf[slot, 0:n_ctx, :] = sc
        s_buf[slot, n_ctx:n_ctx + t, :] = sk
        return jnp.maximum(jnp.max(sc, axis=0, keepdims=True), jnp.max(sk, axis=0, keepdims=True))

    def values(h, slot, m):
        chunks = [(vtc_ref, c * kc, c * kc) for c in range(n_ctx // kc)]
        chunks += [(vtl_ref, c * kc, n_ctx + c * kc) for c in range(t // kc)]
        acc = None
        for vref, v0, s0 in chunks:
            p = jnp.exp2(s_buf[slot, s0:s0 + kc, :] - m).astype(BF16)
            d = _dot(vref[0, h, :, v0:v0 + kc], p)
            acc = d if acc is None else acc + d
        rows = pl.ds(pl.multiple_of(h * DV_A, DV_A), DV_A)
        ot_s[rows, :] = acc[0:DV_A, :] / acc[DV_A:DV_A + 1, :]

    def cell_out(pc):
        rows = pl.ds(pl.multiple_of(pc * LANES, LANES), LANES)
        yb_ref[0, rows, :] = _mlstm_out(hs_ref[0, pc], som_ref[0, rows, :], szm_ref[0, rows, :],
                                        xc_ref[0, rows, :], mhg_ref[...], skip_ref[...])

    def body(i, m_even):
        h = 2 * i
        m_odd = scores(h + 1, 1)
        cell_out(i + 1)
        values(h, 0, m_even)
        m_even = scores(h + 2, 0)
        values(h + 1, 1, m_odd)
        return m_even

    assert hs_ref.shape[1] == H_A // 2
    m_first = scores(0, 0)
    cell_out(0)
    m_even = lax.fori_loop(0, H_A // 2 - 1, body, m_first)
    m_odd = scores(H_A - 1, 1)
    values(H_A - 2, 0, m_even)
    values(H_A - 1, 1, m_odd)
    o_ref[0] = (ot_s[...].T * sza_ref[0].astype(F32)).astype(BF16)


def _attn_call(q, kc, kl, vtc, vtl, sza, hs, som, szm, xc, mhg, skip, *, tq):
    b, _, t, _ = q.shape
    n_ctx = kc.shape[2] // b
    head_blk = lambda n, w: pl.BlockSpec((1, H_A, n, w), lambda i, j: (i, 0, 0, 0))
    const = lambda shape: pl.BlockSpec(shape, lambda i, j: (0,) * len(shape))
    row = lambda w: pl.BlockSpec((1, tq, w), lambda i, j: (i, j, 0))
    half = jax.ShapeDtypeStruct((b, t, W_A), BF16)
    return pl.pallas_call(
        _attn_kernel,
        grid=(b, t // tq),
        in_specs=[pl.BlockSpec((1, H_A, tq, SLAB), lambda i, j: (i, 0, j, 0)),
                  pl.BlockSpec((1, H_A, n_ctx, SLAB), lambda i, j: (0, 0, i, 0)), head_blk(t, SLAB),
                  pl.BlockSpec((1, H_A, DV_EXT, n_ctx), lambda i, j: (0, 0, 0, i)), head_blk(DV_EXT, t),
                  row(W_A),
                  pl.BlockSpec((1, tq // LANES, W_B, LANES), lambda i, j: (i, j, 0, 0)),
                  row(W_B), row(W_B), row(W_B), const((1, W_B)), const((1, W_B))],
        out_specs=[row(W_A), row(W_B)],
        out_shape=[half, jax.ShapeDtypeStruct((b, t, W_B), BF16)],
        scratch_shapes=[pltpu.VMEM((2, n_ctx + t, tq), F32),
                        pltpu.VMEM((W_A, tq), F32)],
        compiler_params=pltpu.CompilerParams(dimension_semantics=("parallel", "parallel"),
                                             vmem_limit_bytes=VMEM_LIMIT),
        name="mla_attention",
    )(q, kc, kl, vtc, vtl, sza, hs, som, szm, xc, mhg, skip)


def _onehot(cond):
    return cond.astype(F32).astype(BF16)


def _block_diag(a):
    bs, nb = QKV_BS, W_B // QKV_BS
    lb, sq = bs.bit_length() - 1, bs * bs
    assert bs == 1 << lb
    iota = lambda shape, d: lax.broadcasted_iota(jnp.int32, shape, d)
    spread = _onehot(iota((nb, W_B), 1) >> lb == iota((nb, W_B), 0))
    b = _dot(a, spread)
    col_o = iota((sq, W_B), 1) & (bs - 1)
    bm = jnp.concatenate([jnp.where(col_o == o, b, 0.0) for o in range(bs)], axis=0).astype(BF16)
    j = iota((W_B, bs * sq), 1)
    pick = _onehot((j & (sq - 1)) == ((iota((W_B, bs * sq), 0) & (bs - 1)) << lb) + (j >> (2 * lb)))
    same_block = iota((W_B, W_B), 0) >> lb == iota((W_B, W_B), 1) >> lb
    return jnp.where(same_block, _dot(pick, bm), 0.0).astype(BF16)


def _mlstm_weight_kernel(aq_ref, ak_ref, av_ref, wgt_ref, bgate_ref,
                         wmq_ref, wmk_ref, wmv_ref, wgc_ref, wgm_ref, bg_ref):
    a_q, a_k, a_v = aq_ref[...], ak_ref[...], av_ref[...]
    wq, wk, wv = (_block_diag(a.astype(BF16)) for a in (a_q, a_k, a_v))
    wmq_ref[...] = wq
    wmk_ref[...] = _block_diag((a_k * (DH_B ** -0.5)).astype(BF16))
    wmv_ref[...] = wv
    dst, src = (lax.broadcasted_iota(jnp.int32, (LANES, 2 * N_HD), d) for d in (0, 1))
    grp = dst >> (H_B.bit_length() - 1)
    src_of_dst = (dst & (H_B - 1)) + H_B * jnp.where(grp == 1, 2, jnp.where(grp == 2, 1, grp))
    perm = _onehot((dst < 2 * N_HD) & (src == src_of_dst))
    wg_t = _dot(perm, wgt_ref[...].astype(BF16)).astype(BF16)
    wgc_ref[...] = (_dot_nt(wq, wg_t[:, 0:W_B]) + _dot_nt(wk, wg_t[:, W_B:2 * W_B])).astype(BF16)
    wgm_ref[...] = _dot_nt(wv, wg_t[:, 2 * W_B:3 * W_B]).astype(BF16)
    bg_ref[...] = jnp.zeros(bg_ref.shape, F32)
    for d, s in enumerate((0, 2, 1, 3)):
        bg_ref[:, d * H_B:(d + 1) * H_B] = bgate_ref[:, s * H_B:(s + 1) * H_B]


def _qb_kernel(wqb_ref, wq_ref):
    wt = wqb_ref[...].T
    hw = DN_A + DR_A
    for hh in range(H_A):
        rope = wt[hh * hw + DN_A:(hh + 1) * hw, :]
        a0, b0, a1, b1 = (rope[i * ROPE_FREQS:(i + 1) * ROPE_FREQS, :] for i in range(4))
        slab = jnp.concatenate([wt[hh * hw:hh * hw + DN_A, :], -b0, a0, -b1, a1, rope], axis=0)
        wq_ref[:, hh * SLAB:(hh + 1) * SLAB] = slab.T.astype(BF16)


def _kvb_kernel(wkvb_ref, wkn_ref, wvt_ref):
    w = wkvb_ref[...]
    assert DN_A + DV_A == SLAB
    lane = lax.broadcasted_iota(jnp.int32, w.shape, 1)
    wkn_ref[...] = jnp.where((lane & (SLAB - 1)) < DN_A, w, 0.0).astype(BF16)
    wt = w.T
    for hh in range(H_A):
        wvt_ref[hh * DV_A:(hh + 1) * DV_A, :] = wt[hh * SLAB + DN_A:(hh + 1) * SLAB, :].astype(BF16)


def _prep_kernel(c_ref, cctx_ref, wada_hbm, bada_ref, wt_hbm, aq_ref, ak_ref, av_ref, wgt_ref, bgate_ref,
                 wqb_ref, wkvb_ref,
                 mod_ref, wina_ref, winr_ref, wmq_ref, wmk_ref, wmv_ref, wgc_ref, wgm_ref, bg_ref,
                 wq_ref, wkn_ref, wvt_ref, wt_buf, wada_buf, sem):
    wt_copy = pltpu.make_async_copy(wt_hbm, wt_buf, sem.at[0])
    wada_copy = pltpu.make_async_copy(wada_hbm, wada_buf, sem.at[1])
    wt_copy.start()
    wada_copy.start()
    _mlstm_weight_kernel(aq_ref, ak_ref, av_ref, wgt_ref, bgate_ref,
                         wmq_ref, wmk_ref, wmv_ref, wgc_ref, wgm_ref, bg_ref)
    _qb_kernel(wqb_ref, wq_ref)
    _kvb_kernel(wkvb_ref, wkn_ref, wvt_ref)
    wt_copy.wait()
    _win_kernel(wt_buf, wina_ref, winr_ref)
    wada_copy.wait()
    _ada_kernel(c_ref, cctx_ref, wada_buf, bada_ref, mod_ref)


def _prep_call(c, c_ctx, w_ada, b_ada, wt, aq, ak, av, wgt, b_gate, w_qb, w_kvb):
    n_in, d = wt.shape
    assert c.shape[0] % SUBLANES == 0 and wgt.shape == (2 * N_HD, 3 * W_B)
    assert aq.shape == ak.shape == av.shape == (QKV_BS * QKV_BS, W_B // QKV_BS)
    bf = lambda *shape: jax.ShapeDtypeStruct(shape, BF16)
    vmem, hbm = pl.BlockSpec(memory_space=pltpu.VMEM), pl.BlockSpec(memory_space=pl.ANY)
    return pl.pallas_call(
        _prep_kernel,
        in_specs=[vmem, vmem, hbm, vmem, hbm] + [vmem] * 7,
        scratch_shapes=[pltpu.VMEM(wt.shape, F32), pltpu.VMEM(w_ada.shape, F32), pltpu.SemaphoreType.DMA((2,))],
        out_shape=[jax.ShapeDtypeStruct((c.shape[0] + 1, 1, w_ada.shape[1]), F32),
                   bf(d, _S_KVA + LANES), bf(d, n_in - _S_KR),
                   bf(W_B, W_B), bf(W_B, W_B), bf(W_B, W_B), bf(W_B, LANES), bf(W_B, LANES),
                   jax.ShapeDtypeStruct((1, LANES), F32),
                   bf(Q_LORA, H_A * SLAB), bf(KV_LORA, H_A * SLAB), bf(W_A, KV_LORA)],
        compiler_params=pltpu.CompilerParams(vmem_limit_bytes=VMEM_LIMIT),
        name="param_prep",
    )(c, c_ctx, w_ada, b_ada, wt, aq, ak, av, wgt, b_gate, w_qb, w_kvb)


_PAD = SUBLANES


def _lane_scans(rows_per_chunk, combine, fill, out):
    npc = CHUNK // LANES
    flat = [x[:, i * LANES:(i + 1) * LANES] for x in rows_per_chunk for i in range(npc)]
    lane = lax.broadcasted_iota(jnp.int32, flat[0].shape, 1)
    pre, suf, sh = list(flat), list(flat), 1
    while sh < LANES:
        pre = [combine(p, jnp.where(lane >= sh, pltpu.roll(p, sh, 1), fill)) for p in pre]
        suf = [combine(s, jnp.where(lane < LANES - sh, pltpu.roll(s, LANES - sh, 1), fill)) for s in suf]
        sh *= 2
        yield
    for c in range(len(rows_per_chunk)):
        p, s = pre[c * npc:(c + 1) * npc], suf[c * npc:(c + 1) * npc]
        tot = [x[:, LANES - 1:LANES] for x in p]
        run = None
        for i in range(npc):
            if run is not None:
                p[i] = combine(p[i], run)
            run = tot[i] if run is None else combine(run, tot[i])
        run = None
        for i in reversed(range(npc)):
            if run is not None:
                s[i] = combine(s[i], run)
            run = tot[i] if run is None else combine(run, tot[i])
        out.append((jnp.concatenate(p, axis=1), jnp.concatenate(s, axis=1)))


def _gate_tables(gts, dests):
    L = CHUNK
    fwd_rows = lax.broadcasted_iota(jnp.int32, (N_HD, L), 0) < H_B
    lis = [gt[0:N_HD, :] for gt in gts]
    sums = []
    yield from _lane_scans([_log_sigmoid(gt[N_HD:2 * N_HD, :]) for gt in gts], jnp.add, 0.0, sums)
    cums = [jnp.where(fwd_rows, ps, ss) for ps, ss in sums]
    rs = [li - cum for li, cum in zip(lis, cums)]
    maxs = []
    yield from _lane_scans(rs, jnp.maximum, -jnp.inf, maxs)
    for (grow_ref, c, gcol_ref, r0), cum, r, (pm, sm) in zip(dests, cums, rs, maxs):
        grow_ref[c] = jnp.concatenate([cum, r, jnp.where(fwd_rows, pm, sm)], axis=0)
        gcol_ref[r0:r0 + L, :] = jnp.concatenate([r, jnp.zeros((LANES - N_HD, L), F32)], axis=0).T


def _fill_pad(pad_s, xm_ref, n):
    zrow = jnp.zeros((_PAD, W_B), F32)
    pad_s[0:_PAD, :] = zrow
    pad_s[_PAD:_PAD + n, :] = xm_ref[...].astype(F32)
    pad_s[_PAD + n:2 * _PAD + n, :] = zrow


def _conv_gate_stage(c, pad_s, xm_ref, cw_ref, cb_ref, wgc_ref, wgm_ref, bg_ref, xc_out, xcb_s):
    L = CHUNK
    rows = slice(c * L, (c + 1) * L)
    r0 = c * L + _PAD
    row_id = lax.broadcasted_iota(jnp.int32, (L, W_B), 0)
    xcur = pad_s[r0:r0 + L, :]
    xprev = jnp.where(row_id == 0, pad_s[r0 - 1:r0, :], pltpu.roll(xcur, 1, 0))
    xnext = jnp.where(row_id == L - 1, pad_s[r0 + L:r0 + L + 1, :], pltpu.roll(xcur, L - 1, 0))
    pre = cb_ref[...] + xprev * cw_ref[0:1, :] + xcur * cw_ref[1:2, :] + xnext * cw_ref[2:3, :]
    xc = _silu(pre)
    xcb = xc.astype(BF16)
    if xc_out is not None:
        xc_out[rows, :] = xcb
    xcb_s[rows, :] = xcb
    g = _dot(xcb, wgc_ref[...]) + _dot(xm_ref[rows, :], wgm_ref[...]) + bg_ref[...]
    return g.T[0:2 * N_HD, :]


def _headwise_stage(c, xm_ref, xcb_s, wq_ref, wk_ref, wv_ref, q_s, k_s, vt_s):
    L = CHUNK
    rows = slice(c * L, (c + 1) * L)
    for hf in range(W_B // MXU_TILE):
        cs = slice(hf * MXU_TILE, (hf + 1) * MXU_TILE)
        xcb = xcb_s[rows, cs]
        if q_s is not None:
            q_s[rows, cs] = _dot(xcb, wq_ref[cs, cs]).astype(BF16)
        k_s[rows, cs] = _dot(xcb, wk_ref[cs, cs]).astype(BF16)
        vt_s[c, cs, :] = _dot(xm_ref[rows, cs], wv_ref[cs, cs]).T.astype(BF16)


_TAB_ROWS = 5 * N_HD


def _chain_tables(growc_s, grow_s, tab_s, nc):
    L = CHUNK
    fwd_rows = lax.broadcasted_iota(jnp.int32, (N_HD, L), 0) < H_B
    gc = growc_s[0]
    steps = [(gc, gc)] + [(grow_s[i], grow_s[nc - 1 - i]) for i in range(nc)]
    parts = []
    for g_f, g_b in steps:
        pick = lambda a: jnp.where(fwd_rows, g_f[a * N_HD:(a + 1) * N_HD, :], g_b[a * N_HD:(a + 1) * N_HD, :])
        cum, r, pm = pick(0), pick(1), pick(2)
        b_end = jnp.where(fwd_rows[:, 0:1], cum[:, L - 1:L], cum[:, 0:1])
        w = r + b_end
        parts.append((cum, pm, b_end, w, jnp.max(w, axis=1, keepdims=True)))
    m0 = jnp.zeros((N_HD, 1), F32)
    for idx, (cum, pm, b_end, w, wmax) in enumerate(parts):
        m_new = jnp.maximum(b_end + m0, wmax)
        mu = jnp.maximum(m0, pm)
        decay = jnp.broadcast_to(jnp.exp(b_end + m0 - m_new), (N_HD, L))
        tab_s[idx] = jnp.concatenate([mu, jnp.exp(m0 - mu), jnp.exp(-(cum + mu)),
                                      jnp.exp(w - m_new), decay], axis=0)
        m0 = m_new


def _tab(tab, a, hd):
    return tab[a * N_HD + hd:a * N_HD + hd + 1, :]


def _state_update(st_ref, hd, vt_ext, k_c, ws_row, decay_hd):
    vw = (vt_ext.astype(F32) * ws_row).astype(BF16)
    st_ref[hd] = decay_hd * st_ref[hd] + _dot(vw, k_c)


def _mlstm_kernel(xm_ref, xmc_ref, cw_ref, cb_ref, wq_ref, wk_ref, wv_ref, wgc_ref, wgm_ref, bg_ref,
                  hs_ref, xc_ref,
                  pad_s, xcb_s, q_s, k_s, vt_s, gcol_s, grow_s,
                  padc_s, xcbc_s, kc_s, vtc_s, gcolc_s, growc_s, tab_s, st_s):
    L = CHUNK
    t = xm_ref.shape[1]
    nc = t // L
    row_i = lax.broadcasted_iota(jnp.int32, (L, L), 0)
    col_i = lax.broadcasted_iota(jnp.int32, (L, L), 1)
    tri = (row_i <= col_i, row_i >= col_i)
    ones_rows = (lax.broadcasted_iota(jnp.int32, (ST_ROWS - DH_B, L), 0) == 0).astype(BF16)
    hsl = lambda hh: slice(hh * DH_B, (hh + 1) * DH_B)
    xm, xmc = xm_ref.at[0], xmc_ref.at[0]
    gate_w = (cw_ref, cb_ref, wgc_ref, wgm_ref, bg_ref)

    _fill_pad(padc_s, xmc, CTX_LEN)
    _fill_pad(pad_s, xm, t)
    gts = [_conv_gate_stage(0, padc_s, xmc, *gate_w, None, xcbc_s)]
    gts += [_conv_gate_stage(c, pad_s, xm, *gate_w, xc_ref.at[0], xcb_s) for c in range(nc)]
    dests = [(growc_s, 0, gcolc_s, 0)] + [(grow_s, c, gcol_s, c * L) for c in range(nc)]
    tables = _gate_tables(gts, dests)
    _headwise_stage(0, xmc, xcbc_s, wq_ref, wk_ref, wv_ref, None, kc_s, vtc_s)
    for c in range(nc):
        next(tables, None)
        next(tables, None)
        _headwise_stage(c, xm, xcb_s, wq_ref, wk_ref, wv_ref, q_s, k_s, vt_s)
    for _ in tables:
        pass

    _chain_tables(growc_s, grow_s, tab_s, nc)
    st_s[...] = jnp.zeros(st_s.shape, F32)
    tab = tab_s[0]
    for hd in range(N_HD):
        hh = hd % H_B
        vt_ext = jnp.concatenate([vtc_s[0, hsl(hh), :], ones_rows], axis=0)
        _state_update(st_s, hd, vt_ext, kc_s[:, hsl(hh)], _tab(tab, 3, hd), _tab(tab, 4, hd)[:, 0:DH_B])

    def body(i, carry, *, accumulate):
        ci = (i, nc - 1 - i)
        rows = tuple(pl.ds(pl.multiple_of(c * L, L), L) for c in ci)
        tab = tab_s[i + 1]
        gcols = tuple(gcol_s[r, :] for r in rows)
        live = {}

        def stage_scores(hd):
            d, hh = hd // H_B, hd % H_B
            k_c = k_s[rows[d], hsl(hh)]
            vt_ext = jnp.concatenate([vt_s[ci[d], hsl(hh), :], ones_rows], axis=0)
            lhs = jnp.concatenate([k_c, st_s[hd].astype(BF16)], axis=0)
            live[hd] = (k_c, vt_ext, _dot_nt(lhs, q_s[rows[d], hsl(hh)]))

        def stage_gate(hd):
            d = hd // H_B
            k_c, vt_ext, res = live[hd]
            rcol = gcols[d][:, hd:hd + 1]
            e = jnp.exp(jnp.where(tri[d], rcol - _tab(tab, 0, hd), -jnp.inf))
            p = (res[0:L, :] * e).astype(BF16)
            _state_update(st_s, hd, vt_ext, k_c, _tab(tab, 3, hd), _tab(tab, 4, hd)[:, 0:DH_B])
            live[hd] = (vt_ext, res[L:L + ST_ROWS, :], p)

        def stage_out(hd):
            d, hh = hd // H_B, hd % H_B
            vt_ext, inter, p = live.pop(hd)
            tot = _tab(tab, 1, hd) * inter + _dot(vt_ext, p)
            den = tot[DH_B:DH_B + 1, :]
            h_t = tot[0:DH_B, :] / jnp.maximum(jnp.abs(den), _tab(tab, 2, hd))
            for pc in range(L // LANES):
                piece = h_t[:, pc * LANES:(pc + 1) * LANES]
                if accumulate:
                    hs_ref[0, ci[d] * (L // LANES) + pc, hsl(hh), :] += piece
                else:
                    hs_ref[0, ci[d] * (L // LANES) + pc, hsl(hh), :] = piece

        stages = (stage_scores, stage_gate, stage_out)
        for step in range(N_HD + len(stages) - 1):
            for si in range(len(stages)):
                if 0 <= step - si < N_HD:
                    stages[si](step - si)
        return carry

    lax.fori_loop(0, nc // 2, functools.partial(body, accumulate=False), 0)
    lax.fori_loop(nc // 2, nc, functools.partial(body, accumulate=True), 0)


def _mlstm_out(hs_t, som, szm, xc, mhg, skip):
    hsum = hs_t.T * som.astype(F32)
    parts = [_layer_norm(hsum[:, hh * DH_B:(hh + 1) * DH_B]) for hh in range(H_B)]
    hb = jnp.concatenate(parts, axis=1) * mhg + skip * xc.astype(F32)
    return (hb * szm.astype(F32)).astype(BF16)


def _mlstm_call(xm, xmc, conv_w, conv_b, wq, wk, wv, wgc, wgm, bg):
    b, t, _ = xm.shape
    nc = t // CHUNK
    assert nc % 2 == 0
    seq = lambda n: pl.BlockSpec((1, n, W_B), lambda i: (i, 0, 0))
    const = lambda shape: pl.BlockSpec(shape, lambda i: (0,) * len(shape))
    scratch = [
        pltpu.VMEM((t + 2 * _PAD, W_B), F32),
        pltpu.VMEM((t, W_B), BF16),
        pltpu.VMEM((t, W_B), BF16),
        pltpu.VMEM((t, W_B), BF16),
        pltpu.VMEM((nc, W_B, CHUNK), BF16),
        pltpu.VMEM((t, LANES), F32),
        pltpu.VMEM((nc, 3 * N_HD, CHUNK), F32),
        pltpu.VMEM((CTX_LEN + 2 * _PAD, W_B), F32),
        pltpu.VMEM((CTX_LEN, W_B), BF16),
        pltpu.VMEM((CTX_LEN, W_B), BF16),
        pltpu.VMEM((1, W_B, CHUNK), BF16),
        pltpu.VMEM((CTX_LEN, LANES), F32),
        pltpu.VMEM((1, 3 * N_HD, CHUNK), F32),
        pltpu.VMEM((nc + 1, _TAB_ROWS, CHUNK), F32),
        pltpu.VMEM((N_HD, ST_ROWS, DH_B), F32),
    ]
    return pl.pallas_call(
        _mlstm_kernel,
        grid=(b,),
        in_specs=[seq(t), seq(CTX_LEN), const((3, W_B)), const((1, W_B)),
                  const((W_B, W_B)), const((W_B, W_B)), const((W_B, W_B)),
                  const((W_B, LANES)), const((W_B, LANES)), const((1, LANES))],
        out_specs=[pl.BlockSpec((1, t // LANES, W_B, LANES), lambda i: (i, 0, 0, 0)), seq(t)],
        out_shape=[jax.ShapeDtypeStruct((b, t // LANES, W_B, LANES), F32),
                   jax.ShapeDtypeStruct((b, t, W_B), BF16)],
        scratch_shapes=scratch,
        compiler_params=pltpu.CompilerParams(dimension_semantics=("arbitrary",),
                                             vmem_limit_bytes=VMEM_LIMIT),
        name="mlstm_bidir",
    )(xm, xmc, conv_w, conv_b, wq, wk, wv, wgc, wgm, bg)


_H_SLOTS = 3


def _out_kernel(h_hbm, mod_ref, ya_ref, yb_ref, wo_ref, g_ref, b_ref, o_ref, h_buf, h_sem, *, n_sub):
    tm = o_ref.shape[1]
    sub = tm // n_sub
    gate = mod_ref[0][:, 2 * D_MODEL:3 * D_MODEL]
    nj = pl.num_programs(1)
    step, n_steps = pl.program_id(0) * nj + pl.program_id(1), pl.num_programs(0) * nj

    def h_copy(k):
        slot = k % _H_SLOTS
        src = h_hbm.at[k // nj, pl.ds(pl.multiple_of((k % nj) * tm, tm), tm), :]
        return pltpu.make_async_copy(src, h_buf.at[slot], h_sem.at[slot])

    @pl.when(step == 0)
    def _():
        h_copy(step).start()
        h_copy(step + 1).start()

    @pl.when(step + 2 < n_steps)
    def _():
        h_copy(step + 2).start()

    h_copy(step).wait()
    h_ref = h_buf.at[step % _H_SLOTS]

    def mix_stage(s):
        rows = slice(s * sub, (s + 1) * sub)
        return (_dot(ya_ref[0, rows, :], wo_ref[0:W_A, :])
                + _dot(yb_ref[0, rows, :], wo_ref[W_A:W_A + W_B, :]))

    def norm_stage(s, y):
        rows = slice(s * sub, (s + 1) * sub)
        o_ref[0, rows, :] = _layer_norm(ALPHA * h_ref[rows, :] + gate * y) * g_ref[...] + b_ref[...]

    y = mix_stage(0)
    for s in range(n_sub):
        y_next = mix_stage(s + 1) if s + 1 < n_sub else None
        norm_stage(s, y)
        y = y_next


def _out_call(h, mod3, ya, yb, wo, g, bb, *, tm, n_sub):
    b, t, _ = h.shape
    const = lambda shape: pl.BlockSpec(shape, lambda i, j: (0,) * len(shape))
    row = lambda w: pl.BlockSpec((1, tm, w), lambda i, j: (i, j, 0))
    assert b * (t // tm) >= 2
    return pl.pallas_call(
        functools.partial(_out_kernel, n_sub=n_sub),
        grid=(b, t // tm),
        in_specs=[pl.BlockSpec(memory_space=pl.ANY), pl.BlockSpec((1, 1, 3 * D_MODEL), lambda i, j: (i, 0, 0)),
                  row(W_A), row(W_B),
                  const((W_A + W_B, D_MODEL)), const((1, D_MODEL)), const((1, D_MODEL))],
        out_specs=row(D_MODEL),
        out_shape=jax.ShapeDtypeStruct((b, t, D_MODEL), F32),
        scratch_shapes=[pltpu.VMEM((_H_SLOTS, tm, D_MODEL), F32), pltpu.SemaphoreType.DMA((_H_SLOTS,))],
        compiler_params=pltpu.CompilerParams(dimension_semantics=("arbitrary", "arbitrary"),
                                             vmem_limit_bytes=VMEM_LIMIT),
        name="out_proj_ln",
    )(h, mod3, ya, yb, wo, g, bb)


def _rope_tables(seq, scale_keep, scale_rope):
    n_rows = seq // GRID_W
    rowp = np.repeat(np.arange(n_rows, dtype=np.float32), GRID_W)
    colp = np.tile(np.arange(GRID_W, dtype=np.float32), n_rows)
    inv = (np.float32(ROPE_BASE) ** (-np.arange(ROPE_FREQS, dtype=np.float32) / np.float32(ROPE_FREQS)))
    ang = np.stack([rowp[:, None] * inv, colp[:, None] * inv], axis=1).astype(np.float32)
    cos = np.broadcast_to(np.cos(ang)[:, :, None, :], (seq, 2, 2, ROPE_FREQS)).reshape(seq, DR_A)
    sin = np.broadcast_to(np.sin(ang)[:, :, None, :], (seq, 2, 2, ROPE_FREQS)).reshape(seq, DR_A)
    z32 = np.zeros((seq, DR_A), np.float32)
    t1 = np.concatenate([np.full((seq, DN_A), scale_keep, np.float32), z32, cos * scale_rope], axis=1)
    t2 = np.concatenate([np.zeros((seq, DN_A), np.float32), z32, sin * scale_rope], axis=1)
    return jnp.asarray(t1, F32), jnp.asarray(t2, F32)


def kernel(x, c, ctx, c_ctx, ln_in_g, ln_in_b, w_ada, b_ada, w_in, g_qa, w_qb, g_kva, w_kvb, conv_w, conv_b, w_mq, w_mk, w_mv, w_gate, b_gate, mh_g, skip, w_out, ln_g, ln_b):
    b, t, _ = x.shape
    l = 0
    r2 = lambda v: v.reshape(1, -1)

    blocks = lambda w: jnp.transpose(w[l], (1, 2, 0)).reshape(QKV_BS * QKV_BS, -1)
    wo = w_out[l].astype(BF16)

    sm_scale = (DN_A + DR_A) ** -0.5 * LOG2_E
    t1q, t2q = _rope_tables(t, sm_scale, sm_scale)
    t1k, t2k = _rope_tables(t, 0.0, 1.0)
    n_ctx = ctx.shape[1]

    mod3, wina, winr, wmq, wmk_s, wmv, wgc, wgm, bg, wq, wkn, wvt = _prep_call(
        c, r2(c_ctx), w_ada[l], r2(b_ada[l]), jnp.swapaxes(w_in[l], 0, 1), blocks(w_mq), blocks(w_mk),
        blocks(w_mv), jnp.swapaxes(w_gate[l], 0, 1), r2(b_gate[l]), w_qb[l], w_kvb[l])

    lng, lnb = r2(ln_in_g), r2(ln_in_b)
    shared = (wina, winr, r2(g_qa[l]), wq, r2(g_kva[l]), wkn, wvt)
    q, kl, vtl, xm, sza, som, szm, h = _proj_call(x, mod3, None, lng, lnb, *shared, t1q, t2q, t1k, t2k,
                                                  tm=1024, n_sub=4, latent=True)
    kc, vtc, xmc = _proj_call(ctx.reshape(1, b * n_ctx, D_MODEL), mod3, b, lng, lnb, *shared,
                              tm=min(1024, b * n_ctx), n_sub=2, latent=False)
    xmc = xmc.reshape(b, n_ctx, W_B)

    hs, xc = _mlstm_call(xm, xmc, conv_w[l], r2(conv_b[l]), wmq, wmk_s, wmv, wgc, wgm, bg)
    ya, yb = _attn_call(q, kc, kl, vtc, vtl, sza, hs, som, szm, xc, r2(mh_g[l]), r2(skip[l]), tq=512)
    return _out_call(h, mod3, ya, yb, wo, r2(ln_g[l]), r2(ln_b[l]), tm=1024, n_sub=4)
```

```python
import functools

import numpy as np
import jax
import jax.numpy as jnp
from jax import lax
from jax.experimental import pallas as pl
from jax.experimental.pallas import tpu as pltpu

F32 = jnp.float32
BF16 = jnp.bfloat16

D_MODEL = 1024
CTX_LEN = 256
GRID_W = 64
H_A, DN_A, DR_A, DV_A = 8, 64, 32, 64
W_A = H_A * DV_A
Q_LORA, KV_LORA = 256, 128
ROPE_FREQS = DR_A // 4
ROPE_BASE = 10000.0
H_B, DH_B = 4, 128
W_B = H_B * DH_B
QKV_BS = 4
DEPTH = 1
ALPHA = (2.0 * DEPTH) ** 0.25
LN_EPS = 1e-5
RMS_EPS = 1e-6
LOG2_E = 1.4426950408889634

LANES = 128
SUBLANES = 8
MXU_TILE = 256
SLAB = LANES
DV_EXT = DV_A + 16
ATTN_KEY_CHUNK = MXU_TILE
CHUNK = 256
N_HD = 2 * H_B
ST_ROWS = DH_B + 16
VMEM_LIMIT = 56 * 1024 * 1024

_NT = (((1,), (1,)), ((), ()))


def _dot(a, b):
    return jnp.dot(a, b, preferred_element_type=F32)


def _dot_nt(a, b):
    return lax.dot_general(a, b, _NT, preferred_element_type=F32)


def _layer_norm(x):
    mu = jnp.mean(x, axis=-1, keepdims=True)
    xc = x - mu
    var = jnp.mean(xc * xc, axis=-1, keepdims=True)
    return xc * lax.rsqrt(var + LN_EPS)


def _rms_norm(x, g):
    return (x * lax.rsqrt(jnp.mean(x * x, axis=-1, keepdims=True) + RMS_EPS)) * g


def _silu(x):
    return x * jax.nn.sigmoid(x)


def _log_sigmoid(x):
    return jnp.minimum(x, 0.0) - jnp.log1p(jnp.exp(-jnp.abs(x)))


def _ada_kernel(c_ref, cctx_ref, w_ref, b_ref, o_ref):
    first = lax.broadcasted_iota(jnp.int32, (SUBLANES, D_MODEL), 0) == 0
    cc = jnp.concatenate([c_ref[...], jnp.where(first, cctx_ref[...], 0.0)], axis=0)
    a = _silu(cc).astype(BF16)
    mod = _dot(a, w_ref[...].astype(BF16)) + b_ref[...]
    for r in range(o_ref.shape[0]):
        o_ref[r] = mod[r:r + 1, :]


_S_KVA, _S_KR = Q_LORA + KV_LORA, Q_LORA + KV_LORA + DR_A


def _win_kernel(wt_ref, wina_ref, winr_ref):
    n_r = winr_ref.shape[1]
    for j in range(_S_KVA // LANES):
        wina_ref[:, j * LANES:(j + 1) * LANES] = wt_ref[j * LANES:(j + 1) * LANES, :].T.astype(BF16)
    kr = wt_ref[_S_KVA:_S_KR, :]
    a0, b0, a1, b1 = (kr[i * ROPE_FREQS:(i + 1) * ROPE_FREQS, :] for i in range(4))
    blk = jnp.concatenate([jnp.zeros((DN_A, D_MODEL), F32), -b0, a0, -b1, a1, kr], axis=0)
    wina_ref[:, _S_KVA:_S_KVA + LANES] = blk.T.astype(BF16)
    for j in range(n_r // MXU_TILE):
        rows = slice(_S_KR + j * MXU_TILE, _S_KR + (j + 1) * MXU_TILE)
        winr_ref[:, j * MXU_TILE:(j + 1) * MXU_TILE] = wt_ref[rows, :].T.astype(BF16)


def _rope_slab(s, t1, t2):
    return s * t1 + pltpu.roll(s, DR_A, 1) * t2


def _proj_kernel(x_ref, mod_ref, lng_ref, lnb_ref, wina_ref, winr_ref, gqa_ref, wq_ref, gkva_ref,
                 wkn_ref, wvt_ref, *refs, latent, n_sub):
    if latent:
        t1q_ref, t2q_ref, t1k_ref, t2k_ref, q_ref, k_ref, vt_ref, xm_ref, sza_ref, som_ref, szm_ref, h_ref = refs
    else:
        k_ref, vt_ref, xm_ref = refs
    sub = x_ref.shape[1] // n_sub
    mod = mod_ref[0]
    shift, scale1 = mod[:, 0:D_MODEL], 1.0 + mod[:, D_MODEL:2 * D_MODEL]
    ones_rows = (lax.broadcasted_iota(jnp.int32, (DV_EXT - DV_A, sub), 0) == 0).astype(BF16)

    def norm_stage(s):
        rows = slice(s * sub, (s + 1) * sub)
        h = _layer_norm(x_ref[0, rows, :]) * lng_ref[...] + lnb_ref[...]
        if latent:
            h_ref[0, rows, :] = h
        return (h * scale1 + shift).astype(BF16)

    def proj_stage(s, u):
        rows = slice(s * sub, (s + 1) * sub)
        p0 = _dot(u, wina_ref[...])
        d_0 = _dot(u, winr_ref[:, 0:512] if latent else winr_ref[:, 512:1024])
        d_om = _dot(u, winr_ref[:, 1024:1536]) if latent else None
        d_zm = _dot(u, winr_ref[:, 1536:2048]) if latent else None
        kvn = _rms_norm(p0[:, 256:384], gkva_ref[...]).astype(BF16)
        kn = _dot(kvn, wkn_ref[...])
        if latent:
            sza_ref[0, rows, :] = _silu(d_0).astype(BF16)
            krr = _rope_slab(p0[:, 384:512], t1k_ref[rows, :], t2k_ref[rows, :])
        else:
            rope_lanes = lax.broadcasted_iota(jnp.int32, (sub, SLAB), 1) >= SLAB - DR_A
            krr = jnp.where(rope_lanes, p0[:, 384:512], 0.0)
        for hh in range(H_A):
            k_ref[0, hh, rows, :] = (kn[:, hh * SLAB:(hh + 1) * SLAB] + krr).astype(BF16)
        vt = _dot_nt(wvt_ref[...], kvn).astype(BF16)
        for hh in range(H_A):
            vt_ref[0, hh, 0:DV_A, rows] = vt[hh * DV_A:(hh + 1) * DV_A, :]
            vt_ref[0, hh, DV_A:DV_EXT, rows] = ones_rows
        if not latent:
            xm_ref[0, rows, :] = d_0.astype(BF16)
        else:
            qn = _rms_norm(p0[:, 0:256], gqa_ref[...]).astype(BF16)
            qs = _dot(qn, wq_ref[...])
            d_xm = _dot(u, winr_ref[:, 512:1024])
            t1q, t2q = t1q_ref[rows, :], t2q_ref[rows, :]
            for hh in range(H_A):
                q_ref[0, hh, rows, :] = _rope_slab(qs[:, hh * SLAB:(hh + 1) * SLAB], t1q, t2q).astype(BF16)
            som_ref[0, rows, :] = jax.nn.sigmoid(d_om).astype(BF16)
            szm_ref[0, rows, :] = _silu(d_zm).astype(BF16)
            xm_ref[0, rows, :] = d_xm.astype(BF16)

    u = norm_stage(0)
    for s in range(n_sub):
        u_next = norm_stage(s + 1) if s + 1 < n_sub else None
        proj_stage(s, u)
        u = u_next


def _proj_call(x, mod3, mod_row, lng, lnb, wina, winr, gqa, wq, gkva, wkn, wvt, *tables, tm, n_sub, latent):
    b, t, _ = x.shape
    const = lambda shape: pl.BlockSpec(shape, lambda i, j: (0,) * len(shape))
    tab = pl.BlockSpec((tm, SLAB), lambda i, j: (j, 0))
    assert len(tables) == (4 if latent else 0)
    row = lambda w: pl.BlockSpec((1, tm, w), lambda i, j: (i, j, 0))
    if mod_row is None:
        mod_spec = pl.BlockSpec((1, 1, 3 * D_MODEL), lambda i, j: (i, 0, 0))
    else:
        mod_spec = pl.BlockSpec((1, 1, 3 * D_MODEL), lambda i, j: (mod_row, 0, 0))
    in_specs = [row(D_MODEL), mod_spec, const((1, D_MODEL)), const((1, D_MODEL)),
                const(wina.shape), const(winr.shape), const((1, Q_LORA)), const((Q_LORA, H_A * SLAB)),
                const((1, KV_LORA)), const((KV_LORA, H_A * SLAB)), const((W_A, KV_LORA))]
    in_specs += [tab] * len(tables)
    k_spec = pl.BlockSpec((1, H_A, tm, SLAB), lambda i, j: (i, 0, j, 0))
    vt_spec = pl.BlockSpec((1, H_A, DV_EXT, tm), lambda i, j: (i, 0, 0, j))
    k_shape = jax.ShapeDtypeStruct((b, H_A, t, SLAB), BF16)
    vt_shape = jax.ShapeDtypeStruct((b, H_A, DV_EXT, t), BF16)
    half = jax.ShapeDtypeStruct((b, t, W_B), BF16)
    if latent:
        out_specs = [k_spec, k_spec, vt_spec, row(W_B), row(W_A), row(W_B), row(W_B), row(D_MODEL)]
        out_shape = [k_shape, k_shape, vt_shape, half, half, half, half,
                     jax.ShapeDtypeStruct((b, t, D_MODEL), F32)]
    else:
        out_specs = [k_spec, vt_spec, row(W_B)]
        out_shape = [k_shape, vt_shape, half]
    return pl.pallas_call(
        functools.partial(_proj_kernel, latent=latent, n_sub=n_sub),
        grid=(b, t // tm),
        in_specs=in_specs, out_specs=out_specs, out_shape=out_shape,
        compiler_params=pltpu.CompilerParams(dimension_semantics=("parallel", "parallel"),
                                             vmem_limit_bytes=VMEM_LIMIT),
        name="in_proj_latent" if latent else "in_proj_ctx",
    )(x, mod3, lng, lnb, wina, winr, gqa, wq, gkva, wkn, wvt, *tables)


def _attn_kernel(q_ref, kc_ref, kl_ref, vtc_ref, vtl_ref, sza_ref, hs_ref, som_ref, szm_ref, xc_ref,
                 mhg_ref, skip_ref, o_ref, yb_ref, s_buf, ot_s):
    n_ctx, t = kc_ref.shape[2], kl_ref.shape[2]
    kc = ATTN_KEY_CHUNK

    def scores(h, slot):
        qh = q_ref[0, h]
        sc = _dot_nt(kc_ref[0, h], qh)
        sk = _dot_nt(kl_ref[0, h], qh)
        s_buf[slot, 0:n_ctx, :] = sc
        s_buf[slot, n_ctx:n_ctx + t, :] = sk
        return jnp.maximum(jnp.max(sc, axis=0, keepdims=True), jnp.max(sk, axis=0, keepdims=True))

    def values(h, slot, m):
        chunks = [(vtc_ref, c * kc, c * kc) for c in range(n_ctx // kc)]
        chunks += [(vtl_ref, c * kc, n_ctx + c * kc) for c in range(t // kc)]
        acc = None
        for vref, v0, s0 in chunks:
            p = jnp.exp2(s_buf[slot, s0:s0 + kc, :] - m).astype(BF16)
            d = _dot(vref[0, h, :, v0:v0 + kc], p)
            acc = d if acc is None else acc + d
        rows = pl.ds(pl.multiple_of(h * DV_A, DV_A), DV_A)
        ot_s[rows, :] = acc[0:DV_A, :] / acc[DV_A:DV_A + 1, :]

    def cell_out(pc):
        rows = pl.ds(pl.multiple_of(pc * LANES, LANES), LANES)
        yb_ref[0, rows, :] = _mlstm_out(hs_ref[0, pc], som_ref[0, rows, :], szm_ref[0, rows, :],
                                        xc_ref[0, rows, :], mhg_ref[...], skip_ref[...])

    def body(i, m_even):
        h = 2 * i
        m_odd = scores(h + 1, 1)
        cell_out(i + 1)
        values(h, 0, m_even)
        m_even = scores(h + 2, 0)
        values(h + 1, 1, m_odd)
        return m_even

    assert hs_ref.shape[1] == H_A // 2
    m_first = scores(0, 0)
    cell_out(0)
    m_even = lax.fori_loop(0, H_A // 2 - 1, body, m_first)
    m_odd = scores(H_A - 1, 1)
    values(H_A - 2, 0, m_even)
    values(H_A - 1, 1, m_odd)
    o_ref[0] = (ot_s[...].T * sza_ref[0].astype(F32)).astype(BF16)


def _attn_call(q, kc, kl, vtc, vtl, sza, hs, som, szm, xc, mhg, skip, *, tq):
    b, _, t, _ = q.shape
    n_ctx = kc.shape[2] // b
    head_blk = lambda n, w: pl.BlockSpec((1, H_A, n, w), lambda i, j: (i, 0, 0, 0))
    const = lambda shape: pl.BlockSpec(shape, lambda i, j: (0,) * len(shape))
    row = lambda w: pl.BlockSpec((1, tq, w), lambda i, j: (i, j, 0))
    half = jax.ShapeDtypeStruct((b, t, W_A), BF16)
    return pl.pallas_call(
        _attn_kernel,
        grid=(b, t // tq),
        in_specs=[pl.BlockSpec((1, H_A, tq, SLAB), lambda i, j: (i, 0, j, 0)),
                  pl.BlockSpec((1, H_A, n_ctx, SLAB), lambda i, j: (0, 0, i, 0)), head_blk(t, SLAB),
                  pl.BlockSpec((1, H_A, DV_EXT, n_ctx), lambda i, j: (0, 0, 0, i)), head_blk(DV_EXT, t),
                  row(W_A),
                  pl.BlockSpec((1, tq // LANES, W_B, LANES), lambda i, j: (i, j, 0, 0)),
                  row(W_B), row(W_B), row(W_B), const((1, W_B)), const((1, W_B))],
        out_specs=[row(W_A), row(W_B)],
        out_shape=[half, jax.ShapeDtypeStruct((b, t, W_B), BF16)],
        scratch_shapes=[pltpu.VMEM((2, n_ctx + t, tq), F32),
                        pltpu.VMEM((W_A, tq), F32)],
        compiler_params=pltpu.CompilerParams(dimension_semantics=("parallel", "parallel"),
                                             vmem_limit_bytes=VMEM_LIMIT),
        name="mla_attention",
    )(q, kc, kl, vtc, vtl, sza, hs, som, szm, xc, mhg, skip)


def _onehot(cond):
    return cond.astype(F32).astype(BF16)


def _block_diag(a):
    bs, nb = QKV_BS, W_B // QKV_BS
    lb, sq = bs.bit_length() - 1, bs * bs
    assert bs == 1 << lb
    iota = lambda shape, d: lax.broadcasted_iota(jnp.int32, shape, d)
    spread = _onehot(iota((nb, W_B), 1) >> lb == iota((nb, W_B), 0))
    b = _dot(a, spread)
    col_o = iota((sq, W_B), 1) & (bs - 1)
    bm = jnp.concatenate([jnp.where(col_o == o, b, 0.0) for o in range(bs)], axis=0).astype(BF16)
    j = iota((W_B, bs * sq), 1)
    pick = _onehot((j & (sq - 1)) == ((iota((W_B, bs * sq), 0) & (bs - 1)) << lb) + (j >> (2 * lb)))
    same_block = iota((W_B, W_B), 0) >> lb == iota((W_B, W_B), 1) >> lb
    return jnp.where(same_block, _dot(pick, bm), 0.0).astype(BF16)


def _mlstm_weight_kernel(aq_ref, ak_ref, av_ref, wgt_ref, bgate_ref,
                         wmq_ref, wmk_ref, wmv_ref, wgc_ref, wgm_ref, bg_ref):
    a_q, a_k, a_v = aq_ref[...], ak_ref[...], av_ref[...]
    wq, wk, wv = (_block_diag(a.astype(BF16)) for a in (a_q, a_k, a_v))
    wmq_ref[...] = wq
    wmk_ref[...] = _block_diag((a_k * (DH_B ** -0.5)).astype(BF16))
    wmv_ref[...] = wv
    dst, src = (lax.broadcasted_iota(jnp.int32, (LANES, 2 * N_HD), d) for d in (0, 1))
    grp = dst >> (H_B.bit_length() - 1)
    src_of_dst = (dst & (H_B - 1)) + H_B * jnp.where(grp == 1, 2, jnp.where(grp == 2, 1, grp))
    perm = _onehot((dst < 2 * N_HD) & (src == src_of_dst))
    wg_t = _dot(perm, wgt_ref[...].astype(BF16)).astype(BF16)
    wgc_ref[...] = (_dot_nt(wq, wg_t[:, 0:W_B]) + _dot_nt(wk, wg_t[:, W_B:2 * W_B])).astype(BF16)
    wgm_ref[...] = _dot_nt(wv, wg_t[:, 2 * W_B:3 * W_B]).astype(BF16)
    bg_ref[...] = jnp.zeros(bg_ref.shape, F32)
    for d, s in enumerate((0, 2, 1, 3)):
        bg_ref[:, d * H_B:(d + 1) * H_B] = bgate_ref[:, s * H_B:(s + 1) * H_B]


def _qb_kernel(wqb_ref, wq_ref):
    wt = wqb_ref[...].T
    hw = DN_A + DR_A
    for hh in range(H_A):
        rope = wt[hh * hw + DN_A:(hh + 1) * hw, :]
        a0, b0, a1, b1 = (rope[i * ROPE_FREQS:(i + 1) * ROPE_FREQS, :] for i in range(4))
        slab = jnp.concatenate([wt[hh * hw:hh * hw + DN_A, :], -b0, a0, -b1, a1, rope], axis=0)
        wq_ref[:, hh * SLAB:(hh + 1) * SLAB] = slab.T.astype(BF16)


def _kvb_kernel(wkvb_ref, wkn_ref, wvt_ref):
    w = wkvb_ref[...]
    assert DN_A + DV_A == SLAB
    lane = lax.broadcasted_iota(jnp.int32, w.shape, 1)
    wkn_ref[...] = jnp.where((lane & (SLAB - 1)) < DN_A, w, 0.0).astype(BF16)
    wt = w.T
    for hh in range(H_A):
        wvt_ref[hh * DV_A:(hh + 1) * DV_A, :] = wt[hh * SLAB + DN_A:(hh + 1) * SLAB, :].astype(BF16)


def _prep_kernel(c_ref, cctx_ref, wada_hbm, bada_ref, wt_hbm, aq_ref, ak_ref, av_ref, wgt_ref, bgate_ref,
                 wqb_ref, wkvb_ref,
                 mod_ref, wina_hbm, winr_hbm, wmq_ref, wmk_ref, wmv_ref, wgc_ref, wgm_ref, bg_ref,
                 wq_ref, wkn_ref, wvt_ref, wt_buf, wada_buf, wina_buf, winr_buf, sem):
    wt_copy = pltpu.make_async_copy(wt_hbm, wt_buf, sem.at[0])
    wada_copy = pltpu.make_async_copy(wada_hbm, wada_buf, sem.at[1])
    wt_copy.start()
    wada_copy.start()
    _mlstm_weight_kernel(aq_ref, ak_ref, av_ref, wgt_ref, bgate_ref,
                         wmq_ref, wmk_ref, wmv_ref, wgc_ref, wgm_ref, bg_ref)
    _qb_kernel(wqb_ref, wq_ref)
    _kvb_kernel(wkvb_ref, wkn_ref, wvt_ref)
    wt_copy.wait()
    _win_kernel(wt_buf, wina_buf, winr_buf)
    outs = [pltpu.make_async_copy(wina_buf, wina_hbm, sem.at[2]),
            pltpu.make_async_copy(winr_buf, winr_hbm, sem.at[3])]
    for cp in outs:
        cp.start()
    wada_copy.wait()
    _ada_kernel(c_ref, cctx_ref, wada_buf, bada_ref, mod_ref)
    for cp in outs:
        cp.wait()


def _prep_call(c, c_ctx, w_ada, b_ada, wt, aq, ak, av, wgt, b_gate, w_qb, w_kvb):
    n_in, d = wt.shape
    assert c.shape[0] % SUBLANES == 0 and wgt.shape == (2 * N_HD, 3 * W_B)
    assert aq.shape == ak.shape == av.shape == (QKV_BS * QKV_BS, W_B // QKV_BS)
    bf = lambda *shape: jax.ShapeDtypeStruct(shape, BF16)
    vmem, hbm = pl.BlockSpec(memory_space=pltpu.VMEM), pl.BlockSpec(memory_space=pl.ANY)
    return pl.pallas_call(
        _prep_kernel,
        in_specs=[vmem, vmem, hbm, vmem, hbm] + [vmem] * 7,
        out_specs=[vmem, hbm, hbm] + [vmem] * 9,
        scratch_shapes=[pltpu.VMEM(wt.shape, F32), pltpu.VMEM(w_ada.shape, F32),
                        pltpu.VMEM((d, _S_KVA + LANES), BF16), pltpu.VMEM((d, n_in - _S_KR), BF16),
                        pltpu.SemaphoreType.DMA((4,))],
        out_shape=[jax.ShapeDtypeStruct((c.shape[0] + 1, 1, w_ada.shape[1]), F32),
                   bf(d, _S_KVA + LANES), bf(d, n_in - _S_KR),
                   bf(W_B, W_B), bf(W_B, W_B), bf(W_B, W_B), bf(W_B, LANES), bf(W_B, LANES),
                   jax.ShapeDtypeStruct((1, LANES), F32),
                   bf(Q_LORA, H_A * SLAB), bf(KV_LORA, H_A * SLAB), bf(W_A, KV_LORA)],
        compiler_params=pltpu.CompilerParams(vmem_limit_bytes=VMEM_LIMIT),
        name="param_prep",
    )(c, c_ctx, w_ada, b_ada, wt, aq, ak, av, wgt, b_gate, w_qb, w_kvb)


_PAD = SUBLANES


def _lane_scans(rows_per_chunk, combine, fill, out):
    npc = CHUNK // LANES
    flat = [x[:, i * LANES:(i + 1) * LANES] for x in rows_per_chunk for i in range(npc)]
    lane = lax.broadcasted_iota(jnp.int32, flat[0].shape, 1)
    pre, suf, sh = list(flat), list(flat), 1
    while sh < LANES:
        pre = [combine(p, jnp.where(lane >= sh, pltpu.roll(p, sh, 1), fill)) for p in pre]
        suf = [combine(s, jnp.where(lane < LANES - sh, pltpu.roll(s, LANES - sh, 1), fill)) for s in suf]
        sh *= 2
        yield
    for c in range(len(rows_per_chunk)):
        p, s = pre[c * npc:(c + 1) * npc], suf[c * npc:(c + 1) * npc]
        tot = [x[:, LANES - 1:LANES] for x in p]
        run = None
        for i in range(npc):
            if run is not None:
                p[i] = combine(p[i], run)
            run = tot[i] if run is None else combine(run, tot[i])
        run = None
        for i in reversed(range(npc)):
            if run is not None:
                s[i] = combine(s[i], run)
            run = tot[i] if run is None else combine(run, tot[i])
        out.append((jnp.concatenate(p, axis=1), jnp.concatenate(s, axis=1)))


def _gate_tables(gts, dests):
    L = CHUNK
    fwd_rows = lax.broadcasted_iota(jnp.int32, (N_HD, L), 0) < H_B
    lis = [gt[0:N_HD, :] for gt in gts]
    sums = []
    yield from _lane_scans([_log_sigmoid(gt[N_HD:2 * N_HD, :]) for gt in gts], jnp.add, 0.0, sums)
    cums = [jnp.where(fwd_rows, ps, ss) for ps, ss in sums]
    rs = [li - cum for li, cum in zip(lis, cums)]
    maxs = []
    yield from _lane_scans(rs, jnp.maximum, -jnp.inf, maxs)
    for (grow_ref, c, gcol_ref, r0), cum, r, (pm, sm) in zip(dests, cums, rs, maxs):
        grow_ref[c] = jnp.concatenate([cum, r, jnp.where(fwd_rows, pm, sm)], axis=0)
        gcol_ref[r0:r0 + L, :] = jnp.concatenate([r, jnp.zeros((LANES - N_HD, L), F32)], axis=0).T


def _fill_pad(pad_s, xm_ref, n):
    zrow = jnp.zeros((_PAD, W_B), F32)
    pad_s[0:_PAD, :] = zrow
    pad_s[_PAD:_PAD + n, :] = xm_ref[...].astype(F32)
    pad_s[_PAD + n:2 * _PAD + n, :] = zrow


def _conv_gate_stage(c, pad_s, xm_ref, cw_ref, cb_ref, wgc_ref, wgm_ref, bg_ref, xc_out, xcb_s):
    L = CHUNK
    rows = slice(c * L, (c + 1) * L)
    r0 = c * L + _PAD
    row_id = lax.broadcasted_iota(jnp.int32, (L, W_B), 0)
    xcur = pad_s[r0:r0 + L, :]
    xprev = jnp.where(row_id == 0, pad_s[r0 - 1:r0, :], pltpu.roll(xcur, 1, 0))
    xnext = jnp.where(row_id == L - 1, pad_s[r0 + L:r0 + L + 1, :], pltpu.roll(xcur, L - 1, 0))
    pre = cb_ref[...] + xprev * cw_ref[0:1, :] + xcur * cw_ref[1:2, :] + xnext * cw_ref[2:3, :]
    xc = _silu(pre)
    xcb = xc.astype(BF16)
    if xc_out is not None:
        xc_out[rows, :] = xcb
    xcb_s[rows, :] = xcb
    g = _dot(xcb, wgc_ref[...]) + _dot(xm_ref[rows, :], wgm_ref[...]) + bg_ref[...]
    return g.T[0:2 * N_HD, :]


def _headwise_stage(c, xm_ref, xcb_s, wq_ref, wk_ref, wv_ref, q_s, k_s, vt_s):
    L = CHUNK
    rows = slice(c * L, (c + 1) * L)
    for hf in range(W_B // MXU_TILE):
        cs = slice(hf * MXU_TILE, (hf + 1) * MXU_TILE)
        xcb = xcb_s[rows, cs]
        if q_s is not None:
            q_s[rows, cs] = _dot(xcb, wq_ref[cs, cs]).astype(BF16)
        k_s[rows, cs] = _dot(xcb, wk_ref[cs, cs]).astype(BF16)
        vt_s[c, cs, :] = _dot(xm_ref[rows, cs], wv_ref[cs, cs]).T.astype(BF16)


_TAB_ROWS = 5 * N_HD


def _chain_tables(growc_s, grow_s, tab_s, nc):
    L = CHUNK
    fwd_rows = lax.broadcasted_iota(jnp.int32, (N_HD, L), 0) < H_B
    gc = growc_s[0]
    steps = [(gc, gc)] + [(grow_s[i], grow_s[nc - 1 - i]) for i in range(nc)]
    parts = []
    for g_f, g_b in steps:
        pick = lambda a: jnp.where(fwd_rows, g_f[a * N_HD:(a + 1) * N_HD, :], g_b[a * N_HD:(a + 1) * N_HD, :])
        cum, r, pm = pick(0), pick(1), pick(2)
        b_end = jnp.where(fwd_rows[:, 0:1], cum[:, L - 1:L], cum[:, 0:1])
        w = r + b_end
        parts.append((cum, pm, b_end, w, jnp.max(w, axis=1, keepdims=True)))
    m0 = jnp.zeros((N_HD, 1), F32)
    for idx, (cum, pm, b_end, w, wmax) in enumerate(parts):
        m_new = jnp.maximum(b_end + m0, wmax)
        mu = jnp.maximum(m0, pm)
        decay = jnp.broadcast_to(jnp.exp(b_end + m0 - m_new), (N_HD, L))
        tab_s[idx] = jnp.concatenate([mu, jnp.exp(m0 - mu), jnp.exp(-(cum + mu)),
                                      jnp.exp(w - m_new), decay], axis=0)
        m0 = m_new


def _tab(tab, a, hd):
    return tab[a * N_HD + hd:a * N_HD + hd + 1, :]


def _state_update(st_ref, hd, vt_ext, k_c, ws_row, decay_hd):
    vw = (vt_ext.astype(F32) * ws_row).astype(BF16)
    st_ref[hd] = decay_hd * st_ref[hd] + _dot(vw, k_c)


def _mlstm_kernel(xm_ref, xmc_ref, cw_ref, cb_ref, wq_ref, wk_ref, wv_ref, wgc_ref, wgm_ref, bg_ref,
                  hs_ref, xc_ref,
                  pad_s, xcb_s, q_s, k_s, vt_s, gcol_s, grow_s,
                  padc_s, xcbc_s, kc_s, vtc_s, gcolc_s, growc_s, tab_s, st_s):
    L = CHUNK
    t = xm_ref.shape[1]
    nc = t // L
    row_i = lax.broadcasted_iota(jnp.int32, (L, L), 0)
    col_i = lax.broadcasted_iota(jnp.int32, (L, L), 1)
    tri = (row_i <= col_i, row_i >= col_i)
    ones_rows = (lax.broadcasted_iota(jnp.int32, (ST_ROWS - DH_B, L), 0) == 0).astype(BF16)
    hsl = lambda hh: slice(hh * DH_B, (hh + 1) * DH_B)
    xm, xmc = xm_ref.at[0], xmc_ref.at[0]
    gate_w = (cw_ref, cb_ref, wgc_ref, wgm_ref, bg_ref)

    _fill_pad(padc_s, xmc, CTX_LEN)
    _fill_pad(pad_s, xm, t)
    gts = [_conv_gate_stage(0, padc_s, xmc, *gate_w, None, xcbc_s)]
    gts += [_conv_gate_stage(c, pad_s, xm, *gate_w, xc_ref.at[0], xcb_s) for c in range(nc)]
    dests = [(growc_s, 0, gcolc_s, 0)] + [(grow_s, c, gcol_s, c * L) for c in range(nc)]
    tables = _gate_tables(gts, dests)
    _headwise_stage(0, xmc, xcbc_s, wq_ref, wk_ref, wv_ref, None, kc_s, vtc_s)
    for c in range(nc):
        next(tables, None)
        next(tables, None)
        _headwise_stage(c, xm, xcb_s, wq_ref, wk_ref, wv_ref, q_s, k_s, vt_s)
    for _ in tables:
        pass

    _chain_tables(growc_s, grow_s, tab_s, nc)
    st_s[...] = jnp.zeros(st_s.shape, F32)
    tab = tab_s[0]
    for hd in range(N_HD):
        hh = hd % H_B
        vt_ext = jnp.concatenate([vtc_s[0, hsl(hh), :], ones_rows], axis=0)
        _state_update(st_s, hd, vt_ext, kc_s[:, hsl(hh)], _tab(tab, 3, hd), _tab(tab, 4, hd)[:, 0:DH_B])

    def body(i, carry, *, accumulate):
        ci = (i, nc - 1 - i)
        rows = tuple(pl.ds(pl.multiple_of(c * L, L), L) for c in ci)
        tab = tab_s[i + 1]
        gcols = tuple(gcol_s[r, :] for r in rows)
        live = {}

        def stage_scores(hd):
            d, hh = hd // H_B, hd % H_B
            k_c = k_s[rows[d], hsl(hh)]
            vt_ext = jnp.concatenate([vt_s[ci[d], hsl(hh), :], ones_rows], axis=0)
            lhs = jnp.concatenate([k_c, st_s[hd].astype(BF16)], axis=0)
            live[hd] = (k_c, vt_ext, _dot_nt(lhs, q_s[rows[d], hsl(hh)]))

        def stage_gate(hd):
            d = hd // H_B
            k_c, vt_ext, res = live[hd]
            rcol = gcols[d][:, hd:hd + 1]
            e = jnp.exp(jnp.where(tri[d], rcol - _tab(tab, 0, hd), -jnp.inf))
            p = (res[0:L, :] * e).astype(BF16)
            _state_update(st_s, hd, vt_ext, k_c, _tab(tab, 3, hd), _tab(tab, 4, hd)[:, 0:DH_B])
            live[hd] = (vt_ext, res[L:L + ST_ROWS, :], p)

        def stage_out(hd):
            d, hh = hd // H_B, hd % H_B
            vt_ext, inter, p = live.pop(hd)
            tot = _tab(tab, 1, hd) * inter + _dot(vt_ext, p)
            den = tot[DH_B:DH_B + 1, :]
            h_t = tot[0:DH_B, :] / jnp.maximum(jnp.abs(den), _tab(tab, 2, hd))
            for pc in range(L // LANES):
                piece = h_t[:, pc * LANES:(pc + 1) * LANES]
                if accumulate:
                    hs_ref[0, ci[d] * (L // LANES) + pc, hsl(hh), :] += piece
                else:
                    hs_ref[0, ci[d] * (L // LANES) + pc, hsl(hh), :] = piece

        stages = (stage_scores, stage_gate, stage_out)
        for step in range(N_HD + len(stages) - 1):
            for si in range(len(stages)):
                if 0 <= step - si < N_HD:
                    stages[si](step - si)
        return carry

    lax.fori_loop(0, nc // 2, functools.partial(body, accumulate=False), 0)
    lax.fori_loop(nc // 2, nc, functools.partial(body, accumulate=True), 0)


def _mlstm_out(hs_t, som, szm, xc, mhg, skip):
    hsum = hs_t.T * som.astype(F32)
    parts = [_layer_norm(hsum[:, hh * DH_B:(hh + 1) * DH_B]) for hh in range(H_B)]
    hb = jnp.concatenate(parts, axis=1) * mhg + skip * xc.astype(F32)
    return (hb * szm.astype(F32)).astype(BF16)


def _mlstm_call(xm, xmc, conv_w, conv_b, wq, wk, wv, wgc, wgm, bg):
    b, t, _ = xm.shape
    nc = t // CHUNK
    assert nc % 2 == 0
    seq = lambda n: pl.BlockSpec((1, n, W_B), lambda i: (i, 0, 0))
    const = lambda shape: pl.BlockSpec(shape, lambda i: (0,) * len(shape))
    scratch = [
        pltpu.VMEM((t + 2 * _PAD, W_B), F32),
        pltpu.VMEM((t, W_B), BF16),
        pltpu.VMEM((t, W_B), BF16),
        pltpu.VMEM((t, W_B), BF16),
        pltpu.VMEM((nc, W_B, CHUNK), BF16),
        pltpu.VMEM((t, LANES), F32),
        pltpu.VMEM((nc, 3 * N_HD, CHUNK), F32),
        pltpu.VMEM((CTX_LEN + 2 * _PAD, W_B), F32),
        pltpu.VMEM((CTX_LEN, W_B), BF16),
        pltpu.VMEM((CTX_LEN, W_B), BF16),
        pltpu.VMEM((1, W_B, CHUNK), BF16),
        pltpu.VMEM((CTX_LEN, LANES), F32),
        pltpu.VMEM((1, 3 * N_HD, CHUNK), F32),
        pltpu.VMEM((nc + 1, _TAB_ROWS, CHUNK), F32),
        pltpu.VMEM((N_HD, ST_ROWS, DH_B), F32),
    ]
    return pl.pallas_call(
        _mlstm_kernel,
        grid=(b,),
        in_specs=[seq(t), seq(CTX_LEN), const((3, W_B)), const((1, W_B)),
                  const((W_B, W_B)), const((W_B, W_B)), const((W_B, W_B)),
                  const((W_B, LANES)), const((W_B, LANES)), const((1, LANES))],
        out_specs=[pl.BlockSpec((1, t // LANES, W_B, LANES), lambda i: (i, 0, 0, 0)), seq(t)],
        out_shape=[jax.ShapeDtypeStruct((b, t // LANES, W_B, LANES), F32),
                   jax.ShapeDtypeStruct((b, t, W_B), BF16)],
        scratch_shapes=scratch,
        compiler_params=pltpu.CompilerParams(dimension_semantics=("arbitrary",),
                                             vmem_limit_bytes=VMEM_LIMIT),
        name="mlstm_bidir",
    )(xm, xmc, conv_w, conv_b, wq, wk, wv, wgc, wgm, bg)


_H_SLOTS = 3


def _out_kernel(h_hbm, mod_ref, ya_ref, yb_ref, wo_ref, g_ref, b_ref, o_ref, h_buf, h_sem, *, n_sub):
    tm = o_ref.shape[1]
    sub = tm // n_sub
    gate = mod_ref[0][:, 2 * D_MODEL:3 * D_MODEL]
    nj = pl.num_programs(1)
    step, n_steps = pl.program_id(0) * nj + pl.program_id(1), pl.num_programs(0) * nj

    def h_copy(k):
        slot = k % _H_SLOTS
        src = h_hbm.at[k // nj, pl.ds(pl.multiple_of((k % nj) * tm, tm), tm), :]
        return pltpu.make_async_copy(src, h_buf.at[slot], h_sem.at[slot])

    @pl.when(step == 0)
    def _():
        h_copy(step).start()
        h_copy(step + 1).start()

    @pl.when(step + 2 < n_steps)
    def _():
        h_copy(step + 2).start()

    h_copy(step).wait()
    h_ref = h_buf.at[step % _H_SLOTS]

    def mix_stage(s):
        rows = slice(s * sub, (s + 1) * sub)
        return (_dot(ya_ref[0, rows, :], wo_ref[0:W_A, :])
                + _dot(yb_ref[0, rows, :], wo_ref[W_A:W_A + W_B, :]))

    def norm_stage(s, y):
        rows = slice(s * sub, (s + 1) * sub)
        o_ref[0, rows, :] = _layer_norm(ALPHA * h_ref[rows, :] + gate * y) * g_ref[...] + b_ref[...]

    y = mix_stage(0)
    for s in range(n_sub):
        y_next = mix_stage(s + 1) if s + 1 < n_sub else None
        norm_stage(s, y)
        y = y_next


def _out_call(h, mod3, ya, yb, wo, g, bb, *, tm, n_sub):
    b, t, _ = h.shape
    const = lambda shape: pl.BlockSpec(shape, lambda i, j: (0,) * len(shape))
    row = lambda w: pl.BlockSpec((1, tm, w), lambda i, j: (i, j, 0))
    assert b * (t // tm) >= 2
    return pl.pallas_call(
        functools.partial(_out_kernel, n_sub=n_sub),
        grid=(b, t // tm),
        in_specs=[pl.BlockSpec(memory_space=pl.ANY), pl.BlockSpec((1, 1, 3 * D_MODEL), lambda i, j: (i, 0, 0)),
                  row(W_A), row(W_B),
                  const((W_A + W_B, D_MODEL)), const((1, D_MODEL)), const((1, D_MODEL))],
        out_specs=row(D_MODEL),
        out_shape=jax.ShapeDtypeStruct((b, t, D_MODEL), F32),
        scratch_shapes=[pltpu.VMEM((_H_SLOTS, tm, D_MODEL), F32), pltpu.SemaphoreType.DMA((_H_SLOTS,))],
        compiler_params=pltpu.CompilerParams(dimension_semantics=("arbitrary", "arbitrary"),
                                             vmem_limit_bytes=VMEM_LIMIT),
        name="out_proj_ln",
    )(h, mod3, ya, yb, wo, g, bb)


def _rope_tables(seq, scale_keep, scale_rope):
    n_rows = seq // GRID_W
    rowp = np.repeat(np.arange(n_rows, dtype=np.float32), GRID_W)
    colp = np.tile(np.arange(GRID_W, dtype=np.float32), n_rows)
    inv = (np.float32(ROPE_BASE) ** (-np.arange(ROPE_FREQS, dtype=np.float32) / np.float32(ROPE_FREQS)))
    ang = np.stack([rowp[:, None] * inv, colp[:, None] * inv], axis=1).astype(np.float32)
    cos = np.broadcast_to(np.cos(ang)[:, :, None, :], (seq, 2, 2, ROPE_FREQS)).reshape(seq, DR_A)
    sin = np.broadcast_to(np.sin(ang)[:, :, None, :], (seq, 2, 2, ROPE_FREQS)).reshape(seq, DR_A)
    z32 = np.zeros((seq, DR_A), np.float32)
    t1 = np.concatenate([np.full((seq, DN_A), scale_keep, np.float32), z32, cos * scale_rope], axis=1)
    t2 = np.concatenate([np.zeros((seq, DN_A), np.float32), z32, sin * scale_rope], axis=1)
    return jnp.asarray(t1, F32), jnp.asarray(t2, F32)


def kernel(x, c, ctx, c_ctx, ln_in_g, ln_in_b, w_ada, b_ada, w_in, g_qa, w_qb, g_kva, w_kvb, conv_w, conv_b, w_mq, w_mk, w_mv, w_gate, b_gate, mh_g, skip, w_out, ln_g, ln_b):
    b, t, _ = x.shape
    l = 0
    r2 = lambda v: v.reshape(1, -1)

    blocks = lambda w: jnp.transpose(w[l], (1, 2, 0)).reshape(QKV_BS * QKV_BS, -1)
    wo = w_out[l].astype(BF16)

    sm_scale = (DN_A + DR_A) ** -0.5 * LOG2_E
    t1q, t2q = _rope_tables(t, sm_scale, sm_scale)
    t1k, t2k = _rope_tables(t, 0.0, 1.0)
    n_ctx = ctx.shape[1]

    mod3, wina, winr, wmq, wmk_s, wmv, wgc, wgm, bg, wq, wkn, wvt = _prep_call(
        c, r2(c_ctx), w_ada[l], r2(b_ada[l]), jnp.swapaxes(w_in[l], 0, 1), blocks(w_mq), blocks(w_mk),
        blocks(w_mv), jnp.swapaxes(w_gate[l], 0, 1), r2(b_gate[l]), w_qb[l], w_kvb[l])

    lng, lnb = r2(ln_in_g), r2(ln_in_b)
    shared = (wina, winr, r2(g_qa[l]), wq, r2(g_kva[l]), wkn, wvt)
    q, kl, vtl, xm, sza, som, szm, h = _proj_call(x, mod3, None, lng, lnb, *shared, t1q, t2q, t1k, t2k,
                                                  tm=1024, n_sub=4, latent=True)
    kc, vtc, xmc = _proj_call(ctx.reshape(1, b * n_ctx, D_MODEL), mod3, b, lng, lnb, *shared,
                              tm=min(1024, b * n_ctx), n_sub=2, latent=False)
    xmc = xmc.reshape(b, n_ctx, W_B)

    hs, xc = _mlstm_call(xm, xmc, conv_w[l], r2(conv_b[l]), wmq, wmk_s, wmv, wgc, wgm, bg)
    ya, yb = _attn_call(q, kc, kl, vtc, vtl, sza, hs, som, szm, xc, r2(mh_g[l]), r2(skip[l]), tq=512)
    return _out_call(h, mod3, ya, yb, wo, r2(ln_g[l]), r2(ln_b[l]), tm=1024, n_sub=4)
```
